```python
import math
import jax, jax.numpy as jnp
from jax import lax
import numpy as np

D_MODEL = 2048
BATCH = 2
SEQ = 8192
DEPTH = 1
DEC_BATCH = 16
DEC_SEQ = 16
PAST_LEN = 4096

CHUNK = 64
A_HEADS = 8
A_KV_HEADS = 2
A_REP = A_HEADS // A_KV_HEADS
A_HEAD_DIM = 128
IDX_HEADS = 16
IDX_DIM = 64
TOPK_MAX = 256
B_HEADS = 4
B_HEAD_DIM = 128
N_REL_BUCKETS = 32
REL_MAX_DIST = 128
N_ATTN_HEADS = A_HEADS + B_HEADS
IN_SIZES = (A_HEADS * A_HEAD_DIM, A_KV_HEADS * A_HEAD_DIM, A_KV_HEADS * A_HEAD_DIM,
            IDX_HEADS * IDX_DIM, IDX_DIM, IDX_HEADS,
            B_HEADS * 2 * B_HEAD_DIM, B_HEADS * 2 * B_HEAD_DIM, B_HEADS * 2 * B_HEAD_DIM)
D_IN = sum(IN_SIZES)
MIX_WIDTH = A_HEADS * A_HEAD_DIM + B_HEADS * 2 * B_HEAD_DIM
N_EXPERTS = 64
TOP_K = 8
N_GROUPS = 8
TOPK_GROUPS = 4
EXPERT_DIM = 512
SHARED_DIM = 512
ROUTED_SCALE = 2.5
EPS = 1e-6
QBLOCK = 128

kernel_name = "hybrid_dsa_diffattn_moe_stream_step"


def _rmsnorm(x, g):
    xf = x.astype(jnp.float32)
    y = xf * lax.rsqrt(jnp.mean(xf * xf, axis=-1, keepdims=True) + EPS)
    return (y * g.astype(jnp.float32)).astype(x.dtype)


def _rel_bucket(rel):
    nb = N_REL_BUCKETS // 2
    max_exact = nb // 2
    n = jnp.abs(rel)
    nf = jnp.maximum(n, 1).astype(jnp.float32)
    large = max_exact + (jnp.log(nf / max_exact) / math.log(REL_MAX_DIST / max_exact)
                         * (nb - max_exact)).astype(jnp.int32)
    large = jnp.minimum(large, nb - 1)
    return jnp.where(rel > 0, nb, 0) + jnp.where(n < max_exact, n, large)


def _split_cols(proj):
    offs = [int(o) for o in np.cumsum(IN_SIZES)[:-1]]
    return jnp.split(proj, offs, axis=-1)


def _sweep(fn, q_arrays, qpos):
    B, T = q_arrays[0].shape[:2]
    qb = min(QBLOCK, T)
    nb = T // qb
    blocks = tuple(jnp.swapaxes(a.reshape((B, nb, qb) + a.shape[2:]), 0, 1) for a in q_arrays)
    out = lax.map(lambda xs: fn(*xs), blocks + (qpos.reshape(nb, qb),))
    return jnp.swapaxes(out, 0, 1).reshape(B, T, out.shape[-1])


def _dsa_block(qa, qi, iw, qpos, ka, va, ki, kpos, bias_tab, topk):
    B, QB = qa.shape[:2]
    visible = (kpos[None, :] // CHUNK) <= (qpos[:, None] // CHUNK)
    dots = jnp.einsum('bqhd,bsd->bqhs', qi, ki).astype(jnp.float32) * IDX_DIM ** -0.5
    score = jnp.einsum('bqh,bqhs->bqs', iw.astype(jnp.float32) * IDX_HEADS ** -0.5, jax.nn.relu(dots))
    score = jnp.where(visible[None], score, -jnp.inf)
    _, sel = lax.top_k(score, topk)
    bidx = jnp.arange(B)[:, None, None]
    ks = ka[bidx, sel]
    vs = va[bidx, sel]
    sel_pos = kpos[sel]
    valid = (sel_pos // CHUNK) <= (qpos[None, :, None] // CHUNK)
    bias = bias_tab[_rel_bucket(sel_pos - qpos[None, :, None])][..., :A_HEADS]
    bias = bias.reshape(B, QB, topk, A_KV_HEADS, A_REP).transpose(0, 1, 3, 4, 2)
    q = qa.reshape(B, QB, A_KV_HEADS, A_REP, A_HEAD_DIM)
    logits = jnp.einsum('bqgrd,bqkgd->bqgrk', q, ks).astype(jnp.float32) * A_HEAD_DIM ** -0.5 + bias
    logits = jnp.where(valid[:, :, None, None, :], logits, -jnp.inf)
    p = jax.nn.softmax(logits, axis=-1).astype(vs.dtype)
    out = jnp.einsum('bqgrk,bqkgd->bqgrd', p, vs)
    return out.reshape(B, QB, A_HEADS * A_HEAD_DIM)


def _diff_block(qb, qpos, kb, vb, kpos, bias_tab, lam, lam_init, subln_g):
    B, QB = qb.shape[:2]
    visible = (kpos[None, :] // CHUNK) <= (qpos[:, None] // CHUNK)
    logits = jnp.einsum('bqhcd,bshcd->bhcqs', qb, kb).astype(jnp.float32) * B_HEAD_DIM ** -0.5
    bias = bias_tab[_rel_bucket(kpos[None, :] - qpos[:, None])][..., A_HEADS:]
    logits = logits + jnp.transpose(bias, (2, 0, 1))[None, :, None]
    logits = jnp.where(visible, logits, -jnp.inf)
    p = jax.nn.softmax(logits, axis=-1)
    a = (p[:, :, 0] - lam * p[:, :, 1]).astype(vb.dtype)
    o = jnp.einsum('bhqs,bshe->bqhe', a, vb)
    o = _rmsnorm(o, subln_g) * (1.0 - lam_init)
    return o.reshape(B, QB, B_HEADS * 2 * B_HEAD_DIM)


def _swiglu(t, wg, wu, wd):
    return (jax.nn.silu(t @ wg) * (t @ wu)) @ wd


def _moe(h, w_router, router_bias, w_gate, w_up, w_down, ws_gate, ws_up, ws_down):
    B, T, D = h.shape
    t = h.reshape(B * T, D)
    scores = jax.nn.sigmoid((t @ w_router).astype(jnp.float32))
    choice = scores + router_bias.astype(jnp.float32)
    grp = choice.reshape(-1, N_GROUPS, N_EXPERTS // N_GROUPS)
    grp_score = lax.top_k(grp, 2)[0].sum(-1)
    _, gidx = lax.top_k(grp_score, TOPK_GROUPS)
    gmask = jax.nn.one_hot(gidx, N_GROUPS, dtype=jnp.float32).sum(-2)
    emask = jnp.repeat(gmask, N_EXPERTS // N_GROUPS, axis=-1)
    _, eidx = lax.top_k(jnp.where(emask > 0, choice, -jnp.inf), TOP_K)
    sel = jnp.take_along_axis(scores, eidx, axis=-1)
    wts = sel / jnp.sum(sel, axis=-1, keepdims=True) * ROUTED_SCALE
    gates = jnp.sum(jax.nn.one_hot(eidx, N_EXPERTS, dtype=jnp.float32) * wts[..., None], axis=-2)

    def body(acc, xs):
        wg, wu, wd, g = xs
        return acc + g[:, None].astype(t.dtype) * _swiglu(t, wg, wu, wd), None

    acc, _ = lax.scan(body, _swiglu(t, ws_gate, ws_up, ws_down), (w_gate, w_up, w_down, gates.T))
    return acc.reshape(B, T, D)


def _layer(x, c, past, layer_idx, rel_bias, w_ada, b_ada, norm_a_g, w_in, w_out, diff_lam, subln_g,
           norm_f_g, w_router, router_bias, w_gate, w_up, w_down, ws_gate, ws_up, ws_down):
    B, T, _ = x.shape
    mod = (jax.nn.silu(c) @ w_ada + b_ada)[:, None, :]
    sh_a, sc_a, g_a, sh_f, sc_f, g_f = jnp.split(mod, 6, axis=-1)
    h = _rmsnorm(x, norm_a_g) * (1.0 + sc_a) + sh_a
    qa, ka, va, qi, ki, iw, qb, kb, vb = _split_cols(h @ w_in)
    qa = qa.reshape(B, T, A_HEADS, A_HEAD_DIM)
    ka = ka.reshape(B, T, A_KV_HEADS, A_HEAD_DIM)
    va = va.reshape(B, T, A_KV_HEADS, A_HEAD_DIM)
    qi = qi.reshape(B, T, IDX_HEADS, IDX_DIM)
    qb = qb.reshape(B, T, B_HEADS, 2, B_HEAD_DIM)
    kb = kb.reshape(B, T, B_HEADS, 2, B_HEAD_DIM)
    vb = vb.reshape(B, T, B_HEADS, 2 * B_HEAD_DIM)
    new_rows = (ka, va, ki, kb, vb)
    if past is None:
        offset = 0
        ka_all, va_all, ki_all, kb_all, vb_all = new_rows
    else:
        offset = past[0].shape[1]
        ka_all, va_all, ki_all, kb_all, vb_all = (jnp.concatenate([pc, nr], axis=1) for pc, nr in zip(past, new_rows))
    L = offset + T
    qpos = offset + jnp.arange(T, dtype=jnp.int32)
    kpos = jnp.arange(L, dtype=jnp.int32)
    topk = min(TOPK_MAX, L // 4)

    out_a = _sweep(lambda q_, qi_, iw_, p_: _dsa_block(q_, qi_, iw_, p_, ka_all, va_all, ki_all, kpos, rel_bias, topk),
                   (qa, qi, iw), qpos)
    lam_init = 0.8 - 0.6 * math.exp(-0.3 * layer_idx)
    dl = diff_lam.astype(jnp.float32)
    lam = jnp.exp(jnp.sum(dl[0] * dl[1])) - jnp.exp(jnp.sum(dl[2] * dl[3])) + lam_init
    out_b = _sweep(lambda q_, p_: _diff_block(q_, p_, kb_all, vb_all, kpos, rel_bias, lam, lam_init, subln_g),
                   (qb,), qpos)
    x = x + g_a * (jnp.concatenate([out_a, out_b], axis=-1) @ w_out)
    h = _rmsnorm(x, norm_f_g) * (1.0 + sc_f) + sh_f
    x = x + g_f * _moe(h, w_router, router_bias, w_gate, w_up, w_down, ws_gate, ws_up, ws_down)
    return x, new_rows


def _stack(rows, i):
    return jnp.stack([r[i] for r in rows], axis=0)


def setup_inputs(seed: int = 0) -> dict:
    key = jax.random.key(seed)
    keys = list(jax.random.split(key, 32))

    def nrm(i, shape, s):
        return jax.random.normal(keys[i], shape, jnp.float32) * s

    D = D_MODEL
    return {
        "x_prompt": nrm(0, (BATCH, SEQ, D), 1.0),
        "x_sample": nrm(1, (DEC_BATCH, DEC_SEQ, D), 1.0),
        "c_prompt": nrm(2, (BATCH, D), 1.0),
        "c_sample": nrm(3, (DEC_BATCH, D), 1.0),
        "cache_a_k": nrm(4, (DEPTH, DEC_BATCH, PAST_LEN, A_KV_HEADS, A_HEAD_DIM), 1.0),
        "cache_a_v": nrm(5, (DEPTH, DEC_BATCH, PAST_LEN, A_KV_HEADS, A_HEAD_DIM), 1.0),
        "cache_a_kidx": nrm(6, (DEPTH, DEC_BATCH, PAST_LEN, IDX_DIM), 1.0),
        "cache_b_k": nrm(7, (DEPTH, DEC_BATCH, PAST_LEN, B_HEADS, 2, B_HEAD_DIM), 1.0),
        "cache_b_v": nrm(8, (DEPTH, DEC_BATCH, PAST_LEN, B_HEADS, 2 * B_HEAD_DIM), 1.0),
        "rel_bias": nrm(9, (N_REL_BUCKETS, N_ATTN_HEADS), 0.5),
        "w_ada": nrm(10, (DEPTH, D, 6 * D), 0.5 * D ** -0.5),
        "b_ada": nrm(11, (DEPTH, 6 * D), 0.02),
        "norm_a_g": 1.0 + nrm(12, (DEPTH, D), 0.02),
        "w_in": nrm(13, (DEPTH, D, D_IN), D ** -0.5),
        "w_out": nrm(14, (DEPTH, MIX_WIDTH, D), MIX_WIDTH ** -0.5),
        "diff_lam": nrm(15, (DEPTH, 4, B_HEAD_DIM), 0.1),
        "subln_g": 1.0 + nrm(16, (DEPTH, 2 * B_HEAD_DIM), 0.02),
        "norm_f_g": 1.0 + nrm(17, (DEPTH, D), 0.02),
        "w_router": nrm(18, (DEPTH, D, N_EXPERTS), D ** -0.5),
        "router_bias": nrm(19, (DEPTH, N_EXPERTS), 0.01),
        "w_gate": nrm(20, (DEPTH, N_EXPERTS, D, EXPERT_DIM), D ** -0.5),
        "w_up": nrm(21, (DEPTH, N_EXPERTS, D, EXPERT_DIM), D ** -0.5),
        "w_down": nrm(22, (DEPTH, N_EXPERTS, EXPERT_DIM, D), EXPERT_DIM ** -0.5),
        "ws_gate": nrm(23, (DEPTH, D, SHARED_DIM), D ** -0.5),
        "ws_up": nrm(24, (DEPTH, D, SHARED_DIM), D ** -0.5),
        "ws_down": nrm(25, (DEPTH, SHARED_DIM, D), SHARED_DIM ** -0.5),
        "final_g": 1.0 + nrm(26, (D,), 0.02),
    }


def reference(x_prompt, x_sample, c_prompt, c_sample, cache_a_k, cache_a_v, cache_a_kidx, cache_b_k, cache_b_v,
              rel_bias, w_ada, b_ada, norm_a_g, w_in, w_out, diff_lam, subln_g, norm_f_g, w_router, router_bias,
              w_gate, w_up, w_down, ws_gate, ws_up, ws_down, final_g):
    xp, xs = x_prompt, x_sample
    rows_p, rows_s = [], []
    for l in range(DEPTH):
        lw = (w_ada[l], b_ada[l], norm_a_g[l], w_in[l], w_out[l], diff_lam[l], subln_g[l], norm_f_g[l],
              w_router[l], router_bias[l], w_gate[l], w_up[l], w_down[l], ws_gate[l], ws_up[l], ws_down[l])
        xp, rp = _layer(xp, c_prompt, None, l, rel_bias, *lw)
        past = (cache_a_k[l], cache_a_v[l], cache_a_kidx[l], cache_b_k[l], cache_b_v[l])
        xs, rs = _layer(xs, c_sample, past, l, rel_bias, *lw)
        rows_p.append(rp)
        rows_s.append(rs)
    y_prompt = _rmsnorm(xp, final_g)
    y_sample = _rmsnorm(xs, final_g)
    return (y_prompt, y_sample,
            _stack(rows_p, 0), _stack(rows_p, 1), _stack(rows_p, 2), _stack(rows_p, 3), _stack(rows_p, 4),
            _stack(rows_s, 0), _stack(rows_s, 1), _stack(rows_s, 2), _stack(rows_s, 3), _stack(rows_s, 4))
```

```python
import functools
import math

import numpy as np
import jax
import jax.numpy as jnp
from jax import lax
from jax.experimental import pallas as pl
from jax.experimental.pallas import tpu as pltpu

D_MODEL = 2048
CHUNK = 64
A_HEADS = 8
A_KV_HEADS = 2
A_REP = A_HEADS // A_KV_HEADS
A_HEAD_DIM = 128
IDX_HEADS = 16
IDX_DIM = 64
TOPK_MAX = 256
B_HEADS = 4
B_HEAD_DIM = 128
N_REL_BUCKETS = 32
REL_MAX_DIST = 128
N_ATTN_HEADS = A_HEADS + B_HEADS
N_EXPERTS = 64
TOP_K = 8
N_GROUPS = 8
TOPK_GROUPS = 4
EXPERT_DIM = 512
SHARED_DIM = 512
ROUTED_SCALE = 2.5
EPS = 1e-6

LANES = 128
INT_MIN = -(2 ** 31)
NEG_BIG = -1e30
VMEM_LIMIT = 52 * 1024 * 1024

F32 = jnp.float32
BF16 = jnp.bfloat16
NT_DIMS = (((1,), (1,)), ((), ()))


def _params(n_axes):
    return pltpu.CompilerParams(dimension_semantics=("arbitrary",) * n_axes,
                                vmem_limit_bytes=VMEM_LIMIT)


def _resident(shape, index_map):
    return pl.BlockSpec(shape, index_map, pipeline_mode=pl.Buffered(1))


def _bucket_thresholds():
    nb = N_REL_BUCKETS // 2
    max_exact = nb // 2
    n = np.arange(max_exact, 4 * REL_MAX_DIST, dtype=np.int32)
    nf = n.astype(np.float32)
    large = max_exact + (np.log(nf / np.float32(max_exact)) / np.float32(math.log(REL_MAX_DIST / max_exact))
                         * np.float32(nb - max_exact)).astype(np.int32)
    large = np.minimum(large, nb - 1)
    assert np.all(np.diff(large) >= 0)
    ths = [int(n[np.argmax(large >= b)]) for b in range(max_exact + 1, nb)]
    return max_exact, nb, ths


_MAX_EXACT, _NB, _BUCKET_THS = _bucket_thresholds()
assert _BUCKET_THS[-1] <= LANES + 1
FAR_BUCKET = _NB - 1


def _ada_kernel(c_ref, w_ref, b_ref, o_ref):
    c = c_ref[...]
    s = c * jax.nn.sigmoid(c)
    o_ref[...] = jnp.dot(s, w_ref[...], preferred_element_type=F32,
                         precision=lax.Precision.HIGHEST) + b_ref[...]


def _ada_call(c, w_ada, b_ada):
    rows, d = c.shape
    n = w_ada.shape[1]
    tn = 1536
    assert n % tn == 0
    return pl.pallas_call(
        _ada_kernel,
        grid=(n // tn,),
        in_specs=[pl.BlockSpec((rows, d), lambda j: (0, 0)),
                  pl.BlockSpec((d, tn), lambda j: (0, j)),
                  pl.BlockSpec((1, tn), lambda j: (0, j))],
        out_specs=pl.BlockSpec((rows, tn), lambda j: (0, j)),
        out_shape=jax.ShapeDtypeStruct((rows, n), F32),
        compiler_params=_params(1),
        name="ada_mod",
    )(c, w_ada, b_ada.reshape(1, n))


def _modulated_norm(x, g, sc, sh):
    y = x * lax.rsqrt(jnp.mean(x * x, axis=-1, keepdims=True) + EPS) * g
    return y * (1.0 + sc) + sh


def _norm_proj_kernel(x_ref, sc_ref, sh_ref, g_ref, w_ref, *out_refs, defs):
    h = _modulated_norm(x_ref[...], g_ref[...], sc_ref[...], sh_ref[...]).astype(BF16)
    done = {}
    for (start, width, store_w), o_ref in zip(defs, out_refs):
        if (start, width) not in done:
            done[(start, width)] = jnp.dot(h, w_ref[:, start:start + width], preferred_element_type=F32)
        o_ref[...] = done[(start, width)][:, :store_w].astype(o_ref.dtype)


def _norm_proj_call(x2d, sc, sh, mod_map, g, w_bf, defs, dtypes, tm, name):
    n, d = x2d.shape
    assert n % tm == 0
    r = sc.shape[1]
    out_shapes = [jax.ShapeDtypeStruct((n, sw), dt) for (_, _, sw), dt in zip(defs, dtypes)]
    out_specs = [pl.BlockSpec((tm, sw), lambda i: (i, 0)) for (_, _, sw) in defs]
    return pl.pallas_call(
        functools.partial(_norm_proj_kernel, defs=tuple(defs)),
        grid=(n // tm,),
        in_specs=[pl.BlockSpec((tm, d), lambda i: (i, 0)),
                  pl.BlockSpec((None, r, d), mod_map),
                  pl.BlockSpec((None, r, d), mod_map),
                  pl.BlockSpec((1, d), lambda i: (0, 0)),
                  _resident(w_bf.shape, lambda i: (0, 0))],
        out_specs=out_specs,
        out_shape=out_shapes,
        compiler_params=_params(1),
        name=name,
    )(x2d, sc, sh, g.reshape(1, d), w_bf)


def _bias_tile_kernel(tab_ref, o_ref, *, qb):
    _, rows, width = o_ref.shape
    r = lax.broadcasted_iota(jnp.int32, (rows, width), 0)
    koff = lax.broadcasted_iota(jnp.int32, (rows, width), 1) - LANES
    rel = koff - r
    n = jnp.abs(rel)
    large = jnp.full((rows, width), _MAX_EXACT, jnp.int32)
    for th in _BUCKET_THS:
        large = large + jnp.where(n >= th, 1, 0)
    bucket = jnp.where(rel > 0, _NB, 0) + jnp.where(n < _MAX_EXACT, n, large)
    visible = ((koff >> 6) <= (r >> 6)) & (koff < qb)
    for h in range(N_ATTN_HEADS):
        b = jnp.zeros((rows, width), F32)
        for bk in range(N_REL_BUCKETS):
            b = jnp.where(bucket == bk, tab_ref[bk, h], b)
        o_ref[h] = jnp.where(visible, b, NEG_BIG)


def _bias_tile_call(rel_bias, qb):
    qbr = -(-qb // LANES) * LANES
    return pl.pallas_call(
        functools.partial(_bias_tile_kernel, qb=qb),
        in_specs=[pl.BlockSpec(memory_space=pltpu.SMEM)],
        out_specs=pl.BlockSpec(memory_space=pltpu.VMEM),
        out_shape=jax.ShapeDtypeStruct((N_ATTN_HEADS, qb, LANES + qbr), F32),
        name="bias_tiles",
    )(rel_bias)


def _zones(q0, kb):
    far_end = jnp.maximum(q0 - LANES, 0)
    n_far = far_end // kb
    n_mid = (far_end - n_far * kb) // LANES
    return n_far, n_mid


def _dsa_kernel(tab_ref, q_ref, qi_ref, iw_ref, k_ref, v_ref, ki_ref, tile_ref, o_ref,
                keys_scr, m_scr, l_scr, acc_scr, *, qb, kb, qoff, topk):
    i = pl.program_id(1)
    q0 = qoff + i * qb
    qbr = tile_ref.shape[2] - LANES
    sub = kb // LANES
    w = iw_ref[:, :IDX_HEADS] * (IDX_HEADS ** -0.5 * IDX_DIM ** -0.5)

    def score_keys(start, width):
        kib = ki_ref[pl.ds(start, width), :]
        acc = jnp.zeros((qb, width), F32)
        for h in range(IDX_HEADS):
            d = lax.dot_general(qi_ref[:, h * LANES:(h + 1) * LANES], kib, NT_DIMS,
                                preferred_element_type=F32)
            acc = acc + w[:, h:h + 1] * jnp.maximum(d, 0.0)
        bits = pltpu.bitcast(acc, jnp.int32)
        return bits ^ ((bits >> 31) & 0x7FFFFFFF)

    n_full = q0 // kb
    n_rem = (q0 - n_full * kb) // LANES

    def full_body(j, carry):
        key = score_keys(pl.multiple_of(j * kb, kb), kb)
        for c in range(sub):
            keys_scr[j * sub + c] = key[:, c * LANES:(c + 1) * LANES]
        return carry

    lax.fori_loop(0, n_full, full_body, 0)

    def rem_body(j, carry):
        blk = n_full * sub + j
        keys_scr[blk] = score_keys(pl.multiple_of(blk * LANES, LANES), LANES)
        return carry

    lax.fori_loop(0, n_rem, rem_body, 0)

    cur_blk = q0 // LANES
    key = score_keys(pl.multiple_of(q0, LANES), qbr)
    key = jnp.where(tile_ref[0, :, LANES:] > 0.5 * NEG_BIG, key, INT_MIN)
    for c in range(qbr // LANES):
        keys_scr[cur_blk + c] = key[:, c * LANES:(c + 1) * LANES]
    n_blk = cur_blk + qbr // LANES

    def count_ge(t):
        tb = jnp.broadcast_to(t, (qb, LANES))

        def body(j, cnt):
            return cnt + jnp.where(keys_scr[j] >= tb, 1, 0)

        cnt = lax.fori_loop(0, n_blk, body, jnp.zeros((qb, LANES), jnp.int32))
        return jnp.sum(cnt, axis=-1, keepdims=True)

    def bit_body(it, t):
        cand = t + (jnp.int32(1) << (31 - it))
        return jnp.where(count_ge(cand) >= topk, cand, t)

    thr = lax.fori_loop(0, 32, bit_body, jnp.full((qb, 1), INT_MIN, jnp.int32))
    thr = jnp.maximum(thr, INT_MIN + 1)

    m_scr[...] = jnp.full(m_scr.shape, NEG_BIG, F32)
    l_scr[...] = jnp.zeros(l_scr.shape, F32)
    acc_scr[...] = jnp.zeros(acc_scr.shape, F32)
    scale = A_HEAD_DIM ** -0.5

    def attend(start, width, blk0, bias_fn):
        kblk = k_ref[pl.ds(start, width), :]
        vblk = v_ref[pl.ds(start, width), :]
        keyblk = jnp.concatenate([keys_scr[blk0 + c] for c in range(width // LANES)], axis=1)
        sel = keyblk >= thr
        for h in range(A_HEADS):
            g = h // A_REP
            s = lax.dot_general(q_ref[:, h * A_HEAD_DIM:(h + 1) * A_HEAD_DIM],
                                kblk[:, g * A_HEAD_DIM:(g + 1) * A_HEAD_DIM], NT_DIMS,
                                preferred_element_type=F32) * scale + bias_fn(h)
            s = jnp.where(sel, s, NEG_BIG)
            m_old = m_scr[h]
            m_new = jnp.maximum(m_old, jnp.max(s, axis=-1, keepdims=True))
            p = jnp.where(sel, jnp.exp(s - m_new), 0.0)
            alpha = jnp.exp(m_old - m_new)
            l_scr[h] = alpha * l_scr[h] + jnp.sum(p, axis=-1, keepdims=True)
            acc_scr[h] = alpha * acc_scr[h] + jnp.dot(
                p.astype(BF16), vblk[:, g * A_HEAD_DIM:(g + 1) * A_HEAD_DIM], preferred_element_type=F32)
            m_scr[h] = m_new

    n_far, n_mid = _zones(q0, kb)
    far_bias = lambda h: tab_ref[FAR_BUCKET, h]

    def far_body(j, carry):
        attend(pl.multiple_of(j * kb, kb), kb, j * sub, far_bias)
        return carry

    lax.fori_loop(0, n_far, far_body, 0)

    def mid_body(j, carry):
        blk = n_far * sub + j
        attend(pl.multiple_of(blk * LANES, LANES), LANES, blk, far_bias)
        return carry

    lax.fori_loop(0, n_mid, mid_body, 0)

    @pl.when(q0 > 0)
    def _():
        attend(pl.multiple_of(q0 - LANES, LANES), LANES, cur_blk - 1, lambda h: tile_ref[h, :, :LANES])

    attend(pl.multiple_of(q0, LANES), qbr, cur_blk, lambda h: tile_ref[h, :, LANES:])

    for h in range(A_HEADS):
        o_ref[:, h * A_HEAD_DIM:(h + 1) * A_HEAD_DIM] = (acc_scr[h] / l_scr[h]).astype(o_ref.dtype)


def _dsa_call(rel_bias, tiles, q, qi, iw, k, v, ki, *, qb, kb, qoff, topk):
    b, t, _ = q.shape
    lp = k.shape[1]
    qbr = tiles.shape[2] - LANES
    assert t % qb == 0 and qoff % LANES == 0 and qoff + t - qb + qbr <= lp
    n_slabs = (qoff + t - qb + qbr) // LANES
    return pl.pallas_call(
        functools.partial(_dsa_kernel, qb=qb, kb=kb, qoff=qoff, topk=topk),
        grid=(b, t // qb),
        in_specs=[pl.BlockSpec(memory_space=pltpu.SMEM),
                  pl.BlockSpec((None, qb, q.shape[2]), lambda bi, i: (bi, i, 0)),
                  pl.BlockSpec((None, qb, qi.shape[2]), lambda bi, i: (bi, i, 0)),
                  pl.BlockSpec((None, qb, iw.shape[2]), lambda bi, i: (bi, i, 0)),
                  pl.BlockSpec((None, lp, k.shape[2]), lambda bi, i: (bi, 0, 0)),
                  pl.BlockSpec((None, lp, v.shape[2]), lambda bi, i: (bi, 0, 0)),
                  pl.BlockSpec((None, lp, ki.shape[2]), lambda bi, i: (bi, 0, 0)),
                  _resident(tiles.shape, lambda bi, i: (0, 0, 0))],
        out_specs=pl.BlockSpec((None, qb, A_HEADS * A_HEAD_DIM), lambda bi, i: (bi, i, 0)),
        out_shape=jax.ShapeDtypeStruct((b, t, A_HEADS * A_HEAD_DIM), BF16),
        scratch_shapes=[pltpu.VMEM((n_slabs, qb, LANES), jnp.int32),
                        pltpu.VMEM((A_HEADS, qb, 1), F32),
                        pltpu.VMEM((A_HEADS, qb, 1), F32),
                        pltpu.VMEM((A_HEADS, qb, A_HEAD_DIM), F32)],
        compiler_params=_params(2),
        name="mixer_a",
    )(rel_bias, q, qi, iw, k, v, ki, tiles)


def _diff_kernel(tab_ref, lam_ref, sg_ref, q_ref, k_ref, v_ref, tile_ref, o_ref,
                 m_scr, l_scr, acc_scr, *, qb, kb, qoff, lam_init):
    hd = pl.program_id(1)
    i = pl.program_id(2)
    q0 = qoff + i * qb
    qbr = tile_ref.shape[2] - LANES
    sub = kb // LANES
    scale = B_HEAD_DIM ** -0.5
    m_scr[...] = jnp.full(m_scr.shape, NEG_BIG, F32)
    l_scr[...] = jnp.zeros(l_scr.shape, F32)
    acc_scr[...] = jnp.zeros(acc_scr.shape, F32)

    def attend(start, width, bias):
        kblk = k_ref[pl.ds(start, width), :]
        vblk = v_ref[pl.ds(start, width), :]
        for c in range(2):
            s = lax.dot_general(q_ref[:, c * B_HEAD_DIM:(c + 1) * B_HEAD_DIM],
                                kblk[:, c * B_HEAD_DIM:(c + 1) * B_HEAD_DIM], NT_DIMS,
                                preferred_element_type=F32) * scale + bias
            m_old = m_scr[c]
            m_new = jnp.maximum(m_old, jnp.max(s, axis=-1, keepdims=True))
            p = jnp.exp(s - m_new)
            alpha = jnp.exp(m_old - m_new)
            l_scr[c] = alpha * l_scr[c] + jnp.sum(p, axis=-1, keepdims=True)
            acc_scr[c] = alpha * acc_scr[c] + jnp.dot(p.astype(BF16), vblk, preferred_element_type=F32)
            m_scr[c] = m_new

    n_far, n_mid = _zones(q0, kb)
    far_bias = tab_ref[FAR_BUCKET, A_HEADS + hd]

    def far_body(j, carry):
        attend(pl.multiple_of(j * kb, kb), kb, far_bias)
        return carry

    lax.fori_loop(0, n_far, far_body, 0)

    def mid_body(j, carry):
        attend(pl.multiple_of((n_far * sub + j) * LANES, LANES), LANES, far_bias)
        return carry

    lax.fori_loop(0, n_mid, mid_body, 0)

    @pl.when(q0 > 0)
    def _():
        attend(pl.multiple_of(q0 - LANES, LANES), LANES, tile_ref[0, :, :LANES])

    attend(pl.multiple_of(q0, LANES), qbr, tile_ref[0, :, LANES:])

    lam = lam_ref[0, 0]
    o = acc_scr[0] / l_scr[0] - lam * (acc_scr[1] / l_scr[1])
    o = o * lax.rsqrt(jnp.mean(o * o, axis=-1, keepdims=True) + EPS) * sg_ref[...]
    o_ref[...] = (o * (1.0 - lam_init)).astype(o_ref.dtype)


def _diff_call(rel_bias, tiles, lam, subln_g, q, k, v, *, qb, kb, qoff, lam_init):
    b, t, _ = q.shape
    lp = k.shape[1]
    hw = 2 * B_HEAD_DIM
    qbr = tiles.shape[2] - LANES
    assert t % qb == 0 and qoff % LANES == 0 and qoff + t - qb + qbr <= lp
    return pl.pallas_call(
        functools.partial(_diff_kernel, qb=qb, kb=kb, qoff=qoff, lam_init=lam_init),
        grid=(b, B_HEADS, t // qb),
        in_specs=[pl.BlockSpec(memory_space=pltpu.SMEM),
                  pl.BlockSpec(memory_space=pltpu.SMEM),
                  pl.BlockSpec((1, hw), lambda bi, h, i: (0, 0)),
                  pl.BlockSpec((None, qb, hw), lambda bi, h, i: (bi, i, h)),
                  pl.BlockSpec((None, lp, hw), lambda bi, h, i: (bi, 0, h)),
                  pl.BlockSpec((None, lp, hw), lambda bi, h, i: (bi, 0, h)),
                  pl.BlockSpec((1, qb, tiles.shape[2]), lambda bi, h, i: (A_HEADS + h, 0, 0))],
        out_specs=pl.BlockSpec((None, qb, hw), lambda bi, h, i: (bi, i, h)),
        out_shape=jax.ShapeDtypeStruct((b, t, B_HEADS * hw), BF16),
        scratch_shapes=[pltpu.VMEM((2, qb, 1), F32),
                        pltpu.VMEM((2, qb, 1), F32),
                        pltpu.VMEM((2, qb, hw), F32)],
        compiler_params=_params(3),
        name="mixer_b",
    )(rel_bias, lam, subln_g.reshape(1, hw), q, k, v, tiles)


def _lam_kernel(dl_ref, o_ref, *, lam_init):
    dl = dl_ref[...]
    s1 = jnp.sum(dl[0:1] * dl[1:2], axis=-1, keepdims=True)
    s2 = jnp.sum(dl[2:3] * dl[3:4], axis=-1, keepdims=True)
    o_ref[...] = jnp.exp(s1) - jnp.exp(s2) + lam_init


def _lam_call(diff_lam, lam_init):
    return pl.pallas_call(
        functools.partial(_lam_kernel, lam_init=lam_init),
        out_shape=jax.ShapeDtypeStruct((1, 1), F32),
        name="diff_lambda",
    )(diff_lam)


def _route(logits_t, rbias):
    tm = logits_t.shape[1]
    per = N_EXPERTS // N_GROUPS
    scores = jax.nn.sigmoid(logits_t).reshape(N_GROUPS, per, tm)
    ch = scores + rbias.reshape(N_GROUPS, per, 1)
    sub = lax.broadcasted_iota(jnp.int32, ch.shape, 1)
    grp = lax.broadcasted_iota(jnp.int32, ch.shape, 0)
    m1 = jnp.max(ch, axis=1, keepdims=True)
    i1 = jnp.min(jnp.where(ch == m1, sub, per), axis=1, keepdims=True)
    m2 = jnp.max(jnp.where(sub == i1, -jnp.inf, ch), axis=1, keepdims=True)
    gs = jnp.broadcast_to(m1 + m2, ch.shape)
    rank = jnp.zeros(ch.shape, jnp.int32)
    for g2 in range(N_GROUPS):
        o = gs[g2:g2 + 1]
        rank = rank + jnp.where(o > gs, 1, 0) + jnp.where(o == gs, 1, 0) * jnp.where(grp > g2, 1, 0)
    cm = jnp.where(rank < TOPK_GROUPS, ch, -jnp.inf)
    eid = grp * per + sub
    erank = jnp.zeros(cm.shape, jnp.int32)
    for e2 in range(N_EXPERTS):
        o = cm[e2 // per:e2 // per + 1, e2 % per:e2 % per + 1, :]
        erank = erank + jnp.where(o > cm, 1, 0) + jnp.where(o == cm, 1, 0) * jnp.where(eid > e2, 1, 0)
    sel = jnp.where(erank < TOP_K, scores, 0.0)
    denom = jnp.sum(jnp.sum(sel, axis=1, keepdims=True), axis=0, keepdims=True)
    return (sel / denom * ROUTED_SCALE).reshape(N_EXPERTS, tm)


def _outproj_kernel(x_ref, oa_ref, ob_ref, ga_ref, sc_ref, sh_ref, g_ref, w_ref, wr_ref, rb_ref,
                    x1_ref, h2_ref, gt_ref):
    half = oa_ref.shape[1]
    mix = (jnp.dot(oa_ref[...], w_ref[:half, :], preferred_element_type=F32)
           + jnp.dot(ob_ref[...], w_ref[half:, :], preferred_element_type=F32))
    x1 = x_ref[...] + ga_ref[...] * mix
    x1_ref[...] = x1
    h2 = _modulated_norm(x1, g_ref[...], sc_ref[...], sh_ref[...])
    h2_ref[...] = h2.astype(h2_ref.dtype)
    logits_t = lax.dot_general(wr_ref[...], h2, NT_DIMS, preferred_element_type=F32,
                               precision=lax.Precision.HIGHEST)
    gt_ref[...] = _route(logits_t, rb_ref[...])


def _outproj_call(x2d, oa, ob, ga, sc, sh, mod_map, g, w_bf, wr_t, rbias, tm):
    n, d = x2d.shape
    assert n % tm == 0
    r = sc.shape[1]
    mod_spec = pl.BlockSpec((None, r, d), mod_map)
    return pl.pallas_call(
        _outproj_kernel,
        grid=(n // tm,),
        in_specs=[pl.BlockSpec((tm, d), lambda i: (i, 0)),
                  pl.BlockSpec((tm, oa.shape[1]), lambda i: (i, 0)),
                  pl.BlockSpec((tm, ob.shape[1]), lambda i: (i, 0)),
                  mod_spec, mod_spec, mod_spec,
                  pl.BlockSpec((1, d), lambda i: (0, 0)),
                  _resident(w_bf.shape, lambda i: (0, 0)),
                  _resident(wr_t.shape, lambda i: (0, 0)),
                  pl.BlockSpec((N_EXPERTS, 1), lambda i: (0, 0))],
        out_specs=[pl.BlockSpec((tm, d), lambda i: (i, 0)),
                   pl.BlockSpec((tm, d), lambda i: (i, 0)),
                   pl.BlockSpec((N_EXPERTS, tm), lambda i: (0, i))],
        out_shape=[jax.ShapeDtypeStruct((n, d), F32),
                   jax.ShapeDtypeStruct((n, d), BF16),
                   jax.ShapeDtypeStruct((N_EXPERTS, n), F32)],
        compiler_params=_params(1),
        name="out_proj_router",
    )(x2d, oa, ob, ga, sc, sh, g.reshape(1, d), w_bf, wr_t, rbias)


def _swiglu_hidden(h, wg, wu):
    a = jnp.dot(h, wg, preferred_element_type=F32)
    u = jnp.dot(h, wu, preferred_element_type=F32)
    return a * jax.nn.sigmoid(a) * u


def _moe_kernel(h_ref, gates_ref, wg_ref, wu_ref, wd_ref, sg_ref, su_ref, sd_ref, o_ref):
    e = pl.program_id(1)
    h = h_ref[...]

    @pl.when(e == 0)
    def _():
        hid = _swiglu_hidden(h, sg_ref[...], su_ref[...])
        o_ref[...] = jnp.dot(hid.astype(BF16), sd_ref[...], preferred_element_type=F32)

    lane = lax.broadcasted_iota(jnp.int32, gates_ref.shape, 1)
    gate = jnp.sum(jnp.where(lane == e, gates_ref[...], 0.0), axis=-1, keepdims=True)
    hid = _swiglu_hidden(h, wg_ref[...], wu_ref[...]) * gate
    o_ref[...] += jnp.dot(hid.astype(BF16), wd_ref[...], preferred_element_type=F32)


def _moe_call(h2, gates, wg, wu, wd, sg, su, sd, tm):
    n, d = h2.shape
    assert n % tm == 0
    ne, _, de = wg.shape
    return pl.pallas_call(
        _moe_kernel,
        grid=(n // tm, ne),
        in_specs=[pl.BlockSpec((tm, d), lambda i, e: (i, 0)),
                  pl.BlockSpec((tm, ne), lambda i, e: (i, 0)),
                  pl.BlockSpec((None, d, de), lambda i, e: (e, 0, 0)),
                  pl.BlockSpec((None, d, de), lambda i, e: (e, 0, 0)),
                  pl.BlockSpec((None, de, d), lambda i, e: (e, 0, 0)),
                  _resident(sg.shape, lambda i, e: (0, 0)),
                  _resident(su.shape, lambda i, e: (0, 0)),
                  _resident(sd.shape, lambda i, e: (0, 0))],
        out_specs=pl.BlockSpec((tm, d), lambda i, e: (i, 0)),
        out_shape=jax.ShapeDtypeStruct((n, d), F32),
        compiler_params=_params(2),
        name="moe",
    )(h2, gates, wg, wu, wd, sg, su, sd)


def _final_kernel(x_ref, m_ref, gf_ref, g_ref, o_ref):
    x = x_ref[...] + gf_ref[...] * m_ref[...]
    o_ref[...] = x * lax.rsqrt(jnp.mean(x * x, axis=-1, keepdims=True) + EPS) * g_ref[...]


def _final_call(x1, moe, gf, mod_map, g, tm):
    n, d = x1.shape
    assert n % tm == 0
    r = gf.shape[1]
    return pl.pallas_call(
        _final_kernel,
        grid=(n // tm,),
        in_specs=[pl.BlockSpec((tm, d), lambda i: (i, 0)),
                  pl.BlockSpec((tm, d), lambda i: (i, 0)),
                  pl.BlockSpec((None, r, d), mod_map),
                  pl.BlockSpec((1, d), lambda i: (0, 0))],
        out_specs=pl.BlockSpec((tm, d), lambda i: (i, 0)),
        out_shape=jax.ShapeDtypeStruct((n, d), F32),
        compiler_params=_params(1),
        name="final_norm",
    )(x1, moe, gf, g.reshape(1, d))


_QA, _KA, _VA, _QI, _KI, _IW, _QB, _KB, _VB = range(9)
_IN_SIZES = (A_HEADS * A_HEAD_DIM, A_KV_HEADS * A_HEAD_DIM, A_KV_HEADS * A_HEAD_DIM,
             IDX_HEADS * IDX_DIM, IDX_DIM, IDX_HEADS,
             B_HEADS * 2 * B_HEAD_DIM, B_HEADS * 2 * B_HEAD_DIM, B_HEADS * 2 * B_HEAD_DIM)


def _split_w_in(w_in):
    d = w_in.shape[0]
    offs = np.concatenate([[0], np.cumsum(_IN_SIZES)])
    cols = [w_in[:, offs[k]:offs[k + 1]].astype(BF16) for k in range(9)]
    zero = lambda n: jnp.zeros((d, n), BF16)
    qi = jnp.concatenate([cols[_QI].reshape(d, IDX_HEADS, IDX_DIM),
                          jnp.zeros((d, IDX_HEADS, LANES - IDX_DIM), BF16)], axis=-1).reshape(d, IDX_HEADS * LANES)
    w_a = jnp.concatenate([cols[_QA], cols[_KA], cols[_VA], qi,
                           cols[_KI], zero(LANES - IDX_DIM), cols[_IW], zero(LANES - IDX_HEADS)], axis=1)
    w_b = jnp.concatenate([cols[_QB], cols[_KB], cols[_VB]], axis=1)
    return w_a, w_b


def _group_forward(x, mod, past, rel_bias, lam, weights, *, lam_init, qb, kb, tm, per_row_mod):
    (norm_a_g, w_a, w_b, subln_g) = weights
    b, t, d = x.shape
    n = b * t
    x2d = x.reshape(n, d)
    sh_a, sc_a = mod[0], mod[1]
    if per_row_mod:
        expand = lambda m: jnp.repeat(m, t, axis=0).reshape(1, n, d)
        mod_map = lambda i: (0, i, 0)
    else:
        expand = lambda m: m.reshape(b, 1, d)
        mod_map = lambda i: ((i * tm) // t, 0, 0)
    sc3, sh3 = expand(sc_a), expand(sh_a)

    na = A_HEADS * A_HEAD_DIM
    nk = A_KV_HEADS * A_HEAD_DIM
    nq = IDX_HEADS * LANES
    c_ka, c_va, c_qi = na, na + nk, na + 2 * nk
    c_ki, c_iw = c_qi + nq, c_qi + nq + LANES
    defs_a = [(0, na, na), (c_ka, nk, nk), (c_ka, nk, nk), (c_va, nk, nk), (c_va, nk, nk),
              (c_qi, nq, nq), (c_ki, LANES, IDX_DIM), (c_ki, LANES, LANES), (c_iw, LANES, LANES)]
    dt_a = [BF16, F32, BF16, F32, BF16, BF16, F32, BF16, F32]
    qa, ka, ka_bf, va, va_bf, qi, ki, ki_bf, iw = _norm_proj_call(
        x2d, sc3, sh3, mod_map, norm_a_g, w_a, defs_a, dt_a, tm, "in_proj_a")
    nb = B_HEADS * 2 * B_HEAD_DIM
    defs_b = [(0, nb, nb), (nb, nb, nb), (nb, nb, nb), (2 * nb, nb, nb), (2 * nb, nb, nb)]
    dt_b = [BF16, F32, BF16, F32, BF16]
    qbm, kbm, kb_bf, vbm, vb_bf = _norm_proj_call(
        x2d, sc3, sh3, mod_map, norm_a_g, w_b, defs_b, dt_b, tm, "in_proj_b")

    r3 = lambda a: a.reshape(b, t, a.shape[-1])
    k_a, v_a, k_i, k_b, v_b = r3(ka_bf), r3(va_bf), r3(ki_bf), r3(kb_bf), r3(vb_bf)
    qoff = 0
    if past is not None:
        pk, pv, pi, pbk, pbv = past
        qoff = pk.shape[1]
        qbr = -(-qb // LANES) * LANES
        pad = qbr - t

        def join(p, new):
            p2 = p.reshape(b, qoff, -1).astype(BF16)
            if p2.shape[-1] < new.shape[-1]:
                p2 = jnp.concatenate([p2, jnp.zeros((b, qoff, new.shape[-1] - p2.shape[-1]), BF16)], axis=-1)
            return jnp.concatenate([p2, new, jnp.zeros((b, pad, new.shape[-1]), BF16)], axis=1)

        k_a, v_a, k_i, k_b, v_b = join(pk, k_a), join(pv, v_a), join(pi, k_i), join(pbk, k_b), join(pbv, v_b)
    length = qoff + t
    topk = min(TOPK_MAX, length // 4)

    tiles = _bias_tile_call(rel_bias, qb)
    out_a = _dsa_call(rel_bias, tiles, r3(qa), r3(qi), r3(iw), k_a, v_a, k_i, qb=qb, kb=kb, qoff=qoff, topk=topk)
    out_b = _diff_call(rel_bias, tiles, lam, subln_g, r3(qbm), k_b, v_b, qb=qb, kb=kb, qoff=qoff,
                       lam_init=lam_init)
    rows = (ka.reshape(1, b, t, A_KV_HEADS, A_HEAD_DIM), va.reshape(1, b, t, A_KV_HEADS, A_HEAD_DIM),
            ki.reshape(1, b, t, IDX_DIM), kbm.reshape(1, b, t, B_HEADS, 2, B_HEAD_DIM),
            vbm.reshape(1, b, t, B_HEADS, 2 * B_HEAD_DIM))
    return x2d, out_a.reshape(n, -1), out_b.reshape(n, -1), rows, expand, mod_map


def kernel(x_prompt, x_sample, c_prompt, c_sample, cache_a_k, cache_a_v, cache_a_kidx, cache_b_k, cache_b_v,
           rel_bias, w_ada, b_ada, norm_a_g, w_in, w_out, diff_lam, subln_g, norm_f_g, w_router, router_bias,
           w_gate, w_up, w_down, ws_gate, ws_up, ws_down, final_g):
    assert w_ada.shape[0] == 1, "single-layer model"
    d = x_prompt.shape[-1]
    bp, tp, _ = x_prompt.shape
    bs, ts, _ = x_sample.shape
    lam_init = 0.8 - 0.6 * math.exp(-0.3 * 0)

    c_all = jnp.concatenate([c_prompt, c_sample], axis=0)
    rows = -(-c_all.shape[0] // 8) * 8
    c_pad = jnp.concatenate([c_all, jnp.zeros((rows - c_all.shape[0], d), F32)], axis=0)
    mod_all = _ada_call(c_pad, w_ada[0], b_ada[0])
    mod_p = [mod_all[:bp, k * d:(k + 1) * d] for k in range(6)]
    mod_s = [mod_all[bp:bp + bs, k * d:(k + 1) * d] for k in range(6)]

    w_a, w_b = _split_w_in(w_in[0])
    lam = _lam_call(diff_lam[0], lam_init)
    weights = (norm_a_g[0], w_a, w_b, subln_g[0])
    past = (cache_a_k[0], cache_a_v[0], cache_a_kidx[0], cache_b_k[0], cache_b_v[0])

    tm_p = 256
    tm_s = bs * ts
    xp2, oa_p, ob_p, rows_p, exp_p, map_p = _group_forward(
        x_prompt, mod_p, None, rel_bias, lam, weights, lam_init=lam_init, qb=128, kb=512, tm=tm_p,
        per_row_mod=False)
    xs2, oa_s, ob_s, rows_s, exp_s, map_s = _group_forward(
        x_sample, mod_s, past, rel_bias, lam, weights, lam_init=lam_init, qb=ts, kb=512, tm=tm_s,
        per_row_mod=True)

    w_out_bf = w_out[0].astype(BF16)
    wr_t = w_router[0].T
    rb = router_bias[0].reshape(N_EXPERTS, 1)
    x1_p, h2_p, gt_p = _outproj_call(xp2, oa_p, ob_p, exp_p(mod_p[2]), exp_p(mod_p[4]), exp_p(mod_p[3]),
                                     map_p, norm_f_g[0], w_out_bf, wr_t, rb, tm_p)
    x1_s, h2_s, gt_s = _outproj_call(xs2, oa_s, ob_s, exp_s(mod_s[2]), exp_s(mod_s[4]), exp_s(mod_s[3]),
                                     map_s, norm_f_g[0], w_out_bf, wr_t, rb, tm_s)

    h2 = jnp.concatenate([h2_p, h2_s], axis=0)
    gates = jnp.concatenate([gt_p, gt_s], axis=1).T
    n_all = h2.shape[0]
    tm_moe = next(c for c in (640, 512, 256, 128, 64, 32, 16, 8) if n_all % c == 0)
    moe = _moe_call(h2, gates, w_gate[0].astype(BF16), w_up[0].astype(BF16), w_down[0].astype(BF16),
                    ws_gate[0].astype(BF16), ws_up[0].astype(BF16), ws_down[0].astype(BF16), tm_moe)
    n_p = bp * tp
    y_p = _final_call(x1_p, moe[:n_p], exp_p(mod_p[5]), map_p, final_g, tm_p)
    y_s = _final_call(x1_s, moe[n_p:], exp_s(mod_s[5]), map_s, final_g, tm_s)
    return (y_p.reshape(bp, tp, d), y_s.reshape(bs, ts, d)) + rows_p + rows_s
```

```python
import functools
import math

import numpy as np
import jax
import jax.numpy as jnp
from jax import lax
from jax.experimental import pallas as pl
from jax.experimental.pallas import tpu as pltpu

D_MODEL = 2048
CHUNK = 64
A_HEADS = 8
A_KV_HEADS = 2
A_REP = A_HEADS // A_KV_HEADS
A_HEAD_DIM = 128
IDX_HEADS = 16
IDX_DIM = 64
TOPK_MAX = 256
B_HEADS = 4
B_HEAD_DIM = 128
N_REL_BUCKETS = 32
REL_MAX_DIST = 128
N_ATTN_HEADS = A_HEADS + B_HEADS
N_EXPERTS = 64
TOP_K = 8
N_GROUPS = 8
TOPK_GROUPS = 4
EXPERT_DIM = 512
SHARED_DIM = 512
ROUTED_SCALE = 2.5
EPS = 1e-6

LANES = 128
INT_MIN = -(2 ** 31)
NEG_BIG = -1e30
VMEM_LIMIT = 52 * 1024 * 1024

F32 = jnp.float32
BF16 = jnp.bfloat16
NT_DIMS = (((1,), (1,)), ((), ()))


def _params(n_axes):
    return pltpu.CompilerParams(dimension_semantics=("arbitrary",) * n_axes,
                                vmem_limit_bytes=VMEM_LIMIT)


def _resident(shape, index_map):
    return pl.BlockSpec(shape, index_map, pipeline_mode=pl.Buffered(1))


def _bucket_thresholds():
    nb = N_REL_BUCKETS // 2
    max_exact = nb // 2
    n = np.arange(max_exact, 4 * REL_MAX_DIST, dtype=np.int32)
    nf = n.astype(np.float32)
    large = max_exact + (np.log(nf / np.float32(max_exact)) / np.float32(math.log(REL_MAX_DIST / max_exact))
                         * np.float32(nb - max_exact)).astype(np.int32)
    large = np.minimum(large, nb - 1)
    assert np.all(np.diff(large) >= 0)
    ths = [int(n[np.argmax(large >= b)]) for b in range(max_exact + 1, nb)]
    return max_exact, nb, ths


_MAX_EXACT, _NB, _BUCKET_THS = _bucket_thresholds()
assert _BUCKET_THS[-1] <= LANES + 1
FAR_BUCKET = _NB - 1


def _ada_kernel(c_ref, w_ref, b_ref, o_ref):
    c = c_ref[...]
    s = c * jax.nn.sigmoid(c)
    o_ref[...] = jnp.dot(s, w_ref[...], preferred_element_type=F32,
                         precision=lax.Precision.HIGHEST) + b_ref[...]


def _ada_call(c, w_ada, b_ada):
    rows, d = c.shape
    n = w_ada.shape[1]
    tn = 1536
    assert n % tn == 0
    return pl.pallas_call(
        _ada_kernel,
        grid=(n // tn,),
        in_specs=[pl.BlockSpec((rows, d), lambda j: (0, 0)),
                  pl.BlockSpec((d, tn), lambda j: (0, j)),
                  pl.BlockSpec((1, tn), lambda j: (0, j))],
        out_specs=pl.BlockSpec((rows, tn), lambda j: (0, j)),
        out_shape=jax.ShapeDtypeStruct((rows, n), F32),
        compiler_params=_params(1),
        name="ada_mod",
    )(c, w_ada, b_ada.reshape(1, n))


def _modulated_norm(x, g, sc, sh):
    y = x * lax.rsqrt(jnp.mean(x * x, axis=-1, keepdims=True) + EPS) * g
    return y * (1.0 + sc) + sh


def _norm_proj_kernel(x_ref, sc_ref, sh_ref, g_ref, w_ref, *out_refs, defs):
    h = _modulated_norm(x_ref[...], g_ref[...], sc_ref[...], sh_ref[...]).astype(BF16)
    done = {}
    for (start, width, store_w, mult), o_ref in zip(defs, out_refs):
        if (start, width) not in done:
            done[(start, width)] = jnp.dot(h, w_ref[:, start:start + width], preferred_element_type=F32)
        r = done[(start, width)][:, :store_w]
        o_ref[...] = (r if mult is None else r * mult).astype(o_ref.dtype)


def _norm_proj_call(x2d, sc, sh, mod_map, g, w_bf, defs, dtypes, tm, name):
    n, d = x2d.shape
    assert n % tm == 0
    r = sc.shape[1]
    out_shapes = [jax.ShapeDtypeStruct((n, d_[2]), dt) for d_, dt in zip(defs, dtypes)]
    out_specs = [pl.BlockSpec((tm, d_[2]), lambda i: (i, 0)) for d_ in defs]
    return pl.pallas_call(
        functools.partial(_norm_proj_kernel, defs=tuple(defs)),
        grid=(n // tm,),
        in_specs=[pl.BlockSpec((tm, d), lambda i: (i, 0)),
                  pl.BlockSpec((None, r, d), mod_map),
                  pl.BlockSpec((None, r, d), mod_map),
                  pl.BlockSpec((1, d), lambda i: (0, 0)),
                  _resident(w_bf.shape, lambda i: (0, 0))],
        out_specs=out_specs,
        out_shape=out_shapes,
        compiler_params=_params(1),
        name=name,
    )(x2d, sc, sh, g.reshape(1, d), w_bf)


def _bias_tile_kernel(tab_ref, o_ref, *, qb):
    _, rows, width = o_ref.shape
    r = lax.broadcasted_iota(jnp.int32, (rows, width), 0)
    koff = lax.broadcasted_iota(jnp.int32, (rows, width), 1) - LANES
    rel = koff - r
    n = jnp.abs(rel)
    large = jnp.full((rows, width), _MAX_EXACT, jnp.int32)
    for th in _BUCKET_THS:
        large = large + jnp.where(n >= th, 1, 0)
    bucket = jnp.where(rel > 0, _NB, 0) + jnp.where(n < _MAX_EXACT, n, large)
    visible = ((koff >> 6) <= (r >> 6)) & (koff < qb)
    for h in range(N_ATTN_HEADS):
        b = jnp.zeros((rows, width), F32)
        for bk in range(N_REL_BUCKETS):
            b = jnp.where(bucket == bk, tab_ref[bk, h], b)
        o_ref[h] = jnp.where(visible, b, NEG_BIG)


def _bias_tile_call(rel_bias, qb):
    qbr = -(-qb // LANES) * LANES
    return pl.pallas_call(
        functools.partial(_bias_tile_kernel, qb=qb),
        in_specs=[pl.BlockSpec(memory_space=pltpu.SMEM)],
        out_specs=pl.BlockSpec(memory_space=pltpu.VMEM),
        out_shape=jax.ShapeDtypeStruct((N_ATTN_HEADS, qb, LANES + qbr), F32),
        name="bias_tiles",
    )(rel_bias)


def _zones(q0, kb):
    far_end = jnp.maximum(q0 - LANES, 0)
    n_far = far_end // kb
    n_mid = (far_end - n_far * kb) // LANES
    return n_far, n_mid


def _sweep_far(q0, kb, fn):
    n_far, n_mid = _zones(q0, kb)
    sub = kb // LANES

    def far_body(j, carry):
        fn(pl.multiple_of(j * kb, kb), kb, j * sub)
        return carry

    lax.fori_loop(0, n_far, far_body, 0)

    def mid_body(j, carry):
        blk = n_far * sub + j
        fn(pl.multiple_of(blk * LANES, LANES), LANES, blk)
        return carry

    lax.fori_loop(0, n_mid, mid_body, 0)


def _dsa_kernel(tab_ref, q_ref, qi_ref, iw_ref, k_ref, v_ref, ki_ref, tile_ref, o_ref,
                keys_scr, mfar_scr, mnear_scr, l_scr, acc_scr, *, qb, kb, qoff, topk):
    i = pl.program_id(1)
    q0 = qoff + i * qb
    qbr = tile_ref.shape[2] - LANES
    sub = kb // LANES
    w = iw_ref[:, :IDX_HEADS] * (IDX_HEADS ** -0.5 * IDX_DIM ** -0.5)

    def score_keys(start, width):
        kib = ki_ref[pl.ds(start, width), :]
        acc = jnp.zeros((qb, width), F32)
        for h in range(IDX_HEADS):
            d = lax.dot_general(qi_ref[:, h * LANES:(h + 1) * LANES], kib, NT_DIMS,
                                preferred_element_type=F32)
            acc = acc + w[:, h:h + 1] * jnp.maximum(d, 0.0)
        bits = pltpu.bitcast(acc, jnp.int32)
        return bits ^ ((bits >> 31) & 0x7FFFFFFF)

    n_full = q0 // kb
    n_rem = (q0 - n_full * kb) // LANES

    def full_body(j, carry):
        key = score_keys(pl.multiple_of(j * kb, kb), kb)
        for c in range(sub):
            keys_scr[j * sub + c] = key[:, c * LANES:(c + 1) * LANES]
        return carry

    lax.fori_loop(0, n_full, full_body, 0)

    def rem_body(j, carry):
        blk = n_full * sub + j
        keys_scr[blk] = score_keys(pl.multiple_of(blk * LANES, LANES), LANES)
        return carry

    lax.fori_loop(0, n_rem, rem_body, 0)

    cur_blk = q0 // LANES
    key = score_keys(pl.multiple_of(q0, LANES), qbr)
    key = jnp.where(tile_ref[0, :, LANES:] > 0.5 * NEG_BIG, key, INT_MIN)
    for c in range(qbr // LANES):
        keys_scr[cur_blk + c] = key[:, c * LANES:(c + 1) * LANES]
    n_blk = cur_blk + qbr // LANES

    def count_ge(t):
        tb = jnp.broadcast_to(t, (qb, LANES))
        n4 = n_blk // 4

        def body4(j, cnt):
            for u in range(4):
                cnt = cnt + jnp.where(keys_scr[4 * j + u] >= tb, 1, 0)
            return cnt

        def body1(j, cnt):
            return cnt + jnp.where(keys_scr[4 * n4 + j] >= tb, 1, 0)

        cnt = lax.fori_loop(0, n4, body4, jnp.zeros((qb, LANES), jnp.int32))
        cnt = lax.fori_loop(0, n_blk - 4 * n4, body1, cnt)
        return jnp.sum(cnt, axis=-1, keepdims=True)

    def bit_body(it, t):
        cand = t + (jnp.int32(1) << (31 - it))
        return jnp.where(count_ge(cand) >= topk, cand, t)

    thr = lax.fori_loop(0, 32, bit_body, jnp.full((qb, 1), INT_MIN, jnp.int32))
    thr = jnp.maximum(thr, INT_MIN + 1)
    thr_b = jnp.broadcast_to(thr, (qb, LANES))

    hd = A_HEAD_DIM

    def select_mask(blk0, n):
        return [jnp.where(keys_scr[blk0 + c] >= thr_b, 0.0, NEG_BIG) for c in range(n)]

    def logits(h, kblk):
        g = h // A_REP
        s = lax.dot_general(q_ref[:, h * hd:(h + 1) * hd], kblk[:, g * hd:(g + 1) * hd], NT_DIMS,
                            preferred_element_type=F32)
        return [s[:, c * LANES:(c + 1) * LANES] for c in range(s.shape[1] // LANES)]

    def tile_cols(h, col0, n):
        return [tile_ref[h, :, col0 + c * LANES:col0 + (c + 1) * LANES] for c in range(n)]

    mfar_scr[...] = jnp.full(mfar_scr.shape, NEG_BIG, F32)
    mnear_scr[...] = jnp.full(mnear_scr.shape, NEG_BIG, F32)

    def max_far(start, width, blk0):
        kblk = k_ref[pl.ds(start, width), :]
        am = select_mask(blk0, width // LANES)
        for h in range(A_HEADS):
            mfar_scr[h] = functools.reduce(
                jnp.maximum, [s + a for s, a in zip(logits(h, kblk), am)], mfar_scr[h])

    def max_near(start, width, blk0, col0):
        kblk = k_ref[pl.ds(start, width), :]
        am = select_mask(blk0, width // LANES)
        for h in range(A_HEADS):
            mnear_scr[h] = functools.reduce(
                jnp.maximum,
                [s + a + b for s, a, b in zip(logits(h, kblk), am, tile_cols(h, col0, width // LANES))],
                mnear_scr[h])

    prev_start = pl.multiple_of(q0 - LANES, LANES)
    cur_start = pl.multiple_of(q0, LANES)
    _sweep_far(q0, kb, max_far)
    pl.when(q0 > 0)(lambda: max_near(prev_start, LANES, cur_blk - 1, 0))
    max_near(cur_start, qbr, cur_blk, LANES)

    for h in range(A_HEADS):
        fb = tab_ref[FAR_BUCKET, h]
        m = jnp.maximum(jnp.max(mfar_scr[h], axis=-1, keepdims=True) + fb,
                        jnp.max(mnear_scr[h], axis=-1, keepdims=True))
        mfar_scr[h] = jnp.broadcast_to(fb - m, (qb, LANES))
        mnear_scr[h] = jnp.broadcast_to(-m, (qb, LANES))
    l_scr[...] = jnp.zeros(l_scr.shape, F32)
    acc_scr[...] = jnp.zeros(acc_scr.shape, F32)

    def accumulate(h, p, vblk):
        g = h // A_REP
        l_scr[h] += functools.reduce(jnp.add, p)
        acc_scr[h] += jnp.dot(jnp.concatenate(p, axis=1).astype(BF16), vblk[:, g * hd:(g + 1) * hd],
                              preferred_element_type=F32)

    def acc_far(start, width, blk0):
        kblk = k_ref[pl.ds(start, width), :]
        vblk = v_ref[pl.ds(start, width), :]
        am = select_mask(blk0, width // LANES)
        for h in range(A_HEADS):
            shift = mfar_scr[h]
            accumulate(h, [jnp.exp(s + a + shift) for s, a in zip(logits(h, kblk), am)], vblk)

    def acc_near(start, width, blk0, col0):
        kblk = k_ref[pl.ds(start, width), :]
        vblk = v_ref[pl.ds(start, width), :]
        am = select_mask(blk0, width // LANES)
        for h in range(A_HEADS):
            shift = mnear_scr[h]
            accumulate(h, [jnp.exp(s + a + b + shift) for s, a, b in
                           zip(logits(h, kblk), am, tile_cols(h, col0, width // LANES))], vblk)

    _sweep_far(q0, kb, acc_far)
    pl.when(q0 > 0)(lambda: acc_near(prev_start, LANES, cur_blk - 1, 0))
    acc_near(cur_start, qbr, cur_blk, LANES)

    for h in range(A_HEADS):
        o_ref[:, h * hd:(h + 1) * hd] = (
            acc_scr[h] / jnp.sum(l_scr[h], axis=-1, keepdims=True)).astype(o_ref.dtype)


def _dsa_call(rel_bias, tiles, q, qi, iw, k, v, ki, *, qb, kb, qoff, topk):
    b, t, _ = q.shape
    lp = k.shape[1]
    qbr = tiles.shape[2] - LANES
    assert t % qb == 0 and qoff % LANES == 0 and qoff + t - qb + qbr <= lp
    n_slabs = (qoff + t - qb + qbr) // LANES
    return pl.pallas_call(
        functools.partial(_dsa_kernel, qb=qb, kb=kb, qoff=qoff, topk=topk),
        grid=(b, t // qb),
        in_specs=[pl.BlockSpec(memory_space=pltpu.SMEM),
                  pl.BlockSpec((None, qb, q.shape[2]), lambda bi, i: (bi, i, 0)),
                  pl.BlockSpec((None, qb, qi.shape[2]), lambda bi, i: (bi, i, 0)),
                  pl.BlockSpec((None, qb, iw.shape[2]), lambda bi, i: (bi, i, 0)),
                  pl.BlockSpec((None, lp, k.shape[2]), lambda bi, i: (bi, 0, 0)),
                  pl.BlockSpec((None, lp, v.shape[2]), lambda bi, i: (bi, 0, 0)),
                  pl.BlockSpec((None, lp, ki.shape[2]), lambda bi, i: (bi, 0, 0)),
                  _resident(tiles.shape, lambda bi, i: (0, 0, 0))],
        out_specs=pl.BlockSpec((None, qb, A_HEADS * A_HEAD_DIM), lambda bi, i: (bi, i, 0)),
        out_shape=jax.ShapeDtypeStruct((b, t, A_HEADS * A_HEAD_DIM), BF16),
        scratch_shapes=[pltpu.VMEM((n_slabs, qb, LANES), jnp.int32),
                        pltpu.VMEM((A_HEADS, qb, LANES), F32),
                        pltpu.VMEM((A_HEADS, qb, LANES), F32),
                        pltpu.VMEM((A_HEADS, qb, LANES), F32),
                        pltpu.VMEM((A_HEADS, qb, A_HEAD_DIM), F32)],
        compiler_params=_params(2),
        name="mixer_a",
    )(rel_bias, q, qi, iw, k, v, ki, tiles)


def _diff_kernel(tab_ref, lam_ref, sg_ref, q_ref, k_ref, v_ref, tile_ref, o_ref,
                 mfar_scr, mnear_scr, l_scr, acc_scr, *, qb, kb, qoff, lam_init):
    head = pl.program_id(1)
    i = pl.program_id(2)
    q0 = qoff + i * qb
    qbr = tile_ref.shape[2] - LANES
    hd = B_HEAD_DIM
    far_bias = tab_ref[FAR_BUCKET, A_HEADS + head]

    def logits(c, kblk):
        s = lax.dot_general(q_ref[:, c * hd:(c + 1) * hd], kblk[:, c * hd:(c + 1) * hd], NT_DIMS,
                            preferred_element_type=F32)
        return [s[:, u * LANES:(u + 1) * LANES] for u in range(s.shape[1] // LANES)]

    def tile_cols(col0, n):
        return [tile_ref[0, :, col0 + u * LANES:col0 + (u + 1) * LANES] for u in range(n)]

    mfar_scr[...] = jnp.full(mfar_scr.shape, NEG_BIG, F32)
    mnear_scr[...] = jnp.full(mnear_scr.shape, NEG_BIG, F32)

    def max_far(start, width, blk0):
        kblk = k_ref[pl.ds(start, width), :]
        for c in range(2):
            mfar_scr[c] = functools.reduce(jnp.maximum, logits(c, kblk), mfar_scr[c])

    def max_near(start, width, col0):
        kblk = k_ref[pl.ds(start, width), :]
        for c in range(2):
            mnear_scr[c] = functools.reduce(
                jnp.maximum, [s + b for s, b in zip(logits(c, kblk), tile_cols(col0, width // LANES))],
                mnear_scr[c])

    prev_start = pl.multiple_of(q0 - LANES, LANES)
    cur_start = pl.multiple_of(q0, LANES)
    _sweep_far(q0, kb, max_far)
    pl.when(q0 > 0)(lambda: max_near(prev_start, LANES, 0))
    max_near(cur_start, qbr, LANES)

    for c in range(2):
        m = jnp.maximum(jnp.max(mfar_scr[c], axis=-1, keepdims=True) + far_bias,
                        jnp.max(mnear_scr[c], axis=-1, keepdims=True))
        mfar_scr[c] = jnp.broadcast_to(far_bias - m, (qb, LANES))
        mnear_scr[c] = jnp.broadcast_to(-m, (qb, LANES))
    l_scr[...] = jnp.zeros(l_scr.shape, F32)
    acc_scr[...] = jnp.zeros(acc_scr.shape, F32)

    def accumulate(c, p, vblk):
        l_scr[c] += functools.reduce(jnp.add, p)
        acc_scr[c] += jnp.dot(jnp.concatenate(p, axis=1).astype(BF16), vblk, preferred_element_type=F32)

    def acc_far(start, width, blk0):
        kblk = k_ref[pl.ds(start, width), :]
        vblk = v_ref[pl.ds(start, width), :]
        for c in range(2):
            shift = mfar_scr[c]
            accumulate(c, [jnp.exp(s + shift) for s in logits(c, kblk)], vblk)

    def acc_near(start, width, col0):
        kblk = k_ref[pl.ds(start, width), :]
        vblk = v_ref[pl.ds(start, width), :]
        for c in range(2):
            shift = mnear_scr[c]
            accumulate(c, [jnp.exp(s + b + shift) for s, b in
                           zip(logits(c, kblk), tile_cols(col0, width // LANES))], vblk)

    _sweep_far(q0, kb, acc_far)
    pl.when(q0 > 0)(lambda: acc_near(prev_start, LANES, 0))
    acc_near(cur_start, qbr, LANES)

    lam = lam_ref[0, 0]
    o = (acc_scr[0] / jnp.sum(l_scr[0], axis=-1, keepdims=True)
         - lam * (acc_scr[1] / jnp.sum(l_scr[1], axis=-1, keepdims=True)))
    o = o * lax.rsqrt(jnp.mean(o * o, axis=-1, keepdims=True) + EPS) * sg_ref[...]
    o_ref[...] = (o * (1.0 - lam_init)).astype(o_ref.dtype)


def _diff_call(rel_bias, tiles, lam, subln_g, q, k, v, *, qb, kb, qoff, lam_init):
    b, t, _ = q.shape
    lp = k.shape[1]
    hw = 2 * B_HEAD_DIM
    qbr = tiles.shape[2] - LANES
    assert t % qb == 0 and qoff % LANES == 0 and qoff + t - qb + qbr <= lp
    return pl.pallas_call(
        functools.partial(_diff_kernel, qb=qb, kb=kb, qoff=qoff, lam_init=lam_init),
        grid=(b, B_HEADS, t // qb),
        in_specs=[pl.BlockSpec(memory_space=pltpu.SMEM),
                  pl.BlockSpec(memory_space=pltpu.SMEM),
                  pl.BlockSpec((1, hw), lambda bi, h, i: (0, 0)),
                  pl.BlockSpec((None, qb, hw), lambda bi, h, i: (bi, i, h)),
                  pl.BlockSpec((None, lp, hw), lambda bi, h, i: (bi, 0, h)),
                  pl.BlockSpec((None, lp, hw), lambda bi, h, i: (bi, 0, h)),
                  pl.BlockSpec((1, qb, tiles.shape[2]), lambda bi, h, i: (A_HEADS + h, 0, 0))],
        out_specs=pl.BlockSpec((None, qb, hw), lambda bi, h, i: (bi, i, h)),
        out_shape=jax.ShapeDtypeStruct((b, t, B_HEADS * hw), BF16),
        scratch_shapes=[pltpu.VMEM((2, qb, LANES), F32),
                        pltpu.VMEM((2, qb, LANES), F32),
                        pltpu.VMEM((2, qb, LANES), F32),
                        pltpu.VMEM((2, qb, hw), F32)],
        compiler_params=_params(3),
        name="mixer_b",
    )(rel_bias, lam, subln_g.reshape(1, hw), q, k, v, tiles)


def _lam_kernel(dl_ref, o_ref, *, lam_init):
    dl = dl_ref[...]
    s1 = jnp.sum(dl[0:1] * dl[1:2], axis=-1, keepdims=True)
    s2 = jnp.sum(dl[2:3] * dl[3:4], axis=-1, keepdims=True)
    o_ref[...] = jnp.exp(s1) - jnp.exp(s2) + lam_init


def _lam_call(diff_lam, lam_init):
    return pl.pallas_call(
        functools.partial(_lam_kernel, lam_init=lam_init),
        out_shape=jax.ShapeDtypeStruct((1, 1), F32),
        name="diff_lambda",
    )(diff_lam)


def _route(logits_t, rbias):
    tm = logits_t.shape[1]
    per = N_EXPERTS // N_GROUPS
    scores = jax.nn.sigmoid(logits_t).reshape(N_GROUPS, per, tm)
    ch = scores + rbias.reshape(N_GROUPS, per, 1)
    sub = lax.broadcasted_iota(jnp.int32, ch.shape, 1)
    grp = lax.broadcasted_iota(jnp.int32, ch.shape, 0)
    m1 = jnp.max(ch, axis=1, keepdims=True)
    i1 = jnp.min(jnp.where(ch == m1, sub, per), axis=1, keepdims=True)
    m2 = jnp.max(jnp.where(sub == i1, -jnp.inf, ch), axis=1, keepdims=True)
    gs = jnp.broadcast_to(m1 + m2, ch.shape)
    rank = jnp.zeros(ch.shape, jnp.int32)
    for g2 in range(N_GROUPS):
        o = gs[g2:g2 + 1]
        rank = rank + jnp.where(o > gs, 1, 0) + jnp.where(o == gs, 1, 0) * jnp.where(grp > g2, 1, 0)
    cm = jnp.where(rank < TOPK_GROUPS, ch, -jnp.inf)
    eid = grp * per + sub
    erank = jnp.zeros(cm.shape, jnp.int32)
    for e2 in range(N_EXPERTS):
        o = cm[e2 // per:e2 // per + 1, e2 % per:e2 % per + 1, :]
        erank = erank + jnp.where(o > cm, 1, 0) + jnp.where(o == cm, 1, 0) * jnp.where(eid > e2, 1, 0)
    sel = jnp.where(erank < TOP_K, scores, 0.0)
    denom = jnp.sum(jnp.sum(sel, axis=1, keepdims=True), axis=0, keepdims=True)
    return (sel / denom * ROUTED_SCALE).reshape(N_EXPERTS, tm)


def _outproj_kernel(x_ref, oa_ref, ob_ref, ga_ref, sc_ref, sh_ref, g_ref, w_ref, wr_ref, rb_ref,
                    x1_ref, h2_ref, gt_ref):
    half = oa_ref.shape[1]
    mix = (jnp.dot(oa_ref[...], w_ref[:half, :], preferred_element_type=F32)
           + jnp.dot(ob_ref[...], w_ref[half:, :], preferred_element_type=F32))
    x1 = x_ref[...] + ga_ref[...] * mix
    x1_ref[...] = x1
    h2 = _modulated_norm(x1, g_ref[...], sc_ref[...], sh_ref[...])
    h2_ref[...] = h2.astype(h2_ref.dtype)
    logits_t = lax.dot_general(wr_ref[...], h2, NT_DIMS, preferred_element_type=F32,
                               precision=lax.Precision.HIGHEST)
    gt_ref[...] = _route(logits_t, rb_ref[...])


def _outproj_call(x2d, oa, ob, ga, sc, sh, mod_map, g, w_bf, wr_t, rbias, tm):
    n, d = x2d.shape
    assert n % tm == 0
    r = sc.shape[1]
    mod_spec = pl.BlockSpec((None, r, d), mod_map)
    return pl.pallas_call(
        _outproj_kernel,
        grid=(n // tm,),
        in_specs=[pl.BlockSpec((tm, d), lambda i: (i, 0)),
                  pl.BlockSpec((tm, oa.shape[1]), lambda i: (i, 0)),
                  pl.BlockSpec((tm, ob.shape[1]), lambda i: (i, 0)),
                  mod_spec, mod_spec, mod_spec,
                  pl.BlockSpec((1, d), lambda i: (0, 0)),
                  _resident(w_bf.shape, lambda i: (0, 0)),
                  _resident(wr_t.shape, lambda i: (0, 0)),
                  pl.BlockSpec((N_EXPERTS, 1), lambda i: (0, 0))],
        out_specs=[pl.BlockSpec((tm, d), lambda i: (i, 0)),
                   pl.BlockSpec((tm, d), lambda i: (i, 0)),
                   pl.BlockSpec((N_EXPERTS, tm), lambda i: (0, i))],
        out_shape=[jax.ShapeDtypeStruct((n, d), F32),
                   jax.ShapeDtypeStruct((n, d), BF16),
                   jax.ShapeDtypeStruct((N_EXPERTS, n), F32)],
        compiler_params=_params(1),
        name="out_proj_router",
    )(x2d, oa, ob, ga, sc, sh, g.reshape(1, d), w_bf, wr_t, rbias)


def _swiglu_hidden(h, wg, wu):
    a = jnp.dot(h, wg, preferred_element_type=F32)
    u = jnp.dot(h, wu, preferred_element_type=F32)
    return a * jax.nn.sigmoid(a) * u


def _moe_kernel(h_ref, gates_ref, wg_ref, wu_ref, wd_ref, sg_ref, su_ref, sd_ref, o_ref):
    e = pl.program_id(1)
    h = h_ref[...]

    @pl.when(e == 0)
    def _():
        hid = _swiglu_hidden(h, sg_ref[...], su_ref[...])
        o_ref[...] = jnp.dot(hid.astype(BF16), sd_ref[...], preferred_element_type=F32)

    lane = lax.broadcasted_iota(jnp.int32, gates_ref.shape, 1)
    gate = jnp.sum(jnp.where(lane == e, gates_ref[...], 0.0), axis=-1, keepdims=True)
    hid = _swiglu_hidden(h, wg_ref[...], wu_ref[...]) * gate
    o_ref[...] += jnp.dot(hid.astype(BF16), wd_ref[...], preferred_element_type=F32)


def _moe_call(h2, gates, wg, wu, wd, sg, su, sd, tm):
    n, d = h2.shape
    assert n % tm == 0
    ne, _, de = wg.shape
    return pl.pallas_call(
        _moe_kernel,
        grid=(n // tm, ne),
        in_specs=[pl.BlockSpec((tm, d), lambda i, e: (i, 0)),
                  pl.BlockSpec((tm, ne), lambda i, e: (i, 0)),
                  pl.BlockSpec((None, d, de), lambda i, e: (e, 0, 0)),
                  pl.BlockSpec((None, d, de), lambda i, e: (e, 0, 0)),
                  pl.BlockSpec((None, de, d), lambda i, e: (e, 0, 0)),
                  _resident(sg.shape, lambda i, e: (0, 0)),
                  _resident(su.shape, lambda i, e: (0, 0)),
                  _resident(sd.shape, lambda i, e: (0, 0))],
        out_specs=pl.BlockSpec((tm, d), lambda i, e: (i, 0)),
        out_shape=jax.ShapeDtypeStruct((n, d), F32),
        compiler_params=_params(2),
        name="moe",
    )(h2, gates, wg, wu, wd, sg, su, sd)


def _final_kernel(x_ref, m_ref, gf_ref, g_ref, o_ref):
    x = x_ref[...] + gf_ref[...] * m_ref[...]
    o_ref[...] = x * lax.rsqrt(jnp.mean(x * x, axis=-1, keepdims=True) + EPS) * g_ref[...]


def _final_call(x1, moe, gf, mod_map, g, tm):
    n, d = x1.shape
    assert n % tm == 0
    r = gf.shape[1]
    return pl.pallas_call(
        _final_kernel,
        grid=(n // tm,),
        in_specs=[pl.BlockSpec((tm, d), lambda i: (i, 0)),
                  pl.BlockSpec((tm, d), lambda i: (i, 0)),
                  pl.BlockSpec((None, r, d), mod_map),
                  pl.BlockSpec((1, d), lambda i: (0, 0))],
        out_specs=pl.BlockSpec((tm, d), lambda i: (i, 0)),
        out_shape=jax.ShapeDtypeStruct((n, d), F32),
        compiler_params=_params(1),
        name="final_norm",
    )(x1, moe, gf, g.reshape(1, d))


_QA, _KA, _VA, _QI, _KI, _IW, _QB, _KB, _VB = range(9)
_IN_SIZES = (A_HEADS * A_HEAD_DIM, A_KV_HEADS * A_HEAD_DIM, A_KV_HEADS * A_HEAD_DIM,
             IDX_HEADS * IDX_DIM, IDX_DIM, IDX_HEADS,
             B_HEADS * 2 * B_HEAD_DIM, B_HEADS * 2 * B_HEAD_DIM, B_HEADS * 2 * B_HEAD_DIM)


def _split_w_in(w_in):
    d = w_in.shape[0]
    offs = np.concatenate([[0], np.cumsum(_IN_SIZES)])
    cols = [w_in[:, offs[k]:offs[k + 1]].astype(BF16) for k in range(9)]
    zero = lambda n: jnp.zeros((d, n), BF16)
    qi = jnp.concatenate([cols[_QI].reshape(d, IDX_HEADS, IDX_DIM),
                          jnp.zeros((d, IDX_HEADS, LANES - IDX_DIM), BF16)], axis=-1).reshape(d, IDX_HEADS * LANES)
    w_a = jnp.concatenate([cols[_QA], cols[_KA], cols[_VA], qi,
                           cols[_KI], zero(LANES - IDX_DIM), cols[_IW], zero(LANES - IDX_HEADS)], axis=1)
    w_b = jnp.concatenate([cols[_QB], cols[_KB], cols[_VB]], axis=1)
    return w_a, w_b


def _group_forward(x, mod, past, rel_bias, lam, weights, *, lam_init, qb, kb, tm, per_row_mod):
    (norm_a_g, w_a, w_b, subln_g) = weights
    b, t, d = x.shape
    n = b * t
    x2d = x.reshape(n, d)
    sh_a, sc_a = mod[0], mod[1]
    if per_row_mod:
        expand = lambda m: jnp.repeat(m, t, axis=0).reshape(1, n, d)
        mod_map = lambda i: (0, i, 0)
    else:
        expand = lambda m: m.reshape(b, 1, d)
        mod_map = lambda i: ((i * tm) // t, 0, 0)
    sc3, sh3 = expand(sc_a), expand(sh_a)

    na = A_HEADS * A_HEAD_DIM
    nk = A_KV_HEADS * A_HEAD_DIM
    nq = IDX_HEADS * LANES
    c_ka, c_va, c_qi = na, na + nk, na + 2 * nk
    c_ki, c_iw = c_qi + nq, c_qi + nq + LANES
    defs_a = [(0, na, na, A_HEAD_DIM ** -0.5), (c_ka, nk, nk, None), (c_ka, nk, nk, None),
              (c_va, nk, nk, None), (c_va, nk, nk, None), (c_qi, nq, nq, None),
              (c_ki, LANES, IDX_DIM, None), (c_ki, LANES, LANES, None), (c_iw, LANES, LANES, None)]
    dt_a = [BF16, F32, BF16, F32, BF16, BF16, F32, BF16, F32]
    qa, ka, ka_bf, va, va_bf, qi, ki, ki_bf, iw = _norm_proj_call(
        x2d, sc3, sh3, mod_map, norm_a_g, w_a, defs_a, dt_a, tm, "in_proj_a")
    nb = B_HEADS * 2 * B_HEAD_DIM
    defs_b = [(0, nb, nb, B_HEAD_DIM ** -0.5), (nb, nb, nb, None), (nb, nb, nb, None),
              (2 * nb, nb, nb, None), (2 * nb, nb, nb, None)]
    dt_b = [BF16, F32, BF16, F32, BF16]
    qbm, kbm, kb_bf, vbm, vb_bf = _norm_proj_call(
        x2d, sc3, sh3, mod_map, norm_a_g, w_b, defs_b, dt_b, tm, "in_proj_b")

    r3 = lambda a: a.reshape(b, t, a.shape[-1])
    k_a, v_a, k_i, k_b, v_b = r3(ka_bf), r3(va_bf), r3(ki_bf), r3(kb_bf), r3(vb_bf)
    qoff = 0
    if past is not None:
        pk, pv, pi, pbk, pbv = past
        qoff = pk.shape[1]
        qbr = -(-qb // LANES) * LANES
        pad = qbr - t

        def join(p, new):
            p2 = p.reshape(b, qoff, -1).astype(BF16)
            if p2.shape[-1] < new.shape[-1]:
                p2 = jnp.concatenate([p2, jnp.zeros((b, qoff, new.shape[-1] - p2.shape[-1]), BF16)], axis=-1)
            return jnp.concatenate([p2, new, jnp.zeros((b, pad, new.shape[-1]), BF16)], axis=1)

        k_a, v_a, k_i, k_b, v_b = join(pk, k_a), join(pv, v_a), join(pi, k_i), join(pbk, k_b), join(pbv, v_b)
    length = qoff + t
    topk = min(TOPK_MAX, length // 4)

    tiles = _bias_tile_call(rel_bias, qb)
    out_a = _dsa_call(rel_bias, tiles, r3(qa), r3(qi), r3(iw), k_a, v_a, k_i, qb=qb, kb=kb, qoff=qoff, topk=topk)
    out_b = _diff_call(rel_bias, tiles, lam, subln_g, r3(qbm), k_b, v_b, qb=qb, kb=kb, qoff=qoff,
                       lam_init=lam_init)
    rows = (ka.reshape(1, b, t, A_KV_HEADS, A_HEAD_DIM), va.reshape(1, b, t, A_KV_HEADS, A_HEAD_DIM),
            ki.reshape(1, b, t, IDX_DIM), kbm.reshape(1, b, t, B_HEADS, 2, B_HEAD_DIM),
            vbm.reshape(1, b, t, B_HEADS, 2 * B_HEAD_DIM))
    return x2d, out_a.reshape(n, -1), out_b.reshape(n, -1), rows, expand, mod_map


def kernel(x_prompt, x_sample, c_prompt, c_sample, cache_a_k, cache_a_v, cache_a_kidx, cache_b_k, cache_b_v,
           rel_bias, w_ada, b_ada, norm_a_g, w_in, w_out, diff_lam, subln_g, norm_f_g, w_router, router_bias,
           w_gate, w_up, w_down, ws_gate, ws_up, ws_down, final_g):
    assert w_ada.shape[0] == 1, "single-layer model"
    d = x_prompt.shape[-1]
    bp, tp, _ = x_prompt.shape
    bs, ts, _ = x_sample.shape
    lam_init = 0.8 - 0.6 * math.exp(-0.3 * 0)

    c_all = jnp.concatenate([c_prompt, c_sample], axis=0)
    rows = -(-c_all.shape[0] // 8) * 8
    c_pad = jnp.concatenate([c_all, jnp.zeros((rows - c_all.shape[0], d), F32)], axis=0)
    mod_all = _ada_call(c_pad, w_ada[0], b_ada[0])
    mod_p = [mod_all[:bp, k * d:(k + 1) * d] for k in range(6)]
    mod_s = [mod_all[bp:bp + bs, k * d:(k + 1) * d] for k in range(6)]

    w_a, w_b = _split_w_in(w_in[0])
    lam = _lam_call(diff_lam[0], lam_init)
    weights = (norm_a_g[0], w_a, w_b, subln_g[0])
    past = (cache_a_k[0], cache_a_v[0], cache_a_kidx[0], cache_b_k[0], cache_b_v[0])

    tm_p = 256
    tm_s = bs * ts
    xp2, oa_p, ob_p, rows_p, exp_p, map_p = _group_forward(
        x_prompt, mod_p, None, rel_bias, lam, weights, lam_init=lam_init, qb=128, kb=512, tm=tm_p,
        per_row_mod=False)
    xs2, oa_s, ob_s, rows_s, exp_s, map_s = _group_forward(
        x_sample, mod_s, past, rel_bias, lam, weights, lam_init=lam_init, qb=ts, kb=512, tm=tm_s,
        per_row_mod=True)

    w_out_bf = w_out[0].astype(BF16)
    wr_t = w_router[0].T
    rb = router_bias[0].reshape(N_EXPERTS, 1)
    x1_p, h2_p, gt_p = _outproj_call(xp2, oa_p, ob_p, exp_p(mod_p[2]), exp_p(mod_p[4]), exp_p(mod_p[3]),
                                     map_p, norm_f_g[0], w_out_bf, wr_t, rb, tm_p)
    x1_s, h2_s, gt_s = _outproj_call(xs2, oa_s, ob_s, exp_s(mod_s[2]), exp_s(mod_s[4]), exp_s(mod_s[3]),
                                     map_s, norm_f_g[0], w_out_bf, wr_t, rb, tm_s)

    h2 = jnp.concatenate([h2_p, h2_s], axis=0)
    gates = jnp.concatenate([gt_p, gt_s], axis=1).T
    n_all = h2.shape[0]
    tm_moe = next(c for c in (640, 512, 256, 128, 64, 32, 16, 8) if n_all % c == 0)
    moe = _moe_call(h2, gates, w_gate[0].astype(BF16), w_up[0].astype(BF16), w_down[0].astype(BF16),
                    ws_gate[0].astype(BF16), ws_up[0].astype(BF16), ws_down[0].astype(BF16), tm_moe)
    n_p = bp * tp
    y_p = _final_call(x1_p, moe[:n_p], exp_p(mod_p[5]), map_p, final_g, tm_p)
    y_s = _final_call(x1_s, moe[n_p:], exp_s(mod_s[5]), map_s, final_g, tm_s)
    return (y_p.reshape(bp, tp, d), y_s.reshape(bs, ts, d)) + rows_p + rows_s
```

```python
import functools
import math

import numpy as np
import jax
import jax.numpy as jnp
from jax import lax
from jax.experimental import pallas as pl
from jax.experimental.pallas import tpu as pltpu

D_MODEL = 2048
CHUNK = 64
A_HEADS = 8
A_KV_HEADS = 2
A_REP = A_HEADS // A_KV_HEADS
A_HEAD_DIM = 128
IDX_HEADS = 16
IDX_DIM = 64
TOPK_MAX = 256
B_HEADS = 4
B_HEAD_DIM = 128
N_REL_BUCKETS = 32
REL_MAX_DIST = 128
N_ATTN_HEADS = A_HEADS + B_HEADS
N_EXPERTS = 64
TOP_K = 8
N_GROUPS = 8
TOPK_GROUPS = 4
EXPERT_DIM = 512
SHARED_DIM = 512
ROUTED_SCALE = 2.5
EPS = 1e-6

LANES = 128
INT_MIN = -(2 ** 31)
NEG_BIG = -1e30
VMEM_LIMIT = 52 * 1024 * 1024

F32 = jnp.float32
BF16 = jnp.bfloat16
NT_DIMS = (((1,), (1,)), ((), ()))


def _params(n_axes):
    return pltpu.CompilerParams(dimension_semantics=("arbitrary",) * n_axes,
                                vmem_limit_bytes=VMEM_LIMIT)


def _resident(shape, index_map):
    return pl.BlockSpec(shape, index_map, pipeline_mode=pl.Buffered(1))


def _bucket_thresholds():
    nb = N_REL_BUCKETS // 2
    max_exact = nb // 2
    n = np.arange(max_exact, 4 * REL_MAX_DIST, dtype=np.int32)
    nf = n.astype(np.float32)
    large = max_exact + (np.log(nf / np.float32(max_exact)) / np.float32(math.log(REL_MAX_DIST / max_exact))
                         * np.float32(nb - max_exact)).astype(np.int32)
    large = np.minimum(large, nb - 1)
    assert np.all(np.diff(large) >= 0)
    ths = [int(n[np.argmax(large >= b)]) for b in range(max_exact + 1, nb)]
    return max_exact, nb, ths


_MAX_EXACT, _NB, _BUCKET_THS = _bucket_thresholds()
assert _BUCKET_THS[-1] <= LANES + 1
FAR_BUCKET = _NB - 1


def _ada_kernel(c_ref, w_ref, b_ref, o_ref):
    c = c_ref[...]
    s = c * jax.nn.sigmoid(c)
    o_ref[...] = jnp.dot(s, w_ref[...], preferred_element_type=F32,
                         precision=lax.Precision.HIGHEST) + b_ref[...]


def _ada_call(c, w_ada, b_ada):
    rows, d = c.shape
    n = w_ada.shape[1]
    tn = 1536
    assert n % tn == 0
    return pl.pallas_call(
        _ada_kernel,
        grid=(n // tn,),
        in_specs=[pl.BlockSpec((rows, d), lambda j: (0, 0)),
                  pl.BlockSpec((d, tn), lambda j: (0, j)),
                  pl.BlockSpec((1, tn), lambda j: (0, j))],
        out_specs=pl.BlockSpec((rows, tn), lambda j: (0, j)),
        out_shape=jax.ShapeDtypeStruct((rows, n), F32),
        compiler_params=_params(1),
        name="ada_mod",
    )(c, w_ada, b_ada.reshape(1, n))


def _modulated_norm(x, g, sc, sh):
    y = x * lax.rsqrt(jnp.mean(x * x, axis=-1, keepdims=True) + EPS) * g
    return y * (1.0 + sc) + sh


def _norm_proj_kernel(x_ref, sc_ref, sh_ref, g_ref, w_ref, *out_refs, defs):
    h = _modulated_norm(x_ref[...], g_ref[...], sc_ref[...], sh_ref[...]).astype(BF16)
    done = {}
    for (start, width, store_w, mult), o_ref in zip(defs, out_refs):
        if (start, width) not in done:
            done[(start, width)] = jnp.dot(h, w_ref[:, start:start + width], preferred_element_type=F32)
        r = done[(start, width)][:, :store_w]
        o_ref[...] = (r if mult is None else r * mult).astype(o_ref.dtype)


def _norm_proj_call(x2d, sc, sh, mod_map, g, w_bf, defs, dtypes, tm, name):
    n, d = x2d.shape
    assert n % tm == 0
    r = sc.shape[1]
    out_shapes = [jax.ShapeDtypeStruct((n, d_[2]), dt) for d_, dt in zip(defs, dtypes)]
    out_specs = [pl.BlockSpec((tm, d_[2]), lambda i: (i, 0)) for d_ in defs]
    return pl.pallas_call(
        functools.partial(_norm_proj_kernel, defs=tuple(defs)),
        grid=(n // tm,),
        in_specs=[pl.BlockSpec((tm, d), lambda i: (i, 0)),
                  pl.BlockSpec((None, r, d), mod_map),
                  pl.BlockSpec((None, r, d), mod_map),
                  pl.BlockSpec((1, d), lambda i: (0, 0)),
                  _resident(w_bf.shape, lambda i: (0, 0))],
        out_specs=out_specs,
        out_shape=out_shapes,
        compiler_params=_params(1),
        name=name,
    )(x2d, sc, sh, g.reshape(1, d), w_bf)


def _bias_tile_kernel(tab_ref, o_ref, *, qb):
    _, rows, width = o_ref.shape
    r = lax.broadcasted_iota(jnp.int32, (rows, width), 0)
    koff = lax.broadcasted_iota(jnp.int32, (rows, width), 1) - LANES
    rel = koff - r
    n = jnp.abs(rel)
    large = jnp.full((rows, width), _MAX_EXACT, jnp.int32)
    for th in _BUCKET_THS:
        large = large + jnp.where(n >= th, 1, 0)
    bucket = jnp.where(rel > 0, _NB, 0) + jnp.where(n < _MAX_EXACT, n, large)
    visible = ((koff >> 6) <= (r >> 6)) & (koff < qb)
    for h in range(N_ATTN_HEADS):
        b = jnp.zeros((rows, width), F32)
        for bk in range(N_REL_BUCKETS):
            b = jnp.where(bucket == bk, tab_ref[bk, h], b)
        o_ref[h] = jnp.where(visible, b, NEG_BIG)


def _bias_tile_call(rel_bias, qb):
    qbr = -(-qb // LANES) * LANES
    return pl.pallas_call(
        functools.partial(_bias_tile_kernel, qb=qb),
        in_specs=[pl.BlockSpec(memory_space=pltpu.SMEM)],
        out_specs=pl.BlockSpec(memory_space=pltpu.VMEM),
        out_shape=jax.ShapeDtypeStruct((N_ATTN_HEADS, qb, LANES + qbr), F32),
        name="bias_tiles",
    )(rel_bias)


def _sweep_far(q0, qb, kb, fn):
    far_end = jnp.maximum(q0 - LANES, 0)
    n_full = far_end // kb
    sub = kb // LANES

    def body(j, carry):
        fn(pl.multiple_of(j * kb, kb), j * sub, None)
        return carry

    lax.fori_loop(0, n_full, body, 0)

    @pl.when(far_end > n_full * kb)
    def _():
        start = pl.multiple_of(n_full * kb, kb)
        col = start + lax.broadcasted_iota(jnp.int32, (qb, LANES), 1)
        fn(start, n_full * sub, [jnp.where(col + c * LANES < far_end, 0.0, NEG_BIG) for c in range(sub)])


def _dsa_kernel(tab_ref, q_ref, qi_ref, iw_ref, k_ref, v_ref, ki_ref, tile_ref, o_ref,
                keys_scr, mfar_scr, mnear_scr, l_scr, acc_scr, *, qb, kb, qoff, topk):
    i = pl.program_id(1)
    q0 = qoff + i * qb
    qbr = tile_ref.shape[2] - LANES
    sub = kb // LANES
    w = iw_ref[:, :IDX_HEADS] * (IDX_HEADS ** -0.5 * IDX_DIM ** -0.5)

    def score_keys(start, width):
        kib = ki_ref[pl.ds(start, width), :]
        acc = jnp.zeros((qb, width), F32)
        for h in range(IDX_HEADS):
            d = lax.dot_general(qi_ref[:, h * LANES:(h + 1) * LANES], kib, NT_DIMS,
                                preferred_element_type=F32)
            acc = acc + w[:, h:h + 1] * jnp.maximum(d, 0.0)
        bits = pltpu.bitcast(acc, jnp.int32)
        return bits ^ ((bits >> 31) & 0x7FFFFFFF)

    n_score = (q0 + kb - 1) // kb

    def score_body(j, carry):
        key = score_keys(pl.multiple_of(j * kb, kb), kb)
        for c in range(sub):
            keys_scr[j * sub + c] = key[:, c * LANES:(c + 1) * LANES]
        return carry

    lax.fori_loop(0, n_score, score_body, 0)

    cur_blk = q0 // LANES
    key = score_keys(pl.multiple_of(q0, LANES), qbr)
    key = jnp.where(tile_ref[0, :, LANES:] > 0.5 * NEG_BIG, key, INT_MIN)
    for c in range(qbr // LANES):
        keys_scr[cur_blk + c] = key[:, c * LANES:(c + 1) * LANES]
    n_blk = cur_blk + qbr // LANES

    def count_ge(t):
        tb = jnp.broadcast_to(t, (qb, LANES))
        n4 = n_blk // 4

        def body4(j, cnt):
            for u in range(4):
                cnt = cnt + jnp.where(keys_scr[4 * j + u] >= tb, 1, 0)
            return cnt

        def body1(j, cnt):
            return cnt + jnp.where(keys_scr[4 * n4 + j] >= tb, 1, 0)

        cnt = lax.fori_loop(0, n4, body4, jnp.zeros((qb, LANES), jnp.int32))
        cnt = lax.fori_loop(0, n_blk - 4 * n4, body1, cnt)
        return jnp.sum(cnt, axis=-1, keepdims=True)

    def bit_body(it, t):
        cand = t + (jnp.int32(1) << (31 - it))
        return jnp.where(count_ge(cand) >= topk, cand, t)

    thr = lax.fori_loop(0, 32, bit_body, jnp.full((qb, 1), INT_MIN, jnp.int32))
    thr = jnp.maximum(thr, INT_MIN + 1)
    thr_b = jnp.broadcast_to(thr, (qb, LANES))

    hd = A_HEAD_DIM

    def select_mask(blk0, n):
        return [jnp.where(keys_scr[blk0 + c] >= thr_b, 0.0, NEG_BIG) for c in range(n)]

    def logits(h, kblk):
        g = h // A_REP
        s = lax.dot_general(q_ref[:, h * hd:(h + 1) * hd], kblk[:, g * hd:(g + 1) * hd], NT_DIMS,
                            preferred_element_type=F32)
        return [s[:, c * LANES:(c + 1) * LANES] for c in range(s.shape[1] // LANES)]

    def tile_cols(h, col0, n):
        return [tile_ref[h, :, col0 + c * LANES:col0 + (c + 1) * LANES] for c in range(n)]

    mfar_scr[...] = jnp.full(mfar_scr.shape, NEG_BIG, F32)
    mnear_scr[...] = jnp.full(mnear_scr.shape, NEG_BIG, F32)

    def far_mask(blk0, colmask):
        am = select_mask(blk0, sub)
        return am if colmask is None else [a + c for a, c in zip(am, colmask)]

    def max_far(start, blk0, colmask):
        kblk = k_ref[pl.ds(start, kb), :]
        am = far_mask(blk0, colmask)
        for h in range(A_HEADS):
            mfar_scr[h] = functools.reduce(
                jnp.maximum, [s + a for s, a in zip(logits(h, kblk), am)], mfar_scr[h])

    def max_near(start, width, blk0, col0):
        kblk = k_ref[pl.ds(start, width), :]
        am = select_mask(blk0, width // LANES)
        for h in range(A_HEADS):
            mnear_scr[h] = functools.reduce(
                jnp.maximum,
                [s + a + b for s, a, b in zip(logits(h, kblk), am, tile_cols(h, col0, width // LANES))],
                mnear_scr[h])

    prev_start = pl.multiple_of(q0 - LANES, LANES)
    cur_start = pl.multiple_of(q0, LANES)
    _sweep_far(q0, qb, kb, max_far)
    pl.when(q0 > 0)(lambda: max_near(prev_start, LANES, cur_blk - 1, 0))
    max_near(cur_start, qbr, cur_blk, LANES)

    for h in range(A_HEADS):
        fb = tab_ref[FAR_BUCKET, h]
        m = jnp.maximum(jnp.max(mfar_scr[h], axis=-1, keepdims=True) + fb,
                        jnp.max(mnear_scr[h], axis=-1, keepdims=True))
        mfar_scr[h] = jnp.broadcast_to(fb - m, (qb, LANES))
        mnear_scr[h] = jnp.broadcast_to(-m, (qb, LANES))
    l_scr[...] = jnp.zeros(l_scr.shape, F32)
    acc_scr[...] = jnp.zeros(acc_scr.shape, F32)

    def accumulate(h, p, vblk):
        g = h // A_REP
        l_scr[h] += functools.reduce(jnp.add, p)
        acc_scr[h] += jnp.dot(jnp.concatenate(p, axis=1).astype(BF16), vblk[:, g * hd:(g + 1) * hd],
                              preferred_element_type=F32)

    def acc_far(start, blk0, colmask):
        kblk = k_ref[pl.ds(start, kb), :]
        vblk = v_ref[pl.ds(start, kb), :]
        am = far_mask(blk0, colmask)
        for h in range(A_HEADS):
            shift = mfar_scr[h]
            accumulate(h, [jnp.exp(s + a + shift) for s, a in zip(logits(h, kblk), am)], vblk)

    def acc_near(start, width, blk0, col0):
        kblk = k_ref[pl.ds(start, width), :]
        vblk = v_ref[pl.ds(start, width), :]
        am = select_mask(blk0, width // LANES)
        for h in range(A_HEADS):
            shift = mnear_scr[h]
            accumulate(h, [jnp.exp(s + a + b + shift) for s, a, b in
                           zip(logits(h, kblk), am, tile_cols(h, col0, width // LANES))], vblk)

    _sweep_far(q0, qb, kb, acc_far)
    pl.when(q0 > 0)(lambda: acc_near(prev_start, LANES, cur_blk - 1, 0))
    acc_near(cur_start, qbr, cur_blk, LANES)

    for h in range(A_HEADS):
        o_ref[:, h * hd:(h + 1) * hd] = (
            acc_scr[h] / jnp.sum(l_scr[h], axis=-1, keepdims=True)).astype(o_ref.dtype)


def _dsa_call(rel_bias, tiles, q, qi, iw, k, v, ki, *, qb, kb, qoff, topk):
    b, t, _ = q.shape
    lp = k.shape[1]
    qbr = tiles.shape[2] - LANES
    q0_max = qoff + t - qb
    key_end = max(-(-q0_max // kb) * kb, q0_max + qbr)
    assert t % qb == 0 and qoff % LANES == 0 and key_end <= lp
    n_slabs = key_end // LANES
    return pl.pallas_call(
        functools.partial(_dsa_kernel, qb=qb, kb=kb, qoff=qoff, topk=topk),
        grid=(b, t // qb),
        in_specs=[pl.BlockSpec(memory_space=pltpu.SMEM),
                  pl.BlockSpec((None, qb, q.shape[2]), lambda bi, i: (bi, i, 0)),
                  pl.BlockSpec((None, qb, qi.shape[2]), lambda bi, i: (bi, i, 0)),
                  pl.BlockSpec((None, qb, iw.shape[2]), lambda bi, i: (bi, i, 0)),
                  pl.BlockSpec((None, lp, k.shape[2]), lambda bi, i: (bi, 0, 0)),
                  pl.BlockSpec((None, lp, v.shape[2]), lambda bi, i: (bi, 0, 0)),
                  pl.BlockSpec((None, lp, ki.shape[2]), lambda bi, i: (bi, 0, 0)),
                  _resident(tiles.shape, lambda bi, i: (0, 0, 0))],
        out_specs=pl.BlockSpec((None, qb, A_HEADS * A_HEAD_DIM), lambda bi, i: (bi, i, 0)),
        out_shape=jax.ShapeDtypeStruct((b, t, A_HEADS * A_HEAD_DIM), BF16),
        scratch_shapes=[pltpu.VMEM((n_slabs, qb, LANES), jnp.int32),
                        pltpu.VMEM((A_HEADS, qb, LANES), F32),
                        pltpu.VMEM((A_HEADS, qb, LANES), F32),
                        pltpu.VMEM((A_HEADS, qb, LANES), F32),
                        pltpu.VMEM((A_HEADS, qb, A_HEAD_DIM), F32)],
        compiler_params=_params(2),
        name="mixer_a",
    )(rel_bias, q, qi, iw, k, v, ki, tiles)


def _diff_kernel(tab_ref, lam_ref, sg_ref, q_ref, k_ref, v_ref, tile_ref, o_ref,
                 mfar_scr, mnear_scr, l_scr, acc_scr, *, qb, kb, qoff, lam_init):
    head = pl.program_id(1)
    i = pl.program_id(2)
    q0 = qoff + i * qb
    qbr = tile_ref.shape[2] - LANES
    hd = B_HEAD_DIM
    far_bias = tab_ref[FAR_BUCKET, A_HEADS + head]

    def logits(c, kblk):
        s = lax.dot_general(q_ref[:, c * hd:(c + 1) * hd], kblk[:, c * hd:(c + 1) * hd], NT_DIMS,
                            preferred_element_type=F32)
        return [s[:, u * LANES:(u + 1) * LANES] for u in range(s.shape[1] // LANES)]

    def tile_cols(col0, n):
        return [tile_ref[0, :, col0 + u * LANES:col0 + (u + 1) * LANES] for u in range(n)]

    mfar_scr[...] = jnp.full(mfar_scr.shape, NEG_BIG, F32)
    mnear_scr[...] = jnp.full(mnear_scr.shape, NEG_BIG, F32)

    def max_far(start, blk0, colmask):
        kblk = k_ref[pl.ds(start, kb), :]
        for c in range(2):
            s = logits(c, kblk)
            if colmask is not None:
                s = [x + m for x, m in zip(s, colmask)]
            mfar_scr[c] = functools.reduce(jnp.maximum, s, mfar_scr[c])

    def max_near(start, width, col0):
        kblk = k_ref[pl.ds(start, width), :]
        for c in range(2):
            mnear_scr[c] = functools.reduce(
                jnp.maximum, [s + b for s, b in zip(logits(c, kblk), tile_cols(col0, width // LANES))],
                mnear_scr[c])

    prev_start = pl.multiple_of(q0 - LANES, LANES)
    cur_start = pl.multiple_of(q0, LANES)
    _sweep_far(q0, qb, kb, max_far)
    pl.when(q0 > 0)(lambda: max_near(prev_start, LANES, 0))
    max_near(cur_start, qbr, LANES)

    for c in range(2):
        m = jnp.maximum(jnp.max(mfar_scr[c], axis=-1, keepdims=True) + far_bias,
                        jnp.max(mnear_scr[c], axis=-1, keepdims=True))
        mfar_scr[c] = jnp.broadcast_to(far_bias - m, (qb, LANES))
        mnear_scr[c] = jnp.broadcast_to(-m, (qb, LANES))
    l_scr[...] = jnp.zeros(l_scr.shape, F32)
    acc_scr[...] = jnp.zeros(acc_scr.shape, F32)

    def accumulate(c, p, vblk):
        l_scr[c] += functools.reduce(jnp.add, p)
        acc_scr[c] += jnp.dot(jnp.concatenate(p, axis=1).astype(BF16), vblk, preferred_element_type=F32)

    def acc_far(start, blk0, colmask):
        kblk = k_ref[pl.ds(start, kb), :]
        vblk = v_ref[pl.ds(start, kb), :]
        for c in range(2):
            shift = mfar_scr[c]
            shifts = [shift] * (kb // LANES) if colmask is None else [shift + m for m in colmask]
            accumulate(c, [jnp.exp(s + sh) for s, sh in zip(logits(c, kblk), shifts)], vblk)

    def acc_near(start, width, col0):
        kblk = k_ref[pl.ds(start, width), :]
        vblk = v_ref[pl.ds(start, width), :]
        for c in range(2):
            shift = mnear_scr[c]
            accumulate(c, [jnp.exp(s + b + shift) for s, b in
                           zip(logits(c, kblk), tile_cols(col0, width // LANES))], vblk)

    _sweep_far(q0, qb, kb, acc_far)
    pl.when(q0 > 0)(lambda: acc_near(prev_start, LANES, 0))
    acc_near(cur_start, qbr, LANES)

    lam = lam_ref[0, 0]
    o = (acc_scr[0] / jnp.sum(l_scr[0], axis=-1, keepdims=True)
         - lam * (acc_scr[1] / jnp.sum(l_scr[1], axis=-1, keepdims=True)))
    o = o * lax.rsqrt(jnp.mean(o * o, axis=-1, keepdims=True) + EPS) * sg_ref[...]
    o_ref[...] = (o * (1.0 - lam_init)).astype(o_ref.dtype)


def _diff_call(rel_bias, tiles, lam, subln_g, q, k, v, *, qb, kb, qoff, lam_init):
    b, t, _ = q.shape
    lp = k.shape[1]
    hw = 2 * B_HEAD_DIM
    qbr = tiles.shape[2] - LANES
    q0_max = qoff + t - qb
    key_end = max(-(-max(q0_max - LANES, 0) // kb) * kb, q0_max + qbr)
    assert t % qb == 0 and qoff % LANES == 0 and key_end <= lp
    return pl.pallas_call(
        functools.partial(_diff_kernel, qb=qb, kb=kb, qoff=qoff, lam_init=lam_init),
        grid=(b, B_HEADS, t // qb),
        in_specs=[pl.BlockSpec(memory_space=pltpu.SMEM),
                  pl.BlockSpec(memory_space=pltpu.SMEM),
                  pl.BlockSpec((1, hw), lambda bi, h, i: (0, 0)),
                  pl.BlockSpec((None, qb, hw), lambda bi, h, i: (bi, i, h)),
                  pl.BlockSpec((None, lp, hw), lambda bi, h, i: (bi, 0, h)),
                  pl.BlockSpec((None, lp, hw), lambda bi, h, i: (bi, 0, h)),
                  pl.BlockSpec((1, qb, tiles.shape[2]), lambda bi, h, i: (A_HEADS + h, 0, 0))],
        out_specs=pl.BlockSpec((None, qb, hw), lambda bi, h, i: (bi, i, h)),
        out_shape=jax.ShapeDtypeStruct((b, t, B_HEADS * hw), BF16),
        scratch_shapes=[pltpu.VMEM((2, qb, LANES), F32),
                        pltpu.VMEM((2, qb, LANES), F32),
                        pltpu.VMEM((2, qb, LANES), F32),
                        pltpu.VMEM((2, qb, hw), F32)],
        compiler_params=_params(3),
        name="mixer_b",
    )(rel_bias, lam, subln_g.reshape(1, hw), q, k, v, tiles)


def _lam_kernel(dl_ref, o_ref, *, lam_init):
    dl = dl_ref[...]
    s1 = jnp.sum(dl[0:1] * dl[1:2], axis=-1, keepdims=True)
    s2 = jnp.sum(dl[2:3] * dl[3:4], axis=-1, keepdims=True)
    o_ref[...] = jnp.exp(s1) - jnp.exp(s2) + lam_init


def _lam_call(diff_lam, lam_init):
    return pl.pallas_call(
        functools.partial(_lam_kernel, lam_init=lam_init),
        out_shape=jax.ShapeDtypeStruct((1, 1), F32),
        name="diff_lambda",
    )(diff_lam)


def _route(logits_t, rbias):
    tm = logits_t.shape[1]
    per = N_EXPERTS // N_GROUPS
    scores = jax.nn.sigmoid(logits_t).reshape(N_GROUPS, per, tm)
    ch = scores + rbias.reshape(N_GROUPS, per, 1)
    sub = lax.broadcasted_iota(jnp.int32, ch.shape, 1)
    grp = lax.broadcasted_iota(jnp.int32, ch.shape, 0)
    m1 = jnp.max(ch, axis=1, keepdims=True)
    i1 = jnp.min(jnp.where(ch == m1, sub, per), axis=1, keepdims=True)
    m2 = jnp.max(jnp.where(sub == i1, -jnp.inf, ch), axis=1, keepdims=True)
    gs = jnp.broadcast_to(m1 + m2, ch.shape)
    rank = jnp.zeros(ch.shape, jnp.int32)
    for g2 in range(N_GROUPS):
        o = gs[g2:g2 + 1]
        rank = rank + jnp.where(o > gs, 1, 0) + jnp.where(o == gs, 1, 0) * jnp.where(grp > g2, 1, 0)
    cm = jnp.where(rank < TOPK_GROUPS, ch, -jnp.inf)
    eid = grp * per + sub
    erank = jnp.zeros(cm.shape, jnp.int32)
    for e2 in range(N_EXPERTS):
        o = cm[e2 // per:e2 // per + 1, e2 % per:e2 % per + 1, :]
        erank = erank + jnp.where(o > cm, 1, 0) + jnp.where(o == cm, 1, 0) * jnp.where(eid > e2, 1, 0)
    sel = jnp.where(erank < TOP_K, scores, 0.0)
    denom = jnp.sum(jnp.sum(sel, axis=1, keepdims=True), axis=0, keepdims=True)
    return (sel / denom * ROUTED_SCALE).reshape(N_EXPERTS, tm)


def _outproj_kernel(x_ref, oa_ref, ob_ref, ga_ref, sc_ref, sh_ref, g_ref, w_ref, wr_ref, rb_ref,
                    x1_ref, h2_ref, gt_ref):
    half = oa_ref.shape[1]
    mix = (jnp.dot(oa_ref[...], w_ref[:half, :], preferred_element_type=F32)
           + jnp.dot(ob_ref[...], w_ref[half:, :], preferred_element_type=F32))
    x1 = x_ref[...] + ga_ref[...] * mix
    x1_ref[...] = x1
    h2 = _modulated_norm(x1, g_ref[...], sc_ref[...], sh_ref[...])
    h2_ref[...] = h2.astype(h2_ref.dtype)
    logits_t = lax.dot_general(wr_ref[...], h2, NT_DIMS, preferred_element_type=F32,
                               precision=lax.Precision.HIGHEST)
    gt_ref[...] = _route(logits_t, rb_ref[...])


def _outproj_call(x2d, oa, ob, ga, sc, sh, mod_map, g, w_bf, wr_t, rbias, tm):
    n, d = x2d.shape
    assert n % tm == 0
    r = sc.shape[1]
    mod_spec = pl.BlockSpec((None, r, d), mod_map)
    return pl.pallas_call(
        _outproj_kernel,
        grid=(n // tm,),
        in_specs=[pl.BlockSpec((tm, d), lambda i: (i, 0)),
                  pl.BlockSpec((tm, oa.shape[1]), lambda i: (i, 0)),
                  pl.BlockSpec((tm, ob.shape[1]), lambda i: (i, 0)),
                  mod_spec, mod_spec, mod_spec,
                  pl.BlockSpec((1, d), lambda i: (0, 0)),
                  _resident(w_bf.shape, lambda i: (0, 0)),
                  _resident(wr_t.shape, lambda i: (0, 0)),
                  pl.BlockSpec((N_EXPERTS, 1), lambda i: (0, 0))],
        out_specs=[pl.BlockSpec((tm, d), lambda i: (i, 0)),
                   pl.BlockSpec((tm, d), lambda i: (i, 0)),
                   pl.BlockSpec((N_EXPERTS, tm), lambda i: (0, i))],
        out_shape=[jax.ShapeDtypeStruct((n, d), F32),
                   jax.ShapeDtypeStruct((n, d), BF16),
                   jax.ShapeDtypeStruct((N_EXPERTS, n), F32)],
        compiler_params=_params(1),
        name="out_proj_router",
    )(x2d, oa, ob, ga, sc, sh, g.reshape(1, d), w_bf, wr_t, rbias)


def _swiglu_hidden(h, wg, wu):
    a = jnp.dot(h, wg, preferred_element_type=F32)
    u = jnp.dot(h, wu, preferred_element_type=F32)
    return a * jax.nn.sigmoid(a) * u


def _moe_kernel(h_ref, gates_ref, wg_ref, wu_ref, wd_ref, sg_ref, su_ref, sd_ref, o_ref):
    e = pl.program_id(1)
    h = h_ref[...]

    @pl.when(e == 0)
    def _():
        hid = _swiglu_hidden(h, sg_ref[...], su_ref[...])
        o_ref[...] = jnp.dot(hid.astype(BF16), sd_ref[...], preferred_element_type=F32)

    lane = lax.broadcasted_iota(jnp.int32, gates_ref.shape, 1)
    gate = jnp.sum(jnp.where(lane == e, gates_ref[...], 0.0), axis=-1, keepdims=True)
    hid = _swiglu_hidden(h, wg_ref[...], wu_ref[...]) * gate
    o_ref[...] += jnp.dot(hid.astype(BF16), wd_ref[...], preferred_element_type=F32)


def _moe_call(h2, gates, wg, wu, wd, sg, su, sd, tm):
    n, d = h2.shape
    assert n % tm == 0
    ne, _, de = wg.shape
    return pl.pallas_call(
        _moe_kernel,
        grid=(n // tm, ne),
        in_specs=[pl.BlockSpec((tm, d), lambda i, e: (i, 0)),
                  pl.BlockSpec((tm, ne), lambda i, e: (i, 0)),
                  pl.BlockSpec((None, d, de), lambda i, e: (e, 0, 0)),
                  pl.BlockSpec((None, d, de), lambda i, e: (e, 0, 0)),
                  pl.BlockSpec((None, de, d), lambda i, e: (e, 0, 0)),
                  _resident(sg.shape, lambda i, e: (0, 0)),
                  _resident(su.shape, lambda i, e: (0, 0)),
                  _resident(sd.shape, lambda i, e: (0, 0))],
        out_specs=pl.BlockSpec((tm, d), lambda i, e: (i, 0)),
        out_shape=jax.ShapeDtypeStruct((n, d), F32),
        compiler_params=_params(2),
        name="moe",
    )(h2, gates, wg, wu, wd, sg, su, sd)


def _final_kernel(x_ref, m_ref, gf_ref, g_ref, o_ref):
    x = x_ref[...] + gf_ref[...] * m_ref[...]
    o_ref[...] = x * lax.rsqrt(jnp.mean(x * x, axis=-1, keepdims=True) + EPS) * g_ref[...]


def _final_call(x1, moe, gf, mod_map, g, tm):
    n, d = x1.shape
    assert n % tm == 0
    r = gf.shape[1]
    return pl.pallas_call(
        _final_kernel,
        grid=(n // tm,),
        in_specs=[pl.BlockSpec((tm, d), lambda i: (i, 0)),
                  pl.BlockSpec((tm, d), lambda i: (i, 0)),
                  pl.BlockSpec((None, r, d), mod_map),
                  pl.BlockSpec((1, d), lambda i: (0, 0))],
        out_specs=pl.BlockSpec((tm, d), lambda i: (i, 0)),
        out_shape=jax.ShapeDtypeStruct((n, d), F32),
        compiler_params=_params(1),
        name="final_norm",
    )(x1, moe, gf, g.reshape(1, d))


_QA, _KA, _VA, _QI, _KI, _IW, _QB, _KB, _VB = range(9)
_IN_SIZES = (A_HEADS * A_HEAD_DIM, A_KV_HEADS * A_HEAD_DIM, A_KV_HEADS * A_HEAD_DIM,
             IDX_HEADS * IDX_DIM, IDX_DIM, IDX_HEADS,
             B_HEADS * 2 * B_HEAD_DIM, B_HEADS * 2 * B_HEAD_DIM, B_HEADS * 2 * B_HEAD_DIM)


def _split_w_in(w_in):
    d = w_in.shape[0]
    offs = np.concatenate([[0], np.cumsum(_IN_SIZES)])
    cols = [w_in[:, offs[k]:offs[k + 1]].astype(BF16) for k in range(9)]
    zero = lambda n: jnp.zeros((d, n), BF16)
    qi = jnp.concatenate([cols[_QI].reshape(d, IDX_HEADS, IDX_DIM),
                          jnp.zeros((d, IDX_HEADS, LANES - IDX_DIM), BF16)], axis=-1).reshape(d, IDX_HEADS * LANES)
    w_a = jnp.concatenate([cols[_QA], cols[_KA], cols[_VA], qi,
                           cols[_KI], zero(LANES - IDX_DIM), cols[_IW], zero(LANES - IDX_HEADS)], axis=1)
    w_b = jnp.concatenate([cols[_QB], cols[_KB], cols[_VB]], axis=1)
    return w_a, w_b


def _group_forward(x, mod, past, rel_bias, lam, weights, *, lam_init, qb, kb, tm, per_row_mod):
    (norm_a_g, w_a, w_b, subln_g) = weights
    b, t, d = x.shape
    n = b * t
    x2d = x.reshape(n, d)
    sh_a, sc_a = mod[0], mod[1]
    if per_row_mod:
        expand = lambda m: jnp.repeat(m, t, axis=0).reshape(1, n, d)
        mod_map = lambda i: (0, i, 0)
    else:
        expand = lambda m: m.reshape(b, 1, d)
        mod_map = lambda i: ((i * tm) // t, 0, 0)
    sc3, sh3 = expand(sc_a), expand(sh_a)

    na = A_HEADS * A_HEAD_DIM
    nk = A_KV_HEADS * A_HEAD_DIM
    nq = IDX_HEADS * LANES
    c_ka, c_va, c_qi = na, na + nk, na + 2 * nk
    c_ki, c_iw = c_qi + nq, c_qi + nq + LANES
    defs_a = [(0, na, na, A_HEAD_DIM ** -0.5), (c_ka, nk, nk, None), (c_ka, nk, nk, None),
              (c_va, nk, nk, None), (c_va, nk, nk, None), (c_qi, nq, nq, None),
              (c_ki, LANES, IDX_DIM, None), (c_ki, LANES, LANES, None), (c_iw, LANES, LANES, None)]
    dt_a = [BF16, F32, BF16, F32, BF16, BF16, F32, BF16, F32]
    qa, ka, ka_bf, va, va_bf, qi, ki, ki_bf, iw = _norm_proj_call(
        x2d, sc3, sh3, mod_map, norm_a_g, w_a, defs_a, dt_a, tm, "in_proj_a")
    nb = B_HEADS * 2 * B_HEAD_DIM
    defs_b = [(0, nb, nb, B_HEAD_DIM ** -0.5), (nb, nb, nb, None), (nb, nb, nb, None),
              (2 * nb, nb, nb, None), (2 * nb, nb, nb, None)]
    dt_b = [BF16, F32, BF16, F32, BF16]
    qbm, kbm, kb_bf, vbm, vb_bf = _norm_proj_call(
        x2d, sc3, sh3, mod_map, norm_a_g, w_b, defs_b, dt_b, tm, "in_proj_b")

    r3 = lambda a: a.reshape(b, t, a.shape[-1])
    k_a, v_a, k_i, k_b, v_b = r3(ka_bf), r3(va_bf), r3(ki_bf), r3(kb_bf), r3(vb_bf)
    qoff = 0
    if past is not None:
        pk, pv, pi, pbk, pbv = past
        qoff = pk.shape[1]
        qbr = -(-qb // LANES) * LANES
        pad = max(-(-qoff // kb) * kb, qoff + qbr) - (qoff + t)

        def join(p, new):
            p2 = p.reshape(b, qoff, -1).astype(BF16)
            if p2.shape[-1] < new.shape[-1]:
                p2 = jnp.concatenate([p2, jnp.zeros((b, qoff, new.shape[-1] - p2.shape[-1]), BF16)], axis=-1)
            return jnp.concatenate([p2, new, jnp.zeros((b, pad, new.shape[-1]), BF16)], axis=1)

        k_a, v_a, k_i, k_b, v_b = join(pk, k_a), join(pv, v_a), join(pi, k_i), join(pbk, k_b), join(pbv, v_b)
    length = qoff + t
    topk = min(TOPK_MAX, length // 4)

    tiles = _bias_tile_call(rel_bias, qb)
    out_a = _dsa_call(rel_bias, tiles, r3(qa), r3(qi), r3(iw), k_a, v_a, k_i, qb=qb, kb=kb, qoff=qoff, topk=topk)
    out_b = _diff_call(rel_bias, tiles, lam, subln_g, r3(qbm), k_b, v_b, qb=qb, kb=kb, qoff=qoff,
                       lam_init=lam_init)
    rows = (ka.reshape(1, b, t, A_KV_HEADS, A_HEAD_DIM), va.reshape(1, b, t, A_KV_HEADS, A_HEAD_DIM),
            ki.reshape(1, b, t, IDX_DIM), kbm.reshape(1, b, t, B_HEADS, 2, B_HEAD_DIM),
            vbm.reshape(1, b, t, B_HEADS, 2 * B_HEAD_DIM))
    return x2d, out_a.reshape(n, -1), out_b.reshape(n, -1), rows, expand, mod_map


def kernel(x_prompt, x_sample, c_prompt, c_sample, cache_a_k, cache_a_v, cache_a_kidx, cache_b_k, cache_b_v,
           rel_bias, w_ada, b_ada, norm_a_g, w_in, w_out, diff_lam, subln_g, norm_f_g, w_router, router_bias,
           w_gate, w_up, w_down, ws_gate, ws_up, ws_down, final_g):
    assert w_ada.shape[0] == 1, "single-layer model"
    d = x_prompt.shape[-1]
    bp, tp, _ = x_prompt.shape
    bs, ts, _ = x_sample.shape
    lam_init = 0.8 - 0.6 * math.exp(-0.3 * 0)

    c_all = jnp.concatenate([c_prompt, c_sample], axis=0)
    rows = -(-c_all.shape[0] // 8) * 8
    c_pad = jnp.concatenate([c_all, jnp.zeros((rows - c_all.shape[0], d), F32)], axis=0)
    mod_all = _ada_call(c_pad, w_ada[0], b_ada[0])
    mod_p = [mod_all[:bp, k * d:(k + 1) * d] for k in range(6)]
    mod_s = [mod_all[bp:bp + bs, k * d:(k + 1) * d] for k in range(6)]

    w_a, w_b = _split_w_in(w_in[0])
    lam = _lam_call(diff_lam[0], lam_init)
    weights = (norm_a_g[0], w_a, w_b, subln_g[0])
    past = (cache_a_k[0], cache_a_v[0], cache_a_kidx[0], cache_b_k[0], cache_b_v[0])

    tm_p = 256
    tm_s = bs * ts
    xp2, oa_p, ob_p, rows_p, exp_p, map_p = _group_forward(
        x_prompt, mod_p, None, rel_bias, lam, weights, lam_init=lam_init, qb=256, kb=512, tm=tm_p,
        per_row_mod=False)
    xs2, oa_s, ob_s, rows_s, exp_s, map_s = _group_forward(
        x_sample, mod_s, past, rel_bias, lam, weights, lam_init=lam_init, qb=ts, kb=512, tm=tm_s,
        per_row_mod=True)

    w_out_bf = w_out[0].astype(BF16)
    wr_t = w_router[0].T
    rb = router_bias[0].reshape(N_EXPERTS, 1)
    x1_p, h2_p, gt_p = _outproj_call(xp2, oa_p, ob_p, exp_p(mod_p[2]), exp_p(mod_p[4]), exp_p(mod_p[3]),
                                     map_p, norm_f_g[0], w_out_bf, wr_t, rb, tm_p)
    x1_s, h2_s, gt_s = _outproj_call(xs2, oa_s, ob_s, exp_s(mod_s[2]), exp_s(mod_s[4]), exp_s(mod_s[3]),
                                     map_s, norm_f_g[0], w_out_bf, wr_t, rb, tm_s)

    h2 = jnp.concatenate([h2_p, h2_s], axis=0)
    gates = jnp.concatenate([gt_p, gt_s], axis=1).T
    n_all = h2.shape[0]
    tm_moe = next(c for c in (640, 512, 256, 128, 64, 32, 16, 8) if n_all % c == 0)
    moe = _moe_call(h2, gates, w_gate[0].astype(BF16), w_up[0].astype(BF16), w_down[0].astype(BF16),
                    ws_gate[0].astype(BF16), ws_up[0].astype(BF16), ws_down[0].astype(BF16), tm_moe)
    n_p = bp * tp
    y_p = _final_call(x1_p, moe[:n_p], exp_p(mod_p[5]), map_p, final_g, tm_p)
    y_s = _final_call(x1_s, moe[n_p:], exp_s(mod_s[5]), map_s, final_g, tm_s)
    return (y_p.reshape(bp, tp, d), y_s.reshape(bs, ts, d)) + rows_p + rows_s
```

```python
import functools
import math

import numpy as np
import jax
import jax.numpy as jnp
from jax import lax
from jax.experimental import pallas as pl
from jax.experimental.pallas import tpu as pltpu
from jax.experimental.pallas import tpu_sc as plsc

D_MODEL = 2048
CHUNK = 64
A_HEADS = 8
A_KV_HEADS = 2
A_REP = A_HEADS // A_KV_HEADS
A_HEAD_DIM = 128
IDX_HEADS = 16
IDX_DIM = 64
TOPK_MAX = 256
B_HEADS = 4
B_HEAD_DIM = 128
N_REL_BUCKETS = 32
REL_MAX_DIST = 128
N_ATTN_HEADS = A_HEADS + B_HEADS
N_EXPERTS = 64
TOP_K = 8
N_GROUPS = 8
TOPK_GROUPS = 4
EXPERT_DIM = 512
SHARED_DIM = 512
ROUTED_SCALE = 2.5
EPS = 1e-6

LANES = 128
INT_MIN = -(2 ** 31)
NEG_BIG = -1e30
VMEM_LIMIT = 52 * 1024 * 1024

F32 = jnp.float32
BF16 = jnp.bfloat16
NT_DIMS = (((1,), (1,)), ((), ()))


def _params(n_axes):
    return pltpu.CompilerParams(dimension_semantics=("arbitrary",) * n_axes,
                                vmem_limit_bytes=VMEM_LIMIT)


def _resident(shape, index_map):
    return pl.BlockSpec(shape, index_map, pipeline_mode=pl.Buffered(1))


def _bucket_thresholds():
    nb = N_REL_BUCKETS // 2
    max_exact = nb // 2
    n = np.arange(max_exact, 4 * REL_MAX_DIST, dtype=np.int32)
    nf = n.astype(np.float32)
    large = max_exact + (np.log(nf / np.float32(max_exact)) / np.float32(math.log(REL_MAX_DIST / max_exact))
                         * np.float32(nb - max_exact)).astype(np.int32)
    large = np.minimum(large, nb - 1)
    assert np.all(np.diff(large) >= 0)
    ths = [int(n[np.argmax(large >= b)]) for b in range(max_exact + 1, nb)]
    return max_exact, nb, ths


_MAX_EXACT, _NB, _BUCKET_THS = _bucket_thresholds()
assert _BUCKET_THS[-1] <= LANES + 1
FAR_BUCKET = _NB - 1


def _ada_kernel(c_ref, w_ref, b_ref, o_ref):
    c = c_ref[...]
    s = c * jax.nn.sigmoid(c)
    o_ref[...] = jnp.dot(s, w_ref[...], preferred_element_type=F32,
                         precision=lax.Precision.HIGHEST) + b_ref[...]


def _ada_call(c, w_ada, b_ada):
    rows, d = c.shape
    n = w_ada.shape[1]
    tn = 1536
    assert n % tn == 0
    return pl.pallas_call(
        _ada_kernel,
        grid=(n // tn,),
        in_specs=[pl.BlockSpec((rows, d), lambda j: (0, 0)),
                  pl.BlockSpec((d, tn), lambda j: (0, j)),
                  pl.BlockSpec((1, tn), lambda j: (0, j))],
        out_specs=pl.BlockSpec((rows, tn), lambda j: (0, j)),
        out_shape=jax.ShapeDtypeStruct((rows, n), F32),
        compiler_params=_params(1),
        name="ada_mod",
    )(c, w_ada, b_ada.reshape(1, n))


def _modulated_norm(x, g, sc, sh):
    y = x * lax.rsqrt(jnp.mean(x * x, axis=-1, keepdims=True) + EPS) * g
    return y * (1.0 + sc) + sh


def _norm_proj_kernel(x_ref, sc_ref, sh_ref, g_ref, w_ref, *out_refs, defs):
    h = _modulated_norm(x_ref[...], g_ref[...], sc_ref[...], sh_ref[...]).astype(BF16)
    done = {}
    for (start, width, store_w, mult), o_ref in zip(defs, out_refs):
        if (start, width) not in done:
            done[(start, width)] = jnp.dot(h, w_ref[:, start:start + width], preferred_element_type=F32)
        r = done[(start, width)][:, :store_w]
        o_ref[...] = (r if mult is None else r * mult).astype(o_ref.dtype)


def _norm_proj_call(x2d, sc, sh, mod_map, g, w_bf, defs, dtypes, tm, name):
    n, d = x2d.shape
    assert n % tm == 0
    r = sc.shape[1]
    out_shapes = [jax.ShapeDtypeStruct((n, d_[2]), dt) for d_, dt in zip(defs, dtypes)]
    out_specs = [pl.BlockSpec((tm, d_[2]), lambda i: (i, 0)) for d_ in defs]
    return pl.pallas_call(
        functools.partial(_norm_proj_kernel, defs=tuple(defs)),
        grid=(n // tm,),
        in_specs=[pl.BlockSpec((tm, d), lambda i: (i, 0)),
                  pl.BlockSpec((None, r, d), mod_map),
                  pl.BlockSpec((None, r, d), mod_map),
                  pl.BlockSpec((1, d), lambda i: (0, 0)),
                  _resident(w_bf.shape, lambda i: (0, 0))],
        out_specs=out_specs,
        out_shape=out_shapes,
        compiler_params=_params(1),
        name=name,
    )(x2d, sc, sh, g.reshape(1, d), w_bf)


def _bias_tile_kernel(tab_ref, o_ref, *, qb):
    _, rows, width = o_ref.shape
    r = lax.broadcasted_iota(jnp.int32, (rows, width), 0)
    koff = lax.broadcasted_iota(jnp.int32, (rows, width), 1) - LANES
    rel = koff - r
    n = jnp.abs(rel)
    large = jnp.full((rows, width), _MAX_EXACT, jnp.int32)
    for th in _BUCKET_THS:
        large = large + jnp.where(n >= th, 1, 0)
    bucket = jnp.where(rel > 0, _NB, 0) + jnp.where(n < _MAX_EXACT, n, large)
    visible = ((koff >> 6) <= (r >> 6)) & (koff < qb)
    for h in range(N_ATTN_HEADS):
        b = jnp.zeros((rows, width), F32)
        for bk in range(N_REL_BUCKETS):
            b = jnp.where(bucket == bk, tab_ref[bk, h], b)
        o_ref[h] = jnp.where(visible, b, NEG_BIG)


def _bias_tile_call(rel_bias, qb):
    qbr = -(-qb // LANES) * LANES
    return pl.pallas_call(
        functools.partial(_bias_tile_kernel, qb=qb),
        in_specs=[pl.BlockSpec(memory_space=pltpu.SMEM)],
        out_specs=pl.BlockSpec(memory_space=pltpu.VMEM),
        out_shape=jax.ShapeDtypeStruct((N_ATTN_HEADS, qb, LANES + qbr), F32),
        name="bias_tiles",
    )(rel_bias)


def _sweep_far(q0, qb, kb, fn):
    far_end = jnp.maximum(q0 - LANES, 0)
    n_full = far_end // kb
    sub = kb // LANES

    def body(j, carry):
        fn(pl.multiple_of(j * kb, kb), j * sub, None)
        return carry

    lax.fori_loop(0, n_full, body, 0)

    @pl.when(far_end > n_full * kb)
    def _():
        start = pl.multiple_of(n_full * kb, kb)
        col = start + lax.broadcasted_iota(jnp.int32, (qb, LANES), 1)
        fn(start, n_full * sub, [jnp.where(col + c * LANES < far_end, 0.0, NEG_BIG) for c in range(sub)])


def _dsa_kernel(tab_ref, q_ref, qi_ref, iw_ref, k_ref, v_ref, ki_ref, tile_ref, o_ref,
                keys_scr, mfar_scr, mnear_scr, l_scr, acc_scr, *, qb, kb, qoff, topk):
    i = pl.program_id(1)
    q0 = qoff + i * qb
    qbr = tile_ref.shape[2] - LANES
    sub = kb // LANES
    w = iw_ref[:, :IDX_HEADS] * (IDX_HEADS ** -0.5 * IDX_DIM ** -0.5)

    def score_keys(start, width):
        kib = ki_ref[pl.ds(start, width), :]
        acc = jnp.zeros((qb, width), F32)
        for h in range(IDX_HEADS):
            d = lax.dot_general(qi_ref[:, h * LANES:(h + 1) * LANES], kib, NT_DIMS,
                                preferred_element_type=F32)
            acc = acc + w[:, h:h + 1] * jnp.maximum(d, 0.0)
        bits = pltpu.bitcast(acc, jnp.int32)
        return bits ^ ((bits >> 31) & 0x7FFFFFFF)

    n_score = (q0 + kb - 1) // kb

    def score_body(j, carry):
        key = score_keys(pl.multiple_of(j * kb, kb), kb)
        for c in range(sub):
            keys_scr[j * sub + c] = key[:, c * LANES:(c + 1) * LANES]
        return carry

    lax.fori_loop(0, n_score, score_body, 0)

    cur_blk = q0 // LANES
    key = score_keys(pl.multiple_of(q0, LANES), qbr)
    key = jnp.where(tile_ref[0, :, LANES:] > 0.5 * NEG_BIG, key, INT_MIN)
    for c in range(qbr // LANES):
        keys_scr[cur_blk + c] = key[:, c * LANES:(c + 1) * LANES]
    n_blk = cur_blk + qbr // LANES

    def count_ge(t):
        tb = jnp.broadcast_to(t, (qb, LANES))
        n4 = n_blk // 4

        def body4(j, cnt):
            for u in range(4):
                cnt = cnt + jnp.where(keys_scr[4 * j + u] >= tb, 1, 0)
            return cnt

        def body1(j, cnt):
            return cnt + jnp.where(keys_scr[4 * n4 + j] >= tb, 1, 0)

        cnt = lax.fori_loop(0, n4, body4, jnp.zeros((qb, LANES), jnp.int32))
        cnt = lax.fori_loop(0, n_blk - 4 * n4, body1, cnt)
        return jnp.sum(cnt, axis=-1, keepdims=True)

    def bit_body(it, t):
        cand = t + (jnp.int32(1) << (31 - it))
        return jnp.where(count_ge(cand) >= topk, cand, t)

    thr = lax.fori_loop(0, 32, bit_body, jnp.full((qb, 1), INT_MIN, jnp.int32))
    thr = jnp.maximum(thr, INT_MIN + 1)
    thr_b = jnp.broadcast_to(thr, (qb, LANES))

    hd = A_HEAD_DIM

    def select_mask(blk0, n):
        return [jnp.where(keys_scr[blk0 + c] >= thr_b, 0.0, NEG_BIG) for c in range(n)]

    def logits(h, kblk):
        g = h // A_REP
        s = lax.dot_general(q_ref[:, h * hd:(h + 1) * hd], kblk[:, g * hd:(g + 1) * hd], NT_DIMS,
                            preferred_element_type=F32)
        return [s[:, c * LANES:(c + 1) * LANES] for c in range(s.shape[1] // LANES)]

    def tile_cols(h, col0, n):
        return [tile_ref[h, :, col0 + c * LANES:col0 + (c + 1) * LANES] for c in range(n)]

    mfar_scr[...] = jnp.full(mfar_scr.shape, NEG_BIG, F32)
    mnear_scr[...] = jnp.full(mnear_scr.shape, NEG_BIG, F32)

    def far_mask(blk0, colmask):
        am = select_mask(blk0, sub)
        return am if colmask is None else [a + c for a, c in zip(am, colmask)]

    def max_far(start, blk0, colmask):
        kblk = k_ref[pl.ds(start, kb), :]
        am = far_mask(blk0, colmask)
        for h in range(A_HEADS):
            mfar_scr[h] = functools.reduce(
                jnp.maximum, [s + a for s, a in zip(logits(h, kblk), am)], mfar_scr[h])

    def max_near(start, width, blk0, col0):
        kblk = k_ref[pl.ds(start, width), :]
        am = select_mask(blk0, width // LANES)
        for h in range(A_HEADS):
            mnear_scr[h] = functools.reduce(
                jnp.maximum,
                [s + a + b for s, a, b in zip(logits(h, kblk), am, tile_cols(h, col0, width // LANES))],
                mnear_scr[h])

    prev_start = pl.multiple_of(q0 - LANES, LANES)
    cur_start = pl.multiple_of(q0, LANES)
    _sweep_far(q0, qb, kb, max_far)
    pl.when(q0 > 0)(lambda: max_near(prev_start, LANES, cur_blk - 1, 0))
    max_near(cur_start, qbr, cur_blk, LANES)

    for h in range(A_HEADS):
        fb = tab_ref[FAR_BUCKET, h]
        m = jnp.maximum(jnp.max(mfar_scr[h], axis=-1, keepdims=True) + fb,
                        jnp.max(mnear_scr[h], axis=-1, keepdims=True))
        mfar_scr[h] = jnp.broadcast_to(fb - m, (qb, LANES))
        mnear_scr[h] = jnp.broadcast_to(-m, (qb, LANES))
    l_scr[...] = jnp.zeros(l_scr.shape, F32)
    acc_scr[...] = jnp.zeros(acc_scr.shape, F32)

    def accumulate(h, p, vblk):
        g = h // A_REP
        l_scr[h] += functools.reduce(jnp.add, p)
        acc_scr[h] += jnp.dot(jnp.concatenate(p, axis=1).astype(BF16), vblk[:, g * hd:(g + 1) * hd],
                              preferred_element_type=F32)

    def acc_far(start, blk0, colmask):
        kblk = k_ref[pl.ds(start, kb), :]
        vblk = v_ref[pl.ds(start, kb), :]
        am = far_mask(blk0, colmask)
        for h in range(A_HEADS):
            shift = mfar_scr[h]
            accumulate(h, [jnp.exp(s + a + shift) for s, a in zip(logits(h, kblk), am)], vblk)

    def acc_near(start, width, blk0, col0):
        kblk = k_ref[pl.ds(start, width), :]
        vblk = v_ref[pl.ds(start, width), :]
        am = select_mask(blk0, width // LANES)
        for h in range(A_HEADS):
            shift = mnear_scr[h]
            accumulate(h, [jnp.exp(s + a + b + shift) for s, a, b in
                           zip(logits(h, kblk), am, tile_cols(h, col0, width // LANES))], vblk)

    _sweep_far(q0, qb, kb, acc_far)
    pl.when(q0 > 0)(lambda: acc_near(prev_start, LANES, cur_blk - 1, 0))
    acc_near(cur_start, qbr, cur_blk, LANES)

    for h in range(A_HEADS):
        o_ref[:, h * hd:(h + 1) * hd] = (
            acc_scr[h] / jnp.sum(l_scr[h], axis=-1, keepdims=True)).astype(o_ref.dtype)


def _dsa_call(rel_bias, tiles, q, qi, iw, k, v, ki, *, qb, kb, qoff, topk):
    b, t, _ = q.shape
    lp = k.shape[1]
    qbr = tiles.shape[2] - LANES
    q0_max = qoff + t - qb
    key_end = max(-(-q0_max // kb) * kb, q0_max + qbr)
    assert t % qb == 0 and qoff % LANES == 0 and key_end <= lp
    n_slabs = key_end // LANES
    return pl.pallas_call(
        functools.partial(_dsa_kernel, qb=qb, kb=kb, qoff=qoff, topk=topk),
        grid=(b, t // qb),
        in_specs=[pl.BlockSpec(memory_space=pltpu.SMEM),
                  pl.BlockSpec((None, qb, q.shape[2]), lambda bi, i: (bi, i, 0)),
                  pl.BlockSpec((None, qb, qi.shape[2]), lambda bi, i: (bi, i, 0)),
                  pl.BlockSpec((None, qb, iw.shape[2]), lambda bi, i: (bi, i, 0)),
                  pl.BlockSpec((None, lp, k.shape[2]), lambda bi, i: (bi, 0, 0)),
                  pl.BlockSpec((None, lp, v.shape[2]), lambda bi, i: (bi, 0, 0)),
                  pl.BlockSpec((None, lp, ki.shape[2]), lambda bi, i: (bi, 0, 0)),
                  _resident(tiles.shape, lambda bi, i: (0, 0, 0))],
        out_specs=pl.BlockSpec((None, qb, A_HEADS * A_HEAD_DIM), lambda bi, i: (bi, i, 0)),
        out_shape=jax.ShapeDtypeStruct((b, t, A_HEADS * A_HEAD_DIM), BF16),
        scratch_shapes=[pltpu.VMEM((n_slabs, qb, LANES), jnp.int32),
                        pltpu.VMEM((A_HEADS, qb, LANES), F32),
                        pltpu.VMEM((A_HEADS, qb, LANES), F32),
                        pltpu.VMEM((A_HEADS, qb, LANES), F32),
                        pltpu.VMEM((A_HEADS, qb, A_HEAD_DIM), F32)],
        compiler_params=_params(2),
        name="mixer_a",
    )(rel_bias, q, qi, iw, k, v, ki, tiles)


def _diff_kernel(tab_ref, lam_ref, sg_ref, q_ref, k_ref, v_ref, tile_ref, o_ref,
                 mfar_scr, mnear_scr, l_scr, acc_scr, *, qb, kb, qoff, lam_init):
    head = pl.program_id(1)
    i = pl.program_id(2)
    q0 = qoff + i * qb
    qbr = tile_ref.shape[2] - LANES
    hd = B_HEAD_DIM
    far_bias = tab_ref[FAR_BUCKET, A_HEADS + head]

    def logits(c, kblk):
        s = lax.dot_general(q_ref[:, c * hd:(c + 1) * hd], kblk[:, c * hd:(c + 1) * hd], NT_DIMS,
                            preferred_element_type=F32)
        return [s[:, u * LANES:(u + 1) * LANES] for u in range(s.shape[1] // LANES)]

    def tile_cols(col0, n):
        return [tile_ref[0, :, col0 + u * LANES:col0 + (u + 1) * LANES] for u in range(n)]

    mfar_scr[...] = jnp.full(mfar_scr.shape, NEG_BIG, F32)
    mnear_scr[...] = jnp.full(mnear_scr.shape, NEG_BIG, F32)

    def max_far(start, blk0, colmask):
        kblk = k_ref[pl.ds(start, kb), :]
        for c in range(2):
            s = logits(c, kblk)
            if colmask is not None:
                s = [x + m for x, m in zip(s, colmask)]
            mfar_scr[c] = functools.reduce(jnp.maximum, s, mfar_scr[c])

    def max_near(start, width, col0):
        kblk = k_ref[pl.ds(start, width), :]
        for c in range(2):
            mnear_scr[c] = functools.reduce(
                jnp.maximum, [s + b for s, b in zip(logits(c, kblk), tile_cols(col0, width // LANES))],
                mnear_scr[c])

    prev_start = pl.multiple_of(q0 - LANES, LANES)
    cur_start = pl.multiple_of(q0, LANES)
    _sweep_far(q0, qb, kb, max_far)
    pl.when(q0 > 0)(lambda: max_near(prev_start, LANES, 0))
    max_near(cur_start, qbr, LANES)

    for c in range(2):
        m = jnp.maximum(jnp.max(mfar_scr[c], axis=-1, keepdims=True) + far_bias,
                        jnp.max(mnear_scr[c], axis=-1, keepdims=True))
        mfar_scr[c] = jnp.broadcast_to(far_bias - m, (qb, LANES))
        mnear_scr[c] = jnp.broadcast_to(-m, (qb, LANES))
    l_scr[...] = jnp.zeros(l_scr.shape, F32)
    acc_scr[...] = jnp.zeros(acc_scr.shape, F32)

    def accumulate(c, p, vblk):
        l_scr[c] += functools.reduce(jnp.add, p)
        acc_scr[c] += jnp.dot(jnp.concatenate(p, axis=1).astype(BF16), vblk, preferred_element_type=F32)

    def acc_far(start, blk0, colmask):
        kblk = k_ref[pl.ds(start, kb), :]
        vblk = v_ref[pl.ds(start, kb), :]
        for c in range(2):
            shift = mfar_scr[c]
            shifts = [shift] * (kb // LANES) if colmask is None else [shift + m for m in colmask]
            accumulate(c, [jnp.exp(s + sh) for s, sh in zip(logits(c, kblk), shifts)], vblk)

    def acc_near(start, width, col0):
        kblk = k_ref[pl.ds(start, width), :]
        vblk = v_ref[pl.ds(start, width), :]
        for c in range(2):
            shift = mnear_scr[c]
            accumulate(c, [jnp.exp(s + b + shift) for s, b in
                           zip(logits(c, kblk), tile_cols(col0, width // LANES))], vblk)

    _sweep_far(q0, qb, kb, acc_far)
    pl.when(q0 > 0)(lambda: acc_near(prev_start, LANES, 0))
    acc_near(cur_start, qbr, LANES)

    lam = lam_ref[0, 0]
    o = (acc_scr[0] / jnp.sum(l_scr[0], axis=-1, keepdims=True)
         - lam * (acc_scr[1] / jnp.sum(l_scr[1], axis=-1, keepdims=True)))
    o = o * lax.rsqrt(jnp.mean(o * o, axis=-1, keepdims=True) + EPS) * sg_ref[...]
    o_ref[...] = (o * (1.0 - lam_init)).astype(o_ref.dtype)


def _diff_call(rel_bias, tiles, lam, subln_g, q, k, v, *, qb, kb, qoff, lam_init):
    b, t, _ = q.shape
    lp = k.shape[1]
    hw = 2 * B_HEAD_DIM
    qbr = tiles.shape[2] - LANES
    q0_max = qoff + t - qb
    key_end = max(-(-max(q0_max - LANES, 0) // kb) * kb, q0_max + qbr)
    assert t % qb == 0 and qoff % LANES == 0 and key_end <= lp
    return pl.pallas_call(
        functools.partial(_diff_kernel, qb=qb, kb=kb, qoff=qoff, lam_init=lam_init),
        grid=(b, B_HEADS, t // qb),
        in_specs=[pl.BlockSpec(memory_space=pltpu.SMEM),
                  pl.BlockSpec(memory_space=pltpu.SMEM),
                  pl.BlockSpec((1, hw), lambda bi, h, i: (0, 0)),
                  pl.BlockSpec((None, qb, hw), lambda bi, h, i: (bi, i, h)),
                  pl.BlockSpec((None, lp, hw), lambda bi, h, i: (bi, 0, h)),
                  pl.BlockSpec((None, lp, hw), lambda bi, h, i: (bi, 0, h)),
                  pl.BlockSpec((1, qb, tiles.shape[2]), lambda bi, h, i: (A_HEADS + h, 0, 0))],
        out_specs=pl.BlockSpec((None, qb, hw), lambda bi, h, i: (bi, i, h)),
        out_shape=jax.ShapeDtypeStruct((b, t, B_HEADS * hw), BF16),
        scratch_shapes=[pltpu.VMEM((2, qb, LANES), F32),
                        pltpu.VMEM((2, qb, LANES), F32),
                        pltpu.VMEM((2, qb, LANES), F32),
                        pltpu.VMEM((2, qb, hw), F32)],
        compiler_params=_params(3),
        name="mixer_b",
    )(rel_bias, lam, subln_g.reshape(1, hw), q, k, v, tiles)


def _lam_kernel(dl_ref, o_ref, *, lam_init):
    dl = dl_ref[...]
    s1 = jnp.sum(dl[0:1] * dl[1:2], axis=-1, keepdims=True)
    s2 = jnp.sum(dl[2:3] * dl[3:4], axis=-1, keepdims=True)
    o_ref[...] = jnp.exp(s1) - jnp.exp(s2) + lam_init


def _lam_call(diff_lam, lam_init):
    return pl.pallas_call(
        functools.partial(_lam_kernel, lam_init=lam_init),
        out_shape=jax.ShapeDtypeStruct((1, 1), F32),
        name="diff_lambda",
    )(diff_lam)


def _route(logits_t, rbias, counts):
    tm = logits_t.shape[1]
    per = N_EXPERTS // N_GROUPS
    scores = jax.nn.sigmoid(logits_t).reshape(N_GROUPS, per, tm)
    ch = scores + rbias.reshape(N_GROUPS, per, 1)
    sub = lax.broadcasted_iota(jnp.int32, ch.shape, 1)
    grp = lax.broadcasted_iota(jnp.int32, ch.shape, 0)
    m1 = jnp.max(ch, axis=1, keepdims=True)
    i1 = jnp.min(jnp.where(ch == m1, sub, per), axis=1, keepdims=True)
    m2 = jnp.max(jnp.where(sub == i1, -jnp.inf, ch), axis=1, keepdims=True)
    gs = jnp.broadcast_to(m1 + m2, ch.shape)
    rank = jnp.zeros(ch.shape, jnp.int32)
    for g2 in range(N_GROUPS):
        o = gs[g2:g2 + 1]
        rank = rank + jnp.where(o > gs, 1, 0) + jnp.where(o == gs, 1, 0) * jnp.where(grp > g2, 1, 0)
    cm = jnp.where(rank < TOPK_GROUPS, ch, -jnp.inf)
    eid = grp * per + sub
    erank = jnp.zeros(cm.shape, jnp.int32)
    for e2 in range(N_EXPERTS):
        o = cm[e2 // per:e2 // per + 1, e2 % per:e2 % per + 1, :]
        erank = erank + jnp.where(o > cm, 1, 0) + jnp.where(o == cm, 1, 0) * jnp.where(eid > e2, 1, 0)
    chosen = erank < TOP_K
    sel = jnp.where(chosen, scores, 0.0)
    denom = jnp.sum(jnp.sum(sel, axis=1, keepdims=True), axis=0, keepdims=True)
    gates = sel / denom * ROUTED_SCALE

    ones = jnp.where(chosen, 1.0, 0.0).reshape(N_EXPERTS, tm)
    earlier = (lax.broadcasted_iota(jnp.int32, (tm, tm), 0) < lax.broadcasted_iota(jnp.int32, (tm, tm), 1))
    before = jnp.dot(ones.astype(BF16), jnp.where(earlier, 1.0, 0.0).astype(BF16), preferred_element_type=F32)
    pos = (before + counts.astype(F32)).reshape(N_GROUPS, per, tm)
    new_counts = counts + jnp.sum(ones, axis=1, keepdims=True).astype(jnp.int32)

    def per_slot(values):
        rows = [jnp.sum(jnp.sum(jnp.where(erank == k, values, 0.0), axis=1, keepdims=True), axis=0)
                for k in range(TOP_K)]
        return jnp.concatenate(rows, axis=0)

    return (per_slot(eid.astype(F32)).astype(jnp.int32), per_slot(pos).astype(jnp.int32), per_slot(gates),
            new_counts)


def _pack_pair(a, b):
    ua = pltpu.bitcast(a.astype(BF16).astype(F32), jnp.uint32)
    ub = pltpu.bitcast(b.astype(BF16).astype(F32), jnp.uint32)
    return pltpu.bitcast(ua | (ub >> 16), jnp.int32)


def _unpack_pair(p):
    u = pltpu.bitcast(p, jnp.uint32)
    a = pltpu.bitcast(u & jnp.uint32(0xFFFF0000), F32)
    b = pltpu.bitcast(u << 16, F32)
    return a.astype(BF16), b.astype(BF16)


PARTS = 4
PART_W = D_MODEL // 2 // PARTS


def _pack_rows(x):
    half = x.shape[1] // 2
    return jnp.stack([_pack_pair(x[:, j * PART_W:(j + 1) * PART_W],
                                 x[:, half + j * PART_W:half + (j + 1) * PART_W]) for j in range(PARTS)])


def _packed_matmul(parts_ref, w_ref):
    half = w_ref.shape[0] // 2
    acc = None
    for j in range(PARTS):
        a, b = _unpack_pair(parts_ref[j])
        t = (jnp.dot(a, w_ref[j * PART_W:(j + 1) * PART_W, :], preferred_element_type=F32)
             + jnp.dot(b, w_ref[half + j * PART_W:half + (j + 1) * PART_W, :], preferred_element_type=F32))
        acc = t if acc is None else acc + t
    return acc


def _outproj_kernel(x_ref, oa_ref, ob_ref, ga_ref, sc_ref, sh_ref, g_ref, w_ref, wr_ref, rb_ref, cin_ref,
                    x1_ref, h2_ref, ek_ref, pk_ref, gk_ref, cout_ref, cnt_scr):
    @pl.when(pl.program_id(0) == 0)
    def _():
        cnt_scr[...] = cin_ref[...]

    half = oa_ref.shape[1]
    mix = (jnp.dot(oa_ref[...], w_ref[:half, :], preferred_element_type=F32)
           + jnp.dot(ob_ref[...], w_ref[half:, :], preferred_element_type=F32))
    x1 = x_ref[...] + ga_ref[...] * mix
    x1_ref[...] = x1
    h2 = _modulated_norm(x1, g_ref[...], sc_ref[...], sh_ref[...])
    h2_ref[...] = _pack_rows(h2)
    logits_t = lax.dot_general(wr_ref[...], h2, NT_DIMS, preferred_element_type=F32,
                               precision=lax.Precision.HIGHEST)
    ek, pk, gk, counts = _route(logits_t, rb_ref[...], cnt_scr[...])
    ek_ref[...] = ek
    pk_ref[...] = pk
    gk_ref[...] = gk
    cnt_scr[...] = counts
    cout_ref[...] = counts


def _outproj_call(x2d, oa, ob, ga, sc, sh, mod_map, g, w_bf, wr_t, rbias, counts_in, tm):
    n, d = x2d.shape
    assert n % tm == 0
    r = sc.shape[1]
    mod_spec = pl.BlockSpec((None, r, d), mod_map)
    slot_spec = pl.BlockSpec((TOP_K, tm), lambda i: (0, i))
    return pl.pallas_call(
        _outproj_kernel,
        grid=(n // tm,),
        in_specs=[pl.BlockSpec((tm, d), lambda i: (i, 0)),
                  pl.BlockSpec((tm, oa.shape[1]), lambda i: (i, 0)),
                  pl.BlockSpec((tm, ob.shape[1]), lambda i: (i, 0)),
                  mod_spec, mod_spec, mod_spec,
                  pl.BlockSpec((1, d), lambda i: (0, 0)),
                  _resident(w_bf.shape, lambda i: (0, 0)),
                  _resident(wr_t.shape, lambda i: (0, 0)),
                  pl.BlockSpec((N_EXPERTS, 1), lambda i: (0, 0)),
                  pl.BlockSpec((N_EXPERTS, 1), lambda i: (0, 0))],
        out_specs=[pl.BlockSpec((tm, d), lambda i: (i, 0)),
                   pl.BlockSpec((PARTS, tm, PART_W), lambda i: (0, i, 0)),
                   slot_spec, slot_spec, slot_spec,
                   pl.BlockSpec((N_EXPERTS, 1), lambda i: (0, 0))],
        out_shape=[jax.ShapeDtypeStruct((n, d), F32),
                   jax.ShapeDtypeStruct((PARTS, n, PART_W), jnp.int32),
                   jax.ShapeDtypeStruct((TOP_K, n), jnp.int32),
                   jax.ShapeDtypeStruct((TOP_K, n), jnp.int32),
                   jax.ShapeDtypeStruct((TOP_K, n), F32),
                   jax.ShapeDtypeStruct((N_EXPERTS, 1), jnp.int32)],
        scratch_shapes=[pltpu.VMEM((N_EXPERTS, 1), jnp.int32)],
        compiler_params=_params(1),
        name="out_proj_router",
    )(x2d, oa, ob, ga, sc, sh, g.reshape(1, d), w_bf, wr_t, rbias, counts_in)


SC_WINDOW = 128
EXPERT_TILE = 512


def _sc_mesh():
    return plsc.VectorSubcoreMesh(core_axis_name="core", subcore_axis_name="subcore")


def _sc_scatter_rows(x, idx, n_out):
    n, w = x.shape
    steps = n // SC_WINDOW
    total = idx.shape[1] // SC_WINDOW

    @pl.kernel(out_type=jax.ShapeDtypeStruct((n_out, w), x.dtype), mesh=_sc_mesh(), scratch_types=[],
               name="moe_dispatch")
    def scatter(x_hbm, i_hbm, o_hbm):
        def body(x_vmem, i_vmem):
            pltpu.sync_copy(x_vmem, o_hbm.at[i_vmem.at[0]])

        pltpu.emit_pipeline(
            body, grid=(total,),
            in_specs=[pl.BlockSpec((SC_WINDOW, w), lambda i: (i % steps, 0)),
                      pl.BlockSpec((1, SC_WINDOW), lambda i: (0, i))],
            out_specs=[], core_axis_name=("core", "subcore"),
            dimension_semantics=(pltpu.PARALLEL,))(x_hbm, i_hbm)

    return scatter(x, idx)


def _sc_gather_rows(y, idx):
    w = y.shape[1]
    n = idx.shape[1]

    @pl.kernel(out_type=jax.ShapeDtypeStruct((n, w), y.dtype), mesh=_sc_mesh(), scratch_types=[],
               name="moe_collect")
    def gather(y_hbm, i_hbm, o_hbm):
        def body(i_vmem, o_vmem):
            pltpu.sync_copy(y_hbm.at[i_vmem.at[0]], o_vmem)

        pltpu.emit_pipeline(
            body, grid=(n // SC_WINDOW,),
            in_specs=[pl.BlockSpec((1, SC_WINDOW), lambda i: (0, i))],
            out_specs=[pl.BlockSpec((SC_WINDOW, w), lambda i: (i, 0))],
            core_axis_name=("core", "subcore"),
            dimension_semantics=(pltpu.PARALLEL,))(i_hbm, o_hbm)

    return gather(y, idx)


def _swiglu(parts_ref, wg_ref, wu_ref, wd_ref):
    a = _packed_matmul(parts_ref, wg_ref)
    u = _packed_matmul(parts_ref, wu_ref)
    hid = (a * jax.nn.sigmoid(a) * u).astype(BF16)
    return jnp.dot(hid, wd_ref[...], preferred_element_type=F32)


def _expert_kernel(te_ref, nu_ref, x_ref, wg_ref, wu_ref, wd_ref, y_ref, wg_scr, wu_scr, wd_scr):
    i = pl.program_id(0)

    @pl.when(i < nu_ref[0])
    def _():
        @pl.when((i == 0) | (te_ref[i] != te_ref[jnp.maximum(i - 1, 0)]))
        def _():
            wg_scr[...] = wg_ref[...].astype(BF16)
            wu_scr[...] = wu_ref[...].astype(BF16)
            wd_scr[...] = wd_ref[...].astype(BF16)

        y_ref[...] = _pack_rows(_swiglu(x_ref, wg_scr, wu_scr, wd_scr))


def _expert_call(tile_expert, n_used, x_sorted, wg, wu, wd):
    _, r, _ = x_sorted.shape
    ne, d, de = wg.shape
    n_tiles = r // EXPERT_TILE
    row_map = lambda i, te, nu: (0, jnp.minimum(i, nu[0] - 1), 0)
    return pl.pallas_call(
        _expert_kernel,
        grid_spec=pltpu.PrefetchScalarGridSpec(
            num_scalar_prefetch=2,
            grid=(n_tiles,),
            in_specs=[pl.BlockSpec((PARTS, EXPERT_TILE, PART_W), row_map),
                      pl.BlockSpec((None, d, de), lambda i, te, nu: (te[i], 0, 0)),
                      pl.BlockSpec((None, d, de), lambda i, te, nu: (te[i], 0, 0)),
                      pl.BlockSpec((None, de, d), lambda i, te, nu: (te[i], 0, 0))],
            out_specs=pl.BlockSpec((PARTS, EXPERT_TILE, PART_W), row_map),
            scratch_shapes=[pltpu.VMEM((d, de), BF16), pltpu.VMEM((d, de), BF16), pltpu.VMEM((de, d), BF16)]),
        out_shape=jax.ShapeDtypeStruct(x_sorted.shape, jnp.int32),
        compiler_params=_params(1),
        name="moe_experts",
    )(tile_expert, n_used, x_sorted, wg, wu, wd)


def _final_kernel(x_ref, h_ref, rows_ref, gk_ref, gf_ref, g_ref, sg_ref, su_ref, sd_ref, o_ref):
    moe = _swiglu(h_ref, sg_ref, su_ref, sd_ref)
    half = moe.shape[1] // 2
    lo = [moe[:, j * PART_W:(j + 1) * PART_W] for j in range(PARTS)]
    hi = [moe[:, half + j * PART_W:half + (j + 1) * PART_W] for j in range(PARTS)]
    for k in range(TOP_K):
        gate = gk_ref[:, k:k + 1]
        for j in range(PARTS):
            a, b = _unpack_pair(rows_ref[k, j])
            lo[j] = lo[j] + gate * a.astype(F32)
            hi[j] = hi[j] + gate * b.astype(F32)
    x = x_ref[...] + gf_ref[...] * jnp.concatenate(lo + hi, axis=1)
    o_ref[...] = x * lax.rsqrt(jnp.mean(x * x, axis=-1, keepdims=True) + EPS) * g_ref[...]


def _final_call(x1, h2p, rows, row_block0, gk_t, gf, mod_map, g, sg, su, sd, tm):
    n, d = x1.shape
    assert n % tm == 0
    r = gf.shape[1]
    return pl.pallas_call(
        _final_kernel,
        grid=(n // tm,),
        in_specs=[pl.BlockSpec((tm, d), lambda i: (i, 0)),
                  pl.BlockSpec((PARTS, tm, PART_W), lambda i: (0, i, 0)),
                  pl.BlockSpec((TOP_K, PARTS, tm, PART_W), lambda i: (0, 0, i + row_block0, 0)),
                  pl.BlockSpec((tm, TOP_K), lambda i: (i, 0)),
                  pl.BlockSpec((None, r, d), mod_map),
                  pl.BlockSpec((1, d), lambda i: (0, 0)),
                  _resident(sg.shape, lambda i: (0, 0)),
                  _resident(su.shape, lambda i: (0, 0)),
                  _resident(sd.shape, lambda i: (0, 0))],
        out_specs=pl.BlockSpec((tm, d), lambda i: (i, 0)),
        out_shape=jax.ShapeDtypeStruct((n, d), F32),
        compiler_params=_params(1),
        name="final_norm",
    )(x1, h2p, rows, gk_t, gf, g.reshape(1, d), sg, su, sd)


_QA, _KA, _VA, _QI, _KI, _IW, _QB, _KB, _VB = range(9)
_IN_SIZES = (A_HEADS * A_HEAD_DIM, A_KV_HEADS * A_HEAD_DIM, A_KV_HEADS * A_HEAD_DIM,
             IDX_HEADS * IDX_DIM, IDX_DIM, IDX_HEADS,
             B_HEADS * 2 * B_HEAD_DIM, B_HEADS * 2 * B_HEAD_DIM, B_HEADS * 2 * B_HEAD_DIM)


def _split_w_in(w_in):
    d = w_in.shape[0]
    offs = np.concatenate([[0], np.cumsum(_IN_SIZES)])
    cols = [w_in[:, offs[k]:offs[k + 1]].astype(BF16) for k in range(9)]
    zero = lambda n: jnp.zeros((d, n), BF16)
    qi = jnp.concatenate([cols[_QI].reshape(d, IDX_HEADS, IDX_DIM),
                          jnp.zeros((d, IDX_HEADS, LANES - IDX_DIM), BF16)], axis=-1).reshape(d, IDX_HEADS * LANES)
    w_a = jnp.concatenate([cols[_QA], cols[_KA], cols[_VA], qi,
                           cols[_KI], zero(LANES - IDX_DIM), cols[_IW], zero(LANES - IDX_HEADS)], axis=1)
    w_b = jnp.concatenate([cols[_QB], cols[_KB], cols[_VB]], axis=1)
    return w_a, w_b


def _group_forward(x, mod, past, rel_bias, lam, weights, *, lam_init, qb, kb, tm, per_row_mod):
    (norm_a_g, w_a, w_b, subln_g) = weights
    b, t, d = x.shape
    n = b * t
    x2d = x.reshape(n, d)
    sh_a, sc_a = mod[0], mod[1]
    if per_row_mod:
        expand = lambda m: jnp.repeat(m, t, axis=0).reshape(1, n, d)
        mod_map = lambda i: (0, i, 0)
    else:
        expand = lambda m: m.reshape(b, 1, d)
        mod_map = lambda i: ((i * tm) // t, 0, 0)
    sc3, sh3 = expand(sc_a), expand(sh_a)

    na = A_HEADS * A_HEAD_DIM
    nk = A_KV_HEADS * A_HEAD_DIM
    nq = IDX_HEADS * LANES
    c_ka, c_va, c_qi = na, na + nk, na + 2 * nk
    c_ki, c_iw = c_qi + nq, c_qi + nq + LANES
    defs_a = [(0, na, na, A_HEAD_DIM ** -0.5), (c_ka, nk, nk, None), (c_ka, nk, nk, None),
              (c_va, nk, nk, None), (c_va, nk, nk, None), (c_qi, nq, nq, None),
              (c_ki, LANES, IDX_DIM, None), (c_ki, LANES, LANES, None), (c_iw, LANES, LANES, None)]
    dt_a = [BF16, F32, BF16, F32, BF16, BF16, F32, BF16, F32]
    qa, ka, ka_bf, va, va_bf, qi, ki, ki_bf, iw = _norm_proj_call(
        x2d, sc3, sh3, mod_map, norm_a_g, w_a, defs_a, dt_a, tm, "in_proj_a")
    nb = B_HEADS * 2 * B_HEAD_DIM
    defs_b = [(0, nb, nb, B_HEAD_DIM ** -0.5), (nb, nb, nb, None), (nb, nb, nb, None),
              (2 * nb, nb, nb, None), (2 * nb, nb, nb, None)]
    dt_b = [BF16, F32, BF16, F32, BF16]
    qbm, kbm, kb_bf, vbm, vb_bf = _norm_proj_call(
        x2d, sc3, sh3, mod_map, norm_a_g, w_b, defs_b, dt_b, tm, "in_proj_b")

    r3 = lambda a: a.reshape(b, t, a.shape[-1])
    k_a, v_a, k_i, k_b, v_b = r3(ka_bf), r3(va_bf), r3(ki_bf), r3(kb_bf), r3(vb_bf)
    qoff = 0
    if past is not None:
        pk, pv, pi, pbk, pbv = past
        qoff = pk.shape[1]
        qbr = -(-qb // LANES) * LANES
        pad = max(-(-qoff // kb) * kb, qoff + qbr) - (qoff + t)

        def join(p, new):
            p2 = p.reshape(b, qoff, -1).astype(BF16)
            if p2.shape[-1] < new.shape[-1]:
                p2 = jnp.concatenate([p2, jnp.zeros((b, qoff, new.shape[-1] - p2.shape[-1]), BF16)], axis=-1)
            return jnp.concatenate([p2, new, jnp.zeros((b, pad, new.shape[-1]), BF16)], axis=1)

        k_a, v_a, k_i, k_b, v_b = join(pk, k_a), join(pv, v_a), join(pi, k_i), join(pbk, k_b), join(pbv, v_b)
    length = qoff + t
    topk = min(TOPK_MAX, length // 4)

    tiles = _bias_tile_call(rel_bias, qb)
    out_a = _dsa_call(rel_bias, tiles, r3(qa), r3(qi), r3(iw), k_a, v_a, k_i, qb=qb, kb=kb, qoff=qoff, topk=topk)
    out_b = _diff_call(rel_bias, tiles, lam, subln_g, r3(qbm), k_b, v_b, qb=qb, kb=kb, qoff=qoff,
                       lam_init=lam_init)
    rows = (ka.reshape(1, b, t, A_KV_HEADS, A_HEAD_DIM), va.reshape(1, b, t, A_KV_HEADS, A_HEAD_DIM),
            ki.reshape(1, b, t, IDX_DIM), kbm.reshape(1, b, t, B_HEADS, 2, B_HEAD_DIM),
            vbm.reshape(1, b, t, B_HEADS, 2 * B_HEAD_DIM))
    return x2d, out_a.reshape(n, -1), out_b.reshape(n, -1), rows, expand, mod_map


def kernel(x_prompt, x_sample, c_prompt, c_sample, cache_a_k, cache_a_v, cache_a_kidx, cache_b_k, cache_b_v,
           rel_bias, w_ada, b_ada, norm_a_g, w_in, w_out, diff_lam, subln_g, norm_f_g, w_router, router_bias,
           w_gate, w_up, w_down, ws_gate, ws_up, ws_down, final_g):
    assert w_ada.shape[0] == 1, "single-layer model"
    d = x_prompt.shape[-1]
    bp, tp, _ = x_prompt.shape
    bs, ts, _ = x_sample.shape
    lam_init = 0.8 - 0.6 * math.exp(-0.3 * 0)

    c_all = jnp.concatenate([c_prompt, c_sample], axis=0)
    rows = -(-c_all.shape[0] // 8) * 8
    c_pad = jnp.concatenate([c_all, jnp.zeros((rows - c_all.shape[0], d), F32)], axis=0)
    mod_all = _ada_call(c_pad, w_ada[0], b_ada[0])
    mod_p = [mod_all[:bp, k * d:(k + 1) * d] for k in range(6)]
    mod_s = [mod_all[bp:bp + bs, k * d:(k + 1) * d] for k in range(6)]

    w_a, w_b = _split_w_in(w_in[0])
    lam = _lam_call(diff_lam[0], lam_init)
    weights = (norm_a_g[0], w_a, w_b, subln_g[0])
    past = (cache_a_k[0], cache_a_v[0], cache_a_kidx[0], cache_b_k[0], cache_b_v[0])

    tm_p = 256
    tm_s = bs * ts
    xp2, oa_p, ob_p, rows_p, exp_p, map_p = _group_forward(
        x_prompt, mod_p, None, rel_bias, lam, weights, lam_init=lam_init, qb=256, kb=512, tm=tm_p,
        per_row_mod=False)
    xs2, oa_s, ob_s, rows_s, exp_s, map_s = _group_forward(
        x_sample, mod_s, past, rel_bias, lam, weights, lam_init=lam_init, qb=ts, kb=512, tm=tm_s,
        per_row_mod=True)

    w_out_bf = w_out[0].astype(BF16)
    wr_t = w_router[0].T
    rb = router_bias[0].reshape(N_EXPERTS, 1)
    no_rows = jnp.zeros((N_EXPERTS, 1), jnp.int32)
    x1_p, h2_p, ek_p, pk_p, gk_p, cnt_p = _outproj_call(
        xp2, oa_p, ob_p, exp_p(mod_p[2]), exp_p(mod_p[4]), exp_p(mod_p[3]), map_p, norm_f_g[0], w_out_bf, wr_t, rb,
        no_rows, tm_p)
    x1_s, h2_s, ek_s, pk_s, gk_s, cnt_all = _outproj_call(
        xs2, oa_s, ob_s, exp_s(mod_s[2]), exp_s(mod_s[4]), exp_s(mod_s[3]), map_s, norm_f_g[0], w_out_bf, wr_t, rb,
        cnt_p, tm_s)

    n_p, n_s = bp * tp, bs * ts
    n_all = n_p + n_s
    n_rows = -(-(n_all * TOP_K) // EXPERT_TILE) * EXPERT_TILE + N_EXPERTS * EXPERT_TILE
    counts = cnt_all[:, 0]
    padded = (counts + EXPERT_TILE - 1) // EXPERT_TILE * EXPERT_TILE
    ends = jnp.cumsum(padded)
    starts = ends - padded
    tile_first_row = jnp.arange(n_rows // EXPERT_TILE, dtype=jnp.int32) * EXPERT_TILE
    tile_expert = jnp.minimum(jnp.searchsorted(ends, tile_first_row, side="right"), N_EXPERTS - 1).astype(jnp.int32)
    n_used = (ends[-1:] // EXPERT_TILE).astype(jnp.int32)
    dest = jnp.take(starts, jnp.concatenate([ek_p, ek_s], axis=1)) + jnp.concatenate([pk_p, pk_s], axis=1)
    piece = jnp.arange(PARTS, dtype=jnp.int32)[None, :, None]
    idx = (dest[:, None, :] + piece * n_rows).reshape(1, TOP_K * PARTS * n_all).astype(jnp.int32)

    h2_all = jnp.concatenate([h2_p, h2_s], axis=1).reshape(PARTS * n_all, PART_W)
    x_sorted = _sc_scatter_rows(h2_all, idx, PARTS * n_rows).reshape(PARTS, n_rows, PART_W)
    y_sorted = _expert_call(tile_expert, n_used, x_sorted, w_gate[0], w_up[0], w_down[0])
    rows = _sc_gather_rows(y_sorted.reshape(PARTS * n_rows, PART_W), idx).reshape(TOP_K, PARTS, n_all, PART_W)

    shared = (ws_gate[0].astype(BF16), ws_up[0].astype(BF16), ws_down[0].astype(BF16))
    assert n_p % tm_s == 0
    y_p = _final_call(x1_p, h2_p, rows, 0, gk_p.T, exp_p(mod_p[5]), map_p, final_g, *shared, tm_p)
    y_s = _final_call(x1_s, h2_s, rows, n_p // tm_s, gk_s.T, exp_s(mod_s[5]), map_s, final_g, *shared, tm_s)
    return (y_p.reshape(bp, tp, d), y_s.reshape(bs, ts, d)) + rows_p + rows_s
```

```python
import functools
import math

import numpy as np
import jax
import jax.numpy as jnp
from jax import lax
from jax.experimental import pallas as pl
from jax.experimental.pallas import tpu as pltpu
from jax.experimental.pallas import tpu_sc as plsc

D_MODEL = 2048
CHUNK = 64
A_HEADS = 8
A_KV_HEADS = 2
A_REP = A_HEADS // A_KV_HEADS
A_HEAD_DIM = 128
IDX_HEADS = 16
IDX_DIM = 64
TOPK_MAX = 256
B_HEADS = 4
B_HEAD_DIM = 128
N_REL_BUCKETS = 32
REL_MAX_DIST = 128
N_ATTN_HEADS = A_HEADS + B_HEADS
N_EXPERTS = 64
TOP_K = 8
N_GROUPS = 8
TOPK_GROUPS = 4
EXPERT_DIM = 512
SHARED_DIM = 512
ROUTED_SCALE = 2.5
EPS = 1e-6

LANES = 128
INT_MIN = -(2 ** 31)
NEG_BIG = -1e30
VMEM_LIMIT = 52 * 1024 * 1024

F32 = jnp.float32
BF16 = jnp.bfloat16
NT_DIMS = (((1,), (1,)), ((), ()))


def _params(n_axes):
    return pltpu.CompilerParams(dimension_semantics=("arbitrary",) * n_axes,
                                vmem_limit_bytes=VMEM_LIMIT)


def _resident(shape, index_map):
    return pl.BlockSpec(shape, index_map, pipeline_mode=pl.Buffered(1))


def _bucket_thresholds():
    nb = N_REL_BUCKETS // 2
    max_exact = nb // 2
    n = np.arange(max_exact, 4 * REL_MAX_DIST, dtype=np.int32)
    nf = n.astype(np.float32)
    large = max_exact + (np.log(nf / np.float32(max_exact)) / np.float32(math.log(REL_MAX_DIST / max_exact))
                         * np.float32(nb - max_exact)).astype(np.int32)
    large = np.minimum(large, nb - 1)
    assert np.all(np.diff(large) >= 0)
    ths = [int(n[np.argmax(large >= b)]) for b in range(max_exact + 1, nb)]
    return max_exact, nb, ths


_MAX_EXACT, _NB, _BUCKET_THS = _bucket_thresholds()
assert _BUCKET_THS[-1] <= LANES + 1
FAR_BUCKET = _NB - 1


def _ada_kernel(c_ref, w_ref, b_ref, o_ref):
    c = c_ref[...]
    s = c * jax.nn.sigmoid(c)
    o_ref[...] = jnp.dot(s, w_ref[...], preferred_element_type=F32,
                         precision=lax.Precision.HIGHEST) + b_ref[...]


def _ada_call(c, w_ada, b_ada):
    rows, d = c.shape
    n = w_ada.shape[1]
    tn = 1536
    assert n % tn == 0
    return pl.pallas_call(
        _ada_kernel,
        grid=(n // tn,),
        in_specs=[pl.BlockSpec((rows, d), lambda j: (0, 0)),
                  pl.BlockSpec((d, tn), lambda j: (0, j)),
                  pl.BlockSpec((1, tn), lambda j: (0, j))],
        out_specs=pl.BlockSpec((rows, tn), lambda j: (0, j)),
        out_shape=jax.ShapeDtypeStruct((rows, n), F32),
        compiler_params=_params(1),
        name="ada_mod",
    )(c, w_ada, b_ada.reshape(1, n))


def _modulated_norm(x, g, sc, sh):
    y = x * lax.rsqrt(jnp.mean(x * x, axis=-1, keepdims=True) + EPS) * g
    return y * (1.0 + sc) + sh


def _norm_proj_kernel(x_ref, sc_ref, sh_ref, g_ref, w_ref, *out_refs, defs):
    h = _modulated_norm(x_ref[...], g_ref[...], sc_ref[...], sh_ref[...]).astype(BF16)
    done = {}
    for (start, width, store_w, mult), o_ref in zip(defs, out_refs):
        if (start, width) not in done:
            done[(start, width)] = jnp.dot(h, w_ref[:, start:start + width], preferred_element_type=F32)
        r = done[(start, width)][:, :store_w]
        o_ref[...] = (r if mult is None else r * mult).astype(o_ref.dtype)


def _norm_proj_call(x2d, sc, sh, mod_map, g, w_bf, defs, dtypes, tm, name):
    n, d = x2d.shape
    assert n % tm == 0
    r = sc.shape[1]
    out_shapes = [jax.ShapeDtypeStruct((n, d_[2]), dt) for d_, dt in zip(defs, dtypes)]
    out_specs = [pl.BlockSpec((tm, d_[2]), lambda i: (i, 0)) for d_ in defs]
    return pl.pallas_call(
        functools.partial(_norm_proj_kernel, defs=tuple(defs)),
        grid=(n // tm,),
        in_specs=[pl.BlockSpec((tm, d), lambda i: (i, 0)),
                  pl.BlockSpec((None, r, d), mod_map),
                  pl.BlockSpec((None, r, d), mod_map),
                  pl.BlockSpec((1, d), lambda i: (0, 0)),
                  _resident(w_bf.shape, lambda i: (0, 0))],
        out_specs=out_specs,
        out_shape=out_shapes,
        compiler_params=_params(1),
        name=name,
    )(x2d, sc, sh, g.reshape(1, d), w_bf)


def _bias_tile_kernel(tab_ref, o_ref, *, qb):
    _, rows, width = o_ref.shape
    r = lax.broadcasted_iota(jnp.int32, (rows, width), 0)
    koff = lax.broadcasted_iota(jnp.int32, (rows, width), 1) - LANES
    rel = koff - r
    n = jnp.abs(rel)
    large = jnp.full((rows, width), _MAX_EXACT, jnp.int32)
    for th in _BUCKET_THS:
        large = large + jnp.where(n >= th, 1, 0)
    bucket = jnp.where(rel > 0, _NB, 0) + jnp.where(n < _MAX_EXACT, n, large)
    visible = ((koff >> 6) <= (r >> 6)) & (koff < qb)
    for h in range(N_ATTN_HEADS):
        b = jnp.zeros((rows, width), F32)
        for bk in range(N_REL_BUCKETS):
            b = jnp.where(bucket == bk, tab_ref[bk, h], b)
        o_ref[h] = jnp.where(visible, b, NEG_BIG)


def _bias_tile_call(rel_bias, qb):
    qbr = -(-qb // LANES) * LANES
    return pl.pallas_call(
        functools.partial(_bias_tile_kernel, qb=qb),
        in_specs=[pl.BlockSpec(memory_space=pltpu.SMEM)],
        out_specs=pl.BlockSpec(memory_space=pltpu.VMEM),
        out_shape=jax.ShapeDtypeStruct((N_ATTN_HEADS, qb, LANES + qbr), F32),
        name="bias_tiles",
    )(rel_bias)


def _sweep_far(q0, qb, kb, fn):
    far_end = jnp.maximum(q0 - LANES, 0)
    n_full = far_end // kb
    sub = kb // LANES

    def body(j, carry):
        fn(pl.multiple_of(j * kb, kb), j * sub, None)
        return carry

    lax.fori_loop(0, n_full, body, 0)

    @pl.when(far_end > n_full * kb)
    def _():
        start = pl.multiple_of(n_full * kb, kb)
        col = start + lax.broadcasted_iota(jnp.int32, (qb, LANES), 1)
        fn(start, n_full * sub, [jnp.where(col + c * LANES < far_end, 0.0, NEG_BIG) for c in range(sub)])


def _key_rows(ref, start, width):
    blk = ref[pl.ds(start, width), :].astype(BF16)
    pad = -blk.shape[1] % LANES
    return blk if pad == 0 else jnp.concatenate([blk, jnp.zeros((width, pad), BF16)], axis=1)


def _dsa_kernel(tab_ref, q_ref, qi_ref, iw_ref, k_ref, v_ref, ki_ref, kc_ref, vc_ref, kic_ref, tile_ref, o_ref,
                keys_scr, mfar_scr, mnear_scr, l_scr, acc_scr, *, qb, kb, qoff, topk):
    i = pl.program_id(1)
    q0 = qoff + i * qb
    qbr = tile_ref.shape[2] - LANES
    sub = kb // LANES
    w = iw_ref[:, :IDX_HEADS] * (IDX_HEADS ** -0.5 * IDX_DIM ** -0.5)

    def score_keys(kib):
        width = kib.shape[0]
        acc = jnp.zeros((qb, width), F32)
        for h in range(IDX_HEADS):
            d = lax.dot_general(qi_ref[:, h * LANES:(h + 1) * LANES], kib, NT_DIMS,
                                preferred_element_type=F32)
            acc = acc + w[:, h:h + 1] * jnp.maximum(d, 0.0)
        bits = pltpu.bitcast(acc, jnp.int32)
        return bits ^ ((bits >> 31) & 0x7FFFFFFF)

    n_score = (q0 + kb - 1) // kb

    def score_body(j, carry):
        key = score_keys(_key_rows(ki_ref, pl.multiple_of(j * kb, kb), kb))
        for c in range(sub):
            keys_scr[j * sub + c] = key[:, c * LANES:(c + 1) * LANES]
        return carry

    lax.fori_loop(0, n_score, score_body, 0)

    cur_blk = q0 // LANES
    key = score_keys(kic_ref[...])
    key = jnp.where(tile_ref[0, :, LANES:] > 0.5 * NEG_BIG, key, INT_MIN)
    for c in range(qbr // LANES):
        keys_scr[cur_blk + c] = key[:, c * LANES:(c + 1) * LANES]
    n_blk = cur_blk + qbr // LANES

    def count_ge(t):
        tb = jnp.broadcast_to(t, (qb, LANES))
        n4 = n_blk // 4

        def body4(j, cnt):
            for u in range(4):
                cnt = cnt + jnp.where(keys_scr[4 * j + u] >= tb, 1, 0)
            return cnt

        def body1(j, cnt):
            return cnt + jnp.where(keys_scr[4 * n4 + j] >= tb, 1, 0)

        cnt = lax.fori_loop(0, n4, body4, jnp.zeros((qb, LANES), jnp.int32))
        cnt = lax.fori_loop(0, n_blk - 4 * n4, body1, cnt)
        return jnp.sum(cnt, axis=-1, keepdims=True)

    def bit_body(it, t):
        cand = t + (jnp.int32(1) << (31 - it))
        return jnp.where(count_ge(cand) >= topk, cand, t)

    thr = lax.fori_loop(0, 32, bit_body, jnp.full((qb, 1), INT_MIN, jnp.int32))
    thr = jnp.maximum(thr, INT_MIN + 1)
    thr_b = jnp.broadcast_to(thr, (qb, LANES))

    hd = A_HEAD_DIM

    def select_mask(blk0, n):
        return [jnp.where(keys_scr[blk0 + c] >= thr_b, 0.0, NEG_BIG) for c in range(n)]

    def logits(h, kblk):
        g = h // A_REP
        s = lax.dot_general(q_ref[:, h * hd:(h + 1) * hd], kblk[:, g * hd:(g + 1) * hd], NT_DIMS,
                            preferred_element_type=F32)
        return [s[:, c * LANES:(c + 1) * LANES] for c in range(s.shape[1] // LANES)]

    def tile_cols(h, col0, n):
        return [tile_ref[h, :, col0 + c * LANES:col0 + (c + 1) * LANES] for c in range(n)]

    mfar_scr[...] = jnp.full(mfar_scr.shape, NEG_BIG, F32)
    mnear_scr[...] = jnp.full(mnear_scr.shape, NEG_BIG, F32)

    def far_mask(blk0, colmask):
        am = select_mask(blk0, sub)
        return am if colmask is None else [a + c for a, c in zip(am, colmask)]

    def max_far(start, blk0, colmask):
        kblk = _key_rows(k_ref, start, kb)
        am = far_mask(blk0, colmask)
        for h in range(A_HEADS):
            mfar_scr[h] = functools.reduce(
                jnp.maximum, [s + a for s, a in zip(logits(h, kblk), am)], mfar_scr[h])

    def max_near(kblk, blk0, col0):
        n = kblk.shape[0] // LANES
        am = select_mask(blk0, n)
        for h in range(A_HEADS):
            mnear_scr[h] = functools.reduce(
                jnp.maximum, [s + a + b for s, a, b in zip(logits(h, kblk), am, tile_cols(h, col0, n))],
                mnear_scr[h])

    prev_start = pl.multiple_of(q0 - LANES, LANES)
    _sweep_far(q0, qb, kb, max_far)
    pl.when(q0 > 0)(lambda: max_near(_key_rows(k_ref, prev_start, LANES), cur_blk - 1, 0))
    max_near(kc_ref[...], cur_blk, LANES)

    for h in range(A_HEADS):
        fb = tab_ref[FAR_BUCKET, h]
        m = jnp.maximum(jnp.max(mfar_scr[h], axis=-1, keepdims=True) + fb,
                        jnp.max(mnear_scr[h], axis=-1, keepdims=True))
        mfar_scr[h] = jnp.broadcast_to(fb - m, (qb, LANES))
        mnear_scr[h] = jnp.broadcast_to(-m, (qb, LANES))
    l_scr[...] = jnp.zeros(l_scr.shape, F32)
    acc_scr[...] = jnp.zeros(acc_scr.shape, F32)

    def accumulate(h, p, vblk):
        g = h // A_REP
        l_scr[h] += functools.reduce(jnp.add, p)
        acc_scr[h] += jnp.dot(jnp.concatenate(p, axis=1).astype(BF16), vblk[:, g * hd:(g + 1) * hd],
                              preferred_element_type=F32)

    def acc_far(start, blk0, colmask):
        kblk = _key_rows(k_ref, start, kb)
        vblk = _key_rows(v_ref, start, kb)
        am = far_mask(blk0, colmask)
        for h in range(A_HEADS):
            shift = mfar_scr[h]
            accumulate(h, [jnp.exp(s + a + shift) for s, a in zip(logits(h, kblk), am)], vblk)

    def acc_near(kblk, vblk, blk0, col0):
        n = kblk.shape[0] // LANES
        am = select_mask(blk0, n)
        for h in range(A_HEADS):
            shift = mnear_scr[h]
            accumulate(h, [jnp.exp(s + a + b + shift) for s, a, b in
                           zip(logits(h, kblk), am, tile_cols(h, col0, n))], vblk)

    _sweep_far(q0, qb, kb, acc_far)
    pl.when(q0 > 0)(lambda: acc_near(_key_rows(k_ref, prev_start, LANES), _key_rows(v_ref, prev_start, LANES),
                                     cur_blk - 1, 0))
    acc_near(kc_ref[...], vc_ref[...], cur_blk, LANES)

    for h in range(A_HEADS):
        o_ref[:, h * hd:(h + 1) * hd] = (
            acc_scr[h] / jnp.sum(l_scr[h], axis=-1, keepdims=True)).astype(o_ref.dtype)


def _dsa_call(rel_bias, tiles, q, qi, iw, far, cur, *, qb, kb, qoff, topk):
    b, t, _ = q.shape
    lf = far[0].shape[1]
    qbr = tiles.shape[2] - LANES
    q0_max = qoff + t - qb
    far_end = -(-q0_max // kb) * kb
    assert t % qb == 0 and qoff % LANES == 0 and far_end <= lf and cur[0].shape[1] == (t // qb) * qbr
    n_slabs = max(far_end, q0_max + qbr) // LANES
    far_spec = lambda a: pl.BlockSpec((None, lf, a.shape[2]), lambda bi, i: (bi, 0, 0))
    cur_spec = lambda a: pl.BlockSpec((None, qbr, a.shape[2]), lambda bi, i: (bi, i, 0))
    return pl.pallas_call(
        functools.partial(_dsa_kernel, qb=qb, kb=kb, qoff=qoff, topk=topk),
        grid=(b, t // qb),
        in_specs=[pl.BlockSpec(memory_space=pltpu.SMEM),
                  pl.BlockSpec((None, qb, q.shape[2]), lambda bi, i: (bi, i, 0)),
                  pl.BlockSpec((None, qb, qi.shape[2]), lambda bi, i: (bi, i, 0)),
                  pl.BlockSpec((None, qb, iw.shape[2]), lambda bi, i: (bi, i, 0)),
                  far_spec(far[0]), far_spec(far[1]), far_spec(far[2]),
                  cur_spec(cur[0]), cur_spec(cur[1]), cur_spec(cur[2]),
                  _resident(tiles.shape, lambda bi, i: (0, 0, 0))],
        out_specs=pl.BlockSpec((None, qb, A_HEADS * A_HEAD_DIM), lambda bi, i: (bi, i, 0)),
        out_shape=jax.ShapeDtypeStruct((b, t, A_HEADS * A_HEAD_DIM), BF16),
        scratch_shapes=[pltpu.VMEM((n_slabs, qb, LANES), jnp.int32),
                        pltpu.VMEM((A_HEADS, qb, LANES), F32),
                        pltpu.VMEM((A_HEADS, qb, LANES), F32),
                        pltpu.VMEM((A_HEADS, qb, LANES), F32),
                        pltpu.VMEM((A_HEADS, qb, A_HEAD_DIM), F32)],
        compiler_params=_params(2),
        name="mixer_a",
    )(rel_bias, q, qi, iw, *far, *cur, tiles)


def _diff_kernel(tab_ref, lam_ref, sg_ref, q_ref, k_ref, v_ref, kc_ref, vc_ref, tile_ref, o_ref,
                 mfar_scr, mnear_scr, l_scr, acc_scr, *, qb, kb, qoff, lam_init):
    head = pl.program_id(1)
    i = pl.program_id(2)
    q0 = qoff + i * qb
    qbr = tile_ref.shape[2] - LANES
    hd = B_HEAD_DIM
    far_bias = tab_ref[FAR_BUCKET, A_HEADS + head]

    def logits(c, kblk):
        s = lax.dot_general(q_ref[:, c * hd:(c + 1) * hd], kblk[:, c * hd:(c + 1) * hd], NT_DIMS,
                            preferred_element_type=F32)
        return [s[:, u * LANES:(u + 1) * LANES] for u in range(s.shape[1] // LANES)]

    def tile_cols(col0, n):
        return [tile_ref[0, :, col0 + u * LANES:col0 + (u + 1) * LANES] for u in range(n)]

    mfar_scr[...] = jnp.full(mfar_scr.shape, NEG_BIG, F32)
    mnear_scr[...] = jnp.full(mnear_scr.shape, NEG_BIG, F32)

    def max_far(start, blk0, colmask):
        kblk = _key_rows(k_ref, start, kb)
        for c in range(2):
            s = logits(c, kblk)
            if colmask is not None:
                s = [x + m for x, m in zip(s, colmask)]
            mfar_scr[c] = functools.reduce(jnp.maximum, s, mfar_scr[c])

    def max_near(kblk, col0):
        for c in range(2):
            mnear_scr[c] = functools.reduce(
                jnp.maximum, [s + b for s, b in zip(logits(c, kblk), tile_cols(col0, kblk.shape[0] // LANES))],
                mnear_scr[c])

    prev_start = pl.multiple_of(q0 - LANES, LANES)
    _sweep_far(q0, qb, kb, max_far)
    pl.when(q0 > 0)(lambda: max_near(_key_rows(k_ref, prev_start, LANES), 0))
    max_near(kc_ref[...], LANES)

    for c in range(2):
        m = jnp.maximum(jnp.max(mfar_scr[c], axis=-1, keepdims=True) + far_bias,
                        jnp.max(mnear_scr[c], axis=-1, keepdims=True))
        mfar_scr[c] = jnp.broadcast_to(far_bias - m, (qb, LANES))
        mnear_scr[c] = jnp.broadcast_to(-m, (qb, LANES))
    l_scr[...] = jnp.zeros(l_scr.shape, F32)
    acc_scr[...] = jnp.zeros(acc_scr.shape, F32)

    def accumulate(c, p, vblk):
        l_scr[c] += functools.reduce(jnp.add, p)
        acc_scr[c] += jnp.dot(jnp.concatenate(p, axis=1).astype(BF16), vblk, preferred_element_type=F32)

    def acc_far(start, blk0, colmask):
        kblk = _key_rows(k_ref, start, kb)
        vblk = _key_rows(v_ref, start, kb)
        for c in range(2):
            shift = mfar_scr[c]
            shifts = [shift] * (kb // LANES) if colmask is None else [shift + m for m in colmask]
            accumulate(c, [jnp.exp(s + sh) for s, sh in zip(logits(c, kblk), shifts)], vblk)

    def acc_near(kblk, vblk, col0):
        for c in range(2):
            shift = mnear_scr[c]
            accumulate(c, [jnp.exp(s + b + shift) for s, b in
                           zip(logits(c, kblk), tile_cols(col0, kblk.shape[0] // LANES))], vblk)

    _sweep_far(q0, qb, kb, acc_far)
    pl.when(q0 > 0)(lambda: acc_near(_key_rows(k_ref, prev_start, LANES), _key_rows(v_ref, prev_start, LANES), 0))
    acc_near(kc_ref[...], vc_ref[...], LANES)

    lam = lam_ref[0, 0]
    o = (acc_scr[0] / jnp.sum(l_scr[0], axis=-1, keepdims=True)
         - lam * (acc_scr[1] / jnp.sum(l_scr[1], axis=-1, keepdims=True)))
    o = o * lax.rsqrt(jnp.mean(o * o, axis=-1, keepdims=True) + EPS) * sg_ref[...]
    o_ref[...] = (o * (1.0 - lam_init)).astype(o_ref.dtype)


def _diff_call(rel_bias, tiles, lam, subln_g, q, far, cur, *, qb, kb, qoff, lam_init):
    b, t, _ = q.shape
    lf = far[0].shape[1]
    hw = 2 * B_HEAD_DIM
    qbr = tiles.shape[2] - LANES
    q0_max = qoff + t - qb
    far_end = -(-max(q0_max - LANES, 0) // kb) * kb
    assert t % qb == 0 and qoff % LANES == 0 and far_end <= lf and cur[0].shape[1] == (t // qb) * qbr
    far_spec = pl.BlockSpec((None, lf, hw), lambda bi, h, i: (bi, 0, h))
    cur_spec = pl.BlockSpec((None, qbr, hw), lambda bi, h, i: (bi, i, h))
    return pl.pallas_call(
        functools.partial(_diff_kernel, qb=qb, kb=kb, qoff=qoff, lam_init=lam_init),
        grid=(b, B_HEADS, t // qb),
        in_specs=[pl.BlockSpec(memory_space=pltpu.SMEM),
                  pl.BlockSpec(memory_space=pltpu.SMEM),
                  pl.BlockSpec((1, hw), lambda bi, h, i: (0, 0)),
                  pl.BlockSpec((None, qb, hw), lambda bi, h, i: (bi, i, h)),
                  far_spec, far_spec, cur_spec, cur_spec,
                  pl.BlockSpec((1, qb, tiles.shape[2]), lambda bi, h, i: (A_HEADS + h, 0, 0))],
        out_specs=pl.BlockSpec((None, qb, hw), lambda bi, h, i: (bi, i, h)),
        out_shape=jax.ShapeDtypeStruct((b, t, B_HEADS * hw), BF16),
        scratch_shapes=[pltpu.VMEM((2, qb, LANES), F32),
                        pltpu.VMEM((2, qb, LANES), F32),
                        pltpu.VMEM((2, qb, LANES), F32),
                        pltpu.VMEM((2, qb, hw), F32)],
        compiler_params=_params(3),
        name="mixer_b",
    )(rel_bias, lam, subln_g.reshape(1, hw), q, *far, *cur, tiles)


def _lam_kernel(dl_ref, o_ref, *, lam_init):
    dl = dl_ref[...]
    s1 = jnp.sum(dl[0:1] * dl[1:2], axis=-1, keepdims=True)
    s2 = jnp.sum(dl[2:3] * dl[3:4], axis=-1, keepdims=True)
    o_ref[...] = jnp.exp(s1) - jnp.exp(s2) + lam_init


def _lam_call(diff_lam, lam_init):
    return pl.pallas_call(
        functools.partial(_lam_kernel, lam_init=lam_init),
        out_shape=jax.ShapeDtypeStruct((1, 1), F32),
        name="diff_lambda",
    )(diff_lam)


def _route(logits_t, rbias, counts):
    tm = logits_t.shape[1]
    per = N_EXPERTS // N_GROUPS
    scores = jax.nn.sigmoid(logits_t).reshape(N_GROUPS, per, tm)
    ch = scores + rbias.reshape(N_GROUPS, per, 1)
    sub = lax.broadcasted_iota(jnp.int32, ch.shape, 1)
    grp = lax.broadcasted_iota(jnp.int32, ch.shape, 0)
    m1 = jnp.max(ch, axis=1, keepdims=True)
    i1 = jnp.min(jnp.where(ch == m1, sub, per), axis=1, keepdims=True)
    m2 = jnp.max(jnp.where(sub == i1, -jnp.inf, ch), axis=1, keepdims=True)
    gs = jnp.broadcast_to(m1 + m2, ch.shape)
    rank = jnp.zeros(ch.shape, jnp.int32)
    for g2 in range(N_GROUPS):
        o = gs[g2:g2 + 1]
        rank = rank + jnp.where(o > gs, 1, 0) + jnp.where(o == gs, 1, 0) * jnp.where(grp > g2, 1, 0)
    cm = jnp.where(rank < TOPK_GROUPS, ch, -jnp.inf)
    eid = grp * per + sub
    erank = jnp.zeros(cm.shape, jnp.int32)
    for e2 in range(N_EXPERTS):
        o = cm[e2 // per:e2 // per + 1, e2 % per:e2 % per + 1, :]
        erank = erank + jnp.where(o > cm, 1, 0) + jnp.where(o == cm, 1, 0) * jnp.where(eid > e2, 1, 0)
    chosen = erank < TOP_K
    sel = jnp.where(chosen, scores, 0.0)
    denom = jnp.sum(jnp.sum(sel, axis=1, keepdims=True), axis=0, keepdims=True)
    gates = sel / denom * ROUTED_SCALE

    ones = jnp.where(chosen, 1.0, 0.0).reshape(N_EXPERTS, tm)
    earlier = (lax.broadcasted_iota(jnp.int32, (tm, tm), 0) < lax.broadcasted_iota(jnp.int32, (tm, tm), 1))
    before = jnp.dot(ones.astype(BF16), jnp.where(earlier, 1.0, 0.0).astype(BF16), preferred_element_type=F32)
    pos = (before + counts.astype(F32)).reshape(N_GROUPS, per, tm)
    new_counts = counts + jnp.sum(ones, axis=1, keepdims=True).astype(jnp.int32)

    def per_slot(values):
        rows = [jnp.sum(jnp.sum(jnp.where(erank == k, values, 0.0), axis=1, keepdims=True), axis=0)
                for k in range(TOP_K)]
        return jnp.concatenate(rows, axis=0)

    return (per_slot(eid.astype(F32)).astype(jnp.int32), per_slot(pos).astype(jnp.int32), per_slot(gates),
            new_counts)


def _pack_pair(a, b):
    ua = pltpu.bitcast(a.astype(BF16).astype(F32), jnp.uint32)
    ub = pltpu.bitcast(b.astype(BF16).astype(F32), jnp.uint32)
    return pltpu.bitcast(ua | (ub >> 16), jnp.int32)


def _unpack_pair(p):
    u = pltpu.bitcast(p, jnp.uint32)
    a = pltpu.bitcast(u & jnp.uint32(0xFFFF0000), F32)
    b = pltpu.bitcast(u << 16, F32)
    return a.astype(BF16), b.astype(BF16)


PARTS = 4
PART_W = D_MODEL // 2 // PARTS


def _pack_rows(x):
    half = x.shape[1] // 2
    return jnp.stack([_pack_pair(x[:, j * PART_W:(j + 1) * PART_W],
                                 x[:, half + j * PART_W:half + (j + 1) * PART_W]) for j in range(PARTS)])


def _packed_matmul(parts_ref, w_ref):
    half = w_ref.shape[0] // 2
    acc = None
    for j in range(PARTS):
        a, b = _unpack_pair(parts_ref[j])
        t = (jnp.dot(a, w_ref[j * PART_W:(j + 1) * PART_W, :], preferred_element_type=F32)
             + jnp.dot(b, w_ref[half + j * PART_W:half + (j + 1) * PART_W, :], preferred_element_type=F32))
        acc = t if acc is None else acc + t
    return acc


def _outproj_kernel(x_ref, oa_ref, ob_ref, ga_ref, sc_ref, sh_ref, g_ref, w_ref, wr_ref, rb_ref, cin_ref,
                    x1_ref, h2_ref, ek_ref, pk_ref, gk_ref, cout_ref, cnt_scr):
    @pl.when(pl.program_id(0) == 0)
    def _():
        cnt_scr[...] = cin_ref[...]

    half = oa_ref.shape[1]
    mix = (jnp.dot(oa_ref[...], w_ref[:half, :], preferred_element_type=F32)
           + jnp.dot(ob_ref[...], w_ref[half:, :], preferred_element_type=F32))
    x1 = x_ref[...] + ga_ref[...] * mix
    x1_ref[...] = x1
    h2 = _modulated_norm(x1, g_ref[...], sc_ref[...], sh_ref[...])
    h2_ref[...] = _pack_rows(h2)
    logits_t = lax.dot_general(wr_ref[...], h2, NT_DIMS, preferred_element_type=F32,
                               precision=lax.Precision.HIGHEST)
    ek, pk, gk, counts = _route(logits_t, rb_ref[...], cnt_scr[...])
    ek_ref[...] = ek
    pk_ref[...] = pk
    gk_ref[...] = gk
    cnt_scr[...] = counts
    cout_ref[...] = counts


def _outproj_call(x2d, oa, ob, ga, sc, sh, mod_map, g, w_bf, wr_t, rbias, counts_in, tm):
    n, d = x2d.shape
    assert n % tm == 0
    r = sc.shape[1]
    mod_spec = pl.BlockSpec((None, r, d), mod_map)
    slot_spec = pl.BlockSpec((TOP_K, tm), lambda i: (0, i))
    return pl.pallas_call(
        _outproj_kernel,
        grid=(n // tm,),
        in_specs=[pl.BlockSpec((tm, d), lambda i: (i, 0)),
                  pl.BlockSpec((tm, oa.shape[1]), lambda i: (i, 0)),
                  pl.BlockSpec((tm, ob.shape[1]), lambda i: (i, 0)),
                  mod_spec, mod_spec, mod_spec,
                  pl.BlockSpec((1, d), lambda i: (0, 0)),
                  _resident(w_bf.shape, lambda i: (0, 0)),
                  _resident(wr_t.shape, lambda i: (0, 0)),
                  pl.BlockSpec((N_EXPERTS, 1), lambda i: (0, 0)),
                  pl.BlockSpec((N_EXPERTS, 1), lambda i: (0, 0))],
        out_specs=[pl.BlockSpec((tm, d), lambda i: (i, 0)),
                   pl.BlockSpec((PARTS, tm, PART_W), lambda i: (0, i, 0)),
                   slot_spec, slot_spec, slot_spec,
                   pl.BlockSpec((N_EXPERTS, 1), lambda i: (0, 0))],
        out_shape=[jax.ShapeDtypeStruct((n, d), F32),
                   jax.ShapeDtypeStruct((PARTS, n, PART_W), jnp.int32),
                   jax.ShapeDtypeStruct((TOP_K, n), jnp.int32),
                   jax.ShapeDtypeStruct((TOP_K, n), jnp.int32),
                   jax.ShapeDtypeStruct((TOP_K, n), F32),
                   jax.ShapeDtypeStruct((N_EXPERTS, 1), jnp.int32)],
        scratch_shapes=[pltpu.VMEM((N_EXPERTS, 1), jnp.int32)],
        compiler_params=_params(1),
        name="out_proj_router",
    )(x2d, oa, ob, ga, sc, sh, g.reshape(1, d), w_bf, wr_t, rbias, counts_in)


SC_WINDOW = 128
EXPERT_TILE = 512


def _sc_mesh():
    return plsc.VectorSubcoreMesh(core_axis_name="core", subcore_axis_name="subcore")


def _sc_scatter_rows(x, idx, n_out):
    n, w = x.shape
    steps = n // SC_WINDOW
    total = idx.shape[1] // SC_WINDOW

    @pl.kernel(out_type=jax.ShapeDtypeStruct((n_out, w), x.dtype), mesh=_sc_mesh(), scratch_types=[],
               name="moe_dispatch")
    def scatter(x_hbm, i_hbm, o_hbm):
        def body(x_vmem, i_vmem):
            pltpu.sync_copy(x_vmem, o_hbm.at[i_vmem.at[0]])

        pltpu.emit_pipeline(
            body, grid=(total,),
            in_specs=[pl.BlockSpec((SC_WINDOW, w), lambda i: (i % steps, 0)),
                      pl.BlockSpec((1, SC_WINDOW), lambda i: (0, i))],
            out_specs=[], core_axis_name=("core", "subcore"),
            dimension_semantics=(pltpu.PARALLEL,))(x_hbm, i_hbm)

    return scatter(x, idx)


def _sc_gather_rows(y, idx):
    w = y.shape[1]
    n = idx.shape[1]

    @pl.kernel(out_type=jax.ShapeDtypeStruct((n, w), y.dtype), mesh=_sc_mesh(), scratch_types=[],
               name="moe_collect")
    def gather(y_hbm, i_hbm, o_hbm):
        def body(i_vmem, o_vmem):
            pltpu.sync_copy(y_hbm.at[i_vmem.at[0]], o_vmem)

        pltpu.emit_pipeline(
            body, grid=(n // SC_WINDOW,),
            in_specs=[pl.BlockSpec((1, SC_WINDOW), lambda i: (0, i))],
            out_specs=[pl.BlockSpec((SC_WINDOW, w), lambda i: (i, 0))],
            core_axis_name=("core", "subcore"),
            dimension_semantics=(pltpu.PARALLEL,))(i_hbm, o_hbm)

    return gather(y, idx)


def _swiglu(parts_ref, wg_ref, wu_ref, wd_ref):
    a = _packed_matmul(parts_ref, wg_ref)
    u = _packed_matmul(parts_ref, wu_ref)
    hid = (a * jax.nn.sigmoid(a) * u).astype(BF16)
    return jnp.dot(hid, wd_ref[...], preferred_element_type=F32)


def _expert_kernel(te_ref, nu_ref, x_ref, wg_ref, wu_ref, wd_ref, y_ref, wg_scr, wu_scr, wd_scr):
    i = pl.program_id(0)

    @pl.when(i < nu_ref[0])
    def _():
        @pl.when((i == 0) | (te_ref[i] != te_ref[jnp.maximum(i - 1, 0)]))
        def _():
            wg_scr[...] = wg_ref[...].astype(BF16)
            wu_scr[...] = wu_ref[...].astype(BF16)
            wd_scr[...] = wd_ref[...].astype(BF16)

        y_ref[...] = _pack_rows(_swiglu(x_ref, wg_scr, wu_scr, wd_scr))


def _expert_call(tile_expert, n_used, x_sorted, wg, wu, wd):
    _, r, _ = x_sorted.shape
    ne, d, de = wg.shape
    n_tiles = r // EXPERT_TILE
    row_map = lambda i, te, nu: (0, jnp.minimum(i, nu[0] - 1), 0)
    return pl.pallas_call(
        _expert_kernel,
        grid_spec=pltpu.PrefetchScalarGridSpec(
            num_scalar_prefetch=2,
            grid=(n_tiles,),
            in_specs=[pl.BlockSpec((PARTS, EXPERT_TILE, PART_W), row_map),
                      pl.BlockSpec((None, d, de), lambda i, te, nu: (te[i], 0, 0)),
                      pl.BlockSpec((None, d, de), lambda i, te, nu: (te[i], 0, 0)),
                      pl.BlockSpec((None, de, d), lambda i, te, nu: (te[i], 0, 0))],
            out_specs=pl.BlockSpec((PARTS, EXPERT_TILE, PART_W), row_map),
            scratch_shapes=[pltpu.VMEM((d, de), BF16), pltpu.VMEM((d, de), BF16), pltpu.VMEM((de, d), BF16)]),
        out_shape=jax.ShapeDtypeStruct(x_sorted.shape, jnp.int32),
        compiler_params=_params(1),
        name="moe_experts",
    )(tile_expert, n_used, x_sorted, wg, wu, wd)


def _final_kernel(x_ref, h_ref, rows_ref, gk_ref, gf_ref, g_ref, sg_ref, su_ref, sd_ref, o_ref):
    moe = _swiglu(h_ref, sg_ref, su_ref, sd_ref)
    half = moe.shape[1] // 2
    lo = [moe[:, j * PART_W:(j + 1) * PART_W] for j in range(PARTS)]
    hi = [moe[:, half + j * PART_W:half + (j + 1) * PART_W] for j in range(PARTS)]
    for k in range(TOP_K):
        gate = gk_ref[:, k:k + 1]
        for j in range(PARTS):
            a, b = _unpack_pair(rows_ref[k, j])
            lo[j] = lo[j] + gate * a.astype(F32)
            hi[j] = hi[j] + gate * b.astype(F32)
    x = x_ref[...] + gf_ref[...] * jnp.concatenate(lo + hi, axis=1)
    o_ref[...] = x * lax.rsqrt(jnp.mean(x * x, axis=-1, keepdims=True) + EPS) * g_ref[...]


def _final_call(x1, h2p, rows, row_block0, gk_t, gf, mod_map, g, sg, su, sd, tm):
    n, d = x1.shape
    assert n % tm == 0
    r = gf.shape[1]
    return pl.pallas_call(
        _final_kernel,
        grid=(n // tm,),
        in_specs=[pl.BlockSpec((tm, d), lambda i: (i, 0)),
                  pl.BlockSpec((PARTS, tm, PART_W), lambda i: (0, i, 0)),
                  pl.BlockSpec((TOP_K, PARTS, tm, PART_W), lambda i: (0, 0, i + row_block0, 0)),
                  pl.BlockSpec((tm, TOP_K), lambda i: (i, 0)),
                  pl.BlockSpec((None, r, d), mod_map),
                  pl.BlockSpec((1, d), lambda i: (0, 0)),
                  _resident(sg.shape, lambda i: (0, 0)),
                  _resident(su.shape, lambda i: (0, 0)),
                  _resident(sd.shape, lambda i: (0, 0))],
        out_specs=pl.BlockSpec((tm, d), lambda i: (i, 0)),
        out_shape=jax.ShapeDtypeStruct((n, d), F32),
        compiler_params=_params(1),
        name="final_norm",
    )(x1, h2p, rows, gk_t, gf, g.reshape(1, d), sg, su, sd)


_QA, _KA, _VA, _QI, _KI, _IW, _QB, _KB, _VB = range(9)
_IN_SIZES = (A_HEADS * A_HEAD_DIM, A_KV_HEADS * A_HEAD_DIM, A_KV_HEADS * A_HEAD_DIM,
             IDX_HEADS * IDX_DIM, IDX_DIM, IDX_HEADS,
             B_HEADS * 2 * B_HEAD_DIM, B_HEADS * 2 * B_HEAD_DIM, B_HEADS * 2 * B_HEAD_DIM)


def _split_w_in(w_in):
    d = w_in.shape[0]
    offs = np.concatenate([[0], np.cumsum(_IN_SIZES)])
    cols = [w_in[:, offs[k]:offs[k + 1]].astype(BF16) for k in range(9)]
    zero = lambda n: jnp.zeros((d, n), BF16)
    qi = jnp.concatenate([cols[_QI].reshape(d, IDX_HEADS, IDX_DIM),
                          jnp.zeros((d, IDX_HEADS, LANES - IDX_DIM), BF16)], axis=-1).reshape(d, IDX_HEADS * LANES)
    w_a = jnp.concatenate([cols[_QA], cols[_KA], cols[_VA], qi,
                           cols[_KI], zero(LANES - IDX_DIM), cols[_IW], zero(LANES - IDX_HEADS)], axis=1)
    w_b = jnp.concatenate([cols[_QB], cols[_KB], cols[_VB]], axis=1)
    return w_a, w_b


def _group_forward(x, mod, past, rel_bias, lam, weights, *, lam_init, qb, kb, tm, per_row_mod):
    (norm_a_g, w_a, w_b, subln_g) = weights
    b, t, d = x.shape
    n = b * t
    x2d = x.reshape(n, d)
    sh_a, sc_a = mod[0], mod[1]
    if per_row_mod:
        expand = lambda m: jnp.repeat(m, t, axis=0).reshape(1, n, d)
        mod_map = lambda i: (0, i, 0)
    else:
        expand = lambda m: m.reshape(b, 1, d)
        mod_map = lambda i: ((i * tm) // t, 0, 0)
    sc3, sh3 = expand(sc_a), expand(sh_a)

    na = A_HEADS * A_HEAD_DIM
    nk = A_KV_HEADS * A_HEAD_DIM
    nq = IDX_HEADS * LANES
    c_ka, c_va, c_qi = na, na + nk, na + 2 * nk
    c_ki, c_iw = c_qi + nq, c_qi + nq + LANES
    defs_a = [(0, na, na, A_HEAD_DIM ** -0.5), (c_ka, nk, nk, None), (c_ka, nk, nk, None),
              (c_va, nk, nk, None), (c_va, nk, nk, None), (c_qi, nq, nq, None),
              (c_ki, LANES, IDX_DIM, None), (c_ki, LANES, LANES, None), (c_iw, LANES, LANES, None)]
    dt_a = [BF16, F32, BF16, F32, BF16, BF16, F32, BF16, F32]
    qa, ka, ka_bf, va, va_bf, qi, ki, ki_bf, iw = _norm_proj_call(
        x2d, sc3, sh3, mod_map, norm_a_g, w_a, defs_a, dt_a, tm, "in_proj_a")
    nb = B_HEADS * 2 * B_HEAD_DIM
    defs_b = [(0, nb, nb, B_HEAD_DIM ** -0.5), (nb, nb, nb, None), (nb, nb, nb, None),
              (2 * nb, nb, nb, None), (2 * nb, nb, nb, None)]
    dt_b = [BF16, F32, BF16, F32, BF16]
    qbm, kbm, kb_bf, vbm, vb_bf = _norm_proj_call(
        x2d, sc3, sh3, mod_map, norm_a_g, w_b, defs_b, dt_b, tm, "in_proj_b")

    r3 = lambda a: a.reshape(b, t, a.shape[-1])
    cur = [r3(ka_bf), r3(va_bf), r3(ki_bf), r3(kb_bf), r3(vb_bf)]
    if past is None:
        qoff = 0
        far = cur
    else:
        assert t == qb, "a group with cached keys is one query block per batch row"
        qoff = past[0].shape[1]
        qbr = -(-qb // LANES) * LANES
        cur = [jnp.concatenate([a, jnp.zeros((b, qbr - t, a.shape[-1]), BF16)], axis=1) for a in cur]
        far = [p.reshape(b, qoff, -1) for p in past]
        short = -qoff % kb
        if short:
            far = [jnp.concatenate([p, jnp.zeros((b, short, p.shape[-1]), p.dtype)], axis=1) for p in far]
    length = qoff + t
    topk = min(TOPK_MAX, length // 4)

    tiles = _bias_tile_call(rel_bias, qb)
    out_a = _dsa_call(rel_bias, tiles, r3(qa), r3(qi), r3(iw), far[:3], cur[:3], qb=qb, kb=kb, qoff=qoff, topk=topk)
    out_b = _diff_call(rel_bias, tiles, lam, subln_g, r3(qbm), far[3:], cur[3:], qb=qb, kb=kb, qoff=qoff,
                       lam_init=lam_init)
    rows = (ka.reshape(1, b, t, A_KV_HEADS, A_HEAD_DIM), va.reshape(1, b, t, A_KV_HEADS, A_HEAD_DIM),
            ki.reshape(1, b, t, IDX_DIM), kbm.reshape(1, b, t, B_HEADS, 2, B_HEAD_DIM),
            vbm.reshape(1, b, t, B_HEADS, 2 * B_HEAD_DIM))
    return x2d, out_a.reshape(n, -1), out_b.reshape(n, -1), rows, expand, mod_map


def kernel(x_prompt, x_sample, c_prompt, c_sample, cache_a_k, cache_a_v, cache_a_kidx, cache_b_k, cache_b_v,
           rel_bias, w_ada, b_ada, norm_a_g, w_in, w_out, diff_lam, subln_g, norm_f_g, w_router, router_bias,
           w_gate, w_up, w_down, ws_gate, ws_up, ws_down, final_g):
    assert w_ada.shape[0] == 1, "single-layer model"
    d = x_prompt.shape[-1]
    bp, tp, _ = x_prompt.shape
    bs, ts, _ = x_sample.shape
    lam_init = 0.8 - 0.6 * math.exp(-0.3 * 0)

    c_all = jnp.concatenate([c_prompt, c_sample], axis=0)
    rows = -(-c_all.shape[0] // 8) * 8
    c_pad = jnp.concatenate([c_all, jnp.zeros((rows - c_all.shape[0], d), F32)], axis=0)
    mod_all = _ada_call(c_pad, w_ada[0], b_ada[0])
    mod_p = [mod_all[:bp, k * d:(k + 1) * d] for k in range(6)]
    mod_s = [mod_all[bp:bp + bs, k * d:(k + 1) * d] for k in range(6)]

    w_a, w_b = _split_w_in(w_in[0])
    lam = _lam_call(diff_lam[0], lam_init)
    weights = (norm_a_g[0], w_a, w_b, subln_g[0])
    past = (cache_a_k[0], cache_a_v[0], cache_a_kidx[0], cache_b_k[0], cache_b_v[0])

    tm_p = 256
    tm_s = bs * ts
    xp2, oa_p, ob_p, rows_p, exp_p, map_p = _group_forward(
        x_prompt, mod_p, None, rel_bias, lam, weights, lam_init=lam_init, qb=256, kb=512, tm=tm_p,
        per_row_mod=False)
    xs2, oa_s, ob_s, rows_s, exp_s, map_s = _group_forward(
        x_sample, mod_s, past, rel_bias, lam, weights, lam_init=lam_init, qb=ts, kb=512, tm=tm_s,
        per_row_mod=True)

    w_out_bf = w_out[0].astype(BF16)
    wr_t = w_router[0].T
    rb = router_bias[0].reshape(N_EXPERTS, 1)
    no_rows = jnp.zeros((N_EXPERTS, 1), jnp.int32)
    x1_p, h2_p, ek_p, pk_p, gk_p, cnt_p = _outproj_call(
        xp2, oa_p, ob_p, exp_p(mod_p[2]), exp_p(mod_p[4]), exp_p(mod_p[3]), map_p, norm_f_g[0], w_out_bf, wr_t, rb,
        no_rows, tm_p)
    x1_s, h2_s, ek_s, pk_s, gk_s, cnt_all = _outproj_call(
        xs2, oa_s, ob_s, exp_s(mod_s[2]), exp_s(mod_s[4]), exp_s(mod_s[3]), map_s, norm_f_g[0], w_out_bf, wr_t, rb,
        cnt_p, tm_s)

    n_p, n_s = bp * tp, bs * ts
    n_all = n_p + n_s
    n_rows = -(-(n_all * TOP_K) // EXPERT_TILE) * EXPERT_TILE + N_EXPERTS * EXPERT_TILE
    counts = cnt_all[:, 0]
    padded = (counts + EXPERT_TILE - 1) // EXPERT_TILE * EXPERT_TILE
    ends = jnp.cumsum(padded)
    starts = ends - padded
    tile_first_row = jnp.arange(n_rows // EXPERT_TILE, dtype=jnp.int32) * EXPERT_TILE
    tile_expert = jnp.minimum(jnp.sum(ends[None, :] <= tile_first_row[:, None], axis=1), N_EXPERTS - 1).astype(jnp.int32)
    n_used = (ends[-1:] // EXPERT_TILE).astype(jnp.int32)
    ek = jnp.concatenate([ek_p, ek_s], axis=1)
    expert_ids = jnp.arange(N_EXPERTS, dtype=jnp.int32)[:, None, None]
    dest = jnp.sum(jnp.where(ek[None] == expert_ids, starts[:, None, None], 0), axis=0) \
        + jnp.concatenate([pk_p, pk_s], axis=1)
    piece = jnp.arange(PARTS, dtype=jnp.int32)[None, :, None]
    idx = (dest[:, None, :] + piece * n_rows).reshape(1, TOP_K * PARTS * n_all).astype(jnp.int32)

    h2_all = jnp.concatenate([h2_p, h2_s], axis=1).reshape(PARTS * n_all, PART_W)
    x_sorted = _sc_scatter_rows(h2_all, idx, PARTS * n_rows).reshape(PARTS, n_rows, PART_W)
    y_sorted = _expert_call(tile_expert, n_used, x_sorted, w_gate[0], w_up[0], w_down[0])
    rows = _sc_gather_rows(y_sorted.reshape(PARTS * n_rows, PART_W), idx).reshape(TOP_K, PARTS, n_all, PART_W)

    shared = (ws_gate[0].astype(BF16), ws_up[0].astype(BF16), ws_down[0].astype(BF16))
    assert n_p % tm_s == 0
    y_p = _final_call(x1_p, h2_p, rows, 0, gk_p.T, exp_p(mod_p[5]), map_p, final_g, *shared, tm_p)
    y_s = _final_call(x1_s, h2_s, rows, n_p // tm_s, gk_s.T, exp_s(mod_s[5]), map_s, final_g, *shared, tm_s)
    return (y_p.reshape(bp, tp, d), y_s.reshape(bs, ts, d)) + rows_p + rows_s
```

```python
import functools
import math

import numpy as np
import jax
import jax.numpy as jnp
from jax import lax
from jax.experimental import pallas as pl
from jax.experimental.pallas import tpu as pltpu
from jax.experimental.pallas import tpu_sc as plsc

D_MODEL = 2048
CHUNK = 64
A_HEADS = 8
A_KV_HEADS = 2
A_REP = A_HEADS // A_KV_HEADS
A_HEAD_DIM = 128
IDX_HEADS = 16
IDX_DIM = 64
TOPK_MAX = 256
B_HEADS = 4
B_HEAD_DIM = 128
N_REL_BUCKETS = 32
REL_MAX_DIST = 128
N_ATTN_HEADS = A_HEADS + B_HEADS
N_EXPERTS = 64
TOP_K = 8
N_GROUPS = 8
TOPK_GROUPS = 4
EXPERT_DIM = 512
SHARED_DIM = 512
ROUTED_SCALE = 2.5
EPS = 1e-6

LANES = 128
INT_MIN = -(2 ** 31)
NEG_BIG = -1e30
VMEM_LIMIT = 52 * 1024 * 1024

F32 = jnp.float32
BF16 = jnp.bfloat16
NT_DIMS = (((1,), (1,)), ((), ()))


def _params(n_axes):
    return pltpu.CompilerParams(dimension_semantics=("arbitrary",) * n_axes,
                                vmem_limit_bytes=VMEM_LIMIT)


def _resident(shape, index_map):
    return pl.BlockSpec(shape, index_map, pipeline_mode=pl.Buffered(1))


def _bucket_thresholds():
    nb = N_REL_BUCKETS // 2
    max_exact = nb // 2
    n = np.arange(max_exact, 4 * REL_MAX_DIST, dtype=np.int32)
    nf = n.astype(np.float32)
    large = max_exact + (np.log(nf / np.float32(max_exact)) / np.float32(math.log(REL_MAX_DIST / max_exact))
                         * np.float32(nb - max_exact)).astype(np.int32)
    large = np.minimum(large, nb - 1)
    assert np.all(np.diff(large) >= 0)
    ths = [int(n[np.argmax(large >= b)]) for b in range(max_exact + 1, nb)]
    return max_exact, nb, ths


_MAX_EXACT, _NB, _BUCKET_THS = _bucket_thresholds()
assert _BUCKET_THS[-1] <= LANES + 1
FAR_BUCKET = _NB - 1


def _ada_kernel(c_ref, w_ref, b_ref, o_ref):
    c = c_ref[...]
    s = c * jax.nn.sigmoid(c)
    o_ref[...] = jnp.dot(s, w_ref[...], preferred_element_type=F32,
                         precision=lax.Precision.HIGHEST) + b_ref[...]


def _ada_call(c, w_ada, b_ada):
    rows, d = c.shape
    n = w_ada.shape[1]
    tn = 1536
    assert n % tn == 0
    return pl.pallas_call(
        _ada_kernel,
        grid=(n // tn,),
        in_specs=[pl.BlockSpec((rows, d), lambda j: (0, 0)),
                  pl.BlockSpec((d, tn), lambda j: (0, j)),
                  pl.BlockSpec((1, tn), lambda j: (0, j))],
        out_specs=pl.BlockSpec((rows, tn), lambda j: (0, j)),
        out_shape=jax.ShapeDtypeStruct((rows, n), F32),
        compiler_params=_params(1),
        name="ada_mod",
    )(c, w_ada, b_ada.reshape(1, n))


def _modulated_norm(x, g, sc, sh):
    y = x * lax.rsqrt(jnp.mean(x * x, axis=-1, keepdims=True) + EPS) * g
    return y * (1.0 + sc) + sh


def _norm_proj_kernel(x_ref, sc_ref, sh_ref, g_ref, w_ref, *out_refs, defs):
    h = _modulated_norm(x_ref[...], g_ref[...], sc_ref[...], sh_ref[...]).astype(BF16)
    done = {}
    for (start, width, store_w, mult), o_ref in zip(defs, out_refs):
        if (start, width) not in done:
            done[(start, width)] = jnp.dot(h, w_ref[:, start:start + width], preferred_element_type=F32)
        r = done[(start, width)][:, :store_w]
        o_ref[...] = (r if mult is None else r * mult).astype(o_ref.dtype)


def _norm_proj_call(x2d, sc, sh, mod_map, g, w_bf, defs, dtypes, tm, name):
    n, d = x2d.shape
    assert n % tm == 0
    r = sc.shape[1]
    out_shapes = [jax.ShapeDtypeStruct((n, d_[2]), dt) for d_, dt in zip(defs, dtypes)]
    out_specs = [pl.BlockSpec((tm, d_[2]), lambda i: (i, 0)) for d_ in defs]
    return pl.pallas_call(
        functools.partial(_norm_proj_kernel, defs=tuple(defs)),
        grid=(n // tm,),
        in_specs=[pl.BlockSpec((tm, d), lambda i: (i, 0)),
                  pl.BlockSpec((None, r, d), mod_map),
                  pl.BlockSpec((None, r, d), mod_map),
                  pl.BlockSpec((1, d), lambda i: (0, 0)),
                  _resident(w_bf.shape, lambda i: (0, 0))],
        out_specs=out_specs,
        out_shape=out_shapes,
        compiler_params=_params(1),
        name=name,
    )(x2d, sc, sh, g.reshape(1, d), w_bf)


def _bias_tile_kernel(tab_ref, o_ref, *, qb):
    _, rows, width = o_ref.shape
    r = lax.broadcasted_iota(jnp.int32, (rows, width), 0)
    koff = lax.broadcasted_iota(jnp.int32, (rows, width), 1) - LANES
    rel = koff - r
    n = jnp.abs(rel)
    large = jnp.full((rows, width), _MAX_EXACT, jnp.int32)
    for th in _BUCKET_THS:
        large = large + jnp.where(n >= th, 1, 0)
    bucket = jnp.where(rel > 0, _NB, 0) + jnp.where(n < _MAX_EXACT, n, large)
    visible = ((koff >> 6) <= (r >> 6)) & (koff < qb)
    for h in range(N_ATTN_HEADS):
        b = jnp.zeros((rows, width), F32)
        for bk in range(N_REL_BUCKETS):
            b = jnp.where(bucket == bk, tab_ref[bk, h], b)
        o_ref[h] = jnp.where(visible, b, NEG_BIG)


def _bias_tile_call(rel_bias, qb):
    qbr = -(-qb // LANES) * LANES
    return pl.pallas_call(
        functools.partial(_bias_tile_kernel, qb=qb),
        in_specs=[pl.BlockSpec(memory_space=pltpu.SMEM)],
        out_specs=pl.BlockSpec(memory_space=pltpu.VMEM),
        out_shape=jax.ShapeDtypeStruct((N_ATTN_HEADS, qb, LANES + qbr), F32),
        name="bias_tiles",
    )(rel_bias)


def _sweep_far(q0, qb, kb, fn):
    far_end = jnp.maximum(q0 - LANES, 0)
    n_full = far_end // kb
    sub = kb // LANES

    def whole(j):
        fn(pl.multiple_of(j * kb, kb), j * sub, None)

    def pair(j, carry):
        whole(2 * j)
        whole(2 * j + 1)
        return carry

    lax.fori_loop(0, n_full // 2, pair, 0)
    pl.when(n_full % 2 == 1)(lambda: whole(n_full - 1))

    @pl.when(far_end > n_full * kb)
    def _():
        start = pl.multiple_of(n_full * kb, kb)
        col = start + lax.broadcasted_iota(jnp.int32, (qb, LANES), 1)
        fn(start, n_full * sub, [jnp.where(col + c * LANES < far_end, 0.0, NEG_BIG) for c in range(sub)])


def _key_rows(ref, start, width):
    blk = ref[pl.ds(start, width), :].astype(BF16)
    pad = -blk.shape[1] % LANES
    return blk if pad == 0 else jnp.concatenate([blk, jnp.zeros((width, pad), BF16)], axis=1)


def _dsa_kernel(tab_ref, q_ref, qi_ref, iw_ref, k_ref, v_ref, ki_ref, kc_ref, vc_ref, kic_ref, tile_ref, o_ref,
                keys_scr, half_scr, thr_scr, mfar_scr, mnear_scr, l_scr, acc_scr, *, qb, kb, qoff, topk):
    i = pl.program_id(1)
    q0 = qoff + i * qb
    qbr = tile_ref.shape[2] - LANES
    sub = kb // LANES
    w = iw_ref[:, :IDX_HEADS] * (IDX_HEADS ** -0.5 * IDX_DIM ** -0.5)

    def score_keys(kib):
        width = kib.shape[0]
        acc = jnp.zeros((qb, width), F32)
        for h in range(IDX_HEADS):
            d = lax.dot_general(qi_ref[:, h * LANES:(h + 1) * LANES], kib, NT_DIMS,
                                preferred_element_type=F32)
            acc = acc + w[:, h:h + 1] * jnp.maximum(d, 0.0)
        bits = pltpu.bitcast(acc, jnp.int32)
        return bits ^ ((bits >> 31) & 0x7FFFFFFF)

    n_score = (q0 + kb - 1) // kb

    def store_keys(slab0, key):
        for c in range(key.shape[1] // LANES):
            part = key[:, c * LANES:(c + 1) * LANES]
            keys_scr[slab0 + c] = part
            half_scr[slab0 + c] = (part >> 16).astype(jnp.int16)

    def score_body(j, carry):
        store_keys(j * sub, score_keys(_key_rows(ki_ref, pl.multiple_of(j * kb, kb), kb)))
        return carry

    lax.fori_loop(0, n_score, score_body, 0)

    cur_blk = q0 // LANES
    key = score_keys(kic_ref[...])
    store_keys(cur_blk, jnp.where(tile_ref[0, :, LANES:] > 0.5 * NEG_BIG, key, INT_MIN))
    n_blk = cur_blk + qbr // LANES

    rb = min(qb, LANES)
    lo16 = -(2 ** 15)
    n4 = n_blk // 4

    chunks = [pl.ds(c * rb, rb) for c in range(qb // rb)]

    def count_ge(ts):
        tbs = [jnp.broadcast_to(t, (rb, LANES)).astype(jnp.int16) for t in ts]

        def count(j, cnts):
            return tuple(cnt + jnp.where(half_scr[j, rows, :] >= tb, jnp.int16(1), jnp.int16(0))
                         for cnt, rows, tb in zip(cnts, chunks, tbs))

        def body4(j, cnts):
            for u in range(4):
                cnts = count(4 * j + u, cnts)
            return cnts

        cnts = lax.fori_loop(0, n4, body4, tuple(jnp.zeros((rb, LANES), jnp.int16) for _ in chunks))
        cnts = lax.fori_loop(4 * n4, n_blk, count, cnts)
        return [jnp.sum(cnt.astype(jnp.int32), axis=-1, keepdims=True) for cnt in cnts]

    def search(needed):
        def bit_body(it, ts):
            cands = [t + (jnp.int32(1) << (15 - it)) for t in ts]
            return tuple(jnp.where(c >= n, cand, t) for c, n, cand, t in zip(count_ge(cands), needed, cands, ts))

        return lax.fori_loop(0, 16, bit_body, tuple(jnp.full((rb, 1), lo16, jnp.int32) for _ in chunks))

    t_hi = search([topk] * len(chunks))
    top = -lo16 - 1
    above = [jnp.where(t == top, 0, c)
             for t, c in zip(t_hi, count_ge([jnp.minimum(t + 1, top) for t in t_hi]))]
    t_hi_b = [jnp.broadcast_to(t, (rb, LANES)) for t in t_hi]

    def low_plane(j, carry):
        for rows, tb in zip(chunks, t_hi_b):
            k32 = keys_scr[j, rows, :]
            low = (k32 & 0xFFFF) + lo16
            half_scr[j, rows, :] = jnp.where((k32 >> 16) == tb, low, lo16).astype(jnp.int16)
        return carry

    lax.fori_loop(0, n_blk, low_plane, 0)
    t_lo = search([topk - a for a in above])
    for rows, th, tl in zip(chunks, t_hi, t_lo):
        t = th * 65536 + (tl - lo16)
        thr_scr[rows, :] = jnp.broadcast_to(jnp.maximum(t, INT_MIN + 1), (rb, LANES))

    hd = A_HEAD_DIM

    def select_mask(blk0, n):
        return [jnp.where(keys_scr[blk0 + c] >= thr_scr[...], 0.0, NEG_BIG) for c in range(n)]

    def logits(h, kblk):
        g = h // A_REP
        s = lax.dot_general(q_ref[:, h * hd:(h + 1) * hd], kblk[:, g * hd:(g + 1) * hd], NT_DIMS,
                            preferred_element_type=F32)
        return [s[:, c * LANES:(c + 1) * LANES] for c in range(s.shape[1] // LANES)]

    def tile_cols(h, col0, n):
        return [tile_ref[h, :, col0 + c * LANES:col0 + (c + 1) * LANES] for c in range(n)]

    mfar_scr[...] = jnp.full(mfar_scr.shape, NEG_BIG, F32)
    mnear_scr[...] = jnp.full(mnear_scr.shape, NEG_BIG, F32)

    def far_mask(blk0, colmask):
        am = select_mask(blk0, sub)
        return am if colmask is None else [a + c for a, c in zip(am, colmask)]

    def max_far(start, blk0, colmask):
        kblk = _key_rows(k_ref, start, kb)
        am = far_mask(blk0, colmask)
        for h in range(A_HEADS):
            mfar_scr[h] = functools.reduce(
                jnp.maximum, [s + a for s, a in zip(logits(h, kblk), am)], mfar_scr[h])

    def max_near(kblk, blk0, col0):
        n = kblk.shape[0] // LANES
        am = select_mask(blk0, n)
        for h in range(A_HEADS):
            mnear_scr[h] = functools.reduce(
                jnp.maximum, [s + a + b for s, a, b in zip(logits(h, kblk), am, tile_cols(h, col0, n))],
                mnear_scr[h])

    prev_start = pl.multiple_of(q0 - LANES, LANES)
    _sweep_far(q0, qb, kb, max_far)
    pl.when(q0 > 0)(lambda: max_near(_key_rows(k_ref, prev_start, LANES), cur_blk - 1, 0))
    max_near(kc_ref[...], cur_blk, LANES)

    for h in range(A_HEADS):
        fb = tab_ref[FAR_BUCKET, h]
        m = jnp.maximum(jnp.max(mfar_scr[h], axis=-1, keepdims=True) + fb,
                        jnp.max(mnear_scr[h], axis=-1, keepdims=True))
        mfar_scr[h] = jnp.broadcast_to(fb - m, (qb, LANES))
        mnear_scr[h] = jnp.broadcast_to(-m, (qb, LANES))
    l_scr[...] = jnp.zeros(l_scr.shape, F32)
    acc_scr[...] = jnp.zeros(acc_scr.shape, F32)

    def accumulate(h, p, vblk):
        g = h // A_REP
        l_scr[h] += functools.reduce(jnp.add, p)
        acc_scr[h] += jnp.dot(jnp.concatenate(p, axis=1).astype(BF16), vblk[:, g * hd:(g + 1) * hd],
                              preferred_element_type=F32)

    def acc_far(start, blk0, colmask):
        kblk = _key_rows(k_ref, start, kb)
        vblk = _key_rows(v_ref, start, kb)
        am = far_mask(blk0, colmask)
        for h in range(A_HEADS):
            shift = mfar_scr[h]
            accumulate(h, [jnp.exp(s + a + shift) for s, a in zip(logits(h, kblk), am)], vblk)

    def acc_near(kblk, vblk, blk0, col0):
        n = kblk.shape[0] // LANES
        am = select_mask(blk0, n)
        for h in range(A_HEADS):
            shift = mnear_scr[h]
            accumulate(h, [jnp.exp(s + a + b + shift) for s, a, b in
                           zip(logits(h, kblk), am, tile_cols(h, col0, n))], vblk)

    _sweep_far(q0, qb, kb, acc_far)
    pl.when(q0 > 0)(lambda: acc_near(_key_rows(k_ref, prev_start, LANES), _key_rows(v_ref, prev_start, LANES),
                                     cur_blk - 1, 0))
    acc_near(kc_ref[...], vc_ref[...], cur_blk, LANES)

    for h in range(A_HEADS):
        o_ref[:, h * hd:(h + 1) * hd] = (
            acc_scr[h] / jnp.sum(l_scr[h], axis=-1, keepdims=True)).astype(o_ref.dtype)


def _dsa_call(rel_bias, tiles, q, qi, iw, far, cur, *, qb, kb, qoff, topk):
    b, t, _ = q.shape
    lf = far[0].shape[1]
    qbr = tiles.shape[2] - LANES
    q0_max = qoff + t - qb
    far_end = -(-q0_max // kb) * kb
    assert t % qb == 0 and qoff % LANES == 0 and far_end <= lf and cur[0].shape[1] == (t // qb) * qbr
    n_slabs = max(far_end, q0_max + qbr) // LANES
    mode = {"pipeline_mode": pl.Buffered(1)} if t // qb > 1 else {}
    far_spec = lambda a: pl.BlockSpec((None, lf, a.shape[2]), lambda bi, i: (bi, 0, 0), **mode)
    cur_spec = lambda a: pl.BlockSpec((None, qbr, a.shape[2]), lambda bi, i: (bi, i, 0))
    return pl.pallas_call(
        functools.partial(_dsa_kernel, qb=qb, kb=kb, qoff=qoff, topk=topk),
        grid=(b, t // qb),
        in_specs=[pl.BlockSpec(memory_space=pltpu.SMEM),
                  pl.BlockSpec((None, qb, q.shape[2]), lambda bi, i: (bi, i, 0)),
                  pl.BlockSpec((None, qb, qi.shape[2]), lambda bi, i: (bi, i, 0)),
                  pl.BlockSpec((None, qb, iw.shape[2]), lambda bi, i: (bi, i, 0)),
                  far_spec(far[0]), far_spec(far[1]), far_spec(far[2]),
                  cur_spec(cur[0]), cur_spec(cur[1]), cur_spec(cur[2]),
                  _resident(tiles.shape, lambda bi, i: (0, 0, 0))],
        out_specs=pl.BlockSpec((None, qb, A_HEADS * A_HEAD_DIM), lambda bi, i: (bi, i, 0)),
        out_shape=jax.ShapeDtypeStruct((b, t, A_HEADS * A_HEAD_DIM), BF16),
        scratch_shapes=[pltpu.VMEM((n_slabs, qb, LANES), jnp.int32),
                        pltpu.VMEM((n_slabs, qb, LANES), jnp.int16),
                        pltpu.VMEM((qb, LANES), jnp.int32),
                        pltpu.VMEM((A_HEADS, qb, LANES), F32),
                        pltpu.VMEM((A_HEADS, qb, LANES), F32),
                        pltpu.VMEM((A_HEADS, qb, LANES), F32),
                        pltpu.VMEM((A_HEADS, qb, A_HEAD_DIM), F32)],
        compiler_params=_params(2),
        name="mixer_a",
    )(rel_bias, q, qi, iw, *far, *cur, tiles)


def _diff_kernel(tab_ref, lam_ref, sg_ref, q_ref, k_ref, v_ref, kc_ref, vc_ref, tile_ref, o_ref,
                 mfar_scr, mnear_scr, l_scr, acc_scr, *, qb, kb, qoff, lam_init):
    head = pl.program_id(1)
    i = pl.program_id(2)
    q0 = qoff + i * qb
    qbr = tile_ref.shape[2] - LANES
    hd = B_HEAD_DIM
    far_bias = tab_ref[FAR_BUCKET, A_HEADS + head]

    def logits(c, kblk):
        s = lax.dot_general(q_ref[:, c * hd:(c + 1) * hd], kblk[:, c * hd:(c + 1) * hd], NT_DIMS,
                            preferred_element_type=F32)
        return [s[:, u * LANES:(u + 1) * LANES] for u in range(s.shape[1] // LANES)]

    def tile_cols(col0, n):
        return [tile_ref[0, :, col0 + u * LANES:col0 + (u + 1) * LANES] for u in range(n)]

    mfar_scr[...] = jnp.full(mfar_scr.shape, NEG_BIG, F32)
    mnear_scr[...] = jnp.full(mnear_scr.shape, NEG_BIG, F32)

    def max_far(start, blk0, colmask):
        kblk = _key_rows(k_ref, start, kb)
        for c in range(2):
            s = logits(c, kblk)
            if colmask is not None:
                s = [x + m for x, m in zip(s, colmask)]
            mfar_scr[c] = functools.reduce(jnp.maximum, s, mfar_scr[c])

    def max_near(kblk, col0):
        for c in range(2):
            mnear_scr[c] = functools.reduce(
                jnp.maximum, [s + b for s, b in zip(logits(c, kblk), tile_cols(col0, kblk.shape[0] // LANES))],
                mnear_scr[c])

    prev_start = pl.multiple_of(q0 - LANES, LANES)
    _sweep_far(q0, qb, kb, max_far)
    pl.when(q0 > 0)(lambda: max_near(_key_rows(k_ref, prev_start, LANES), 0))
    max_near(kc_ref[...], LANES)

    for c in range(2):
        m = jnp.maximum(jnp.max(mfar_scr[c], axis=-1, keepdims=True) + far_bias,
                        jnp.max(mnear_scr[c], axis=-1, keepdims=True))
        mfar_scr[c] = jnp.broadcast_to(far_bias - m, (qb, LANES))
        mnear_scr[c] = jnp.broadcast_to(-m, (qb, LANES))
    l_scr[...] = jnp.zeros(l_scr.shape, F32)
    acc_scr[...] = jnp.zeros(acc_scr.shape, F32)

    def accumulate(c, p, vblk):
        l_scr[c] += functools.reduce(jnp.add, p)
        acc_scr[c] += jnp.dot(jnp.concatenate(p, axis=1).astype(BF16), vblk, preferred_element_type=F32)

    def acc_far(start, blk0, colmask):
        kblk = _key_rows(k_ref, start, kb)
        vblk = _key_rows(v_ref, start, kb)
        for c in range(2):
            shift = mfar_scr[c]
            shifts = [shift] * (kb // LANES) if colmask is None else [shift + m for m in colmask]
            accumulate(c, [jnp.exp(s + sh) for s, sh in zip(logits(c, kblk), shifts)], vblk)

    def acc_near(kblk, vblk, col0):
        for c in range(2):
            shift = mnear_scr[c]
            accumulate(c, [jnp.exp(s + b + shift) for s, b in
                           zip(logits(c, kblk), tile_cols(col0, kblk.shape[0] // LANES))], vblk)

    _sweep_far(q0, qb, kb, acc_far)
    pl.when(q0 > 0)(lambda: acc_near(_key_rows(k_ref, prev_start, LANES), _key_rows(v_ref, prev_start, LANES), 0))
    acc_near(kc_ref[...], vc_ref[...], LANES)

    lam = lam_ref[0, 0]
    o = (acc_scr[0] / jnp.sum(l_scr[0], axis=-1, keepdims=True)
         - lam * (acc_scr[1] / jnp.sum(l_scr[1], axis=-1, keepdims=True)))
    o = o * lax.rsqrt(jnp.mean(o * o, axis=-1, keepdims=True) + EPS) * sg_ref[...]
    o_ref[...] = (o * (1.0 - lam_init)).astype(o_ref.dtype)


def _diff_call(rel_bias, tiles, lam, subln_g, q, far, cur, *, qb, kb, qoff, lam_init):
    b, t, _ = q.shape
    lf = far[0].shape[1]
    hw = 2 * B_HEAD_DIM
    qbr = tiles.shape[2] - LANES
    q0_max = qoff + t - qb
    far_end = -(-max(q0_max - LANES, 0) // kb) * kb
    assert t % qb == 0 and qoff % LANES == 0 and far_end <= lf and cur[0].shape[1] == (t // qb) * qbr
    far_spec = pl.BlockSpec((None, lf, hw), lambda bi, h, i: (bi, 0, h))
    cur_spec = pl.BlockSpec((None, qbr, hw), lambda bi, h, i: (bi, i, h))
    return pl.pallas_call(
        functools.partial(_diff_kernel, qb=qb, kb=kb, qoff=qoff, lam_init=lam_init),
        grid=(b, B_HEADS, t // qb),
        in_specs=[pl.BlockSpec(memory_space=pltpu.SMEM),
                  pl.BlockSpec(memory_space=pltpu.SMEM),
                  pl.BlockSpec((1, hw), lambda bi, h, i: (0, 0)),
                  pl.BlockSpec((None, qb, hw), lambda bi, h, i: (bi, i, h)),
                  far_spec, far_spec, cur_spec, cur_spec,
                  pl.BlockSpec((1, qb, tiles.shape[2]), lambda bi, h, i: (A_HEADS + h, 0, 0))],
        out_specs=pl.BlockSpec((None, qb, hw), lambda bi, h, i: (bi, i, h)),
        out_shape=jax.ShapeDtypeStruct((b, t, B_HEADS * hw), BF16),
        scratch_shapes=[pltpu.VMEM((2, qb, LANES), F32),
                        pltpu.VMEM((2, qb, LANES), F32),
                        pltpu.VMEM((2, qb, LANES), F32),
                        pltpu.VMEM((2, qb, hw), F32)],
        compiler_params=_params(3),
        name="mixer_b",
    )(rel_bias, lam, subln_g.reshape(1, hw), q, *far, *cur, tiles)


def _lam_kernel(dl_ref, o_ref, *, lam_init):
    dl = dl_ref[...]
    s1 = jnp.sum(dl[0:1] * dl[1:2], axis=-1, keepdims=True)
    s2 = jnp.sum(dl[2:3] * dl[3:4], axis=-1, keepdims=True)
    o_ref[...] = jnp.exp(s1) - jnp.exp(s2) + lam_init


def _lam_call(diff_lam, lam_init):
    return pl.pallas_call(
        functools.partial(_lam_kernel, lam_init=lam_init),
        out_shape=jax.ShapeDtypeStruct((1, 1), F32),
        name="diff_lambda",
    )(diff_lam)


def _route(logits_t, rbias, counts):
    tm = logits_t.shape[1]
    per = N_EXPERTS // N_GROUPS
    scores = jax.nn.sigmoid(logits_t).reshape(N_GROUPS, per, tm)
    ch = scores + rbias.reshape(N_GROUPS, per, 1)
    sub = lax.broadcasted_iota(jnp.int32, ch.shape, 1)
    grp = lax.broadcasted_iota(jnp.int32, ch.shape, 0)
    m1 = jnp.max(ch, axis=1, keepdims=True)
    i1 = jnp.min(jnp.where(ch == m1, sub, per), axis=1, keepdims=True)
    m2 = jnp.max(jnp.where(sub == i1, -jnp.inf, ch), axis=1, keepdims=True)
    gs = jnp.broadcast_to(m1 + m2, ch.shape)
    rank = jnp.zeros(ch.shape, jnp.int32)
    for g2 in range(N_GROUPS):
        o = gs[g2:g2 + 1]
        rank = rank + jnp.where(o > gs, 1, 0) + jnp.where(o == gs, 1, 0) * jnp.where(grp > g2, 1, 0)
    cm = jnp.where(rank < TOPK_GROUPS, ch, -jnp.inf)
    eid = grp * per + sub
    erank = jnp.zeros(cm.shape, jnp.int32)
    for e2 in range(N_EXPERTS):
        o = cm[e2 // per:e2 // per + 1, e2 % per:e2 % per + 1, :]
        erank = erank + jnp.where(o > cm, 1, 0) + jnp.where(o == cm, 1, 0) * jnp.where(eid > e2, 1, 0)
    chosen = erank < TOP_K
    sel = jnp.where(chosen, scores, 0.0)
    denom = jnp.sum(jnp.sum(sel, axis=1, keepdims=True), axis=0, keepdims=True)
    gates = sel / denom * ROUTED_SCALE

    ones = jnp.where(chosen, 1.0, 0.0).reshape(N_EXPERTS, tm)
    earlier = (lax.broadcasted_iota(jnp.int32, (tm, tm), 0) < lax.broadcasted_iota(jnp.int32, (tm, tm), 1))
    before = jnp.dot(ones.astype(BF16), jnp.where(earlier, 1.0, 0.0).astype(BF16), preferred_element_type=F32)
    pos = (before + counts.astype(F32)).reshape(N_GROUPS, per, tm)
    new_counts = counts + jnp.sum(ones, axis=1, keepdims=True).astype(jnp.int32)

    def per_slot(values):
        rows = [jnp.sum(jnp.sum(jnp.where(erank == k, values, 0.0), axis=1, keepdims=True), axis=0)
                for k in range(TOP_K)]
        return jnp.concatenate(rows, axis=0)

    return (per_slot(eid.astype(F32)).astype(jnp.int32), per_slot(pos).astype(jnp.int32), per_slot(gates),
            new_counts)


def _pack_pair(a, b):
    ua = pltpu.bitcast(a.astype(BF16).astype(F32), jnp.uint32)
    ub = pltpu.bitcast(b.astype(BF16).astype(F32), jnp.uint32)
    return pltpu.bitcast(ua | (ub >> 16), jnp.int32)


def _unpack_pair(p):
    u = pltpu.bitcast(p, jnp.uint32)
    a = pltpu.bitcast(u & jnp.uint32(0xFFFF0000), F32)
    b = pltpu.bitcast(u << 16, F32)
    return a.astype(BF16), b.astype(BF16)


PARTS = 4
PART_W = D_MODEL // 2 // PARTS


def _pack_rows(x):
    half = x.shape[1] // 2
    return jnp.stack([_pack_pair(x[:, j * PART_W:(j + 1) * PART_W],
                                 x[:, half + j * PART_W:half + (j + 1) * PART_W]) for j in range(PARTS)])


def _packed_matmul(parts_ref, w_ref):
    half = w_ref.shape[0] // 2
    acc = None
    for j in range(PARTS):
        a, b = _unpack_pair(parts_ref[j])
        t = (jnp.dot(a, w_ref[j * PART_W:(j + 1) * PART_W, :], preferred_element_type=F32)
             + jnp.dot(b, w_ref[half + j * PART_W:half + (j + 1) * PART_W, :], preferred_element_type=F32))
        acc = t if acc is None else acc + t
    return acc


def _outproj_kernel(x_ref, oa_ref, ob_ref, ga_ref, sc_ref, sh_ref, g_ref, w_ref, wr_ref, rb_ref, cin_ref,
                    x1_ref, h2_ref, ek_ref, pk_ref, gk_ref, cout_ref, cnt_scr):
    @pl.when(pl.program_id(0) == 0)
    def _():
        cnt_scr[...] = cin_ref[...]

    half = oa_ref.shape[1]
    mix = (jnp.dot(oa_ref[...], w_ref[:half, :], preferred_element_type=F32)
           + jnp.dot(ob_ref[...], w_ref[half:, :], preferred_element_type=F32))
    x1 = x_ref[...] + ga_ref[...] * mix
    x1_ref[...] = x1
    h2 = _modulated_norm(x1, g_ref[...], sc_ref[...], sh_ref[...])
    h2_ref[...] = _pack_rows(h2)
    logits_t = lax.dot_general(wr_ref[...], h2, NT_DIMS, preferred_element_type=F32,
                               precision=lax.Precision.HIGHEST)
    ek, pk, gk, counts = _route(logits_t, rb_ref[...], cnt_scr[...])
    ek_ref[...] = ek
    pk_ref[...] = pk
    gk_ref[...] = gk
    cnt_scr[...] = counts
    cout_ref[...] = counts


def _outproj_call(x2d, oa, ob, ga, sc, sh, mod_map, g, w_bf, wr_t, rbias, counts_in, tm):
    n, d = x2d.shape
    assert n % tm == 0
    r = sc.shape[1]
    mod_spec = pl.BlockSpec((None, r, d), mod_map)
    slot_spec = pl.BlockSpec((TOP_K, tm), lambda i: (0, i))
    return pl.pallas_call(
        _outproj_kernel,
        grid=(n // tm,),
        in_specs=[pl.BlockSpec((tm, d), lambda i: (i, 0)),
                  pl.BlockSpec((tm, oa.shape[1]), lambda i: (i, 0)),
                  pl.BlockSpec((tm, ob.shape[1]), lambda i: (i, 0)),
                  mod_spec, mod_spec, mod_spec,
                  pl.BlockSpec((1, d), lambda i: (0, 0)),
                  _resident(w_bf.shape, lambda i: (0, 0)),
                  _resident(wr_t.shape, lambda i: (0, 0)),
                  pl.BlockSpec((N_EXPERTS, 1), lambda i: (0, 0)),
                  pl.BlockSpec((N_EXPERTS, 1), lambda i: (0, 0))],
        out_specs=[pl.BlockSpec((tm, d), lambda i: (i, 0)),
                   pl.BlockSpec((PARTS, tm, PART_W), lambda i: (0, i, 0)),
                   slot_spec, slot_spec, slot_spec,
                   pl.BlockSpec((N_EXPERTS, 1), lambda i: (0, 0))],
        out_shape=[jax.ShapeDtypeStruct((n, d), F32),
                   jax.ShapeDtypeStruct((PARTS, n, PART_W), jnp.int32),
                   jax.ShapeDtypeStruct((TOP_K, n), jnp.int32),
                   jax.ShapeDtypeStruct((TOP_K, n), jnp.int32),
                   jax.ShapeDtypeStruct((TOP_K, n), F32),
                   jax.ShapeDtypeStruct((N_EXPERTS, 1), jnp.int32)],
        scratch_shapes=[pltpu.VMEM((N_EXPERTS, 1), jnp.int32)],
        compiler_params=_params(1),
        name="out_proj_router",
    )(x2d, oa, ob, ga, sc, sh, g.reshape(1, d), w_bf, wr_t, rbias, counts_in)


SC_WINDOW = 128
EXPERT_TILE = 512


def _sc_mesh():
    return plsc.VectorSubcoreMesh(core_axis_name="core", subcore_axis_name="subcore")


def _sc_scatter_rows(x, idx, n_out):
    n, w = x.shape
    steps = n // SC_WINDOW
    total = idx.shape[1] // SC_WINDOW

    @pl.kernel(out_type=jax.ShapeDtypeStruct((n_out, w), x.dtype), mesh=_sc_mesh(), scratch_types=[],
               name="moe_dispatch")
    def scatter(x_hbm, i_hbm, o_hbm):
        def body(x_vmem, i_vmem):
            pltpu.sync_copy(x_vmem, o_hbm.at[i_vmem.at[0]])

        pltpu.emit_pipeline(
            body, grid=(total,),
            in_specs=[pl.BlockSpec((SC_WINDOW, w), lambda i: (i % steps, 0)),
                      pl.BlockSpec((1, SC_WINDOW), lambda i: (0, i))],
            out_specs=[], core_axis_name=("core", "subcore"),
            dimension_semantics=(pltpu.PARALLEL,))(x_hbm, i_hbm)

    return scatter(x, idx)


def _sc_gather_rows(y, idx):
    w = y.shape[1]
    n = idx.shape[1]

    @pl.kernel(out_type=jax.ShapeDtypeStruct((n, w), y.dtype), mesh=_sc_mesh(), scratch_types=[],
               name="moe_collect")
    def gather(y_hbm, i_hbm, o_hbm):
        def body(i_vmem, o_vmem):
            pltpu.sync_copy(y_hbm.at[i_vmem.at[0]], o_vmem)

        pltpu.emit_pipeline(
            body, grid=(n // SC_WINDOW,),
            in_specs=[pl.BlockSpec((1, SC_WINDOW), lambda i: (0, i))],
            out_specs=[pl.BlockSpec((SC_WINDOW, w), lambda i: (i, 0))],
            core_axis_name=("core", "subcore"),
            dimension_semantics=(pltpu.PARALLEL,))(i_hbm, o_hbm)

    return gather(y, idx)


def _swiglu(parts_ref, wg_ref, wu_ref, wd_ref):
    a = _packed_matmul(parts_ref, wg_ref)
    u = _packed_matmul(parts_ref, wu_ref)
    hid = (a * jax.nn.sigmoid(a) * u).astype(BF16)
    return jnp.dot(hid, wd_ref[...], preferred_element_type=F32)


def _expert_kernel(te_ref, nu_ref, x_ref, wg_ref, wu_ref, wd_ref, y_ref, wg_scr, wu_scr, wd_scr):
    i = pl.program_id(0)

    @pl.when(i < nu_ref[0])
    def _():
        @pl.when((i == 0) | (te_ref[i] != te_ref[jnp.maximum(i - 1, 0)]))
        def _():
            wg_scr[...] = wg_ref[...].astype(BF16)
            wu_scr[...] = wu_ref[...].astype(BF16)
            wd_scr[...] = wd_ref[...].astype(BF16)

        y_ref[...] = _pack_rows(_swiglu(x_ref, wg_scr, wu_scr, wd_scr))


def _expert_call(tile_expert, n_used, x_sorted, wg, wu, wd):
    _, r, _ = x_sorted.shape
    ne, d, de = wg.shape
    n_tiles = r // EXPERT_TILE
    row_map = lambda i, te, nu: (0, jnp.minimum(i, nu[0] - 1), 0)
    return pl.pallas_call(
        _expert_kernel,
        grid_spec=pltpu.PrefetchScalarGridSpec(
            num_scalar_prefetch=2,
            grid=(n_tiles,),
            in_specs=[pl.BlockSpec((PARTS, EXPERT_TILE, PART_W), row_map),
                      pl.BlockSpec((None, d, de), lambda i, te, nu: (te[i], 0, 0)),
                      pl.BlockSpec((None, d, de), lambda i, te, nu: (te[i], 0, 0)),
                      pl.BlockSpec((None, de, d), lambda i, te, nu: (te[i], 0, 0))],
            out_specs=pl.BlockSpec((PARTS, EXPERT_TILE, PART_W), row_map),
            scratch_shapes=[pltpu.VMEM((d, de), BF16), pltpu.VMEM((d, de), BF16), pltpu.VMEM((de, d), BF16)]),
        out_shape=jax.ShapeDtypeStruct(x_sorted.shape, jnp.int32),
        compiler_params=_params(1),
        name="moe_experts",
    )(tile_expert, n_used, x_sorted, wg, wu, wd)


def _final_kernel(x_ref, h_ref, rows_ref, gk_ref, gf_ref, g_ref, sg_ref, su_ref, sd_ref, o_ref):
    moe = _swiglu(h_ref, sg_ref, su_ref, sd_ref)
    half = moe.shape[1] // 2
    lo = [moe[:, j * PART_W:(j + 1) * PART_W] for j in range(PARTS)]
    hi = [moe[:, half + j * PART_W:half + (j + 1) * PART_W] for j in range(PARTS)]
    for k in range(TOP_K):
        gate = gk_ref[:, k:k + 1]
        for j in range(PARTS):
            a, b = _unpack_pair(rows_ref[k, j])
            lo[j] = lo[j] + gate * a.astype(F32)
            hi[j] = hi[j] + gate * b.astype(F32)
    x = x_ref[...] + gf_ref[...] * jnp.concatenate(lo + hi, axis=1)
    o_ref[...] = x * lax.rsqrt(jnp.mean(x * x, axis=-1, keepdims=True) + EPS) * g_ref[...]


def _final_call(x1, h2p, rows, row_block0, gk_t, gf, mod_map, g, sg, su, sd, tm):
    n, d = x1.shape
    assert n % tm == 0
    r = gf.shape[1]
    return pl.pallas_call(
        _final_kernel,
        grid=(n // tm,),
        in_specs=[pl.BlockSpec((tm, d), lambda i: (i, 0)),
                  pl.BlockSpec((PARTS, tm, PART_W), lambda i: (0, i, 0)),
                  pl.BlockSpec((TOP_K, PARTS, tm, PART_W), lambda i: (0, 0, i + row_block0, 0)),
                  pl.BlockSpec((tm, TOP_K), lambda i: (i, 0)),
                  pl.BlockSpec((None, r, d), mod_map),
                  pl.BlockSpec((1, d), lambda i: (0, 0)),
                  _resident(sg.shape, lambda i: (0, 0)),
                  _resident(su.shape, lambda i: (0, 0)),
                  _resident(sd.shape, lambda i: (0, 0))],
        out_specs=pl.BlockSpec((tm, d), lambda i: (i, 0)),
        out_shape=jax.ShapeDtypeStruct((n, d), F32),
        compiler_params=_params(1),
        name="final_norm",
    )(x1, h2p, rows, gk_t, gf, g.reshape(1, d), sg, su, sd)


_QA, _KA, _VA, _QI, _KI, _IW, _QB, _KB, _VB = range(9)
_IN_SIZES = (A_HEADS * A_HEAD_DIM, A_KV_HEADS * A_HEAD_DIM, A_KV_HEADS * A_HEAD_DIM,
             IDX_HEADS * IDX_DIM, IDX_DIM, IDX_HEADS,
             B_HEADS * 2 * B_HEAD_DIM, B_HEADS * 2 * B_HEAD_DIM, B_HEADS * 2 * B_HEAD_DIM)


def _split_w_in(w_in):
    d = w_in.shape[0]
    offs = np.concatenate([[0], np.cumsum(_IN_SIZES)])
    cols = [w_in[:, offs[k]:offs[k + 1]].astype(BF16) for k in range(9)]
    zero = lambda n: jnp.zeros((d, n), BF16)
    qi = jnp.concatenate([cols[_QI].reshape(d, IDX_HEADS, IDX_DIM),
                          jnp.zeros((d, IDX_HEADS, LANES - IDX_DIM), BF16)], axis=-1).reshape(d, IDX_HEADS * LANES)
    w_a = jnp.concatenate([cols[_QA], cols[_KA], cols[_VA], qi,
                           cols[_KI], zero(LANES - IDX_DIM), cols[_IW], zero(LANES - IDX_HEADS)], axis=1)
    w_b = jnp.concatenate([cols[_QB], cols[_KB], cols[_VB]], axis=1)
    return w_a, w_b


def _group_forward(x, mod, past, rel_bias, lam, weights, *, lam_init, qb, kb, tm, per_row_mod):
    (norm_a_g, w_a, w_b, subln_g) = weights
    b, t, d = x.shape
    n = b * t
    x2d = x.reshape(n, d)
    sh_a, sc_a = mod[0], mod[1]
    if per_row_mod:
        expand = lambda m: jnp.repeat(m, t, axis=0).reshape(1, n, d)
        mod_map = lambda i: (0, i, 0)
    else:
        expand = lambda m: m.reshape(b, 1, d)
        mod_map = lambda i: ((i * tm) // t, 0, 0)
    sc3, sh3 = expand(sc_a), expand(sh_a)

    na = A_HEADS * A_HEAD_DIM
    nk = A_KV_HEADS * A_HEAD_DIM
    nq = IDX_HEADS * LANES
    c_ka, c_va, c_qi = na, na + nk, na + 2 * nk
    c_ki, c_iw = c_qi + nq, c_qi + nq + LANES
    defs_a = [(0, na, na, A_HEAD_DIM ** -0.5), (c_ka, nk, nk, None), (c_ka, nk, nk, None),
              (c_va, nk, nk, None), (c_va, nk, nk, None), (c_qi, nq, nq, None),
              (c_ki, LANES, IDX_DIM, None), (c_ki, LANES, LANES, None), (c_iw, LANES, LANES, None)]
    dt_a = [BF16, F32, BF16, F32, BF16, BF16, F32, BF16, F32]
    qa, ka, ka_bf, va, va_bf, qi, ki, ki_bf, iw = _norm_proj_call(
        x2d, sc3, sh3, mod_map, norm_a_g, w_a, defs_a, dt_a, tm, "in_proj_a")
    nb = B_HEADS * 2 * B_HEAD_DIM
    defs_b = [(0, nb, nb, B_HEAD_DIM ** -0.5), (nb, nb, nb, None), (nb, nb, nb, None),
              (2 * nb, nb, nb, None), (2 * nb, nb, nb, None)]
    dt_b = [BF16, F32, BF16, F32, BF16]
    qbm, kbm, kb_bf, vbm, vb_bf = _norm_proj_call(
        x2d, sc3, sh3, mod_map, norm_a_g, w_b, defs_b, dt_b, tm, "in_proj_b")

    r3 = lambda a: a.reshape(b, t, a.shape[-1])
    cur = [r3(ka_bf), r3(va_bf), r3(ki_bf), r3(kb_bf), r3(vb_bf)]
    if past is None:
        qoff = 0
        far = cur
    else:
        assert t == qb, "a group with cached keys is one query block per batch row"
        qoff = past[0].shape[1]
        qbr = -(-qb // LANES) * LANES
        cur = [jnp.concatenate([a, jnp.zeros((b, qbr - t, a.shape[-1]), BF16)], axis=1) for a in cur]
        far = [p.reshape(b, qoff, -1) for p in past]
        short = -qoff % kb
        if short:
            far = [jnp.concatenate([p, jnp.zeros((b, short, p.shape[-1]), p.dtype)], axis=1) for p in far]
    length = qoff + t
    topk = min(TOPK_MAX, length // 4)

    tiles = _bias_tile_call(rel_bias, qb)
    out_a = _dsa_call(rel_bias, tiles, r3(qa), r3(qi), r3(iw), far[:3], cur[:3], qb=qb, kb=kb, qoff=qoff, topk=topk)
    out_b = _diff_call(rel_bias, tiles, lam, subln_g, r3(qbm), far[3:], cur[3:], qb=qb, kb=kb, qoff=qoff,
                       lam_init=lam_init)
    rows = (ka.reshape(1, b, t, A_KV_HEADS, A_HEAD_DIM), va.reshape(1, b, t, A_KV_HEADS, A_HEAD_DIM),
            ki.reshape(1, b, t, IDX_DIM), kbm.reshape(1, b, t, B_HEADS, 2, B_HEAD_DIM),
            vbm.reshape(1, b, t, B_HEADS, 2 * B_HEAD_DIM))
    return x2d, out_a.reshape(n, -1), out_b.reshape(n, -1), rows, expand, mod_map


def kernel(x_prompt, x_sample, c_prompt, c_sample, cache_a_k, cache_a_v, cache_a_kidx, cache_b_k, cache_b_v,
           rel_bias, w_ada, b_ada, norm_a_g, w_in, w_out, diff_lam, subln_g, norm_f_g, w_router, router_bias,
           w_gate, w_up, w_down, ws_gate, ws_up, ws_down, final_g):
    assert w_ada.shape[0] == 1, "single-layer model"
    d = x_prompt.shape[-1]
    bp, tp, _ = x_prompt.shape
    bs, ts, _ = x_sample.shape
    lam_init = 0.8 - 0.6 * math.exp(-0.3 * 0)

    c_all = jnp.concatenate([c_prompt, c_sample], axis=0)
    rows = -(-c_all.shape[0] // 8) * 8
    c_pad = jnp.concatenate([c_all, jnp.zeros((rows - c_all.shape[0], d), F32)], axis=0)
    mod_all = _ada_call(c_pad, w_ada[0], b_ada[0])
    mod_p = [mod_all[:bp, k * d:(k + 1) * d] for k in range(6)]
    mod_s = [mod_all[bp:bp + bs, k * d:(k + 1) * d] for k in range(6)]

    w_a, w_b = _split_w_in(w_in[0])
    lam = _lam_call(diff_lam[0], lam_init)
    weights = (norm_a_g[0], w_a, w_b, subln_g[0])
    past = (cache_a_k[0], cache_a_v[0], cache_a_kidx[0], cache_b_k[0], cache_b_v[0])

    tm_p = 256
    tm_s = bs * ts
    xp2, oa_p, ob_p, rows_p, exp_p, map_p = _group_forward(
        x_prompt, mod_p, None, rel_bias, lam, weights, lam_init=lam_init, qb=256, kb=512, tm=tm_p,
        per_row_mod=False)
    xs2, oa_s, ob_s, rows_s, exp_s, map_s = _group_forward(
        x_sample, mod_s, past, rel_bias, lam, weights, lam_init=lam_init, qb=ts, kb=512, tm=tm_s,
        per_row_mod=True)

    w_out_bf = w_out[0].astype(BF16)
    wr_t = w_router[0].T
    rb = router_bias[0].reshape(N_EXPERTS, 1)
    no_rows = jnp.zeros((N_EXPERTS, 1), jnp.int32)
    x1_p, h2_p, ek_p, pk_p, gk_p, cnt_p = _outproj_call(
        xp2, oa_p, ob_p, exp_p(mod_p[2]), exp_p(mod_p[4]), exp_p(mod_p[3]), map_p, norm_f_g[0], w_out_bf, wr_t, rb,
        no_rows, tm_p)
    x1_s, h2_s, ek_s, pk_s, gk_s, cnt_all = _outproj_call(
        xs2, oa_s, ob_s, exp_s(mod_s[2]), exp_s(mod_s[4]), exp_s(mod_s[3]), map_s, norm_f_g[0], w_out_bf, wr_t, rb,
        cnt_p, tm_s)

    n_p, n_s = bp * tp, bs * ts
    n_all = n_p + n_s
    n_rows = -(-(n_all * TOP_K) // EXPERT_TILE) * EXPERT_TILE + N_EXPERTS * EXPERT_TILE
    counts = cnt_all[:, 0]
    padded = (counts + EXPERT_TILE - 1) // EXPERT_TILE * EXPERT_TILE
    ends = jnp.cumsum(padded)
    starts = ends - padded
    tile_first_row = jnp.arange(n_rows // EXPERT_TILE, dtype=jnp.int32) * EXPERT_TILE
    tile_expert = jnp.minimum(jnp.sum(ends[None, :] <= tile_first_row[:, None], axis=1), N_EXPERTS - 1).astype(jnp.int32)
    n_used = (ends[-1:] // EXPERT_TILE).astype(jnp.int32)
    ek = jnp.concatenate([ek_p, ek_s], axis=1)
    expert_ids = jnp.arange(N_EXPERTS, dtype=jnp.int32)[:, None, None]
    dest = jnp.sum(jnp.where(ek[None] == expert_ids, starts[:, None, None], 0), axis=0) \
        + jnp.concatenate([pk_p, pk_s], axis=1)
    piece = jnp.arange(PARTS, dtype=jnp.int32)[None, :, None]
    idx = (dest[:, None, :] + piece * n_rows).reshape(1, TOP_K * PARTS * n_all).astype(jnp.int32)

    h2_all = jnp.concatenate([h2_p, h2_s], axis=1).reshape(PARTS * n_all, PART_W)
    x_sorted = _sc_scatter_rows(h2_all, idx, PARTS * n_rows).reshape(PARTS, n_rows, PART_W)
    y_sorted = _expert_call(tile_expert, n_used, x_sorted, w_gate[0], w_up[0], w_down[0])
    rows = _sc_gather_rows(y_sorted.reshape(PARTS * n_rows, PART_W), idx).reshape(TOP_K, PARTS, n_all, PART_W)

    shared = (ws_gate[0].astype(BF16), ws_up[0].astype(BF16), ws_down[0].astype(BF16))
    assert n_p % tm_s == 0
    y_p = _final_call(x1_p, h2_p, rows, 0, gk_p.T, exp_p(mod_p[5]), map_p, final_g, *shared, tm_p)
    y_s = _final_call(x1_s, h2_s, rows, n_p // tm_s, gk_s.T, exp_s(mod_s[5]), map_s, final_g, *shared, tm_s)
    return (y_p.reshape(bp, tp, d), y_s.reshape(bs, ts, d)) + rows_p + rows_s
```

```python
import functools
import math

import numpy as np
import jax
import jax.numpy as jnp
from jax import lax
from jax.experimental import pallas as pl
from jax.experimental.pallas import tpu as pltpu
from jax.experimental.pallas import tpu_sc as plsc

D_MODEL = 2048
CHUNK = 64
A_HEADS = 8
A_KV_HEADS = 2
A_REP = A_HEADS // A_KV_HEADS
A_HEAD_DIM = 128
IDX_HEADS = 16
IDX_DIM = 64
TOPK_MAX = 256
B_HEADS = 4
B_HEAD_DIM = 128
N_REL_BUCKETS = 32
REL_MAX_DIST = 128
N_ATTN_HEADS = A_HEADS + B_HEADS
N_EXPERTS = 64
TOP_K = 8
N_GROUPS = 8
TOPK_GROUPS = 4
EXPERT_DIM = 512
SHARED_DIM = 512
ROUTED_SCALE = 2.5
EPS = 1e-6

LANES = 128
INT_MIN = -(2 ** 31)
NEG_BIG = -1e30
VMEM_LIMIT = 52 * 1024 * 1024

F32 = jnp.float32
BF16 = jnp.bfloat16
NT_DIMS = (((1,), (1,)), ((), ()))


def _params(n_axes):
    return pltpu.CompilerParams(dimension_semantics=("arbitrary",) * n_axes,
                                vmem_limit_bytes=VMEM_LIMIT)


def _resident(shape, index_map):
    return pl.BlockSpec(shape, index_map, pipeline_mode=pl.Buffered(1))


def _bucket_thresholds():
    nb = N_REL_BUCKETS // 2
    max_exact = nb // 2
    n = np.arange(max_exact, 4 * REL_MAX_DIST, dtype=np.int32)
    nf = n.astype(np.float32)
    large = max_exact + (np.log(nf / np.float32(max_exact)) / np.float32(math.log(REL_MAX_DIST / max_exact))
                         * np.float32(nb - max_exact)).astype(np.int32)
    large = np.minimum(large, nb - 1)
    assert np.all(np.diff(large) >= 0)
    ths = [int(n[np.argmax(large >= b)]) for b in range(max_exact + 1, nb)]
    return max_exact, nb, ths


_MAX_EXACT, _NB, _BUCKET_THS = _bucket_thresholds()
assert _BUCKET_THS[-1] <= LANES + 1
FAR_BUCKET = _NB - 1


def _ada_kernel(c_ref, w_ref, b_ref, o_ref):
    c = c_ref[...]
    s = c * jax.nn.sigmoid(c)
    o_ref[...] = jnp.dot(s, w_ref[...], preferred_element_type=F32,
                         precision=lax.Precision.HIGHEST) + b_ref[...]


def _ada_call(c, w_ada, b_ada):
    rows, d = c.shape
    n = w_ada.shape[1]
    tn = 1536
    assert n % tn == 0
    return pl.pallas_call(
        _ada_kernel,
        grid=(n // tn,),
        in_specs=[pl.BlockSpec((rows, d), lambda j: (0, 0)),
                  pl.BlockSpec((d, tn), lambda j: (0, j)),
                  pl.BlockSpec((1, tn), lambda j: (0, j))],
        out_specs=pl.BlockSpec((rows, tn), lambda j: (0, j)),
        out_shape=jax.ShapeDtypeStruct((rows, n), F32),
        compiler_params=_params(1),
        name="ada_mod",
    )(c, w_ada, b_ada.reshape(1, n))


def _modulated_norm(x, g, sc, sh):
    y = x * lax.rsqrt(jnp.mean(x * x, axis=-1, keepdims=True) + EPS) * g
    return y * (1.0 + sc) + sh


def _norm_proj_kernel(x_ref, sc_ref, sh_ref, g_ref, w_ref, *out_refs, defs):
    h = _modulated_norm(x_ref[...], g_ref[...], sc_ref[...], sh_ref[...]).astype(BF16)
    done = {}
    for (start, width, store_w, mult), o_ref in zip(defs, out_refs):
        if (start, width) not in done:
            done[(start, width)] = jnp.dot(h, w_ref[:, start:start + width], preferred_element_type=F32)
        r = done[(start, width)][:, :store_w]
        o_ref[...] = (r if mult is None else r * mult).astype(o_ref.dtype)


def _norm_proj_call(x2d, sc, sh, mod_map, g, w_bf, defs, dtypes, tm, name):
    n, d = x2d.shape
    assert n % tm == 0
    r = sc.shape[1]
    out_shapes = [jax.ShapeDtypeStruct((n, d_[2]), dt) for d_, dt in zip(defs, dtypes)]
    out_specs = [pl.BlockSpec((tm, d_[2]), lambda i: (i, 0)) for d_ in defs]
    return pl.pallas_call(
        functools.partial(_norm_proj_kernel, defs=tuple(defs)),
        grid=(n // tm,),
        in_specs=[pl.BlockSpec((tm, d), lambda i: (i, 0)),
                  pl.BlockSpec((None, r, d), mod_map),
                  pl.BlockSpec((None, r, d), mod_map),
                  pl.BlockSpec((1, d), lambda i: (0, 0)),
                  _resident(w_bf.shape, lambda i: (0, 0))],
        out_specs=out_specs,
        out_shape=out_shapes,
        compiler_params=_params(1),
        name=name,
    )(x2d, sc, sh, g.reshape(1, d), w_bf)


def _bias_tile_kernel(tab_ref, o_ref, *, qb):
    _, rows, width = o_ref.shape
    r = lax.broadcasted_iota(jnp.int32, (rows, width), 0)
    koff = lax.broadcasted_iota(jnp.int32, (rows, width), 1) - LANES
    rel = koff - r
    n = jnp.abs(rel)
    large = jnp.full((rows, width), _MAX_EXACT, jnp.int32)
    for th in _BUCKET_THS:
        large = large + jnp.where(n >= th, 1, 0)
    bucket = jnp.where(rel > 0, _NB, 0) + jnp.where(n < _MAX_EXACT, n, large)
    visible = ((koff >> 6) <= (r >> 6)) & (koff < qb)
    for h in range(N_ATTN_HEADS):
        b = jnp.zeros((rows, width), F32)
        for bk in range(N_REL_BUCKETS):
            b = jnp.where(bucket == bk, tab_ref[bk, h], b)
        o_ref[h] = jnp.where(visible, b, NEG_BIG)


def _bias_tile_call(rel_bias, qb):
    qbr = -(-qb // LANES) * LANES
    return pl.pallas_call(
        functools.partial(_bias_tile_kernel, qb=qb),
        in_specs=[pl.BlockSpec(memory_space=pltpu.SMEM)],
        out_specs=pl.BlockSpec(memory_space=pltpu.VMEM),
        out_shape=jax.ShapeDtypeStruct((N_ATTN_HEADS, qb, LANES + qbr), F32),
        name="bias_tiles",
    )(rel_bias)


def _sweep_far(q0, qb, kb, fn):
    far_end = jnp.maximum(q0 - LANES, 0)
    n_full = far_end // kb
    sub = kb // LANES

    def whole(j):
        fn(pl.multiple_of(j * kb, kb), j * sub, None)

    def pair(j, carry):
        whole(2 * j)
        whole(2 * j + 1)
        return carry

    lax.fori_loop(0, n_full // 2, pair, 0)
    pl.when(n_full % 2 == 1)(lambda: whole(n_full - 1))

    @pl.when(far_end > n_full * kb)
    def _():
        start = pl.multiple_of(n_full * kb, kb)
        col = start + lax.broadcasted_iota(jnp.int32, (qb, LANES), 1)
        fn(start, n_full * sub, [jnp.where(col + c * LANES < far_end, 0.0, NEG_BIG) for c in range(sub)])


def _key_rows(ref, start, width):
    blk = ref[pl.ds(start, width), :].astype(BF16)
    pad = -blk.shape[1] % LANES
    return blk if pad == 0 else jnp.concatenate([blk, jnp.zeros((width, pad), BF16)], axis=1)


def _dsa_kernel(tab_ref, q_ref, qi_ref, iw_ref, k_ref, v_ref, ki_ref, kc_ref, vc_ref, kic_ref, tile_ref, o_ref,
                keys_scr, half_scr, thr_scr, mfar_scr, mnear_scr, l_scr, acc_scr, *, qb, kb, qoff, topk):
    i = pl.program_id(1)
    q0 = qoff + i * qb
    qbr = tile_ref.shape[2] - LANES
    sub = kb // LANES
    w = iw_ref[:, :IDX_HEADS] * (IDX_HEADS ** -0.5 * IDX_DIM ** -0.5)

    def score_keys(kib):
        width = kib.shape[0]
        acc = jnp.zeros((qb, width), F32)
        for h in range(IDX_HEADS):
            d = lax.dot_general(qi_ref[:, h * LANES:(h + 1) * LANES], kib, NT_DIMS,
                                preferred_element_type=F32)
            acc = acc + w[:, h:h + 1] * jnp.maximum(d, 0.0)
        bits = pltpu.bitcast(acc, jnp.int32)
        return bits ^ ((bits >> 31) & 0x7FFFFFFF)

    n_score = (q0 + kb - 1) // kb

    def store_keys(slab0, key):
        for c in range(key.shape[1] // LANES):
            part = key[:, c * LANES:(c + 1) * LANES]
            keys_scr[slab0 + c] = part
            half_scr[slab0 + c] = (part >> 16).astype(jnp.int16)

    def score_body(j, carry):
        store_keys(j * sub, score_keys(_key_rows(ki_ref, pl.multiple_of(j * kb, kb), kb)))
        return carry

    lax.fori_loop(0, n_score, score_body, 0)

    cur_blk = q0 // LANES
    key = score_keys(kic_ref[...])
    store_keys(cur_blk, jnp.where(tile_ref[0, :, LANES:] > 0.5 * NEG_BIG, key, INT_MIN))
    n_blk = cur_blk + qbr // LANES

    rb = min(qb, LANES)
    lo16 = -(2 ** 15)
    n4 = n_blk // 4

    chunks = [pl.ds(c * rb, rb) for c in range(qb // rb)]
    ones = jnp.ones((LANES, LANES), BF16)

    def count_ge(ts):
        tbs = [t.astype(jnp.int16) for t in ts]

        def count(j, cnts):
            return tuple(cnt + jnp.where(half_scr[j, rows, :] >= tb, jnp.int16(1), jnp.int16(0))
                         for cnt, rows, tb in zip(cnts, chunks, tbs))

        def body4(j, cnts):
            for u in range(4):
                cnts = count(4 * j + u, cnts)
            return cnts

        cnts = lax.fori_loop(0, n4, body4, tuple(jnp.zeros((rb, LANES), jnp.int16) for _ in chunks))
        cnts = lax.fori_loop(4 * n4, n_blk, count, cnts)
        return [jnp.dot(cnt.astype(F32).astype(BF16), ones, preferred_element_type=F32) for cnt in cnts]

    def search(needed):
        def bit_body(it, ts):
            cands = [t + (jnp.int32(1) << (15 - it)) for t in ts]
            return tuple(jnp.where(c >= n, cand, t) for c, n, cand, t in zip(count_ge(cands), needed, cands, ts))

        return lax.fori_loop(0, 16, bit_body, tuple(jnp.full((rb, LANES), lo16, jnp.int32) for _ in chunks))

    t_hi = search([float(topk)] * len(chunks))
    top = -lo16 - 1
    above = [jnp.where(t == top, 0.0, c)
             for t, c in zip(t_hi, count_ge([jnp.minimum(t + 1, top) for t in t_hi]))]

    def low_plane(j, carry):
        for rows, tb in zip(chunks, t_hi):
            k32 = keys_scr[j, rows, :]
            low = (k32 & 0xFFFF) + lo16
            half_scr[j, rows, :] = jnp.where((k32 >> 16) == tb, low, lo16).astype(jnp.int16)
        return carry

    lax.fori_loop(0, n_blk, low_plane, 0)
    t_lo = search([topk - a for a in above])
    for rows, th, tl in zip(chunks, t_hi, t_lo):
        thr_scr[rows, :] = jnp.maximum(th * 65536 + (tl - lo16), INT_MIN + 1)

    hd = A_HEAD_DIM

    def select_mask(blk0, n):
        return [jnp.where(keys_scr[blk0 + c] >= thr_scr[...], 0.0, NEG_BIG) for c in range(n)]

    def logits(h, kblk):
        g = h // A_REP
        s = lax.dot_general(q_ref[:, h * hd:(h + 1) * hd], kblk[:, g * hd:(g + 1) * hd], NT_DIMS,
                            preferred_element_type=F32)
        return [s[:, c * LANES:(c + 1) * LANES] for c in range(s.shape[1] // LANES)]

    def tile_cols(h, col0, n):
        return [tile_ref[h, :, col0 + c * LANES:col0 + (c + 1) * LANES] for c in range(n)]

    mfar_scr[...] = jnp.full(mfar_scr.shape, NEG_BIG, F32)
    mnear_scr[...] = jnp.full(mnear_scr.shape, NEG_BIG, F32)

    def far_mask(blk0, colmask):
        am = select_mask(blk0, sub)
        return am if colmask is None else [a + c for a, c in zip(am, colmask)]

    def max_far(start, blk0, colmask):
        kblk = _key_rows(k_ref, start, kb)
        am = far_mask(blk0, colmask)
        for h in range(A_HEADS):
            mfar_scr[h] = functools.reduce(
                jnp.maximum, [s + a for s, a in zip(logits(h, kblk), am)], mfar_scr[h])

    def max_near(kblk, blk0, col0):
        n = kblk.shape[0] // LANES
        am = select_mask(blk0, n)
        for h in range(A_HEADS):
            mnear_scr[h] = functools.reduce(
                jnp.maximum, [s + a + b for s, a, b in zip(logits(h, kblk), am, tile_cols(h, col0, n))],
                mnear_scr[h])

    prev_start = pl.multiple_of(q0 - LANES, LANES)
    _sweep_far(q0, qb, kb, max_far)
    pl.when(q0 > 0)(lambda: max_near(_key_rows(k_ref, prev_start, LANES), cur_blk - 1, 0))
    max_near(kc_ref[...], cur_blk, LANES)

    for h in range(A_HEADS):
        fb = tab_ref[FAR_BUCKET, h]
        m = jnp.maximum(jnp.max(mfar_scr[h], axis=-1, keepdims=True) + fb,
                        jnp.max(mnear_scr[h], axis=-1, keepdims=True))
        mfar_scr[h] = jnp.broadcast_to(fb - m, (qb, LANES))
        mnear_scr[h] = jnp.broadcast_to(-m, (qb, LANES))
    l_scr[...] = jnp.zeros(l_scr.shape, F32)
    acc_scr[...] = jnp.zeros(acc_scr.shape, F32)

    def accumulate(h, p, vblk):
        g = h // A_REP
        l_scr[h] += functools.reduce(jnp.add, p)
        acc_scr[h] += jnp.dot(jnp.concatenate(p, axis=1).astype(BF16), vblk[:, g * hd:(g + 1) * hd],
                              preferred_element_type=F32)

    def acc_far(start, blk0, colmask):
        kblk = _key_rows(k_ref, start, kb)
        vblk = _key_rows(v_ref, start, kb)
        am = far_mask(blk0, colmask)
        for h in range(A_HEADS):
            shift = mfar_scr[h]
            accumulate(h, [jnp.exp(s + a + shift) for s, a in zip(logits(h, kblk), am)], vblk)

    def acc_near(kblk, vblk, blk0, col0):
        n = kblk.shape[0] // LANES
        am = select_mask(blk0, n)
        for h in range(A_HEADS):
            shift = mnear_scr[h]
            accumulate(h, [jnp.exp(s + a + b + shift) for s, a, b in
                           zip(logits(h, kblk), am, tile_cols(h, col0, n))], vblk)

    _sweep_far(q0, qb, kb, acc_far)
    pl.when(q0 > 0)(lambda: acc_near(_key_rows(k_ref, prev_start, LANES), _key_rows(v_ref, prev_start, LANES),
                                     cur_blk - 1, 0))
    acc_near(kc_ref[...], vc_ref[...], cur_blk, LANES)

    for h in range(A_HEADS):
        o_ref[:, h * hd:(h + 1) * hd] = (
            acc_scr[h] / jnp.sum(l_scr[h], axis=-1, keepdims=True)).astype(o_ref.dtype)


def _dsa_call(rel_bias, tiles, q, qi, iw, far, cur, *, qb, kb, qoff, topk):
    b, t, _ = q.shape
    lf = far[0].shape[1]
    qbr = tiles.shape[2] - LANES
    q0_max = qoff + t - qb
    far_end = -(-q0_max // kb) * kb
    assert t % qb == 0 and qoff % LANES == 0 and far_end <= lf and cur[0].shape[1] == (t // qb) * qbr
    n_slabs = max(far_end, q0_max + qbr) // LANES
    mode = {"pipeline_mode": pl.Buffered(1)} if t // qb > 1 else {}
    far_spec = lambda a: pl.BlockSpec((None, lf, a.shape[2]), lambda bi, i: (bi, 0, 0), **mode)
    cur_spec = lambda a: pl.BlockSpec((None, qbr, a.shape[2]), lambda bi, i: (bi, i, 0))
    return pl.pallas_call(
        functools.partial(_dsa_kernel, qb=qb, kb=kb, qoff=qoff, topk=topk),
        grid=(b, t // qb),
        in_specs=[pl.BlockSpec(memory_space=pltpu.SMEM),
                  pl.BlockSpec((None, qb, q.shape[2]), lambda bi, i: (bi, i, 0)),
                  pl.BlockSpec((None, qb, qi.shape[2]), lambda bi, i: (bi, i, 0)),
                  pl.BlockSpec((None, qb, iw.shape[2]), lambda bi, i: (bi, i, 0)),
                  far_spec(far[0]), far_spec(far[1]), far_spec(far[2]),
                  cur_spec(cur[0]), cur_spec(cur[1]), cur_spec(cur[2]),
                  _resident(tiles.shape, lambda bi, i: (0, 0, 0))],
        out_specs=pl.BlockSpec((None, qb, A_HEADS * A_HEAD_DIM), lambda bi, i: (bi, i, 0)),
        out_shape=jax.ShapeDtypeStruct((b, t, A_HEADS * A_HEAD_DIM), BF16),
        scratch_shapes=[pltpu.VMEM((n_slabs, qb, LANES), jnp.int32),
                        pltpu.VMEM((n_slabs, qb, LANES), jnp.int16),
                        pltpu.VMEM((qb, LANES), jnp.int32),
                        pltpu.VMEM((A_HEADS, qb, LANES), F32),
                        pltpu.VMEM((A_HEADS, qb, LANES), F32),
                        pltpu.VMEM((A_HEADS, qb, LANES), F32),
                        pltpu.VMEM((A_HEADS, qb, A_HEAD_DIM), F32)],
        compiler_params=_params(2),
        name="mixer_a",
    )(rel_bias, q, qi, iw, *far, *cur, tiles)


def _diff_kernel(tab_ref, lam_ref, sg_ref, q_ref, k_ref, v_ref, kc_ref, vc_ref, tile_ref, o_ref,
                 mfar_scr, mnear_scr, l_scr, acc_scr, *, qb, kb, qoff, lam_init):
    head = pl.program_id(1)
    i = pl.program_id(2)
    q0 = qoff + i * qb
    qbr = tile_ref.shape[2] - LANES
    hd = B_HEAD_DIM
    far_bias = tab_ref[FAR_BUCKET, A_HEADS + head]

    def logits(c, kblk):
        s = lax.dot_general(q_ref[:, c * hd:(c + 1) * hd], kblk[:, c * hd:(c + 1) * hd], NT_DIMS,
                            preferred_element_type=F32)
        return [s[:, u * LANES:(u + 1) * LANES] for u in range(s.shape[1] // LANES)]

    def tile_cols(col0, n):
        return [tile_ref[0, :, col0 + u * LANES:col0 + (u + 1) * LANES] for u in range(n)]

    mfar_scr[...] = jnp.full(mfar_scr.shape, NEG_BIG, F32)
    mnear_scr[...] = jnp.full(mnear_scr.shape, NEG_BIG, F32)

    def max_far(start, blk0, colmask):
        kblk = _key_rows(k_ref, start, kb)
        for c in range(2):
            s = logits(c, kblk)
            if colmask is not None:
                s = [x + m for x, m in zip(s, colmask)]
            mfar_scr[c] = functools.reduce(jnp.maximum, s, mfar_scr[c])

    def max_near(kblk, col0):
        for c in range(2):
            mnear_scr[c] = functools.reduce(
                jnp.maximum, [s + b for s, b in zip(logits(c, kblk), tile_cols(col0, kblk.shape[0] // LANES))],
                mnear_scr[c])

    prev_start = pl.multiple_of(q0 - LANES, LANES)
    _sweep_far(q0, qb, kb, max_far)
    pl.when(q0 > 0)(lambda: max_near(_key_rows(k_ref, prev_start, LANES), 0))
    max_near(kc_ref[...], LANES)

    for c in range(2):
        m = jnp.maximum(jnp.max(mfar_scr[c], axis=-1, keepdims=True) + far_bias,
                        jnp.max(mnear_scr[c], axis=-1, keepdims=True))
        mfar_scr[c] = jnp.broadcast_to(far_bias - m, (qb, LANES))
        mnear_scr[c] = jnp.broadcast_to(-m, (qb, LANES))
    l_scr[...] = jnp.zeros(l_scr.shape, F32)
    acc_scr[...] = jnp.zeros(acc_scr.shape, F32)

    def accumulate(c, p, vblk):
        l_scr[c] += functools.reduce(jnp.add, p)
        acc_scr[c] += jnp.dot(jnp.concatenate(p, axis=1).astype(BF16), vblk, preferred_element_type=F32)

    def acc_far(start, blk0, colmask):
        kblk = _key_rows(k_ref, start, kb)
        vblk = _key_rows(v_ref, start, kb)
        for c in range(2):
            shift = mfar_scr[c]
            shifts = [shift] * (kb // LANES) if colmask is None else [shift + m for m in colmask]
            accumulate(c, [jnp.exp(s + sh) for s, sh in zip(logits(c, kblk), shifts)], vblk)

    def acc_near(kblk, vblk, col0):
        for c in range(2):
            shift = mnear_scr[c]
            accumulate(c, [jnp.exp(s + b + shift) for s, b in
                           zip(logits(c, kblk), tile_cols(col0, kblk.shape[0] // LANES))], vblk)

    _sweep_far(q0, qb, kb, acc_far)
    pl.when(q0 > 0)(lambda: acc_near(_key_rows(k_ref, prev_start, LANES), _key_rows(v_ref, prev_start, LANES), 0))
    acc_near(kc_ref[...], vc_ref[...], LANES)

    lam = lam_ref[0, 0]
    o = (acc_scr[0] / jnp.sum(l_scr[0], axis=-1, keepdims=True)
         - lam * (acc_scr[1] / jnp.sum(l_scr[1], axis=-1, keepdims=True)))
    o = o * lax.rsqrt(jnp.mean(o * o, axis=-1, keepdims=True) + EPS) * sg_ref[...]
    o_ref[...] = (o * (1.0 - lam_init)).astype(o_ref.dtype)


def _diff_call(rel_bias, tiles, lam, subln_g, q, far, cur, *, qb, kb, qoff, lam_init):
    b, t, _ = q.shape
    lf = far[0].shape[1]
    hw = 2 * B_HEAD_DIM
    qbr = tiles.shape[2] - LANES
    q0_max = qoff + t - qb
    far_end = -(-max(q0_max - LANES, 0) // kb) * kb
    assert t % qb == 0 and qoff % LANES == 0 and far_end <= lf and cur[0].shape[1] == (t // qb) * qbr
    far_spec = pl.BlockSpec((None, lf, hw), lambda bi, h, i: (bi, 0, h))
    cur_spec = pl.BlockSpec((None, qbr, hw), lambda bi, h, i: (bi, i, h))
    return pl.pallas_call(
        functools.partial(_diff_kernel, qb=qb, kb=kb, qoff=qoff, lam_init=lam_init),
        grid=(b, B_HEADS, t // qb),
        in_specs=[pl.BlockSpec(memory_space=pltpu.SMEM),
                  pl.BlockSpec(memory_space=pltpu.SMEM),
                  pl.BlockSpec((1, hw), lambda bi, h, i: (0, 0)),
                  pl.BlockSpec((None, qb, hw), lambda bi, h, i: (bi, i, h)),
                  far_spec, far_spec, cur_spec, cur_spec,
                  pl.BlockSpec((1, qb, tiles.shape[2]), lambda bi, h, i: (A_HEADS + h, 0, 0))],
        out_specs=pl.BlockSpec((None, qb, hw), lambda bi, h, i: (bi, i, h)),
        out_shape=jax.ShapeDtypeStruct((b, t, B_HEADS * hw), BF16),
        scratch_shapes=[pltpu.VMEM((2, qb, LANES), F32),
                        pltpu.VMEM((2, qb, LANES), F32),
                        pltpu.VMEM((2, qb, LANES), F32),
                        pltpu.VMEM((2, qb, hw), F32)],
        compiler_params=_params(3),
        name="mixer_b",
    )(rel_bias, lam, subln_g.reshape(1, hw), q, *far, *cur, tiles)


def _lam_kernel(dl_ref, o_ref, *, lam_init):
    dl = dl_ref[...]
    s1 = jnp.sum(dl[0:1] * dl[1:2], axis=-1, keepdims=True)
    s2 = jnp.sum(dl[2:3] * dl[3:4], axis=-1, keepdims=True)
    o_ref[...] = jnp.exp(s1) - jnp.exp(s2) + lam_init


def _lam_call(diff_lam, lam_init):
    return pl.pallas_call(
        functools.partial(_lam_kernel, lam_init=lam_init),
        out_shape=jax.ShapeDtypeStruct((1, 1), F32),
        name="diff_lambda",
    )(diff_lam)


def _route(logits_t, rbias, counts):
    tm = logits_t.shape[1]
    per = N_EXPERTS // N_GROUPS
    scores = jax.nn.sigmoid(logits_t).reshape(N_GROUPS, per, tm)
    ch = scores + rbias.reshape(N_GROUPS, per, 1)
    sub = lax.broadcasted_iota(jnp.int32, ch.shape, 1)
    grp = lax.broadcasted_iota(jnp.int32, ch.shape, 0)
    m1 = jnp.max(ch, axis=1, keepdims=True)
    i1 = jnp.min(jnp.where(ch == m1, sub, per), axis=1, keepdims=True)
    m2 = jnp.max(jnp.where(sub == i1, -jnp.inf, ch), axis=1, keepdims=True)
    gs = jnp.broadcast_to(m1 + m2, ch.shape)
    rank = jnp.zeros(ch.shape, jnp.int32)
    for g2 in range(N_GROUPS):
        o = gs[g2:g2 + 1]
        rank = rank + jnp.where(o > gs, 1, 0) + jnp.where(o == gs, 1, 0) * jnp.where(grp > g2, 1, 0)
    cm = jnp.where(rank < TOPK_GROUPS, ch, -jnp.inf)
    eid = grp * per + sub
    erank = jnp.zeros(cm.shape, jnp.int32)
    for e2 in range(N_EXPERTS):
        o = cm[e2 // per:e2 // per + 1, e2 % per:e2 % per + 1, :]
        erank = erank + jnp.where(o > cm, 1, 0) + jnp.where(o == cm, 1, 0) * jnp.where(eid > e2, 1, 0)
    chosen = erank < TOP_K
    sel = jnp.where(chosen, scores, 0.0)
    denom = jnp.sum(jnp.sum(sel, axis=1, keepdims=True), axis=0, keepdims=True)
    gates = sel / denom * ROUTED_SCALE

    ones = jnp.where(chosen, 1.0, 0.0).reshape(N_EXPERTS, tm)
    earlier = (lax.broadcasted_iota(jnp.int32, (tm, tm), 0) < lax.broadcasted_iota(jnp.int32, (tm, tm), 1))
    before = jnp.dot(ones.astype(BF16), jnp.where(earlier, 1.0, 0.0).astype(BF16), preferred_element_type=F32)
    pos = (before + counts.astype(F32)).reshape(N_GROUPS, per, tm)
    new_counts = counts + jnp.sum(ones, axis=1, keepdims=True).astype(jnp.int32)

    def per_slot(values):
        rows = [jnp.sum(jnp.sum(jnp.where(erank == k, values, 0.0), axis=1, keepdims=True), axis=0)
                for k in range(TOP_K)]
        return jnp.concatenate(rows, axis=0)

    return (per_slot(eid.astype(F32)).astype(jnp.int32), per_slot(pos).astype(jnp.int32), per_slot(gates),
            new_counts)


def _pack_pair(a, b):
    ua = pltpu.bitcast(a.astype(BF16).astype(F32), jnp.uint32)
    ub = pltpu.bitcast(b.astype(BF16).astype(F32), jnp.uint32)
    return pltpu.bitcast(ua | (ub >> 16), jnp.int32)


def _unpack_pair(p):
    u = pltpu.bitcast(p, jnp.uint32)
    a = pltpu.bitcast(u & jnp.uint32(0xFFFF0000), F32)
    b = pltpu.bitcast(u << 16, F32)
    return a.astype(BF16), b.astype(BF16)


PARTS = 4
PART_W = D_MODEL // 2 // PARTS


def _pack_rows(x):
    half = x.shape[1] // 2
    return jnp.stack([_pack_pair(x[:, j * PART_W:(j + 1) * PART_W],
                                 x[:, half + j * PART_W:half + (j + 1) * PART_W]) for j in range(PARTS)])


def _packed_matmul(parts_ref, w_ref):
    half = w_ref.shape[0] // 2
    acc = None
    for j in range(PARTS):
        a, b = _unpack_pair(parts_ref[j])
        t = (jnp.dot(a, w_ref[j * PART_W:(j + 1) * PART_W, :], preferred_element_type=F32)
             + jnp.dot(b, w_ref[half + j * PART_W:half + (j + 1) * PART_W, :], preferred_element_type=F32))
        acc = t if acc is None else acc + t
    return acc


def _outproj_kernel(x_ref, oa_ref, ob_ref, ga_ref, sc_ref, sh_ref, g_ref, w_ref, wr_ref, rb_ref, cin_ref,
                    x1_ref, h2_ref, ek_ref, pk_ref, gk_ref, cout_ref, cnt_scr):
    @pl.when(pl.program_id(0) == 0)
    def _():
        cnt_scr[...] = cin_ref[...]

    half = oa_ref.shape[1]
    mix = (jnp.dot(oa_ref[...], w_ref[:half, :], preferred_element_type=F32)
           + jnp.dot(ob_ref[...], w_ref[half:, :], preferred_element_type=F32))
    x1 = x_ref[...] + ga_ref[...] * mix
    x1_ref[...] = x1
    h2 = _modulated_norm(x1, g_ref[...], sc_ref[...], sh_ref[...])
    h2_ref[...] = _pack_rows(h2)
    logits_t = lax.dot_general(wr_ref[...], h2, NT_DIMS, preferred_element_type=F32,
                               precision=lax.Precision.HIGHEST)
    ek, pk, gk, counts = _route(logits_t, rb_ref[...], cnt_scr[...])
    ek_ref[...] = ek
    pk_ref[...] = pk
    gk_ref[...] = gk
    cnt_scr[...] = counts
    cout_ref[...] = counts


def _outproj_call(x2d, oa, ob, ga, sc, sh, mod_map, g, w_bf, wr_t, rbias, counts_in, tm):
    n, d = x2d.shape
    assert n % tm == 0
    r = sc.shape[1]
    mod_spec = pl.BlockSpec((None, r, d), mod_map)
    slot_spec = pl.BlockSpec((TOP_K, tm), lambda i: (0, i))
    return pl.pallas_call(
        _outproj_kernel,
        grid=(n // tm,),
        in_specs=[pl.BlockSpec((tm, d), lambda i: (i, 0)),
                  pl.BlockSpec((tm, oa.shape[1]), lambda i: (i, 0)),
                  pl.BlockSpec((tm, ob.shape[1]), lambda i: (i, 0)),
                  mod_spec, mod_spec, mod_spec,
                  pl.BlockSpec((1, d), lambda i: (0, 0)),
                  _resident(w_bf.shape, lambda i: (0, 0)),
                  _resident(wr_t.shape, lambda i: (0, 0)),
                  pl.BlockSpec((N_EXPERTS, 1), lambda i: (0, 0)),
                  pl.BlockSpec((N_EXPERTS, 1), lambda i: (0, 0))],
        out_specs=[pl.BlockSpec((tm, d), lambda i: (i, 0)),
                   pl.BlockSpec((PARTS, tm, PART_W), lambda i: (0, i, 0)),
                   slot_spec, slot_spec, slot_spec,
                   pl.BlockSpec((N_EXPERTS, 1), lambda i: (0, 0))],
        out_shape=[jax.ShapeDtypeStruct((n, d), F32),
                   jax.ShapeDtypeStruct((PARTS, n, PART_W), jnp.int32),
                   jax.ShapeDtypeStruct((TOP_K, n), jnp.int32),
                   jax.ShapeDtypeStruct((TOP_K, n), jnp.int32),
                   jax.ShapeDtypeStruct((TOP_K, n), F32),
                   jax.ShapeDtypeStruct((N_EXPERTS, 1), jnp.int32)],
        scratch_shapes=[pltpu.VMEM((N_EXPERTS, 1), jnp.int32)],
        compiler_params=_params(1),
        name="out_proj_router",
    )(x2d, oa, ob, ga, sc, sh, g.reshape(1, d), w_bf, wr_t, rbias, counts_in)


SC_WINDOW = 128
EXPERT_TILE = 512


def _sc_mesh():
    return plsc.VectorSubcoreMesh(core_axis_name="core", subcore_axis_name="subcore")


def _sc_scatter_rows(x, idx, n_out):
    n, w = x.shape
    steps = n // SC_WINDOW
    total = idx.shape[1] // SC_WINDOW

    @pl.kernel(out_type=jax.ShapeDtypeStruct((n_out, w), x.dtype), mesh=_sc_mesh(), scratch_types=[],
               name="moe_dispatch")
    def scatter(x_hbm, i_hbm, o_hbm):
        def body(x_vmem, i_vmem):
            pltpu.sync_copy(x_vmem, o_hbm.at[i_vmem.at[0]])

        pltpu.emit_pipeline(
            body, grid=(total,),
            in_specs=[pl.BlockSpec((SC_WINDOW, w), lambda i: (i % steps, 0)),
                      pl.BlockSpec((1, SC_WINDOW), lambda i: (0, i))],
            out_specs=[], core_axis_name=("core", "subcore"),
            dimension_semantics=(pltpu.PARALLEL,))(x_hbm, i_hbm)

    return scatter(x, idx)


def _sc_gather_rows(y, idx):
    w = y.shape[1]
    n = idx.shape[1]

    @pl.kernel(out_type=jax.ShapeDtypeStruct((n, w), y.dtype), mesh=_sc_mesh(), scratch_types=[],
               name="moe_collect")
    def gather(y_hbm, i_hbm, o_hbm):
        def body(i_vmem, o_vmem):
            pltpu.sync_copy(y_hbm.at[i_vmem.at[0]], o_vmem)

        pltpu.emit_pipeline(
            body, grid=(n // SC_WINDOW,),
            in_specs=[pl.BlockSpec((1, SC_WINDOW), lambda i: (0, i))],
            out_specs=[pl.BlockSpec((SC_WINDOW, w), lambda i: (i, 0))],
            core_axis_name=("core", "subcore"),
            dimension_semantics=(pltpu.PARALLEL,))(i_hbm, o_hbm)

    return gather(y, idx)


def _swiglu(parts_ref, wg_ref, wu_ref, wd_ref):
    a = _packed_matmul(parts_ref, wg_ref)
    u = _packed_matmul(parts_ref, wu_ref)
    hid = (a * jax.nn.sigmoid(a) * u).astype(BF16)
    return jnp.dot(hid, wd_ref[...], preferred_element_type=F32)


def _expert_kernel(te_ref, nu_ref, x_ref, wg_ref, wu_ref, wd_ref, y_ref, wg_scr, wu_scr, wd_scr):
    i = pl.program_id(0)

    @pl.when(i < nu_ref[0])
    def _():
        @pl.when((i == 0) | (te_ref[i] != te_ref[jnp.maximum(i - 1, 0)]))
        def _():
            wg_scr[...] = wg_ref[...].astype(BF16)
            wu_scr[...] = wu_ref[...].astype(BF16)
            wd_scr[...] = wd_ref[...].astype(BF16)

        y_ref[...] = _pack_rows(_swiglu(x_ref, wg_scr, wu_scr, wd_scr))


def _expert_call(tile_expert, n_used, x_sorted, wg, wu, wd):
    _, r, _ = x_sorted.shape
    ne, d, de = wg.shape
    n_tiles = r // EXPERT_TILE
    row_map = lambda i, te, nu: (0, jnp.minimum(i, nu[0] - 1), 0)
    return pl.pallas_call(
        _expert_kernel,
        grid_spec=pltpu.PrefetchScalarGridSpec(
            num_scalar_prefetch=2,
            grid=(n_tiles,),
            in_specs=[pl.BlockSpec((PARTS, EXPERT_TILE, PART_W), row_map),
                      pl.BlockSpec((None, d, de), lambda i, te, nu: (te[i], 0, 0)),
                      pl.BlockSpec((None, d, de), lambda i, te, nu: (te[i], 0, 0)),
                      pl.BlockSpec((None, de, d), lambda i, te, nu: (te[i], 0, 0))],
            out_specs=pl.BlockSpec((PARTS, EXPERT_TILE, PART_W), row_map),
            scratch_shapes=[pltpu.VMEM((d, de), BF16), pltpu.VMEM((d, de), BF16), pltpu.VMEM((de, d), BF16)]),
        out_shape=jax.ShapeDtypeStruct(x_sorted.shape, jnp.int32),
        compiler_params=_params(1),
        name="moe_experts",
    )(tile_expert, n_used, x_sorted, wg, wu, wd)


def _final_kernel(x_ref, h_ref, rows_ref, gk_ref, gf_ref, g_ref, sg_ref, su_ref, sd_ref, o_ref):
    moe = _swiglu(h_ref, sg_ref, su_ref, sd_ref)
    half = moe.shape[1] // 2
    lo = [moe[:, j * PART_W:(j + 1) * PART_W] for j in range(PARTS)]
    hi = [moe[:, half + j * PART_W:half + (j + 1) * PART_W] for j in range(PARTS)]
    for k in range(TOP_K):
        gate = gk_ref[:, k:k + 1]
        for j in range(PARTS):
            a, b = _unpack_pair(rows_ref[k, j])
            lo[j] = lo[j] + gate * a.astype(F32)
            hi[j] = hi[j] + gate * b.astype(F32)
    x = x_ref[...] + gf_ref[...] * jnp.concatenate(lo + hi, axis=1)
    o_ref[...] = x * lax.rsqrt(jnp.mean(x * x, axis=-1, keepdims=True) + EPS) * g_ref[...]


def _final_call(x1, h2p, rows, row_block0, gk_t, gf, mod_map, g, sg, su, sd, tm):
    n, d = x1.shape
    assert n % tm == 0
    r = gf.shape[1]
    return pl.pallas_call(
        _final_kernel,
        grid=(n // tm,),
        in_specs=[pl.BlockSpec((tm, d), lambda i: (i, 0)),
                  pl.BlockSpec((PARTS, tm, PART_W), lambda i: (0, i, 0)),
                  pl.BlockSpec((TOP_K, PARTS, tm, PART_W), lambda i: (0, 0, i + row_block0, 0)),
                  pl.BlockSpec((tm, TOP_K), lambda i: (i, 0)),
                  pl.BlockSpec((None, r, d), mod_map),
                  pl.BlockSpec((1, d), lambda i: (0, 0)),
                  _resident(sg.shape, lambda i: (0, 0)),
                  _resident(su.shape, lambda i: (0, 0)),
                  _resident(sd.shape, lambda i: (0, 0))],
        out_specs=pl.BlockSpec((tm, d), lambda i: (i, 0)),
        out_shape=jax.ShapeDtypeStruct((n, d), F32),
        compiler_params=_params(1),
        name="final_norm",
    )(x1, h2p, rows, gk_t, gf, g.reshape(1, d), sg, su, sd)


_QA, _KA, _VA, _QI, _KI, _IW, _QB, _KB, _VB = range(9)
_IN_SIZES = (A_HEADS * A_HEAD_DIM, A_KV_HEADS * A_HEAD_DIM, A_KV_HEADS * A_HEAD_DIM,
             IDX_HEADS * IDX_DIM, IDX_DIM, IDX_HEADS,
             B_HEADS * 2 * B_HEAD_DIM, B_HEADS * 2 * B_HEAD_DIM, B_HEADS * 2 * B_HEAD_DIM)


def _split_w_in(w_in):
    d = w_in.shape[0]
    offs = np.concatenate([[0], np.cumsum(_IN_SIZES)])
    cols = [w_in[:, offs[k]:offs[k + 1]].astype(BF16) for k in range(9)]
    zero = lambda n: jnp.zeros((d, n), BF16)
    qi = jnp.concatenate([cols[_QI].reshape(d, IDX_HEADS, IDX_DIM),
                          jnp.zeros((d, IDX_HEADS, LANES - IDX_DIM), BF16)], axis=-1).reshape(d, IDX_HEADS * LANES)
    w_a = jnp.concatenate([cols[_QA], cols[_KA], cols[_VA], qi,
                           cols[_KI], zero(LANES - IDX_DIM), cols[_IW], zero(LANES - IDX_HEADS)], axis=1)
    w_b = jnp.concatenate([cols[_QB], cols[_KB], cols[_VB]], axis=1)
    return w_a, w_b


def _group_forward(x, mod, past, rel_bias, lam, weights, *, lam_init, qb, kb, tm, per_row_mod):
    (norm_a_g, w_a, w_b, subln_g) = weights
    b, t, d = x.shape
    n = b * t
    x2d = x.reshape(n, d)
    sh_a, sc_a = mod[0], mod[1]
    if per_row_mod:
        expand = lambda m: jnp.repeat(m, t, axis=0).reshape(1, n, d)
        mod_map = lambda i: (0, i, 0)
    else:
        expand = lambda m: m.reshape(b, 1, d)
        mod_map = lambda i: ((i * tm) // t, 0, 0)
    sc3, sh3 = expand(sc_a), expand(sh_a)

    na = A_HEADS * A_HEAD_DIM
    nk = A_KV_HEADS * A_HEAD_DIM
    nq = IDX_HEADS * LANES
    c_ka, c_va, c_qi = na, na + nk, na + 2 * nk
    c_ki, c_iw = c_qi + nq, c_qi + nq + LANES
    defs_a = [(0, na, na, A_HEAD_DIM ** -0.5), (c_ka, nk, nk, None), (c_ka, nk, nk, None),
              (c_va, nk, nk, None), (c_va, nk, nk, None), (c_qi, nq, nq, None),
              (c_ki, LANES, IDX_DIM, None), (c_ki, LANES, LANES, None), (c_iw, LANES, LANES, None)]
    dt_a = [BF16, F32, BF16, F32, BF16, BF16, F32, BF16, F32]
    qa, ka, ka_bf, va, va_bf, qi, ki, ki_bf, iw = _norm_proj_call(
        x2d, sc3, sh3, mod_map, norm_a_g, w_a, defs_a, dt_a, tm, "in_proj_a")
    nb = B_HEADS * 2 * B_HEAD_DIM
    defs_b = [(0, nb, nb, B_HEAD_DIM ** -0.5), (nb, nb, nb, None), (nb, nb, nb, None),
              (2 * nb, nb, nb, None), (2 * nb, nb, nb, None)]
    dt_b = [BF16, F32, BF16, F32, BF16]
    qbm, kbm, kb_bf, vbm, vb_bf = _norm_proj_call(
        x2d, sc3, sh3, mod_map, norm_a_g, w_b, defs_b, dt_b, tm, "in_proj_b")

    r3 = lambda a: a.reshape(b, t, a.shape[-1])
    cur = [r3(ka_bf), r3(va_bf), r3(ki_bf), r3(kb_bf), r3(vb_bf)]
    if past is None:
        qoff = 0
        far = cur
    else:
        assert t == qb, "a group with cached keys is one query block per batch row"
        qoff = past[0].shape[1]
        qbr = -(-qb // LANES) * LANES
        cur = [jnp.concatenate([a, jnp.zeros((b, qbr - t, a.shape[-1]), BF16)], axis=1) for a in cur]
        far = [p.reshape(b, qoff, -1) for p in past]
        short = -qoff % kb
        if short:
            far = [jnp.concatenate([p, jnp.zeros((b, short, p.shape[-1]), p.dtype)], axis=1) for p in far]
    length = qoff + t
    topk = min(TOPK_MAX, length // 4)

    tiles = _bias_tile_call(rel_bias, qb)
    out_a = _dsa_call(rel_bias, tiles, r3(qa), r3(qi), r3(iw), far[:3], cur[:3], qb=qb, kb=kb, qoff=qoff, topk=topk)
    out_b = _diff_call(rel_bias, tiles, lam, subln_g, r3(qbm), far[3:], cur[3:], qb=qb, kb=kb, qoff=qoff,
                       lam_init=lam_init)
    rows = (ka.reshape(1, b, t, A_KV_HEADS, A_HEAD_DIM), va.reshape(1, b, t, A_KV_HEADS, A_HEAD_DIM),
            ki.reshape(1, b, t, IDX_DIM), kbm.reshape(1, b, t, B_HEADS, 2, B_HEAD_DIM),
            vbm.reshape(1, b, t, B_HEADS, 2 * B_HEAD_DIM))
    return x2d, out_a.reshape(n, -1), out_b.reshape(n, -1), rows, expand, mod_map


def kernel(x_prompt, x_sample, c_prompt, c_sample, cache_a_k, cache_a_v, cache_a_kidx, cache_b_k, cache_b_v,
           rel_bias, w_ada, b_ada, norm_a_g, w_in, w_out, diff_lam, subln_g, norm_f_g, w_router, router_bias,
           w_gate, w_up, w_down, ws_gate, ws_up, ws_down, final_g):
    assert w_ada.shape[0] == 1, "single-layer model"
    d = x_prompt.shape[-1]
    bp, tp, _ = x_prompt.shape
    bs, ts, _ = x_sample.shape
    lam_init = 0.8 - 0.6 * math.exp(-0.3 * 0)

    c_all = jnp.concatenate([c_prompt, c_sample], axis=0)
    rows = -(-c_all.shape[0] // 8) * 8
    c_pad = jnp.concatenate([c_all, jnp.zeros((rows - c_all.shape[0], d), F32)], axis=0)
    mod_all = _ada_call(c_pad, w_ada[0], b_ada[0])
    mod_p = [mod_all[:bp, k * d:(k + 1) * d] for k in range(6)]
    mod_s = [mod_all[bp:bp + bs, k * d:(k + 1) * d] for k in range(6)]

    w_a, w_b = _split_w_in(w_in[0])
    lam = _lam_call(diff_lam[0], lam_init)
    weights = (norm_a_g[0], w_a, w_b, subln_g[0])
    past = (cache_a_k[0], cache_a_v[0], cache_a_kidx[0], cache_b_k[0], cache_b_v[0])

    tm_p = 256
    tm_s = bs * ts
    xp2, oa_p, ob_p, rows_p, exp_p, map_p = _group_forward(
        x_prompt, mod_p, None, rel_bias, lam, weights, lam_init=lam_init, qb=256, kb=512, tm=tm_p,
        per_row_mod=False)
    xs2, oa_s, ob_s, rows_s, exp_s, map_s = _group_forward(
        x_sample, mod_s, past, rel_bias, lam, weights, lam_init=lam_init, qb=ts, kb=512, tm=tm_s,
        per_row_mod=True)

    w_out_bf = w_out[0].astype(BF16)
    wr_t = w_router[0].T
    rb = router_bias[0].reshape(N_EXPERTS, 1)
    no_rows = jnp.zeros((N_EXPERTS, 1), jnp.int32)
    x1_p, h2_p, ek_p, pk_p, gk_p, cnt_p = _outproj_call(
        xp2, oa_p, ob_p, exp_p(mod_p[2]), exp_p(mod_p[4]), exp_p(mod_p[3]), map_p, norm_f_g[0], w_out_bf, wr_t, rb,
        no_rows, tm_p)
    x1_s, h2_s, ek_s, pk_s, gk_s, cnt_all = _outproj_call(
        xs2, oa_s, ob_s, exp_s(mod_s[2]), exp_s(mod_s[4]), exp_s(mod_s[3]), map_s, norm_f_g[0], w_out_bf, wr_t, rb,
        cnt_p, tm_s)

    n_p, n_s = bp * tp, bs * ts
    n_all = n_p + n_s
    n_rows = -(-(n_all * TOP_K) // EXPERT_TILE) * EXPERT_TILE + N_EXPERTS * EXPERT_TILE
    counts = cnt_all[:, 0]
    padded = (counts + EXPERT_TILE - 1) // EXPERT_TILE * EXPERT_TILE
    ends = jnp.cumsum(padded)
    starts = ends - padded
    tile_first_row = jnp.arange(n_rows // EXPERT_TILE, dtype=jnp.int32) * EXPERT_TILE
    tile_expert = jnp.minimum(jnp.sum(ends[None, :] <= tile_first_row[:, None], axis=1), N_EXPERTS - 1).astype(jnp.int32)
    n_used = (ends[-1:] // EXPERT_TILE).astype(jnp.int32)
    ek = jnp.concatenate([ek_p, ek_s], axis=1)
    expert_ids = jnp.arange(N_EXPERTS, dtype=jnp.int32)[:, None, None]
    dest = jnp.sum(jnp.where(ek[None] == expert_ids, starts[:, None, None], 0), axis=0) \
        + jnp.concatenate([pk_p, pk_s], axis=1)
    piece = jnp.arange(PARTS, dtype=jnp.int32)[None, :, None]
    idx = (dest[:, None, :] + piece * n_rows).reshape(1, TOP_K * PARTS * n_all).astype(jnp.int32)

    h2_all = jnp.concatenate([h2_p, h2_s], axis=1).reshape(PARTS * n_all, PART_W)
    x_sorted = _sc_scatter_rows(h2_all, idx, PARTS * n_rows).reshape(PARTS, n_rows, PART_W)
    y_sorted = _expert_call(tile_expert, n_used, x_sorted, w_gate[0], w_up[0], w_down[0])
    rows = _sc_gather_rows(y_sorted.reshape(PARTS * n_rows, PART_W), idx).reshape(TOP_K, PARTS, n_all, PART_W)

    shared = (ws_gate[0].astype(BF16), ws_up[0].astype(BF16), ws_down[0].astype(BF16))
    assert n_p % tm_s == 0
    y_p = _final_call(x1_p, h2_p, rows, 0, gk_p.T, exp_p(mod_p[5]), map_p, final_g, *shared, tm_p)
    y_s = _final_call(x1_s, h2_s, rows, n_p // tm_s, gk_s.T, exp_s(mod_s[5]), map_s, final_g, *shared, tm_s)
    return (y_p.reshape(bp, tp, d), y_s.reshape(bs, ts, d)) + rows_p + rows_s
```

```python
import functools
import math

import numpy as np
import jax
import jax.numpy as jnp
from jax import lax
from jax.experimental import pallas as pl
from jax.experimental.pallas import tpu as pltpu
from jax.experimental.pallas import tpu_sc as plsc

D_MODEL = 2048
CHUNK = 64
A_HEADS = 8
A_KV_HEADS = 2
A_REP = A_HEADS // A_KV_HEADS
A_HEAD_DIM = 128
IDX_HEADS = 16
IDX_DIM = 64
TOPK_MAX = 256
B_HEADS = 4
B_HEAD_DIM = 128
N_REL_BUCKETS = 32
REL_MAX_DIST = 128
N_ATTN_HEADS = A_HEADS + B_HEADS
N_EXPERTS = 64
TOP_K = 8
N_GROUPS = 8
TOPK_GROUPS = 4
EXPERT_DIM = 512
SHARED_DIM = 512
ROUTED_SCALE = 2.5
EPS = 1e-6

LANES = 128
INT_MIN = -(2 ** 31)
NEG_BIG = -1e30
VMEM_LIMIT = 52 * 1024 * 1024

F32 = jnp.float32
BF16 = jnp.bfloat16
NT_DIMS = (((1,), (1,)), ((), ()))


def _params(n_axes):
    return pltpu.CompilerParams(dimension_semantics=("arbitrary",) * n_axes,
                                vmem_limit_bytes=VMEM_LIMIT)


def _resident(shape, index_map):
    return pl.BlockSpec(shape, index_map, pipeline_mode=pl.Buffered(1))


def _bucket_thresholds():
    nb = N_REL_BUCKETS // 2
    max_exact = nb // 2
    n = np.arange(max_exact, 4 * REL_MAX_DIST, dtype=np.int32)
    nf = n.astype(np.float32)
    large = max_exact + (np.log(nf / np.float32(max_exact)) / np.float32(math.log(REL_MAX_DIST / max_exact))
                         * np.float32(nb - max_exact)).astype(np.int32)
    large = np.minimum(large, nb - 1)
    assert np.all(np.diff(large) >= 0)
    ths = [int(n[np.argmax(large >= b)]) for b in range(max_exact + 1, nb)]
    return max_exact, nb, ths


_MAX_EXACT, _NB, _BUCKET_THS = _bucket_thresholds()
assert _BUCKET_THS[-1] <= LANES + 1
FAR_BUCKET = _NB - 1


def _ada_kernel(c_ref, w_ref, b_ref, o_ref):
    c = c_ref[...]
    s = c * jax.nn.sigmoid(c)
    o_ref[...] = jnp.dot(s, w_ref[...], preferred_element_type=F32,
                         precision=lax.Precision.HIGHEST) + b_ref[...]


def _ada_call(c, w_ada, b_ada):
    rows, d = c.shape
    n = w_ada.shape[1]
    tn = 1536
    assert n % tn == 0
    return pl.pallas_call(
        _ada_kernel,
        grid=(n // tn,),
        in_specs=[pl.BlockSpec((rows, d), lambda j: (0, 0)),
                  pl.BlockSpec((d, tn), lambda j: (0, j)),
                  pl.BlockSpec((1, tn), lambda j: (0, j))],
        out_specs=pl.BlockSpec((rows, tn), lambda j: (0, j)),
        out_shape=jax.ShapeDtypeStruct((rows, n), F32),
        compiler_params=_params(1),
        name="ada_mod",
    )(c, w_ada, b_ada.reshape(1, n))


def _modulated_norm(x, g, sc, sh):
    y = x * lax.rsqrt(jnp.mean(x * x, axis=-1, keepdims=True) + EPS) * g
    return y * (1.0 + sc) + sh


def _norm_proj_kernel(x_ref, sc_ref, sh_ref, g_ref, w_ref, *out_refs, defs):
    h = _modulated_norm(x_ref[...], g_ref[...], sc_ref[...], sh_ref[...]).astype(BF16)
    done = {}
    for (start, width, store_w, mult), o_ref in zip(defs, out_refs):
        if (start, width) not in done:
            done[(start, width)] = jnp.dot(h, w_ref[:, start:start + width], preferred_element_type=F32)
        r = done[(start, width)][:, :store_w]
        o_ref[...] = (r if mult is None else r * mult).astype(o_ref.dtype)


def _norm_proj_call(x2d, sc, sh, mod_map, g, w_bf, defs, dtypes, tm, name):
    n, d = x2d.shape
    assert n % tm == 0
    r = sc.shape[1]
    out_shapes = [jax.ShapeDtypeStruct((n, d_[2]), dt) for d_, dt in zip(defs, dtypes)]
    out_specs = [pl.BlockSpec((tm, d_[2]), lambda i: (i, 0)) for d_ in defs]
    return pl.pallas_call(
        functools.partial(_norm_proj_kernel, defs=tuple(defs)),
        grid=(n // tm,),
        in_specs=[pl.BlockSpec((tm, d), lambda i: (i, 0)),
                  pl.BlockSpec((None, r, d), mod_map),
                  pl.BlockSpec((None, r, d), mod_map),
                  pl.BlockSpec((1, d), lambda i: (0, 0)),
                  _resident(w_bf.shape, lambda i: (0, 0))],
        out_specs=out_specs,
        out_shape=out_shapes,
        compiler_params=_params(1),
        name=name,
    )(x2d, sc, sh, g.reshape(1, d), w_bf)


def _bias_tile_kernel(tab_ref, o_ref, *, qb):
    _, rows, width = o_ref.shape
    r = lax.broadcasted_iota(jnp.int32, (rows, width), 0)
    koff = lax.broadcasted_iota(jnp.int32, (rows, width), 1) - LANES
    rel = koff - r
    n = jnp.abs(rel)
    large = jnp.full((rows, width), _MAX_EXACT, jnp.int32)
    for th in _BUCKET_THS:
        large = large + jnp.where(n >= th, 1, 0)
    bucket = jnp.where(rel > 0, _NB, 0) + jnp.where(n < _MAX_EXACT, n, large)
    visible = ((koff >> 6) <= (r >> 6)) & (koff < qb)
    for h in range(N_ATTN_HEADS):
        b = jnp.zeros((rows, width), F32)
        for bk in range(N_REL_BUCKETS):
            b = jnp.where(bucket == bk, tab_ref[bk, h], b)
        o_ref[h] = jnp.where(visible, b, NEG_BIG)


def _bias_tile_call(rel_bias, qb):
    qbr = -(-qb // LANES) * LANES
    return pl.pallas_call(
        functools.partial(_bias_tile_kernel, qb=qb),
        in_specs=[pl.BlockSpec(memory_space=pltpu.SMEM)],
        out_specs=pl.BlockSpec(memory_space=pltpu.VMEM),
        out_shape=jax.ShapeDtypeStruct((N_ATTN_HEADS, qb, LANES + qbr), F32),
        name="bias_tiles",
    )(rel_bias)


def _sweep_far(q0, qb, kb, fn):
    far_end = jnp.maximum(q0 - LANES, 0)
    n_full = far_end // kb
    sub = kb // LANES

    def whole(j):
        fn(pl.multiple_of(j * kb, kb), j * sub, None)

    def pair(j, carry):
        whole(2 * j)
        whole(2 * j + 1)
        return carry

    lax.fori_loop(0, n_full // 2, pair, 0)
    pl.when(n_full % 2 == 1)(lambda: whole(n_full - 1))

    @pl.when(far_end > n_full * kb)
    def _():
        start = pl.multiple_of(n_full * kb, kb)
        col = start + lax.broadcasted_iota(jnp.int32, (qb, LANES), 1)
        fn(start, n_full * sub, [jnp.where(col + c * LANES < far_end, 0.0, NEG_BIG) for c in range(sub)])


def _key_rows(ref, start, width):
    blk = ref[pl.ds(start, width), :].astype(BF16)
    pad = -blk.shape[1] % LANES
    return blk if pad == 0 else jnp.concatenate([blk, jnp.zeros((width, pad), BF16)], axis=1)


def _dsa_kernel(tab_ref, q_ref, qi_ref, iw_ref, k_ref, v_ref, ki_ref, kc_ref, vc_ref, kic_ref, tile_ref, o_ref,
                keys_scr, half_scr, thr_scr, mfar_scr, mnear_scr, l_scr, acc_scr, *, qb, kb, qoff, topk):
    i = pl.program_id(1)
    q0 = qoff + i * qb
    qbr = tile_ref.shape[2] - LANES
    sub = kb // LANES
    w = iw_ref[:, :IDX_HEADS] * (IDX_HEADS ** -0.5 * IDX_DIM ** -0.5)

    def score_keys(kib):
        width = kib.shape[0]
        acc = jnp.zeros((qb, width), F32)
        for h in range(IDX_HEADS):
            d = lax.dot_general(qi_ref[:, h * LANES:(h + 1) * LANES], kib, NT_DIMS,
                                preferred_element_type=F32)
            acc = acc + w[:, h:h + 1] * jnp.maximum(d, 0.0)
        bits = pltpu.bitcast(acc, jnp.int32)
        return bits ^ ((bits >> 31) & 0x7FFFFFFF)

    n_score = (q0 + kb - 1) // kb

    def store_keys(slab0, key):
        for c in range(key.shape[1] // LANES):
            part = key[:, c * LANES:(c + 1) * LANES]
            keys_scr[slab0 + c] = part
            half_scr[slab0 + c] = (part >> 16).astype(jnp.int16)

    def score_body(j, carry):
        store_keys(j * sub, score_keys(_key_rows(ki_ref, pl.multiple_of(j * kb, kb), kb)))
        return carry

    lax.fori_loop(0, n_score, score_body, 0)

    cur_blk = q0 // LANES
    key = score_keys(kic_ref[...])
    store_keys(cur_blk, jnp.where(tile_ref[0, :, LANES:] > 0.5 * NEG_BIG, key, INT_MIN))
    n_blk = cur_blk + qbr // LANES

    rb = min(qb, LANES)
    lo16 = -(2 ** 15)
    n4 = n_blk // 4

    chunks = [pl.ds(c * rb, rb) for c in range(qb // rb)]
    ones = jnp.ones((LANES, LANES), BF16)

    def count_ge(ts):
        tbs = [t.astype(jnp.int16) for t in ts]

        def count(j, cnts):
            return tuple(cnt + jnp.where(half_scr[j, rows, :] >= tb, jnp.int16(1), jnp.int16(0))
                         for cnt, rows, tb in zip(cnts, chunks, tbs))

        def body4(j, cnts):
            for u in range(4):
                cnts = count(4 * j + u, cnts)
            return cnts

        cnts = lax.fori_loop(0, n4, body4, tuple(jnp.zeros((rb, LANES), jnp.int16) for _ in chunks))
        cnts = lax.fori_loop(4 * n4, n_blk, count, cnts)
        return [jnp.dot(cnt.astype(F32).astype(BF16), ones, preferred_element_type=F32) for cnt in cnts]

    def search(needed):
        def bit_body(it, ts):
            cands = [t + (jnp.int32(1) << (15 - it)) for t in ts]
            return tuple(jnp.where(c >= n, cand, t) for c, n, cand, t in zip(count_ge(cands), needed, cands, ts))

        return lax.fori_loop(0, 16, bit_body, tuple(jnp.full((rb, LANES), lo16, jnp.int32) for _ in chunks))

    t_hi = search([float(topk)] * len(chunks))
    top = -lo16 - 1
    above = [jnp.where(t == top, 0.0, c)
             for t, c in zip(t_hi, count_ge([jnp.minimum(t + 1, top) for t in t_hi]))]

    def low_plane(j, carry):
        for rows, tb in zip(chunks, t_hi):
            k32 = keys_scr[j, rows, :]
            low = (k32 & 0xFFFF) + lo16
            half_scr[j, rows, :] = jnp.where((k32 >> 16) == tb, low, lo16).astype(jnp.int16)
        return carry

    lax.fori_loop(0, n_blk, low_plane, 0)
    t_lo = search([topk - a for a in above])
    for rows, th, tl in zip(chunks, t_hi, t_lo):
        thr_scr[rows, :] = jnp.maximum(th * 65536 + (tl - lo16), INT_MIN + 1)

    def count32(pred):
        def count(j, cnts):
            return tuple(cnt + jnp.where(pred(keys_scr[j, rows, :], thr_scr[rows, :]), 1, 0)
                         for cnt, rows in zip(cnts, chunks))

        cnts = lax.fori_loop(0, n_blk, count, tuple(jnp.zeros((rb, LANES), jnp.int32) for _ in chunks))
        return [jnp.dot(cnt.astype(F32).astype(BF16), ones, preferred_element_type=F32) for cnt in cnts]

    surplus = functools.reduce(jnp.maximum, [jnp.max(c) for c in count32(lambda k, t: k >= t)]) - topk

    @pl.when(surplus > 0)
    def _():
        keep = [topk - c for c in count32(lambda k, t: k > t)]
        upto = jnp.where(lax.broadcasted_iota(jnp.int32, (LANES, LANES), 0)
                         <= lax.broadcasted_iota(jnp.int32, (LANES, LANES), 1), 1.0, 0.0).astype(BF16)

        def demote(j, seen):
            out = []
            for rows, kp, sn in zip(chunks, keep, seen):
                k32 = keys_scr[j, rows, :]
                tied = k32 == thr_scr[rows, :]
                tied_bf = jnp.where(tied, 1.0, 0.0).astype(BF16)
                rank = sn + jnp.dot(tied_bf, upto, preferred_element_type=F32)
                keys_scr[j, rows, :] = k32 - jnp.where(tied & (rank > kp), 1, 0)
                out.append(sn + jnp.dot(tied_bf, ones, preferred_element_type=F32))
            return tuple(out)

        lax.fori_loop(0, n_blk, demote, tuple(jnp.zeros((rb, LANES), F32) for _ in chunks))

    hd = A_HEAD_DIM

    def select_mask(blk0, n):
        return [jnp.where(keys_scr[blk0 + c] >= thr_scr[...], 0.0, NEG_BIG) for c in range(n)]

    def logits(h, kblk):
        g = h // A_REP
        s = lax.dot_general(q_ref[:, h * hd:(h + 1) * hd], kblk[:, g * hd:(g + 1) * hd], NT_DIMS,
                            preferred_element_type=F32)
        return [s[:, c * LANES:(c + 1) * LANES] for c in range(s.shape[1] // LANES)]

    def tile_cols(h, col0, n):
        return [tile_ref[h, :, col0 + c * LANES:col0 + (c + 1) * LANES] for c in range(n)]

    mfar_scr[...] = jnp.full(mfar_scr.shape, NEG_BIG, F32)
    mnear_scr[...] = jnp.full(mnear_scr.shape, NEG_BIG, F32)

    def far_mask(blk0, colmask):
        am = select_mask(blk0, sub)
        return am if colmask is None else [a + c for a, c in zip(am, colmask)]

    def max_far(start, blk0, colmask):
        kblk = _key_rows(k_ref, start, kb)
        am = far_mask(blk0, colmask)
        for h in range(A_HEADS):
            mfar_scr[h] = functools.reduce(
                jnp.maximum, [s + a for s, a in zip(logits(h, kblk), am)], mfar_scr[h])

    def max_near(kblk, blk0, col0):
        n = kblk.shape[0] // LANES
        am = select_mask(blk0, n)
        for h in range(A_HEADS):
            mnear_scr[h] = functools.reduce(
                jnp.maximum, [s + a + b for s, a, b in zip(logits(h, kblk), am, tile_cols(h, col0, n))],
                mnear_scr[h])

    prev_start = pl.multiple_of(q0 - LANES, LANES)
    _sweep_far(q0, qb, kb, max_far)
    pl.when(q0 > 0)(lambda: max_near(_key_rows(k_ref, prev_start, LANES), cur_blk - 1, 0))
    max_near(kc_ref[...], cur_blk, LANES)

    for h in range(A_HEADS):
        fb = tab_ref[FAR_BUCKET, h]
        m = jnp.maximum(jnp.max(mfar_scr[h], axis=-1, keepdims=True) + fb,
                        jnp.max(mnear_scr[h], axis=-1, keepdims=True))
        mfar_scr[h] = jnp.broadcast_to(fb - m, (qb, LANES))
        mnear_scr[h] = jnp.broadcast_to(-m, (qb, LANES))
    l_scr[...] = jnp.zeros(l_scr.shape, F32)
    acc_scr[...] = jnp.zeros(acc_scr.shape, F32)

    def accumulate(h, p, vblk):
        g = h // A_REP
        l_scr[h] += functools.reduce(jnp.add, p)
        acc_scr[h] += jnp.dot(jnp.concatenate(p, axis=1).astype(BF16), vblk[:, g * hd:(g + 1) * hd],
                              preferred_element_type=F32)

    def acc_far(start, blk0, colmask):
        kblk = _key_rows(k_ref, start, kb)
        vblk = _key_rows(v_ref, start, kb)
        am = far_mask(blk0, colmask)
        for h in range(A_HEADS):
            shift = mfar_scr[h]
            accumulate(h, [jnp.exp(s + a + shift) for s, a in zip(logits(h, kblk), am)], vblk)

    def acc_near(kblk, vblk, blk0, col0):
        n = kblk.shape[0] // LANES
        am = select_mask(blk0, n)
        for h in range(A_HEADS):
            shift = mnear_scr[h]
            accumulate(h, [jnp.exp(s + a + b + shift) for s, a, b in
                           zip(logits(h, kblk), am, tile_cols(h, col0, n))], vblk)

    _sweep_far(q0, qb, kb, acc_far)
    pl.when(q0 > 0)(lambda: acc_near(_key_rows(k_ref, prev_start, LANES), _key_rows(v_ref, prev_start, LANES),
                                     cur_blk - 1, 0))
    acc_near(kc_ref[...], vc_ref[...], cur_blk, LANES)

    for h in range(A_HEADS):
        o_ref[:, h * hd:(h + 1) * hd] = (
            acc_scr[h] / jnp.sum(l_scr[h], axis=-1, keepdims=True)).astype(o_ref.dtype)


def _dsa_call(rel_bias, tiles, q, qi, iw, far, cur, *, qb, kb, qoff, topk):
    b, t, _ = q.shape
    lf = far[0].shape[1]
    qbr = tiles.shape[2] - LANES
    q0_max = qoff + t - qb
    far_end = -(-q0_max // kb) * kb
    assert t % qb == 0 and qoff % LANES == 0 and far_end <= lf and cur[0].shape[1] == (t // qb) * qbr
    n_slabs = max(far_end, q0_max + qbr) // LANES
    mode = {"pipeline_mode": pl.Buffered(1)} if t // qb > 1 else {}
    far_spec = lambda a: pl.BlockSpec((None, lf, a.shape[2]), lambda bi, i: (bi, 0, 0), **mode)
    cur_spec = lambda a: pl.BlockSpec((None, qbr, a.shape[2]), lambda bi, i: (bi, i, 0))
    return pl.pallas_call(
        functools.partial(_dsa_kernel, qb=qb, kb=kb, qoff=qoff, topk=topk),
        grid=(b, t // qb),
        in_specs=[pl.BlockSpec(memory_space=pltpu.SMEM),
                  pl.BlockSpec((None, qb, q.shape[2]), lambda bi, i: (bi, i, 0)),
                  pl.BlockSpec((None, qb, qi.shape[2]), lambda bi, i: (bi, i, 0)),
                  pl.BlockSpec((None, qb, iw.shape[2]), lambda bi, i: (bi, i, 0)),
                  far_spec(far[0]), far_spec(far[1]), far_spec(far[2]),
                  cur_spec(cur[0]), cur_spec(cur[1]), cur_spec(cur[2]),
                  _resident(tiles.shape, lambda bi, i: (0, 0, 0))],
        out_specs=pl.BlockSpec((None, qb, A_HEADS * A_HEAD_DIM), lambda bi, i: (bi, i, 0)),
        out_shape=jax.ShapeDtypeStruct((b, t, A_HEADS * A_HEAD_DIM), BF16),
        scratch_shapes=[pltpu.VMEM((n_slabs, qb, LANES), jnp.int32),
                        pltpu.VMEM((n_slabs, qb, LANES), jnp.int16),
                        pltpu.VMEM((qb, LANES), jnp.int32),
                        pltpu.VMEM((A_HEADS, qb, LANES), F32),
                        pltpu.VMEM((A_HEADS, qb, LANES), F32),
                        pltpu.VMEM((A_HEADS, qb, LANES), F32),
                        pltpu.VMEM((A_HEADS, qb, A_HEAD_DIM), F32)],
        compiler_params=_params(2),
        name="mixer_a",
    )(rel_bias, q, qi, iw, *far, *cur, tiles)


def _diff_kernel(tab_ref, lam_ref, sg_ref, q_ref, k_ref, v_ref, kc_ref, vc_ref, tile_ref, o_ref,
                 mfar_scr, mnear_scr, l_scr, acc_scr, *, qb, kb, qoff, lam_init):
    head = pl.program_id(1)
    i = pl.program_id(2)
    q0 = qoff + i * qb
    qbr = tile_ref.shape[2] - LANES
    hd = B_HEAD_DIM
    far_bias = tab_ref[FAR_BUCKET, A_HEADS + head]

    def logits(c, kblk):
        s = lax.dot_general(q_ref[:, c * hd:(c + 1) * hd], kblk[:, c * hd:(c + 1) * hd], NT_DIMS,
                            preferred_element_type=F32)
        return [s[:, u * LANES:(u + 1) * LANES] for u in range(s.shape[1] // LANES)]

    def tile_cols(col0, n):
        return [tile_ref[0, :, col0 + u * LANES:col0 + (u + 1) * LANES] for u in range(n)]

    mfar_scr[...] = jnp.full(mfar_scr.shape, NEG_BIG, F32)
    mnear_scr[...] = jnp.full(mnear_scr.shape, NEG_BIG, F32)

    def max_far(start, blk0, colmask):
        kblk = _key_rows(k_ref, start, kb)
        for c in range(2):
            s = logits(c, kblk)
            if colmask is not None:
                s = [x + m for x, m in zip(s, colmask)]
            mfar_scr[c] = functools.reduce(jnp.maximum, s, mfar_scr[c])

    def max_near(kblk, col0):
        for c in range(2):
            mnear_scr[c] = functools.reduce(
                jnp.maximum, [s + b for s, b in zip(logits(c, kblk), tile_cols(col0, kblk.shape[0] // LANES))],
                mnear_scr[c])

    prev_start = pl.multiple_of(q0 - LANES, LANES)
    _sweep_far(q0, qb, kb, max_far)
    pl.when(q0 > 0)(lambda: max_near(_key_rows(k_ref, prev_start, LANES), 0))
    max_near(kc_ref[...], LANES)

    for c in range(2):
        m = jnp.maximum(jnp.max(mfar_scr[c], axis=-1, keepdims=True) + far_bias,
                        jnp.max(mnear_scr[c], axis=-1, keepdims=True))
        mfar_scr[c] = jnp.broadcast_to(far_bias - m, (qb, LANES))
        mnear_scr[c] = jnp.broadcast_to(-m, (qb, LANES))
    l_scr[...] = jnp.zeros(l_scr.shape, F32)
    acc_scr[...] = jnp.zeros(acc_scr.shape, F32)

    def accumulate(c, p, vblk):
        l_scr[c] += functools.reduce(jnp.add, p)
        acc_scr[c] += jnp.dot(jnp.concatenate(p, axis=1).astype(BF16), vblk, preferred_element_type=F32)

    def acc_far(start, blk0, colmask):
        kblk = _key_rows(k_ref, start, kb)
        vblk = _key_rows(v_ref, start, kb)
        for c in range(2):
            shift = mfar_scr[c]
            shifts = [shift] * (kb // LANES) if colmask is None else [shift + m for m in colmask]
            accumulate(c, [jnp.exp(s + sh) for s, sh in zip(logits(c, kblk), shifts)], vblk)

    def acc_near(kblk, vblk, col0):
        for c in range(2):
            shift = mnear_scr[c]
            accumulate(c, [jnp.exp(s + b + shift) for s, b in
                           zip(logits(c, kblk), tile_cols(col0, kblk.shape[0] // LANES))], vblk)

    _sweep_far(q0, qb, kb, acc_far)
    pl.when(q0 > 0)(lambda: acc_near(_key_rows(k_ref, prev_start, LANES), _key_rows(v_ref, prev_start, LANES), 0))
    acc_near(kc_ref[...], vc_ref[...], LANES)

    lam = lam_ref[0, 0]
    o = (acc_scr[0] / jnp.sum(l_scr[0], axis=-1, keepdims=True)
         - lam * (acc_scr[1] / jnp.sum(l_scr[1], axis=-1, keepdims=True)))
    o = o * lax.rsqrt(jnp.mean(o * o, axis=-1, keepdims=True) + EPS) * sg_ref[...]
    o_ref[...] = (o * (1.0 - lam_init)).astype(o_ref.dtype)


def _diff_call(rel_bias, tiles, lam, subln_g, q, far, cur, *, qb, kb, qoff, lam_init):
    b, t, _ = q.shape
    lf = far[0].shape[1]
    hw = 2 * B_HEAD_DIM
    qbr = tiles.shape[2] - LANES
    q0_max = qoff + t - qb
    far_end = -(-max(q0_max - LANES, 0) // kb) * kb
    assert t % qb == 0 and qoff % LANES == 0 and far_end <= lf and cur[0].shape[1] == (t // qb) * qbr
    far_spec = pl.BlockSpec((None, lf, hw), lambda bi, h, i: (bi, 0, h))
    cur_spec = pl.BlockSpec((None, qbr, hw), lambda bi, h, i: (bi, i, h))
    return pl.pallas_call(
        functools.partial(_diff_kernel, qb=qb, kb=kb, qoff=qoff, lam_init=lam_init),
        grid=(b, B_HEADS, t // qb),
        in_specs=[pl.BlockSpec(memory_space=pltpu.SMEM),
                  pl.BlockSpec(memory_space=pltpu.SMEM),
                  pl.BlockSpec((1, hw), lambda bi, h, i: (0, 0)),
                  pl.BlockSpec((None, qb, hw), lambda bi, h, i: (bi, i, h)),
                  far_spec, far_spec, cur_spec, cur_spec,
                  pl.BlockSpec((1, qb, tiles.shape[2]), lambda bi, h, i: (A_HEADS + h, 0, 0))],
        out_specs=pl.BlockSpec((None, qb, hw), lambda bi, h, i: (bi, i, h)),
        out_shape=jax.ShapeDtypeStruct((b, t, B_HEADS * hw), BF16),
        scratch_shapes=[pltpu.VMEM((2, qb, LANES), F32),
                        pltpu.VMEM((2, qb, LANES), F32),
                        pltpu.VMEM((2, qb, LANES), F32),
                        pltpu.VMEM((2, qb, hw), F32)],
        compiler_params=_params(3),
        name="mixer_b",
    )(rel_bias, lam, subln_g.reshape(1, hw), q, *far, *cur, tiles)


def _lam_kernel(dl_ref, o_ref, *, lam_init):
    dl = dl_ref[...]
    s1 = jnp.sum(dl[0:1] * dl[1:2], axis=-1, keepdims=True)
    s2 = jnp.sum(dl[2:3] * dl[3:4], axis=-1, keepdims=True)
    o_ref[...] = jnp.exp(s1) - jnp.exp(s2) + lam_init


def _lam_call(diff_lam, lam_init):
    return pl.pallas_call(
        functools.partial(_lam_kernel, lam_init=lam_init),
        out_shape=jax.ShapeDtypeStruct((1, 1), F32),
        name="diff_lambda",
    )(diff_lam)


def _route(logits_t, rbias, counts):
    tm = logits_t.shape[1]
    per = N_EXPERTS // N_GROUPS
    scores = jax.nn.sigmoid(logits_t).reshape(N_GROUPS, per, tm)
    ch = scores + rbias.reshape(N_GROUPS, per, 1)
    sub = lax.broadcasted_iota(jnp.int32, ch.shape, 1)
    grp = lax.broadcasted_iota(jnp.int32, ch.shape, 0)
    m1 = jnp.max(ch, axis=1, keepdims=True)
    i1 = jnp.min(jnp.where(ch == m1, sub, per), axis=1, keepdims=True)
    m2 = jnp.max(jnp.where(sub == i1, -jnp.inf, ch), axis=1, keepdims=True)
    gs = jnp.broadcast_to(m1 + m2, ch.shape)
    rank = jnp.zeros(ch.shape, jnp.int32)
    for g2 in range(N_GROUPS):
        o = gs[g2:g2 + 1]
        rank = rank + jnp.where(o > gs, 1, 0) + jnp.where(o == gs, 1, 0) * jnp.where(grp > g2, 1, 0)
    cm = jnp.where(rank < TOPK_GROUPS, ch, -jnp.inf)
    eid = grp * per + sub
    erank = jnp.zeros(cm.shape, jnp.int32)
    for e2 in range(N_EXPERTS):
        o = cm[e2 // per:e2 // per + 1, e2 % per:e2 % per + 1, :]
        erank = erank + jnp.where(o > cm, 1, 0) + jnp.where(o == cm, 1, 0) * jnp.where(eid > e2, 1, 0)
    chosen = erank < TOP_K
    sel = jnp.where(chosen, scores, 0.0)
    denom = jnp.sum(jnp.sum(sel, axis=1, keepdims=True), axis=0, keepdims=True)
    gates = sel / denom * ROUTED_SCALE

    ones = jnp.where(chosen, 1.0, 0.0).reshape(N_EXPERTS, tm)
    earlier = (lax.broadcasted_iota(jnp.int32, (tm, tm), 0) < lax.broadcasted_iota(jnp.int32, (tm, tm), 1))
    before = jnp.dot(ones.astype(BF16), jnp.where(earlier, 1.0, 0.0).astype(BF16), preferred_element_type=F32)
    pos = (before + counts.astype(F32)).reshape(N_GROUPS, per, tm)
    new_counts = counts + jnp.sum(ones, axis=1, keepdims=True).astype(jnp.int32)

    def per_slot(values):
        rows = [jnp.sum(jnp.sum(jnp.where(erank == k, values, 0.0), axis=1, keepdims=True), axis=0)
                for k in range(TOP_K)]
        return jnp.concatenate(rows, axis=0)

    return (per_slot(eid.astype(F32)).astype(jnp.int32), per_slot(pos).astype(jnp.int32), per_slot(gates),
            new_counts)


def _pack_pair(a, b):
    ua = pltpu.bitcast(a.astype(BF16).astype(F32), jnp.uint32)
    ub = pltpu.bitcast(b.astype(BF16).astype(F32), jnp.uint32)
    return pltpu.bitcast(ua | (ub >> 16), jnp.int32)


def _unpack_pair(p):
    u = pltpu.bitcast(p, jnp.uint32)
    a = pltpu.bitcast(u & jnp.uint32(0xFFFF0000), F32)
    b = pltpu.bitcast(u << 16, F32)
    return a.astype(BF16), b.astype(BF16)


PARTS = 4
PART_W = D_MODEL // 2 // PARTS


def _pack_rows(x):
    half = x.shape[1] // 2
    return jnp.stack([_pack_pair(x[:, j * PART_W:(j + 1) * PART_W],
                                 x[:, half + j * PART_W:half + (j + 1) * PART_W]) for j in range(PARTS)])


def _packed_matmul(parts_ref, w_ref):
    half = w_ref.shape[0] // 2
    acc = None
    for j in range(PARTS):
        a, b = _unpack_pair(parts_ref[j])
        t = (jnp.dot(a, w_ref[j * PART_W:(j + 1) * PART_W, :], preferred_element_type=F32)
             + jnp.dot(b, w_ref[half + j * PART_W:half + (j + 1) * PART_W, :], preferred_element_type=F32))
        acc = t if acc is None else acc + t
    return acc


def _outproj_kernel(x_ref, oa_ref, ob_ref, ga_ref, sc_ref, sh_ref, g_ref, w_ref, wr_ref, rb_ref, cin_ref,
                    x1_ref, h2_ref, ek_ref, pk_ref, gk_ref, cout_ref, cnt_scr):
    @pl.when(pl.program_id(0) == 0)
    def _():
        cnt_scr[...] = cin_ref[...]

    half = oa_ref.shape[1]
    mix = (jnp.dot(oa_ref[...], w_ref[:half, :], preferred_element_type=F32)
           + jnp.dot(ob_ref[...], w_ref[half:, :], preferred_element_type=F32))
    x1 = x_ref[...] + ga_ref[...] * mix
    x1_ref[...] = x1
    h2 = _modulated_norm(x1, g_ref[...], sc_ref[...], sh_ref[...])
    h2_ref[...] = _pack_rows(h2)
    logits_t = lax.dot_general(wr_ref[...], h2, NT_DIMS, preferred_element_type=F32,
                               precision=lax.Precision.HIGHEST)
    ek, pk, gk, counts = _route(logits_t, rb_ref[...], cnt_scr[...])
    ek_ref[...] = ek
    pk_ref[...] = pk
    gk_ref[...] = gk
    cnt_scr[...] = counts
    cout_ref[...] = counts


def _outproj_call(x2d, oa, ob, ga, sc, sh, mod_map, g, w_bf, wr_t, rbias, counts_in, tm):
    n, d = x2d.shape
    assert n % tm == 0
    r = sc.shape[1]
    mod_spec = pl.BlockSpec((None, r, d), mod_map)
    slot_spec = pl.BlockSpec((TOP_K, tm), lambda i: (0, i))
    return pl.pallas_call(
        _outproj_kernel,
        grid=(n // tm,),
        in_specs=[pl.BlockSpec((tm, d), lambda i: (i, 0)),
                  pl.BlockSpec((tm, oa.shape[1]), lambda i: (i, 0)),
                  pl.BlockSpec((tm, ob.shape[1]), lambda i: (i, 0)),
                  mod_spec, mod_spec, mod_spec,
                  pl.BlockSpec((1, d), lambda i: (0, 0)),
                  _resident(w_bf.shape, lambda i: (0, 0)),
                  _resident(wr_t.shape, lambda i: (0, 0)),
                  pl.BlockSpec((N_EXPERTS, 1), lambda i: (0, 0)),
                  pl.BlockSpec((N_EXPERTS, 1), lambda i: (0, 0))],
        out_specs=[pl.BlockSpec((tm, d), lambda i: (i, 0)),
                   pl.BlockSpec((PARTS, tm, PART_W), lambda i: (0, i, 0)),
                   slot_spec, slot_spec, slot_spec,
                   pl.BlockSpec((N_EXPERTS, 1), lambda i: (0, 0))],
        out_shape=[jax.ShapeDtypeStruct((n, d), F32),
                   jax.ShapeDtypeStruct((PARTS, n, PART_W), jnp.int32),
                   jax.ShapeDtypeStruct((TOP_K, n), jnp.int32),
                   jax.ShapeDtypeStruct((TOP_K, n), jnp.int32),
                   jax.ShapeDtypeStruct((TOP_K, n), F32),
                   jax.ShapeDtypeStruct((N_EXPERTS, 1), jnp.int32)],
        scratch_shapes=[pltpu.VMEM((N_EXPERTS, 1), jnp.int32)],
        compiler_params=_params(1),
        name="out_proj_router",
    )(x2d, oa, ob, ga, sc, sh, g.reshape(1, d), w_bf, wr_t, rbias, counts_in)


SC_WINDOW = 128
EXPERT_TILE = 512


def _sc_mesh():
    return plsc.VectorSubcoreMesh(core_axis_name="core", subcore_axis_name="subcore")


def _sc_scatter_rows(x, idx, n_out):
    n, w = x.shape
    steps = n // SC_WINDOW
    total = idx.shape[1] // SC_WINDOW

    @pl.kernel(out_type=jax.ShapeDtypeStruct((n_out, w), x.dtype), mesh=_sc_mesh(), scratch_types=[],
               name="moe_dispatch")
    def scatter(x_hbm, i_hbm, o_hbm):
        def body(x_vmem, i_vmem):
            pltpu.sync_copy(x_vmem, o_hbm.at[i_vmem.at[0]])

        pltpu.emit_pipeline(
            body, grid=(total,),
            in_specs=[pl.BlockSpec((SC_WINDOW, w), lambda i: (i % steps, 0)),
                      pl.BlockSpec((1, SC_WINDOW), lambda i: (0, i))],
            out_specs=[], core_axis_name=("core", "subcore"),
            dimension_semantics=(pltpu.PARALLEL,))(x_hbm, i_hbm)

    return scatter(x, idx)


def _sc_gather_rows(y, idx):
    w = y.shape[1]
    n = idx.shape[1]

    @pl.kernel(out_type=jax.ShapeDtypeStruct((n, w), y.dtype), mesh=_sc_mesh(), scratch_types=[],
               name="moe_collect")
    def gather(y_hbm, i_hbm, o_hbm):
        def body(i_vmem, o_vmem):
            pltpu.sync_copy(y_hbm.at[i_vmem.at[0]], o_vmem)

        pltpu.emit_pipeline(
            body, grid=(n // SC_WINDOW,),
            in_specs=[pl.BlockSpec((1, SC_WINDOW), lambda i: (0, i))],
            out_specs=[pl.BlockSpec((SC_WINDOW, w), lambda i: (i, 0))],
            core_axis_name=("core", "subcore"),
            dimension_semantics=(pltpu.PARALLEL,))(i_hbm, o_hbm)

    return gather(y, idx)


def _swiglu(parts_ref, wg_ref, wu_ref, wd_ref):
    a = _packed_matmul(parts_ref, wg_ref)
    u = _packed_matmul(parts_ref, wu_ref)
    hid = (a * jax.nn.sigmoid(a) * u).astype(BF16)
    return jnp.dot(hid, wd_ref[...], preferred_element_type=F32)


def _expert_kernel(te_ref, nu_ref, x_ref, wg_ref, wu_ref, wd_ref, y_ref, wg_scr, wu_scr, wd_scr):
    i = pl.program_id(0)

    @pl.when(i < nu_ref[0])
    def _():
        @pl.when((i == 0) | (te_ref[i] != te_ref[jnp.maximum(i - 1, 0)]))
        def _():
            wg_scr[...] = wg_ref[...].astype(BF16)
            wu_scr[...] = wu_ref[...].astype(BF16)
            wd_scr[...] = wd_ref[...].astype(BF16)

        y_ref[...] = _pack_rows(_swiglu(x_ref, wg_scr, wu_scr, wd_scr))


def _expert_call(tile_expert, n_used, x_sorted, wg, wu, wd):
    _, r, _ = x_sorted.shape
    ne, d, de = wg.shape
    n_tiles = r // EXPERT_TILE
    row_map = lambda i, te, nu: (0, jnp.minimum(i, nu[0] - 1), 0)
    return pl.pallas_call(
        _expert_kernel,
        grid_spec=pltpu.PrefetchScalarGridSpec(
            num_scalar_prefetch=2,
            grid=(n_tiles,),
            in_specs=[pl.BlockSpec((PARTS, EXPERT_TILE, PART_W), row_map),
                      pl.BlockSpec((None, d, de), lambda i, te, nu: (te[i], 0, 0)),
                      pl.BlockSpec((None, d, de), lambda i, te, nu: (te[i], 0, 0)),
                      pl.BlockSpec((None, de, d), lambda i, te, nu: (te[i], 0, 0))],
            out_specs=pl.BlockSpec((PARTS, EXPERT_TILE, PART_W), row_map),
            scratch_shapes=[pltpu.VMEM((d, de), BF16), pltpu.VMEM((d, de), BF16), pltpu.VMEM((de, d), BF16)]),
        out_shape=jax.ShapeDtypeStruct(x_sorted.shape, jnp.int32),
        compiler_params=_params(1),
        name="moe_experts",
    )(tile_expert, n_used, x_sorted, wg, wu, wd)


def _final_kernel(x_ref, h_ref, rows_ref, gk_ref, gf_ref, g_ref, sg_ref, su_ref, sd_ref, o_ref):
    moe = _swiglu(h_ref, sg_ref, su_ref, sd_ref)
    half = moe.shape[1] // 2
    lo = [moe[:, j * PART_W:(j + 1) * PART_W] for j in range(PARTS)]
    hi = [moe[:, half + j * PART_W:half + (j + 1) * PART_W] for j in range(PARTS)]
    for k in range(TOP_K):
        gate = gk_ref[:, k:k + 1]
        for j in range(PARTS):
            a, b = _unpack_pair(rows_ref[k, j])
            lo[j] = lo[j] + gate * a.astype(F32)
            hi[j] = hi[j] + gate * b.astype(F32)
    x = x_ref[...] + gf_ref[...] * jnp.concatenate(lo + hi, axis=1)
    o_ref[...] = x * lax.rsqrt(jnp.mean(x * x, axis=-1, keepdims=True) + EPS) * g_ref[...]


def _final_call(x1, h2p, rows, row_block0, gk_t, gf, mod_map, g, sg, su, sd, tm):
    n, d = x1.shape
    assert n % tm == 0
    r = gf.shape[1]
    return pl.pallas_call(
        _final_kernel,
        grid=(n // tm,),
        in_specs=[pl.BlockSpec((tm, d), lambda i: (i, 0)),
                  pl.BlockSpec((PARTS, tm, PART_W), lambda i: (0, i, 0)),
                  pl.BlockSpec((TOP_K, PARTS, tm, PART_W), lambda i: (0, 0, i + row_block0, 0)),
                  pl.BlockSpec((tm, TOP_K), lambda i: (i, 0)),
                  pl.BlockSpec((None, r, d), mod_map),
                  pl.BlockSpec((1, d), lambda i: (0, 0)),
                  _resident(sg.shape, lambda i: (0, 0)),
                  _resident(su.shape, lambda i: (0, 0)),
                  _resident(sd.shape, lambda i: (0, 0))],
        out_specs=pl.BlockSpec((tm, d), lambda i: (i, 0)),
        out_shape=jax.ShapeDtypeStruct((n, d), F32),
        compiler_params=_params(1),
        name="final_norm",
    )(x1, h2p, rows, gk_t, gf, g.reshape(1, d), sg, su, sd)


_QA, _KA, _VA, _QI, _KI, _IW, _QB, _KB, _VB = range(9)
_IN_SIZES = (A_HEADS * A_HEAD_DIM, A_KV_HEADS * A_HEAD_DIM, A_KV_HEADS * A_HEAD_DIM,
             IDX_HEADS * IDX_DIM, IDX_DIM, IDX_HEADS,
             B_HEADS * 2 * B_HEAD_DIM, B_HEADS * 2 * B_HEAD_DIM, B_HEADS * 2 * B_HEAD_DIM)


def _split_w_in(w_in):
    d = w_in.shape[0]
    offs = np.concatenate([[0], np.cumsum(_IN_SIZES)])
    cols = [w_in[:, offs[k]:offs[k + 1]].astype(BF16) for k in range(9)]
    zero = lambda n: jnp.zeros((d, n), BF16)
    qi = jnp.concatenate([cols[_QI].reshape(d, IDX_HEADS, IDX_DIM),
                          jnp.zeros((d, IDX_HEADS, LANES - IDX_DIM), BF16)], axis=-1).reshape(d, IDX_HEADS * LANES)
    w_a = jnp.concatenate([cols[_QA], cols[_KA], cols[_VA], qi,
                           cols[_KI], zero(LANES - IDX_DIM), cols[_IW], zero(LANES - IDX_HEADS)], axis=1)
    w_b = jnp.concatenate([cols[_QB], cols[_KB], cols[_VB]], axis=1)
    return w_a, w_b


def _group_forward(x, mod, past, rel_bias, lam, weights, *, lam_init, qb, kb, tm, per_row_mod):
    (norm_a_g, w_a, w_b, subln_g) = weights
    b, t, d = x.shape
    n = b * t
    x2d = x.reshape(n, d)
    sh_a, sc_a = mod[0], mod[1]
    if per_row_mod:
        expand = lambda m: jnp.repeat(m, t, axis=0).reshape(1, n, d)
        mod_map = lambda i: (0, i, 0)
    else:
        expand = lambda m: m.reshape(b, 1, d)
        mod_map = lambda i: ((i * tm) // t, 0, 0)
    sc3, sh3 = expand(sc_a), expand(sh_a)

    na = A_HEADS * A_HEAD_DIM
    nk = A_KV_HEADS * A_HEAD_DIM
    nq = IDX_HEADS * LANES
    c_ka, c_va, c_qi = na, na + nk, na + 2 * nk
    c_ki, c_iw = c_qi + nq, c_qi + nq + LANES
    defs_a = [(0, na, na, A_HEAD_DIM ** -0.5), (c_ka, nk, nk, None), (c_ka, nk, nk, None),
              (c_va, nk, nk, None), (c_va, nk, nk, None), (c_qi, nq, nq, None),
              (c_ki, LANES, IDX_DIM, None), (c_ki, LANES, LANES, None), (c_iw, LANES, LANES, None)]
    dt_a = [BF16, F32, BF16, F32, BF16, BF16, F32, BF16, F32]
    qa, ka, ka_bf, va, va_bf, qi, ki, ki_bf, iw = _norm_proj_call(
        x2d, sc3, sh3, mod_map, norm_a_g, w_a, defs_a, dt_a, tm, "in_proj_a")
    nb = B_HEADS * 2 * B_HEAD_DIM
    defs_b = [(0, nb, nb, B_HEAD_DIM ** -0.5), (nb, nb, nb, None), (nb, nb, nb, None),
              (2 * nb, nb, nb, None), (2 * nb, nb, nb, None)]
    dt_b = [BF16, F32, BF16, F32, BF16]
    qbm, kbm, kb_bf, vbm, vb_bf = _norm_proj_call(
        x2d, sc3, sh3, mod_map, norm_a_g, w_b, defs_b, dt_b, tm, "in_proj_b")

    r3 = lambda a: a.reshape(b, t, a.shape[-1])
    cur = [r3(ka_bf), r3(va_bf), r3(ki_bf), r3(kb_bf), r3(vb_bf)]
    if past is None:
        qoff = 0
        far = cur
    else:
        assert t == qb, "a group with cached keys is one query block per batch row"
        qoff = past[0].shape[1]
        qbr = -(-qb // LANES) * LANES
        cur = [jnp.concatenate([a, jnp.zeros((b, qbr - t, a.shape[-1]), BF16)], axis=1) for a in cur]
        far = [p.reshape(b, qoff, -1) for p in past]
        short = -qoff % kb
        if short:
            far = [jnp.concatenate([p, jnp.zeros((b, short, p.shape[-1]), p.dtype)], axis=1) for p in far]
    length = qoff + t
    topk = min(TOPK_MAX, length // 4)

    tiles = _bias_tile_call(rel_bias, qb)
    out_a = _dsa_call(rel_bias, tiles, r3(qa), r3(qi), r3(iw), far[:3], cur[:3], qb=qb, kb=kb, qoff=qoff, topk=topk)
    out_b = _diff_call(rel_bias, tiles, lam, subln_g, r3(qbm), far[3:], cur[3:], qb=qb, kb=kb, qoff=qoff,
                       lam_init=lam_init)
    rows = (ka.reshape(1, b, t, A_KV_HEADS, A_HEAD_DIM), va.reshape(1, b, t, A_KV_HEADS, A_HEAD_DIM),
            ki.reshape(1, b, t, IDX_DIM), kbm.reshape(1, b, t, B_HEADS, 2, B_HEAD_DIM),
            vbm.reshape(1, b, t, B_HEADS, 2 * B_HEAD_DIM))
    return x2d, out_a.reshape(n, -1), out_b.reshape(n, -1), rows, expand, mod_map


def kernel(x_prompt, x_sample, c_prompt, c_sample, cache_a_k, cache_a_v, cache_a_kidx, cache_b_k, cache_b_v,
           rel_bias, w_ada, b_ada, norm_a_g, w_in, w_out, diff_lam, subln_g, norm_f_g, w_router, router_bias,
           w_gate, w_up, w_down, ws_gate, ws_up, ws_down, final_g):
    assert w_ada.shape[0] == 1, "single-layer model"
    d = x_prompt.shape[-1]
    bp, tp, _ = x_prompt.shape
    bs, ts, _ = x_sample.shape
    lam_init = 0.8 - 0.6 * math.exp(-0.3 * 0)

    c_all = jnp.concatenate([c_prompt, c_sample], axis=0)
    rows = -(-c_all.shape[0] // 8) * 8
    c_pad = jnp.concatenate([c_all, jnp.zeros((rows - c_all.shape[0], d), F32)], axis=0)
    mod_all = _ada_call(c_pad, w_ada[0], b_ada[0])
    mod_p = [mod_all[:bp, k * d:(k + 1) * d] for k in range(6)]
    mod_s = [mod_all[bp:bp + bs, k * d:(k + 1) * d] for k in range(6)]

    w_a, w_b = _split_w_in(w_in[0])
    lam = _lam_call(diff_lam[0], lam_init)
    weights = (norm_a_g[0], w_a, w_b, subln_g[0])
    past = (cache_a_k[0], cache_a_v[0], cache_a_kidx[0], cache_b_k[0], cache_b_v[0])

    tm_p = 256
    tm_s = bs * ts
    xp2, oa_p, ob_p, rows_p, exp_p, map_p = _group_forward(
        x_prompt, mod_p, None, rel_bias, lam, weights, lam_init=lam_init, qb=256, kb=512, tm=tm_p,
        per_row_mod=False)
    xs2, oa_s, ob_s, rows_s, exp_s, map_s = _group_forward(
        x_sample, mod_s, past, rel_bias, lam, weights, lam_init=lam_init, qb=ts, kb=512, tm=tm_s,
        per_row_mod=True)

    w_out_bf = w_out[0].astype(BF16)
    wr_t = w_router[0].T
    rb = router_bias[0].reshape(N_EXPERTS, 1)
    no_rows = jnp.zeros((N_EXPERTS, 1), jnp.int32)
    x1_p, h2_p, ek_p, pk_p, gk_p, cnt_p = _outproj_call(
        xp2, oa_p, ob_p, exp_p(mod_p[2]), exp_p(mod_p[4]), exp_p(mod_p[3]), map_p, norm_f_g[0], w_out_bf, wr_t, rb,
        no_rows, tm_p)
    x1_s, h2_s, ek_s, pk_s, gk_s, cnt_all = _outproj_call(
        xs2, oa_s, ob_s, exp_s(mod_s[2]), exp_s(mod_s[4]), exp_s(mod_s[3]), map_s, norm_f_g[0], w_out_bf, wr_t, rb,
        cnt_p, tm_s)

    n_p, n_s = bp * tp, bs * ts
    n_all = n_p + n_s
    n_rows = -(-(n_all * TOP_K) // EXPERT_TILE) * EXPERT_TILE + N_EXPERTS * EXPERT_TILE
    counts = cnt_all[:, 0]
    padded = (counts + EXPERT_TILE - 1) // EXPERT_TILE * EXPERT_TILE
    ends = jnp.cumsum(padded)
    starts = ends - padded
    tile_first_row = jnp.arange(n_rows // EXPERT_TILE, dtype=jnp.int32) * EXPERT_TILE
    tile_expert = jnp.minimum(jnp.sum(ends[None, :] <= tile_first_row[:, None], axis=1), N_EXPERTS - 1).astype(jnp.int32)
    n_used = (ends[-1:] // EXPERT_TILE).astype(jnp.int32)
    ek = jnp.concatenate([ek_p, ek_s], axis=1)
    expert_ids = jnp.arange(N_EXPERTS, dtype=jnp.int32)[:, None, None]
    dest = jnp.sum(jnp.where(ek[None] == expert_ids, starts[:, None, None], 0), axis=0) \
        + jnp.concatenate([pk_p, pk_s], axis=1)
    piece = jnp.arange(PARTS, dtype=jnp.int32)[None, :, None]
    idx = (dest[:, None, :] + piece * n_rows).reshape(1, TOP_K * PARTS * n_all).astype(jnp.int32)

    h2_all = jnp.concatenate([h2_p, h2_s], axis=1).reshape(PARTS * n_all, PART_W)
    x_sorted = _sc_scatter_rows(h2_all, idx, PARTS * n_rows).reshape(PARTS, n_rows, PART_W)
    y_sorted = _expert_call(tile_expert, n_used, x_sorted, w_gate[0], w_up[0], w_down[0])
    rows = _sc_gather_rows(y_sorted.reshape(PARTS * n_rows, PART_W), idx).reshape(TOP_K, PARTS, n_all, PART_W)

    shared = (ws_gate[0].astype(BF16), ws_up[0].astype(BF16), ws_down[0].astype(BF16))
    assert n_p % tm_s == 0
    y_p = _final_call(x1_p, h2_p, rows, 0, gk_p.T, exp_p(mod_p[5]), map_p, final_g, *shared, tm_p)
    y_s = _final_call(x1_s, h2_s, rows, n_p // tm_s, gk_s.T, exp_s(mod_s[5]), map_s, final_g, *shared, tm_s)
    return (y_p.reshape(bp, tp, d), y_s.reshape(bs, ts, d)) + rows_p + rows_s
```

```python
import functools
import math

import numpy as np
import jax
import jax.numpy as jnp
from jax import lax
from jax.experimental import pallas as pl
from jax.experimental.pallas import tpu as pltpu
from jax.experimental.pallas import tpu_sc as plsc

D_MODEL = 2048
CHUNK = 64
A_HEADS = 8
A_KV_HEADS = 2
A_REP = A_HEADS // A_KV_HEADS
A_HEAD_DIM = 128
IDX_HEADS = 16
IDX_DIM = 64
TOPK_MAX = 256
B_HEADS = 4
B_HEAD_DIM = 128
N_REL_BUCKETS = 32
REL_MAX_DIST = 128
N_ATTN_HEADS = A_HEADS + B_HEADS
N_EXPERTS = 64
TOP_K = 8
N_GROUPS = 8
TOPK_GROUPS = 4
EXPERT_DIM = 512
SHARED_DIM = 512
ROUTED_SCALE = 2.5
EPS = 1e-6

LANES = 128
INT_MIN = -(2 ** 31)
NEG_BIG = -1e30
VMEM_LIMIT = 52 * 1024 * 1024

F32 = jnp.float32
BF16 = jnp.bfloat16
NT_DIMS = (((1,), (1,)), ((), ()))


def _params(n_axes):
    return pltpu.CompilerParams(dimension_semantics=("arbitrary",) * n_axes,
                                vmem_limit_bytes=VMEM_LIMIT)


def _resident(shape, index_map):
    return pl.BlockSpec(shape, index_map, pipeline_mode=pl.Buffered(1))


def _bucket_thresholds():
    nb = N_REL_BUCKETS // 2
    max_exact = nb // 2
    n = np.arange(max_exact, 4 * REL_MAX_DIST, dtype=np.int32)
    nf = n.astype(np.float32)
    large = max_exact + (np.log(nf / np.float32(max_exact)) / np.float32(math.log(REL_MAX_DIST / max_exact))
                         * np.float32(nb - max_exact)).astype(np.int32)
    large = np.minimum(large, nb - 1)
    assert np.all(np.diff(large) >= 0)
    ths = [int(n[np.argmax(large >= b)]) for b in range(max_exact + 1, nb)]
    return max_exact, nb, ths


_MAX_EXACT, _NB, _BUCKET_THS = _bucket_thresholds()
assert _BUCKET_THS[-1] <= LANES + 1
FAR_BUCKET = _NB - 1


def _ada_kernel(c_ref, w_ref, b_ref, o_ref):
    c = c_ref[...]
    s = c * jax.nn.sigmoid(c)
    o_ref[...] = jnp.dot(s, w_ref[...], preferred_element_type=F32,
                         precision=lax.Precision.HIGHEST) + b_ref[...]


def _ada_call(c, w_ada, b_ada):
    rows, d = c.shape
    n = w_ada.shape[1]
    tn = 1536
    assert n % tn == 0
    return pl.pallas_call(
        _ada_kernel,
        grid=(n // tn,),
        in_specs=[pl.BlockSpec((rows, d), lambda j: (0, 0)),
                  pl.BlockSpec((d, tn), lambda j: (0, j)),
                  pl.BlockSpec((1, tn), lambda j: (0, j))],
        out_specs=pl.BlockSpec((rows, tn), lambda j: (0, j)),
        out_shape=jax.ShapeDtypeStruct((rows, n), F32),
        compiler_params=_params(1),
        name="ada_mod",
    )(c, w_ada, b_ada.reshape(1, n))


def _modulated_norm(x, g, sc, sh):
    y = x * lax.rsqrt(jnp.mean(x * x, axis=-1, keepdims=True) + EPS) * g
    return y * (1.0 + sc) + sh


def _norm_proj_kernel(x_ref, sc_ref, sh_ref, g_ref, w_ref, *out_refs, defs):
    h = _modulated_norm(x_ref[...], g_ref[...], sc_ref[...], sh_ref[...]).astype(BF16)
    done = {}
    for (start, width, store_w, mult), o_ref in zip(defs, out_refs):
        if (start, width) not in done:
            done[(start, width)] = jnp.dot(h, w_ref[:, start:start + width], preferred_element_type=F32)
        r = done[(start, width)][:, :store_w]
        o_ref[...] = (r if mult is None else r * mult).astype(o_ref.dtype)


def _norm_proj_call(x2d, sc, sh, mod_map, g, w_bf, defs, dtypes, tm, name):
    n, d = x2d.shape
    assert n % tm == 0
    r = sc.shape[1]
    out_shapes = [jax.ShapeDtypeStruct((n, d_[2]), dt) for d_, dt in zip(defs, dtypes)]
    out_specs = [pl.BlockSpec((tm, d_[2]), lambda i: (i, 0)) for d_ in defs]
    return pl.pallas_call(
        functools.partial(_norm_proj_kernel, defs=tuple(defs)),
        grid=(n // tm,),
        in_specs=[pl.BlockSpec((tm, d), lambda i: (i, 0)),
                  pl.BlockSpec((None, r, d), mod_map),
                  pl.BlockSpec((None, r, d), mod_map),
                  pl.BlockSpec((1, d), lambda i: (0, 0)),
                  _resident(w_bf.shape, lambda i: (0, 0))],
        out_specs=out_specs,
        out_shape=out_shapes,
        compiler_params=_params(1),
        name=name,
    )(x2d, sc, sh, g.reshape(1, d), w_bf)


def _bias_tile_kernel(tab_ref, o_ref, *, qb):
    _, rows, width = o_ref.shape
    r = lax.broadcasted_iota(jnp.int32, (rows, width), 0)
    koff = lax.broadcasted_iota(jnp.int32, (rows, width), 1) - LANES
    rel = koff - r
    n = jnp.abs(rel)
    large = jnp.full((rows, width), _MAX_EXACT, jnp.int32)
    for th in _BUCKET_THS:
        large = large + jnp.where(n >= th, 1, 0)
    bucket = jnp.where(rel > 0, _NB, 0) + jnp.where(n < _MAX_EXACT, n, large)
    visible = ((koff >> 6) <= (r >> 6)) & (koff < qb)
    for h in range(N_ATTN_HEADS):
        b = jnp.zeros((rows, width), F32)
        for bk in range(N_REL_BUCKETS):
            b = jnp.where(bucket == bk, tab_ref[bk, h], b)
        o_ref[h] = jnp.where(visible, b, NEG_BIG)


def _bias_tile_call(rel_bias, qb):
    qbr = -(-qb // LANES) * LANES
    return pl.pallas_call(
        functools.partial(_bias_tile_kernel, qb=qb),
        in_specs=[pl.BlockSpec(memory_space=pltpu.SMEM)],
        out_specs=pl.BlockSpec(memory_space=pltpu.VMEM),
        out_shape=jax.ShapeDtypeStruct((N_ATTN_HEADS, qb, LANES + qbr), F32),
        name="bias_tiles",
    )(rel_bias)


def _sweep_far(q0, qb, kb, fn):
    far_end = jnp.maximum(q0 - LANES, 0)
    n_full = far_end // kb
    sub = kb // LANES

    def whole(j):
        fn(pl.multiple_of(j * kb, kb), j * sub, None)

    def pair(j, carry):
        whole(2 * j)
        whole(2 * j + 1)
        return carry

    lax.fori_loop(0, n_full // 2, pair, 0)
    pl.when(n_full % 2 == 1)(lambda: whole(n_full - 1))

    @pl.when(far_end > n_full * kb)
    def _():
        start = pl.multiple_of(n_full * kb, kb)
        col = start + lax.broadcasted_iota(jnp.int32, (qb, LANES), 1)
        fn(start, n_full * sub, [jnp.where(col + c * LANES < far_end, 0.0, NEG_BIG) for c in range(sub)])


def _key_rows(ref, start, width):
    blk = ref[pl.ds(start, width), :].astype(BF16)
    pad = -blk.shape[1] % LANES
    return blk if pad == 0 else jnp.concatenate([blk, jnp.zeros((width, pad), BF16)], axis=1)


def _dsa_kernel(tab_ref, q_ref, qi_ref, iw_ref, k_ref, v_ref, ki_ref, kc_ref, vc_ref, kic_ref, tile_ref, o_ref,
                keys_scr, half_scr, thr_scr, mfar_scr, mnear_scr, l_scr, acc_scr, *, qb, kb, qoff, topk):
    i = pl.program_id(1)
    q0 = qoff + i * qb
    qbr = tile_ref.shape[2] - LANES
    sub = kb // LANES
    w = iw_ref[:, :IDX_HEADS] * (IDX_HEADS ** -0.5 * IDX_DIM ** -0.5)

    def score_keys(kib):
        width = kib.shape[0]
        acc = jnp.zeros((qb, width), F32)
        for h in range(IDX_HEADS):
            d = lax.dot_general(qi_ref[:, h * LANES:(h + 1) * LANES], kib, NT_DIMS,
                                preferred_element_type=F32)
            acc = acc + w[:, h:h + 1] * jnp.maximum(d, 0.0)
        bits = pltpu.bitcast(acc, jnp.int32)
        return bits ^ ((bits >> 31) & 0x7FFFFFFF)

    n_score = (q0 + kb - 1) // kb

    def store_keys(slab0, key):
        for c in range(key.shape[1] // LANES):
            part = key[:, c * LANES:(c + 1) * LANES]
            keys_scr[slab0 + c] = part
            half_scr[slab0 + c] = (part >> 16).astype(jnp.int16)

    def score_body(j, carry):
        store_keys(j * sub, score_keys(_key_rows(ki_ref, pl.multiple_of(j * kb, kb), kb)))
        return carry

    lax.fori_loop(0, n_score, score_body, 0)

    cur_blk = q0 // LANES
    key = score_keys(kic_ref[...])
    store_keys(cur_blk, jnp.where(tile_ref[0, :, LANES:] > 0.5 * NEG_BIG, key, INT_MIN))
    n_blk = cur_blk + qbr // LANES

    rb = min(qb, LANES)
    lo16 = -(2 ** 15)
    n4 = n_blk // 4

    chunks = [pl.ds(c * rb, rb) for c in range(qb // rb)]
    ones = jnp.ones((LANES, LANES), BF16)

    def count_ge(ts):
        tbs = [t.astype(jnp.int16) for t in ts]

        def count(j, cnts):
            return tuple(cnt + jnp.where(half_scr[j, rows, :] >= tb, jnp.int16(1), jnp.int16(0))
                         for cnt, rows, tb in zip(cnts, chunks, tbs))

        def body4(j, cnts):
            for u in range(4):
                cnts = count(4 * j + u, cnts)
            return cnts

        cnts = lax.fori_loop(0, n4, body4, tuple(jnp.zeros((rb, LANES), jnp.int16) for _ in chunks))
        cnts = lax.fori_loop(4 * n4, n_blk, count, cnts)
        return [jnp.dot(cnt.astype(F32).astype(BF16), ones, preferred_element_type=F32) for cnt in cnts]

    def search(needed):
        def bit_body(it, ts):
            cands = [t + (jnp.int32(1) << (15 - it)) for t in ts]
            return tuple(jnp.where(c >= n, cand, t) for c, n, cand, t in zip(count_ge(cands), needed, cands, ts))

        return lax.fori_loop(0, 16, bit_body, tuple(jnp.full((rb, LANES), lo16, jnp.int32) for _ in chunks))

    t_hi = search([float(topk)] * len(chunks))
    top = -lo16 - 1
    above = [jnp.where(t == top, 0.0, c)
             for t, c in zip(t_hi, count_ge([jnp.minimum(t + 1, top) for t in t_hi]))]

    def low_plane(j, carry):
        for rows, tb in zip(chunks, t_hi):
            k32 = keys_scr[j, rows, :]
            low = (k32 & 0xFFFF) + lo16
            half_scr[j, rows, :] = jnp.where((k32 >> 16) == tb, low, lo16).astype(jnp.int16)
        return carry

    lax.fori_loop(0, n_blk, low_plane, 0)
    t_lo = search([topk - a for a in above])
    for rows, th, tl in zip(chunks, t_hi, t_lo):
        thr_scr[rows, :] = jnp.maximum(th * 65536 + (tl - lo16), INT_MIN + 1)

    def count32(pred):
        def count(j, cnts):
            return tuple(cnt + jnp.where(pred(keys_scr[j, rows, :], thr_scr[rows, :]), 1, 0)
                         for cnt, rows in zip(cnts, chunks))

        cnts = lax.fori_loop(0, n_blk, count, tuple(jnp.zeros((rb, LANES), jnp.int32) for _ in chunks))
        return [jnp.dot(cnt.astype(F32).astype(BF16), ones, preferred_element_type=F32) for cnt in cnts]

    surplus = functools.reduce(jnp.maximum, [jnp.max(c) for c in count32(lambda k, t: k >= t)]) - topk

    @pl.when(surplus > 0)
    def _():
        keep = [topk - c for c in count32(lambda k, t: k > t)]
        upto = jnp.where(lax.broadcasted_iota(jnp.int32, (LANES, LANES), 0)
                         <= lax.broadcasted_iota(jnp.int32, (LANES, LANES), 1), 1.0, 0.0).astype(BF16)

        def demote(j, seen):
            out = []
            for rows, kp, sn in zip(chunks, keep, seen):
                k32 = keys_scr[j, rows, :]
                tied = k32 == thr_scr[rows, :]
                tied_bf = jnp.where(tied, 1.0, 0.0).astype(BF16)
                rank = sn + jnp.dot(tied_bf, upto, preferred_element_type=F32)
                keys_scr[j, rows, :] = k32 - jnp.where(tied & (rank > kp), 1, 0)
                out.append(sn + jnp.dot(tied_bf, ones, preferred_element_type=F32))
            return tuple(out)

        lax.fori_loop(0, n_blk, demote, tuple(jnp.zeros((rb, LANES), F32) for _ in chunks))

    hd = A_HEAD_DIM

    def select_mask(blk0, n):
        return [jnp.where(keys_scr[blk0 + c] >= thr_scr[...], 0.0, NEG_BIG) for c in range(n)]

    def logits(h, kblk):
        g = h // A_REP
        s = lax.dot_general(q_ref[:, h * hd:(h + 1) * hd], kblk[:, g * hd:(g + 1) * hd], NT_DIMS,
                            preferred_element_type=F32)
        return [s[:, c * LANES:(c + 1) * LANES] for c in range(s.shape[1] // LANES)]

    def tile_cols(h, col0, n):
        return [tile_ref[h, :, col0 + c * LANES:col0 + (c + 1) * LANES] for c in range(n)]

    mfar_scr[...] = jnp.full(mfar_scr.shape, NEG_BIG, F32)
    mnear_scr[...] = jnp.full(mnear_scr.shape, NEG_BIG, F32)

    def far_mask(blk0, colmask):
        am = select_mask(blk0, sub)
        return am if colmask is None else [a + c for a, c in zip(am, colmask)]

    def max_far(start, blk0, colmask):
        kblk = _key_rows(k_ref, start, kb)
        am = far_mask(blk0, colmask)
        for h in range(A_HEADS):
            mfar_scr[h] = functools.reduce(
                jnp.maximum, [s + a for s, a in zip(logits(h, kblk), am)], mfar_scr[h])

    def max_near(kblk, blk0, col0):
        n = kblk.shape[0] // LANES
        am = select_mask(blk0, n)
        for h in range(A_HEADS):
            mnear_scr[h] = functools.reduce(
                jnp.maximum, [s + a + b for s, a, b in zip(logits(h, kblk), am, tile_cols(h, col0, n))],
                mnear_scr[h])

    prev_start = pl.multiple_of(q0 - LANES, LANES)
    _sweep_far(q0, qb, kb, max_far)
    pl.when(q0 > 0)(lambda: max_near(_key_rows(k_ref, prev_start, LANES), cur_blk - 1, 0))
    max_near(kc_ref[...], cur_blk, LANES)

    for h in range(A_HEADS):
        fb = tab_ref[FAR_BUCKET, h]
        m = jnp.maximum(jnp.max(mfar_scr[h], axis=-1, keepdims=True) + fb,
                        jnp.max(mnear_scr[h], axis=-1, keepdims=True))
        mfar_scr[h] = jnp.broadcast_to(fb - m, (qb, LANES))
        mnear_scr[h] = jnp.broadcast_to(-m, (qb, LANES))
    l_scr[...] = jnp.zeros(l_scr.shape, F32)
    acc_scr[...] = jnp.zeros(acc_scr.shape, F32)

    def accumulate(h, p, vblk):
        g = h // A_REP
        l_scr[h] += functools.reduce(jnp.add, p)
        acc_scr[h] += jnp.dot(jnp.concatenate(p, axis=1).astype(BF16), vblk[:, g * hd:(g + 1) * hd],
                              preferred_element_type=F32)

    def acc_far(start, blk0, colmask):
        kblk = _key_rows(k_ref, start, kb)
        vblk = _key_rows(v_ref, start, kb)
        am = far_mask(blk0, colmask)
        for h in range(A_HEADS):
            shift = mfar_scr[h]
            accumulate(h, [jnp.exp(s + a + shift) for s, a in zip(logits(h, kblk), am)], vblk)

    def acc_near(kblk, vblk, blk0, col0):
        n = kblk.shape[0] // LANES
        am = select_mask(blk0, n)
        for h in range(A_HEADS):
            shift = mnear_scr[h]
            accumulate(h, [jnp.exp(s + a + b + shift) for s, a, b in
                           zip(logits(h, kblk), am, tile_cols(h, col0, n))], vblk)

    _sweep_far(q0, qb, kb, acc_far)
    pl.when(q0 > 0)(lambda: acc_near(_key_rows(k_ref, prev_start, LANES), _key_rows(v_ref, prev_start, LANES),
                                     cur_blk - 1, 0))
    acc_near(kc_ref[...], vc_ref[...], cur_blk, LANES)

    for h in range(A_HEADS):
        o_ref[:, h * hd:(h + 1) * hd] = (
            acc_scr[h] / jnp.sum(l_scr[h], axis=-1, keepdims=True)).astype(o_ref.dtype)


def _dsa_call(rel_bias, tiles, q, qi, iw, far, cur, *, qb, kb, qoff, topk):
    b, t, _ = q.shape
    lf = far[0].shape[1]
    qbr = tiles.shape[2] - LANES
    q0_max = qoff + t - qb
    far_end = -(-q0_max // kb) * kb
    assert t % qb == 0 and qoff % LANES == 0 and far_end <= lf and cur[0].shape[1] == (t // qb) * qbr
    n_slabs = max(far_end, q0_max + qbr) // LANES
    mode = {"pipeline_mode": pl.Buffered(1)} if t // qb > 1 else {}
    far_spec = lambda a: pl.BlockSpec((None, lf, a.shape[2]), lambda bi, i: (bi, 0, 0), **mode)
    cur_spec = lambda a: pl.BlockSpec((None, qbr, a.shape[2]), lambda bi, i: (bi, i, 0))
    return pl.pallas_call(
        functools.partial(_dsa_kernel, qb=qb, kb=kb, qoff=qoff, topk=topk),
        grid=(b, t // qb),
        in_specs=[pl.BlockSpec(memory_space=pltpu.SMEM),
                  pl.BlockSpec((None, qb, q.shape[2]), lambda bi, i: (bi, i, 0)),
                  pl.BlockSpec((None, qb, qi.shape[2]), lambda bi, i: (bi, i, 0)),
                  pl.BlockSpec((None, qb, iw.shape[2]), lambda bi, i: (bi, i, 0)),
                  far_spec(far[0]), far_spec(far[1]), far_spec(far[2]),
                  cur_spec(cur[0]), cur_spec(cur[1]), cur_spec(cur[2]),
                  _resident(tiles.shape, lambda bi, i: (0, 0, 0))],
        out_specs=pl.BlockSpec((None, qb, A_HEADS * A_HEAD_DIM), lambda bi, i: (bi, i, 0)),
        out_shape=jax.ShapeDtypeStruct((b, t, A_HEADS * A_HEAD_DIM), BF16),
        scratch_shapes=[pltpu.VMEM((n_slabs, qb, LANES), jnp.int32),
                        pltpu.VMEM((n_slabs, qb, LANES), jnp.int16),
                        pltpu.VMEM((qb, LANES), jnp.int32),
                        pltpu.VMEM((A_HEADS, qb, LANES), F32),
                        pltpu.VMEM((A_HEADS, qb, LANES), F32),
                        pltpu.VMEM((A_HEADS, qb, LANES), F32),
                        pltpu.VMEM((A_HEADS, qb, A_HEAD_DIM), F32)],
        compiler_params=_params(2),
        name="mixer_a",
    )(rel_bias, q, qi, iw, *far, *cur, tiles)


def _diff_kernel(tab_ref, lam_ref, sg_ref, q_ref, k_ref, v_ref, kc_ref, vc_ref, tile_ref, o_ref,
                 mfar_scr, mnear_scr, l_scr, acc_scr, *, qb, kb, qoff, lam_init):
    head = pl.program_id(1)
    i = pl.program_id(2)
    q0 = qoff + i * qb
    qbr = tile_ref.shape[2] - LANES
    hd = B_HEAD_DIM
    far_bias = tab_ref[FAR_BUCKET, A_HEADS + head]

    def logits(c, kblk):
        s = lax.dot_general(q_ref[:, c * hd:(c + 1) * hd], kblk[:, c * hd:(c + 1) * hd], NT_DIMS,
                            preferred_element_type=F32)
        return [s[:, u * LANES:(u + 1) * LANES] for u in range(s.shape[1] // LANES)]

    def tile_cols(col0, n):
        return [tile_ref[0, :, col0 + u * LANES:col0 + (u + 1) * LANES] for u in range(n)]

    mfar_scr[...] = jnp.full(mfar_scr.shape, NEG_BIG, F32)
    mnear_scr[...] = jnp.full(mnear_scr.shape, NEG_BIG, F32)

    def max_far(start, blk0, colmask):
        kblk = _key_rows(k_ref, start, kb)
        for c in range(2):
            s = logits(c, kblk)
            if colmask is not None:
                s = [x + m for x, m in zip(s, colmask)]
            mfar_scr[c] = functools.reduce(jnp.maximum, s, mfar_scr[c])

    def max_near(kblk, col0):
        for c in range(2):
            mnear_scr[c] = functools.reduce(
                jnp.maximum, [s + b for s, b in zip(logits(c, kblk), tile_cols(col0, kblk.shape[0] // LANES))],
                mnear_scr[c])

    prev_start = pl.multiple_of(q0 - LANES, LANES)
    _sweep_far(q0, qb, kb, max_far)
    pl.when(q0 > 0)(lambda: max_near(_key_rows(k_ref, prev_start, LANES), 0))
    max_near(kc_ref[...], LANES)

    for c in range(2):
        m = jnp.maximum(jnp.max(mfar_scr[c], axis=-1, keepdims=True) + far_bias,
                        jnp.max(mnear_scr[c], axis=-1, keepdims=True))
        mfar_scr[c] = jnp.broadcast_to(far_bias - m, (qb, LANES))
        mnear_scr[c] = jnp.broadcast_to(-m, (qb, LANES))
    l_scr[...] = jnp.zeros(l_scr.shape, F32)
    acc_scr[...] = jnp.zeros(acc_scr.shape, F32)

    def accumulate(c, p, vblk):
        l_scr[c] += functools.reduce(jnp.add, p)
        acc_scr[c] += jnp.dot(jnp.concatenate(p, axis=1).astype(BF16), vblk, preferred_element_type=F32)

    def acc_far(start, blk0, colmask):
        kblk = _key_rows(k_ref, start, kb)
        vblk = _key_rows(v_ref, start, kb)
        for c in range(2):
            shift = mfar_scr[c]
            shifts = [shift] * (kb // LANES) if colmask is None else [shift + m for m in colmask]
            accumulate(c, [jnp.exp(s + sh) for s, sh in zip(logits(c, kblk), shifts)], vblk)

    def acc_near(kblk, vblk, col0):
        for c in range(2):
            shift = mnear_scr[c]
            accumulate(c, [jnp.exp(s + b + shift) for s, b in
                           zip(logits(c, kblk), tile_cols(col0, kblk.shape[0] // LANES))], vblk)

    _sweep_far(q0, qb, kb, acc_far)
    pl.when(q0 > 0)(lambda: acc_near(_key_rows(k_ref, prev_start, LANES), _key_rows(v_ref, prev_start, LANES), 0))
    acc_near(kc_ref[...], vc_ref[...], LANES)

    lam = lam_ref[0, 0]
    o = (acc_scr[0] / jnp.sum(l_scr[0], axis=-1, keepdims=True)
         - lam * (acc_scr[1] / jnp.sum(l_scr[1], axis=-1, keepdims=True)))
    o = o * lax.rsqrt(jnp.mean(o * o, axis=-1, keepdims=True) + EPS) * sg_ref[...]
    o_ref[...] = (o * (1.0 - lam_init)).astype(o_ref.dtype)


def _diff_call(rel_bias, tiles, lam, subln_g, q, far, cur, *, qb, kb, qoff, lam_init):
    b, t, _ = q.shape
    lf = far[0].shape[1]
    hw = 2 * B_HEAD_DIM
    qbr = tiles.shape[2] - LANES
    q0_max = qoff + t - qb
    far_end = -(-max(q0_max - LANES, 0) // kb) * kb
    assert t % qb == 0 and qoff % LANES == 0 and far_end <= lf and cur[0].shape[1] == (t // qb) * qbr
    far_spec = pl.BlockSpec((None, lf, hw), lambda bi, h, i: (bi, 0, h))
    cur_spec = pl.BlockSpec((None, qbr, hw), lambda bi, h, i: (bi, i, h))
    return pl.pallas_call(
        functools.partial(_diff_kernel, qb=qb, kb=kb, qoff=qoff, lam_init=lam_init),
        grid=(b, B_HEADS, t // qb),
        in_specs=[pl.BlockSpec(memory_space=pltpu.SMEM),
                  pl.BlockSpec(memory_space=pltpu.SMEM),
                  pl.BlockSpec((1, hw), lambda bi, h, i: (0, 0)),
                  pl.BlockSpec((None, qb, hw), lambda bi, h, i: (bi, i, h)),
                  far_spec, far_spec, cur_spec, cur_spec,
                  pl.BlockSpec((1, qb, tiles.shape[2]), lambda bi, h, i: (A_HEADS + h, 0, 0))],
        out_specs=pl.BlockSpec((None, qb, hw), lambda bi, h, i: (bi, i, h)),
        out_shape=jax.ShapeDtypeStruct((b, t, B_HEADS * hw), BF16),
        scratch_shapes=[pltpu.VMEM((2, qb, LANES), F32),
                        pltpu.VMEM((2, qb, LANES), F32),
                        pltpu.VMEM((2, qb, LANES), F32),
                        pltpu.VMEM((2, qb, hw), F32)],
        compiler_params=_params(3),
        name="mixer_b",
    )(rel_bias, lam, subln_g.reshape(1, hw), q, *far, *cur, tiles)


def _lam_kernel(dl_ref, o_ref, *, lam_init):
    dl = dl_ref[...]
    s1 = jnp.sum(dl[0:1] * dl[1:2], axis=-1, keepdims=True)
    s2 = jnp.sum(dl[2:3] * dl[3:4], axis=-1, keepdims=True)
    o_ref[...] = jnp.exp(s1) - jnp.exp(s2) + lam_init


def _lam_call(diff_lam, lam_init):
    return pl.pallas_call(
        functools.partial(_lam_kernel, lam_init=lam_init),
        out_shape=jax.ShapeDtypeStruct((1, 1), F32),
        name="diff_lambda",
    )(diff_lam)


def _route(logits_t, rbias, counts):
    tm = logits_t.shape[1]
    per = N_EXPERTS // N_GROUPS
    scores = jax.nn.sigmoid(logits_t).reshape(N_GROUPS, per, tm)
    ch = scores + rbias.reshape(N_GROUPS, per, 1)
    sub = lax.broadcasted_iota(jnp.int32, ch.shape, 1)
    grp = lax.broadcasted_iota(jnp.int32, ch.shape, 0)
    m1 = jnp.max(ch, axis=1, keepdims=True)
    i1 = jnp.min(jnp.where(ch == m1, sub, per), axis=1, keepdims=True)
    m2 = jnp.max(jnp.where(sub == i1, -jnp.inf, ch), axis=1, keepdims=True)
    gs = jnp.broadcast_to(m1 + m2, ch.shape)
    rank = jnp.zeros(ch.shape, jnp.int32)
    for g2 in range(N_GROUPS):
        o = gs[g2:g2 + 1]
        rank = rank + jnp.where(o > gs, 1, 0) + jnp.where(o == gs, 1, 0) * jnp.where(grp > g2, 1, 0)
    cm = jnp.where(rank < TOPK_GROUPS, ch, -jnp.inf)
    eid = grp * per + sub
    erank = jnp.zeros(cm.shape, jnp.int32)
    for e2 in range(N_EXPERTS):
        o = cm[e2 // per:e2 // per + 1, e2 % per:e2 % per + 1, :]
        erank = erank + jnp.where(o > cm, 1, 0) + jnp.where(o == cm, 1, 0) * jnp.where(eid > e2, 1, 0)
    chosen = erank < TOP_K
    sel = jnp.where(chosen, scores, 0.0)
    denom = jnp.sum(jnp.sum(sel, axis=1, keepdims=True), axis=0, keepdims=True)
    gates = sel / denom * ROUTED_SCALE

    ones = jnp.where(chosen, 1.0, 0.0).reshape(N_EXPERTS, tm)
    earlier = (lax.broadcasted_iota(jnp.int32, (tm, tm), 0) < lax.broadcasted_iota(jnp.int32, (tm, tm), 1))
    before = jnp.dot(ones.astype(BF16), jnp.where(earlier, 1.0, 0.0).astype(BF16), preferred_element_type=F32)
    pos = (before + counts.astype(F32)).reshape(N_GROUPS, per, tm)
    new_counts = counts + jnp.sum(ones, axis=1, keepdims=True).astype(jnp.int32)

    def per_slot(values):
        rows = [jnp.sum(jnp.sum(jnp.where(erank == k, values, 0.0), axis=1, keepdims=True), axis=0)
                for k in range(TOP_K)]
        return jnp.concatenate(rows, axis=0)

    return (per_slot(eid.astype(F32)).astype(jnp.int32), per_slot(pos).astype(jnp.int32), per_slot(gates),
            new_counts)


def _pack_pair(a, b):
    ua = pltpu.bitcast(a.astype(BF16).astype(F32), jnp.uint32)
    ub = pltpu.bitcast(b.astype(BF16).astype(F32), jnp.uint32)
    return pltpu.bitcast(ua | (ub >> 16), jnp.int32)


def _unpack_pair(p):
    u = pltpu.bitcast(p, jnp.uint32)
    a = pltpu.bitcast(u & jnp.uint32(0xFFFF0000), F32)
    b = pltpu.bitcast(u << 16, F32)
    return a.astype(BF16), b.astype(BF16)


PARTS = 4
PART_W = D_MODEL // 2 // PARTS


def _pack_rows(x):
    half = x.shape[1] // 2
    return jnp.stack([_pack_pair(x[:, j * PART_W:(j + 1) * PART_W],
                                 x[:, half + j * PART_W:half + (j + 1) * PART_W]) for j in range(PARTS)])


def _packed_matmul(parts_ref, w_ref):
    half = w_ref.shape[0] // 2
    acc = None
    for j in range(PARTS):
        a, b = _unpack_pair(parts_ref[j])
        t = (jnp.dot(a, w_ref[j * PART_W:(j + 1) * PART_W, :], preferred_element_type=F32)
             + jnp.dot(b, w_ref[half + j * PART_W:half + (j + 1) * PART_W, :], preferred_element_type=F32))
        acc = t if acc is None else acc + t
    return acc


def _outproj_kernel(x_ref, oa_ref, ob_ref, ga_ref, sc_ref, sh_ref, g_ref, w_ref, wr_ref, rb_ref, cin_ref,
                    x1_ref, h2_ref, ek_ref, pk_ref, gk_ref, cout_ref, cnt_scr):
    @pl.when(pl.program_id(0) == 0)
    def _():
        cnt_scr[...] = cin_ref[...]

    half = oa_ref.shape[1]
    mix = (jnp.dot(oa_ref[...], w_ref[:half, :], preferred_element_type=F32)
           + jnp.dot(ob_ref[...], w_ref[half:, :], preferred_element_type=F32))
    x1 = x_ref[...] + ga_ref[...] * mix
    x1_ref[...] = x1
    h2 = _modulated_norm(x1, g_ref[...], sc_ref[...], sh_ref[...])
    h2_ref[...] = _pack_rows(h2)
    logits_t = lax.dot_general(wr_ref[...], h2, NT_DIMS, preferred_element_type=F32,
                               precision=lax.Precision.HIGHEST)
    ek, pk, gk, counts = _route(logits_t, rb_ref[...], cnt_scr[...])
    ek_ref[...] = ek
    pk_ref[...] = pk
    gk_ref[...] = gk
    cnt_scr[...] = counts
    cout_ref[...] = counts


def _outproj_call(x2d, oa, ob, ga, sc, sh, mod_map, g, w_bf, wr_t, rbias, counts_in, tm):
    n, d = x2d.shape
    assert n % tm == 0
    r = sc.shape[1]
    mod_spec = pl.BlockSpec((None, r, d), mod_map)
    slot_spec = pl.BlockSpec((TOP_K, tm), lambda i: (0, i))
    return pl.pallas_call(
        _outproj_kernel,
        grid=(n // tm,),
        in_specs=[pl.BlockSpec((tm, d), lambda i: (i, 0)),
                  pl.BlockSpec((tm, oa.shape[1]), lambda i: (i, 0)),
                  pl.BlockSpec((tm, ob.shape[1]), lambda i: (i, 0)),
                  mod_spec, mod_spec, mod_spec,
                  pl.BlockSpec((1, d), lambda i: (0, 0)),
                  _resident(w_bf.shape, lambda i: (0, 0)),
                  _resident(wr_t.shape, lambda i: (0, 0)),
                  pl.BlockSpec((N_EXPERTS, 1), lambda i: (0, 0)),
                  pl.BlockSpec((N_EXPERTS, 1), lambda i: (0, 0))],
        out_specs=[pl.BlockSpec((tm, d), lambda i: (i, 0)),
                   pl.BlockSpec((PARTS, tm, PART_W), lambda i: (0, i, 0)),
                   slot_spec, slot_spec, slot_spec,
                   pl.BlockSpec((N_EXPERTS, 1), lambda i: (0, 0))],
        out_shape=[jax.ShapeDtypeStruct((n, d), F32),
                   jax.ShapeDtypeStruct((PARTS, n, PART_W), jnp.int32),
                   jax.ShapeDtypeStruct((TOP_K, n), jnp.int32),
                   jax.ShapeDtypeStruct((TOP_K, n), jnp.int32),
                   jax.ShapeDtypeStruct((TOP_K, n), F32),
                   jax.ShapeDtypeStruct((N_EXPERTS, 1), jnp.int32)],
        scratch_shapes=[pltpu.VMEM((N_EXPERTS, 1), jnp.int32)],
        compiler_params=_params(1),
        name="out_proj_router",
    )(x2d, oa, ob, ga, sc, sh, g.reshape(1, d), w_bf, wr_t, rbias, counts_in)


SC_WINDOW = 128
EXPERT_TILE = 512


def _sc_mesh():
    return plsc.VectorSubcoreMesh(core_axis_name="core", subcore_axis_name="subcore")


def _sc_scatter_rows(x, idx, n_out):
    n, w = x.shape
    steps = n // SC_WINDOW
    total = idx.shape[1] // SC_WINDOW

    @pl.kernel(out_type=jax.ShapeDtypeStruct((n_out, w), x.dtype), mesh=_sc_mesh(), scratch_types=[],
               name="moe_dispatch")
    def scatter(x_hbm, i_hbm, o_hbm):
        def body(x_vmem, i_vmem):
            pltpu.sync_copy(x_vmem, o_hbm.at[i_vmem.at[0]])

        pltpu.emit_pipeline(
            body, grid=(total,),
            in_specs=[pl.BlockSpec((SC_WINDOW, w), lambda i: (i % steps, 0)),
                      pl.BlockSpec((1, SC_WINDOW), lambda i: (0, i))],
            out_specs=[], core_axis_name=("core", "subcore"),
            dimension_semantics=(pltpu.PARALLEL,))(x_hbm, i_hbm)

    return scatter(x, idx)


def _sc_gather_rows(y, idx):
    w = y.shape[1]
    n = idx.shape[1]

    @pl.kernel(out_type=jax.ShapeDtypeStruct((n, w), y.dtype), mesh=_sc_mesh(), scratch_types=[],
               name="moe_collect")
    def gather(y_hbm, i_hbm, o_hbm):
        def body(i_vmem, o_vmem):
            pltpu.sync_copy(y_hbm.at[i_vmem.at[0]], o_vmem)

        pltpu.emit_pipeline(
            body, grid=(n // SC_WINDOW,),
            in_specs=[pl.BlockSpec((1, SC_WINDOW), lambda i: (0, i))],
            out_specs=[pl.BlockSpec((SC_WINDOW, w), lambda i: (i, 0))],
            core_axis_name=("core", "subcore"),
            dimension_semantics=(pltpu.PARALLEL,))(i_hbm, o_hbm)

    return gather(y, idx)


def _swiglu(parts_ref, wg_ref, wu_ref, wd_ref):
    a = _packed_matmul(parts_ref, wg_ref)
    u = _packed_matmul(parts_ref, wu_ref)
    hid = (a * jax.nn.sigmoid(a) * u).astype(BF16)
    return jnp.dot(hid, wd_ref[...], preferred_element_type=F32)


def _expert_kernel(te_ref, nu_ref, x_ref, wg_ref, wu_ref, wd_ref, y_ref, wg_scr, wu_scr, wd_scr):
    i = pl.program_id(0)

    @pl.when(i < nu_ref[0])
    def _():
        @pl.when((i == 0) | (te_ref[i] != te_ref[jnp.maximum(i - 1, 0)]))
        def _():
            wg_scr[...] = wg_ref[...].astype(BF16)
            wu_scr[...] = wu_ref[...].astype(BF16)
            wd_scr[...] = wd_ref[...].astype(BF16)

        y_ref[...] = _pack_rows(_swiglu(x_ref, wg_scr, wu_scr, wd_scr))


def _expert_call(tile_expert, n_used, x_sorted, wg, wu, wd):
    _, r, _ = x_sorted.shape
    ne, d, de = wg.shape
    n_tiles = r // EXPERT_TILE
    row_map = lambda i, te, nu: (0, jnp.minimum(i, nu[0] - 1), 0)
    return pl.pallas_call(
        _expert_kernel,
        grid_spec=pltpu.PrefetchScalarGridSpec(
            num_scalar_prefetch=2,
            grid=(n_tiles,),
            in_specs=[pl.BlockSpec((PARTS, EXPERT_TILE, PART_W), row_map),
                      pl.BlockSpec((None, d, de), lambda i, te, nu: (te[i], 0, 0)),
                      pl.BlockSpec((None, d, de), lambda i, te, nu: (te[i], 0, 0)),
                      pl.BlockSpec((None, de, d), lambda i, te, nu: (te[i], 0, 0))],
            out_specs=pl.BlockSpec((PARTS, EXPERT_TILE, PART_W), row_map),
            scratch_shapes=[pltpu.VMEM((d, de), BF16), pltpu.VMEM((d, de), BF16), pltpu.VMEM((de, d), BF16)]),
        out_shape=jax.ShapeDtypeStruct(x_sorted.shape, jnp.int32),
        compiler_params=_params(1),
        name="moe_experts",
    )(tile_expert, n_used, x_sorted, wg, wu, wd)


def _final_kernel(x_ref, h_ref, rows_ref, gk_ref, gf_ref, g_ref, sg_ref, su_ref, sd_ref, o_ref):
    moe = _swiglu(h_ref, sg_ref, su_ref, sd_ref)
    half = moe.shape[1] // 2
    lo = [moe[:, j * PART_W:(j + 1) * PART_W] for j in range(PARTS)]
    hi = [moe[:, half + j * PART_W:half + (j + 1) * PART_W] for j in range(PARTS)]
    for k in range(TOP_K):
        gate = gk_ref[:, k:k + 1]
        for j in range(PARTS):
            a, b = _unpack_pair(rows_ref[k, j])
            lo[j] = lo[j] + gate * a.astype(F32)
            hi[j] = hi[j] + gate * b.astype(F32)
    x = x_ref[...] + gf_ref[...] * jnp.concatenate(lo + hi, axis=1)
    o_ref[...] = x * lax.rsqrt(jnp.mean(x * x, axis=-1, keepdims=True) + EPS) * g_ref[...]


def _final_call(x1, h2p, rows, row_block0, gk_t, gf, mod_map, g, sg, su, sd, tm):
    n, d = x1.shape
    assert n % tm == 0
    r = gf.shape[1]
    return pl.pallas_call(
        _final_kernel,
        grid=(n // tm,),
        in_specs=[pl.BlockSpec((tm, d), lambda i: (i, 0)),
                  pl.BlockSpec((PARTS, tm, PART_W), lambda i: (0, i, 0)),
                  pl.BlockSpec((TOP_K, PARTS, tm, PART_W), lambda i: (0, 0, i + row_block0, 0)),
                  pl.BlockSpec((tm, TOP_K), lambda i: (i, 0)),
                  pl.BlockSpec((None, r, d), mod_map),
                  pl.BlockSpec((1, d), lambda i: (0, 0)),
                  _resident(sg.shape, lambda i: (0, 0)),
                  _resident(su.shape, lambda i: (0, 0)),
                  _resident(sd.shape, lambda i: (0, 0))],
        out_specs=pl.BlockSpec((tm, d), lambda i: (i, 0)),
        out_shape=jax.ShapeDtypeStruct((n, d), F32),
        compiler_params=_params(1),
        name="final_norm",
    )(x1, h2p, rows, gk_t, gf, g.reshape(1, d), sg, su, sd)


_QA, _KA, _VA, _QI, _KI, _IW, _QB, _KB, _VB = range(9)
_IN_SIZES = (A_HEADS * A_HEAD_DIM, A_KV_HEADS * A_HEAD_DIM, A_KV_HEADS * A_HEAD_DIM,
             IDX_HEADS * IDX_DIM, IDX_DIM, IDX_HEADS,
             B_HEADS * 2 * B_HEAD_DIM, B_HEADS * 2 * B_HEAD_DIM, B_HEADS * 2 * B_HEAD_DIM)


def _split_w_in(w_in):
    d = w_in.shape[0]
    offs = np.concatenate([[0], np.cumsum(_IN_SIZES)])
    cols = [w_in[:, offs[k]:offs[k + 1]].astype(BF16) for k in range(9)]
    zero = lambda n: jnp.zeros((d, n), BF16)
    qi = jnp.concatenate([cols[_QI].reshape(d, IDX_HEADS, IDX_DIM),
                          jnp.zeros((d, IDX_HEADS, LANES - IDX_DIM), BF16)], axis=-1).reshape(d, IDX_HEADS * LANES)
    w_a = jnp.concatenate([cols[_QA], cols[_KA], cols[_VA], qi,
                           cols[_KI], zero(LANES - IDX_DIM), cols[_IW], zero(LANES - IDX_HEADS)], axis=1)
    w_b = jnp.concatenate([cols[_QB], cols[_KB], cols[_VB]], axis=1)
    return w_a, w_b


def _group_forward(x, mod, past, rel_bias, lam, weights, *, lam_init, qb, qb_b, kb, tm, per_row_mod):
    (norm_a_g, w_a, w_b, subln_g) = weights
    b, t, d = x.shape
    n = b * t
    x2d = x.reshape(n, d)
    sh_a, sc_a = mod[0], mod[1]
    if per_row_mod:
        expand = lambda m: jnp.repeat(m, t, axis=0).reshape(1, n, d)
        mod_map = lambda i: (0, i, 0)
    else:
        expand = lambda m: m.reshape(b, 1, d)
        mod_map = lambda i: ((i * tm) // t, 0, 0)
    sc3, sh3 = expand(sc_a), expand(sh_a)

    na = A_HEADS * A_HEAD_DIM
    nk = A_KV_HEADS * A_HEAD_DIM
    nq = IDX_HEADS * LANES
    c_ka, c_va, c_qi = na, na + nk, na + 2 * nk
    c_ki, c_iw = c_qi + nq, c_qi + nq + LANES
    defs_a = [(0, na, na, A_HEAD_DIM ** -0.5), (c_ka, nk, nk, None), (c_ka, nk, nk, None),
              (c_va, nk, nk, None), (c_va, nk, nk, None), (c_qi, nq, nq, None),
              (c_ki, LANES, IDX_DIM, None), (c_ki, LANES, LANES, None), (c_iw, LANES, LANES, None)]
    dt_a = [BF16, F32, BF16, F32, BF16, BF16, F32, BF16, F32]
    qa, ka, ka_bf, va, va_bf, qi, ki, ki_bf, iw = _norm_proj_call(
        x2d, sc3, sh3, mod_map, norm_a_g, w_a, defs_a, dt_a, tm, "in_proj_a")
    nb = B_HEADS * 2 * B_HEAD_DIM
    defs_b = [(0, nb, nb, B_HEAD_DIM ** -0.5), (nb, nb, nb, None), (nb, nb, nb, None),
              (2 * nb, nb, nb, None), (2 * nb, nb, nb, None)]
    dt_b = [BF16, F32, BF16, F32, BF16]
    qbm, kbm, kb_bf, vbm, vb_bf = _norm_proj_call(
        x2d, sc3, sh3, mod_map, norm_a_g, w_b, defs_b, dt_b, tm, "in_proj_b")

    r3 = lambda a: a.reshape(b, t, a.shape[-1])
    cur = [r3(ka_bf), r3(va_bf), r3(ki_bf), r3(kb_bf), r3(vb_bf)]
    if past is None:
        qoff = 0
        far = cur
    else:
        assert t == qb, "a group with cached keys is one query block per batch row"
        qoff = past[0].shape[1]
        qbr = -(-qb // LANES) * LANES
        cur = [jnp.concatenate([a, jnp.zeros((b, qbr - t, a.shape[-1]), BF16)], axis=1) for a in cur]
        far = [p.reshape(b, qoff, -1) for p in past]
        short = -qoff % kb
        if short:
            far = [jnp.concatenate([p, jnp.zeros((b, short, p.shape[-1]), p.dtype)], axis=1) for p in far]
    length = qoff + t
    topk = min(TOPK_MAX, length // 4)

    tiles = _bias_tile_call(rel_bias, qb)
    tiles_b = tiles if qb_b == qb else _bias_tile_call(rel_bias, qb_b)
    out_a = _dsa_call(rel_bias, tiles, r3(qa), r3(qi), r3(iw), far[:3], cur[:3], qb=qb, kb=kb, qoff=qoff, topk=topk)
    out_b = _diff_call(rel_bias, tiles_b, lam, subln_g, r3(qbm), far[3:], cur[3:], qb=qb_b, kb=kb, qoff=qoff,
                       lam_init=lam_init)
    rows = (ka.reshape(1, b, t, A_KV_HEADS, A_HEAD_DIM), va.reshape(1, b, t, A_KV_HEADS, A_HEAD_DIM),
            ki.reshape(1, b, t, IDX_DIM), kbm.reshape(1, b, t, B_HEADS, 2, B_HEAD_DIM),
            vbm.reshape(1, b, t, B_HEADS, 2 * B_HEAD_DIM))
    return x2d, out_a.reshape(n, -1), out_b.reshape(n, -1), rows, expand, mod_map


def kernel(x_prompt, x_sample, c_prompt, c_sample, cache_a_k, cache_a_v, cache_a_kidx, cache_b_k, cache_b_v,
           rel_bias, w_ada, b_ada, norm_a_g, w_in, w_out, diff_lam, subln_g, norm_f_g, w_router, router_bias,
           w_gate, w_up, w_down, ws_gate, ws_up, ws_down, final_g):
    assert w_ada.shape[0] == 1, "single-layer model"
    d = x_prompt.shape[-1]
    bp, tp, _ = x_prompt.shape
    bs, ts, _ = x_sample.shape
    lam_init = 0.8 - 0.6 * math.exp(-0.3 * 0)

    c_all = jnp.concatenate([c_prompt, c_sample], axis=0)
    rows = -(-c_all.shape[0] // 8) * 8
    c_pad = jnp.concatenate([c_all, jnp.zeros((rows - c_all.shape[0], d), F32)], axis=0)
    mod_all = _ada_call(c_pad, w_ada[0], b_ada[0])
    mod_p = [mod_all[:bp, k * d:(k + 1) * d] for k in range(6)]
    mod_s = [mod_all[bp:bp + bs, k * d:(k + 1) * d] for k in range(6)]

    w_a, w_b = _split_w_in(w_in[0])
    lam = _lam_call(diff_lam[0], lam_init)
    weights = (norm_a_g[0], w_a, w_b, subln_g[0])
    past = (cache_a_k[0], cache_a_v[0], cache_a_kidx[0], cache_b_k[0], cache_b_v[0])

    tm_p = 256
    tm_s = bs * ts
    xp2, oa_p, ob_p, rows_p, exp_p, map_p = _group_forward(
        x_prompt, mod_p, None, rel_bias, lam, weights, lam_init=lam_init, qb=256, qb_b=512, kb=512, tm=tm_p,
        per_row_mod=False)
    xs2, oa_s, ob_s, rows_s, exp_s, map_s = _group_forward(
        x_sample, mod_s, past, rel_bias, lam, weights, lam_init=lam_init, qb=ts, qb_b=ts, kb=512, tm=tm_s,
        per_row_mod=True)

    w_out_bf = w_out[0].astype(BF16)
    wr_t = w_router[0].T
    rb = router_bias[0].reshape(N_EXPERTS, 1)
    no_rows = jnp.zeros((N_EXPERTS, 1), jnp.int32)
    x1_p, h2_p, ek_p, pk_p, gk_p, cnt_p = _outproj_call(
        xp2, oa_p, ob_p, exp_p(mod_p[2]), exp_p(mod_p[4]), exp_p(mod_p[3]), map_p, norm_f_g[0], w_out_bf, wr_t, rb,
        no_rows, tm_p)
    x1_s, h2_s, ek_s, pk_s, gk_s, cnt_all = _outproj_call(
        xs2, oa_s, ob_s, exp_s(mod_s[2]), exp_s(mod_s[4]), exp_s(mod_s[3]), map_s, norm_f_g[0], w_out_bf, wr_t, rb,
        cnt_p, tm_s)

    n_p, n_s = bp * tp, bs * ts
    n_all = n_p + n_s
    n_rows = -(-(n_all * TOP_K) // EXPERT_TILE) * EXPERT_TILE + N_EXPERTS * EXPERT_TILE
    counts = cnt_all[:, 0]
    padded = (counts + EXPERT_TILE - 1) // EXPERT_TILE * EXPERT_TILE
    ends = jnp.cumsum(padded)
    starts = ends - padded
    tile_first_row = jnp.arange(n_rows // EXPERT_TILE, dtype=jnp.int32) * EXPERT_TILE
    tile_expert = jnp.minimum(jnp.sum(ends[None, :] <= tile_first_row[:, None], axis=1), N_EXPERTS - 1).astype(jnp.int32)
    n_used = (ends[-1:] // EXPERT_TILE).astype(jnp.int32)
    ek = jnp.concatenate([ek_p, ek_s], axis=1)
    expert_ids = jnp.arange(N_EXPERTS, dtype=jnp.int32)[:, None, None]
    dest = jnp.sum(jnp.where(ek[None] == expert_ids, starts[:, None, None], 0), axis=0) \
        + jnp.concatenate([pk_p, pk_s], axis=1)
    piece = jnp.arange(PARTS, dtype=jnp.int32)[None, :, None]
    idx = (dest[:, None, :] + piece * n_rows).reshape(1, TOP_K * PARTS * n_all).astype(jnp.int32)

    h2_all = jnp.concatenate([h2_p, h2_s], axis=1).reshape(PARTS * n_all, PART_W)
    x_sorted = _sc_scatter_rows(h2_all, idx, PARTS * n_rows).reshape(PARTS, n_rows, PART_W)
    y_sorted = _expert_call(tile_expert, n_used, x_sorted, w_gate[0], w_up[0], w_down[0])
    rows = _sc_gather_rows(y_sorted.reshape(PARTS * n_rows, PART_W), idx).reshape(TOP_K, PARTS, n_all, PART_W)

    shared = (ws_gate[0].astype(BF16), ws_up[0].astype(BF16), ws_down[0].astype(BF16))
    assert n_p % tm_s == 0
    y_p = _final_call(x1_p, h2_p, rows, 0, gk_p.T, exp_p(mod_p[5]), map_p, final_g, *shared, tm_p)
    y_s = _final_call(x1_s, h2_s, rows, n_p // tm_s, gk_s.T, exp_s(mod_s[5]), map_s, final_g, *shared, tm_s)
    return (y_p.reshape(bp, tp, d), y_s.reshape(bs, ts, d)) + rows_p + rows_s
```

```python
import functools
import math

import numpy as np
import jax
import jax.numpy as jnp
from jax import lax
from jax.experimental import pallas as pl
from jax.experimental.pallas import tpu as pltpu
from jax.experimental.pallas import tpu_sc as plsc

D_MODEL = 2048
CHUNK = 64
A_HEADS = 8
A_KV_HEADS = 2
A_REP = A_HEADS // A_KV_HEADS
A_HEAD_DIM = 128
IDX_HEADS = 16
IDX_DIM = 64
TOPK_MAX = 256
B_HEADS = 4
B_HEAD_DIM = 128
N_REL_BUCKETS = 32
REL_MAX_DIST = 128
N_ATTN_HEADS = A_HEADS + B_HEADS
N_EXPERTS = 64
TOP_K = 8
N_GROUPS = 8
TOPK_GROUPS = 4
EXPERT_DIM = 512
SHARED_DIM = 512
ROUTED_SCALE = 2.5
EPS = 1e-6

LANES = 128
INT_MIN = -(2 ** 31)
NEG_BIG = -1e30
VMEM_LIMIT = 52 * 1024 * 1024

F32 = jnp.float32
BF16 = jnp.bfloat16
NT_DIMS = (((1,), (1,)), ((), ()))


def _params(n_axes):
    return pltpu.CompilerParams(dimension_semantics=("arbitrary",) * n_axes,
                                vmem_limit_bytes=VMEM_LIMIT)


def _resident(shape, index_map):
    return pl.BlockSpec(shape, index_map, pipeline_mode=pl.Buffered(1))


def _bucket_thresholds():
    nb = N_REL_BUCKETS // 2
    max_exact = nb // 2
    n = np.arange(max_exact, 4 * REL_MAX_DIST, dtype=np.int32)
    nf = n.astype(np.float32)
    large = max_exact + (np.log(nf / np.float32(max_exact)) / np.float32(math.log(REL_MAX_DIST / max_exact))
                         * np.float32(nb - max_exact)).astype(np.int32)
    large = np.minimum(large, nb - 1)
    assert np.all(np.diff(large) >= 0)
    ths = [int(n[np.argmax(large >= b)]) for b in range(max_exact + 1, nb)]
    return max_exact, nb, ths


_MAX_EXACT, _NB, _BUCKET_THS = _bucket_thresholds()
assert _BUCKET_THS[-1] <= LANES + 1
FAR_BUCKET = _NB - 1


def _ada_kernel(c_ref, w_ref, b_ref, o_ref):
    c = c_ref[...]
    s = c * jax.nn.sigmoid(c)
    o_ref[...] = jnp.dot(s, w_ref[...], preferred_element_type=F32,
                         precision=lax.Precision.HIGHEST) + b_ref[...]


def _ada_call(c, w_ada, b_ada):
    rows, d = c.shape
    n = w_ada.shape[1]
    tn = 1536
    assert n % tn == 0
    return pl.pallas_call(
        _ada_kernel,
        grid=(n // tn,),
        in_specs=[pl.BlockSpec((rows, d), lambda j: (0, 0)),
                  pl.BlockSpec((d, tn), lambda j: (0, j)),
                  pl.BlockSpec((1, tn), lambda j: (0, j))],
        out_specs=pl.BlockSpec((rows, tn), lambda j: (0, j)),
        out_shape=jax.ShapeDtypeStruct((rows, n), F32),
        compiler_params=_params(1),
        name="ada_mod",
    )(c, w_ada, b_ada.reshape(1, n))


def _modulated_norm(x, g, sc, sh):
    y = x * lax.rsqrt(jnp.mean(x * x, axis=-1, keepdims=True) + EPS) * g
    return y * (1.0 + sc) + sh


def _norm_proj_kernel(x_ref, sc_ref, sh_ref, g_ref, w_ref, *out_refs, defs):
    h = _modulated_norm(x_ref[...], g_ref[...], sc_ref[...], sh_ref[...]).astype(BF16)
    done = {}
    tm = x_ref.shape[0]
    for (start, width, store_w, mult, order), o_ref in zip(defs, out_refs):
        if (start, width) not in done:
            done[(start, width)] = jnp.dot(h, w_ref[:, start:start + width], preferred_element_type=F32)
        r = done[(start, width)][:, :store_w]
        if order is None:
            o_ref[...] = (r if mult is None else r * mult).astype(o_ref.dtype)
        else:
            for p, cb in enumerate(order):
                o_ref[pl.ds(p, tm, stride=len(order)), :] = r[:, cb * LANES:(cb + 1) * LANES].astype(o_ref.dtype)


def _norm_proj_call(x2d, sc, sh, mod_map, g, w_bf, defs, dtypes, tm, name):
    n, d = x2d.shape
    assert n % tm == 0
    r = sc.shape[1]
    shape = lambda rows, d_: (rows, d_[2]) if d_[4] is None else (rows * len(d_[4]), LANES)
    out_shapes = [jax.ShapeDtypeStruct(shape(n, d_), dt) for d_, dt in zip(defs, dtypes)]
    out_specs = [pl.BlockSpec(shape(tm, d_), lambda i: (i, 0)) for d_ in defs]
    return pl.pallas_call(
        functools.partial(_norm_proj_kernel, defs=tuple(defs)),
        grid=(n // tm,),
        in_specs=[pl.BlockSpec((tm, d), lambda i: (i, 0)),
                  pl.BlockSpec((None, r, d), mod_map),
                  pl.BlockSpec((None, r, d), mod_map),
                  pl.BlockSpec((1, d), lambda i: (0, 0)),
                  _resident(w_bf.shape, lambda i: (0, 0))],
        out_specs=out_specs,
        out_shape=out_shapes,
        compiler_params=_params(1),
        name=name,
    )(x2d, sc, sh, g.reshape(1, d), w_bf)


def _bias_tile_kernel(tab_ref, o_ref, *, qb):
    _, rows, width = o_ref.shape
    r = lax.broadcasted_iota(jnp.int32, (rows, width), 0)
    koff = lax.broadcasted_iota(jnp.int32, (rows, width), 1) - LANES
    rel = koff - r
    n = jnp.abs(rel)
    large = jnp.full((rows, width), _MAX_EXACT, jnp.int32)
    for th in _BUCKET_THS:
        large = large + jnp.where(n >= th, 1, 0)
    bucket = jnp.where(rel > 0, _NB, 0) + jnp.where(n < _MAX_EXACT, n, large)
    visible = ((koff >> 6) <= (r >> 6)) & (koff < qb)
    for h in range(N_ATTN_HEADS):
        b = jnp.zeros((rows, width), F32)
        for bk in range(N_REL_BUCKETS):
            b = jnp.where(bucket == bk, tab_ref[bk, h], b)
        o_ref[h] = jnp.where(visible, b, NEG_BIG)


def _bias_tile_call(rel_bias, qb):
    qbr = -(-qb // LANES) * LANES
    return pl.pallas_call(
        functools.partial(_bias_tile_kernel, qb=qb),
        in_specs=[pl.BlockSpec(memory_space=pltpu.SMEM)],
        out_specs=pl.BlockSpec(memory_space=pltpu.VMEM),
        out_shape=jax.ShapeDtypeStruct((N_ATTN_HEADS, qb, LANES + qbr), F32),
        name="bias_tiles",
    )(rel_bias)


def _sweep_far(q0, qb, kb, fn):
    far_end = jnp.maximum(q0 - LANES, 0)
    n_full = far_end // kb
    sub = kb // LANES

    def whole(j):
        fn(pl.multiple_of(j * kb, kb), j * sub, None)

    def pair(j, carry):
        whole(2 * j)
        whole(2 * j + 1)
        return carry

    lax.fori_loop(0, n_full // 2, pair, 0)
    pl.when(n_full % 2 == 1)(lambda: whole(n_full - 1))

    @pl.when(far_end > n_full * kb)
    def _():
        start = pl.multiple_of(n_full * kb, kb)
        col = start + lax.broadcasted_iota(jnp.int32, (qb, LANES), 1)
        fn(start, n_full * sub, [jnp.where(col + c * LANES < far_end, 0.0, NEG_BIG) for c in range(sub)])


def _key_rows(ref, start, width, pieces=None):
    if pieces is None:
        blk = ref[pl.ds(start, width), :].astype(BF16)
        pad = -blk.shape[1] % LANES
        return blk if pad == 0 else jnp.concatenate([blk, jnp.zeros((width, pad), BF16)], axis=1)
    if len(ref.shape) == 3:
        return jnp.concatenate([ref[pl.ds(start, width), p, :].astype(BF16) for p in pieces], axis=1)
    r, rows = pieces
    window = ref.at[pl.ds(pl.multiple_of(start * r, LANES * r), width * r), :]
    return jnp.concatenate([window[pl.ds(p, width, stride=r), :].astype(BF16) for p in rows], axis=1)


def _dsa_kernel(tab_ref, q_ref, qi_ref, iw_ref, k_ref, v_ref, ki_ref, kc_ref, vc_ref, kic_ref, tile_ref, o_ref,
                keys_scr, half_scr, thr_scr, mfar_scr, mnear_scr, l_scr, acc_scr, *, qb, kb, qoff, topk,
                kv_pieces):
    far_k = lambda start, width: _key_rows(k_ref, start, width, kv_pieces)
    far_v = lambda start, width: _key_rows(v_ref, start, width, kv_pieces)
    i = pl.program_id(1)
    q0 = qoff + i * qb
    qbr = tile_ref.shape[2] - LANES
    sub = kb // LANES
    w = iw_ref[:, :IDX_HEADS] * (IDX_HEADS ** -0.5 * IDX_DIM ** -0.5)

    def score_keys(kib):
        width = kib.shape[0]
        acc = jnp.zeros((qb, width), F32)
        for h in range(IDX_HEADS):
            d = lax.dot_general(qi_ref[:, h * LANES:(h + 1) * LANES], kib, NT_DIMS,
                                preferred_element_type=F32)
            acc = acc + w[:, h:h + 1] * jnp.maximum(d, 0.0)
        bits = pltpu.bitcast(acc, jnp.int32)
        return bits ^ ((bits >> 31) & 0x7FFFFFFF)

    n_score = (q0 + kb - 1) // kb

    def store_keys(slab0, key):
        for c in range(key.shape[1] // LANES):
            part = key[:, c * LANES:(c + 1) * LANES]
            keys_scr[slab0 + c] = part
            half_scr[slab0 + c] = (part >> 16).astype(jnp.int16)

    def score_body(j, carry):
        store_keys(j * sub, score_keys(_key_rows(ki_ref, pl.multiple_of(j * kb, kb), kb)))
        return carry

    lax.fori_loop(0, n_score, score_body, 0)

    cur_blk = q0 // LANES
    key = score_keys(kic_ref[...])
    store_keys(cur_blk, jnp.where(tile_ref[0, :, LANES:] > 0.5 * NEG_BIG, key, INT_MIN))
    n_blk = cur_blk + qbr // LANES

    rb = min(qb, LANES)
    lo16 = -(2 ** 15)
    n4 = n_blk // 4

    chunks = [pl.ds(c * rb, rb) for c in range(qb // rb)]
    ones = jnp.ones((LANES, LANES), BF16)

    def count_ge(ts):
        tbs = [t.astype(jnp.int16) for t in ts]

        def count(j, cnts):
            return tuple(cnt + jnp.where(half_scr[j, rows, :] >= tb, jnp.int16(1), jnp.int16(0))
                         for cnt, rows, tb in zip(cnts, chunks, tbs))

        def body4(j, cnts):
            for u in range(4):
                cnts = count(4 * j + u, cnts)
            return cnts

        cnts = lax.fori_loop(0, n4, body4, tuple(jnp.zeros((rb, LANES), jnp.int16) for _ in chunks))
        cnts = lax.fori_loop(4 * n4, n_blk, count, cnts)
        return [jnp.dot(cnt.astype(F32).astype(BF16), ones, preferred_element_type=F32) for cnt in cnts]

    def search(needed):
        def bit_body(it, ts):
            cands = [t + (jnp.int32(1) << (15 - it)) for t in ts]
            return tuple(jnp.where(c >= n, cand, t) for c, n, cand, t in zip(count_ge(cands), needed, cands, ts))

        return lax.fori_loop(0, 16, bit_body, tuple(jnp.full((rb, LANES), lo16, jnp.int32) for _ in chunks))

    t_hi = search([float(topk)] * len(chunks))
    top = -lo16 - 1
    above = [jnp.where(t == top, 0.0, c)
             for t, c in zip(t_hi, count_ge([jnp.minimum(t + 1, top) for t in t_hi]))]

    def low_plane(j, carry):
        for rows, tb in zip(chunks, t_hi):
            k32 = keys_scr[j, rows, :]
            low = (k32 & 0xFFFF) + lo16
            half_scr[j, rows, :] = jnp.where((k32 >> 16) == tb, low, lo16).astype(jnp.int16)
        return carry

    lax.fori_loop(0, n_blk, low_plane, 0)
    t_lo = search([topk - a for a in above])
    for rows, th, tl in zip(chunks, t_hi, t_lo):
        thr_scr[rows, :] = jnp.maximum(th * 65536 + (tl - lo16), INT_MIN + 1)

    def count32(pred):
        def count(j, cnts):
            return tuple(cnt + jnp.where(pred(keys_scr[j, rows, :], thr_scr[rows, :]), 1, 0)
                         for cnt, rows in zip(cnts, chunks))

        cnts = lax.fori_loop(0, n_blk, count, tuple(jnp.zeros((rb, LANES), jnp.int32) for _ in chunks))
        return [jnp.dot(cnt.astype(F32).astype(BF16), ones, preferred_element_type=F32) for cnt in cnts]

    surplus = functools.reduce(jnp.maximum, [jnp.max(c) for c in count32(lambda k, t: k >= t)]) - topk

    @pl.when(surplus > 0)
    def _():
        keep = [topk - c for c in count32(lambda k, t: k > t)]
        upto = jnp.where(lax.broadcasted_iota(jnp.int32, (LANES, LANES), 0)
                         <= lax.broadcasted_iota(jnp.int32, (LANES, LANES), 1), 1.0, 0.0).astype(BF16)

        def demote(j, seen):
            out = []
            for rows, kp, sn in zip(chunks, keep, seen):
                k32 = keys_scr[j, rows, :]
                tied = k32 == thr_scr[rows, :]
                tied_bf = jnp.where(tied, 1.0, 0.0).astype(BF16)
                rank = sn + jnp.dot(tied_bf, upto, preferred_element_type=F32)
                keys_scr[j, rows, :] = k32 - jnp.where(tied & (rank > kp), 1, 0)
                out.append(sn + jnp.dot(tied_bf, ones, preferred_element_type=F32))
            return tuple(out)

        lax.fori_loop(0, n_blk, demote, tuple(jnp.zeros((rb, LANES), F32) for _ in chunks))

    hd = A_HEAD_DIM

    def select_mask(blk0, n):
        return [jnp.where(keys_scr[blk0 + c] >= thr_scr[...], 0.0, NEG_BIG) for c in range(n)]

    def logits(h, kblk):
        g = h // A_REP
        s = lax.dot_general(q_ref[:, h * hd:(h + 1) * hd], kblk[:, g * hd:(g + 1) * hd], NT_DIMS,
                            preferred_element_type=F32)
        return [s[:, c * LANES:(c + 1) * LANES] for c in range(s.shape[1] // LANES)]

    def tile_cols(h, col0, n):
        return [tile_ref[h, :, col0 + c * LANES:col0 + (c + 1) * LANES] for c in range(n)]

    mfar_scr[...] = jnp.full(mfar_scr.shape, NEG_BIG, F32)
    mnear_scr[...] = jnp.full(mnear_scr.shape, NEG_BIG, F32)

    def far_mask(blk0, colmask):
        am = select_mask(blk0, sub)
        return am if colmask is None else [a + c for a, c in zip(am, colmask)]

    def max_far(start, blk0, colmask):
        kblk = far_k(start, kb)
        am = far_mask(blk0, colmask)
        for h in range(A_HEADS):
            mfar_scr[h] = functools.reduce(
                jnp.maximum, [s + a for s, a in zip(logits(h, kblk), am)], mfar_scr[h])

    def max_near(kblk, blk0, col0):
        n = kblk.shape[0] // LANES
        am = select_mask(blk0, n)
        for h in range(A_HEADS):
            mnear_scr[h] = functools.reduce(
                jnp.maximum, [s + a + b for s, a, b in zip(logits(h, kblk), am, tile_cols(h, col0, n))],
                mnear_scr[h])

    prev_start = pl.multiple_of(q0 - LANES, LANES)
    _sweep_far(q0, qb, kb, max_far)
    pl.when(q0 > 0)(lambda: max_near(far_k(prev_start, LANES), cur_blk - 1, 0))
    max_near(kc_ref[...], cur_blk, LANES)

    for h in range(A_HEADS):
        fb = tab_ref[FAR_BUCKET, h]
        m = jnp.maximum(jnp.max(mfar_scr[h], axis=-1, keepdims=True) + fb,
                        jnp.max(mnear_scr[h], axis=-1, keepdims=True))
        mfar_scr[h] = jnp.broadcast_to(fb - m, (qb, LANES))
        mnear_scr[h] = jnp.broadcast_to(-m, (qb, LANES))
    l_scr[...] = jnp.zeros(l_scr.shape, F32)
    acc_scr[...] = jnp.zeros(acc_scr.shape, F32)

    def accumulate(h, p, vblk):
        g = h // A_REP
        l_scr[h] += functools.reduce(jnp.add, p)
        acc_scr[h] += jnp.dot(jnp.concatenate(p, axis=1).astype(BF16), vblk[:, g * hd:(g + 1) * hd],
                              preferred_element_type=F32)

    def acc_far(start, blk0, colmask):
        kblk = far_k(start, kb)
        vblk = far_v(start, kb)
        am = far_mask(blk0, colmask)
        for h in range(A_HEADS):
            shift = mfar_scr[h]
            accumulate(h, [jnp.exp(s + a + shift) for s, a in zip(logits(h, kblk), am)], vblk)

    def acc_near(kblk, vblk, blk0, col0):
        n = kblk.shape[0] // LANES
        am = select_mask(blk0, n)
        for h in range(A_HEADS):
            shift = mnear_scr[h]
            accumulate(h, [jnp.exp(s + a + b + shift) for s, a, b in
                           zip(logits(h, kblk), am, tile_cols(h, col0, n))], vblk)

    _sweep_far(q0, qb, kb, acc_far)
    pl.when(q0 > 0)(lambda: acc_near(far_k(prev_start, LANES), far_v(prev_start, LANES),
                                     cur_blk - 1, 0))
    acc_near(kc_ref[...], vc_ref[...], cur_blk, LANES)

    for h in range(A_HEADS):
        o_ref[:, h * hd:(h + 1) * hd] = (
            acc_scr[h] / jnp.sum(l_scr[h], axis=-1, keepdims=True)).astype(o_ref.dtype)


def _dsa_call(rel_bias, tiles, q, qi, iw, far, cur, *, qb, kb, qoff, topk, kv_pieces=None):
    b, t, _ = q.shape
    lf = far[2].shape[1]
    qbr = tiles.shape[2] - LANES
    q0_max = qoff + t - qb
    far_end = -(-q0_max // kb) * kb
    assert t % qb == 0 and qoff % LANES == 0 and far_end <= lf and cur[0].shape[1] == (t // qb) * qbr
    n_slabs = max(far_end, q0_max + qbr) // LANES
    mode = {"pipeline_mode": pl.Buffered(1)} if t // qb > 1 else {}
    far_spec = lambda a: pl.BlockSpec((None,) + a.shape[1:], lambda bi, i: (bi, 0, 0), **mode)
    cur_spec = lambda a: pl.BlockSpec((None, qbr, a.shape[2]), lambda bi, i: (bi, i, 0))
    return pl.pallas_call(
        functools.partial(_dsa_kernel, qb=qb, kb=kb, qoff=qoff, topk=topk, kv_pieces=kv_pieces),
        grid=(b, t // qb),
        in_specs=[pl.BlockSpec(memory_space=pltpu.SMEM),
                  pl.BlockSpec((None, qb, q.shape[2]), lambda bi, i: (bi, i, 0)),
                  pl.BlockSpec((None, qb, qi.shape[2]), lambda bi, i: (bi, i, 0)),
                  pl.BlockSpec((None, qb, iw.shape[2]), lambda bi, i: (bi, i, 0)),
                  far_spec(far[0]), far_spec(far[1]), far_spec(far[2]),
                  cur_spec(cur[0]), cur_spec(cur[1]), cur_spec(cur[2]),
                  _resident(tiles.shape, lambda bi, i: (0, 0, 0))],
        out_specs=pl.BlockSpec((None, qb, A_HEADS * A_HEAD_DIM), lambda bi, i: (bi, i, 0)),
        out_shape=jax.ShapeDtypeStruct((b, t, A_HEADS * A_HEAD_DIM), BF16),
        scratch_shapes=[pltpu.VMEM((n_slabs, qb, LANES), jnp.int32),
                        pltpu.VMEM((n_slabs, qb, LANES), jnp.int16),
                        pltpu.VMEM((qb, LANES), jnp.int32),
                        pltpu.VMEM((A_HEADS, qb, LANES), F32),
                        pltpu.VMEM((A_HEADS, qb, LANES), F32),
                        pltpu.VMEM((A_HEADS, qb, LANES), F32),
                        pltpu.VMEM((A_HEADS, qb, A_HEAD_DIM), F32)],
        compiler_params=_params(2),
        name="mixer_a",
    )(rel_bias, q, qi, iw, *far, *cur, tiles)


def _diff_kernel(tab_ref, lam_ref, sg_ref, q_ref, k_ref, v_ref, kc_ref, vc_ref, tile_ref, o_ref,
                 mfar_scr, mnear_scr, l_scr, acc_scr, *, qb, kb, qoff, lam_init, cache_rows):
    hw = 2 * B_HEAD_DIM
    for hl in range(q_ref.shape[1] // hw):
        _diff_head(hl, pl.program_id(1) * (q_ref.shape[1] // hw) + hl, tab_ref, lam_ref, sg_ref, q_ref, k_ref, v_ref,
                   kc_ref, vc_ref, tile_ref, o_ref, mfar_scr, mnear_scr, l_scr, acc_scr,
                   qb=qb, kb=kb, qoff=qoff, lam_init=lam_init, cache_rows=cache_rows)


def _diff_head(hl, head, tab_ref, lam_ref, sg_ref, q_ref, k_ref, v_ref, kc_ref, vc_ref, tile_ref, o_ref,
               mfar_scr, mnear_scr, l_scr, acc_scr, *, qb, kb, qoff, lam_init, cache_rows):
    i = pl.program_id(2)
    q0 = qoff + i * qb
    hd = B_HEAD_DIM
    hw = 2 * hd
    cols = slice(hl * hw, (hl + 1) * hw)
    far_bias = tab_ref[FAR_BUCKET, A_HEADS + head]
    if cache_rows:
        far_k = lambda start, width: _key_rows(k_ref, start, width, (2 * hl, 2 * hl + 1))
        far_v = lambda start, width: _key_rows(v_ref, start, width, (hl, B_HEADS + hl))
    else:
        far_k = lambda start, width: k_ref[pl.ds(start, width), cols].astype(BF16)
        far_v = lambda start, width: v_ref[pl.ds(start, width), cols].astype(BF16)

    def logits(c, kblk):
        s = lax.dot_general(q_ref[:, hl * hw + c * hd:hl * hw + (c + 1) * hd], kblk[:, c * hd:(c + 1) * hd], NT_DIMS,
                            preferred_element_type=F32)
        return [s[:, u * LANES:(u + 1) * LANES] for u in range(s.shape[1] // LANES)]

    def tile_cols(col0, n):
        return [tile_ref[hl, :, col0 + u * LANES:col0 + (u + 1) * LANES] for u in range(n)]

    mfar_scr[...] = jnp.full(mfar_scr.shape, NEG_BIG, F32)
    mnear_scr[...] = jnp.full(mnear_scr.shape, NEG_BIG, F32)

    def max_far(start, blk0, colmask):
        kblk = far_k(start, kb)
        for c in range(2):
            s = logits(c, kblk)
            if colmask is not None:
                s = [x + m for x, m in zip(s, colmask)]
            mfar_scr[c] = functools.reduce(jnp.maximum, s, mfar_scr[c])

    def max_near(kblk, col0):
        for c in range(2):
            mnear_scr[c] = functools.reduce(
                jnp.maximum, [s + b for s, b in zip(logits(c, kblk), tile_cols(col0, kblk.shape[0] // LANES))],
                mnear_scr[c])

    prev_start = pl.multiple_of(q0 - LANES, LANES)
    _sweep_far(q0, qb, kb, max_far)
    pl.when(q0 > 0)(lambda: max_near(far_k(prev_start, LANES), 0))
    max_near(kc_ref[:, cols], LANES)

    for c in range(2):
        m = jnp.maximum(jnp.max(mfar_scr[c], axis=-1, keepdims=True) + far_bias,
                        jnp.max(mnear_scr[c], axis=-1, keepdims=True))
        mfar_scr[c] = jnp.broadcast_to(far_bias - m, (qb, LANES))
        mnear_scr[c] = jnp.broadcast_to(-m, (qb, LANES))
    l_scr[...] = jnp.zeros(l_scr.shape, F32)
    acc_scr[...] = jnp.zeros(acc_scr.shape, F32)

    def accumulate(c, p, vblk):
        l_scr[c] += functools.reduce(jnp.add, p)
        acc_scr[c] += jnp.dot(jnp.concatenate(p, axis=1).astype(BF16), vblk, preferred_element_type=F32)

    def acc_far(start, blk0, colmask):
        kblk = far_k(start, kb)
        vblk = far_v(start, kb)
        for c in range(2):
            shift = mfar_scr[c]
            shifts = [shift] * (kb // LANES) if colmask is None else [shift + m for m in colmask]
            accumulate(c, [jnp.exp(s + sh) for s, sh in zip(logits(c, kblk), shifts)], vblk)

    def acc_near(kblk, vblk, col0):
        for c in range(2):
            shift = mnear_scr[c]
            accumulate(c, [jnp.exp(s + b + shift) for s, b in
                           zip(logits(c, kblk), tile_cols(col0, kblk.shape[0] // LANES))], vblk)

    _sweep_far(q0, qb, kb, acc_far)
    pl.when(q0 > 0)(lambda: acc_near(far_k(prev_start, LANES), far_v(prev_start, LANES), 0))
    acc_near(kc_ref[:, cols], vc_ref[:, cols], LANES)

    lam = lam_ref[0, 0]
    o = (acc_scr[0] / jnp.sum(l_scr[0], axis=-1, keepdims=True)
         - lam * (acc_scr[1] / jnp.sum(l_scr[1], axis=-1, keepdims=True)))
    o = o * lax.rsqrt(jnp.mean(o * o, axis=-1, keepdims=True) + EPS) * sg_ref[...]
    o_ref[:, cols] = (o * (1.0 - lam_init)).astype(o_ref.dtype)


def _diff_call(rel_bias, tiles, lam, subln_g, q, far, cur, *, qb, kb, qoff, lam_init, cache_rows=False):
    b, t, _ = q.shape
    lf = far[0].shape[1]
    hw = 2 * B_HEAD_DIM
    qbr = tiles.shape[2] - LANES
    q0_max = qoff + t - qb
    far_end = -(-max(q0_max - LANES, 0) // kb) * kb
    assert t % qb == 0 and qoff % LANES == 0 and far_end <= lf and cur[0].shape[1] == (t // qb) * qbr
    if cache_rows:
        hstep, w = 1, B_HEADS * hw
        far_spec = pl.BlockSpec((None, lf, 2 * B_HEADS, LANES), lambda bi, h, i: (bi, 0, 0, 0),
                                pipeline_mode=pl.Buffered(1))
        tile_spec = pl.BlockSpec((B_HEADS, qb, tiles.shape[2]), lambda bi, h, i: (A_HEADS // B_HEADS, 0, 0))
    else:
        hstep, w = B_HEADS, hw
        far_spec = pl.BlockSpec((None, lf, hw), lambda bi, h, i: (bi, 0, h))
        tile_spec = pl.BlockSpec((1, qb, tiles.shape[2]), lambda bi, h, i: (A_HEADS + h, 0, 0))
    cur_spec = pl.BlockSpec((None, qbr, w), lambda bi, h, i: (bi, i, h))
    return pl.pallas_call(
        functools.partial(_diff_kernel, qb=qb, kb=kb, qoff=qoff, lam_init=lam_init, cache_rows=cache_rows),
        grid=(b, hstep, t // qb),
        in_specs=[pl.BlockSpec(memory_space=pltpu.SMEM),
                  pl.BlockSpec(memory_space=pltpu.SMEM),
                  pl.BlockSpec((1, hw), lambda bi, h, i: (0, 0)),
                  pl.BlockSpec((None, qb, w), lambda bi, h, i: (bi, i, h)),
                  far_spec, far_spec, cur_spec, cur_spec, tile_spec],
        out_specs=pl.BlockSpec((None, qb, w), lambda bi, h, i: (bi, i, h)),
        out_shape=jax.ShapeDtypeStruct((b, t, B_HEADS * hw), BF16),
        scratch_shapes=[pltpu.VMEM((2, qb, LANES), F32),
                        pltpu.VMEM((2, qb, LANES), F32),
                        pltpu.VMEM((2, qb, LANES), F32),
                        pltpu.VMEM((2, qb, hw), F32)],
        compiler_params=_params(3),
        name="mixer_b",
    )(rel_bias, lam, subln_g.reshape(1, hw), q, *far, *cur, tiles)


def _lam_kernel(dl_ref, o_ref, *, lam_init):
    dl = dl_ref[...]
    s1 = jnp.sum(dl[0:1] * dl[1:2], axis=-1, keepdims=True)
    s2 = jnp.sum(dl[2:3] * dl[3:4], axis=-1, keepdims=True)
    o_ref[...] = jnp.exp(s1) - jnp.exp(s2) + lam_init


def _lam_call(diff_lam, lam_init):
    return pl.pallas_call(
        functools.partial(_lam_kernel, lam_init=lam_init),
        out_shape=jax.ShapeDtypeStruct((1, 1), F32),
        name="diff_lambda",
    )(diff_lam)


def _route(logits_t, rbias, counts):
    tm = logits_t.shape[1]
    per = N_EXPERTS // N_GROUPS
    scores = jax.nn.sigmoid(logits_t).reshape(N_GROUPS, per, tm)
    ch = scores + rbias.reshape(N_GROUPS, per, 1)
    sub = lax.broadcasted_iota(jnp.int32, ch.shape, 1)
    grp = lax.broadcasted_iota(jnp.int32, ch.shape, 0)
    m1 = jnp.max(ch, axis=1, keepdims=True)
    i1 = jnp.min(jnp.where(ch == m1, sub, per), axis=1, keepdims=True)
    m2 = jnp.max(jnp.where(sub == i1, -jnp.inf, ch), axis=1, keepdims=True)
    gs = jnp.broadcast_to(m1 + m2, ch.shape)
    rank = jnp.zeros(ch.shape, jnp.int32)
    for g2 in range(N_GROUPS):
        o = gs[g2:g2 + 1]
        rank = rank + jnp.where(o > gs, 1, 0) + jnp.where(o == gs, 1, 0) * jnp.where(grp > g2, 1, 0)
    cm = jnp.where(rank < TOPK_GROUPS, ch, -jnp.inf)
    eid = grp * per + sub
    erank = jnp.zeros(cm.shape, jnp.int32)
    for e2 in range(N_EXPERTS):
        o = cm[e2 // per:e2 // per + 1, e2 % per:e2 % per + 1, :]
        erank = erank + jnp.where(o > cm, 1, 0) + jnp.where(o == cm, 1, 0) * jnp.where(eid > e2, 1, 0)
    chosen = erank < TOP_K
    sel = jnp.where(chosen, scores, 0.0)
    denom = jnp.sum(jnp.sum(sel, axis=1, keepdims=True), axis=0, keepdims=True)
    gates = sel / denom * ROUTED_SCALE

    ones = jnp.where(chosen, 1.0, 0.0).reshape(N_EXPERTS, tm)
    earlier = (lax.broadcasted_iota(jnp.int32, (tm, tm), 0) < lax.broadcasted_iota(jnp.int32, (tm, tm), 1))
    before = jnp.dot(ones.astype(BF16), jnp.where(earlier, 1.0, 0.0).astype(BF16), preferred_element_type=F32)
    pos = (before + counts.astype(F32)).reshape(N_GROUPS, per, tm)
    new_counts = counts + jnp.sum(ones, axis=1, keepdims=True).astype(jnp.int32)

    def per_slot(values):
        rows = [jnp.sum(jnp.sum(jnp.where(erank == k, values, 0.0), axis=1, keepdims=True), axis=0)
                for k in range(TOP_K)]
        return jnp.concatenate(rows, axis=0)

    return (per_slot(eid.astype(F32)).astype(jnp.int32), per_slot(pos).astype(jnp.int32), per_slot(gates),
            new_counts)


def _pack_pair(a, b):
    ua = pltpu.bitcast(a.astype(BF16).astype(F32), jnp.uint32)
    ub = pltpu.bitcast(b.astype(BF16).astype(F32), jnp.uint32)
    return pltpu.bitcast(ua | (ub >> 16), jnp.int32)


def _unpack_pair(p):
    u = pltpu.bitcast(p, jnp.uint32)
    a = pltpu.bitcast(u & jnp.uint32(0xFFFF0000), F32)
    b = pltpu.bitcast(u << 16, F32)
    return a.astype(BF16), b.astype(BF16)


PARTS = 4
PART_W = D_MODEL // 2 // PARTS


def _pack_rows(x):
    half = x.shape[1] // 2
    return jnp.stack([_pack_pair(x[:, j * PART_W:(j + 1) * PART_W],
                                 x[:, half + j * PART_W:half + (j + 1) * PART_W]) for j in range(PARTS)])


def _packed_matmul(parts_ref, w_ref):
    half = w_ref.shape[0] // 2
    acc = None
    for j in range(PARTS):
        a, b = _unpack_pair(parts_ref[j])
        t = (jnp.dot(a, w_ref[j * PART_W:(j + 1) * PART_W, :], preferred_element_type=F32)
             + jnp.dot(b, w_ref[half + j * PART_W:half + (j + 1) * PART_W, :], preferred_element_type=F32))
        acc = t if acc is None else acc + t
    return acc


def _outproj_kernel(x_ref, oa_ref, ob_ref, ga_ref, sc_ref, sh_ref, g_ref, w_ref, wr_ref, rb_ref, cin_ref,
                    x1_ref, h2_ref, ek_ref, pk_ref, gk_ref, cout_ref, cnt_scr):
    @pl.when(pl.program_id(0) == 0)
    def _():
        cnt_scr[...] = cin_ref[...]

    half = oa_ref.shape[1]
    mix = (jnp.dot(oa_ref[...], w_ref[:half, :], preferred_element_type=F32)
           + jnp.dot(ob_ref[...], w_ref[half:, :], preferred_element_type=F32))
    x1 = x_ref[...] + ga_ref[...] * mix
    x1_ref[...] = x1
    h2 = _modulated_norm(x1, g_ref[...], sc_ref[...], sh_ref[...])
    h2_ref[...] = _pack_rows(h2)
    logits_t = lax.dot_general(wr_ref[...], h2, NT_DIMS, preferred_element_type=F32,
                               precision=lax.Precision.HIGHEST)
    ek, pk, gk, counts = _route(logits_t, rb_ref[...], cnt_scr[...])
    ek_ref[...] = ek
    pk_ref[...] = pk
    gk_ref[...] = gk
    cnt_scr[...] = counts
    cout_ref[...] = counts


def _outproj_call(x2d, oa, ob, ga, sc, sh, mod_map, g, w_bf, wr_t, rbias, counts_in, tm):
    n, d = x2d.shape
    assert n % tm == 0
    r = sc.shape[1]
    mod_spec = pl.BlockSpec((None, r, d), mod_map)
    slot_spec = pl.BlockSpec((TOP_K, tm), lambda i: (0, i))
    return pl.pallas_call(
        _outproj_kernel,
        grid=(n // tm,),
        in_specs=[pl.BlockSpec((tm, d), lambda i: (i, 0)),
                  pl.BlockSpec((tm, oa.shape[1]), lambda i: (i, 0)),
                  pl.BlockSpec((tm, ob.shape[1]), lambda i: (i, 0)),
                  mod_spec, mod_spec, mod_spec,
                  pl.BlockSpec((1, d), lambda i: (0, 0)),
                  _resident(w_bf.shape, lambda i: (0, 0)),
                  _resident(wr_t.shape, lambda i: (0, 0)),
                  pl.BlockSpec((N_EXPERTS, 1), lambda i: (0, 0)),
                  pl.BlockSpec((N_EXPERTS, 1), lambda i: (0, 0))],
        out_specs=[pl.BlockSpec((tm, d), lambda i: (i, 0)),
                   pl.BlockSpec((PARTS, tm, PART_W), lambda i: (0, i, 0)),
                   slot_spec, slot_spec, slot_spec,
                   pl.BlockSpec((N_EXPERTS, 1), lambda i: (0, 0))],
        out_shape=[jax.ShapeDtypeStruct((n, d), F32),
                   jax.ShapeDtypeStruct((PARTS, n, PART_W), jnp.int32),
                   jax.ShapeDtypeStruct((TOP_K, n), jnp.int32),
                   jax.ShapeDtypeStruct((TOP_K, n), jnp.int32),
                   jax.ShapeDtypeStruct((TOP_K, n), F32),
                   jax.ShapeDtypeStruct((N_EXPERTS, 1), jnp.int32)],
        scratch_shapes=[pltpu.VMEM((N_EXPERTS, 1), jnp.int32)],
        compiler_params=_params(1),
        name="out_proj_router",
    )(x2d, oa, ob, ga, sc, sh, g.reshape(1, d), w_bf, wr_t, rbias, counts_in)


SC_WINDOW = 128
EXPERT_TILE = 512


def _sc_mesh():
    return plsc.VectorSubcoreMesh(core_axis_name="core", subcore_axis_name="subcore")


def _sc_scatter_rows(x, idx, n_out):
    n, w = x.shape
    steps = n // SC_WINDOW
    total = idx.shape[1] // SC_WINDOW

    @pl.kernel(out_type=jax.ShapeDtypeStruct((n_out, w), x.dtype), mesh=_sc_mesh(), scratch_types=[],
               name="moe_dispatch")
    def scatter(x_hbm, i_hbm, o_hbm):
        def body(x_vmem, i_vmem):
            pltpu.sync_copy(x_vmem, o_hbm.at[i_vmem.at[0]])

        pltpu.emit_pipeline(
            body, grid=(total,),
            in_specs=[pl.BlockSpec((SC_WINDOW, w), lambda i: (i % steps, 0)),
                      pl.BlockSpec((1, SC_WINDOW), lambda i: (0, i))],
            out_specs=[], core_axis_name=("core", "subcore"),
            dimension_semantics=(pltpu.PARALLEL,))(x_hbm, i_hbm)

    return scatter(x, idx)


def _sc_gather_rows(y, idx):
    w = y.shape[1]
    n = idx.shape[1]

    @pl.kernel(out_type=jax.ShapeDtypeStruct((n, w), y.dtype), mesh=_sc_mesh(), scratch_types=[],
               name="moe_collect")
    def gather(y_hbm, i_hbm, o_hbm):
        def body(i_vmem, o_vmem):
            pltpu.sync_copy(y_hbm.at[i_vmem.at[0]], o_vmem)

        pltpu.emit_pipeline(
            body, grid=(n // SC_WINDOW,),
            in_specs=[pl.BlockSpec((1, SC_WINDOW), lambda i: (0, i))],
            out_specs=[pl.BlockSpec((SC_WINDOW, w), lambda i: (i, 0))],
            core_axis_name=("core", "subcore"),
            dimension_semantics=(pltpu.PARALLEL,))(i_hbm, o_hbm)

    return gather(y, idx)


def _swiglu(parts_ref, wg_ref, wu_ref, wd_ref):
    a = _packed_matmul(parts_ref, wg_ref)
    u = _packed_matmul(parts_ref, wu_ref)
    hid = (a * jax.nn.sigmoid(a) * u).astype(BF16)
    return jnp.dot(hid, wd_ref[...], preferred_element_type=F32)


def _expert_kernel(te_ref, nu_ref, x_ref, wg_ref, wu_ref, wd_ref, y_ref, wg_scr, wu_scr, wd_scr):
    i = pl.program_id(0)

    @pl.when(i < nu_ref[0])
    def _():
        @pl.when((i == 0) | (te_ref[i] != te_ref[jnp.maximum(i - 1, 0)]))
        def _():
            wg_scr[...] = wg_ref[...].astype(BF16)
            wu_scr[...] = wu_ref[...].astype(BF16)
            wd_scr[...] = wd_ref[...].astype(BF16)

        y_ref[...] = _pack_rows(_swiglu(x_ref, wg_scr, wu_scr, wd_scr))


def _expert_call(tile_expert, n_used, x_sorted, wg, wu, wd):
    _, r, _ = x_sorted.shape
    ne, d, de = wg.shape
    n_tiles = r // EXPERT_TILE
    row_map = lambda i, te, nu: (0, jnp.minimum(i, nu[0] - 1), 0)
    return pl.pallas_call(
        _expert_kernel,
        grid_spec=pltpu.PrefetchScalarGridSpec(
            num_scalar_prefetch=2,
            grid=(n_tiles,),
            in_specs=[pl.BlockSpec((PARTS, EXPERT_TILE, PART_W), row_map),
                      pl.BlockSpec((None, d, de), lambda i, te, nu: (te[i], 0, 0)),
                      pl.BlockSpec((None, d, de), lambda i, te, nu: (te[i], 0, 0)),
                      pl.BlockSpec((None, de, d), lambda i, te, nu: (te[i], 0, 0))],
            out_specs=pl.BlockSpec((PARTS, EXPERT_TILE, PART_W), row_map),
            scratch_shapes=[pltpu.VMEM((d, de), BF16), pltpu.VMEM((d, de), BF16), pltpu.VMEM((de, d), BF16)]),
        out_shape=jax.ShapeDtypeStruct(x_sorted.shape, jnp.int32),
        compiler_params=_params(1),
        name="moe_experts",
    )(tile_expert, n_used, x_sorted, wg, wu, wd)


def _final_kernel(x_ref, h_ref, rows_ref, gk_ref, gf_ref, g_ref, sg_ref, su_ref, sd_ref, o_ref):
    moe = _swiglu(h_ref, sg_ref, su_ref, sd_ref)
    half = moe.shape[1] // 2
    lo = [moe[:, j * PART_W:(j + 1) * PART_W] for j in range(PARTS)]
    hi = [moe[:, half + j * PART_W:half + (j + 1) * PART_W] for j in range(PARTS)]
    for k in range(TOP_K):
        gate = gk_ref[:, k:k + 1]
        for j in range(PARTS):
            a, b = _unpack_pair(rows_ref[k, j])
            lo[j] = lo[j] + gate * a.astype(F32)
            hi[j] = hi[j] + gate * b.astype(F32)
    x = x_ref[...] + gf_ref[...] * jnp.concatenate(lo + hi, axis=1)
    o_ref[...] = x * lax.rsqrt(jnp.mean(x * x, axis=-1, keepdims=True) + EPS) * g_ref[...]


def _final_call(x1, h2p, rows, row_block0, gk_t, gf, mod_map, g, sg, su, sd, tm):
    n, d = x1.shape
    assert n % tm == 0
    r = gf.shape[1]
    return pl.pallas_call(
        _final_kernel,
        grid=(n // tm,),
        in_specs=[pl.BlockSpec((tm, d), lambda i: (i, 0)),
                  pl.BlockSpec((PARTS, tm, PART_W), lambda i: (0, i, 0)),
                  pl.BlockSpec((TOP_K, PARTS, tm, PART_W), lambda i: (0, 0, i + row_block0, 0)),
                  pl.BlockSpec((tm, TOP_K), lambda i: (i, 0)),
                  pl.BlockSpec((None, r, d), mod_map),
                  pl.BlockSpec((1, d), lambda i: (0, 0)),
                  _resident(sg.shape, lambda i: (0, 0)),
                  _resident(su.shape, lambda i: (0, 0)),
                  _resident(sd.shape, lambda i: (0, 0))],
        out_specs=pl.BlockSpec((tm, d), lambda i: (i, 0)),
        out_shape=jax.ShapeDtypeStruct((n, d), F32),
        compiler_params=_params(1),
        name="final_norm",
    )(x1, h2p, rows, gk_t, gf, g.reshape(1, d), sg, su, sd)


_QA, _KA, _VA, _QI, _KI, _IW, _QB, _KB, _VB = range(9)
_IN_SIZES = (A_HEADS * A_HEAD_DIM, A_KV_HEADS * A_HEAD_DIM, A_KV_HEADS * A_HEAD_DIM,
             IDX_HEADS * IDX_DIM, IDX_DIM, IDX_HEADS,
             B_HEADS * 2 * B_HEAD_DIM, B_HEADS * 2 * B_HEAD_DIM, B_HEADS * 2 * B_HEAD_DIM)


def _split_w_in(w_in):
    d = w_in.shape[0]
    offs = np.concatenate([[0], np.cumsum(_IN_SIZES)])
    cols = [w_in[:, offs[k]:offs[k + 1]].astype(BF16) for k in range(9)]
    zero = lambda n: jnp.zeros((d, n), BF16)
    qi = jnp.concatenate([cols[_QI].reshape(d, IDX_HEADS, IDX_DIM),
                          jnp.zeros((d, IDX_HEADS, LANES - IDX_DIM), BF16)], axis=-1).reshape(d, IDX_HEADS * LANES)
    w_a = jnp.concatenate([cols[_QA], cols[_KA], cols[_VA], qi,
                           cols[_KI], zero(LANES - IDX_DIM), cols[_IW], zero(LANES - IDX_HEADS)], axis=1)
    w_b = jnp.concatenate([cols[_QB], cols[_KB], cols[_VB]], axis=1)
    return w_a, w_b


def _group_forward(x, mod, past, rel_bias, lam, weights, *, lam_init, qb, qb_b, kb, tm, per_row_mod):
    (norm_a_g, w_a, w_b, subln_g) = weights
    b, t, d = x.shape
    n = b * t
    x2d = x.reshape(n, d)
    sh_a, sc_a = mod[0], mod[1]
    if per_row_mod:
        expand = lambda m: jnp.repeat(m, t, axis=0).reshape(1, n, d)
        mod_map = lambda i: (0, i, 0)
    else:
        expand = lambda m: m.reshape(b, 1, d)
        mod_map = lambda i: ((i * tm) // t, 0, 0)
    sc3, sh3 = expand(sc_a), expand(sh_a)

    na = A_HEADS * A_HEAD_DIM
    nk = A_KV_HEADS * A_HEAD_DIM
    nq = IDX_HEADS * LANES
    c_ka, c_va, c_qi = na, na + nk, na + 2 * nk
    c_ki, c_iw = c_qi + nq, c_qi + nq + LANES
    kv_order = tuple(range(A_KV_HEADS))
    defs_a = [(0, na, na, A_HEAD_DIM ** -0.5, None), (c_ka, nk, nk, None, kv_order), (c_ka, nk, nk, None, None),
              (c_va, nk, nk, None, kv_order), (c_va, nk, nk, None, None), (c_qi, nq, nq, None, None),
              (c_ki, LANES, IDX_DIM, None, None), (c_ki, LANES, LANES, None, None), (c_iw, LANES, LANES, None, None)]
    dt_a = [BF16, F32, BF16, F32, BF16, BF16, F32, BF16, F32]
    qa, ka, ka_bf, va, va_bf, qi, ki, ki_bf, iw = _norm_proj_call(
        x2d, sc3, sh3, mod_map, norm_a_g, w_a, defs_a, dt_a, tm, "in_proj_a")
    nb = B_HEADS * 2 * B_HEAD_DIM
    kb_order = tuple(range(2 * B_HEADS))
    vb_order = tuple(2 * h + half for half in range(2) for h in range(B_HEADS))
    defs_b = [(0, nb, nb, B_HEAD_DIM ** -0.5, None), (nb, nb, nb, None, kb_order), (nb, nb, nb, None, None),
              (2 * nb, nb, nb, None, vb_order), (2 * nb, nb, nb, None, None)]
    dt_b = [BF16, F32, BF16, F32, BF16]
    qbm, kbm, kb_bf, vbm, vb_bf = _norm_proj_call(
        x2d, sc3, sh3, mod_map, norm_a_g, w_b, defs_b, dt_b, tm, "in_proj_b")

    r3 = lambda a: a.reshape(b, t, a.shape[-1])
    cur = [r3(ka_bf), r3(va_bf), r3(ki_bf), r3(kb_bf), r3(vb_bf)]
    kv_pieces, cache_rows = None, False
    if past is None:
        qoff = 0
        far = cur
    else:
        assert t == qb, "a group with cached keys is one query block per batch row"
        qoff = past[0].shape[1]
        qbr = -(-qb // LANES) * LANES
        cur = [jnp.concatenate([a, jnp.zeros((b, qbr - t, a.shape[-1]), BF16)], axis=1) for a in cur]
        pk, pv, pi, pbk, pbv = past
        if qoff % kb == 0:
            kv_pieces, cache_rows = (A_KV_HEADS, tuple(range(A_KV_HEADS))), True
            rows_b = 2 * B_HEADS
            far = [pk.reshape(b, qoff * A_KV_HEADS, A_HEAD_DIM), pv.reshape(b, qoff * A_KV_HEADS, A_HEAD_DIM),
                   pi.reshape(b, qoff, -1), pbk.reshape(b, qoff, rows_b, B_HEAD_DIM),
                   pbv.reshape(b, qoff, B_HEADS, 2, B_HEAD_DIM).transpose(0, 1, 3, 2, 4)
                      .reshape(b, qoff, rows_b, B_HEAD_DIM)]
        else:
            far = [jnp.concatenate([p.reshape(b, qoff, -1), jnp.zeros((b, -qoff % kb, p[0, 0].size), p.dtype)], axis=1)
                   for p in past]
    length = qoff + t
    topk = min(TOPK_MAX, length // 4)

    tiles = _bias_tile_call(rel_bias, qb)
    tiles_b = tiles if qb_b == qb else _bias_tile_call(rel_bias, qb_b)
    out_a = _dsa_call(rel_bias, tiles, r3(qa), r3(qi), r3(iw), far[:3], cur[:3], qb=qb, kb=kb, qoff=qoff, topk=topk,
                      kv_pieces=kv_pieces)
    out_b = _diff_call(rel_bias, tiles_b, lam, subln_g, r3(qbm), far[3:], cur[3:], qb=qb_b, kb=kb, qoff=qoff,
                       lam_init=lam_init, cache_rows=cache_rows)
    rows = (ka.reshape(1, b, t, A_KV_HEADS, A_HEAD_DIM), va.reshape(1, b, t, A_KV_HEADS, A_HEAD_DIM),
            ki.reshape(1, b, t, IDX_DIM), kbm.reshape(1, b, t, B_HEADS, 2, B_HEAD_DIM),
            vbm.reshape(b, t, 2, B_HEADS, B_HEAD_DIM).transpose(0, 1, 3, 2, 4)
               .reshape(1, b, t, B_HEADS, 2 * B_HEAD_DIM))
    return x2d, out_a.reshape(n, -1), out_b.reshape(n, -1), rows, expand, mod_map


def kernel(x_prompt, x_sample, c_prompt, c_sample, cache_a_k, cache_a_v, cache_a_kidx, cache_b_k, cache_b_v,
           rel_bias, w_ada, b_ada, norm_a_g, w_in, w_out, diff_lam, subln_g, norm_f_g, w_router, router_bias,
           w_gate, w_up, w_down, ws_gate, ws_up, ws_down, final_g):
    assert w_ada.shape[0] == 1, "single-layer model"
    d = x_prompt.shape[-1]
    bp, tp, _ = x_prompt.shape
    bs, ts, _ = x_sample.shape
    lam_init = 0.8 - 0.6 * math.exp(-0.3 * 0)

    c_all = jnp.concatenate([c_prompt, c_sample], axis=0)
    rows = -(-c_all.shape[0] // 8) * 8
    c_pad = jnp.concatenate([c_all, jnp.zeros((rows - c_all.shape[0], d), F32)], axis=0)
    mod_all = _ada_call(c_pad, w_ada[0], b_ada[0])
    mod_p = [mod_all[:bp, k * d:(k + 1) * d] for k in range(6)]
    mod_s = [mod_all[bp:bp + bs, k * d:(k + 1) * d] for k in range(6)]

    w_a, w_b = _split_w_in(w_in[0])
    lam = _lam_call(diff_lam[0], lam_init)
    weights = (norm_a_g[0], w_a, w_b, subln_g[0])
    past = (cache_a_k[0], cache_a_v[0], cache_a_kidx[0], cache_b_k[0], cache_b_v[0])

    tm_p = 256
    tm_s = bs * ts
    xp2, oa_p, ob_p, rows_p, exp_p, map_p = _group_forward(
        x_prompt, mod_p, None, rel_bias, lam, weights, lam_init=lam_init, qb=256, qb_b=512, kb=512, tm=tm_p,
        per_row_mod=False)
    xs2, oa_s, ob_s, rows_s, exp_s, map_s = _group_forward(
        x_sample, mod_s, past, rel_bias, lam, weights, lam_init=lam_init, qb=ts, qb_b=ts, kb=512, tm=tm_s,
        per_row_mod=True)

    w_out_bf = w_out[0].astype(BF16)
    wr_t = w_router[0].T
    rb = router_bias[0].reshape(N_EXPERTS, 1)
    no_rows = jnp.zeros((N_EXPERTS, 1), jnp.int32)
    x1_p, h2_p, ek_p, pk_p, gk_p, cnt_p = _outproj_call(
        xp2, oa_p, ob_p, exp_p(mod_p[2]), exp_p(mod_p[4]), exp_p(mod_p[3]), map_p, norm_f_g[0], w_out_bf, wr_t, rb,
        no_rows, tm_p)
    x1_s, h2_s, ek_s, pk_s, gk_s, cnt_all = _outproj_call(
        xs2, oa_s, ob_s, exp_s(mod_s[2]), exp_s(mod_s[4]), exp_s(mod_s[3]), map_s, norm_f_g[0], w_out_bf, wr_t, rb,
        cnt_p, tm_s)

    n_p, n_s = bp * tp, bs * ts
    n_all = n_p + n_s
    n_rows = -(-(n_all * TOP_K) // EXPERT_TILE) * EXPERT_TILE + N_EXPERTS * EXPERT_TILE
    counts = cnt_all[:, 0]
    padded = (counts + EXPERT_TILE - 1) // EXPERT_TILE * EXPERT_TILE
    ends = jnp.cumsum(padded)
    starts = ends - padded
    tile_first_row = jnp.arange(n_rows // EXPERT_TILE, dtype=jnp.int32) * EXPERT_TILE
    tile_expert = jnp.minimum(jnp.sum(ends[None, :] <= tile_first_row[:, None], axis=1), N_EXPERTS - 1).astype(jnp.int32)
    n_used = (ends[-1:] // EXPERT_TILE).astype(jnp.int32)
    ek = jnp.concatenate([ek_p, ek_s], axis=1)
    expert_ids = jnp.arange(N_EXPERTS, dtype=jnp.int32)[:, None, None]
    dest = jnp.sum(jnp.where(ek[None] == expert_ids, starts[:, None, None], 0), axis=0) \
        + jnp.concatenate([pk_p, pk_s], axis=1)
    piece = jnp.arange(PARTS, dtype=jnp.int32)[None, :, None]
    idx = (dest[:, None, :] + piece * n_rows).reshape(1, TOP_K * PARTS * n_all).astype(jnp.int32)

    h2_all = jnp.concatenate([h2_p, h2_s], axis=1).reshape(PARTS * n_all, PART_W)
    x_sorted = _sc_scatter_rows(h2_all, idx, PARTS * n_rows).reshape(PARTS, n_rows, PART_W)
    y_sorted = _expert_call(tile_expert, n_used, x_sorted, w_gate[0], w_up[0], w_down[0])
    rows = _sc_gather_rows(y_sorted.reshape(PARTS * n_rows, PART_W), idx).reshape(TOP_K, PARTS, n_all, PART_W)

    shared = (ws_gate[0].astype(BF16), ws_up[0].astype(BF16), ws_down[0].astype(BF16))
    assert n_p % tm_s == 0
    y_p = _final_call(x1_p, h2_p, rows, 0, gk_p.T, exp_p(mod_p[5]), map_p, final_g, *shared, tm_p)
    y_s = _final_call(x1_s, h2_s, rows, n_p // tm_s, gk_s.T, exp_s(mod_s[5]), map_s, final_g, *shared, tm_s)
    return (y_p.reshape(bp, tp, d), y_s.reshape(bs, ts, d)) + rows_p + rows_s
```

```python
import functools
import math

import numpy as np
import jax
import jax.numpy as jnp
from jax import lax
from jax.experimental import pallas as pl
from jax.experimental.pallas import tpu as pltpu
from jax.experimental.pallas import tpu_sc as plsc

D_MODEL = 2048
CHUNK = 64
A_HEADS = 8
A_KV_HEADS = 2
A_REP = A_HEADS // A_KV_HEADS
A_HEAD_DIM = 128
IDX_HEADS = 16
IDX_DIM = 64
TOPK_MAX = 256
B_HEADS = 4
B_HEAD_DIM = 128
N_REL_BUCKETS = 32
REL_MAX_DIST = 128
N_ATTN_HEADS = A_HEADS + B_HEADS
N_EXPERTS = 64
TOP_K = 8
N_GROUPS = 8
TOPK_GROUPS = 4
EXPERT_DIM = 512
SHARED_DIM = 512
ROUTED_SCALE = 2.5
EPS = 1e-6

LANES = 128
INT_MIN = -(2 ** 31)
NEG_BIG = -1e30
VMEM_LIMIT = 52 * 1024 * 1024

F32 = jnp.float32
BF16 = jnp.bfloat16
NT_DIMS = (((1,), (1,)), ((), ()))


def _params(n_axes):
    return pltpu.CompilerParams(dimension_semantics=("arbitrary",) * n_axes,
                                vmem_limit_bytes=VMEM_LIMIT)


def _resident(shape, index_map):
    return pl.BlockSpec(shape, index_map, pipeline_mode=pl.Buffered(1))


def _bucket_thresholds():
    nb = N_REL_BUCKETS // 2
    max_exact = nb // 2
    n = np.arange(max_exact, 4 * REL_MAX_DIST, dtype=np.int32)
    nf = n.astype(np.float32)
    large = max_exact + (np.log(nf / np.float32(max_exact)) / np.float32(math.log(REL_MAX_DIST / max_exact))
                         * np.float32(nb - max_exact)).astype(np.int32)
    large = np.minimum(large, nb - 1)
    assert np.all(np.diff(large) >= 0)
    ths = [int(n[np.argmax(large >= b)]) for b in range(max_exact + 1, nb)]
    return max_exact, nb, ths


_MAX_EXACT, _NB, _BUCKET_THS = _bucket_thresholds()
assert _BUCKET_THS[-1] <= LANES + 1
FAR_BUCKET = _NB - 1


def _ada_kernel(c_ref, w_ref, b_ref, o_ref):
    c = c_ref[...]
    s = c * jax.nn.sigmoid(c)
    o_ref[...] = jnp.dot(s, w_ref[...], preferred_element_type=F32,
                         precision=lax.Precision.HIGHEST) + b_ref[...]


def _ada_call(c, w_ada, b_ada):
    rows, d = c.shape
    n = w_ada.shape[1]
    tn = 1536
    assert n % tn == 0
    return pl.pallas_call(
        _ada_kernel,
        grid=(n // tn,),
        in_specs=[pl.BlockSpec((rows, d), lambda j: (0, 0)),
                  pl.BlockSpec((d, tn), lambda j: (0, j)),
                  pl.BlockSpec((1, tn), lambda j: (0, j))],
        out_specs=pl.BlockSpec((rows, tn), lambda j: (0, j)),
        out_shape=jax.ShapeDtypeStruct((rows, n), F32),
        compiler_params=_params(1),
        name="ada_mod",
    )(c, w_ada, b_ada.reshape(1, n))


def _modulated_norm(x, g, sc, sh):
    y = x * lax.rsqrt(jnp.mean(x * x, axis=-1, keepdims=True) + EPS) * g
    return y * (1.0 + sc) + sh


def _norm_proj_kernel(x_ref, sc_ref, sh_ref, g_ref, w_ref, *out_refs, defs):
    h = _modulated_norm(x_ref[...], g_ref[...], sc_ref[...], sh_ref[...]).astype(BF16)
    done = {}
    tm = x_ref.shape[0]
    for (start, width, store_w, mult, order), o_ref in zip(defs, out_refs):
        if (start, width) not in done:
            done[(start, width)] = jnp.dot(h, w_ref[:, start:start + width], preferred_element_type=F32)
        r = done[(start, width)][:, :store_w]
        if order is None:
            o_ref[...] = (r if mult is None else r * mult).astype(o_ref.dtype)
        else:
            for p, cb in enumerate(order):
                o_ref[pl.ds(p, tm, stride=len(order)), :] = r[:, cb * LANES:(cb + 1) * LANES].astype(o_ref.dtype)


def _norm_proj_call(x2d, sc, sh, mod_map, g, w_bf, defs, dtypes, tm, name):
    n, d = x2d.shape
    assert n % tm == 0
    r = sc.shape[1]
    shape = lambda rows, d_: (rows, d_[2]) if d_[4] is None else (rows * len(d_[4]), LANES)
    out_shapes = [jax.ShapeDtypeStruct(shape(n, d_), dt) for d_, dt in zip(defs, dtypes)]
    out_specs = [pl.BlockSpec(shape(tm, d_), lambda i: (i, 0)) for d_ in defs]
    return pl.pallas_call(
        functools.partial(_norm_proj_kernel, defs=tuple(defs)),
        grid=(n // tm,),
        in_specs=[pl.BlockSpec((tm, d), lambda i: (i, 0)),
                  pl.BlockSpec((None, r, d), mod_map),
                  pl.BlockSpec((None, r, d), mod_map),
                  pl.BlockSpec((1, d), lambda i: (0, 0)),
                  _resident(w_bf.shape, lambda i: (0, 0))],
        out_specs=out_specs,
        out_shape=out_shapes,
        compiler_params=_params(1),
        name=name,
    )(x2d, sc, sh, g.reshape(1, d), w_bf)


def _bias_tile_kernel(tab_ref, o_ref, *, qb):
    _, rows, width = o_ref.shape
    r = lax.broadcasted_iota(jnp.int32, (rows, width), 0)
    koff = lax.broadcasted_iota(jnp.int32, (rows, width), 1) - LANES
    rel = koff - r
    n = jnp.abs(rel)
    large = jnp.full((rows, width), _MAX_EXACT, jnp.int32)
    for th in _BUCKET_THS:
        large = large + jnp.where(n >= th, 1, 0)
    bucket = jnp.where(rel > 0, _NB, 0) + jnp.where(n < _MAX_EXACT, n, large)
    visible = ((koff >> 6) <= (r >> 6)) & (koff < qb)
    for h in range(N_ATTN_HEADS):
        b = jnp.zeros((rows, width), F32)
        for bk in range(N_REL_BUCKETS):
            b = jnp.where(bucket == bk, tab_ref[bk, h], b)
        o_ref[h] = jnp.where(visible, b, NEG_BIG)


def _bias_tile_call(rel_bias, qb):
    qbr = -(-qb // LANES) * LANES
    return pl.pallas_call(
        functools.partial(_bias_tile_kernel, qb=qb),
        in_specs=[pl.BlockSpec(memory_space=pltpu.SMEM)],
        out_specs=pl.BlockSpec(memory_space=pltpu.VMEM),
        out_shape=jax.ShapeDtypeStruct((N_ATTN_HEADS, qb, LANES + qbr), F32),
        name="bias_tiles",
    )(rel_bias)


def _sweep_far(q0, qb, kb, fn):
    far_end = jnp.maximum(q0 - LANES, 0)
    n_full = far_end // kb
    sub = kb // LANES

    def whole(j):
        fn(pl.multiple_of(j * kb, kb), j * sub, None)

    def pair(j, carry):
        whole(2 * j)
        whole(2 * j + 1)
        return carry

    lax.fori_loop(0, n_full // 2, pair, 0)
    pl.when(n_full % 2 == 1)(lambda: whole(n_full - 1))

    @pl.when(far_end > n_full * kb)
    def _():
        start = pl.multiple_of(n_full * kb, kb)
        col = start + lax.broadcasted_iota(jnp.int32, (qb, LANES), 1)
        fn(start, n_full * sub, [jnp.where(col + c * LANES < far_end, 0.0, NEG_BIG) for c in range(sub)])


def _key_rows(ref, start, width, pieces=None):
    if pieces is None:
        blk = ref[pl.ds(start, width), :].astype(BF16)
        pad = -blk.shape[1] % LANES
        return blk if pad == 0 else jnp.concatenate([blk, jnp.zeros((width, pad), BF16)], axis=1)
    r, rows = pieces
    window = ref.at[pl.ds(pl.multiple_of(start * r, LANES * r), width * r), :]
    return jnp.concatenate([window[pl.ds(p, width, stride=r), :].astype(BF16) for p in rows], axis=1)


def _dsa_kernel(tab_ref, q_ref, qi_ref, iw_ref, k_ref, v_ref, ki_ref, kc_ref, vc_ref, kic_ref, tile_ref, o_ref,
                keys_scr, half_scr, thr_scr, mfar_scr, mnear_scr, l_scr, acc_scr, *, qb, kb, qoff, topk,
                kv_pieces):
    far_k = lambda start, width: _key_rows(k_ref, start, width, kv_pieces)
    far_v = lambda start, width: _key_rows(v_ref, start, width, kv_pieces)
    i = pl.program_id(1)
    q0 = qoff + i * qb
    qbr = tile_ref.shape[2] - LANES
    sub = kb // LANES
    w = iw_ref[:, :IDX_HEADS] * (IDX_HEADS ** -0.5 * IDX_DIM ** -0.5)

    def score_keys(kib):
        width = kib.shape[0]
        acc = jnp.zeros((qb, width), F32)
        for h in range(IDX_HEADS):
            d = lax.dot_general(qi_ref[:, h * LANES:(h + 1) * LANES], kib, NT_DIMS,
                                preferred_element_type=F32)
            acc = acc + w[:, h:h + 1] * jnp.maximum(d, 0.0)
        bits = pltpu.bitcast(acc, jnp.int32)
        return bits ^ ((bits >> 31) & 0x7FFFFFFF)

    n_score = (q0 + kb - 1) // kb

    def store_keys(slab0, key):
        for c in range(key.shape[1] // LANES):
            part = key[:, c * LANES:(c + 1) * LANES]
            keys_scr[slab0 + c] = part
            half_scr[slab0 + c] = (part >> 16).astype(jnp.int16)

    def score_body(j, carry):
        store_keys(j * sub, score_keys(_key_rows(ki_ref, pl.multiple_of(j * kb, kb), kb)))
        return carry

    lax.fori_loop(0, n_score, score_body, 0)

    cur_blk = q0 // LANES
    key = score_keys(kic_ref[...])
    store_keys(cur_blk, jnp.where(tile_ref[0, :, LANES:] > 0.5 * NEG_BIG, key, INT_MIN))
    n_blk = cur_blk + qbr // LANES

    rb = min(qb, LANES)
    lo16 = -(2 ** 15)
    n4 = n_blk // 4

    chunks = [pl.ds(c * rb, rb) for c in range(qb // rb)]
    ones = jnp.ones((LANES, LANES), BF16)

    def count_ge(ts):
        tbs = [t.astype(jnp.int16) for t in ts]

        def count(j, cnts):
            return tuple(cnt + jnp.where(half_scr[j, rows, :] >= tb, jnp.int16(1), jnp.int16(0))
                         for cnt, rows, tb in zip(cnts, chunks, tbs))

        def body4(j, cnts):
            for u in range(4):
                cnts = count(4 * j + u, cnts)
            return cnts

        cnts = lax.fori_loop(0, n4, body4, tuple(jnp.zeros((rb, LANES), jnp.int16) for _ in chunks))
        cnts = lax.fori_loop(4 * n4, n_blk, count, cnts)
        return [jnp.dot(cnt.astype(F32).astype(BF16), ones, preferred_element_type=F32) for cnt in cnts]

    def search(needed):
        def bit_body(it, ts):
            cands = [t + (jnp.int32(1) << (15 - it)) for t in ts]
            return tuple(jnp.where(c >= n, cand, t) for c, n, cand, t in zip(count_ge(cands), needed, cands, ts))

        return lax.fori_loop(0, 16, bit_body, tuple(jnp.full((rb, LANES), lo16, jnp.int32) for _ in chunks))

    t_hi = search([float(topk)] * len(chunks))
    top = -lo16 - 1
    above = [jnp.where(t == top, 0.0, c)
             for t, c in zip(t_hi, count_ge([jnp.minimum(t + 1, top) for t in t_hi]))]

    def low_plane(j, carry):
        for rows, tb in zip(chunks, t_hi):
            k32 = keys_scr[j, rows, :]
            low = (k32 & 0xFFFF) + lo16
            half_scr[j, rows, :] = jnp.where((k32 >> 16) == tb, low, lo16).astype(jnp.int16)
        return carry

    lax.fori_loop(0, n_blk, low_plane, 0)
    t_lo = search([topk - a for a in above])
    for rows, th, tl in zip(chunks, t_hi, t_lo):
        thr_scr[rows, :] = jnp.maximum(th * 65536 + (tl - lo16), INT_MIN + 1)

    def count32(pred):
        def count(j, cnts):
            return tuple(cnt + jnp.where(pred(keys_scr[j, rows, :], thr_scr[rows, :]), 1, 0)
                         for cnt, rows in zip(cnts, chunks))

        cnts = lax.fori_loop(0, n_blk, count, tuple(jnp.zeros((rb, LANES), jnp.int32) for _ in chunks))
        return [jnp.dot(cnt.astype(F32).astype(BF16), ones, preferred_element_type=F32) for cnt in cnts]

    surplus = functools.reduce(jnp.maximum, [jnp.max(c) for c in count32(lambda k, t: k >= t)]) - topk

    @pl.when(surplus > 0)
    def _():
        keep = [topk - c for c in count32(lambda k, t: k > t)]
        upto = jnp.where(lax.broadcasted_iota(jnp.int32, (LANES, LANES), 0)
                         <= lax.broadcasted_iota(jnp.int32, (LANES, LANES), 1), 1.0, 0.0).astype(BF16)

        def demote(j, seen):
            out = []
            for rows, kp, sn in zip(chunks, keep, seen):
                k32 = keys_scr[j, rows, :]
                tied = k32 == thr_scr[rows, :]
                tied_bf = jnp.where(tied, 1.0, 0.0).astype(BF16)
                rank = sn + jnp.dot(tied_bf, upto, preferred_element_type=F32)
                keys_scr[j, rows, :] = k32 - jnp.where(tied & (rank > kp), 1, 0)
                out.append(sn + jnp.dot(tied_bf, ones, preferred_element_type=F32))
            return tuple(out)

        lax.fori_loop(0, n_blk, demote, tuple(jnp.zeros((rb, LANES), F32) for _ in chunks))

    hd = A_HEAD_DIM

    def select_mask(blk0, n):
        return [jnp.where(keys_scr[blk0 + c] >= thr_scr[...], 0.0, NEG_BIG) for c in range(n)]

    def logits(h, kblk):
        g = h // A_REP
        s = lax.dot_general(q_ref[:, h * hd:(h + 1) * hd], kblk[:, g * hd:(g + 1) * hd], NT_DIMS,
                            preferred_element_type=F32)
        return [s[:, c * LANES:(c + 1) * LANES] for c in range(s.shape[1] // LANES)]

    def tile_cols(h, col0, n):
        return [tile_ref[h, :, col0 + c * LANES:col0 + (c + 1) * LANES] for c in range(n)]

    mfar_scr[...] = jnp.full(mfar_scr.shape, NEG_BIG, F32)
    mnear_scr[...] = jnp.full(mnear_scr.shape, NEG_BIG, F32)

    def far_mask(blk0, colmask):
        am = select_mask(blk0, sub)
        return am if colmask is None else [a + c for a, c in zip(am, colmask)]

    def max_far(start, blk0, colmask):
        kblk = far_k(start, kb)
        am = far_mask(blk0, colmask)
        for h in range(A_HEADS):
            mfar_scr[h] = functools.reduce(
                jnp.maximum, [s + a for s, a in zip(logits(h, kblk), am)], mfar_scr[h])

    def max_near(kblk, blk0, col0):
        n = kblk.shape[0] // LANES
        am = select_mask(blk0, n)
        for h in range(A_HEADS):
            mnear_scr[h] = functools.reduce(
                jnp.maximum, [s + a + b for s, a, b in zip(logits(h, kblk), am, tile_cols(h, col0, n))],
                mnear_scr[h])

    prev_start = pl.multiple_of(q0 - LANES, LANES)
    _sweep_far(q0, qb, kb, max_far)
    pl.when(q0 > 0)(lambda: max_near(far_k(prev_start, LANES), cur_blk - 1, 0))
    max_near(kc_ref[...], cur_blk, LANES)

    for h in range(A_HEADS):
        fb = tab_ref[FAR_BUCKET, h]
        m = jnp.maximum(jnp.max(mfar_scr[h], axis=-1, keepdims=True) + fb,
                        jnp.max(mnear_scr[h], axis=-1, keepdims=True))
        mfar_scr[h] = jnp.broadcast_to(fb - m, (qb, LANES))
        mnear_scr[h] = jnp.broadcast_to(-m, (qb, LANES))
    l_scr[...] = jnp.zeros(l_scr.shape, F32)
    acc_scr[...] = jnp.zeros(acc_scr.shape, F32)

    def accumulate(h, p, vblk):
        g = h // A_REP
        l_scr[h] += functools.reduce(jnp.add, p)
        acc_scr[h] += jnp.dot(jnp.concatenate(p, axis=1).astype(BF16), vblk[:, g * hd:(g + 1) * hd],
                              preferred_element_type=F32)

    def acc_far(start, blk0, colmask):
        kblk = far_k(start, kb)
        vblk = far_v(start, kb)
        am = far_mask(blk0, colmask)
        for h in range(A_HEADS):
            shift = mfar_scr[h]
            accumulate(h, [jnp.exp(s + a + shift) for s, a in zip(logits(h, kblk), am)], vblk)

    def acc_near(kblk, vblk, blk0, col0):
        n = kblk.shape[0] // LANES
        am = select_mask(blk0, n)
        for h in range(A_HEADS):
            shift = mnear_scr[h]
            accumulate(h, [jnp.exp(s + a + b + shift) for s, a, b in
                           zip(logits(h, kblk), am, tile_cols(h, col0, n))], vblk)

    _sweep_far(q0, qb, kb, acc_far)
    pl.when(q0 > 0)(lambda: acc_near(far_k(prev_start, LANES), far_v(prev_start, LANES),
                                     cur_blk - 1, 0))
    acc_near(kc_ref[...], vc_ref[...], cur_blk, LANES)

    for h in range(A_HEADS):
        o_ref[:, h * hd:(h + 1) * hd] = (
            acc_scr[h] / jnp.sum(l_scr[h], axis=-1, keepdims=True)).astype(o_ref.dtype)


def _dsa_call(rel_bias, tiles, q, qi, iw, far, cur, *, qb, kb, qoff, topk, kv_pieces=None):
    b, t, _ = q.shape
    lf = far[2].shape[1]
    qbr = tiles.shape[2] - LANES
    q0_max = qoff + t - qb
    far_end = -(-q0_max // kb) * kb
    assert t % qb == 0 and qoff % LANES == 0 and far_end <= lf and cur[0].shape[1] == (t // qb) * qbr
    n_slabs = max(far_end, q0_max + qbr) // LANES
    mode = {"pipeline_mode": pl.Buffered(1)} if t // qb > 1 else {}
    far_spec = lambda a: pl.BlockSpec((None,) + a.shape[1:], lambda bi, i: (bi, 0, 0), **mode)
    cur_spec = lambda a: pl.BlockSpec((None, qbr, a.shape[2]), lambda bi, i: (bi, i, 0))
    return pl.pallas_call(
        functools.partial(_dsa_kernel, qb=qb, kb=kb, qoff=qoff, topk=topk, kv_pieces=kv_pieces),
        grid=(b, t // qb),
        in_specs=[pl.BlockSpec(memory_space=pltpu.SMEM),
                  pl.BlockSpec((None, qb, q.shape[2]), lambda bi, i: (bi, i, 0)),
                  pl.BlockSpec((None, qb, qi.shape[2]), lambda bi, i: (bi, i, 0)),
                  pl.BlockSpec((None, qb, iw.shape[2]), lambda bi, i: (bi, i, 0)),
                  far_spec(far[0]), far_spec(far[1]), far_spec(far[2]),
                  cur_spec(cur[0]), cur_spec(cur[1]), cur_spec(cur[2]),
                  _resident(tiles.shape, lambda bi, i: (0, 0, 0))],
        out_specs=pl.BlockSpec((None, qb, A_HEADS * A_HEAD_DIM), lambda bi, i: (bi, i, 0)),
        out_shape=jax.ShapeDtypeStruct((b, t, A_HEADS * A_HEAD_DIM), BF16),
        scratch_shapes=[pltpu.VMEM((n_slabs, qb, LANES), jnp.int32),
                        pltpu.VMEM((n_slabs, qb, LANES), jnp.int16),
                        pltpu.VMEM((qb, LANES), jnp.int32),
                        pltpu.VMEM((A_HEADS, qb, LANES), F32),
                        pltpu.VMEM((A_HEADS, qb, LANES), F32),
                        pltpu.VMEM((A_HEADS, qb, LANES), F32),
                        pltpu.VMEM((A_HEADS, qb, A_HEAD_DIM), F32)],
        compiler_params=_params(2),
        name="mixer_a",
    )(rel_bias, q, qi, iw, *far, *cur, tiles)


def _diff_kernel(tab_ref, lam_ref, sg_ref, q_ref, k_ref, v_ref, kc_ref, vc_ref, tile_ref, o_ref,
                 mfar_scr, mnear_scr, l_scr, acc_scr, *, qb, kb, qoff, lam_init):
    head = pl.program_id(1)
    i = pl.program_id(2)
    q0 = qoff + i * qb
    hd = B_HEAD_DIM
    far_bias = tab_ref[FAR_BUCKET, A_HEADS + head]
    far_k = lambda start, width: _key_rows(k_ref, start, width)
    far_v = lambda start, width: _key_rows(v_ref, start, width)

    def logits(c, kblk):
        s = lax.dot_general(q_ref[:, c * hd:(c + 1) * hd], kblk[:, c * hd:(c + 1) * hd], NT_DIMS,
                            preferred_element_type=F32)
        return [s[:, u * LANES:(u + 1) * LANES] for u in range(s.shape[1] // LANES)]

    def tile_cols(col0, n):
        return [tile_ref[0, :, col0 + u * LANES:col0 + (u + 1) * LANES] for u in range(n)]

    mfar_scr[...] = jnp.full(mfar_scr.shape, NEG_BIG, F32)
    mnear_scr[...] = jnp.full(mnear_scr.shape, NEG_BIG, F32)

    def max_far(start, blk0, colmask):
        kblk = far_k(start, kb)
        for c in range(2):
            s = logits(c, kblk)
            if colmask is not None:
                s = [x + m for x, m in zip(s, colmask)]
            mfar_scr[c] = functools.reduce(jnp.maximum, s, mfar_scr[c])

    def max_near(kblk, col0):
        for c in range(2):
            mnear_scr[c] = functools.reduce(
                jnp.maximum, [s + b for s, b in zip(logits(c, kblk), tile_cols(col0, kblk.shape[0] // LANES))],
                mnear_scr[c])

    prev_start = pl.multiple_of(q0 - LANES, LANES)
    _sweep_far(q0, qb, kb, max_far)
    pl.when(q0 > 0)(lambda: max_near(far_k(prev_start, LANES), 0))
    max_near(kc_ref[...], LANES)

    for c in range(2):
        m = jnp.maximum(jnp.max(mfar_scr[c], axis=-1, keepdims=True) + far_bias,
                        jnp.max(mnear_scr[c], axis=-1, keepdims=True))
        mfar_scr[c] = jnp.broadcast_to(far_bias - m, (qb, LANES))
        mnear_scr[c] = jnp.broadcast_to(-m, (qb, LANES))
    l_scr[...] = jnp.zeros(l_scr.shape, F32)
    acc_scr[...] = jnp.zeros(acc_scr.shape, F32)

    def accumulate(c, p, vblk):
        l_scr[c] += functools.reduce(jnp.add, p)
        acc_scr[c] += jnp.dot(jnp.concatenate(p, axis=1).astype(BF16), vblk, preferred_element_type=F32)

    def acc_far(start, blk0, colmask):
        kblk = far_k(start, kb)
        vblk = far_v(start, kb)
        for c in range(2):
            shift = mfar_scr[c]
            shifts = [shift] * (kb // LANES) if colmask is None else [shift + m for m in colmask]
            accumulate(c, [jnp.exp(s + sh) for s, sh in zip(logits(c, kblk), shifts)], vblk)

    def acc_near(kblk, vblk, col0):
        for c in range(2):
            shift = mnear_scr[c]
            accumulate(c, [jnp.exp(s + b + shift) for s, b in
                           zip(logits(c, kblk), tile_cols(col0, kblk.shape[0] // LANES))], vblk)

    _sweep_far(q0, qb, kb, acc_far)
    pl.when(q0 > 0)(lambda: acc_near(far_k(prev_start, LANES), far_v(prev_start, LANES), 0))
    acc_near(kc_ref[...], vc_ref[...], LANES)

    lam = lam_ref[0, 0]
    o = (acc_scr[0] / jnp.sum(l_scr[0], axis=-1, keepdims=True)
         - lam * (acc_scr[1] / jnp.sum(l_scr[1], axis=-1, keepdims=True)))
    o = o * lax.rsqrt(jnp.mean(o * o, axis=-1, keepdims=True) + EPS) * sg_ref[...]
    o_ref[...] = (o * (1.0 - lam_init)).astype(o_ref.dtype)


def _diff_call(rel_bias, tiles, lam, subln_g, q, far, cur, *, qb, kb, qoff, lam_init):
    b, t, _ = q.shape
    lf = far[0].shape[1]
    hw = 2 * B_HEAD_DIM
    qbr = tiles.shape[2] - LANES
    q0_max = qoff + t - qb
    far_end = -(-max(q0_max - LANES, 0) // kb) * kb
    assert t % qb == 0 and qoff % LANES == 0 and far_end <= lf and cur[0].shape[1] == (t // qb) * qbr
    far_spec = pl.BlockSpec((None, lf, hw), lambda bi, h, i: (bi, 0, h))
    cur_spec = pl.BlockSpec((None, qbr, hw), lambda bi, h, i: (bi, i, h))
    return pl.pallas_call(
        functools.partial(_diff_kernel, qb=qb, kb=kb, qoff=qoff, lam_init=lam_init),
        grid=(b, B_HEADS, t // qb),
        in_specs=[pl.BlockSpec(memory_space=pltpu.SMEM),
                  pl.BlockSpec(memory_space=pltpu.SMEM),
                  pl.BlockSpec((1, hw), lambda bi, h, i: (0, 0)),
                  pl.BlockSpec((None, qb, hw), lambda bi, h, i: (bi, i, h)),
                  far_spec, far_spec, cur_spec, cur_spec,
                  pl.BlockSpec((1, qb, tiles.shape[2]), lambda bi, h, i: (A_HEADS + h, 0, 0))],
        out_specs=pl.BlockSpec((None, qb, hw), lambda bi, h, i: (bi, i, h)),
        out_shape=jax.ShapeDtypeStruct((b, t, B_HEADS * hw), BF16),
        scratch_shapes=[pltpu.VMEM((2, qb, LANES), F32),
                        pltpu.VMEM((2, qb, LANES), F32),
                        pltpu.VMEM((2, qb, LANES), F32),
                        pltpu.VMEM((2, qb, hw), F32)],
        compiler_params=_params(3),
        name="mixer_b",
    )(rel_bias, lam, subln_g.reshape(1, hw), q, *far, *cur, tiles)


def _lam_kernel(dl_ref, o_ref, *, lam_init):
    dl = dl_ref[...]
    s1 = jnp.sum(dl[0:1] * dl[1:2], axis=-1, keepdims=True)
    s2 = jnp.sum(dl[2:3] * dl[3:4], axis=-1, keepdims=True)
    o_ref[...] = jnp.exp(s1) - jnp.exp(s2) + lam_init


def _lam_call(diff_lam, lam_init):
    return pl.pallas_call(
        functools.partial(_lam_kernel, lam_init=lam_init),
        out_shape=jax.ShapeDtypeStruct((1, 1), F32),
        name="diff_lambda",
    )(diff_lam)


def _route(logits_t, rbias, counts):
    tm = logits_t.shape[1]
    per = N_EXPERTS // N_GROUPS
    scores = jax.nn.sigmoid(logits_t).reshape(N_GROUPS, per, tm)
    ch = scores + rbias.reshape(N_GROUPS, per, 1)
    sub = lax.broadcasted_iota(jnp.int32, ch.shape, 1)
    grp = lax.broadcasted_iota(jnp.int32, ch.shape, 0)
    m1 = jnp.max(ch, axis=1, keepdims=True)
    i1 = jnp.min(jnp.where(ch == m1, sub, per), axis=1, keepdims=True)
    m2 = jnp.max(jnp.where(sub == i1, -jnp.inf, ch), axis=1, keepdims=True)
    gs = jnp.broadcast_to(m1 + m2, ch.shape)
    rank = jnp.zeros(ch.shape, jnp.int32)
    for g2 in range(N_GROUPS):
        o = gs[g2:g2 + 1]
        rank = rank + jnp.where(o > gs, 1, 0) + jnp.where(o == gs, 1, 0) * jnp.where(grp > g2, 1, 0)
    cm = jnp.where(rank < TOPK_GROUPS, ch, -jnp.inf)
    eid = grp * per + sub
    erank = jnp.zeros(cm.shape, jnp.int32)
    for e2 in range(N_EXPERTS):
        o = cm[e2 // per:e2 // per + 1, e2 % per:e2 % per + 1, :]
        erank = erank + jnp.where(o > cm, 1, 0) + jnp.where(o == cm, 1, 0) * jnp.where(eid > e2, 1, 0)
    chosen = erank < TOP_K
    sel = jnp.where(chosen, scores, 0.0)
    denom = jnp.sum(jnp.sum(sel, axis=1, keepdims=True), axis=0, keepdims=True)
    gates = sel / denom * ROUTED_SCALE

    ones = jnp.where(chosen, 1.0, 0.0).reshape(N_EXPERTS, tm)
    earlier = (lax.broadcasted_iota(jnp.int32, (tm, tm), 0) < lax.broadcasted_iota(jnp.int32, (tm, tm), 1))
    before = jnp.dot(ones.astype(BF16), jnp.where(earlier, 1.0, 0.0).astype(BF16), preferred_element_type=F32)
    pos = (before + counts.astype(F32)).reshape(N_GROUPS, per, tm)
    new_counts = counts + jnp.sum(ones, axis=1, keepdims=True).astype(jnp.int32)

    def per_slot(values):
        rows = [jnp.sum(jnp.sum(jnp.where(erank == k, values, 0.0), axis=1, keepdims=True), axis=0)
                for k in range(TOP_K)]
        return jnp.concatenate(rows, axis=0)

    return (per_slot(eid.astype(F32)).astype(jnp.int32), per_slot(pos).astype(jnp.int32), per_slot(gates),
            new_counts)


def _pack_pair(a, b):
    ua = pltpu.bitcast(a.astype(BF16).astype(F32), jnp.uint32)
    ub = pltpu.bitcast(b.astype(BF16).astype(F32), jnp.uint32)
    return pltpu.bitcast(ua | (ub >> 16), jnp.int32)


def _unpack_pair(p):
    u = pltpu.bitcast(p, jnp.uint32)
    a = pltpu.bitcast(u & jnp.uint32(0xFFFF0000), F32)
    b = pltpu.bitcast(u << 16, F32)
    return a.astype(BF16), b.astype(BF16)


PARTS = 4
PART_W = D_MODEL // 2 // PARTS


def _pack_rows(x):
    half = x.shape[1] // 2
    return jnp.stack([_pack_pair(x[:, j * PART_W:(j + 1) * PART_W],
                                 x[:, half + j * PART_W:half + (j + 1) * PART_W]) for j in range(PARTS)])


def _packed_matmul(parts_ref, w_ref):
    half = w_ref.shape[0] // 2
    acc = None
    for j in range(PARTS):
        a, b = _unpack_pair(parts_ref[j])
        t = (jnp.dot(a, w_ref[j * PART_W:(j + 1) * PART_W, :], preferred_element_type=F32)
             + jnp.dot(b, w_ref[half + j * PART_W:half + (j + 1) * PART_W, :], preferred_element_type=F32))
        acc = t if acc is None else acc + t
    return acc


def _outproj_kernel(x_ref, oa_ref, ob_ref, ga_ref, sc_ref, sh_ref, g_ref, w_ref, wr_ref, rb_ref, cin_ref,
                    x1_ref, h2_ref, ek_ref, pk_ref, gk_ref, cout_ref, cnt_scr):
    @pl.when(pl.program_id(0) == 0)
    def _():
        cnt_scr[...] = cin_ref[...]

    half = oa_ref.shape[1]
    mix = (jnp.dot(oa_ref[...], w_ref[:half, :], preferred_element_type=F32)
           + jnp.dot(ob_ref[...], w_ref[half:, :], preferred_element_type=F32))
    x1 = x_ref[...] + ga_ref[...] * mix
    x1_ref[...] = x1
    h2 = _modulated_norm(x1, g_ref[...], sc_ref[...], sh_ref[...])
    h2_ref[...] = _pack_rows(h2)
    logits_t = lax.dot_general(wr_ref[...], h2, NT_DIMS, preferred_element_type=F32,
                               precision=lax.Precision.HIGHEST)
    ek, pk, gk, counts = _route(logits_t, rb_ref[...], cnt_scr[...])
    ek_ref[...] = ek
    pk_ref[...] = pk
    gk_ref[...] = gk
    cnt_scr[...] = counts
    cout_ref[...] = counts


def _outproj_call(x2d, oa, ob, ga, sc, sh, mod_map, g, w_bf, wr_t, rbias, counts_in, tm):
    n, d = x2d.shape
    assert n % tm == 0
    r = sc.shape[1]
    mod_spec = pl.BlockSpec((None, r, d), mod_map)
    slot_spec = pl.BlockSpec((TOP_K, tm), lambda i: (0, i))
    return pl.pallas_call(
        _outproj_kernel,
        grid=(n // tm,),
        in_specs=[pl.BlockSpec((tm, d), lambda i: (i, 0)),
                  pl.BlockSpec((tm, oa.shape[1]), lambda i: (i, 0)),
                  pl.BlockSpec((tm, ob.shape[1]), lambda i: (i, 0)),
                  mod_spec, mod_spec, mod_spec,
                  pl.BlockSpec((1, d), lambda i: (0, 0)),
                  _resident(w_bf.shape, lambda i: (0, 0)),
                  _resident(wr_t.shape, lambda i: (0, 0)),
                  pl.BlockSpec((N_EXPERTS, 1), lambda i: (0, 0)),
                  pl.BlockSpec((N_EXPERTS, 1), lambda i: (0, 0))],
        out_specs=[pl.BlockSpec((tm, d), lambda i: (i, 0)),
                   pl.BlockSpec((PARTS, tm, PART_W), lambda i: (0, i, 0)),
                   slot_spec, slot_spec, slot_spec,
                   pl.BlockSpec((N_EXPERTS, 1), lambda i: (0, 0))],
        out_shape=[jax.ShapeDtypeStruct((n, d), F32),
                   jax.ShapeDtypeStruct((PARTS, n, PART_W), jnp.int32),
                   jax.ShapeDtypeStruct((TOP_K, n), jnp.int32),
                   jax.ShapeDtypeStruct((TOP_K, n), jnp.int32),
                   jax.ShapeDtypeStruct((TOP_K, n), F32),
                   jax.ShapeDtypeStruct((N_EXPERTS, 1), jnp.int32)],
        scratch_shapes=[pltpu.VMEM((N_EXPERTS, 1), jnp.int32)],
        compiler_params=_params(1),
        name="out_proj_router",
    )(x2d, oa, ob, ga, sc, sh, g.reshape(1, d), w_bf, wr_t, rbias, counts_in)


SC_WINDOW = 128
EXPERT_TILE = 512


def _sc_mesh():
    return plsc.VectorSubcoreMesh(core_axis_name="core", subcore_axis_name="subcore")


def _sc_scatter_rows(x, idx, n_out):
    n, w = x.shape
    steps = n // SC_WINDOW
    total = idx.shape[1] // SC_WINDOW

    @pl.kernel(out_type=jax.ShapeDtypeStruct((n_out, w), x.dtype), mesh=_sc_mesh(), scratch_types=[],
               name="moe_dispatch")
    def scatter(x_hbm, i_hbm, o_hbm):
        def body(x_vmem, i_vmem):
            pltpu.sync_copy(x_vmem, o_hbm.at[i_vmem.at[0]])

        pltpu.emit_pipeline(
            body, grid=(total,),
            in_specs=[pl.BlockSpec((SC_WINDOW, w), lambda i: (i % steps, 0)),
                      pl.BlockSpec((1, SC_WINDOW), lambda i: (0, i))],
            out_specs=[], core_axis_name=("core", "subcore"),
            dimension_semantics=(pltpu.PARALLEL,))(x_hbm, i_hbm)

    return scatter(x, idx)


def _sc_gather_rows(y, idx):
    w = y.shape[1]
    n = idx.shape[1]

    @pl.kernel(out_type=jax.ShapeDtypeStruct((n, w), y.dtype), mesh=_sc_mesh(), scratch_types=[],
               name="moe_collect")
    def gather(y_hbm, i_hbm, o_hbm):
        def body(i_vmem, o_vmem):
            pltpu.sync_copy(y_hbm.at[i_vmem.at[0]], o_vmem)

        pltpu.emit_pipeline(
            body, grid=(n // SC_WINDOW,),
            in_specs=[pl.BlockSpec((1, SC_WINDOW), lambda i: (0, i))],
            out_specs=[pl.BlockSpec((SC_WINDOW, w), lambda i: (i, 0))],
            core_axis_name=("core", "subcore"),
            dimension_semantics=(pltpu.PARALLEL,))(i_hbm, o_hbm)

    return gather(y, idx)


def _swiglu(parts_ref, wg_ref, wu_ref, wd_ref):
    a = _packed_matmul(parts_ref, wg_ref)
    u = _packed_matmul(parts_ref, wu_ref)
    hid = (a * jax.nn.sigmoid(a) * u).astype(BF16)
    return jnp.dot(hid, wd_ref[...], preferred_element_type=F32)


def _expert_kernel(te_ref, nu_ref, x_ref, wg_ref, wu_ref, wd_ref, y_ref, wg_scr, wu_scr, wd_scr):
    i = pl.program_id(0)

    @pl.when(i < nu_ref[0])
    def _():
        @pl.when((i == 0) | (te_ref[i] != te_ref[jnp.maximum(i - 1, 0)]))
        def _():
            wg_scr[...] = wg_ref[...].astype(BF16)
            wu_scr[...] = wu_ref[...].astype(BF16)
            wd_scr[...] = wd_ref[...].astype(BF16)

        y_ref[...] = _pack_rows(_swiglu(x_ref, wg_scr, wu_scr, wd_scr))


def _expert_call(tile_expert, n_used, x_sorted, wg, wu, wd):
    _, r, _ = x_sorted.shape
    ne, d, de = wg.shape
    n_tiles = r // EXPERT_TILE
    row_map = lambda i, te, nu: (0, jnp.minimum(i, nu[0] - 1), 0)
    return pl.pallas_call(
        _expert_kernel,
        grid_spec=pltpu.PrefetchScalarGridSpec(
            num_scalar_prefetch=2,
            grid=(n_tiles,),
            in_specs=[pl.BlockSpec((PARTS, EXPERT_TILE, PART_W), row_map),
                      pl.BlockSpec((None, d, de), lambda i, te, nu: (te[i], 0, 0)),
                      pl.BlockSpec((None, d, de), lambda i, te, nu: (te[i], 0, 0)),
                      pl.BlockSpec((None, de, d), lambda i, te, nu: (te[i], 0, 0))],
            out_specs=pl.BlockSpec((PARTS, EXPERT_TILE, PART_W), row_map),
            scratch_shapes=[pltpu.VMEM((d, de), BF16), pltpu.VMEM((d, de), BF16), pltpu.VMEM((de, d), BF16)]),
        out_shape=jax.ShapeDtypeStruct(x_sorted.shape, jnp.int32),
        compiler_params=_params(1),
        name="moe_experts",
    )(tile_expert, n_used, x_sorted, wg, wu, wd)


def _final_kernel(x_ref, h_ref, rows_ref, gk_ref, gf_ref, g_ref, sg_ref, su_ref, sd_ref, o_ref):
    moe = _swiglu(h_ref, sg_ref, su_ref, sd_ref)
    half = moe.shape[1] // 2
    lo = [moe[:, j * PART_W:(j + 1) * PART_W] for j in range(PARTS)]
    hi = [moe[:, half + j * PART_W:half + (j + 1) * PART_W] for j in range(PARTS)]
    for k in range(TOP_K):
        gate = gk_ref[:, k:k + 1]
        for j in range(PARTS):
            a, b = _unpack_pair(rows_ref[k, j])
            lo[j] = lo[j] + gate * a.astype(F32)
            hi[j] = hi[j] + gate * b.astype(F32)
    x = x_ref[...] + gf_ref[...] * jnp.concatenate(lo + hi, axis=1)
    o_ref[...] = x * lax.rsqrt(jnp.mean(x * x, axis=-1, keepdims=True) + EPS) * g_ref[...]


def _final_call(x1, h2p, rows, row_block0, gk_t, gf, mod_map, g, sg, su, sd, tm):
    n, d = x1.shape
    assert n % tm == 0
    r = gf.shape[1]
    return pl.pallas_call(
        _final_kernel,
        grid=(n // tm,),
        in_specs=[pl.BlockSpec((tm, d), lambda i: (i, 0)),
                  pl.BlockSpec((PARTS, tm, PART_W), lambda i: (0, i, 0)),
                  pl.BlockSpec((TOP_K, PARTS, tm, PART_W), lambda i: (0, 0, i + row_block0, 0)),
                  pl.BlockSpec((tm, TOP_K), lambda i: (i, 0)),
                  pl.BlockSpec((None, r, d), mod_map),
                  pl.BlockSpec((1, d), lambda i: (0, 0)),
                  _resident(sg.shape, lambda i: (0, 0)),
                  _resident(su.shape, lambda i: (0, 0)),
                  _resident(sd.shape, lambda i: (0, 0))],
        out_specs=pl.BlockSpec((tm, d), lambda i: (i, 0)),
        out_shape=jax.ShapeDtypeStruct((n, d), F32),
        compiler_params=_params(1),
        name="final_norm",
    )(x1, h2p, rows, gk_t, gf, g.reshape(1, d), sg, su, sd)


_QA, _KA, _VA, _QI, _KI, _IW, _QB, _KB, _VB = range(9)
_IN_SIZES = (A_HEADS * A_HEAD_DIM, A_KV_HEADS * A_HEAD_DIM, A_KV_HEADS * A_HEAD_DIM,
             IDX_HEADS * IDX_DIM, IDX_DIM, IDX_HEADS,
             B_HEADS * 2 * B_HEAD_DIM, B_HEADS * 2 * B_HEAD_DIM, B_HEADS * 2 * B_HEAD_DIM)


def _split_w_in(w_in):
    d = w_in.shape[0]
    offs = np.concatenate([[0], np.cumsum(_IN_SIZES)])
    cols = [w_in[:, offs[k]:offs[k + 1]].astype(BF16) for k in range(9)]
    zero = lambda n: jnp.zeros((d, n), BF16)
    qi = jnp.concatenate([cols[_QI].reshape(d, IDX_HEADS, IDX_DIM),
                          jnp.zeros((d, IDX_HEADS, LANES - IDX_DIM), BF16)], axis=-1).reshape(d, IDX_HEADS * LANES)
    w_a = jnp.concatenate([cols[_QA], cols[_KA], cols[_VA], qi,
                           cols[_KI], zero(LANES - IDX_DIM), cols[_IW], zero(LANES - IDX_HEADS)], axis=1)
    w_b = jnp.concatenate([cols[_QB], cols[_KB], cols[_VB]], axis=1)
    return w_a, w_b


def _group_forward(x, mod, past, rel_bias, lam, weights, *, lam_init, qb, qb_b, kb, tm, per_row_mod):
    (norm_a_g, w_a, w_b, subln_g) = weights
    b, t, d = x.shape
    n = b * t
    x2d = x.reshape(n, d)
    sh_a, sc_a = mod[0], mod[1]
    if per_row_mod:
        expand = lambda m: jnp.repeat(m, t, axis=0).reshape(1, n, d)
        mod_map = lambda i: (0, i, 0)
    else:
        expand = lambda m: m.reshape(b, 1, d)
        mod_map = lambda i: ((i * tm) // t, 0, 0)
    sc3, sh3 = expand(sc_a), expand(sh_a)

    na = A_HEADS * A_HEAD_DIM
    nk = A_KV_HEADS * A_HEAD_DIM
    nq = IDX_HEADS * LANES
    c_ka, c_va, c_qi = na, na + nk, na + 2 * nk
    c_ki, c_iw = c_qi + nq, c_qi + nq + LANES
    kv_order = tuple(range(A_KV_HEADS))
    defs_a = [(0, na, na, A_HEAD_DIM ** -0.5, None), (c_ka, nk, nk, None, kv_order), (c_ka, nk, nk, None, None),
              (c_va, nk, nk, None, kv_order), (c_va, nk, nk, None, None), (c_qi, nq, nq, None, None),
              (c_ki, LANES, IDX_DIM, None, None), (c_ki, LANES, LANES, None, None), (c_iw, LANES, LANES, None, None)]
    dt_a = [BF16, F32, BF16, F32, BF16, BF16, F32, BF16, F32]
    qa, ka, ka_bf, va, va_bf, qi, ki, ki_bf, iw = _norm_proj_call(
        x2d, sc3, sh3, mod_map, norm_a_g, w_a, defs_a, dt_a, tm, "in_proj_a")
    nb = B_HEADS * 2 * B_HEAD_DIM
    kb_order = tuple(range(2 * B_HEADS))
    vb_order = tuple(2 * h + half for half in range(2) for h in range(B_HEADS))
    defs_b = [(0, nb, nb, B_HEAD_DIM ** -0.5, None), (nb, nb, nb, None, kb_order), (nb, nb, nb, None, None),
              (2 * nb, nb, nb, None, vb_order), (2 * nb, nb, nb, None, None)]
    dt_b = [BF16, F32, BF16, F32, BF16]
    qbm, kbm, kb_bf, vbm, vb_bf = _norm_proj_call(
        x2d, sc3, sh3, mod_map, norm_a_g, w_b, defs_b, dt_b, tm, "in_proj_b")

    r3 = lambda a: a.reshape(b, t, a.shape[-1])
    cur = [r3(ka_bf), r3(va_bf), r3(ki_bf), r3(kb_bf), r3(vb_bf)]
    kv_pieces = None
    if past is None:
        qoff = 0
        far = cur
    else:
        assert t == qb, "a group with cached keys is one query block per batch row"
        qoff = past[0].shape[1]
        qbr = -(-qb // LANES) * LANES
        cur = [jnp.concatenate([a, jnp.zeros((b, qbr - t, a.shape[-1]), BF16)], axis=1) for a in cur]
        pk, pv, pi, pbk, pbv = past
        if qoff % kb == 0:
            kv_pieces = (A_KV_HEADS, tuple(range(A_KV_HEADS)))
            far = [pk.reshape(b, qoff * A_KV_HEADS, A_HEAD_DIM), pv.reshape(b, qoff * A_KV_HEADS, A_HEAD_DIM),
                   pi.reshape(b, qoff, -1), pbk.reshape(b, qoff, -1), pbv.reshape(b, qoff, -1)]
        else:
            far = [jnp.concatenate([p.reshape(b, qoff, -1), jnp.zeros((b, -qoff % kb, p[0, 0].size), p.dtype)], axis=1)
                   for p in past]
    length = qoff + t
    topk = min(TOPK_MAX, length // 4)

    tiles = _bias_tile_call(rel_bias, qb)
    tiles_b = tiles if qb_b == qb else _bias_tile_call(rel_bias, qb_b)
    out_a = _dsa_call(rel_bias, tiles, r3(qa), r3(qi), r3(iw), far[:3], cur[:3], qb=qb, kb=kb, qoff=qoff, topk=topk,
                      kv_pieces=kv_pieces)
    out_b = _diff_call(rel_bias, tiles_b, lam, subln_g, r3(qbm), far[3:], cur[3:], qb=qb_b, kb=kb, qoff=qoff,
                       lam_init=lam_init)
    rows = (ka.reshape(1, b, t, A_KV_HEADS, A_HEAD_DIM), va.reshape(1, b, t, A_KV_HEADS, A_HEAD_DIM),
            ki.reshape(1, b, t, IDX_DIM), kbm.reshape(1, b, t, B_HEADS, 2, B_HEAD_DIM),
            vbm.reshape(b, t, 2, B_HEADS, B_HEAD_DIM).transpose(0, 1, 3, 2, 4)
               .reshape(1, b, t, B_HEADS, 2 * B_HEAD_DIM))
    return x2d, out_a.reshape(n, -1), out_b.reshape(n, -1), rows, expand, mod_map


def kernel(x_prompt, x_sample, c_prompt, c_sample, cache_a_k, cache_a_v, cache_a_kidx, cache_b_k, cache_b_v,
           rel_bias, w_ada, b_ada, norm_a_g, w_in, w_out, diff_lam, subln_g, norm_f_g, w_router, router_bias,
           w_gate, w_up, w_down, ws_gate, ws_up, ws_down, final_g):
    assert w_ada.shape[0] == 1, "single-layer model"
    d = x_prompt.shape[-1]
    bp, tp, _ = x_prompt.shape
    bs, ts, _ = x_sample.shape
    lam_init = 0.8 - 0.6 * math.exp(-0.3 * 0)

    c_all = jnp.concatenate([c_prompt, c_sample], axis=0)
    rows = -(-c_all.shape[0] // 8) * 8
    c_pad = jnp.concatenate([c_all, jnp.zeros((rows - c_all.shape[0], d), F32)], axis=0)
    mod_all = _ada_call(c_pad, w_ada[0], b_ada[0])
    mod_p = [mod_all[:bp, k * d:(k + 1) * d] for k in range(6)]
    mod_s = [mod_all[bp:bp + bs, k * d:(k + 1) * d] for k in range(6)]

    w_a, w_b = _split_w_in(w_in[0])
    lam = _lam_call(diff_lam[0], lam_init)
    weights = (norm_a_g[0], w_a, w_b, subln_g[0])
    past = (cache_a_k[0], cache_a_v[0], cache_a_kidx[0], cache_b_k[0], cache_b_v[0])

    tm_p = 256
    tm_s = bs * ts
    xp2, oa_p, ob_p, rows_p, exp_p, map_p = _group_forward(
        x_prompt, mod_p, None, rel_bias, lam, weights, lam_init=lam_init, qb=256, qb_b=512, kb=512, tm=tm_p,
        per_row_mod=False)
    xs2, oa_s, ob_s, rows_s, exp_s, map_s = _group_forward(
        x_sample, mod_s, past, rel_bias, lam, weights, lam_init=lam_init, qb=ts, qb_b=ts, kb=512, tm=tm_s,
        per_row_mod=True)

    w_out_bf = w_out[0].astype(BF16)
    wr_t = w_router[0].T
    rb = router_bias[0].reshape(N_EXPERTS, 1)
    no_rows = jnp.zeros((N_EXPERTS, 1), jnp.int32)
    x1_p, h2_p, ek_p, pk_p, gk_p, cnt_p = _outproj_call(
        xp2, oa_p, ob_p, exp_p(mod_p[2]), exp_p(mod_p[4]), exp_p(mod_p[3]), map_p, norm_f_g[0], w_out_bf, wr_t, rb,
        no_rows, tm_p)
    x1_s, h2_s, ek_s, pk_s, gk_s, cnt_all = _outproj_call(
        xs2, oa_s, ob_s, exp_s(mod_s[2]), exp_s(mod_s[4]), exp_s(mod_s[3]), map_s, norm_f_g[0], w_out_bf, wr_t, rb,
        cnt_p, tm_s)

    n_p, n_s = bp * tp, bs * ts
    n_all = n_p + n_s
    n_rows = -(-(n_all * TOP_K) // EXPERT_TILE) * EXPERT_TILE + N_EXPERTS * EXPERT_TILE
    counts = cnt_all[:, 0]
    padded = (counts + EXPERT_TILE - 1) // EXPERT_TILE * EXPERT_TILE
    ends = jnp.cumsum(padded)
    starts = ends - padded
    tile_first_row = jnp.arange(n_rows // EXPERT_TILE, dtype=jnp.int32) * EXPERT_TILE
    tile_expert = jnp.minimum(jnp.sum(ends[None, :] <= tile_first_row[:, None], axis=1), N_EXPERTS - 1).astype(jnp.int32)
    n_used = (ends[-1:] // EXPERT_TILE).astype(jnp.int32)
    ek = jnp.concatenate([ek_p, ek_s], axis=1)
    expert_ids = jnp.arange(N_EXPERTS, dtype=jnp.int32)[:, None, None]
    dest = jnp.sum(jnp.where(ek[None] == expert_ids, starts[:, None, None], 0), axis=0) \
        + jnp.concatenate([pk_p, pk_s], axis=1)
    piece = jnp.arange(PARTS, dtype=jnp.int32)[None, :, None]
    idx = (dest[:, None, :] + piece * n_rows).reshape(1, TOP_K * PARTS * n_all).astype(jnp.int32)

    h2_all = jnp.concatenate([h2_p, h2_s], axis=1).reshape(PARTS * n_all, PART_W)
    x_sorted = _sc_scatter_rows(h2_all, idx, PARTS * n_rows).reshape(PARTS, n_rows, PART_W)
    y_sorted = _expert_call(tile_expert, n_used, x_sorted, w_gate[0], w_up[0], w_down[0])
    rows = _sc_gather_rows(y_sorted.reshape(PARTS * n_rows, PART_W), idx).reshape(TOP_K, PARTS, n_all, PART_W)

    shared = (ws_gate[0].astype(BF16), ws_up[0].astype(BF16), ws_down[0].astype(BF16))
    assert n_p % tm_s == 0
    y_p = _final_call(x1_p, h2_p, rows, 0, gk_p.T, exp_p(mod_p[5]), map_p, final_g, *shared, tm_p)
    y_s = _final_call(x1_s, h2_s, rows, n_p // tm_s, gk_s.T, exp_s(mod_s[5]), map_s, final_g, *shared, tm_s)
    return (y_p.reshape(bp, tp, d), y_s.reshape(bs, ts, d)) + rows_p + rows_s
```

```python
import functools
import math

import numpy as np
import jax
import jax.numpy as jnp
from jax import lax
from jax.experimental import pallas as pl
from jax.experimental.pallas import tpu as pltpu
from jax.experimental.pallas import tpu_sc as plsc

D_MODEL = 2048
CHUNK = 64
A_HEADS = 8
A_KV_HEADS = 2
A_REP = A_HEADS // A_KV_HEADS
A_HEAD_DIM = 128
IDX_HEADS = 16
IDX_DIM = 64
TOPK_MAX = 256
B_HEADS = 4
B_HEAD_DIM = 128
N_REL_BUCKETS = 32
REL_MAX_DIST = 128
N_ATTN_HEADS = A_HEADS + B_HEADS
N_EXPERTS = 64
TOP_K = 8
N_GROUPS = 8
TOPK_GROUPS = 4
EXPERT_DIM = 512
SHARED_DIM = 512
ROUTED_SCALE = 2.5
EPS = 1e-6

LANES = 128
INT_MIN = -(2 ** 31)
NEG_BIG = -1e30
TOP_LOW = 3
VMEM_LIMIT = 52 * 1024 * 1024

F32 = jnp.float32
BF16 = jnp.bfloat16
NT_DIMS = (((1,), (1,)), ((), ()))


def _params(n_axes):
    return pltpu.CompilerParams(dimension_semantics=("arbitrary",) * n_axes,
                                vmem_limit_bytes=VMEM_LIMIT)


def _resident(shape, index_map):
    return pl.BlockSpec(shape, index_map, pipeline_mode=pl.Buffered(1))


def _bucket_thresholds():
    nb = N_REL_BUCKETS // 2
    max_exact = nb // 2
    n = np.arange(max_exact, 4 * REL_MAX_DIST, dtype=np.int32)
    nf = n.astype(np.float32)
    large = max_exact + (np.log(nf / np.float32(max_exact)) / np.float32(math.log(REL_MAX_DIST / max_exact))
                         * np.float32(nb - max_exact)).astype(np.int32)
    large = np.minimum(large, nb - 1)
    assert np.all(np.diff(large) >= 0)
    ths = [int(n[np.argmax(large >= b)]) for b in range(max_exact + 1, nb)]
    return max_exact, nb, ths


_MAX_EXACT, _NB, _BUCKET_THS = _bucket_thresholds()
assert _BUCKET_THS[-1] <= LANES + 1
FAR_BUCKET = _NB - 1


def _ada_kernel(c_ref, w_ref, b_ref, o_ref):
    c = c_ref[...]
    s = c * jax.nn.sigmoid(c)
    o_ref[...] = jnp.dot(s, w_ref[...], preferred_element_type=F32,
                         precision=lax.Precision.HIGHEST) + b_ref[...]


def _ada_call(c, w_ada, b_ada):
    rows, d = c.shape
    n = w_ada.shape[1]
    tn = 1536
    assert n % tn == 0
    return pl.pallas_call(
        _ada_kernel,
        grid=(n // tn,),
        in_specs=[pl.BlockSpec((rows, d), lambda j: (0, 0)),
                  pl.BlockSpec((d, tn), lambda j: (0, j)),
                  pl.BlockSpec((1, tn), lambda j: (0, j))],
        out_specs=pl.BlockSpec((rows, tn), lambda j: (0, j)),
        out_shape=jax.ShapeDtypeStruct((rows, n), F32),
        compiler_params=_params(1),
        name="ada_mod",
    )(c, w_ada, b_ada.reshape(1, n))


def _modulated_norm(x, g, sc, sh):
    y = x * lax.rsqrt(jnp.mean(x * x, axis=-1, keepdims=True) + EPS) * g
    return y * (1.0 + sc) + sh


def _norm_proj_kernel(x_ref, sc_ref, sh_ref, g_ref, w_ref, *out_refs, defs):
    h = _modulated_norm(x_ref[...], g_ref[...], sc_ref[...], sh_ref[...]).astype(BF16)
    done = {}
    tm = x_ref.shape[0]
    for (start, width, store_w, mult, order), o_ref in zip(defs, out_refs):
        if (start, width) not in done:
            done[(start, width)] = jnp.dot(h, w_ref[:, start:start + width], preferred_element_type=F32)
        r = done[(start, width)][:, :store_w]
        if order is None:
            o_ref[...] = (r if mult is None else r * mult).astype(o_ref.dtype)
        else:
            for p, cb in enumerate(order):
                o_ref[pl.ds(p, tm, stride=len(order)), :] = r[:, cb * LANES:(cb + 1) * LANES].astype(o_ref.dtype)


def _norm_proj_call(x2d, sc, sh, mod_map, g, w_bf, defs, dtypes, tm, name):
    n, d = x2d.shape
    assert n % tm == 0
    r = sc.shape[1]
    shape = lambda rows, d_: (rows, d_[2]) if d_[4] is None else (rows * len(d_[4]), LANES)
    out_shapes = [jax.ShapeDtypeStruct(shape(n, d_), dt) for d_, dt in zip(defs, dtypes)]
    out_specs = [pl.BlockSpec(shape(tm, d_), lambda i: (i, 0)) for d_ in defs]
    return pl.pallas_call(
        functools.partial(_norm_proj_kernel, defs=tuple(defs)),
        grid=(n // tm,),
        in_specs=[pl.BlockSpec((tm, d), lambda i: (i, 0)),
                  pl.BlockSpec((None, r, d), mod_map),
                  pl.BlockSpec((None, r, d), mod_map),
                  pl.BlockSpec((1, d), lambda i: (0, 0)),
                  _resident(w_bf.shape, lambda i: (0, 0))],
        out_specs=out_specs,
        out_shape=out_shapes,
        compiler_params=_params(1),
        name=name,
    )(x2d, sc, sh, g.reshape(1, d), w_bf)


def _bias_tile_kernel(tab_ref, o_ref, *, qb):
    _, rows, width = o_ref.shape
    r = lax.broadcasted_iota(jnp.int32, (rows, width), 0)
    koff = lax.broadcasted_iota(jnp.int32, (rows, width), 1) - LANES
    rel = koff - r
    n = jnp.abs(rel)
    large = jnp.full((rows, width), _MAX_EXACT, jnp.int32)
    for th in _BUCKET_THS:
        large = large + jnp.where(n >= th, 1, 0)
    bucket = jnp.where(rel > 0, _NB, 0) + jnp.where(n < _MAX_EXACT, n, large)
    visible = ((koff >> 6) <= (r >> 6)) & (koff < qb)
    for h in range(N_ATTN_HEADS):
        b = jnp.zeros((rows, width), F32)
        for bk in range(N_REL_BUCKETS):
            b = jnp.where(bucket == bk, tab_ref[bk, h], b)
        o_ref[h] = jnp.where(visible, b, NEG_BIG)


def _bias_tile_call(rel_bias, qb):
    qbr = -(-qb // LANES) * LANES
    return pl.pallas_call(
        functools.partial(_bias_tile_kernel, qb=qb),
        in_specs=[pl.BlockSpec(memory_space=pltpu.SMEM)],
        out_specs=pl.BlockSpec(memory_space=pltpu.VMEM),
        out_shape=jax.ShapeDtypeStruct((N_ATTN_HEADS, qb, LANES + qbr), F32),
        name="bias_tiles",
    )(rel_bias)


def _sweep_far(q0, qb, kb, fn):
    far_end = jnp.maximum(q0 - LANES, 0)
    n_full = far_end // kb
    sub = kb // LANES

    def whole(j):
        fn(pl.multiple_of(j * kb, kb), j * sub, None)

    def pair(j, carry):
        whole(2 * j)
        whole(2 * j + 1)
        return carry

    lax.fori_loop(0, n_full // 2, pair, 0)
    pl.when(n_full % 2 == 1)(lambda: whole(n_full - 1))

    @pl.when(far_end > n_full * kb)
    def _():
        start = pl.multiple_of(n_full * kb, kb)
        col = start + lax.broadcasted_iota(jnp.int32, (qb, LANES), 1)
        fn(start, n_full * sub, [jnp.where(col + c * LANES < far_end, 0.0, NEG_BIG) for c in range(sub)])


def _key_rows(ref, start, width, pieces=None):
    if pieces is None:
        blk = ref[pl.ds(start, width), :].astype(BF16)
        pad = -blk.shape[1] % LANES
        return blk if pad == 0 else jnp.concatenate([blk, jnp.zeros((width, pad), BF16)], axis=1)
    r, rows = pieces
    window = ref.at[pl.ds(pl.multiple_of(start * r, LANES * r), width * r), :]
    return jnp.concatenate([window[pl.ds(p, width, stride=r), :].astype(BF16) for p in rows], axis=1)


def _dsa_kernel(tab_ref, q_ref, qi_ref, iw_ref, k_ref, v_ref, ki_ref, kc_ref, vc_ref, kic_ref, tile_ref, o_ref,
                keys_scr, half_scr, best_scr, thr_scr, mfar_scr, mnear_scr, l_scr, acc_scr, *, qb, kb, qoff, topk,
                kv_pieces):
    far_k = lambda start, width: _key_rows(k_ref, start, width, kv_pieces)
    far_v = lambda start, width: _key_rows(v_ref, start, width, kv_pieces)
    i = pl.program_id(1)
    q0 = qoff + i * qb
    qbr = tile_ref.shape[2] - LANES
    sub = kb // LANES
    w = iw_ref[:, :IDX_HEADS] * (IDX_HEADS ** -0.5 * IDX_DIM ** -0.5)

    def score_keys(kib):
        width = kib.shape[0]
        acc = jnp.zeros((qb, width), F32)
        for h in range(IDX_HEADS):
            d = lax.dot_general(qi_ref[:, h * LANES:(h + 1) * LANES], kib, NT_DIMS,
                                preferred_element_type=F32)
            acc = acc + w[:, h:h + 1] * jnp.maximum(d, 0.0)
        bits = pltpu.bitcast(acc, jnp.int32)
        return bits ^ ((bits >> 31) & 0x7FFFFFFF)

    n_score = (q0 + kb - 1) // kb

    def store_keys(slab0, key):
        for c in range(key.shape[1] // LANES):
            part = key[:, c * LANES:(c + 1) * LANES]
            keys_scr[slab0 + c] = part
            half_scr[slab0 + c] = (part >> 16).astype(jnp.int16)

    def score_body(j, carry):
        store_keys(j * sub, score_keys(_key_rows(ki_ref, pl.multiple_of(j * kb, kb), kb)))
        return carry

    lax.fori_loop(0, n_score, score_body, 0)

    cur_blk = q0 // LANES
    key = score_keys(kic_ref[...])
    store_keys(cur_blk, jnp.where(tile_ref[0, :, LANES:] > 0.5 * NEG_BIG, key, INT_MIN))
    n_blk = cur_blk + qbr // LANES

    rb = min(qb, LANES)
    lo16 = -(2 ** 15)
    n4 = n_blk // 4

    chunks = [pl.ds(c * rb, rb) for c in range(qb // rb)]
    ones = jnp.ones((LANES, LANES), BF16)

    def count_ge(ts):
        tbs = [t.astype(jnp.int16) for t in ts]

        def count(j, cnts):
            return tuple(cnt + jnp.where(half_scr[j, rows, :] >= tb, jnp.int16(1), jnp.int16(0))
                         for cnt, rows, tb in zip(cnts, chunks, tbs))

        def body4(j, cnts):
            for u in range(4):
                cnts = count(4 * j + u, cnts)
            return cnts

        cnts = lax.fori_loop(0, n4, body4, tuple(jnp.zeros((rb, LANES), jnp.int16) for _ in chunks))
        cnts = lax.fori_loop(4 * n4, n_blk, count, cnts)
        return [jnp.dot(cnt.astype(F32).astype(BF16), ones, preferred_element_type=F32) for cnt in cnts]

    def count_ge_best(ts):
        cnts = [sum(jnp.where(best_scr[u, rows, :] >= t.astype(jnp.int16), jnp.int16(1), jnp.int16(0))
                    for u in range(TOP_LOW)) for rows, t in zip(chunks, ts)]
        return [jnp.dot(cnt.astype(F32).astype(BF16), ones, preferred_element_type=F32) for cnt in cnts]

    def search(needed, counter):
        def bit_body(it, ts):
            cands = [t + (jnp.int32(1) << (15 - it)) for t in ts]
            return tuple(jnp.where(c >= n, cand, t) for c, n, cand, t in zip(counter(cands), needed, cands, ts))

        return lax.fori_loop(0, 16, bit_body, tuple(jnp.full((rb, LANES), lo16, jnp.int32) for _ in chunks))

    t_hi = search([float(topk)] * len(chunks), count_ge)
    top = -lo16 - 1
    above = [jnp.where(t == top, 0.0, c)
             for t, c in zip(t_hi, count_ge([jnp.minimum(t + 1, top) for t in t_hi]))]

    def low_plane(rows, tb):
        def body(j, carry):
            best, spill = carry[:TOP_LOW], carry[TOP_LOW]
            k32 = keys_scr[j, rows, :]
            a = jnp.where((k32 >> 16) == tb, (k32 & 0xFFFF) + lo16, lo16).astype(jnp.int16)
            half_scr[j, rows, :] = a
            kept = []
            for b in best:
                keep_b = b >= a
                kept.append(jnp.where(keep_b, b, a))
                a = jnp.where(keep_b, a, b)
            return tuple(kept) + (jnp.where(spill >= a, spill, a),)

        lowest = jnp.full((rb, LANES), lo16, jnp.int16)
        out = lax.fori_loop(0, n_blk, body, (lowest,) * (TOP_LOW + 1))
        for u in range(TOP_LOW):
            best_scr[u, rows, :] = out[u]
        return jnp.max(out[TOP_LOW].astype(jnp.int32))

    spilled = functools.reduce(jnp.maximum, [low_plane(rows, tb) for rows, tb in zip(chunks, t_hi)])
    needed = [topk - a for a in above]
    t_lo = lax.cond(spilled > lo16, lambda: search(needed, count_ge), lambda: search(needed, count_ge_best))
    for rows, th, tl in zip(chunks, t_hi, t_lo):
        thr_scr[rows, :] = jnp.maximum(th * 65536 + (tl - lo16), INT_MIN + 1)

    def count32(pred):
        def count(j, cnts):
            return tuple(cnt + jnp.where(pred(keys_scr[j, rows, :], thr_scr[rows, :]), 1, 0)
                         for cnt, rows in zip(cnts, chunks))

        cnts = lax.fori_loop(0, n_blk, count, tuple(jnp.zeros((rb, LANES), jnp.int32) for _ in chunks))
        return [jnp.dot(cnt.astype(F32).astype(BF16), ones, preferred_element_type=F32) for cnt in cnts]

    surplus = functools.reduce(jnp.maximum, [jnp.max(c) for c in count32(lambda k, t: k >= t)]) - topk

    @pl.when(surplus > 0)
    def _():
        keep = [topk - c for c in count32(lambda k, t: k > t)]
        upto = jnp.where(lax.broadcasted_iota(jnp.int32, (LANES, LANES), 0)
                         <= lax.broadcasted_iota(jnp.int32, (LANES, LANES), 1), 1.0, 0.0).astype(BF16)

        def demote(j, seen):
            out = []
            for rows, kp, sn in zip(chunks, keep, seen):
                k32 = keys_scr[j, rows, :]
                tied = k32 == thr_scr[rows, :]
                tied_bf = jnp.where(tied, 1.0, 0.0).astype(BF16)
                rank = sn + jnp.dot(tied_bf, upto, preferred_element_type=F32)
                keys_scr[j, rows, :] = k32 - jnp.where(tied & (rank > kp), 1, 0)
                out.append(sn + jnp.dot(tied_bf, ones, preferred_element_type=F32))
            return tuple(out)

        lax.fori_loop(0, n_blk, demote, tuple(jnp.zeros((rb, LANES), F32) for _ in chunks))

    hd = A_HEAD_DIM

    def select_mask(blk0, n):
        return [jnp.where(keys_scr[blk0 + c] >= thr_scr[...], 0.0, NEG_BIG) for c in range(n)]

    def logits(h, kblk):
        g = h // A_REP
        s = lax.dot_general(q_ref[:, h * hd:(h + 1) * hd], kblk[:, g * hd:(g + 1) * hd], NT_DIMS,
                            preferred_element_type=F32)
        return [s[:, c * LANES:(c + 1) * LANES] for c in range(s.shape[1] // LANES)]

    def tile_cols(h, col0, n):
        return [tile_ref[h, :, col0 + c * LANES:col0 + (c + 1) * LANES] for c in range(n)]

    mfar_scr[...] = jnp.full(mfar_scr.shape, NEG_BIG, F32)
    mnear_scr[...] = jnp.full(mnear_scr.shape, NEG_BIG, F32)

    def far_mask(blk0, colmask):
        am = select_mask(blk0, sub)
        return am if colmask is None else [a + c for a, c in zip(am, colmask)]

    def max_far(start, blk0, colmask):
        kblk = far_k(start, kb)
        am = far_mask(blk0, colmask)
        for h in range(A_HEADS):
            mfar_scr[h] = functools.reduce(
                jnp.maximum, [s + a for s, a in zip(logits(h, kblk), am)], mfar_scr[h])

    def max_near(kblk, blk0, col0):
        n = kblk.shape[0] // LANES
        am = select_mask(blk0, n)
        for h in range(A_HEADS):
            mnear_scr[h] = functools.reduce(
                jnp.maximum, [s + a + b for s, a, b in zip(logits(h, kblk), am, tile_cols(h, col0, n))],
                mnear_scr[h])

    prev_start = pl.multiple_of(q0 - LANES, LANES)
    _sweep_far(q0, qb, kb, max_far)
    pl.when(q0 > 0)(lambda: max_near(far_k(prev_start, LANES), cur_blk - 1, 0))
    max_near(kc_ref[...], cur_blk, LANES)

    for h in range(A_HEADS):
        fb = tab_ref[FAR_BUCKET, h]
        m = jnp.maximum(jnp.max(mfar_scr[h], axis=-1, keepdims=True) + fb,
                        jnp.max(mnear_scr[h], axis=-1, keepdims=True))
        mfar_scr[h] = jnp.broadcast_to(fb - m, (qb, LANES))
        mnear_scr[h] = jnp.broadcast_to(-m, (qb, LANES))
    l_scr[...] = jnp.zeros(l_scr.shape, F32)
    acc_scr[...] = jnp.zeros(acc_scr.shape, F32)

    def accumulate(h, p, vblk):
        g = h // A_REP
        l_scr[h] += functools.reduce(jnp.add, p)
        acc_scr[h] += jnp.dot(jnp.concatenate(p, axis=1).astype(BF16), vblk[:, g * hd:(g + 1) * hd],
                              preferred_element_type=F32)

    def acc_far(start, blk0, colmask):
        kblk = far_k(start, kb)
        vblk = far_v(start, kb)
        am = far_mask(blk0, colmask)
        for h in range(A_HEADS):
            shift = mfar_scr[h]
            accumulate(h, [jnp.exp(s + a + shift) for s, a in zip(logits(h, kblk), am)], vblk)

    def acc_near(kblk, vblk, blk0, col0):
        n = kblk.shape[0] // LANES
        am = select_mask(blk0, n)
        for h in range(A_HEADS):
            shift = mnear_scr[h]
            accumulate(h, [jnp.exp(s + a + b + shift) for s, a, b in
                           zip(logits(h, kblk), am, tile_cols(h, col0, n))], vblk)

    _sweep_far(q0, qb, kb, acc_far)
    pl.when(q0 > 0)(lambda: acc_near(far_k(prev_start, LANES), far_v(prev_start, LANES),
                                     cur_blk - 1, 0))
    acc_near(kc_ref[...], vc_ref[...], cur_blk, LANES)

    for h in range(A_HEADS):
        o_ref[:, h * hd:(h + 1) * hd] = (
            acc_scr[h] / jnp.sum(l_scr[h], axis=-1, keepdims=True)).astype(o_ref.dtype)


def _dsa_call(rel_bias, tiles, q, qi, iw, far, cur, *, qb, kb, qoff, topk, kv_pieces=None):
    b, t, _ = q.shape
    lf = far[2].shape[1]
    qbr = tiles.shape[2] - LANES
    q0_max = qoff + t - qb
    far_end = -(-q0_max // kb) * kb
    assert t % qb == 0 and qoff % LANES == 0 and far_end <= lf and cur[0].shape[1] == (t // qb) * qbr
    n_slabs = max(far_end, q0_max + qbr) // LANES
    mode = {"pipeline_mode": pl.Buffered(1)} if t // qb > 1 else {}
    far_spec = lambda a: pl.BlockSpec((None,) + a.shape[1:], lambda bi, i: (bi, 0, 0), **mode)
    cur_spec = lambda a: pl.BlockSpec((None, qbr, a.shape[2]), lambda bi, i: (bi, i, 0))
    return pl.pallas_call(
        functools.partial(_dsa_kernel, qb=qb, kb=kb, qoff=qoff, topk=topk, kv_pieces=kv_pieces),
        grid=(b, t // qb),
        in_specs=[pl.BlockSpec(memory_space=pltpu.SMEM),
                  pl.BlockSpec((None, qb, q.shape[2]), lambda bi, i: (bi, i, 0)),
                  pl.BlockSpec((None, qb, qi.shape[2]), lambda bi, i: (bi, i, 0)),
                  pl.BlockSpec((None, qb, iw.shape[2]), lambda bi, i: (bi, i, 0)),
                  far_spec(far[0]), far_spec(far[1]), far_spec(far[2]),
                  cur_spec(cur[0]), cur_spec(cur[1]), cur_spec(cur[2]),
                  _resident(tiles.shape, lambda bi, i: (0, 0, 0))],
        out_specs=pl.BlockSpec((None, qb, A_HEADS * A_HEAD_DIM), lambda bi, i: (bi, i, 0)),
        out_shape=jax.ShapeDtypeStruct((b, t, A_HEADS * A_HEAD_DIM), BF16),
        scratch_shapes=[pltpu.VMEM((n_slabs, qb, LANES), jnp.int32),
                        pltpu.VMEM((n_slabs, qb, LANES), jnp.int16),
                        pltpu.VMEM((TOP_LOW, qb, LANES), jnp.int16),
                        pltpu.VMEM((qb, LANES), jnp.int32),
                        pltpu.VMEM((A_HEADS, qb, LANES), F32),
                        pltpu.VMEM((A_HEADS, qb, LANES), F32),
                        pltpu.VMEM((A_HEADS, qb, LANES), F32),
                        pltpu.VMEM((A_HEADS, qb, A_HEAD_DIM), F32)],
        compiler_params=_params(2),
        name="mixer_a",
    )(rel_bias, q, qi, iw, *far, *cur, tiles)


def _diff_kernel(tab_ref, lam_ref, sg_ref, q_ref, k_ref, v_ref, kc_ref, vc_ref, tile_ref, o_ref,
                 mfar_scr, mnear_scr, l_scr, acc_scr, *, qb, kb, qoff, lam_init):
    head = pl.program_id(1)
    i = pl.program_id(2)
    q0 = qoff + i * qb
    hd = B_HEAD_DIM
    far_bias = tab_ref[FAR_BUCKET, A_HEADS + head]
    far_k = lambda start, width: _key_rows(k_ref, start, width)
    far_v = lambda start, width: _key_rows(v_ref, start, width)

    def logits(c, kblk):
        s = lax.dot_general(q_ref[:, c * hd:(c + 1) * hd], kblk[:, c * hd:(c + 1) * hd], NT_DIMS,
                            preferred_element_type=F32)
        return [s[:, u * LANES:(u + 1) * LANES] for u in range(s.shape[1] // LANES)]

    def tile_cols(col0, n):
        return [tile_ref[0, :, col0 + u * LANES:col0 + (u + 1) * LANES] for u in range(n)]

    mfar_scr[...] = jnp.full(mfar_scr.shape, NEG_BIG, F32)
    mnear_scr[...] = jnp.full(mnear_scr.shape, NEG_BIG, F32)

    def max_far(start, blk0, colmask):
        kblk = far_k(start, kb)
        for c in range(2):
            s = logits(c, kblk)
            if colmask is not None:
                s = [x + m for x, m in zip(s, colmask)]
            mfar_scr[c] = functools.reduce(jnp.maximum, s, mfar_scr[c])

    def max_near(kblk, col0):
        for c in range(2):
            mnear_scr[c] = functools.reduce(
                jnp.maximum, [s + b for s, b in zip(logits(c, kblk), tile_cols(col0, kblk.shape[0] // LANES))],
                mnear_scr[c])

    prev_start = pl.multiple_of(q0 - LANES, LANES)
    _sweep_far(q0, qb, kb, max_far)
    pl.when(q0 > 0)(lambda: max_near(far_k(prev_start, LANES), 0))
    max_near(kc_ref[...], LANES)

    for c in range(2):
        m = jnp.maximum(jnp.max(mfar_scr[c], axis=-1, keepdims=True) + far_bias,
                        jnp.max(mnear_scr[c], axis=-1, keepdims=True))
        mfar_scr[c] = jnp.broadcast_to(far_bias - m, (qb, LANES))
        mnear_scr[c] = jnp.broadcast_to(-m, (qb, LANES))
    l_scr[...] = jnp.zeros(l_scr.shape, F32)
    acc_scr[...] = jnp.zeros(acc_scr.shape, F32)

    def accumulate(c, p, vblk):
        l_scr[c] += functools.reduce(jnp.add, p)
        acc_scr[c] += jnp.dot(jnp.concatenate(p, axis=1).astype(BF16), vblk, preferred_element_type=F32)

    def acc_far(start, blk0, colmask):
        kblk = far_k(start, kb)
        vblk = far_v(start, kb)
        for c in range(2):
            shift = mfar_scr[c]
            shifts = [shift] * (kb // LANES) if colmask is None else [shift + m for m in colmask]
            accumulate(c, [jnp.exp(s + sh) for s, sh in zip(logits(c, kblk), shifts)], vblk)

    def acc_near(kblk, vblk, col0):
        for c in range(2):
            shift = mnear_scr[c]
            accumulate(c, [jnp.exp(s + b + shift) for s, b in
                           zip(logits(c, kblk), tile_cols(col0, kblk.shape[0] // LANES))], vblk)

    _sweep_far(q0, qb, kb, acc_far)
    pl.when(q0 > 0)(lambda: acc_near(far_k(prev_start, LANES), far_v(prev_start, LANES), 0))
    acc_near(kc_ref[...], vc_ref[...], LANES)

    lam = lam_ref[0, 0]
    o = (acc_scr[0] / jnp.sum(l_scr[0], axis=-1, keepdims=True)
         - lam * (acc_scr[1] / jnp.sum(l_scr[1], axis=-1, keepdims=True)))
    o = o * lax.rsqrt(jnp.mean(o * o, axis=-1, keepdims=True) + EPS) * sg_ref[...]
    o_ref[...] = (o * (1.0 - lam_init)).astype(o_ref.dtype)


def _diff_call(rel_bias, tiles, lam, subln_g, q, far, cur, *, qb, kb, qoff, lam_init):
    b, t, _ = q.shape
    lf = far[0].shape[1]
    hw = 2 * B_HEAD_DIM
    qbr = tiles.shape[2] - LANES
    q0_max = qoff + t - qb
    far_end = -(-max(q0_max - LANES, 0) // kb) * kb
    assert t % qb == 0 and qoff % LANES == 0 and far_end <= lf and cur[0].shape[1] == (t // qb) * qbr
    far_spec = pl.BlockSpec((None, lf, hw), lambda bi, h, i: (bi, 0, h))
    cur_spec = pl.BlockSpec((None, qbr, hw), lambda bi, h, i: (bi, i, h))
    return pl.pallas_call(
        functools.partial(_diff_kernel, qb=qb, kb=kb, qoff=qoff, lam_init=lam_init),
        grid=(b, B_HEADS, t // qb),
        in_specs=[pl.BlockSpec(memory_space=pltpu.SMEM),
                  pl.BlockSpec(memory_space=pltpu.SMEM),
                  pl.BlockSpec((1, hw), lambda bi, h, i: (0, 0)),
                  pl.BlockSpec((None, qb, hw), lambda bi, h, i: (bi, i, h)),
                  far_spec, far_spec, cur_spec, cur_spec,
                  pl.BlockSpec((1, qb, tiles.shape[2]), lambda bi, h, i: (A_HEADS + h, 0, 0))],
        out_specs=pl.BlockSpec((None, qb, hw), lambda bi, h, i: (bi, i, h)),
        out_shape=jax.ShapeDtypeStruct((b, t, B_HEADS * hw), BF16),
        scratch_shapes=[pltpu.VMEM((2, qb, LANES), F32),
                        pltpu.VMEM((2, qb, LANES), F32),
                        pltpu.VMEM((2, qb, LANES), F32),
                        pltpu.VMEM((2, qb, hw), F32)],
        compiler_params=_params(3),
        name="mixer_b",
    )(rel_bias, lam, subln_g.reshape(1, hw), q, *far, *cur, tiles)


def _lam_kernel(dl_ref, o_ref, *, lam_init):
    dl = dl_ref[...]
    s1 = jnp.sum(dl[0:1] * dl[1:2], axis=-1, keepdims=True)
    s2 = jnp.sum(dl[2:3] * dl[3:4], axis=-1, keepdims=True)
    o_ref[...] = jnp.exp(s1) - jnp.exp(s2) + lam_init


def _lam_call(diff_lam, lam_init):
    return pl.pallas_call(
        functools.partial(_lam_kernel, lam_init=lam_init),
        out_shape=jax.ShapeDtypeStruct((1, 1), F32),
        name="diff_lambda",
    )(diff_lam)


def _route(logits_t, rbias, counts):
    tm = logits_t.shape[1]
    per = N_EXPERTS // N_GROUPS
    scores = jax.nn.sigmoid(logits_t).reshape(N_GROUPS, per, tm)
    ch = scores + rbias.reshape(N_GROUPS, per, 1)
    sub = lax.broadcasted_iota(jnp.int32, ch.shape, 1)
    grp = lax.broadcasted_iota(jnp.int32, ch.shape, 0)
    m1 = jnp.max(ch, axis=1, keepdims=True)
    i1 = jnp.min(jnp.where(ch == m1, sub, per), axis=1, keepdims=True)
    m2 = jnp.max(jnp.where(sub == i1, -jnp.inf, ch), axis=1, keepdims=True)
    gs = jnp.broadcast_to(m1 + m2, ch.shape)
    rank = jnp.zeros(ch.shape, jnp.int32)
    for g2 in range(N_GROUPS):
        o = gs[g2:g2 + 1]
        rank = rank + jnp.where(o > gs, 1, 0) + jnp.where(o == gs, 1, 0) * jnp.where(grp > g2, 1, 0)
    cm = jnp.where(rank < TOPK_GROUPS, ch, -jnp.inf)
    eid = grp * per + sub
    erank = jnp.zeros(cm.shape, jnp.int32)
    for e2 in range(N_EXPERTS):
        o = cm[e2 // per:e2 // per + 1, e2 % per:e2 % per + 1, :]
        erank = erank + jnp.where(o > cm, 1, 0) + jnp.where(o == cm, 1, 0) * jnp.where(eid > e2, 1, 0)
    chosen = erank < TOP_K
    sel = jnp.where(chosen, scores, 0.0)
    denom = jnp.sum(jnp.sum(sel, axis=1, keepdims=True), axis=0, keepdims=True)
    gates = sel / denom * ROUTED_SCALE

    ones = jnp.where(chosen, 1.0, 0.0).reshape(N_EXPERTS, tm)
    earlier = (lax.broadcasted_iota(jnp.int32, (tm, tm), 0) < lax.broadcasted_iota(jnp.int32, (tm, tm), 1))
    before = jnp.dot(ones.astype(BF16), jnp.where(earlier, 1.0, 0.0).astype(BF16), preferred_element_type=F32)
    pos = (before + counts.astype(F32)).reshape(N_GROUPS, per, tm)
    new_counts = counts + jnp.sum(ones, axis=1, keepdims=True).astype(jnp.int32)

    def per_slot(values):
        rows = [jnp.sum(jnp.sum(jnp.where(erank == k, values, 0.0), axis=1, keepdims=True), axis=0)
                for k in range(TOP_K)]
        return jnp.concatenate(rows, axis=0)

    return (per_slot(eid.astype(F32)).astype(jnp.int32), per_slot(pos).astype(jnp.int32), per_slot(gates),
            new_counts)


def _pack_pair(a, b):
    ua = pltpu.bitcast(a.astype(BF16).astype(F32), jnp.uint32)
    ub = pltpu.bitcast(b.astype(BF16).astype(F32), jnp.uint32)
    return pltpu.bitcast(ua | (ub >> 16), jnp.int32)


def _unpack_pair(p):
    u = pltpu.bitcast(p, jnp.uint32)
    a = pltpu.bitcast(u & jnp.uint32(0xFFFF0000), F32)
    b = pltpu.bitcast(u << 16, F32)
    return a.astype(BF16), b.astype(BF16)


PARTS = 4
PART_W = D_MODEL // 2 // PARTS


def _pack_rows(x):
    half = x.shape[1] // 2
    return jnp.stack([_pack_pair(x[:, j * PART_W:(j + 1) * PART_W],
                                 x[:, half + j * PART_W:half + (j + 1) * PART_W]) for j in range(PARTS)])


def _packed_matmul(parts_ref, w_ref):
    half = w_ref.shape[0] // 2
    acc = None
    for j in range(PARTS):
        a, b = _unpack_pair(parts_ref[j])
        t = (jnp.dot(a, w_ref[j * PART_W:(j + 1) * PART_W, :], preferred_element_type=F32)
             + jnp.dot(b, w_ref[half + j * PART_W:half + (j + 1) * PART_W, :], preferred_element_type=F32))
        acc = t if acc is None else acc + t
    return acc


def _outproj_kernel(x_ref, oa_ref, ob_ref, ga_ref, sc_ref, sh_ref, g_ref, w_ref, wr_ref, rb_ref, cin_ref,
                    x1_ref, h2_ref, ek_ref, pk_ref, gk_ref, cout_ref, cnt_scr):
    @pl.when(pl.program_id(0) == 0)
    def _():
        cnt_scr[...] = cin_ref[...]

    half = oa_ref.shape[1]
    mix = (jnp.dot(oa_ref[...], w_ref[:half, :], preferred_element_type=F32)
           + jnp.dot(ob_ref[...], w_ref[half:, :], preferred_element_type=F32))
    x1 = x_ref[...] + ga_ref[...] * mix
    x1_ref[...] = x1
    h2 = _modulated_norm(x1, g_ref[...], sc_ref[...], sh_ref[...])
    h2_ref[...] = _pack_rows(h2)
    logits_t = lax.dot_general(wr_ref[...], h2, NT_DIMS, preferred_element_type=F32,
                               precision=lax.Precision.HIGHEST)
    ek, pk, gk, counts = _route(logits_t, rb_ref[...], cnt_scr[...])
    ek_ref[...] = ek
    pk_ref[...] = pk
    gk_ref[...] = gk
    cnt_scr[...] = counts
    cout_ref[...] = counts


def _outproj_call(x2d, oa, ob, ga, sc, sh, mod_map, g, w_bf, wr_t, rbias, counts_in, tm):
    n, d = x2d.shape
    assert n % tm == 0
    r = sc.shape[1]
    mod_spec = pl.BlockSpec((None, r, d), mod_map)
    slot_spec = pl.BlockSpec((TOP_K, tm), lambda i: (0, i))
    return pl.pallas_call(
        _outproj_kernel,
        grid=(n // tm,),
        in_specs=[pl.BlockSpec((tm, d), lambda i: (i, 0)),
                  pl.BlockSpec((tm, oa.shape[1]), lambda i: (i, 0)),
                  pl.BlockSpec((tm, ob.shape[1]), lambda i: (i, 0)),
                  mod_spec, mod_spec, mod_spec,
                  pl.BlockSpec((1, d), lambda i: (0, 0)),
                  _resident(w_bf.shape, lambda i: (0, 0)),
                  _resident(wr_t.shape, lambda i: (0, 0)),
                  pl.BlockSpec((N_EXPERTS, 1), lambda i: (0, 0)),
                  pl.BlockSpec((N_EXPERTS, 1), lambda i: (0, 0))],
        out_specs=[pl.BlockSpec((tm, d), lambda i: (i, 0)),
                   pl.BlockSpec((PARTS, tm, PART_W), lambda i: (0, i, 0)),
                   slot_spec, slot_spec, slot_spec,
                   pl.BlockSpec((N_EXPERTS, 1), lambda i: (0, 0))],
        out_shape=[jax.ShapeDtypeStruct((n, d), F32),
                   jax.ShapeDtypeStruct((PARTS, n, PART_W), jnp.int32),
                   jax.ShapeDtypeStruct((TOP_K, n), jnp.int32),
                   jax.ShapeDtypeStruct((TOP_K, n), jnp.int32),
                   jax.ShapeDtypeStruct((TOP_K, n), F32),
                   jax.ShapeDtypeStruct((N_EXPERTS, 1), jnp.int32)],
        scratch_shapes=[pltpu.VMEM((N_EXPERTS, 1), jnp.int32)],
        compiler_params=_params(1),
        name="out_proj_router",
    )(x2d, oa, ob, ga, sc, sh, g.reshape(1, d), w_bf, wr_t, rbias, counts_in)


SC_WINDOW = 128
EXPERT_TILE = 512


def _sc_mesh():
    return plsc.VectorSubcoreMesh(core_axis_name="core", subcore_axis_name="subcore")


def _sc_scatter_rows(x, idx, n_out):
    n, w = x.shape
    steps = n // SC_WINDOW
    total = idx.shape[1] // SC_WINDOW

    @pl.kernel(out_type=jax.ShapeDtypeStruct((n_out, w), x.dtype), mesh=_sc_mesh(), scratch_types=[],
               name="moe_dispatch")
    def scatter(x_hbm, i_hbm, o_hbm):
        def body(x_vmem, i_vmem):
            pltpu.sync_copy(x_vmem, o_hbm.at[i_vmem.at[0]])

        pltpu.emit_pipeline(
            body, grid=(total,),
            in_specs=[pl.BlockSpec((SC_WINDOW, w), lambda i: (i % steps, 0)),
                      pl.BlockSpec((1, SC_WINDOW), lambda i: (0, i))],
            out_specs=[], core_axis_name=("core", "subcore"),
            dimension_semantics=(pltpu.PARALLEL,))(x_hbm, i_hbm)

    return scatter(x, idx)


def _sc_gather_rows(y, idx):
    w = y.shape[1]
    n = idx.shape[1]

    @pl.kernel(out_type=jax.ShapeDtypeStruct((n, w), y.dtype), mesh=_sc_mesh(), scratch_types=[],
               name="moe_collect")
    def gather(y_hbm, i_hbm, o_hbm):
        def body(i_vmem, o_vmem):
            pltpu.sync_copy(y_hbm.at[i_vmem.at[0]], o_vmem)

        pltpu.emit_pipeline(
            body, grid=(n // SC_WINDOW,),
            in_specs=[pl.BlockSpec((1, SC_WINDOW), lambda i: (0, i))],
            out_specs=[pl.BlockSpec((SC_WINDOW, w), lambda i: (i, 0))],
            core_axis_name=("core", "subcore"),
            dimension_semantics=(pltpu.PARALLEL,))(i_hbm, o_hbm)

    return gather(y, idx)


def _swiglu(parts_ref, wg_ref, wu_ref, wd_ref):
    a = _packed_matmul(parts_ref, wg_ref)
    u = _packed_matmul(parts_ref, wu_ref)
    hid = (a * jax.nn.sigmoid(a) * u).astype(BF16)
    return jnp.dot(hid, wd_ref[...], preferred_element_type=F32)


def _expert_kernel(te_ref, nu_ref, x_ref, wg_ref, wu_ref, wd_ref, y_ref, wg_scr, wu_scr, wd_scr):
    i = pl.program_id(0)

    @pl.when(i < nu_ref[0])
    def _():
        @pl.when((i == 0) | (te_ref[i] != te_ref[jnp.maximum(i - 1, 0)]))
        def _():
            wg_scr[...] = wg_ref[...].astype(BF16)
            wu_scr[...] = wu_ref[...].astype(BF16)
            wd_scr[...] = wd_ref[...].astype(BF16)

        y_ref[...] = _pack_rows(_swiglu(x_ref, wg_scr, wu_scr, wd_scr))


def _expert_call(tile_expert, n_used, x_sorted, wg, wu, wd):
    _, r, _ = x_sorted.shape
    ne, d, de = wg.shape
    n_tiles = r // EXPERT_TILE
    row_map = lambda i, te, nu: (0, jnp.minimum(i, nu[0] - 1), 0)
    return pl.pallas_call(
        _expert_kernel,
        grid_spec=pltpu.PrefetchScalarGridSpec(
            num_scalar_prefetch=2,
            grid=(n_tiles,),
            in_specs=[pl.BlockSpec((PARTS, EXPERT_TILE, PART_W), row_map),
                      pl.BlockSpec((None, d, de), lambda i, te, nu: (te[i], 0, 0)),
                      pl.BlockSpec((None, d, de), lambda i, te, nu: (te[i], 0, 0)),
                      pl.BlockSpec((None, de, d), lambda i, te, nu: (te[i], 0, 0))],
            out_specs=pl.BlockSpec((PARTS, EXPERT_TILE, PART_W), row_map),
            scratch_shapes=[pltpu.VMEM((d, de), BF16), pltpu.VMEM((d, de), BF16), pltpu.VMEM((de, d), BF16)]),
        out_shape=jax.ShapeDtypeStruct(x_sorted.shape, jnp.int32),
        compiler_params=_params(1),
        name="moe_experts",
    )(tile_expert, n_used, x_sorted, wg, wu, wd)


def _final_kernel(x_ref, h_ref, rows_ref, gk_ref, gf_ref, g_ref, sg_ref, su_ref, sd_ref, o_ref):
    moe = _swiglu(h_ref, sg_ref, su_ref, sd_ref)
    half = moe.shape[1] // 2
    lo = [moe[:, j * PART_W:(j + 1) * PART_W] for j in range(PARTS)]
    hi = [moe[:, half + j * PART_W:half + (j + 1) * PART_W] for j in range(PARTS)]
    for k in range(TOP_K):
        gate = gk_ref[:, k:k + 1]
        for j in range(PARTS):
            a, b = _unpack_pair(rows_ref[k, j])
            lo[j] = lo[j] + gate * a.astype(F32)
            hi[j] = hi[j] + gate * b.astype(F32)
    x = x_ref[...] + gf_ref[...] * jnp.concatenate(lo + hi, axis=1)
    o_ref[...] = x * lax.rsqrt(jnp.mean(x * x, axis=-1, keepdims=True) + EPS) * g_ref[...]


def _final_call(x1, h2p, rows, row_block0, gk_t, gf, mod_map, g, sg, su, sd, tm):
    n, d = x1.shape
    assert n % tm == 0
    r = gf.shape[1]
    return pl.pallas_call(
        _final_kernel,
        grid=(n // tm,),
        in_specs=[pl.BlockSpec((tm, d), lambda i: (i, 0)),
                  pl.BlockSpec((PARTS, tm, PART_W), lambda i: (0, i, 0)),
                  pl.BlockSpec((TOP_K, PARTS, tm, PART_W), lambda i: (0, 0, i + row_block0, 0)),
                  pl.BlockSpec((tm, TOP_K), lambda i: (i, 0)),
                  pl.BlockSpec((None, r, d), mod_map),
                  pl.BlockSpec((1, d), lambda i: (0, 0)),
                  _resident(sg.shape, lambda i: (0, 0)),
                  _resident(su.shape, lambda i: (0, 0)),
                  _resident(sd.shape, lambda i: (0, 0))],
        out_specs=pl.BlockSpec((tm, d), lambda i: (i, 0)),
        out_shape=jax.ShapeDtypeStruct((n, d), F32),
        compiler_params=_params(1),
        name="final_norm",
    )(x1, h2p, rows, gk_t, gf, g.reshape(1, d), sg, su, sd)


_QA, _KA, _VA, _QI, _KI, _IW, _QB, _KB, _VB = range(9)
_IN_SIZES = (A_HEADS * A_HEAD_DIM, A_KV_HEADS * A_HEAD_DIM, A_KV_HEADS * A_HEAD_DIM,
             IDX_HEADS * IDX_DIM, IDX_DIM, IDX_HEADS,
             B_HEADS * 2 * B_HEAD_DIM, B_HEADS * 2 * B_HEAD_DIM, B_HEADS * 2 * B_HEAD_DIM)


def _split_w_in(w_in):
    d = w_in.shape[0]
    offs = np.concatenate([[0], np.cumsum(_IN_SIZES)])
    cols = [w_in[:, offs[k]:offs[k + 1]].astype(BF16) for k in range(9)]
    zero = lambda n: jnp.zeros((d, n), BF16)
    qi = jnp.concatenate([cols[_QI].reshape(d, IDX_HEADS, IDX_DIM),
                          jnp.zeros((d, IDX_HEADS, LANES - IDX_DIM), BF16)], axis=-1).reshape(d, IDX_HEADS * LANES)
    w_a = jnp.concatenate([cols[_QA], cols[_KA], cols[_VA], qi,
                           cols[_KI], zero(LANES - IDX_DIM), cols[_IW], zero(LANES - IDX_HEADS)], axis=1)
    w_b = jnp.concatenate([cols[_QB], cols[_KB], cols[_VB]], axis=1)
    return w_a, w_b


def _group_forward(x, mod, past, rel_bias, lam, weights, *, lam_init, qb, qb_b, kb, tm, per_row_mod):
    (norm_a_g, w_a, w_b, subln_g) = weights
    b, t, d = x.shape
    n = b * t
    x2d = x.reshape(n, d)
    sh_a, sc_a = mod[0], mod[1]
    if per_row_mod:
        expand = lambda m: jnp.repeat(m, t, axis=0).reshape(1, n, d)
        mod_map = lambda i: (0, i, 0)
    else:
        expand = lambda m: m.reshape(b, 1, d)
        mod_map = lambda i: ((i * tm) // t, 0, 0)
    sc3, sh3 = expand(sc_a), expand(sh_a)

    na = A_HEADS * A_HEAD_DIM
    nk = A_KV_HEADS * A_HEAD_DIM
    nq = IDX_HEADS * LANES
    c_ka, c_va, c_qi = na, na + nk, na + 2 * nk
    c_ki, c_iw = c_qi + nq, c_qi + nq + LANES
    kv_order = tuple(range(A_KV_HEADS))
    defs_a = [(0, na, na, A_HEAD_DIM ** -0.5, None), (c_ka, nk, nk, None, kv_order), (c_ka, nk, nk, None, None),
              (c_va, nk, nk, None, kv_order), (c_va, nk, nk, None, None), (c_qi, nq, nq, None, None),
              (c_ki, LANES, IDX_DIM, None, None), (c_ki, LANES, LANES, None, None), (c_iw, LANES, LANES, None, None)]
    dt_a = [BF16, F32, BF16, F32, BF16, BF16, F32, BF16, F32]
    qa, ka, ka_bf, va, va_bf, qi, ki, ki_bf, iw = _norm_proj_call(
        x2d, sc3, sh3, mod_map, norm_a_g, w_a, defs_a, dt_a, tm, "in_proj_a")
    nb = B_HEADS * 2 * B_HEAD_DIM
    kb_order = tuple(range(2 * B_HEADS))
    vb_order = tuple(2 * h + half for half in range(2) for h in range(B_HEADS))
    defs_b = [(0, nb, nb, B_HEAD_DIM ** -0.5, None), (nb, nb, nb, None, kb_order), (nb, nb, nb, None, None),
              (2 * nb, nb, nb, None, vb_order), (2 * nb, nb, nb, None, None)]
    dt_b = [BF16, F32, BF16, F32, BF16]
    qbm, kbm, kb_bf, vbm, vb_bf = _norm_proj_call(
        x2d, sc3, sh3, mod_map, norm_a_g, w_b, defs_b, dt_b, tm, "in_proj_b")

    r3 = lambda a: a.reshape(b, t, a.shape[-1])
    cur = [r3(ka_bf), r3(va_bf), r3(ki_bf), r3(kb_bf), r3(vb_bf)]
    kv_pieces = None
    if past is None:
        qoff = 0
        far = cur
    else:
        assert t == qb, "a group with cached keys is one query block per batch row"
        qoff = past[0].shape[1]
        qbr = -(-qb // LANES) * LANES
        cur = [jnp.concatenate([a, jnp.zeros((b, qbr - t, a.shape[-1]), BF16)], axis=1) for a in cur]
        pk, pv, pi, pbk, pbv = past
        if qoff % kb == 0:
            kv_pieces = (A_KV_HEADS, tuple(range(A_KV_HEADS)))
            far = [pk.reshape(b, qoff * A_KV_HEADS, A_HEAD_DIM), pv.reshape(b, qoff * A_KV_HEADS, A_HEAD_DIM),
                   pi.reshape(b, qoff, -1), pbk.reshape(b, qoff, -1), pbv.reshape(b, qoff, -1)]
        else:
            far = [jnp.concatenate([p.reshape(b, qoff, -1), jnp.zeros((b, -qoff % kb, p[0, 0].size), p.dtype)], axis=1)
                   for p in past]
    length = qoff + t
    topk = min(TOPK_MAX, length // 4)

    tiles = _bias_tile_call(rel_bias, qb)
    tiles_b = tiles if qb_b == qb else _bias_tile_call(rel_bias, qb_b)
    out_a = _dsa_call(rel_bias, tiles, r3(qa), r3(qi), r3(iw), far[:3], cur[:3], qb=qb, kb=kb, qoff=qoff, topk=topk,
                      kv_pieces=kv_pieces)
    out_b = _diff_call(rel_bias, tiles_b, lam, subln_g, r3(qbm), far[3:], cur[3:], qb=qb_b, kb=kb, qoff=qoff,
                       lam_init=lam_init)
    rows = (ka.reshape(1, b, t, A_KV_HEADS, A_HEAD_DIM), va.reshape(1, b, t, A_KV_HEADS, A_HEAD_DIM),
            ki.reshape(1, b, t, IDX_DIM), kbm.reshape(1, b, t, B_HEADS, 2, B_HEAD_DIM),
            vbm.reshape(b, t, 2, B_HEADS, B_HEAD_DIM).transpose(0, 1, 3, 2, 4)
               .reshape(1, b, t, B_HEADS, 2 * B_HEAD_DIM))
    return x2d, out_a.reshape(n, -1), out_b.reshape(n, -1), rows, expand, mod_map


def kernel(x_prompt, x_sample, c_prompt, c_sample, cache_a_k, cache_a_v, cache_a_kidx, cache_b_k, cache_b_v,
           rel_bias, w_ada, b_ada, norm_a_g, w_in, w_out, diff_lam, subln_g, norm_f_g, w_router, router_bias,
           w_gate, w_up, w_down, ws_gate, ws_up, ws_down, final_g):
    assert w_ada.shape[0] == 1, "single-layer model"
    d = x_prompt.shape[-1]
    bp, tp, _ = x_prompt.shape
    bs, ts, _ = x_sample.shape
    lam_init = 0.8 - 0.6 * math.exp(-0.3 * 0)

    c_all = jnp.concatenate([c_prompt, c_sample], axis=0)
    rows = -(-c_all.shape[0] // 8) * 8
    c_pad = jnp.concatenate([c_all, jnp.zeros((rows - c_all.shape[0], d), F32)], axis=0)
    mod_all = _ada_call(c_pad, w_ada[0], b_ada[0])
    mod_p = [mod_all[:bp, k * d:(k + 1) * d] for k in range(6)]
    mod_s = [mod_all[bp:bp + bs, k * d:(k + 1) * d] for k in range(6)]

    w_a, w_b = _split_w_in(w_in[0])
    lam = _lam_call(diff_lam[0], lam_init)
    weights = (norm_a_g[0], w_a, w_b, subln_g[0])
    past = (cache_a_k[0], cache_a_v[0], cache_a_kidx[0], cache_b_k[0], cache_b_v[0])

    tm_p = 256
    tm_s = bs * ts
    xp2, oa_p, ob_p, rows_p, exp_p, map_p = _group_forward(
        x_prompt, mod_p, None, rel_bias, lam, weights, lam_init=lam_init, qb=256, qb_b=512, kb=512, tm=tm_p,
        per_row_mod=False)
    xs2, oa_s, ob_s, rows_s, exp_s, map_s = _group_forward(
        x_sample, mod_s, past, rel_bias, lam, weights, lam_init=lam_init, qb=ts, qb_b=ts, kb=512, tm=tm_s,
        per_row_mod=True)

    w_out_bf = w_out[0].astype(BF16)
    wr_t = w_router[0].T
    rb = router_bias[0].reshape(N_EXPERTS, 1)
    no_rows = jnp.zeros((N_EXPERTS, 1), jnp.int32)
    x1_p, h2_p, ek_p, pk_p, gk_p, cnt_p = _outproj_call(
        xp2, oa_p, ob_p, exp_p(mod_p[2]), exp_p(mod_p[4]), exp_p(mod_p[3]), map_p, norm_f_g[0], w_out_bf, wr_t, rb,
        no_rows, tm_p)
    x1_s, h2_s, ek_s, pk_s, gk_s, cnt_all = _outproj_call(
        xs2, oa_s, ob_s, exp_s(mod_s[2]), exp_s(mod_s[4]), exp_s(mod_s[3]), map_s, norm_f_g[0], w_out_bf, wr_t, rb,
        cnt_p, tm_s)

    n_p, n_s = bp * tp, bs * ts
    n_all = n_p + n_s
    n_rows = -(-(n_all * TOP_K) // EXPERT_TILE) * EXPERT_TILE + N_EXPERTS * EXPERT_TILE
    counts = cnt_all[:, 0]
    padded = (counts + EXPERT_TILE - 1) // EXPERT_TILE * EXPERT_TILE
    ends = jnp.cumsum(padded)
    starts = ends - padded
    tile_first_row = jnp.arange(n_rows // EXPERT_TILE, dtype=jnp.int32) * EXPERT_TILE
    tile_expert = jnp.minimum(jnp.sum(ends[None, :] <= tile_first_row[:, None], axis=1), N_EXPERTS - 1).astype(jnp.int32)
    n_used = (ends[-1:] // EXPERT_TILE).astype(jnp.int32)
    ek = jnp.concatenate([ek_p, ek_s], axis=1)
    expert_ids = jnp.arange(N_EXPERTS, dtype=jnp.int32)[:, None, None]
    dest = jnp.sum(jnp.where(ek[None] == expert_ids, starts[:, None, None], 0), axis=0) \
        + jnp.concatenate([pk_p, pk_s], axis=1)
    piece = jnp.arange(PARTS, dtype=jnp.int32)[None, :, None]
    idx = (dest[:, None, :] + piece * n_rows).reshape(1, TOP_K * PARTS * n_all).astype(jnp.int32)

    h2_all = jnp.concatenate([h2_p, h2_s], axis=1).reshape(PARTS * n_all, PART_W)
    x_sorted = _sc_scatter_rows(h2_all, idx, PARTS * n_rows).reshape(PARTS, n_rows, PART_W)
    y_sorted = _expert_call(tile_expert, n_used, x_sorted, w_gate[0], w_up[0], w_down[0])
    rows = _sc_gather_rows(y_sorted.reshape(PARTS * n_rows, PART_W), idx).reshape(TOP_K, PARTS, n_all, PART_W)

    shared = (ws_gate[0].astype(BF16), ws_up[0].astype(BF16), ws_down[0].astype(BF16))
    assert n_p % tm_s == 0
    y_p = _final_call(x1_p, h2_p, rows, 0, gk_p.T, exp_p(mod_p[5]), map_p, final_g, *shared, tm_p)
    y_s = _final_call(x1_s, h2_s, rows, n_p // tm_s, gk_s.T, exp_s(mod_s[5]), map_s, final_g, *shared, tm_s)
    return (y_p.reshape(bp, tp, d), y_s.reshape(bs, ts, d)) + rows_p + rows_s
```

```python
import functools
import math

import numpy as np
import jax
import jax.numpy as jnp
from jax import lax
from jax.experimental import pallas as pl
from jax.experimental.pallas import tpu as pltpu
from jax.experimental.pallas import tpu_sc as plsc

D_MODEL = 2048
CHUNK = 64
A_HEADS = 8
A_KV_HEADS = 2
A_REP = A_HEADS // A_KV_HEADS
A_HEAD_DIM = 128
IDX_HEADS = 16
IDX_DIM = 64
TOPK_MAX = 256
B_HEADS = 4
B_HEAD_DIM = 128
N_REL_BUCKETS = 32
REL_MAX_DIST = 128
N_ATTN_HEADS = A_HEADS + B_HEADS
N_EXPERTS = 64
TOP_K = 8
N_GROUPS = 8
TOPK_GROUPS = 4
EXPERT_DIM = 512
SHARED_DIM = 512
ROUTED_SCALE = 2.5
EPS = 1e-6

LANES = 128
INT_MIN = -(2 ** 31)
NEG_BIG = -1e30
TOP_LOW = 3
VMEM_LIMIT = 52 * 1024 * 1024

F32 = jnp.float32
BF16 = jnp.bfloat16
NT_DIMS = (((1,), (1,)), ((), ()))


def _params(n_axes):
    return pltpu.CompilerParams(dimension_semantics=("arbitrary",) * n_axes,
                                vmem_limit_bytes=VMEM_LIMIT)


def _resident(shape, index_map):
    return pl.BlockSpec(shape, index_map, pipeline_mode=pl.Buffered(1))


def _bucket_thresholds():
    nb = N_REL_BUCKETS // 2
    max_exact = nb // 2
    n = np.arange(max_exact, 4 * REL_MAX_DIST, dtype=np.int32)
    nf = n.astype(np.float32)
    large = max_exact + (np.log(nf / np.float32(max_exact)) / np.float32(math.log(REL_MAX_DIST / max_exact))
                         * np.float32(nb - max_exact)).astype(np.int32)
    large = np.minimum(large, nb - 1)
    assert np.all(np.diff(large) >= 0)
    ths = [int(n[np.argmax(large >= b)]) for b in range(max_exact + 1, nb)]
    return max_exact, nb, ths


_MAX_EXACT, _NB, _BUCKET_THS = _bucket_thresholds()
assert _BUCKET_THS[-1] <= LANES + 1
FAR_BUCKET = _NB - 1


def _ada_kernel(c_ref, w_ref, b_ref, o_ref):
    c = c_ref[...]
    s = c * jax.nn.sigmoid(c)
    o_ref[...] = jnp.dot(s, w_ref[...], preferred_element_type=F32,
                         precision=lax.Precision.HIGHEST) + b_ref[...]


def _ada_call(c, w_ada, b_ada):
    rows, d = c.shape
    n = w_ada.shape[1]
    tn = 1536
    assert n % tn == 0
    return pl.pallas_call(
        _ada_kernel,
        grid=(n // tn,),
        in_specs=[pl.BlockSpec((rows, d), lambda j: (0, 0)),
                  pl.BlockSpec((d, tn), lambda j: (0, j)),
                  pl.BlockSpec((1, tn), lambda j: (0, j))],
        out_specs=pl.BlockSpec((rows, tn), lambda j: (0, j)),
        out_shape=jax.ShapeDtypeStruct((rows, n), F32),
        compiler_params=_params(1),
        name="ada_mod",
    )(c, w_ada, b_ada.reshape(1, n))


def _modulated_norm(x, g, sc, sh):
    y = x * lax.rsqrt(jnp.mean(x * x, axis=-1, keepdims=True) + EPS) * g
    return y * (1.0 + sc) + sh


def _norm_proj_kernel(x_ref, sc_ref, sh_ref, g_ref, w_ref, *out_refs, defs):
    h = _modulated_norm(x_ref[...], g_ref[...], sc_ref[...], sh_ref[...]).astype(BF16)
    done = {}
    tm = x_ref.shape[0]
    for (start, width, store_w, mult, order), o_ref in zip(defs, out_refs):
        if (start, width) not in done:
            done[(start, width)] = jnp.dot(h, w_ref[:, start:start + width], preferred_element_type=F32)
        r = done[(start, width)][:, :store_w]
        if order is None:
            o_ref[...] = (r if mult is None else r * mult).astype(o_ref.dtype)
        else:
            for p, cb in enumerate(order):
                o_ref[pl.ds(p, tm, stride=len(order)), :] = r[:, cb * LANES:(cb + 1) * LANES].astype(o_ref.dtype)


def _norm_proj_call(x2d, sc, sh, mod_map, g, w_bf, defs, dtypes, tm, name):
    n, d = x2d.shape
    assert n % tm == 0
    r = sc.shape[1]
    shape = lambda rows, d_: (rows, d_[2]) if d_[4] is None else (rows * len(d_[4]), LANES)
    out_shapes = [jax.ShapeDtypeStruct(shape(n, d_), dt) for d_, dt in zip(defs, dtypes)]
    out_specs = [pl.BlockSpec(shape(tm, d_), lambda i: (i, 0)) for d_ in defs]
    return pl.pallas_call(
        functools.partial(_norm_proj_kernel, defs=tuple(defs)),
        grid=(n // tm,),
        in_specs=[pl.BlockSpec((tm, d), lambda i: (i, 0)),
                  pl.BlockSpec((None, r, d), mod_map),
                  pl.BlockSpec((None, r, d), mod_map),
                  pl.BlockSpec((1, d), lambda i: (0, 0)),
                  _resident(w_bf.shape, lambda i: (0, 0))],
        out_specs=out_specs,
        out_shape=out_shapes,
        compiler_params=_params(1),
        name=name,
    )(x2d, sc, sh, g.reshape(1, d), w_bf)


def _bias_tile_kernel(tab_ref, o_ref, *, qb):
    _, rows, width = o_ref.shape
    r = lax.broadcasted_iota(jnp.int32, (rows, width), 0)
    koff = lax.broadcasted_iota(jnp.int32, (rows, width), 1) - LANES
    rel = koff - r
    n = jnp.abs(rel)
    large = jnp.full((rows, width), _MAX_EXACT, jnp.int32)
    for th in _BUCKET_THS:
        large = large + jnp.where(n >= th, 1, 0)
    bucket = jnp.where(rel > 0, _NB, 0) + jnp.where(n < _MAX_EXACT, n, large)
    visible = ((koff >> 6) <= (r >> 6)) & (koff < qb)
    for h in range(N_ATTN_HEADS):
        b = jnp.zeros((rows, width), F32)
        for bk in range(N_REL_BUCKETS):
            b = jnp.where(bucket == bk, tab_ref[bk, h], b)
        o_ref[h] = jnp.where(visible, b, NEG_BIG)


def _bias_tile_call(rel_bias, qb):
    qbr = -(-qb // LANES) * LANES
    return pl.pallas_call(
        functools.partial(_bias_tile_kernel, qb=qb),
        in_specs=[pl.BlockSpec(memory_space=pltpu.SMEM)],
        out_specs=pl.BlockSpec(memory_space=pltpu.VMEM),
        out_shape=jax.ShapeDtypeStruct((N_ATTN_HEADS, qb, LANES + qbr), F32),
        name="bias_tiles",
    )(rel_bias)


def _sweep_far(q0, qb, kb, fn):
    far_end = jnp.maximum(q0 - LANES, 0)
    n_full = far_end // kb
    sub = kb // LANES

    def whole(j):
        fn(pl.multiple_of(j * kb, kb), j * sub, None)

    def pair(j, carry):
        whole(2 * j)
        whole(2 * j + 1)
        return carry

    lax.fori_loop(0, n_full // 2, pair, 0)
    pl.when(n_full % 2 == 1)(lambda: whole(n_full - 1))

    @pl.when(far_end > n_full * kb)
    def _():
        start = pl.multiple_of(n_full * kb, kb)
        col = start + lax.broadcasted_iota(jnp.int32, (qb, LANES), 1)
        fn(start, n_full * sub, [jnp.where(col + c * LANES < far_end, 0.0, NEG_BIG) for c in range(sub)])


def _key_rows(ref, start, width, pieces=None):
    if pieces is None:
        blk = ref[pl.ds(start, width), :].astype(BF16)
        pad = -blk.shape[1] % LANES
        return blk if pad == 0 else jnp.concatenate([blk, jnp.zeros((width, pad), BF16)], axis=1)
    r, rows = pieces
    window = ref.at[pl.ds(pl.multiple_of(start * r, LANES * r), width * r), :]
    return jnp.concatenate([window[pl.ds(p, width, stride=r), :].astype(BF16) for p in rows], axis=1)


def _dsa_kernel(tab_ref, q_ref, qi_ref, iw_ref, k_ref, v_ref, ki_ref, kc_ref, vc_ref, kic_ref, tile_ref, o_ref,
                keys_scr, half_scr, best_scr, thr_scr, mfar_scr, mnear_scr, l_scr, acc_scr, *, qb, kb, qoff, topk,
                kv_pieces):
    far_k = lambda start, width: _key_rows(k_ref, start, width, kv_pieces)
    far_v = lambda start, width: _key_rows(v_ref, start, width, kv_pieces)
    i = pl.program_id(1)
    q0 = qoff + i * qb
    qbr = tile_ref.shape[2] - LANES
    sub = kb // LANES
    w = iw_ref[:, :IDX_HEADS] * (IDX_HEADS ** -0.5 * IDX_DIM ** -0.5)

    def score_keys(kib):
        width = kib.shape[0]
        acc = jnp.zeros((qb, width), F32)
        for h in range(IDX_HEADS):
            d = lax.dot_general(qi_ref[:, h * LANES:(h + 1) * LANES], kib, NT_DIMS,
                                preferred_element_type=F32)
            acc = acc + w[:, h:h + 1] * jnp.maximum(d, 0.0)
        bits = pltpu.bitcast(acc, jnp.int32)
        return bits ^ ((bits >> 31) & 0x7FFFFFFF)

    n_score = (q0 + kb - 1) // kb

    def store_keys(slab0, key):
        for c in range(key.shape[1] // LANES):
            part = key[:, c * LANES:(c + 1) * LANES]
            keys_scr[slab0 + c] = part
            half_scr[slab0 + c] = (part >> 16).astype(jnp.int16)

    def score_body(j, carry):
        store_keys(j * sub, score_keys(_key_rows(ki_ref, pl.multiple_of(j * kb, kb), kb)))
        return carry

    lax.fori_loop(0, n_score, score_body, 0)

    cur_blk = q0 // LANES
    key = score_keys(kic_ref[...])
    store_keys(cur_blk, jnp.where(tile_ref[0, :, LANES:] > 0.5 * NEG_BIG, key, INT_MIN))
    n_blk = cur_blk + qbr // LANES

    rb = min(qb, LANES)
    lo16 = -(2 ** 15)
    n4 = n_blk // 4

    chunks = [pl.ds(c * rb, rb) for c in range(qb // rb)]
    ones = jnp.ones((LANES, LANES), BF16)

    def count_ge(ts):
        tbs = [t.astype(jnp.int16) for t in ts]

        def count(j, cnts):
            return tuple(cnt + jnp.where(half_scr[j, rows, :] >= tb, jnp.int16(1), jnp.int16(0))
                         for cnt, rows, tb in zip(cnts, chunks, tbs))

        def body4(j, cnts):
            for u in range(4):
                cnts = count(4 * j + u, cnts)
            return cnts

        cnts = lax.fori_loop(0, n4, body4, tuple(jnp.zeros((rb, LANES), jnp.int16) for _ in chunks))
        cnts = lax.fori_loop(4 * n4, n_blk, count, cnts)
        return [jnp.dot(cnt.astype(F32).astype(BF16), ones, preferred_element_type=F32) for cnt in cnts]

    def count_ge_best(ts):
        cnts = [sum(jnp.where(best_scr[u, rows, :] >= t.astype(jnp.int16), jnp.int16(1), jnp.int16(0))
                    for u in range(TOP_LOW)) for rows, t in zip(chunks, ts)]
        return [jnp.dot(cnt.astype(F32).astype(BF16), ones, preferred_element_type=F32) for cnt in cnts]

    def search(needed, counter):
        def bit_body(it, ts):
            cands = [t + (jnp.int32(1) << (15 - it)) for t in ts]
            return tuple(jnp.where(c >= n, cand, t) for c, n, cand, t in zip(counter(cands), needed, cands, ts))

        return lax.fori_loop(0, 16, bit_body, tuple(jnp.full((rb, LANES), lo16, jnp.int32) for _ in chunks))

    assert topk <= 2 * LANES

    def bracket(rows):
        def body(j, carry):
            m1, m2 = carry
            a = half_scr[j, rows, :]
            first = a > m1
            return jnp.where(first, a, m1), jnp.where(first, m1, jnp.where(a > m2, a, m2))

        lowest = jnp.full((rb, LANES), lo16, jnp.int16)
        m1, m2 = lax.fori_loop(0, n_blk, body, (lowest, lowest))
        lo = jnp.min(m2.astype(jnp.int32), axis=-1, keepdims=True)
        hi = jnp.max(m1.astype(jnp.int32), axis=-1, keepdims=True)
        return jnp.broadcast_to(lo, (rb, LANES)), jnp.broadcast_to(hi, (rb, LANES))

    def search_between(los, his, needed, counter):
        span = functools.reduce(jnp.maximum, [jnp.max(h - l) for l, h in zip(los, his)])
        trips = sum(jnp.where(span >= (1 << b), 1, 0) for b in range(16))

        def body(it, carry):
            los, his = carry
            mids = [l + ((h - l + 1) >> 1) for l, h in zip(los, his)]
            ok = [c >= n for c, n in zip(counter(mids), needed)]
            return (tuple(jnp.where(k, m, l) for k, m, l in zip(ok, mids, los)),
                    tuple(jnp.where(k, h, m - 1) for k, m, h in zip(ok, mids, his)))

        return lax.fori_loop(0, trips, body, (tuple(los), tuple(his)))[0]

    brackets = [bracket(rows) for rows in chunks]
    t_hi = search_between([b[0] for b in brackets], [b[1] for b in brackets], [float(topk)] * len(chunks), count_ge)
    top = -lo16 - 1
    above = [jnp.where(t == top, 0.0, c)
             for t, c in zip(t_hi, count_ge([jnp.minimum(t + 1, top) for t in t_hi]))]

    def low_plane(rows, tb):
        def body(j, carry):
            best, spill = carry[:TOP_LOW], carry[TOP_LOW]
            k32 = keys_scr[j, rows, :]
            a = jnp.where((k32 >> 16) == tb, (k32 & 0xFFFF) + lo16, lo16).astype(jnp.int16)
            half_scr[j, rows, :] = a
            kept = []
            for b in best:
                keep_b = b >= a
                kept.append(jnp.where(keep_b, b, a))
                a = jnp.where(keep_b, a, b)
            return tuple(kept) + (jnp.where(spill >= a, spill, a),)

        lowest = jnp.full((rb, LANES), lo16, jnp.int16)
        out = lax.fori_loop(0, n_blk, body, (lowest,) * (TOP_LOW + 1))
        for u in range(TOP_LOW):
            best_scr[u, rows, :] = out[u]
        return jnp.max(out[TOP_LOW].astype(jnp.int32))

    spilled = functools.reduce(jnp.maximum, [low_plane(rows, tb) for rows, tb in zip(chunks, t_hi)])
    needed = [topk - a for a in above]
    t_lo = lax.cond(spilled > lo16, lambda: search(needed, count_ge), lambda: search(needed, count_ge_best))
    for rows, th, tl in zip(chunks, t_hi, t_lo):
        thr_scr[rows, :] = jnp.maximum(th * 65536 + (tl - lo16), INT_MIN + 1)

    def count32(pred):
        def count(j, cnts):
            return tuple(cnt + jnp.where(pred(keys_scr[j, rows, :], thr_scr[rows, :]), 1, 0)
                         for cnt, rows in zip(cnts, chunks))

        cnts = lax.fori_loop(0, n_blk, count, tuple(jnp.zeros((rb, LANES), jnp.int32) for _ in chunks))
        return [jnp.dot(cnt.astype(F32).astype(BF16), ones, preferred_element_type=F32) for cnt in cnts]

    surplus = functools.reduce(jnp.maximum, [jnp.max(c) for c in count32(lambda k, t: k >= t)]) - topk

    @pl.when(surplus > 0)
    def _():
        keep = [topk - c for c in count32(lambda k, t: k > t)]
        upto = jnp.where(lax.broadcasted_iota(jnp.int32, (LANES, LANES), 0)
                         <= lax.broadcasted_iota(jnp.int32, (LANES, LANES), 1), 1.0, 0.0).astype(BF16)

        def demote(j, seen):
            out = []
            for rows, kp, sn in zip(chunks, keep, seen):
                k32 = keys_scr[j, rows, :]
                tied = k32 == thr_scr[rows, :]
                tied_bf = jnp.where(tied, 1.0, 0.0).astype(BF16)
                rank = sn + jnp.dot(tied_bf, upto, preferred_element_type=F32)
                keys_scr[j, rows, :] = k32 - jnp.where(tied & (rank > kp), 1, 0)
                out.append(sn + jnp.dot(tied_bf, ones, preferred_element_type=F32))
            return tuple(out)

        lax.fori_loop(0, n_blk, demote, tuple(jnp.zeros((rb, LANES), F32) for _ in chunks))

    hd = A_HEAD_DIM

    def select_mask(blk0, n):
        return [jnp.where(keys_scr[blk0 + c] >= thr_scr[...], 0.0, NEG_BIG) for c in range(n)]

    def logits(h, kblk):
        g = h // A_REP
        s = lax.dot_general(q_ref[:, h * hd:(h + 1) * hd], kblk[:, g * hd:(g + 1) * hd], NT_DIMS,
                            preferred_element_type=F32)
        return [s[:, c * LANES:(c + 1) * LANES] for c in range(s.shape[1] // LANES)]

    def tile_cols(h, col0, n):
        return [tile_ref[h, :, col0 + c * LANES:col0 + (c + 1) * LANES] for c in range(n)]

    mfar_scr[...] = jnp.full(mfar_scr.shape, NEG_BIG, F32)
    mnear_scr[...] = jnp.full(mnear_scr.shape, NEG_BIG, F32)

    def far_mask(blk0, colmask):
        am = select_mask(blk0, sub)
        return am if colmask is None else [a + c for a, c in zip(am, colmask)]

    def max_far(start, blk0, colmask):
        kblk = far_k(start, kb)
        am = far_mask(blk0, colmask)
        for h in range(A_HEADS):
            mfar_scr[h] = functools.reduce(
                jnp.maximum, [s + a for s, a in zip(logits(h, kblk), am)], mfar_scr[h])

    def max_near(kblk, blk0, col0):
        n = kblk.shape[0] // LANES
        am = select_mask(blk0, n)
        for h in range(A_HEADS):
            mnear_scr[h] = functools.reduce(
                jnp.maximum, [s + a + b for s, a, b in zip(logits(h, kblk), am, tile_cols(h, col0, n))],
                mnear_scr[h])

    prev_start = pl.multiple_of(q0 - LANES, LANES)
    _sweep_far(q0, qb, kb, max_far)
    pl.when(q0 > 0)(lambda: max_near(far_k(prev_start, LANES), cur_blk - 1, 0))
    max_near(kc_ref[...], cur_blk, LANES)

    for h in range(A_HEADS):
        fb = tab_ref[FAR_BUCKET, h]
        m = jnp.maximum(jnp.max(mfar_scr[h], axis=-1, keepdims=True) + fb,
                        jnp.max(mnear_scr[h], axis=-1, keepdims=True))
        mfar_scr[h] = jnp.broadcast_to(fb - m, (qb, LANES))
        mnear_scr[h] = jnp.broadcast_to(-m, (qb, LANES))
    l_scr[...] = jnp.zeros(l_scr.shape, F32)
    acc_scr[...] = jnp.zeros(acc_scr.shape, F32)

    def accumulate(h, p, vblk):
        g = h // A_REP
        l_scr[h] += functools.reduce(jnp.add, p)
        acc_scr[h] += jnp.dot(jnp.concatenate(p, axis=1).astype(BF16), vblk[:, g * hd:(g + 1) * hd],
                              preferred_element_type=F32)

    def acc_far(start, blk0, colmask):
        kblk = far_k(start, kb)
        vblk = far_v(start, kb)
        am = far_mask(blk0, colmask)
        for h in range(A_HEADS):
            shift = mfar_scr[h]
            accumulate(h, [jnp.exp(s + a + shift) for s, a in zip(logits(h, kblk), am)], vblk)

    def acc_near(kblk, vblk, blk0, col0):
        n = kblk.shape[0] // LANES
        am = select_mask(blk0, n)
        for h in range(A_HEADS):
            shift = mnear_scr[h]
            accumulate(h, [jnp.exp(s + a + b + shift) for s, a, b in
                           zip(logits(h, kblk), am, tile_cols(h, col0, n))], vblk)

    _sweep_far(q0, qb, kb, acc_far)
    pl.when(q0 > 0)(lambda: acc_near(far_k(prev_start, LANES), far_v(prev_start, LANES),
                                     cur_blk - 1, 0))
    acc_near(kc_ref[...], vc_ref[...], cur_blk, LANES)

    for h in range(A_HEADS):
        o_ref[:, h * hd:(h + 1) * hd] = (
            acc_scr[h] / jnp.sum(l_scr[h], axis=-1, keepdims=True)).astype(o_ref.dtype)


def _dsa_call(rel_bias, tiles, q, qi, iw, far, cur, *, qb, kb, qoff, topk, kv_pieces=None):
    b, t, _ = q.shape
    lf = far[2].shape[1]
    qbr = tiles.shape[2] - LANES
    q0_max = qoff + t - qb
    far_end = -(-q0_max // kb) * kb
    assert t % qb == 0 and qoff % LANES == 0 and far_end <= lf and cur[0].shape[1] == (t // qb) * qbr
    n_slabs = max(far_end, q0_max + qbr) // LANES
    mode = {"pipeline_mode": pl.Buffered(1)} if t // qb > 1 else {}
    far_spec = lambda a: pl.BlockSpec((None,) + a.shape[1:], lambda bi, i: (bi, 0, 0), **mode)
    cur_spec = lambda a: pl.BlockSpec((None, qbr, a.shape[2]), lambda bi, i: (bi, i, 0))
    return pl.pallas_call(
        functools.partial(_dsa_kernel, qb=qb, kb=kb, qoff=qoff, topk=topk, kv_pieces=kv_pieces),
        grid=(b, t // qb),
        in_specs=[pl.BlockSpec(memory_space=pltpu.SMEM),
                  pl.BlockSpec((None, qb, q.shape[2]), lambda bi, i: (bi, i, 0)),
                  pl.BlockSpec((None, qb, qi.shape[2]), lambda bi, i: (bi, i, 0)),
                  pl.BlockSpec((None, qb, iw.shape[2]), lambda bi, i: (bi, i, 0)),
                  far_spec(far[0]), far_spec(far[1]), far_spec(far[2]),
                  cur_spec(cur[0]), cur_spec(cur[1]), cur_spec(cur[2]),
                  _resident(tiles.shape, lambda bi, i: (0, 0, 0))],
        out_specs=pl.BlockSpec((None, qb, A_HEADS * A_HEAD_DIM), lambda bi, i: (bi, i, 0)),
        out_shape=jax.ShapeDtypeStruct((b, t, A_HEADS * A_HEAD_DIM), BF16),
        scratch_shapes=[pltpu.VMEM((n_slabs, qb, LANES), jnp.int32),
                        pltpu.VMEM((n_slabs, qb, LANES), jnp.int16),
                        pltpu.VMEM((TOP_LOW, qb, LANES), jnp.int16),
                        pltpu.VMEM((qb, LANES), jnp.int32),
                        pltpu.VMEM((A_HEADS, qb, LANES), F32),
                        pltpu.VMEM((A_HEADS, qb, LANES), F32),
                        pltpu.VMEM((A_HEADS, qb, LANES), F32),
                        pltpu.VMEM((A_HEADS, qb, A_HEAD_DIM), F32)],
        compiler_params=_params(2),
        name="mixer_a",
    )(rel_bias, q, qi, iw, *far, *cur, tiles)


def _diff_kernel(tab_ref, lam_ref, sg_ref, q_ref, k_ref, v_ref, kc_ref, vc_ref, tile_ref, o_ref,
                 mfar_scr, mnear_scr, l_scr, acc_scr, *, qb, kb, qoff, lam_init):
    head = pl.program_id(1)
    i = pl.program_id(2)
    q0 = qoff + i * qb
    hd = B_HEAD_DIM
    far_bias = tab_ref[FAR_BUCKET, A_HEADS + head]
    far_k = lambda start, width: _key_rows(k_ref, start, width)
    far_v = lambda start, width: _key_rows(v_ref, start, width)

    def logits(c, kblk):
        s = lax.dot_general(q_ref[:, c * hd:(c + 1) * hd], kblk[:, c * hd:(c + 1) * hd], NT_DIMS,
                            preferred_element_type=F32)
        return [s[:, u * LANES:(u + 1) * LANES] for u in range(s.shape[1] // LANES)]

    def tile_cols(col0, n):
        return [tile_ref[0, :, col0 + u * LANES:col0 + (u + 1) * LANES] for u in range(n)]

    mfar_scr[...] = jnp.full(mfar_scr.shape, NEG_BIG, F32)
    mnear_scr[...] = jnp.full(mnear_scr.shape, NEG_BIG, F32)

    def max_far(start, blk0, colmask):
        kblk = far_k(start, kb)
        for c in range(2):
            s = logits(c, kblk)
            if colmask is not None:
                s = [x + m for x, m in zip(s, colmask)]
            mfar_scr[c] = functools.reduce(jnp.maximum, s, mfar_scr[c])

    def max_near(kblk, col0):
        for c in range(2):
            mnear_scr[c] = functools.reduce(
                jnp.maximum, [s + b for s, b in zip(logits(c, kblk), tile_cols(col0, kblk.shape[0] // LANES))],
                mnear_scr[c])

    prev_start = pl.multiple_of(q0 - LANES, LANES)
    _sweep_far(q0, qb, kb, max_far)
    pl.when(q0 > 0)(lambda: max_near(far_k(prev_start, LANES), 0))
    max_near(kc_ref[...], LANES)

    for c in range(2):
        m = jnp.maximum(jnp.max(mfar_scr[c], axis=-1, keepdims=True) + far_bias,
                        jnp.max(mnear_scr[c], axis=-1, keepdims=True))
        mfar_scr[c] = jnp.broadcast_to(far_bias - m, (qb, LANES))
        mnear_scr[c] = jnp.broadcast_to(-m, (qb, LANES))
    l_scr[...] = jnp.zeros(l_scr.shape, F32)
    acc_scr[...] = jnp.zeros(acc_scr.shape, F32)

    def accumulate(c, p, vblk):
        l_scr[c] += functools.reduce(jnp.add, p)
        acc_scr[c] += jnp.dot(jnp.concatenate(p, axis=1).astype(BF16), vblk, preferred_element_type=F32)

    def acc_far(start, blk0, colmask):
        kblk = far_k(start, kb)
        vblk = far_v(start, kb)
        for c in range(2):
            shift = mfar_scr[c]
            shifts = [shift] * (kb // LANES) if colmask is None else [shift + m for m in colmask]
            accumulate(c, [jnp.exp(s + sh) for s, sh in zip(logits(c, kblk), shifts)], vblk)

    def acc_near(kblk, vblk, col0):
        for c in range(2):
            shift = mnear_scr[c]
            accumulate(c, [jnp.exp(s + b + shift) for s, b in
                           zip(logits(c, kblk), tile_cols(col0, kblk.shape[0] // LANES))], vblk)

    _sweep_far(q0, qb, kb, acc_far)
    pl.when(q0 > 0)(lambda: acc_near(far_k(prev_start, LANES), far_v(prev_start, LANES), 0))
    acc_near(kc_ref[...], vc_ref[...], LANES)

    lam = lam_ref[0, 0]
    o = (acc_scr[0] / jnp.sum(l_scr[0], axis=-1, keepdims=True)
         - lam * (acc_scr[1] / jnp.sum(l_scr[1], axis=-1, keepdims=True)))
    o = o * lax.rsqrt(jnp.mean(o * o, axis=-1, keepdims=True) + EPS) * sg_ref[...]
    o_ref[...] = (o * (1.0 - lam_init)).astype(o_ref.dtype)


def _diff_call(rel_bias, tiles, lam, subln_g, q, far, cur, *, qb, kb, qoff, lam_init):
    b, t, _ = q.shape
    lf = far[0].shape[1]
    hw = 2 * B_HEAD_DIM
    qbr = tiles.shape[2] - LANES
    q0_max = qoff + t - qb
    far_end = -(-max(q0_max - LANES, 0) // kb) * kb
    assert t % qb == 0 and qoff % LANES == 0 and far_end <= lf and cur[0].shape[1] == (t // qb) * qbr
    far_spec = pl.BlockSpec((None, lf, hw), lambda bi, h, i: (bi, 0, h))
    cur_spec = pl.BlockSpec((None, qbr, hw), lambda bi, h, i: (bi, i, h))
    return pl.pallas_call(
        functools.partial(_diff_kernel, qb=qb, kb=kb, qoff=qoff, lam_init=lam_init),
        grid=(b, B_HEADS, t // qb),
        in_specs=[pl.BlockSpec(memory_space=pltpu.SMEM),
                  pl.BlockSpec(memory_space=pltpu.SMEM),
                  pl.BlockSpec((1, hw), lambda bi, h, i: (0, 0)),
                  pl.BlockSpec((None, qb, hw), lambda bi, h, i: (bi, i, h)),
                  far_spec, far_spec, cur_spec, cur_spec,
                  pl.BlockSpec((1, qb, tiles.shape[2]), lambda bi, h, i: (A_HEADS + h, 0, 0))],
        out_specs=pl.BlockSpec((None, qb, hw), lambda bi, h, i: (bi, i, h)),
        out_shape=jax.ShapeDtypeStruct((b, t, B_HEADS * hw), BF16),
        scratch_shapes=[pltpu.VMEM((2, qb, LANES), F32),
                        pltpu.VMEM((2, qb, LANES), F32),
                        pltpu.VMEM((2, qb, LANES), F32),
                        pltpu.VMEM((2, qb, hw), F32)],
        compiler_params=_params(3),
        name="mixer_b",
    )(rel_bias, lam, subln_g.reshape(1, hw), q, *far, *cur, tiles)


def _lam_kernel(dl_ref, o_ref, *, lam_init):
    dl = dl_ref[...]
    s1 = jnp.sum(dl[0:1] * dl[1:2], axis=-1, keepdims=True)
    s2 = jnp.sum(dl[2:3] * dl[3:4], axis=-1, keepdims=True)
    o_ref[...] = jnp.exp(s1) - jnp.exp(s2) + lam_init


def _lam_call(diff_lam, lam_init):
    return pl.pallas_call(
        functools.partial(_lam_kernel, lam_init=lam_init),
        out_shape=jax.ShapeDtypeStruct((1, 1), F32),
        name="diff_lambda",
    )(diff_lam)


def _route(logits_t, rbias, counts):
    tm = logits_t.shape[1]
    per = N_EXPERTS // N_GROUPS
    scores = jax.nn.sigmoid(logits_t).reshape(N_GROUPS, per, tm)
    ch = scores + rbias.reshape(N_GROUPS, per, 1)
    sub = lax.broadcasted_iota(jnp.int32, ch.shape, 1)
    grp = lax.broadcasted_iota(jnp.int32, ch.shape, 0)
    m1 = jnp.max(ch, axis=1, keepdims=True)
    i1 = jnp.min(jnp.where(ch == m1, sub, per), axis=1, keepdims=True)
    m2 = jnp.max(jnp.where(sub == i1, -jnp.inf, ch), axis=1, keepdims=True)
    gs = jnp.broadcast_to(m1 + m2, ch.shape)
    rank = jnp.zeros(ch.shape, jnp.int32)
    for g2 in range(N_GROUPS):
        o = gs[g2:g2 + 1]
        rank = rank + jnp.where(o > gs, 1, 0) + jnp.where(o == gs, 1, 0) * jnp.where(grp > g2, 1, 0)
    cm = jnp.where(rank < TOPK_GROUPS, ch, -jnp.inf)
    eid = grp * per + sub
    erank = jnp.zeros(cm.shape, jnp.int32)
    for e2 in range(N_EXPERTS):
        o = cm[e2 // per:e2 // per + 1, e2 % per:e2 % per + 1, :]
        erank = erank + jnp.where(o > cm, 1, 0) + jnp.where(o == cm, 1, 0) * jnp.where(eid > e2, 1, 0)
    chosen = erank < TOP_K
    sel = jnp.where(chosen, scores, 0.0)
    denom = jnp.sum(jnp.sum(sel, axis=1, keepdims=True), axis=0, keepdims=True)
    gates = sel / denom * ROUTED_SCALE

    ones = jnp.where(chosen, 1.0, 0.0).reshape(N_EXPERTS, tm)
    earlier = (lax.broadcasted_iota(jnp.int32, (tm, tm), 0) < lax.broadcasted_iota(jnp.int32, (tm, tm), 1))
    before = jnp.dot(ones.astype(BF16), jnp.where(earlier, 1.0, 0.0).astype(BF16), preferred_element_type=F32)
    pos = (before + counts.astype(F32)).reshape(N_GROUPS, per, tm)
    new_counts = counts + jnp.sum(ones, axis=1, keepdims=True).astype(jnp.int32)

    def per_slot(values):
        rows = [jnp.sum(jnp.sum(jnp.where(erank == k, values, 0.0), axis=1, keepdims=True), axis=0)
                for k in range(TOP_K)]
        return jnp.concatenate(rows, axis=0)

    return (per_slot(eid.astype(F32)).astype(jnp.int32), per_slot(pos).astype(jnp.int32), per_slot(gates),
            new_counts)


def _pack_pair(a, b):
    ua = pltpu.bitcast(a.astype(BF16).astype(F32), jnp.uint32)
    ub = pltpu.bitcast(b.astype(BF16).astype(F32), jnp.uint32)
    return pltpu.bitcast(ua | (ub >> 16), jnp.int32)


def _unpack_pair(p):
    u = pltpu.bitcast(p, jnp.uint32)
    a = pltpu.bitcast(u & jnp.uint32(0xFFFF0000), F32)
    b = pltpu.bitcast(u << 16, F32)
    return a.astype(BF16), b.astype(BF16)


PARTS = 4
PART_W = D_MODEL // 2 // PARTS


def _pack_rows(x):
    half = x.shape[1] // 2
    return jnp.stack([_pack_pair(x[:, j * PART_W:(j + 1) * PART_W],
                                 x[:, half + j * PART_W:half + (j + 1) * PART_W]) for j in range(PARTS)])


def _packed_matmul(parts_ref, w_ref):
    half = w_ref.shape[0] // 2
    acc = None
    for j in range(PARTS):
        a, b = _unpack_pair(parts_ref[j])
        t = (jnp.dot(a, w_ref[j * PART_W:(j + 1) * PART_W, :], preferred_element_type=F32)
             + jnp.dot(b, w_ref[half + j * PART_W:half + (j + 1) * PART_W, :], preferred_element_type=F32))
        acc = t if acc is None else acc + t
    return acc


def _outproj_kernel(x_ref, oa_ref, ob_ref, ga_ref, sc_ref, sh_ref, g_ref, w_ref, wr_ref, rb_ref, cin_ref,
                    x1_ref, h2_ref, ek_ref, pk_ref, gk_ref, cout_ref, cnt_scr):
    @pl.when(pl.program_id(0) == 0)
    def _():
        cnt_scr[...] = cin_ref[...]

    half = oa_ref.shape[1]
    mix = (jnp.dot(oa_ref[...], w_ref[:half, :], preferred_element_type=F32)
           + jnp.dot(ob_ref[...], w_ref[half:, :], preferred_element_type=F32))
    x1 = x_ref[...] + ga_ref[...] * mix
    x1_ref[...] = x1
    h2 = _modulated_norm(x1, g_ref[...], sc_ref[...], sh_ref[...])
    h2_ref[...] = _pack_rows(h2)
    logits_t = lax.dot_general(wr_ref[...], h2, NT_DIMS, preferred_element_type=F32,
                               precision=lax.Precision.HIGHEST)
    ek, pk, gk, counts = _route(logits_t, rb_ref[...], cnt_scr[...])
    ek_ref[...] = ek
    pk_ref[...] = pk
    gk_ref[...] = gk
    cnt_scr[...] = counts
    cout_ref[...] = counts


def _outproj_call(x2d, oa, ob, ga, sc, sh, mod_map, g, w_bf, wr_t, rbias, counts_in, tm):
    n, d = x2d.shape
    assert n % tm == 0
    r = sc.shape[1]
    mod_spec = pl.BlockSpec((None, r, d), mod_map)
    slot_spec = pl.BlockSpec((TOP_K, tm), lambda i: (0, i))
    return pl.pallas_call(
        _outproj_kernel,
        grid=(n // tm,),
        in_specs=[pl.BlockSpec((tm, d), lambda i: (i, 0)),
                  pl.BlockSpec((tm, oa.shape[1]), lambda i: (i, 0)),
                  pl.BlockSpec((tm, ob.shape[1]), lambda i: (i, 0)),
                  mod_spec, mod_spec, mod_spec,
                  pl.BlockSpec((1, d), lambda i: (0, 0)),
                  _resident(w_bf.shape, lambda i: (0, 0)),
                  _resident(wr_t.shape, lambda i: (0, 0)),
                  pl.BlockSpec((N_EXPERTS, 1), lambda i: (0, 0)),
                  pl.BlockSpec((N_EXPERTS, 1), lambda i: (0, 0))],
        out_specs=[pl.BlockSpec((tm, d), lambda i: (i, 0)),
                   pl.BlockSpec((PARTS, tm, PART_W), lambda i: (0, i, 0)),
                   slot_spec, slot_spec, slot_spec,
                   pl.BlockSpec((N_EXPERTS, 1), lambda i: (0, 0))],
        out_shape=[jax.ShapeDtypeStruct((n, d), F32),
                   jax.ShapeDtypeStruct((PARTS, n, PART_W), jnp.int32),
                   jax.ShapeDtypeStruct((TOP_K, n), jnp.int32),
                   jax.ShapeDtypeStruct((TOP_K, n), jnp.int32),
                   jax.ShapeDtypeStruct((TOP_K, n), F32),
                   jax.ShapeDtypeStruct((N_EXPERTS, 1), jnp.int32)],
        scratch_shapes=[pltpu.VMEM((N_EXPERTS, 1), jnp.int32)],
        compiler_params=_params(1),
        name="out_proj_router",
    )(x2d, oa, ob, ga, sc, sh, g.reshape(1, d), w_bf, wr_t, rbias, counts_in)


SC_WINDOW = 128
EXPERT_TILE = 512


def _sc_mesh():
    return plsc.VectorSubcoreMesh(core_axis_name="core", subcore_axis_name="subcore")


def _sc_scatter_rows(x, idx, n_out):
    n, w = x.shape
    steps = n // SC_WINDOW
    total = idx.shape[1] // SC_WINDOW

    @pl.kernel(out_type=jax.ShapeDtypeStruct((n_out, w), x.dtype), mesh=_sc_mesh(), scratch_types=[],
               name="moe_dispatch")
    def scatter(x_hbm, i_hbm, o_hbm):
        def body(x_vmem, i_vmem):
            pltpu.sync_copy(x_vmem, o_hbm.at[i_vmem.at[0]])

        pltpu.emit_pipeline(
            body, grid=(total,),
            in_specs=[pl.BlockSpec((SC_WINDOW, w), lambda i: (i % steps, 0)),
                      pl.BlockSpec((1, SC_WINDOW), lambda i: (0, i))],
            out_specs=[], core_axis_name=("core", "subcore"),
            dimension_semantics=(pltpu.PARALLEL,))(x_hbm, i_hbm)

    return scatter(x, idx)


def _sc_gather_rows(y, idx):
    w = y.shape[1]
    n = idx.shape[1]

    @pl.kernel(out_type=jax.ShapeDtypeStruct((n, w), y.dtype), mesh=_sc_mesh(), scratch_types=[],
               name="moe_collect")
    def gather(y_hbm, i_hbm, o_hbm):
        def body(i_vmem, o_vmem):
            pltpu.sync_copy(y_hbm.at[i_vmem.at[0]], o_vmem)

        pltpu.emit_pipeline(
            body, grid=(n // SC_WINDOW,),
            in_specs=[pl.BlockSpec((1, SC_WINDOW), lambda i: (0, i))],
            out_specs=[pl.BlockSpec((SC_WINDOW, w), lambda i: (i, 0))],
            core_axis_name=("core", "subcore"),
            dimension_semantics=(pltpu.PARALLEL,))(i_hbm, o_hbm)

    return gather(y, idx)


def _swiglu(parts_ref, wg_ref, wu_ref, wd_ref):
    a = _packed_matmul(parts_ref, wg_ref)
    u = _packed_matmul(parts_ref, wu_ref)
    hid = (a * jax.nn.sigmoid(a) * u).astype(BF16)
    return jnp.dot(hid, wd_ref[...], preferred_element_type=F32)


def _expert_kernel(te_ref, nu_ref, x_ref, wg_ref, wu_ref, wd_ref, y_ref, wg_scr, wu_scr, wd_scr):
    i = pl.program_id(0)

    @pl.when(i < nu_ref[0])
    def _():
        @pl.when((i == 0) | (te_ref[i] != te_ref[jnp.maximum(i - 1, 0)]))
        def _():
            wg_scr[...] = wg_ref[...].astype(BF16)
            wu_scr[...] = wu_ref[...].astype(BF16)
            wd_scr[...] = wd_ref[...].astype(BF16)

        y_ref[...] = _pack_rows(_swiglu(x_ref, wg_scr, wu_scr, wd_scr))


def _expert_call(tile_expert, n_used, x_sorted, wg, wu, wd):
    _, r, _ = x_sorted.shape
    ne, d, de = wg.shape
    n_tiles = r // EXPERT_TILE
    row_map = lambda i, te, nu: (0, jnp.minimum(i, nu[0] - 1), 0)
    return pl.pallas_call(
        _expert_kernel,
        grid_spec=pltpu.PrefetchScalarGridSpec(
            num_scalar_prefetch=2,
            grid=(n_tiles,),
            in_specs=[pl.BlockSpec((PARTS, EXPERT_TILE, PART_W), row_map),
                      pl.BlockSpec((None, d, de), lambda i, te, nu: (te[i], 0, 0)),
                      pl.BlockSpec((None, d, de), lambda i, te, nu: (te[i], 0, 0)),
                      pl.BlockSpec((None, de, d), lambda i, te, nu: (te[i], 0, 0))],
            out_specs=pl.BlockSpec((PARTS, EXPERT_TILE, PART_W), row_map),
            scratch_shapes=[pltpu.VMEM((d, de), BF16), pltpu.VMEM((d, de), BF16), pltpu.VMEM((de, d), BF16)]),
        out_shape=jax.ShapeDtypeStruct(x_sorted.shape, jnp.int32),
        compiler_params=_params(1),
        name="moe_experts",
    )(tile_expert, n_used, x_sorted, wg, wu, wd)


def _final_kernel(x_ref, h_ref, rows_ref, gk_ref, gf_ref, g_ref, sg_ref, su_ref, sd_ref, o_ref):
    moe = _swiglu(h_ref, sg_ref, su_ref, sd_ref)
    half = moe.shape[1] // 2
    lo = [moe[:, j * PART_W:(j + 1) * PART_W] for j in range(PARTS)]
    hi = [moe[:, half + j * PART_W:half + (j + 1) * PART_W] for j in range(PARTS)]
    for k in range(TOP_K):
        gate = gk_ref[:, k:k + 1]
        for j in range(PARTS):
            a, b = _unpack_pair(rows_ref[k, j])
            lo[j] = lo[j] + gate * a.astype(F32)
            hi[j] = hi[j] + gate * b.astype(F32)
    x = x_ref[...] + gf_ref[...] * jnp.concatenate(lo + hi, axis=1)
    o_ref[...] = x * lax.rsqrt(jnp.mean(x * x, axis=-1, keepdims=True) + EPS) * g_ref[...]


def _final_call(x1, h2p, rows, row_block0, gk_t, gf, mod_map, g, sg, su, sd, tm):
    n, d = x1.shape
    assert n % tm == 0
    r = gf.shape[1]
    return pl.pallas_call(
        _final_kernel,
        grid=(n // tm,),
        in_specs=[pl.BlockSpec((tm, d), lambda i: (i, 0)),
                  pl.BlockSpec((PARTS, tm, PART_W), lambda i: (0, i, 0)),
                  pl.BlockSpec((TOP_K, PARTS, tm, PART_W), lambda i: (0, 0, i + row_block0, 0)),
                  pl.BlockSpec((tm, TOP_K), lambda i: (i, 0)),
                  pl.BlockSpec((None, r, d), mod_map),
                  pl.BlockSpec((1, d), lambda i: (0, 0)),
                  _resident(sg.shape, lambda i: (0, 0)),
                  _resident(su.shape, lambda i: (0, 0)),
                  _resident(sd.shape, lambda i: (0, 0))],
        out_specs=pl.BlockSpec((tm, d), lambda i: (i, 0)),
        out_shape=jax.ShapeDtypeStruct((n, d), F32),
        compiler_params=_params(1),
        name="final_norm",
    )(x1, h2p, rows, gk_t, gf, g.reshape(1, d), sg, su, sd)


_QA, _KA, _VA, _QI, _KI, _IW, _QB, _KB, _VB = range(9)
_IN_SIZES = (A_HEADS * A_HEAD_DIM, A_KV_HEADS * A_HEAD_DIM, A_KV_HEADS * A_HEAD_DIM,
             IDX_HEADS * IDX_DIM, IDX_DIM, IDX_HEADS,
             B_HEADS * 2 * B_HEAD_DIM, B_HEADS * 2 * B_HEAD_DIM, B_HEADS * 2 * B_HEAD_DIM)


def _split_w_in(w_in):
    d = w_in.shape[0]
    offs = np.concatenate([[0], np.cumsum(_IN_SIZES)])
    cols = [w_in[:, offs[k]:offs[k + 1]].astype(BF16) for k in range(9)]
    zero = lambda n: jnp.zeros((d, n), BF16)
    qi = jnp.concatenate([cols[_QI].reshape(d, IDX_HEADS, IDX_DIM),
                          jnp.zeros((d, IDX_HEADS, LANES - IDX_DIM), BF16)], axis=-1).reshape(d, IDX_HEADS * LANES)
    w_a = jnp.concatenate([cols[_QA], cols[_KA], cols[_VA], qi,
                           cols[_KI], zero(LANES - IDX_DIM), cols[_IW], zero(LANES - IDX_HEADS)], axis=1)
    w_b = jnp.concatenate([cols[_QB], cols[_KB], cols[_VB]], axis=1)
    return w_a, w_b


def _group_forward(x, mod, past, rel_bias, lam, weights, *, lam_init, qb, qb_b, kb, tm, per_row_mod):
    (norm_a_g, w_a, w_b, subln_g) = weights
    b, t, d = x.shape
    n = b * t
    x2d = x.reshape(n, d)
    sh_a, sc_a = mod[0], mod[1]
    if per_row_mod:
        expand = lambda m: jnp.repeat(m, t, axis=0).reshape(1, n, d)
        mod_map = lambda i: (0, i, 0)
    else:
        expand = lambda m: m.reshape(b, 1, d)
        mod_map = lambda i: ((i * tm) // t, 0, 0)
    sc3, sh3 = expand(sc_a), expand(sh_a)

    na = A_HEADS * A_HEAD_DIM
    nk = A_KV_HEADS * A_HEAD_DIM
    nq = IDX_HEADS * LANES
    c_ka, c_va, c_qi = na, na + nk, na + 2 * nk
    c_ki, c_iw = c_qi + nq, c_qi + nq + LANES
    kv_order = tuple(range(A_KV_HEADS))
    defs_a = [(0, na, na, A_HEAD_DIM ** -0.5, None), (c_ka, nk, nk, None, kv_order), (c_ka, nk, nk, None, None),
              (c_va, nk, nk, None, kv_order), (c_va, nk, nk, None, None), (c_qi, nq, nq, None, None),
              (c_ki, LANES, IDX_DIM, None, None), (c_ki, LANES, LANES, None, None), (c_iw, LANES, LANES, None, None)]
    dt_a = [BF16, F32, BF16, F32, BF16, BF16, F32, BF16, F32]
    qa, ka, ka_bf, va, va_bf, qi, ki, ki_bf, iw = _norm_proj_call(
        x2d, sc3, sh3, mod_map, norm_a_g, w_a, defs_a, dt_a, tm, "in_proj_a")
    nb = B_HEADS * 2 * B_HEAD_DIM
    kb_order = tuple(range(2 * B_HEADS))
    vb_order = tuple(2 * h + half for half in range(2) for h in range(B_HEADS))
    defs_b = [(0, nb, nb, B_HEAD_DIM ** -0.5, None), (nb, nb, nb, None, kb_order), (nb, nb, nb, None, None),
              (2 * nb, nb, nb, None, vb_order), (2 * nb, nb, nb, None, None)]
    dt_b = [BF16, F32, BF16, F32, BF16]
    qbm, kbm, kb_bf, vbm, vb_bf = _norm_proj_call(
        x2d, sc3, sh3, mod_map, norm_a_g, w_b, defs_b, dt_b, tm, "in_proj_b")

    r3 = lambda a: a.reshape(b, t, a.shape[-1])
    cur = [r3(ka_bf), r3(va_bf), r3(ki_bf), r3(kb_bf), r3(vb_bf)]
    kv_pieces = None
    if past is None:
        qoff = 0
        far = cur
    else:
        assert t == qb, "a group with cached keys is one query block per batch row"
        qoff = past[0].shape[1]
        qbr = -(-qb // LANES) * LANES
        cur = [jnp.concatenate([a, jnp.zeros((b, qbr - t, a.shape[-1]), BF16)], axis=1) for a in cur]
        pk, pv, pi, pbk, pbv = past
        if qoff % kb == 0:
            kv_pieces = (A_KV_HEADS, tuple(range(A_KV_HEADS)))
            far = [pk.reshape(b, qoff * A_KV_HEADS, A_HEAD_DIM), pv.reshape(b, qoff * A_KV_HEADS, A_HEAD_DIM),
                   pi.reshape(b, qoff, -1), pbk.reshape(b, qoff, -1), pbv.reshape(b, qoff, -1)]
        else:
            far = [jnp.concatenate([p.reshape(b, qoff, -1), jnp.zeros((b, -qoff % kb, p[0, 0].size), p.dtype)], axis=1)
                   for p in past]
    length = qoff + t
    topk = min(TOPK_MAX, length // 4)

    tiles = _bias_tile_call(rel_bias, qb)
    tiles_b = tiles if qb_b == qb else _bias_tile_call(rel_bias, qb_b)
    out_a = _dsa_call(rel_bias, tiles, r3(qa), r3(qi), r3(iw), far[:3], cur[:3], qb=qb, kb=kb, qoff=qoff, topk=topk,
                      kv_pieces=kv_pieces)
    out_b = _diff_call(rel_bias, tiles_b, lam, subln_g, r3(qbm), far[3:], cur[3:], qb=qb_b, kb=kb, qoff=qoff,
                       lam_init=lam_init)
    rows = (ka.reshape(1, b, t, A_KV_HEADS, A_HEAD_DIM), va.reshape(1, b, t, A_KV_HEADS, A_HEAD_DIM),
            ki.reshape(1, b, t, IDX_DIM), kbm.reshape(1, b, t, B_HEADS, 2, B_HEAD_DIM),
            vbm.reshape(b, t, 2, B_HEADS, B_HEAD_DIM).transpose(0, 1, 3, 2, 4)
               .reshape(1, b, t, B_HEADS, 2 * B_HEAD_DIM))
    return x2d, out_a.reshape(n, -1), out_b.reshape(n, -1), rows, expand, mod_map


def kernel(x_prompt, x_sample, c_prompt, c_sample, cache_a_k, cache_a_v, cache_a_kidx, cache_b_k, cache_b_v,
           rel_bias, w_ada, b_ada, norm_a_g, w_in, w_out, diff_lam, subln_g, norm_f_g, w_router, router_bias,
           w_gate, w_up, w_down, ws_gate, ws_up, ws_down, final_g):
    assert w_ada.shape[0] == 1, "single-layer model"
    d = x_prompt.shape[-1]
    bp, tp, _ = x_prompt.shape
    bs, ts, _ = x_sample.shape
    lam_init = 0.8 - 0.6 * math.exp(-0.3 * 0)

    c_all = jnp.concatenate([c_prompt, c_sample], axis=0)
    rows = -(-c_all.shape[0] // 8) * 8
    c_pad = jnp.concatenate([c_all, jnp.zeros((rows - c_all.shape[0], d), F32)], axis=0)
    mod_all = _ada_call(c_pad, w_ada[0], b_ada[0])
    mod_p = [mod_all[:bp, k * d:(k + 1) * d] for k in range(6)]
    mod_s = [mod_all[bp:bp + bs, k * d:(k + 1) * d] for k in range(6)]

    w_a, w_b = _split_w_in(w_in[0])
    lam = _lam_call(diff_lam[0], lam_init)
    weights = (norm_a_g[0], w_a, w_b, subln_g[0])
    past = (cache_a_k[0], cache_a_v[0], cache_a_kidx[0], cache_b_k[0], cache_b_v[0])

    tm_p = 256
    tm_s = bs * ts
    xp2, oa_p, ob_p, rows_p, exp_p, map_p = _group_forward(
        x_prompt, mod_p, None, rel_bias, lam, weights, lam_init=lam_init, qb=256, qb_b=512, kb=512, tm=tm_p,
        per_row_mod=False)
    xs2, oa_s, ob_s, rows_s, exp_s, map_s = _group_forward(
        x_sample, mod_s, past, rel_bias, lam, weights, lam_init=lam_init, qb=ts, qb_b=ts, kb=512, tm=tm_s,
        per_row_mod=True)

    w_out_bf = w_out[0].astype(BF16)
    wr_t = w_router[0].T
    rb = router_bias[0].reshape(N_EXPERTS, 1)
    no_rows = jnp.zeros((N_EXPERTS, 1), jnp.int32)
    x1_p, h2_p, ek_p, pk_p, gk_p, cnt_p = _outproj_call(
        xp2, oa_p, ob_p, exp_p(mod_p[2]), exp_p(mod_p[4]), exp_p(mod_p[3]), map_p, norm_f_g[0], w_out_bf, wr_t, rb,
        no_rows, tm_p)
    x1_s, h2_s, ek_s, pk_s, gk_s, cnt_all = _outproj_call(
        xs2, oa_s, ob_s, exp_s(mod_s[2]), exp_s(mod_s[4]), exp_s(mod_s[3]), map_s, norm_f_g[0], w_out_bf, wr_t, rb,
        cnt_p, tm_s)

    n_p, n_s = bp * tp, bs * ts
    n_all = n_p + n_s
    n_rows = -(-(n_all * TOP_K) // EXPERT_TILE) * EXPERT_TILE + N_EXPERTS * EXPERT_TILE
    counts = cnt_all[:, 0]
    padded = (counts + EXPERT_TILE - 1) // EXPERT_TILE * EXPERT_TILE
    ends = jnp.cumsum(padded)
    starts = ends - padded
    tile_first_row = jnp.arange(n_rows // EXPERT_TILE, dtype=jnp.int32) * EXPERT_TILE
    tile_expert = jnp.minimum(jnp.sum(ends[None, :] <= tile_first_row[:, None], axis=1), N_EXPERTS - 1).astype(jnp.int32)
    n_used = (ends[-1:] // EXPERT_TILE).astype(jnp.int32)
    ek = jnp.concatenate([ek_p, ek_s], axis=1)
    expert_ids = jnp.arange(N_EXPERTS, dtype=jnp.int32)[:, None, None]
    dest = jnp.sum(jnp.where(ek[None] == expert_ids, starts[:, None, None], 0), axis=0) \
        + jnp.concatenate([pk_p, pk_s], axis=1)
    piece = jnp.arange(PARTS, dtype=jnp.int32)[None, :, None]
    idx = (dest[:, None, :] + piece * n_rows).reshape(1, TOP_K * PARTS * n_all).astype(jnp.int32)

    h2_all = jnp.concatenate([h2_p, h2_s], axis=1).reshape(PARTS * n_all, PART_W)
    x_sorted = _sc_scatter_rows(h2_all, idx, PARTS * n_rows).reshape(PARTS, n_rows, PART_W)
    y_sorted = _expert_call(tile_expert, n_used, x_sorted, w_gate[0], w_up[0], w_down[0])
    rows = _sc_gather_rows(y_sorted.reshape(PARTS * n_rows, PART_W), idx).reshape(TOP_K, PARTS, n_all, PART_W)

    shared = (ws_gate[0].astype(BF16), ws_up[0].astype(BF16), ws_down[0].astype(BF16))
    assert n_p % tm_s == 0
    y_p = _final_call(x1_p, h2_p, rows, 0, gk_p.T, exp_p(mod_p[5]), map_p, final_g, *shared, tm_p)
    y_s = _final_call(x1_s, h2_s, rows, n_p // tm_s, gk_s.T, exp_s(mod_s[5]), map_s, final_g, *shared, tm_s)
    return (y_p.reshape(bp, tp, d), y_s.reshape(bs, ts, d)) + rows_p + rows_s
```

```python
import functools
import math

import numpy as np
import jax
import jax.numpy as jnp
from jax import lax
from jax.experimental import pallas as pl
from jax.experimental.pallas import tpu as pltpu
from jax.experimental.pallas import tpu_sc as plsc

D_MODEL = 2048
CHUNK = 64
A_HEADS = 8
A_KV_HEADS = 2
A_REP = A_HEADS // A_KV_HEADS
A_HEAD_DIM = 128
IDX_HEADS = 16
IDX_DIM = 64
TOPK_MAX = 256
B_HEADS = 4
B_HEAD_DIM = 128
N_REL_BUCKETS = 32
REL_MAX_DIST = 128
N_ATTN_HEADS = A_HEADS + B_HEADS
N_EXPERTS = 64
TOP_K = 8
N_GROUPS = 8
TOPK_GROUPS = 4
EXPERT_DIM = 512
SHARED_DIM = 512
ROUTED_SCALE = 2.5
EPS = 1e-6

LANES = 128
INT_MIN = -(2 ** 31)
NEG_BIG = -1e30
TOP_LOW = 3
VMEM_LIMIT = 52 * 1024 * 1024

F32 = jnp.float32
BF16 = jnp.bfloat16
NT_DIMS = (((1,), (1,)), ((), ()))


def _params(n_axes):
    return pltpu.CompilerParams(dimension_semantics=("arbitrary",) * n_axes,
                                vmem_limit_bytes=VMEM_LIMIT)


def _resident(shape, index_map):
    return pl.BlockSpec(shape, index_map, pipeline_mode=pl.Buffered(1))


def _bucket_thresholds():
    nb = N_REL_BUCKETS // 2
    max_exact = nb // 2
    n = np.arange(max_exact, 4 * REL_MAX_DIST, dtype=np.int32)
    nf = n.astype(np.float32)
    large = max_exact + (np.log(nf / np.float32(max_exact)) / np.float32(math.log(REL_MAX_DIST / max_exact))
                         * np.float32(nb - max_exact)).astype(np.int32)
    large = np.minimum(large, nb - 1)
    assert np.all(np.diff(large) >= 0)
    ths = [int(n[np.argmax(large >= b)]) for b in range(max_exact + 1, nb)]
    return max_exact, nb, ths


_MAX_EXACT, _NB, _BUCKET_THS = _bucket_thresholds()
assert _BUCKET_THS[-1] <= LANES + 1
FAR_BUCKET = _NB - 1


def _ada_kernel(c_ref, w_ref, b_ref, o_ref):
    c = c_ref[...]
    s = c * jax.nn.sigmoid(c)
    o_ref[...] = jnp.dot(s, w_ref[...], preferred_element_type=F32,
                         precision=lax.Precision.HIGHEST) + b_ref[...]


def _ada_call(c, w_ada, b_ada):
    rows, d = c.shape
    n = w_ada.shape[1]
    tn = 1536
    assert n % tn == 0
    return pl.pallas_call(
        _ada_kernel,
        grid=(n // tn,),
        in_specs=[pl.BlockSpec((rows, d), lambda j: (0, 0)),
                  pl.BlockSpec((d, tn), lambda j: (0, j)),
                  pl.BlockSpec((1, tn), lambda j: (0, j))],
        out_specs=pl.BlockSpec((rows, tn), lambda j: (0, j)),
        out_shape=jax.ShapeDtypeStruct((rows, n), F32),
        compiler_params=_params(1),
        name="ada_mod",
    )(c, w_ada, b_ada.reshape(1, n))


def _modulated_norm(x, g, sc, sh):
    y = x * lax.rsqrt(jnp.mean(x * x, axis=-1, keepdims=True) + EPS) * g
    return y * (1.0 + sc) + sh


def _norm_proj_kernel(x_ref, sc_ref, sh_ref, g_ref, w_ref, *out_refs, defs):
    h = _modulated_norm(x_ref[...], g_ref[...], sc_ref[...], sh_ref[...]).astype(BF16)
    done = {}
    tm = x_ref.shape[0]
    for (start, width, store_w, mult, order), o_ref in zip(defs, out_refs):
        if (start, width) not in done:
            done[(start, width)] = jnp.dot(h, w_ref[:, start:start + width], preferred_element_type=F32)
        r = done[(start, width)][:, :store_w]
        if order is None:
            o_ref[...] = (r if mult is None else r * mult).astype(o_ref.dtype)
        else:
            for p, cb in enumerate(order):
                o_ref[pl.ds(p, tm, stride=len(order)), :] = r[:, cb * LANES:(cb + 1) * LANES].astype(o_ref.dtype)


def _norm_proj_call(x2d, sc, sh, mod_map, g, w_bf, defs, dtypes, tm, name):
    n, d = x2d.shape
    assert n % tm == 0
    r = sc.shape[1]
    shape = lambda rows, d_: (rows, d_[2]) if d_[4] is None else (rows * len(d_[4]), LANES)
    out_shapes = [jax.ShapeDtypeStruct(shape(n, d_), dt) for d_, dt in zip(defs, dtypes)]
    out_specs = [pl.BlockSpec(shape(tm, d_), lambda i: (i, 0)) for d_ in defs]
    return pl.pallas_call(
        functools.partial(_norm_proj_kernel, defs=tuple(defs)),
        grid=(n // tm,),
        in_specs=[pl.BlockSpec((tm, d), lambda i: (i, 0)),
                  pl.BlockSpec((None, r, d), mod_map),
                  pl.BlockSpec((None, r, d), mod_map),
                  pl.BlockSpec((1, d), lambda i: (0, 0)),
                  _resident(w_bf.shape, lambda i: (0, 0))],
        out_specs=out_specs,
        out_shape=out_shapes,
        compiler_params=_params(1),
        name=name,
    )(x2d, sc, sh, g.reshape(1, d), w_bf)


def _bias_tile_kernel(tab_ref, o_ref, *, qb):
    _, rows, width = o_ref.shape
    r = lax.broadcasted_iota(jnp.int32, (rows, width), 0)
    koff = lax.broadcasted_iota(jnp.int32, (rows, width), 1) - LANES
    rel = koff - r
    n = jnp.abs(rel)
    large = jnp.full((rows, width), _MAX_EXACT, jnp.int32)
    for th in _BUCKET_THS:
        large = large + jnp.where(n >= th, 1, 0)
    bucket = jnp.where(rel > 0, _NB, 0) + jnp.where(n < _MAX_EXACT, n, large)
    visible = ((koff >> 6) <= (r >> 6)) & (koff < qb)
    for h in range(N_ATTN_HEADS):
        b = jnp.zeros((rows, width), F32)
        for bk in range(N_REL_BUCKETS):
            b = jnp.where(bucket == bk, tab_ref[bk, h], b)
        o_ref[h] = jnp.where(visible, b, NEG_BIG)


def _bias_tile_call(rel_bias, qb):
    qbr = -(-qb // LANES) * LANES
    return pl.pallas_call(
        functools.partial(_bias_tile_kernel, qb=qb),
        in_specs=[pl.BlockSpec(memory_space=pltpu.SMEM)],
        out_specs=pl.BlockSpec(memory_space=pltpu.VMEM),
        out_shape=jax.ShapeDtypeStruct((N_ATTN_HEADS, qb, LANES + qbr), F32),
        name="bias_tiles",
    )(rel_bias)


def _sweep_far(q0, qb, kb, fn):
    far_end = jnp.maximum(q0 - LANES, 0)
    n_full = far_end // kb
    sub = kb // LANES

    def whole(j):
        fn(pl.multiple_of(j * kb, kb), j * sub, None)

    def pair(j, carry):
        whole(2 * j)
        whole(2 * j + 1)
        return carry

    lax.fori_loop(0, n_full // 2, pair, 0)
    pl.when(n_full % 2 == 1)(lambda: whole(n_full - 1))

    @pl.when(far_end > n_full * kb)
    def _():
        start = pl.multiple_of(n_full * kb, kb)
        col = start + lax.broadcasted_iota(jnp.int32, (qb, LANES), 1)
        fn(start, n_full * sub, [jnp.where(col + c * LANES < far_end, 0.0, NEG_BIG) for c in range(sub)])


def _key_rows(ref, start, width, pieces=None):
    if pieces is None:
        blk = ref[pl.ds(start, width), :].astype(BF16)
        pad = -blk.shape[1] % LANES
        return blk if pad == 0 else jnp.concatenate([blk, jnp.zeros((width, pad), BF16)], axis=1)
    r, rows = pieces
    window = ref.at[pl.ds(pl.multiple_of(start * r, LANES * r), width * r), :]
    return jnp.concatenate([window[pl.ds(p, width, stride=r), :].astype(BF16) for p in rows], axis=1)


def _dsa_kernel(tab_ref, q_ref, qi_ref, iw_ref, k_ref, v_ref, ki_ref, kc_ref, vc_ref, kic_ref, tile_ref, o_ref,
                keys_scr, half_scr, best_scr, thr_scr, mfar_scr, mnear_scr, l_scr, acc_scr, *, qb, kb, qoff, topk,
                kv_pieces):
    far_k = lambda start, width: _key_rows(k_ref, start, width, kv_pieces)
    far_v = lambda start, width: _key_rows(v_ref, start, width, kv_pieces)
    i = pl.program_id(1)
    q0 = qoff + i * qb
    qbr = tile_ref.shape[2] - LANES
    sub = kb // LANES
    w = iw_ref[:, :IDX_HEADS] * (IDX_HEADS ** -0.5 * IDX_DIM ** -0.5)

    def score_keys(kib):
        width = kib.shape[0]
        acc = jnp.zeros((qb, width), F32)
        for h in range(IDX_HEADS):
            d = lax.dot_general(qi_ref[:, h * LANES:(h + 1) * LANES], kib, NT_DIMS,
                                preferred_element_type=F32)
            acc = acc + w[:, h:h + 1] * jnp.maximum(d, 0.0)
        bits = pltpu.bitcast(acc, jnp.int32)
        return bits ^ ((bits >> 31) & 0x7FFFFFFF)

    n_score = (q0 + kb - 1) // kb

    def store_keys(slab0, key):
        for c in range(key.shape[1] // LANES):
            part = key[:, c * LANES:(c + 1) * LANES]
            keys_scr[slab0 + c] = part
            half_scr[slab0 + c] = (part >> 16).astype(jnp.int16)

    def score_body(j, carry):
        store_keys(j * sub, score_keys(_key_rows(ki_ref, pl.multiple_of(j * kb, kb), kb)))
        return carry

    lax.fori_loop(0, n_score, score_body, 0)

    cur_blk = q0 // LANES
    key = score_keys(kic_ref[...])
    store_keys(cur_blk, jnp.where(tile_ref[0, :, LANES:] > 0.5 * NEG_BIG, key, INT_MIN))
    n_blk = cur_blk + qbr // LANES

    rb = min(qb, LANES)
    lo16 = -(2 ** 15)
    n4 = n_blk // 4

    chunks = [pl.ds(c * rb, rb) for c in range(qb // rb)]
    ones = jnp.ones((LANES, LANES), BF16)

    def count_ge(ts):
        tbs = [t.astype(jnp.int16) for t in ts]

        def count(j, cnts):
            return tuple(cnt + jnp.where(half_scr[j, rows, :] >= tb, jnp.int16(1), jnp.int16(0))
                         for cnt, rows, tb in zip(cnts, chunks, tbs))

        def body4(j, cnts):
            for u in range(4):
                cnts = count(4 * j + u, cnts)
            return cnts

        cnts = lax.fori_loop(0, n4, body4, tuple(jnp.zeros((rb, LANES), jnp.int16) for _ in chunks))
        cnts = lax.fori_loop(4 * n4, n_blk, count, cnts)
        return [jnp.dot(cnt.astype(F32).astype(BF16), ones, preferred_element_type=F32) for cnt in cnts]

    def count_ge_best(ts):
        cnts = [sum(jnp.where(best_scr[u, rows, :] >= t.astype(jnp.int16), jnp.int16(1), jnp.int16(0))
                    for u in range(TOP_LOW)) for rows, t in zip(chunks, ts)]
        return [jnp.dot(cnt.astype(F32).astype(BF16), ones, preferred_element_type=F32) for cnt in cnts]

    def search(needed, counter):
        def bit_body(it, ts):
            cands = [t + (jnp.int32(1) << (15 - it)) for t in ts]
            return tuple(jnp.where(c >= n, cand, t) for c, n, cand, t in zip(counter(cands), needed, cands, ts))

        return lax.fori_loop(0, 16, bit_body, tuple(jnp.full((rb, LANES), lo16, jnp.int32) for _ in chunks))

    t_hi = search([float(topk)] * len(chunks), count_ge)
    top = -lo16 - 1
    above = [jnp.where(t == top, 0.0, c)
             for t, c in zip(t_hi, count_ge([jnp.minimum(t + 1, top) for t in t_hi]))]

    def low_plane(rows, tb):
        def body(j, carry):
            best, spill = carry[:TOP_LOW], carry[TOP_LOW]
            k32 = keys_scr[j, rows, :]
            a = jnp.where((k32 >> 16) == tb, (k32 & 0xFFFF) + lo16, lo16).astype(jnp.int16)
            half_scr[j, rows, :] = a
            kept = []
            for b in best:
                keep_b = b >= a
                kept.append(jnp.where(keep_b, b, a))
                a = jnp.where(keep_b, a, b)
            return tuple(kept) + (jnp.where(spill >= a, spill, a),)

        lowest = jnp.full((rb, LANES), lo16, jnp.int16)
        out = lax.fori_loop(0, n_blk, body, (lowest,) * (TOP_LOW + 1))
        for u in range(TOP_LOW):
            best_scr[u, rows, :] = out[u]
        return jnp.max(out[TOP_LOW].astype(jnp.int32))

    spilled = functools.reduce(jnp.maximum, [low_plane(rows, tb) for rows, tb in zip(chunks, t_hi)])
    needed = [topk - a for a in above]
    t_lo = lax.cond(spilled > lo16, lambda: search(needed, count_ge), lambda: search(needed, count_ge_best))
    for rows, th, tl in zip(chunks, t_hi, t_lo):
        thr_scr[rows, :] = jnp.maximum(th * 65536 + (tl - lo16), INT_MIN + 1)

    def count32(pred):
        def count(j, cnts):
            return tuple(cnt + jnp.where(pred(keys_scr[j, rows, :], thr_scr[rows, :]), 1, 0)
                         for cnt, rows in zip(cnts, chunks))

        cnts = lax.fori_loop(0, n_blk, count, tuple(jnp.zeros((rb, LANES), jnp.int32) for _ in chunks))
        return [jnp.dot(cnt.astype(F32).astype(BF16), ones, preferred_element_type=F32) for cnt in cnts]

    surplus = functools.reduce(jnp.maximum, [jnp.max(c) for c in count32(lambda k, t: k >= t)]) - topk

    @pl.when(surplus > 0)
    def _():
        keep = [topk - c for c in count32(lambda k, t: k > t)]
        upto = jnp.where(lax.broadcasted_iota(jnp.int32, (LANES, LANES), 0)
                         <= lax.broadcasted_iota(jnp.int32, (LANES, LANES), 1), 1.0, 0.0).astype(BF16)

        def demote(j, seen):
            out = []
            for rows, kp, sn in zip(chunks, keep, seen):
                k32 = keys_scr[j, rows, :]
                tied = k32 == thr_scr[rows, :]
                tied_bf = jnp.where(tied, 1.0, 0.0).astype(BF16)
                rank = sn + jnp.dot(tied_bf, upto, preferred_element_type=F32)
                keys_scr[j, rows, :] = k32 - jnp.where(tied & (rank > kp), 1, 0)
                out.append(sn + jnp.dot(tied_bf, ones, preferred_element_type=F32))
            return tuple(out)

        lax.fori_loop(0, n_blk, demote, tuple(jnp.zeros((rb, LANES), F32) for _ in chunks))

    hd = A_HEAD_DIM

    def select_mask(blk0, n):
        return [jnp.where(keys_scr[blk0 + c] >= thr_scr[...], 0.0, NEG_BIG) for c in range(n)]

    def logits(h, kblk):
        g = h // A_REP
        s = lax.dot_general(q_ref[:, h * hd:(h + 1) * hd], kblk[:, g * hd:(g + 1) * hd], NT_DIMS,
                            preferred_element_type=F32)
        return [s[:, c * LANES:(c + 1) * LANES] for c in range(s.shape[1] // LANES)]

    def tile_cols(h, col0, n):
        return [tile_ref[h, :, col0 + c * LANES:col0 + (c + 1) * LANES] for c in range(n)]

    mfar_scr[...] = jnp.full(mfar_scr.shape, NEG_BIG, F32)
    mnear_scr[...] = jnp.full(mnear_scr.shape, NEG_BIG, F32)

    def far_mask(blk0, colmask):
        am = select_mask(blk0, sub)
        return am if colmask is None else [a + c for a, c in zip(am, colmask)]

    def max_far(start, blk0, colmask):
        kblk = far_k(start, kb)
        am = far_mask(blk0, colmask)
        for h in range(A_HEADS):
            mfar_scr[h] = functools.reduce(
                jnp.maximum, [s + a for s, a in zip(logits(h, kblk), am)], mfar_scr[h])

    def max_near(kblk, blk0, col0):
        n = kblk.shape[0] // LANES
        am = select_mask(blk0, n)
        for h in range(A_HEADS):
            mnear_scr[h] = functools.reduce(
                jnp.maximum, [s + a + b for s, a, b in zip(logits(h, kblk), am, tile_cols(h, col0, n))],
                mnear_scr[h])

    prev_start = pl.multiple_of(q0 - LANES, LANES)
    _sweep_far(q0, qb, kb, max_far)
    pl.when(q0 > 0)(lambda: max_near(far_k(prev_start, LANES), cur_blk - 1, 0))
    max_near(kc_ref[...], cur_blk, LANES)

    for h in range(A_HEADS):
        fb = tab_ref[FAR_BUCKET, h]
        m = jnp.maximum(jnp.max(mfar_scr[h], axis=-1, keepdims=True) + fb,
                        jnp.max(mnear_scr[h], axis=-1, keepdims=True))
        mfar_scr[h] = jnp.broadcast_to(fb - m, (qb, LANES))
        mnear_scr[h] = jnp.broadcast_to(-m, (qb, LANES))
    l_scr[...] = jnp.zeros(l_scr.shape, F32)
    acc_scr[...] = jnp.zeros(acc_scr.shape, F32)

    def accumulate(h, p, vblk):
        g = h // A_REP
        l_scr[h] += functools.reduce(jnp.add, p)
        acc_scr[h] += jnp.dot(jnp.concatenate(p, axis=1).astype(BF16), vblk[:, g * hd:(g + 1) * hd],
                              preferred_element_type=F32)

    def acc_far(start, blk0, colmask):
        kblk = far_k(start, kb)
        vblk = far_v(start, kb)
        am = far_mask(blk0, colmask)
        for h in range(A_HEADS):
            shift = mfar_scr[h]
            accumulate(h, [jnp.exp(s + a + shift) for s, a in zip(logits(h, kblk), am)], vblk)

    def acc_near(kblk, vblk, blk0, col0):
        n = kblk.shape[0] // LANES
        am = select_mask(blk0, n)
        for h in range(A_HEADS):
            shift = mnear_scr[h]
            accumulate(h, [jnp.exp(s + a + b + shift) for s, a, b in
                           zip(logits(h, kblk), am, tile_cols(h, col0, n))], vblk)

    _sweep_far(q0, qb, kb, acc_far)
    pl.when(q0 > 0)(lambda: acc_near(far_k(prev_start, LANES), far_v(prev_start, LANES),
                                     cur_blk - 1, 0))
    acc_near(kc_ref[...], vc_ref[...], cur_blk, LANES)

    for h in range(A_HEADS):
        o_ref[:, h * hd:(h + 1) * hd] = (
            acc_scr[h] / jnp.sum(l_scr[h], axis=-1, keepdims=True)).astype(o_ref.dtype)


def _dsa_call(rel_bias, tiles, q, qi, iw, far, cur, *, qb, kb, qoff, topk, kv_pieces=None):
    b, t, _ = q.shape
    lf = far[2].shape[1]
    qbr = tiles.shape[2] - LANES
    q0_max = qoff + t - qb
    far_end = -(-q0_max // kb) * kb
    assert t % qb == 0 and qoff % LANES == 0 and far_end <= lf and cur[0].shape[1] == (t // qb) * qbr
    n_slabs = max(far_end, q0_max + qbr) // LANES
    mode = {"pipeline_mode": pl.Buffered(1)} if t // qb > 1 else {}
    far_spec = lambda a: pl.BlockSpec((None,) + a.shape[1:], lambda bi, i: (bi, 0, 0), **mode)
    cur_spec = lambda a: pl.BlockSpec((None, qbr, a.shape[2]), lambda bi, i: (bi, i, 0))
    return pl.pallas_call(
        functools.partial(_dsa_kernel, qb=qb, kb=kb, qoff=qoff, topk=topk, kv_pieces=kv_pieces),
        grid=(b, t // qb),
        in_specs=[pl.BlockSpec(memory_space=pltpu.SMEM),
                  pl.BlockSpec((None, qb, q.shape[2]), lambda bi, i: (bi, i, 0)),
                  pl.BlockSpec((None, qb, qi.shape[2]), lambda bi, i: (bi, i, 0)),
                  pl.BlockSpec((None, qb, iw.shape[2]), lambda bi, i: (bi, i, 0)),
                  far_spec(far[0]), far_spec(far[1]), far_spec(far[2]),
                  cur_spec(cur[0]), cur_spec(cur[1]), cur_spec(cur[2]),
                  _resident(tiles.shape, lambda bi, i: (0, 0, 0))],
        out_specs=pl.BlockSpec((None, qb, A_HEADS * A_HEAD_DIM), lambda bi, i: (bi, i, 0)),
        out_shape=jax.ShapeDtypeStruct((b, t, A_HEADS * A_HEAD_DIM), BF16),
        scratch_shapes=[pltpu.VMEM((n_slabs, qb, LANES), jnp.int32),
                        pltpu.VMEM((n_slabs, qb, LANES), jnp.int16),
                        pltpu.VMEM((TOP_LOW, qb, LANES), jnp.int16),
                        pltpu.VMEM((qb, LANES), jnp.int32),
                        pltpu.VMEM((A_HEADS, qb, LANES), F32),
                        pltpu.VMEM((A_HEADS, qb, LANES), F32),
                        pltpu.VMEM((A_HEADS, qb, LANES), F32),
                        pltpu.VMEM((A_HEADS, qb, A_HEAD_DIM), F32)],
        compiler_params=_params(2),
        name="mixer_a",
    )(rel_bias, q, qi, iw, *far, *cur, tiles)


def _diff_kernel(tab_ref, lam_ref, sg_ref, q_ref, k_ref, v_ref, kc_ref, vc_ref, tile_ref, o_ref,
                 mfar_scr, mnear_scr, l_scr, acc_scr, *, qb, kb, qoff, lam_init):
    head = pl.program_id(1)
    i = pl.program_id(2)
    q0 = qoff + i * qb
    hd = B_HEAD_DIM
    far_bias = tab_ref[FAR_BUCKET, A_HEADS + head]
    far_k = lambda start, width: _key_rows(k_ref, start, width)
    far_v = lambda start, width: _key_rows(v_ref, start, width)

    def logits(c, kblk):
        s = lax.dot_general(q_ref[:, c * hd:(c + 1) * hd], kblk[:, c * hd:(c + 1) * hd], NT_DIMS,
                            preferred_element_type=F32)
        return [s[:, u * LANES:(u + 1) * LANES] for u in range(s.shape[1] // LANES)]

    def tile_cols(col0, n):
        return [tile_ref[0, :, col0 + u * LANES:col0 + (u + 1) * LANES] for u in range(n)]

    mfar_scr[...] = jnp.full(mfar_scr.shape, NEG_BIG, F32)
    mnear_scr[...] = jnp.full(mnear_scr.shape, NEG_BIG, F32)

    def max_far(start, blk0, colmask):
        kblk = far_k(start, kb)
        for c in range(2):
            s = logits(c, kblk)
            if colmask is not None:
                s = [x + m for x, m in zip(s, colmask)]
            mfar_scr[c] = functools.reduce(jnp.maximum, s, mfar_scr[c])

    def max_near(kblk, col0):
        for c in range(2):
            mnear_scr[c] = functools.reduce(
                jnp.maximum, [s + b for s, b in zip(logits(c, kblk), tile_cols(col0, kblk.shape[0] // LANES))],
                mnear_scr[c])

    prev_start = pl.multiple_of(q0 - LANES, LANES)
    _sweep_far(q0, qb, kb, max_far)
    pl.when(q0 > 0)(lambda: max_near(far_k(prev_start, LANES), 0))
    max_near(kc_ref[...], LANES)

    for c in range(2):
        m = jnp.maximum(jnp.max(mfar_scr[c], axis=-1, keepdims=True) + far_bias,
                        jnp.max(mnear_scr[c], axis=-1, keepdims=True))
        mfar_scr[c] = jnp.broadcast_to(far_bias - m, (qb, LANES))
        mnear_scr[c] = jnp.broadcast_to(-m, (qb, LANES))
    l_scr[...] = jnp.zeros(l_scr.shape, F32)
    acc_scr[...] = jnp.zeros(acc_scr.shape, F32)

    def accumulate(c, p, vblk):
        l_scr[c] += functools.reduce(jnp.add, p)
        acc_scr[c] += jnp.dot(jnp.concatenate(p, axis=1).astype(BF16), vblk, preferred_element_type=F32)

    def acc_far(start, blk0, colmask):
        kblk = far_k(start, kb)
        vblk = far_v(start, kb)
        for c in range(2):
            shift = mfar_scr[c]
            shifts = [shift] * (kb // LANES) if colmask is None else [shift + m for m in colmask]
            accumulate(c, [jnp.exp(s + sh) for s, sh in zip(logits(c, kblk), shifts)], vblk)

    def acc_near(kblk, vblk, col0):
        for c in range(2):
            shift = mnear_scr[c]
            accumulate(c, [jnp.exp(s + b + shift) for s, b in
                           zip(logits(c, kblk), tile_cols(col0, kblk.shape[0] // LANES))], vblk)

    _sweep_far(q0, qb, kb, acc_far)
    pl.when(q0 > 0)(lambda: acc_near(far_k(prev_start, LANES), far_v(prev_start, LANES), 0))
    acc_near(kc_ref[...], vc_ref[...], LANES)

    lam = lam_ref[0, 0]
    o = (acc_scr[0] / jnp.sum(l_scr[0], axis=-1, keepdims=True)
         - lam * (acc_scr[1] / jnp.sum(l_scr[1], axis=-1, keepdims=True)))
    o = o * lax.rsqrt(jnp.mean(o * o, axis=-1, keepdims=True) + EPS) * sg_ref[...]
    o_ref[...] = (o * (1.0 - lam_init)).astype(o_ref.dtype)


def _diff_call(rel_bias, tiles, lam, subln_g, q, far, cur, *, qb, kb, qoff, lam_init):
    b, t, _ = q.shape
    lf = far[0].shape[1]
    hw = 2 * B_HEAD_DIM
    qbr = tiles.shape[2] - LANES
    q0_max = qoff + t - qb
    far_end = -(-max(q0_max - LANES, 0) // kb) * kb
    assert t % qb == 0 and qoff % LANES == 0 and far_end <= lf and cur[0].shape[1] == (t // qb) * qbr
    far_spec = pl.BlockSpec((None, lf, hw), lambda bi, h, i: (bi, 0, h))
    cur_spec = pl.BlockSpec((None, qbr, hw), lambda bi, h, i: (bi, i, h))
    return pl.pallas_call(
        functools.partial(_diff_kernel, qb=qb, kb=kb, qoff=qoff, lam_init=lam_init),
        grid=(b, B_HEADS, t // qb),
        in_specs=[pl.BlockSpec(memory_space=pltpu.SMEM),
                  pl.BlockSpec(memory_space=pltpu.SMEM),
                  pl.BlockSpec((1, hw), lambda bi, h, i: (0, 0)),
                  pl.BlockSpec((None, qb, hw), lambda bi, h, i: (bi, i, h)),
                  far_spec, far_spec, cur_spec, cur_spec,
                  pl.BlockSpec((1, qb, tiles.shape[2]), lambda bi, h, i: (A_HEADS + h, 0, 0))],
        out_specs=pl.BlockSpec((None, qb, hw), lambda bi, h, i: (bi, i, h)),
        out_shape=jax.ShapeDtypeStruct((b, t, B_HEADS * hw), BF16),
        scratch_shapes=[pltpu.VMEM((2, qb, LANES), F32),
                        pltpu.VMEM((2, qb, LANES), F32),
                        pltpu.VMEM((2, qb, LANES), F32),
                        pltpu.VMEM((2, qb, hw), F32)],
        compiler_params=_params(3),
        name="mixer_b",
    )(rel_bias, lam, subln_g.reshape(1, hw), q, *far, *cur, tiles)


def _lam_kernel(dl_ref, o_ref, *, lam_init):
    dl = dl_ref[...]
    s1 = jnp.sum(dl[0:1] * dl[1:2], axis=-1, keepdims=True)
    s2 = jnp.sum(dl[2:3] * dl[3:4], axis=-1, keepdims=True)
    o_ref[...] = jnp.exp(s1) - jnp.exp(s2) + lam_init


def _lam_call(diff_lam, lam_init):
    return pl.pallas_call(
        functools.partial(_lam_kernel, lam_init=lam_init),
        out_shape=jax.ShapeDtypeStruct((1, 1), F32),
        name="diff_lambda",
    )(diff_lam)


def _route(logits_t, rbias, counts):
    tm = logits_t.shape[1]
    per = N_EXPERTS // N_GROUPS
    scores = jax.nn.sigmoid(logits_t).reshape(N_GROUPS, per, tm)
    ch = scores + rbias.reshape(N_GROUPS, per, 1)
    sub = lax.broadcasted_iota(jnp.int32, ch.shape, 1)
    grp = lax.broadcasted_iota(jnp.int32, ch.shape, 0)
    m1 = jnp.max(ch, axis=1, keepdims=True)
    i1 = jnp.min(jnp.where(ch == m1, sub, per), axis=1, keepdims=True)
    m2 = jnp.max(jnp.where(sub == i1, -jnp.inf, ch), axis=1, keepdims=True)
    gs = jnp.broadcast_to(m1 + m2, ch.shape)
    rank = jnp.zeros(ch.shape, jnp.int32)
    for g2 in range(N_GROUPS):
        o = gs[g2:g2 + 1]
        rank = rank + jnp.where(o > gs, 1, 0) + jnp.where(o == gs, 1, 0) * jnp.where(grp > g2, 1, 0)
    cm = jnp.where(rank < TOPK_GROUPS, ch, -jnp.inf)
    eid = grp * per + sub
    erank = jnp.zeros(cm.shape, jnp.int32)
    for e2 in range(N_EXPERTS):
        o = cm[e2 // per:e2 // per + 1, e2 % per:e2 % per + 1, :]
        erank = erank + jnp.where(o > cm, 1, 0) + jnp.where(o == cm, 1, 0) * jnp.where(eid > e2, 1, 0)
    chosen = erank < TOP_K
    sel = jnp.where(chosen, scores, 0.0)
    denom = jnp.sum(jnp.sum(sel, axis=1, keepdims=True), axis=0, keepdims=True)
    gates = sel / denom * ROUTED_SCALE

    ones = jnp.where(chosen, 1.0, 0.0).reshape(N_EXPERTS, tm)
    earlier = (lax.broadcasted_iota(jnp.int32, (tm, tm), 0) < lax.broadcasted_iota(jnp.int32, (tm, tm), 1))
    before = jnp.dot(ones.astype(BF16), jnp.where(earlier, 1.0, 0.0).astype(BF16), preferred_element_type=F32)
    pos = (before + counts.astype(F32)).reshape(N_GROUPS, per, tm)
    new_counts = counts + jnp.sum(ones, axis=1, keepdims=True).astype(jnp.int32)

    def per_slot(values):
        rows = [jnp.sum(jnp.sum(jnp.where(erank == k, values, 0.0), axis=1, keepdims=True), axis=0)
                for k in range(TOP_K)]
        return jnp.concatenate(rows, axis=0)

    return (per_slot(eid.astype(F32)).astype(jnp.int32), per_slot(pos).astype(jnp.int32), per_slot(gates),
            new_counts)


def _pack_pair(a, b):
    ua = pltpu.bitcast(a.astype(BF16).astype(F32), jnp.uint32)
    ub = pltpu.bitcast(b.astype(BF16).astype(F32), jnp.uint32)
    return pltpu.bitcast(ua | (ub >> 16), jnp.int32)


def _unpack_pair(p):
    u = pltpu.bitcast(p, jnp.uint32)
    a = pltpu.bitcast(u & jnp.uint32(0xFFFF0000), F32)
    b = pltpu.bitcast(u << 16, F32)
    return a.astype(BF16), b.astype(BF16)


PARTS = 4
PART_W = D_MODEL // 2 // PARTS


def _pack_rows(x):
    half = x.shape[1] // 2
    return jnp.stack([_pack_pair(x[:, j * PART_W:(j + 1) * PART_W],
                                 x[:, half + j * PART_W:half + (j + 1) * PART_W]) for j in range(PARTS)])


def _packed_matmul(parts_ref, w_ref):
    half = w_ref.shape[0] // 2
    acc = None
    for j in range(PARTS):
        a, b = _unpack_pair(parts_ref[j])
        t = (jnp.dot(a, w_ref[j * PART_W:(j + 1) * PART_W, :], preferred_element_type=F32)
             + jnp.dot(b, w_ref[half + j * PART_W:half + (j + 1) * PART_W, :], preferred_element_type=F32))
        acc = t if acc is None else acc + t
    return acc


def _outproj_kernel(x_ref, oa_ref, ob_ref, ga_ref, sc_ref, sh_ref, g_ref, w_ref, wr_ref, rb_ref, cin_ref,
                    x1_ref, h2_ref, ek_ref, pk_ref, gk_ref, cout_ref, cnt_scr):
    @pl.when(pl.program_id(0) == 0)
    def _():
        cnt_scr[...] = cin_ref[...]

    half = oa_ref.shape[1]
    mix = (jnp.dot(oa_ref[...], w_ref[:half, :], preferred_element_type=F32)
           + jnp.dot(ob_ref[...], w_ref[half:, :], preferred_element_type=F32))
    x1 = x_ref[...] + ga_ref[...] * mix
    x1_ref[...] = x1
    h2 = _modulated_norm(x1, g_ref[...], sc_ref[...], sh_ref[...])
    h2_ref[...] = _pack_rows(h2)
    logits_t = lax.dot_general(wr_ref[...], h2, NT_DIMS, preferred_element_type=F32,
                               precision=lax.Precision.HIGHEST)
    ek, pk, gk, counts = _route(logits_t, rb_ref[...], cnt_scr[...])
    ek_ref[...] = ek
    pk_ref[...] = pk
    gk_ref[...] = gk
    cnt_scr[...] = counts
    cout_ref[...] = counts


def _outproj_call(x2d, oa, ob, ga, sc, sh, mod_map, g, w_bf, wr_t, rbias, counts_in, tm):
    n, d = x2d.shape
    assert n % tm == 0
    r = sc.shape[1]
    mod_spec = pl.BlockSpec((None, r, d), mod_map)
    slot_spec = pl.BlockSpec((TOP_K, tm), lambda i: (0, i))
    return pl.pallas_call(
        _outproj_kernel,
        grid=(n // tm,),
        in_specs=[pl.BlockSpec((tm, d), lambda i: (i, 0)),
                  pl.BlockSpec((tm, oa.shape[1]), lambda i: (i, 0)),
                  pl.BlockSpec((tm, ob.shape[1]), lambda i: (i, 0)),
                  mod_spec, mod_spec, mod_spec,
                  pl.BlockSpec((1, d), lambda i: (0, 0)),
                  _resident(w_bf.shape, lambda i: (0, 0)),
                  _resident(wr_t.shape, lambda i: (0, 0)),
                  pl.BlockSpec((N_EXPERTS, 1), lambda i: (0, 0)),
                  pl.BlockSpec((N_EXPERTS, 1), lambda i: (0, 0))],
        out_specs=[pl.BlockSpec((tm, d), lambda i: (i, 0)),
                   pl.BlockSpec((PARTS, tm, PART_W), lambda i: (0, i, 0)),
                   slot_spec, slot_spec, slot_spec,
                   pl.BlockSpec((N_EXPERTS, 1), lambda i: (0, 0))],
        out_shape=[jax.ShapeDtypeStruct((n, d), F32),
                   jax.ShapeDtypeStruct((PARTS, n, PART_W), jnp.int32),
                   jax.ShapeDtypeStruct((TOP_K, n), jnp.int32),
                   jax.ShapeDtypeStruct((TOP_K, n), jnp.int32),
                   jax.ShapeDtypeStruct((TOP_K, n), F32),
                   jax.ShapeDtypeStruct((N_EXPERTS, 1), jnp.int32)],
        scratch_shapes=[pltpu.VMEM((N_EXPERTS, 1), jnp.int32)],
        compiler_params=_params(1),
        name="out_proj_router",
    )(x2d, oa, ob, ga, sc, sh, g.reshape(1, d), w_bf, wr_t, rbias, counts_in)


SC_WINDOW = 128
EXPERT_TILE = 512


def _sc_mesh():
    return plsc.VectorSubcoreMesh(core_axis_name="core", subcore_axis_name="subcore")


def _sc_scatter_rows(x, idx, n_out):
    n, w = x.shape
    steps = n // SC_WINDOW
    total = idx.shape[1] // SC_WINDOW

    @pl.kernel(out_type=jax.ShapeDtypeStruct((n_out, w), x.dtype), mesh=_sc_mesh(), scratch_types=[],
               name="moe_dispatch")
    def scatter(x_hbm, i_hbm, o_hbm):
        def body(x_vmem, i_vmem):
            pltpu.sync_copy(x_vmem, o_hbm.at[i_vmem.at[0]])

        pltpu.emit_pipeline(
            body, grid=(total,),
            in_specs=[pl.BlockSpec((SC_WINDOW, w), lambda i: (i % steps, 0)),
                      pl.BlockSpec((1, SC_WINDOW), lambda i: (0, i))],
            out_specs=[], core_axis_name=("core", "subcore"),
            dimension_semantics=(pltpu.PARALLEL,))(x_hbm, i_hbm)

    return scatter(x, idx)


def _sc_gather_rows(y, idx):
    w = y.shape[1]
    n = idx.shape[1]

    @pl.kernel(out_type=jax.ShapeDtypeStruct((n, w), y.dtype), mesh=_sc_mesh(), scratch_types=[],
               name="moe_collect")
    def gather(y_hbm, i_hbm, o_hbm):
        def body(i_vmem, o_vmem):
            pltpu.sync_copy(y_hbm.at[i_vmem.at[0]], o_vmem)

        pltpu.emit_pipeline(
            body, grid=(n // SC_WINDOW,),
            in_specs=[pl.BlockSpec((1, SC_WINDOW), lambda i: (0, i))],
            out_specs=[pl.BlockSpec((SC_WINDOW, w), lambda i: (i, 0))],
            core_axis_name=("core", "subcore"),
            dimension_semantics=(pltpu.PARALLEL,))(i_hbm, o_hbm)

    return gather(y, idx)


def _swiglu(parts_ref, wg_ref, wu_ref, wd_ref):
    a = _packed_matmul(parts_ref, wg_ref)
    u = _packed_matmul(parts_ref, wu_ref)
    hid = (a * jax.nn.sigmoid(a) * u).astype(BF16)
    return jnp.dot(hid, wd_ref[...], preferred_element_type=F32)


def _expert_kernel(te_ref, nu_ref, x_ref, wg_ref, wu_ref, wd_ref, y_ref, wg_scr, wu_scr, wd_scr):
    i = pl.program_id(0)

    @pl.when(i < nu_ref[0])
    def _():
        @pl.when((i == 0) | (te_ref[i] != te_ref[jnp.maximum(i - 1, 0)]))
        def _():
            wg_scr[...] = wg_ref[...].astype(BF16)
            wu_scr[...] = wu_ref[...].astype(BF16)
            wd_scr[...] = wd_ref[...].astype(BF16)

        y_ref[...] = _pack_rows(_swiglu(x_ref, wg_scr, wu_scr, wd_scr))


def _expert_call(tile_expert, n_used, x_sorted, wg, wu, wd):
    _, r, _ = x_sorted.shape
    ne, d, de = wg.shape
    n_tiles = r // EXPERT_TILE
    row_map = lambda i, te, nu: (0, jnp.minimum(i, nu[0] - 1), 0)
    return pl.pallas_call(
        _expert_kernel,
        grid_spec=pltpu.PrefetchScalarGridSpec(
            num_scalar_prefetch=2,
            grid=(n_tiles,),
            in_specs=[pl.BlockSpec((PARTS, EXPERT_TILE, PART_W), row_map),
                      pl.BlockSpec((None, d, de), lambda i, te, nu: (te[i], 0, 0)),
                      pl.BlockSpec((None, d, de), lambda i, te, nu: (te[i], 0, 0)),
                      pl.BlockSpec((None, de, d), lambda i, te, nu: (te[i], 0, 0))],
            out_specs=pl.BlockSpec((PARTS, EXPERT_TILE, PART_W), row_map),
            scratch_shapes=[pltpu.VMEM((d, de), BF16), pltpu.VMEM((d, de), BF16), pltpu.VMEM((de, d), BF16)]),
        out_shape=jax.ShapeDtypeStruct(x_sorted.shape, jnp.int32),
        compiler_params=_params(1),
        name="moe_experts",
    )(tile_expert, n_used, x_sorted, wg, wu, wd)


def _shared_kernel(h_ref, sg_ref, su_ref, sd_ref, o_ref):
    o_ref[...] = _swiglu(h_ref, sg_ref, su_ref, sd_ref)


def _shared_call(h2p, sg, su, sd, tm):
    _, n, _ = h2p.shape
    d = sd.shape[1]
    assert n % tm == 0
    return pl.pallas_call(
        _shared_kernel,
        grid=(n // tm,),
        in_specs=[pl.BlockSpec((PARTS, tm, PART_W), lambda i: (0, i, 0)),
                  _resident(sg.shape, lambda i: (0, 0)),
                  _resident(su.shape, lambda i: (0, 0)),
                  _resident(sd.shape, lambda i: (0, 0))],
        out_specs=pl.BlockSpec((tm, d), lambda i: (i, 0)),
        out_shape=jax.ShapeDtypeStruct((n, d), F32),
        compiler_params=_params(1),
        name="moe_shared",
    )(h2p, sg, su, sd)


def _final_kernel(x_ref, shared_ref, rows_ref, gk_ref, gf_ref, g_ref, *rest):
    o_ref = rest[-1]
    moe = shared_ref[...]
    half = moe.shape[1] // 2
    lo = [moe[:, j * PART_W:(j + 1) * PART_W] for j in range(PARTS)]
    hi = [moe[:, half + j * PART_W:half + (j + 1) * PART_W] for j in range(PARTS)]
    for k in range(TOP_K):
        gate = gk_ref[:, k:k + 1]
        for j in range(PARTS):
            a, b = _unpack_pair(rows_ref[k, j])
            lo[j] = lo[j] + gate * a.astype(F32)
            hi[j] = hi[j] + gate * b.astype(F32)
    x = x_ref[...] + gf_ref[...] * jnp.concatenate(lo + hi, axis=1)
    o_ref[...] = x * lax.rsqrt(jnp.mean(x * x, axis=-1, keepdims=True) + EPS) * g_ref[...]


def _final_call(x1, shared, shared_block0, rows, gk_t, gf, mod_map, g, tm, first, count, y_prev=None):
    n, d = x1.shape
    assert n % tm == 0
    r = gf.shape[1]
    in_specs = [pl.BlockSpec((tm, d), lambda i: (i + first, 0)),
                pl.BlockSpec((tm, d), lambda i: (i + first + shared_block0, 0)),
                pl.BlockSpec((TOP_K, PARTS, tm, PART_W), lambda i: (0, 0, i, 0)),
                pl.BlockSpec((tm, TOP_K), lambda i: (i + first, 0)),
                pl.BlockSpec((None, r, d), lambda i: mod_map(i + first)),
                pl.BlockSpec((1, d), lambda i: (0, 0))]
    args = [x1, shared, rows, gk_t, gf, g.reshape(1, d)]
    aliases = {}
    if y_prev is not None:
        in_specs.append(pl.BlockSpec(memory_space=pl.ANY))
        args.append(y_prev)
        aliases = {len(args) - 1: 0}
    return pl.pallas_call(
        _final_kernel,
        grid=(count,),
        in_specs=in_specs,
        out_specs=pl.BlockSpec((tm, d), lambda i: (i + first, 0)),
        out_shape=jax.ShapeDtypeStruct((n, d), F32),
        input_output_aliases=aliases,
        compiler_params=_params(1),
        name="final_norm",
    )(*args)


_QA, _KA, _VA, _QI, _KI, _IW, _QB, _KB, _VB = range(9)
_IN_SIZES = (A_HEADS * A_HEAD_DIM, A_KV_HEADS * A_HEAD_DIM, A_KV_HEADS * A_HEAD_DIM,
             IDX_HEADS * IDX_DIM, IDX_DIM, IDX_HEADS,
             B_HEADS * 2 * B_HEAD_DIM, B_HEADS * 2 * B_HEAD_DIM, B_HEADS * 2 * B_HEAD_DIM)


def _split_w_in(w_in):
    d = w_in.shape[0]
    offs = np.concatenate([[0], np.cumsum(_IN_SIZES)])
    cols = [w_in[:, offs[k]:offs[k + 1]].astype(BF16) for k in range(9)]
    zero = lambda n: jnp.zeros((d, n), BF16)
    qi = jnp.concatenate([cols[_QI].reshape(d, IDX_HEADS, IDX_DIM),
                          jnp.zeros((d, IDX_HEADS, LANES - IDX_DIM), BF16)], axis=-1).reshape(d, IDX_HEADS * LANES)
    w_a = jnp.concatenate([cols[_QA], cols[_KA], cols[_VA], qi,
                           cols[_KI], zero(LANES - IDX_DIM), cols[_IW], zero(LANES - IDX_HEADS)], axis=1)
    w_b = jnp.concatenate([cols[_QB], cols[_KB], cols[_VB]], axis=1)
    return w_a, w_b


def _group_forward(x, mod, past, rel_bias, lam, weights, *, lam_init, qb, qb_b, kb, tm, per_row_mod):
    (norm_a_g, w_a, w_b, subln_g) = weights
    b, t, d = x.shape
    n = b * t
    x2d = x.reshape(n, d)
    sh_a, sc_a = mod[0], mod[1]
    if per_row_mod:
        expand = lambda m: jnp.repeat(m, t, axis=0).reshape(1, n, d)
        mod_map = lambda i: (0, i, 0)
    else:
        expand = lambda m: m.reshape(b, 1, d)
        mod_map = lambda i: ((i * tm) // t, 0, 0)
    sc3, sh3 = expand(sc_a), expand(sh_a)

    na = A_HEADS * A_HEAD_DIM
    nk = A_KV_HEADS * A_HEAD_DIM
    nq = IDX_HEADS * LANES
    c_ka, c_va, c_qi = na, na + nk, na + 2 * nk
    c_ki, c_iw = c_qi + nq, c_qi + nq + LANES
    kv_order = tuple(range(A_KV_HEADS))
    defs_a = [(0, na, na, A_HEAD_DIM ** -0.5, None), (c_ka, nk, nk, None, kv_order), (c_ka, nk, nk, None, None),
              (c_va, nk, nk, None, kv_order), (c_va, nk, nk, None, None), (c_qi, nq, nq, None, None),
              (c_ki, LANES, IDX_DIM, None, None), (c_ki, LANES, LANES, None, None), (c_iw, LANES, LANES, None, None)]
    dt_a = [BF16, F32, BF16, F32, BF16, BF16, F32, BF16, F32]
    qa, ka, ka_bf, va, va_bf, qi, ki, ki_bf, iw = _norm_proj_call(
        x2d, sc3, sh3, mod_map, norm_a_g, w_a, defs_a, dt_a, tm, "in_proj_a")
    nb = B_HEADS * 2 * B_HEAD_DIM
    kb_order = tuple(range(2 * B_HEADS))
    vb_order = tuple(2 * h + half for half in range(2) for h in range(B_HEADS))
    defs_b = [(0, nb, nb, B_HEAD_DIM ** -0.5, None), (nb, nb, nb, None, kb_order), (nb, nb, nb, None, None),
              (2 * nb, nb, nb, None, vb_order), (2 * nb, nb, nb, None, None)]
    dt_b = [BF16, F32, BF16, F32, BF16]
    qbm, kbm, kb_bf, vbm, vb_bf = _norm_proj_call(
        x2d, sc3, sh3, mod_map, norm_a_g, w_b, defs_b, dt_b, tm, "in_proj_b")

    r3 = lambda a: a.reshape(b, t, a.shape[-1])
    cur = [r3(ka_bf), r3(va_bf), r3(ki_bf), r3(kb_bf), r3(vb_bf)]
    kv_pieces = None
    if past is None:
        qoff = 0
        far = cur
    else:
        assert t == qb, "a group with cached keys is one query block per batch row"
        qoff = past[0].shape[1]
        qbr = -(-qb // LANES) * LANES
        cur = [jnp.concatenate([a, jnp.zeros((b, qbr - t, a.shape[-1]), BF16)], axis=1) for a in cur]
        pk, pv, pi, pbk, pbv = past
        if qoff % kb == 0:
            kv_pieces = (A_KV_HEADS, tuple(range(A_KV_HEADS)))
            far = [pk.reshape(b, qoff * A_KV_HEADS, A_HEAD_DIM), pv.reshape(b, qoff * A_KV_HEADS, A_HEAD_DIM),
                   pi.reshape(b, qoff, -1), pbk.reshape(b, qoff, -1), pbv.reshape(b, qoff, -1)]
        else:
            far = [jnp.concatenate([p.reshape(b, qoff, -1), jnp.zeros((b, -qoff % kb, p[0, 0].size), p.dtype)], axis=1)
                   for p in past]
    length = qoff + t
    topk = min(TOPK_MAX, length // 4)

    tiles = _bias_tile_call(rel_bias, qb)
    tiles_b = tiles if qb_b == qb else _bias_tile_call(rel_bias, qb_b)
    out_a = _dsa_call(rel_bias, tiles, r3(qa), r3(qi), r3(iw), far[:3], cur[:3], qb=qb, kb=kb, qoff=qoff, topk=topk,
                      kv_pieces=kv_pieces)
    out_b = _diff_call(rel_bias, tiles_b, lam, subln_g, r3(qbm), far[3:], cur[3:], qb=qb_b, kb=kb, qoff=qoff,
                       lam_init=lam_init)
    rows = (ka.reshape(1, b, t, A_KV_HEADS, A_HEAD_DIM), va.reshape(1, b, t, A_KV_HEADS, A_HEAD_DIM),
            ki.reshape(1, b, t, IDX_DIM), kbm.reshape(1, b, t, B_HEADS, 2, B_HEAD_DIM),
            vbm.reshape(b, t, 2, B_HEADS, B_HEAD_DIM).transpose(0, 1, 3, 2, 4)
               .reshape(1, b, t, B_HEADS, 2 * B_HEAD_DIM))
    return x2d, out_a.reshape(n, -1), out_b.reshape(n, -1), rows, expand, mod_map


def kernel(x_prompt, x_sample, c_prompt, c_sample, cache_a_k, cache_a_v, cache_a_kidx, cache_b_k, cache_b_v,
           rel_bias, w_ada, b_ada, norm_a_g, w_in, w_out, diff_lam, subln_g, norm_f_g, w_router, router_bias,
           w_gate, w_up, w_down, ws_gate, ws_up, ws_down, final_g):
    assert w_ada.shape[0] == 1, "single-layer model"
    d = x_prompt.shape[-1]
    bp, tp, _ = x_prompt.shape
    bs, ts, _ = x_sample.shape
    lam_init = 0.8 - 0.6 * math.exp(-0.3 * 0)

    c_all = jnp.concatenate([c_prompt, c_sample], axis=0)
    rows = -(-c_all.shape[0] // 8) * 8
    c_pad = jnp.concatenate([c_all, jnp.zeros((rows - c_all.shape[0], d), F32)], axis=0)
    mod_all = _ada_call(c_pad, w_ada[0], b_ada[0])
    mod_p = [mod_all[:bp, k * d:(k + 1) * d] for k in range(6)]
    mod_s = [mod_all[bp:bp + bs, k * d:(k + 1) * d] for k in range(6)]

    w_a, w_b = _split_w_in(w_in[0])
    lam = _lam_call(diff_lam[0], lam_init)
    weights = (norm_a_g[0], w_a, w_b, subln_g[0])
    past = (cache_a_k[0], cache_a_v[0], cache_a_kidx[0], cache_b_k[0], cache_b_v[0])

    tm_p = 256
    tm_s = bs * ts
    xp2, oa_p, ob_p, rows_p, exp_p, map_p = _group_forward(
        x_prompt, mod_p, None, rel_bias, lam, weights, lam_init=lam_init, qb=256, qb_b=512, kb=512, tm=tm_p,
        per_row_mod=False)
    xs2, oa_s, ob_s, rows_s, exp_s, map_s = _group_forward(
        x_sample, mod_s, past, rel_bias, lam, weights, lam_init=lam_init, qb=ts, qb_b=ts, kb=512, tm=tm_s,
        per_row_mod=True)

    w_out_bf = w_out[0].astype(BF16)
    wr_t = w_router[0].T
    rb = router_bias[0].reshape(N_EXPERTS, 1)
    no_rows = jnp.zeros((N_EXPERTS, 1), jnp.int32)
    x1_p, h2_p, ek_p, pk_p, gk_p, cnt_p = _outproj_call(
        xp2, oa_p, ob_p, exp_p(mod_p[2]), exp_p(mod_p[4]), exp_p(mod_p[3]), map_p, norm_f_g[0], w_out_bf, wr_t, rb,
        no_rows, tm_p)
    x1_s, h2_s, ek_s, pk_s, gk_s, cnt_all = _outproj_call(
        xs2, oa_s, ob_s, exp_s(mod_s[2]), exp_s(mod_s[4]), exp_s(mod_s[3]), map_s, norm_f_g[0], w_out_bf, wr_t, rb,
        cnt_p, tm_s)

    n_p, n_s = bp * tp, bs * ts
    n_all = n_p + n_s
    n_rows = -(-(n_all * TOP_K) // EXPERT_TILE) * EXPERT_TILE + N_EXPERTS * EXPERT_TILE
    counts = cnt_all[:, 0]
    padded = (counts + EXPERT_TILE - 1) // EXPERT_TILE * EXPERT_TILE
    ends = jnp.cumsum(padded)
    starts = ends - padded
    tile_first_row = jnp.arange(n_rows // EXPERT_TILE, dtype=jnp.int32) * EXPERT_TILE
    tile_expert = jnp.minimum(jnp.sum(ends[None, :] <= tile_first_row[:, None], axis=1), N_EXPERTS - 1).astype(jnp.int32)
    n_used = (ends[-1:] // EXPERT_TILE).astype(jnp.int32)
    ek = jnp.concatenate([ek_p, ek_s], axis=1)
    expert_ids = jnp.arange(N_EXPERTS, dtype=jnp.int32)[:, None, None]
    dest = jnp.sum(jnp.where(ek[None] == expert_ids, starts[:, None, None], 0), axis=0) \
        + jnp.concatenate([pk_p, pk_s], axis=1)
    piece = jnp.arange(PARTS, dtype=jnp.int32)[None, :, None]
    idx = (dest[:, None, :] + piece * n_rows).astype(jnp.int32)

    h2_all = jnp.concatenate([h2_p, h2_s], axis=1)
    x_sorted = _sc_scatter_rows(h2_all.reshape(PARTS * n_all, PART_W), idx.reshape(1, -1),
                                PARTS * n_rows).reshape(PARTS, n_rows, PART_W)
    tm_shared = next(c for c in (640, 512, 256, 128, 64, 32, 16, 8) if n_all % c == 0)
    shared = _shared_call(h2_all, ws_gate[0].astype(BF16), ws_up[0].astype(BF16), ws_down[0].astype(BF16), tm_shared)
    y_sorted = _expert_call(tile_expert, n_used, x_sorted, w_gate[0], w_up[0], w_down[0]).reshape(PARTS * n_rows, PART_W)

    def finish(x1, gk_t, gf, mod_map, tm, token0, pieces):
        per = x1.shape[0] // tm // pieces
        assert x1.shape[0] == per * pieces * tm and token0 % tm == 0
        y = None
        for p in range(pieces):
            t0 = token0 + p * per * tm
            rows = _sc_gather_rows(y_sorted, idx[:, :, t0:t0 + per * tm].reshape(1, -1))
            y = _final_call(x1, shared, token0 // tm, rows.reshape(TOP_K, PARTS, per * tm, PART_W), gk_t, gf, mod_map,
                            final_g, tm, p * per, per, y)
        return y

    y_p = finish(x1_p, gk_p.T, exp_p(mod_p[5]), map_p, tm_p, 0, bp)
    y_s = finish(x1_s, gk_s.T, exp_s(mod_s[5]), map_s, tm_s, n_p, 1)
    return (y_p.reshape(bp, tp, d), y_s.reshape(bs, ts, d)) + rows_p + rows_s
```

```python
import functools
import math

import numpy as np
import jax
import jax.numpy as jnp
from jax import lax
from jax.experimental import pallas as pl
from jax.experimental.pallas import tpu as pltpu
from jax.experimental.pallas import tpu_sc as plsc

D_MODEL = 2048
CHUNK = 64
A_HEADS = 8
A_KV_HEADS = 2
A_REP = A_HEADS // A_KV_HEADS
A_HEAD_DIM = 128
IDX_HEADS = 16
IDX_DIM = 64
TOPK_MAX = 256
B_HEADS = 4
B_HEAD_DIM = 128
N_REL_BUCKETS = 32
REL_MAX_DIST = 128
N_ATTN_HEADS = A_HEADS + B_HEADS
N_EXPERTS = 64
TOP_K = 8
N_GROUPS = 8
TOPK_GROUPS = 4
EXPERT_DIM = 512
SHARED_DIM = 512
ROUTED_SCALE = 2.5
EPS = 1e-6

LANES = 128
INT_MIN = -(2 ** 31)
NEG_BIG = -1e30
TOP_LOW = 3
VMEM_LIMIT = 52 * 1024 * 1024

F32 = jnp.float32
BF16 = jnp.bfloat16
NT_DIMS = (((1,), (1,)), ((), ()))


def _params(n_axes):
    return pltpu.CompilerParams(dimension_semantics=("arbitrary",) * n_axes,
                                vmem_limit_bytes=VMEM_LIMIT)


def _resident(shape, index_map):
    return pl.BlockSpec(shape, index_map, pipeline_mode=pl.Buffered(1))


def _bucket_thresholds():
    nb = N_REL_BUCKETS // 2
    max_exact = nb // 2
    n = np.arange(max_exact, 4 * REL_MAX_DIST, dtype=np.int32)
    nf = n.astype(np.float32)
    large = max_exact + (np.log(nf / np.float32(max_exact)) / np.float32(math.log(REL_MAX_DIST / max_exact))
                         * np.float32(nb - max_exact)).astype(np.int32)
    large = np.minimum(large, nb - 1)
    assert np.all(np.diff(large) >= 0)
    ths = [int(n[np.argmax(large >= b)]) for b in range(max_exact + 1, nb)]
    return max_exact, nb, ths


_MAX_EXACT, _NB, _BUCKET_THS = _bucket_thresholds()
assert _BUCKET_THS[-1] <= LANES + 1
FAR_BUCKET = _NB - 1


def _ada_kernel(c_ref, w_ref, b_ref, o_ref):
    c = c_ref[...]
    s = c * jax.nn.sigmoid(c)
    o_ref[...] = jnp.dot(s, w_ref[...], preferred_element_type=F32,
                         precision=lax.Precision.HIGHEST) + b_ref[...]


def _ada_call(c, w_ada, b_ada):
    rows, d = c.shape
    n = w_ada.shape[1]
    tn = 1536
    assert n % tn == 0
    return pl.pallas_call(
        _ada_kernel,
        grid=(n // tn,),
        in_specs=[pl.BlockSpec((rows, d), lambda j: (0, 0)),
                  pl.BlockSpec((d, tn), lambda j: (0, j)),
                  pl.BlockSpec((1, tn), lambda j: (0, j))],
        out_specs=pl.BlockSpec((rows, tn), lambda j: (0, j)),
        out_shape=jax.ShapeDtypeStruct((rows, n), F32),
        compiler_params=_params(1),
        name="ada_mod",
    )(c, w_ada, b_ada.reshape(1, n))


def _modulated_norm(x, g, sc, sh):
    y = x * lax.rsqrt(jnp.mean(x * x, axis=-1, keepdims=True) + EPS) * g
    return y * (1.0 + sc) + sh


def _norm_proj_kernel(x_ref, sc_ref, sh_ref, g_ref, w_ref, *out_refs, defs):
    h = _modulated_norm(x_ref[...], g_ref[...], sc_ref[...], sh_ref[...]).astype(BF16)
    done = {}
    tm = x_ref.shape[0]
    for (start, width, store_w, mult, order), o_ref in zip(defs, out_refs):
        if (start, width) not in done:
            done[(start, width)] = jnp.dot(h, w_ref[:, start:start + width], preferred_element_type=F32)
        r = done[(start, width)][:, :store_w]
        if order is None:
            o_ref[...] = (r if mult is None else r * mult).astype(o_ref.dtype)
        else:
            for p, cb in enumerate(order):
                o_ref[pl.ds(p, tm, stride=len(order)), :] = r[:, cb * LANES:(cb + 1) * LANES].astype(o_ref.dtype)


def _norm_proj_call(x2d, sc, sh, mod_map, g, w_bf, defs, dtypes, tm, name):
    n, d = x2d.shape
    assert n % tm == 0
    r = sc.shape[1]
    shape = lambda rows, d_: (rows, d_[2]) if d_[4] is None else (rows * len(d_[4]), LANES)
    out_shapes = [jax.ShapeDtypeStruct(shape(n, d_), dt) for d_, dt in zip(defs, dtypes)]
    out_specs = [pl.BlockSpec(shape(tm, d_), lambda i: (i, 0)) for d_ in defs]
    return pl.pallas_call(
        functools.partial(_norm_proj_kernel, defs=tuple(defs)),
        grid=(n // tm,),
        in_specs=[pl.BlockSpec((tm, d), lambda i: (i, 0)),
                  pl.BlockSpec((None, r, d), mod_map),
                  pl.BlockSpec((None, r, d), mod_map),
                  pl.BlockSpec((1, d), lambda i: (0, 0)),
                  _resident(w_bf.shape, lambda i: (0, 0))],
        out_specs=out_specs,
        out_shape=out_shapes,
        compiler_params=_params(1),
        name=name,
    )(x2d, sc, sh, g.reshape(1, d), w_bf)


def _bias_tile_kernel(tab_ref, o_ref, *, qb):
    _, rows, width = o_ref.shape
    r = lax.broadcasted_iota(jnp.int32, (rows, width), 0)
    koff = lax.broadcasted_iota(jnp.int32, (rows, width), 1) - LANES
    rel = koff - r
    n = jnp.abs(rel)
    large = jnp.full((rows, width), _MAX_EXACT, jnp.int32)
    for th in _BUCKET_THS:
        large = large + jnp.where(n >= th, 1, 0)
    bucket = jnp.where(rel > 0, _NB, 0) + jnp.where(n < _MAX_EXACT, n, large)
    visible = ((koff >> 6) <= (r >> 6)) & (koff < qb)
    for h in range(N_ATTN_HEADS):
        b = jnp.zeros((rows, width), F32)
        for bk in range(N_REL_BUCKETS):
            b = jnp.where(bucket == bk, tab_ref[bk, h], b)
        o_ref[h] = jnp.where(visible, b, NEG_BIG)


def _bias_tile_call(rel_bias, qb):
    qbr = -(-qb // LANES) * LANES
    return pl.pallas_call(
        functools.partial(_bias_tile_kernel, qb=qb),
        in_specs=[pl.BlockSpec(memory_space=pltpu.SMEM)],
        out_specs=pl.BlockSpec(memory_space=pltpu.VMEM),
        out_shape=jax.ShapeDtypeStruct((N_ATTN_HEADS, qb, LANES + qbr), F32),
        name="bias_tiles",
    )(rel_bias)


def _sweep_far(q0, qb, kb, fn):
    far_end = jnp.maximum(q0 - LANES, 0)
    n_full = far_end // kb
    sub = kb // LANES

    def whole(j):
        fn(pl.multiple_of(j * kb, kb), j * sub, None)

    def pair(j, carry):
        whole(2 * j)
        whole(2 * j + 1)
        return carry

    lax.fori_loop(0, n_full // 2, pair, 0)
    pl.when(n_full % 2 == 1)(lambda: whole(n_full - 1))

    @pl.when(far_end > n_full * kb)
    def _():
        start = pl.multiple_of(n_full * kb, kb)
        col = start + lax.broadcasted_iota(jnp.int32, (qb, LANES), 1)
        fn(start, n_full * sub, [jnp.where(col + c * LANES < far_end, 0.0, NEG_BIG) for c in range(sub)])


def _key_rows(ref, start, width, pieces=None):
    if pieces is None:
        blk = ref[pl.ds(start, width), :].astype(BF16)
        pad = -blk.shape[1] % LANES
        return blk if pad == 0 else jnp.concatenate([blk, jnp.zeros((width, pad), BF16)], axis=1)
    r, rows = pieces
    window = ref.at[pl.ds(pl.multiple_of(start * r, LANES * r), width * r), :]
    return jnp.concatenate([window[pl.ds(p, width, stride=r), :].astype(BF16) for p in rows], axis=1)


def _dsa_kernel(tab_ref, q_ref, qi_ref, iw_ref, k_ref, v_ref, ki_ref, kc_ref, vc_ref, kic_ref, tile_ref, o_ref,
                keys_scr, half_scr, best_scr, thr_scr, mfar_scr, mnear_scr, l_scr, acc_scr, *, qb, kb, qoff, topk,
                kv_pieces):
    far_k = lambda start, width: _key_rows(k_ref, start, width, kv_pieces)
    far_v = lambda start, width: _key_rows(v_ref, start, width, kv_pieces)
    i = pl.program_id(1)
    q0 = qoff + i * qb
    qbr = tile_ref.shape[2] - LANES
    sub = kb // LANES
    w = iw_ref[:, :IDX_HEADS] * (IDX_HEADS ** -0.5 * IDX_DIM ** -0.5)

    def score_keys(kib):
        width = kib.shape[0]
        acc = jnp.zeros((qb, width), F32)
        for h in range(IDX_HEADS):
            d = lax.dot_general(qi_ref[:, h * LANES:(h + 1) * LANES], kib, NT_DIMS,
                                preferred_element_type=F32)
            acc = acc + w[:, h:h + 1] * jnp.maximum(d, 0.0)
        bits = pltpu.bitcast(acc, jnp.int32)
        return bits ^ ((bits >> 31) & 0x7FFFFFFF)

    n_score = (q0 + kb - 1) // kb

    def store_keys(slab0, key):
        for c in range(key.shape[1] // LANES):
            part = key[:, c * LANES:(c + 1) * LANES]
            keys_scr[slab0 + c] = part
            half_scr[slab0 + c] = (part >> 16).astype(jnp.int16)

    def score_body(j, carry):
        store_keys(j * sub, score_keys(_key_rows(ki_ref, pl.multiple_of(j * kb, kb), kb)))
        return carry

    lax.fori_loop(0, n_score, score_body, 0)

    cur_blk = q0 // LANES
    key = score_keys(kic_ref[...])
    store_keys(cur_blk, jnp.where(tile_ref[0, :, LANES:] > 0.5 * NEG_BIG, key, INT_MIN))
    n_blk = cur_blk + qbr // LANES

    rb = min(qb, LANES)
    lo16 = -(2 ** 15)
    n4 = n_blk // 4

    chunks = [pl.ds(c * rb, rb) for c in range(qb // rb)]
    ones = jnp.ones((LANES, LANES), BF16)

    def count_ge(ts):
        tbs = [t.astype(jnp.int16) for t in ts]

        def count(j, cnts):
            return tuple(cnt + jnp.where(half_scr[j, rows, :] >= tb, jnp.int16(1), jnp.int16(0))
                         for cnt, rows, tb in zip(cnts, chunks, tbs))

        def body4(j, cnts):
            for u in range(4):
                cnts = count(4 * j + u, cnts)
            return cnts

        cnts = lax.fori_loop(0, n4, body4, tuple(jnp.zeros((rb, LANES), jnp.int16) for _ in chunks))
        cnts = lax.fori_loop(4 * n4, n_blk, count, cnts)
        return [jnp.dot(cnt.astype(F32).astype(BF16), ones, preferred_element_type=F32) for cnt in cnts]

    def count_ge_best(ts):
        cnts = [sum(jnp.where(best_scr[u, rows, :] >= t.astype(jnp.int16), jnp.int16(1), jnp.int16(0))
                    for u in range(TOP_LOW)) for rows, t in zip(chunks, ts)]
        return [jnp.dot(cnt.astype(F32).astype(BF16), ones, preferred_element_type=F32) for cnt in cnts]

    def search(needed, counter):
        def bit_body(it, ts):
            cands = [t + (jnp.int32(1) << (15 - it)) for t in ts]
            return tuple(jnp.where(c >= n, cand, t) for c, n, cand, t in zip(counter(cands), needed, cands, ts))

        return lax.fori_loop(0, 16, bit_body, tuple(jnp.full((rb, LANES), lo16, jnp.int32) for _ in chunks))

    t_hi = search([float(topk)] * len(chunks), count_ge)
    top = -lo16 - 1
    above = [jnp.where(t == top, 0.0, c)
             for t, c in zip(t_hi, count_ge([jnp.minimum(t + 1, top) for t in t_hi]))]

    def low_plane(rows, tb):
        def body(j, carry):
            best, spill = carry[:TOP_LOW], carry[TOP_LOW]
            k32 = keys_scr[j, rows, :]
            a = jnp.where((k32 >> 16) == tb, (k32 & 0xFFFF) + lo16, lo16).astype(jnp.int16)
            half_scr[j, rows, :] = a
            kept = []
            for b in best:
                keep_b = b >= a
                kept.append(jnp.where(keep_b, b, a))
                a = jnp.where(keep_b, a, b)
            return tuple(kept) + (jnp.where(spill >= a, spill, a),)

        lowest = jnp.full((rb, LANES), lo16, jnp.int16)
        out = lax.fori_loop(0, n_blk, body, (lowest,) * (TOP_LOW + 1))
        for u in range(TOP_LOW):
            best_scr[u, rows, :] = out[u]
        return jnp.max(out[TOP_LOW].astype(jnp.int32))

    spilled = functools.reduce(jnp.maximum, [low_plane(rows, tb) for rows, tb in zip(chunks, t_hi)])
    needed = [topk - a for a in above]
    t_lo = lax.cond(spilled > lo16, lambda: search(needed, count_ge), lambda: search(needed, count_ge_best))
    for rows, th, tl in zip(chunks, t_hi, t_lo):
        thr_scr[rows, :] = jnp.maximum(th * 65536 + (tl - lo16), INT_MIN + 1)

    def count32(pred):
        def count(j, cnts):
            return tuple(cnt + jnp.where(pred(keys_scr[j, rows, :], thr_scr[rows, :]), 1, 0)
                         for cnt, rows in zip(cnts, chunks))

        cnts = lax.fori_loop(0, n_blk, count, tuple(jnp.zeros((rb, LANES), jnp.int32) for _ in chunks))
        return [jnp.dot(cnt.astype(F32).astype(BF16), ones, preferred_element_type=F32) for cnt in cnts]

    surplus = functools.reduce(jnp.maximum, [jnp.max(c) for c in count32(lambda k, t: k >= t)]) - topk

    @pl.when(surplus > 0)
    def _():
        keep = [topk - c for c in count32(lambda k, t: k > t)]
        upto = jnp.where(lax.broadcasted_iota(jnp.int32, (LANES, LANES), 0)
                         <= lax.broadcasted_iota(jnp.int32, (LANES, LANES), 1), 1.0, 0.0).astype(BF16)

        def demote(j, seen):
            out = []
            for rows, kp, sn in zip(chunks, keep, seen):
                k32 = keys_scr[j, rows, :]
                tied = k32 == thr_scr[rows, :]
                tied_bf = jnp.where(tied, 1.0, 0.0).astype(BF16)
                rank = sn + jnp.dot(tied_bf, upto, preferred_element_type=F32)
                keys_scr[j, rows, :] = k32 - jnp.where(tied & (rank > kp), 1, 0)
                out.append(sn + jnp.dot(tied_bf, ones, preferred_element_type=F32))
            return tuple(out)

        lax.fori_loop(0, n_blk, demote, tuple(jnp.zeros((rb, LANES), F32) for _ in chunks))

    hd = A_HEAD_DIM

    def select_mask(blk0, n):
        return [jnp.where(keys_scr[blk0 + c] >= thr_scr[...], 0.0, NEG_BIG) for c in range(n)]

    def logits(h, kblk):
        g = h // A_REP
        s = lax.dot_general(q_ref[:, h * hd:(h + 1) * hd], kblk[:, g * hd:(g + 1) * hd], NT_DIMS,
                            preferred_element_type=F32)
        return [s[:, c * LANES:(c + 1) * LANES] for c in range(s.shape[1] // LANES)]

    def tile_cols(h, col0, n):
        return [tile_ref[h, :, col0 + c * LANES:col0 + (c + 1) * LANES] for c in range(n)]

    mfar_scr[...] = jnp.full(mfar_scr.shape, NEG_BIG, F32)
    mnear_scr[...] = jnp.full(mnear_scr.shape, NEG_BIG, F32)

    def far_mask(blk0, colmask):
        am = select_mask(blk0, sub)
        return am if colmask is None else [a + c for a, c in zip(am, colmask)]

    def max_far(start, blk0, colmask):
        kblk = far_k(start, kb)
        am = far_mask(blk0, colmask)
        for h in range(A_HEADS):
            mfar_scr[h] = functools.reduce(
                jnp.maximum, [s + a for s, a in zip(logits(h, kblk), am)], mfar_scr[h])

    def max_near(kblk, blk0, col0):
        n = kblk.shape[0] // LANES
        am = select_mask(blk0, n)
        for h in range(A_HEADS):
            mnear_scr[h] = functools.reduce(
                jnp.maximum, [s + a + b for s, a, b in zip(logits(h, kblk), am, tile_cols(h, col0, n))],
                mnear_scr[h])

    prev_start = pl.multiple_of(q0 - LANES, LANES)
    _sweep_far(q0, qb, kb, max_far)
    pl.when(q0 > 0)(lambda: max_near(far_k(prev_start, LANES), cur_blk - 1, 0))
    max_near(kc_ref[...], cur_blk, LANES)

    for h in range(A_HEADS):
        fb = tab_ref[FAR_BUCKET, h]
        m = jnp.maximum(jnp.max(mfar_scr[h], axis=-1, keepdims=True) + fb,
                        jnp.max(mnear_scr[h], axis=-1, keepdims=True))
        mfar_scr[h] = jnp.broadcast_to(fb - m, (qb, LANES))
        mnear_scr[h] = jnp.broadcast_to(-m, (qb, LANES))
    l_scr[...] = jnp.zeros(l_scr.shape, F32)
    acc_scr[...] = jnp.zeros(acc_scr.shape, F32)

    def accumulate(h, p, vblk):
        g = h // A_REP
        l_scr[h] += functools.reduce(jnp.add, p)
        acc_scr[h] += jnp.dot(jnp.concatenate(p, axis=1).astype(BF16), vblk[:, g * hd:(g + 1) * hd],
                              preferred_element_type=F32)

    def acc_far(start, blk0, colmask):
        kblk = far_k(start, kb)
        vblk = far_v(start, kb)
        am = far_mask(blk0, colmask)
        for h in range(A_HEADS):
            shift = mfar_scr[h]
            accumulate(h, [jnp.exp(s + a + shift) for s, a in zip(logits(h, kblk), am)], vblk)

    def acc_near(kblk, vblk, blk0, col0):
        n = kblk.shape[0] // LANES
        am = select_mask(blk0, n)
        for h in range(A_HEADS):
            shift = mnear_scr[h]
            accumulate(h, [jnp.exp(s + a + b + shift) for s, a, b in
                           zip(logits(h, kblk), am, tile_cols(h, col0, n))], vblk)

    _sweep_far(q0, qb, kb, acc_far)
    pl.when(q0 > 0)(lambda: acc_near(far_k(prev_start, LANES), far_v(prev_start, LANES),
                                     cur_blk - 1, 0))
    acc_near(kc_ref[...], vc_ref[...], cur_blk, LANES)

    for h in range(A_HEADS):
        o_ref[:, h * hd:(h + 1) * hd] = (
            acc_scr[h] / jnp.sum(l_scr[h], axis=-1, keepdims=True)).astype(o_ref.dtype)


def _dsa_call(rel_bias, tiles, q, qi, iw, far, cur, *, qb, kb, qoff, topk, kv_pieces=None):
    b, t, _ = q.shape
    lf = far[2].shape[1]
    qbr = tiles.shape[2] - LANES
    q0_max = qoff + t - qb
    far_end = -(-q0_max // kb) * kb
    assert t % qb == 0 and qoff % LANES == 0 and far_end <= lf and cur[0].shape[1] == (t // qb) * qbr
    n_slabs = max(far_end, q0_max + qbr) // LANES
    mode = {"pipeline_mode": pl.Buffered(1)} if t // qb > 1 else {}
    far_spec = lambda a: pl.BlockSpec((None,) + a.shape[1:], lambda bi, i: (bi, 0, 0), **mode)
    cur_spec = lambda a: pl.BlockSpec((None, qbr, a.shape[2]), lambda bi, i: (bi, i, 0))
    return pl.pallas_call(
        functools.partial(_dsa_kernel, qb=qb, kb=kb, qoff=qoff, topk=topk, kv_pieces=kv_pieces),
        grid=(b, t // qb),
        in_specs=[pl.BlockSpec(memory_space=pltpu.SMEM),
                  pl.BlockSpec((None, qb, q.shape[2]), lambda bi, i: (bi, i, 0)),
                  pl.BlockSpec((None, qb, qi.shape[2]), lambda bi, i: (bi, i, 0)),
                  pl.BlockSpec((None, qb, iw.shape[2]), lambda bi, i: (bi, i, 0)),
                  far_spec(far[0]), far_spec(far[1]), far_spec(far[2]),
                  cur_spec(cur[0]), cur_spec(cur[1]), cur_spec(cur[2]),
                  _resident(tiles.shape, lambda bi, i: (0, 0, 0))],
        out_specs=pl.BlockSpec((None, qb, A_HEADS * A_HEAD_DIM), lambda bi, i: (bi, i, 0)),
        out_shape=jax.ShapeDtypeStruct((b, t, A_HEADS * A_HEAD_DIM), BF16),
        scratch_shapes=[pltpu.VMEM((n_slabs, qb, LANES), jnp.int32),
                        pltpu.VMEM((n_slabs, qb, LANES), jnp.int16),
                        pltpu.VMEM((TOP_LOW, qb, LANES), jnp.int16),
                        pltpu.VMEM((qb, LANES), jnp.int32),
                        pltpu.VMEM((A_HEADS, qb, LANES), F32),
                        pltpu.VMEM((A_HEADS, qb, LANES), F32),
                        pltpu.VMEM((A_HEADS, qb, LANES), F32),
                        pltpu.VMEM((A_HEADS, qb, A_HEAD_DIM), F32)],
        compiler_params=_params(2),
        name="mixer_a",
    )(rel_bias, q, qi, iw, *far, *cur, tiles)


def _diff_kernel(tab_ref, lam_ref, sg_ref, q_ref, k_ref, v_ref, kc_ref, vc_ref, tile_ref, o_ref,
                 mfar_scr, mnear_scr, l_scr, acc_scr, *, qb, kb, qoff, lam_init):
    head = pl.program_id(1)
    i = pl.program_id(2)
    q0 = qoff + i * qb
    hd = B_HEAD_DIM
    far_bias = tab_ref[FAR_BUCKET, A_HEADS + head]
    far_k = lambda start, width: _key_rows(k_ref, start, width)
    far_v = lambda start, width: _key_rows(v_ref, start, width)

    def logits(c, kblk):
        s = lax.dot_general(q_ref[:, c * hd:(c + 1) * hd], kblk[:, c * hd:(c + 1) * hd], NT_DIMS,
                            preferred_element_type=F32)
        return [s[:, u * LANES:(u + 1) * LANES] for u in range(s.shape[1] // LANES)]

    def tile_cols(col0, n):
        return [tile_ref[0, :, col0 + u * LANES:col0 + (u + 1) * LANES] for u in range(n)]

    mfar_scr[...] = jnp.full(mfar_scr.shape, NEG_BIG, F32)
    mnear_scr[...] = jnp.full(mnear_scr.shape, NEG_BIG, F32)

    def max_far(start, blk0, colmask):
        kblk = far_k(start, kb)
        for c in range(2):
            s = logits(c, kblk)
            if colmask is not None:
                s = [x + m for x, m in zip(s, colmask)]
            mfar_scr[c] = functools.reduce(jnp.maximum, s, mfar_scr[c])

    def max_near(kblk, col0):
        for c in range(2):
            mnear_scr[c] = functools.reduce(
                jnp.maximum, [s + b for s, b in zip(logits(c, kblk), tile_cols(col0, kblk.shape[0] // LANES))],
                mnear_scr[c])

    prev_start = pl.multiple_of(q0 - LANES, LANES)
    _sweep_far(q0, qb, kb, max_far)
    pl.when(q0 > 0)(lambda: max_near(far_k(prev_start, LANES), 0))
    max_near(kc_ref[...], LANES)

    for c in range(2):
        m = jnp.maximum(jnp.max(mfar_scr[c], axis=-1, keepdims=True) + far_bias,
                        jnp.max(mnear_scr[c], axis=-1, keepdims=True))
        mfar_scr[c] = jnp.broadcast_to(far_bias - m, (qb, LANES))
        mnear_scr[c] = jnp.broadcast_to(-m, (qb, LANES))
    l_scr[...] = jnp.zeros(l_scr.shape, F32)
    acc_scr[...] = jnp.zeros(acc_scr.shape, F32)

    def accumulate(c, p, vblk):
        l_scr[c] += functools.reduce(jnp.add, p)
        acc_scr[c] += jnp.dot(jnp.concatenate(p, axis=1).astype(BF16), vblk, preferred_element_type=F32)

    def acc_far(start, blk0, colmask):
        kblk = far_k(start, kb)
        vblk = far_v(start, kb)
        for c in range(2):
            shift = mfar_scr[c]
            shifts = [shift] * (kb // LANES) if colmask is None else [shift + m for m in colmask]
            accumulate(c, [jnp.exp(s + sh) for s, sh in zip(logits(c, kblk), shifts)], vblk)

    def acc_near(kblk, vblk, col0):
        for c in range(2):
            shift = mnear_scr[c]
            accumulate(c, [jnp.exp(s + b + shift) for s, b in
                           zip(logits(c, kblk), tile_cols(col0, kblk.shape[0] // LANES))], vblk)

    _sweep_far(q0, qb, kb, acc_far)
    pl.when(q0 > 0)(lambda: acc_near(far_k(prev_start, LANES), far_v(prev_start, LANES), 0))
    acc_near(kc_ref[...], vc_ref[...], LANES)

    lam = lam_ref[0, 0]
    o = (acc_scr[0] / jnp.sum(l_scr[0], axis=-1, keepdims=True)
         - lam * (acc_scr[1] / jnp.sum(l_scr[1], axis=-1, keepdims=True)))
    o = o * lax.rsqrt(jnp.mean(o * o, axis=-1, keepdims=True) + EPS) * sg_ref[...]
    o_ref[...] = (o * (1.0 - lam_init)).astype(o_ref.dtype)


def _diff_call(rel_bias, tiles, lam, subln_g, q, far, cur, *, qb, kb, qoff, lam_init):
    b, t, _ = q.shape
    lf = far[0].shape[1]
    hw = 2 * B_HEAD_DIM
    qbr = tiles.shape[2] - LANES
    q0_max = qoff + t - qb
    far_end = -(-max(q0_max - LANES, 0) // kb) * kb
    assert t % qb == 0 and qoff % LANES == 0 and far_end <= lf and cur[0].shape[1] == (t // qb) * qbr
    far_spec = pl.BlockSpec((None, lf, hw), lambda bi, h, i: (bi, 0, h))
    cur_spec = pl.BlockSpec((None, qbr, hw), lambda bi, h, i: (bi, i, h))
    return pl.pallas_call(
        functools.partial(_diff_kernel, qb=qb, kb=kb, qoff=qoff, lam_init=lam_init),
        grid=(b, B_HEADS, t // qb),
        in_specs=[pl.BlockSpec(memory_space=pltpu.SMEM),
                  pl.BlockSpec(memory_space=pltpu.SMEM),
                  pl.BlockSpec((1, hw), lambda bi, h, i: (0, 0)),
                  pl.BlockSpec((None, qb, hw), lambda bi, h, i: (bi, i, h)),
                  far_spec, far_spec, cur_spec, cur_spec,
                  pl.BlockSpec((1, qb, tiles.shape[2]), lambda bi, h, i: (A_HEADS + h, 0, 0))],
        out_specs=pl.BlockSpec((None, qb, hw), lambda bi, h, i: (bi, i, h)),
        out_shape=jax.ShapeDtypeStruct((b, t, B_HEADS * hw), BF16),
        scratch_shapes=[pltpu.VMEM((2, qb, LANES), F32),
                        pltpu.VMEM((2, qb, LANES), F32),
                        pltpu.VMEM((2, qb, LANES), F32),
                        pltpu.VMEM((2, qb, hw), F32)],
        compiler_params=_params(3),
        name="mixer_b",
    )(rel_bias, lam, subln_g.reshape(1, hw), q, *far, *cur, tiles)


def _lam_kernel(dl_ref, o_ref, *, lam_init):
    dl = dl_ref[...]
    s1 = jnp.sum(dl[0:1] * dl[1:2], axis=-1, keepdims=True)
    s2 = jnp.sum(dl[2:3] * dl[3:4], axis=-1, keepdims=True)
    o_ref[...] = jnp.exp(s1) - jnp.exp(s2) + lam_init


def _lam_call(diff_lam, lam_init):
    return pl.pallas_call(
        functools.partial(_lam_kernel, lam_init=lam_init),
        out_shape=jax.ShapeDtypeStruct((1, 1), F32),
        name="diff_lambda",
    )(diff_lam)


def _route(logits_t, rbias, counts):
    tm = logits_t.shape[1]
    per = N_EXPERTS // N_GROUPS
    scores = jax.nn.sigmoid(logits_t).reshape(N_GROUPS, per, tm)
    ch = scores + rbias.reshape(N_GROUPS, per, 1)
    sub = lax.broadcasted_iota(jnp.int32, ch.shape, 1)
    grp = lax.broadcasted_iota(jnp.int32, ch.shape, 0)
    m1 = jnp.max(ch, axis=1, keepdims=True)
    i1 = jnp.min(jnp.where(ch == m1, sub, per), axis=1, keepdims=True)
    m2 = jnp.max(jnp.where(sub == i1, -jnp.inf, ch), axis=1, keepdims=True)
    gs = jnp.broadcast_to(m1 + m2, ch.shape)
    rank = jnp.zeros(ch.shape, jnp.int32)
    for g2 in range(N_GROUPS):
        o = gs[g2:g2 + 1]
        rank = rank + jnp.where(o > gs, 1, 0) + jnp.where(o == gs, 1, 0) * jnp.where(grp > g2, 1, 0)
    cm = jnp.where(rank < TOPK_GROUPS, ch, -jnp.inf)
    eid = grp * per + sub
    erank = jnp.zeros(cm.shape, jnp.int32)
    for e2 in range(N_EXPERTS):
        o = cm[e2 // per:e2 // per + 1, e2 % per:e2 % per + 1, :]
        erank = erank + jnp.where(o > cm, 1, 0) + jnp.where(o == cm, 1, 0) * jnp.where(eid > e2, 1, 0)
    chosen = erank < TOP_K
    sel = jnp.where(chosen, scores, 0.0)
    denom = jnp.sum(jnp.sum(sel, axis=1, keepdims=True), axis=0, keepdims=True)
    gates = sel / denom * ROUTED_SCALE

    ones = jnp.where(chosen, 1.0, 0.0).reshape(N_EXPERTS, tm)
    earlier = (lax.broadcasted_iota(jnp.int32, (tm, tm), 0) < lax.broadcasted_iota(jnp.int32, (tm, tm), 1))
    before = jnp.dot(ones.astype(BF16), jnp.where(earlier, 1.0, 0.0).astype(BF16), preferred_element_type=F32)
    pos = (before + counts.astype(F32)).reshape(N_GROUPS, per, tm)
    new_counts = counts + jnp.sum(ones, axis=1, keepdims=True).astype(jnp.int32)

    def per_slot(values):
        rows = [jnp.sum(jnp.sum(jnp.where(erank == k, values, 0.0), axis=1, keepdims=True), axis=0)
                for k in range(TOP_K)]
        return jnp.concatenate(rows, axis=0)

    return (per_slot(eid.astype(F32)).astype(jnp.int32), per_slot(pos).astype(jnp.int32), per_slot(gates),
            new_counts)


def _pack_pair(a, b):
    ua = pltpu.bitcast(a.astype(BF16).astype(F32), jnp.uint32)
    ub = pltpu.bitcast(b.astype(BF16).astype(F32), jnp.uint32)
    return pltpu.bitcast(ua | (ub >> 16), jnp.int32)


def _unpack_pair(p):
    u = pltpu.bitcast(p, jnp.uint32)
    a = pltpu.bitcast(u & jnp.uint32(0xFFFF0000), F32)
    b = pltpu.bitcast(u << 16, F32)
    return a.astype(BF16), b.astype(BF16)


PARTS = 4
PART_W = D_MODEL // 2 // PARTS


def _pack_rows(x):
    half = x.shape[1] // 2
    return jnp.stack([_pack_pair(x[:, j * PART_W:(j + 1) * PART_W],
                                 x[:, half + j * PART_W:half + (j + 1) * PART_W]) for j in range(PARTS)])


def _packed_matmul(parts_ref, w_ref):
    half = w_ref.shape[0] // 2
    acc = None
    for j in range(PARTS):
        a, b = _unpack_pair(parts_ref[j])
        t = (jnp.dot(a, w_ref[j * PART_W:(j + 1) * PART_W, :], preferred_element_type=F32)
             + jnp.dot(b, w_ref[half + j * PART_W:half + (j + 1) * PART_W, :], preferred_element_type=F32))
        acc = t if acc is None else acc + t
    return acc


def _outproj_kernel(x_ref, oa_ref, ob_ref, ga_ref, sc_ref, sh_ref, g_ref, w_ref, wr_ref, rb_ref, cin_ref,
                    x1_ref, h2_ref, ek_ref, pk_ref, gk_ref, cout_ref, cnt_scr):
    @pl.when(pl.program_id(0) == 0)
    def _():
        cnt_scr[...] = cin_ref[...]

    half = oa_ref.shape[1]
    mix = (jnp.dot(oa_ref[...], w_ref[:half, :], preferred_element_type=F32)
           + jnp.dot(ob_ref[...], w_ref[half:, :], preferred_element_type=F32))
    x1 = x_ref[...] + ga_ref[...] * mix
    x1_ref[...] = x1
    h2 = _modulated_norm(x1, g_ref[...], sc_ref[...], sh_ref[...])
    h2_ref[...] = _pack_rows(h2)
    logits_t = lax.dot_general(wr_ref[...], h2, NT_DIMS, preferred_element_type=F32,
                               precision=lax.Precision.HIGHEST)
    ek, pk, gk, counts = _route(logits_t, rb_ref[...], cnt_scr[...])
    ek_ref[...] = ek
    pk_ref[...] = pk
    gk_ref[...] = gk
    cnt_scr[...] = counts
    cout_ref[...] = counts


def _outproj_call(x2d, oa, ob, ga, sc, sh, mod_map, g, w_bf, wr_t, rbias, counts_in, tm):
    n, d = x2d.shape
    assert n % tm == 0
    r = sc.shape[1]
    mod_spec = pl.BlockSpec((None, r, d), mod_map)
    slot_spec = pl.BlockSpec((TOP_K, tm), lambda i: (0, i))
    return pl.pallas_call(
        _outproj_kernel,
        grid=(n // tm,),
        in_specs=[pl.BlockSpec((tm, d), lambda i: (i, 0)),
                  pl.BlockSpec((tm, oa.shape[1]), lambda i: (i, 0)),
                  pl.BlockSpec((tm, ob.shape[1]), lambda i: (i, 0)),
                  mod_spec, mod_spec, mod_spec,
                  pl.BlockSpec((1, d), lambda i: (0, 0)),
                  _resident(w_bf.shape, lambda i: (0, 0)),
                  _resident(wr_t.shape, lambda i: (0, 0)),
                  pl.BlockSpec((N_EXPERTS, 1), lambda i: (0, 0)),
                  pl.BlockSpec((N_EXPERTS, 1), lambda i: (0, 0))],
        out_specs=[pl.BlockSpec((tm, d), lambda i: (i, 0)),
                   pl.BlockSpec((PARTS, tm, PART_W), lambda i: (0, i, 0)),
                   slot_spec, slot_spec, slot_spec,
                   pl.BlockSpec((N_EXPERTS, 1), lambda i: (0, 0))],
        out_shape=[jax.ShapeDtypeStruct((n, d), F32),
                   jax.ShapeDtypeStruct((PARTS, n, PART_W), jnp.int32),
                   jax.ShapeDtypeStruct((TOP_K, n), jnp.int32),
                   jax.ShapeDtypeStruct((TOP_K, n), jnp.int32),
                   jax.ShapeDtypeStruct((TOP_K, n), F32),
                   jax.ShapeDtypeStruct((N_EXPERTS, 1), jnp.int32)],
        scratch_shapes=[pltpu.VMEM((N_EXPERTS, 1), jnp.int32)],
        compiler_params=_params(1),
        name="out_proj_router",
    )(x2d, oa, ob, ga, sc, sh, g.reshape(1, d), w_bf, wr_t, rbias, counts_in)


SC_WINDOW = 128
EXPERT_TILE = 512


def _sc_mesh():
    return plsc.VectorSubcoreMesh(core_axis_name="core", subcore_axis_name="subcore")


def _sc_scatter_rows(x, idx, n_out):
    n, w = x.shape
    steps = n // SC_WINDOW
    total = idx.shape[1] // SC_WINDOW

    @pl.kernel(out_type=jax.ShapeDtypeStruct((n_out, w), x.dtype), mesh=_sc_mesh(), scratch_types=[],
               name="moe_dispatch")
    def scatter(x_hbm, i_hbm, o_hbm):
        def body(x_vmem, i_vmem):
            pltpu.sync_copy(x_vmem, o_hbm.at[i_vmem.at[0]])

        pltpu.emit_pipeline(
            body, grid=(total,),
            in_specs=[pl.BlockSpec((SC_WINDOW, w), lambda i: (i % steps, 0)),
                      pl.BlockSpec((1, SC_WINDOW), lambda i: (0, i))],
            out_specs=[], core_axis_name=("core", "subcore"),
            dimension_semantics=(pltpu.PARALLEL,))(x_hbm, i_hbm)

    return scatter(x, idx)


def _sc_gather_rows(y, idx):
    w = y.shape[1]
    n = idx.shape[1]

    @pl.kernel(out_type=jax.ShapeDtypeStruct((n, w), y.dtype), mesh=_sc_mesh(), scratch_types=[],
               name="moe_collect")
    def gather(y_hbm, i_hbm, o_hbm):
        def body(i_vmem, o_vmem):
            pltpu.sync_copy(y_hbm.at[i_vmem.at[0]], o_vmem)

        pltpu.emit_pipeline(
            body, grid=(n // SC_WINDOW,),
            in_specs=[pl.BlockSpec((1, SC_WINDOW), lambda i: (0, i))],
            out_specs=[pl.BlockSpec((SC_WINDOW, w), lambda i: (i, 0))],
            core_axis_name=("core", "subcore"),
            dimension_semantics=(pltpu.PARALLEL,))(i_hbm, o_hbm)

    return gather(y, idx)


def _swiglu(parts_ref, wg_ref, wu_ref, wd_ref):
    a = _packed_matmul(parts_ref, wg_ref)
    u = _packed_matmul(parts_ref, wu_ref)
    hid = (a * jax.nn.sigmoid(a) * u).astype(BF16)
    return jnp.dot(hid, wd_ref[...], preferred_element_type=F32)


def _expert_kernel(te_ref, nu_ref, x_ref, wg_ref, wu_ref, wd_ref, y_ref, wg_scr, wu_scr, wd_scr):
    i = pl.program_id(0)

    @pl.when(i < nu_ref[0])
    def _():
        @pl.when((i == 0) | (te_ref[i] != te_ref[jnp.maximum(i - 1, 0)]))
        def _():
            wg_scr[...] = wg_ref[...].astype(BF16)
            wu_scr[...] = wu_ref[...].astype(BF16)
            wd_scr[...] = wd_ref[...].astype(BF16)

        y_ref[...] = _pack_rows(_swiglu(x_ref, wg_scr, wu_scr, wd_scr))


def _expert_call(tile_expert, n_used, x_sorted, wg, wu, wd):
    _, r, _ = x_sorted.shape
    ne, d, de = wg.shape
    n_tiles = r // EXPERT_TILE
    row_map = lambda i, te, nu: (0, jnp.minimum(i, nu[0] - 1), 0)
    return pl.pallas_call(
        _expert_kernel,
        grid_spec=pltpu.PrefetchScalarGridSpec(
            num_scalar_prefetch=2,
            grid=(n_tiles,),
            in_specs=[pl.BlockSpec((PARTS, EXPERT_TILE, PART_W), row_map),
                      pl.BlockSpec((None, d, de), lambda i, te, nu: (te[i], 0, 0)),
                      pl.BlockSpec((None, d, de), lambda i, te, nu: (te[i], 0, 0)),
                      pl.BlockSpec((None, de, d), lambda i, te, nu: (te[i], 0, 0))],
            out_specs=pl.BlockSpec((PARTS, EXPERT_TILE, PART_W), row_map),
            scratch_shapes=[pltpu.VMEM((d, de), BF16), pltpu.VMEM((d, de), BF16), pltpu.VMEM((de, d), BF16)]),
        out_shape=jax.ShapeDtypeStruct(x_sorted.shape, jnp.int32),
        compiler_params=_params(1),
        name="moe_experts",
    )(tile_expert, n_used, x_sorted, wg, wu, wd)


def _final_kernel(x_ref, h_ref, rows_ref, gk_ref, gf_ref, g_ref, sg_ref, su_ref, sd_ref, o_ref):
    moe = _swiglu(h_ref, sg_ref, su_ref, sd_ref)
    half = moe.shape[1] // 2
    lo = [moe[:, j * PART_W:(j + 1) * PART_W] for j in range(PARTS)]
    hi = [moe[:, half + j * PART_W:half + (j + 1) * PART_W] for j in range(PARTS)]
    for k in range(TOP_K):
        gate = gk_ref[:, k:k + 1]
        for j in range(PARTS):
            a, b = _unpack_pair(rows_ref[k, j])
            lo[j] = lo[j] + gate * a.astype(F32)
            hi[j] = hi[j] + gate * b.astype(F32)
    x = x_ref[...] + gf_ref[...] * jnp.concatenate(lo + hi, axis=1)
    o_ref[...] = x * lax.rsqrt(jnp.mean(x * x, axis=-1, keepdims=True) + EPS) * g_ref[...]


def _final_call(x1, h2p, rows, row_block0, gk_t, gf, mod_map, g, sg, su, sd, tm):
    n, d = x1.shape
    assert n % tm == 0
    r = gf.shape[1]
    return pl.pallas_call(
        _final_kernel,
        grid=(n // tm,),
        in_specs=[pl.BlockSpec((tm, d), lambda i: (i, 0)),
                  pl.BlockSpec((PARTS, tm, PART_W), lambda i: (0, i, 0)),
                  pl.BlockSpec((TOP_K, PARTS, tm, PART_W), lambda i: (0, 0, i + row_block0, 0)),
                  pl.BlockSpec((tm, TOP_K), lambda i: (i, 0)),
                  pl.BlockSpec((None, r, d), mod_map),
                  pl.BlockSpec((1, d), lambda i: (0, 0)),
                  _resident(sg.shape, lambda i: (0, 0)),
                  _resident(su.shape, lambda i: (0, 0)),
                  _resident(sd.shape, lambda i: (0, 0))],
        out_specs=pl.BlockSpec((tm, d), lambda i: (i, 0)),
        out_shape=jax.ShapeDtypeStruct((n, d), F32),
        compiler_params=_params(1),
        name="final_norm",
    )(x1, h2p, rows, gk_t, gf, g.reshape(1, d), sg, su, sd)


_QA, _KA, _VA, _QI, _KI, _IW, _QB, _KB, _VB = range(9)
_IN_SIZES = (A_HEADS * A_HEAD_DIM, A_KV_HEADS * A_HEAD_DIM, A_KV_HEADS * A_HEAD_DIM,
             IDX_HEADS * IDX_DIM, IDX_DIM, IDX_HEADS,
             B_HEADS * 2 * B_HEAD_DIM, B_HEADS * 2 * B_HEAD_DIM, B_HEADS * 2 * B_HEAD_DIM)


def _split_w_in(w_in):
    d = w_in.shape[0]
    offs = np.concatenate([[0], np.cumsum(_IN_SIZES)])
    cols = [w_in[:, offs[k]:offs[k + 1]].astype(BF16) for k in range(9)]
    zero = lambda n: jnp.zeros((d, n), BF16)
    qi = jnp.concatenate([cols[_QI].reshape(d, IDX_HEADS, IDX_DIM),
                          jnp.zeros((d, IDX_HEADS, LANES - IDX_DIM), BF16)], axis=-1).reshape(d, IDX_HEADS * LANES)
    w_a = jnp.concatenate([cols[_QA], cols[_KA], cols[_VA], qi,
                           cols[_KI], zero(LANES - IDX_DIM), cols[_IW], zero(LANES - IDX_HEADS)], axis=1)
    w_b = jnp.concatenate([cols[_QB], cols[_KB], cols[_VB]], axis=1)
    return w_a, w_b


def _group_forward(x, mod, past, rel_bias, lam, weights, *, lam_init, qb, qb_b, kb, tm, per_row_mod):
    (norm_a_g, w_a, w_b, subln_g) = weights
    b, t, d = x.shape
    n = b * t
    x2d = x.reshape(n, d)
    sh_a, sc_a = mod[0], mod[1]
    if per_row_mod:
        expand = lambda m: jnp.repeat(m, t, axis=0).reshape(1, n, d)
        mod_map = lambda i: (0, i, 0)
    else:
        expand = lambda m: m.reshape(b, 1, d)
        mod_map = lambda i: ((i * tm) // t, 0, 0)
    sc3, sh3 = expand(sc_a), expand(sh_a)

    na = A_HEADS * A_HEAD_DIM
    nk = A_KV_HEADS * A_HEAD_DIM
    nq = IDX_HEADS * LANES
    c_ka, c_va, c_qi = na, na + nk, na + 2 * nk
    c_ki, c_iw = c_qi + nq, c_qi + nq + LANES
    kv_order = tuple(range(A_KV_HEADS))
    defs_a = [(0, na, na, A_HEAD_DIM ** -0.5, None), (c_ka, nk, nk, None, kv_order), (c_ka, nk, nk, None, None),
              (c_va, nk, nk, None, kv_order), (c_va, nk, nk, None, None), (c_qi, nq, nq, None, None),
              (c_ki, LANES, IDX_DIM, None, None), (c_ki, LANES, LANES, None, None), (c_iw, LANES, LANES, None, None)]
    dt_a = [BF16, F32, BF16, F32, BF16, BF16, F32, BF16, F32]
    qa, ka, ka_bf, va, va_bf, qi, ki, ki_bf, iw = _norm_proj_call(
        x2d, sc3, sh3, mod_map, norm_a_g, w_a, defs_a, dt_a, tm, "in_proj_a")
    nb = B_HEADS * 2 * B_HEAD_DIM
    kb_order = tuple(range(2 * B_HEADS))
    vb_order = tuple(2 * h + half for half in range(2) for h in range(B_HEADS))
    defs_b = [(0, nb, nb, B_HEAD_DIM ** -0.5, None), (nb, nb, nb, None, kb_order), (nb, nb, nb, None, None),
              (2 * nb, nb, nb, None, vb_order), (2 * nb, nb, nb, None, None)]
    dt_b = [BF16, F32, BF16, F32, BF16]
    qbm, kbm, kb_bf, vbm, vb_bf = _norm_proj_call(
        x2d, sc3, sh3, mod_map, norm_a_g, w_b, defs_b, dt_b, tm, "in_proj_b")

    r3 = lambda a: a.reshape(b, t, a.shape[-1])
    cur = [r3(ka_bf), r3(va_bf), r3(ki_bf), r3(kb_bf), r3(vb_bf)]
    kv_pieces = None
    if past is None:
        qoff = 0
        far = cur
    else:
        assert t == qb, "a group with cached keys is one query block per batch row"
        qoff = past[0].shape[1]
        qbr = -(-qb // LANES) * LANES
        cur = [jnp.concatenate([a, jnp.zeros((b, qbr - t, a.shape[-1]), BF16)], axis=1) for a in cur]
        pk, pv, pi, pbk, pbv = past
        if qoff % kb == 0:
            kv_pieces = (A_KV_HEADS, tuple(range(A_KV_HEADS)))
            far = [pk.reshape(b, qoff * A_KV_HEADS, A_HEAD_DIM), pv.reshape(b, qoff * A_KV_HEADS, A_HEAD_DIM),
                   pi.reshape(b, qoff, -1), pbk.reshape(b, qoff, -1), pbv.reshape(b, qoff, -1)]
        else:
            far = [jnp.concatenate([p.reshape(b, qoff, -1), jnp.zeros((b, -qoff % kb, p[0, 0].size), p.dtype)], axis=1)
                   for p in past]
    length = qoff + t
    topk = min(TOPK_MAX, length // 4)

    tiles = _bias_tile_call(rel_bias, qb)
    tiles_b = tiles if qb_b == qb else _bias_tile_call(rel_bias, qb_b)
    out_a = _dsa_call(rel_bias, tiles, r3(qa), r3(qi), r3(iw), far[:3], cur[:3], qb=qb, kb=kb, qoff=qoff, topk=topk,
                      kv_pieces=kv_pieces)
    out_b = _diff_call(rel_bias, tiles_b, lam, subln_g, r3(qbm), far[3:], cur[3:], qb=qb_b, kb=kb, qoff=qoff,
                       lam_init=lam_init)
    rows = (ka.reshape(1, b, t, A_KV_HEADS, A_HEAD_DIM), va.reshape(1, b, t, A_KV_HEADS, A_HEAD_DIM),
            ki.reshape(1, b, t, IDX_DIM), kbm.reshape(1, b, t, B_HEADS, 2, B_HEAD_DIM),
            vbm.reshape(b, t, 2, B_HEADS, B_HEAD_DIM).transpose(0, 1, 3, 2, 4)
               .reshape(1, b, t, B_HEADS, 2 * B_HEAD_DIM))
    return x2d, out_a.reshape(n, -1), out_b.reshape(n, -1), rows, expand, mod_map


def kernel(x_prompt, x_sample, c_prompt, c_sample, cache_a_k, cache_a_v, cache_a_kidx, cache_b_k, cache_b_v,
           rel_bias, w_ada, b_ada, norm_a_g, w_in, w_out, diff_lam, subln_g, norm_f_g, w_router, router_bias,
           w_gate, w_up, w_down, ws_gate, ws_up, ws_down, final_g):
    assert w_ada.shape[0] == 1, "single-layer model"
    d = x_prompt.shape[-1]
    bp, tp, _ = x_prompt.shape
    bs, ts, _ = x_sample.shape
    lam_init = 0.8 - 0.6 * math.exp(-0.3 * 0)

    c_all = jnp.concatenate([c_prompt, c_sample], axis=0)
    rows = -(-c_all.shape[0] // 8) * 8
    c_pad = jnp.concatenate([c_all, jnp.zeros((rows - c_all.shape[0], d), F32)], axis=0)
    mod_all = _ada_call(c_pad, w_ada[0], b_ada[0])
    mod_p = [mod_all[:bp, k * d:(k + 1) * d] for k in range(6)]
    mod_s = [mod_all[bp:bp + bs, k * d:(k + 1) * d] for k in range(6)]

    w_a, w_b = _split_w_in(w_in[0])
    lam = _lam_call(diff_lam[0], lam_init)
    weights = (norm_a_g[0], w_a, w_b, subln_g[0])
    past = (cache_a_k[0], cache_a_v[0], cache_a_kidx[0], cache_b_k[0], cache_b_v[0])

    tm_p = 256
    tm_s = bs * ts
    xp2, oa_p, ob_p, rows_p, exp_p, map_p = _group_forward(
        x_prompt, mod_p, None, rel_bias, lam, weights, lam_init=lam_init, qb=256, qb_b=512, kb=512, tm=tm_p,
        per_row_mod=False)
    xs2, oa_s, ob_s, rows_s, exp_s, map_s = _group_forward(
        x_sample, mod_s, past, rel_bias, lam, weights, lam_init=lam_init, qb=ts, qb_b=ts, kb=512, tm=tm_s,
        per_row_mod=True)

    w_out_bf = w_out[0].astype(BF16)
    wr_t = w_router[0].T
    rb = router_bias[0].reshape(N_EXPERTS, 1)
    no_rows = jnp.zeros((N_EXPERTS, 1), jnp.int32)
    x1_p, h2_p, ek_p, pk_p, gk_p, cnt_p = _outproj_call(
        xp2, oa_p, ob_p, exp_p(mod_p[2]), exp_p(mod_p[4]), exp_p(mod_p[3]), lambda i: map_p(2 * i), norm_f_g[0],
        w_out_bf, wr_t, rb, no_rows, 2 * tm_p)
    x1_s, h2_s, ek_s, pk_s, gk_s, cnt_all = _outproj_call(
        xs2, oa_s, ob_s, exp_s(mod_s[2]), exp_s(mod_s[4]), exp_s(mod_s[3]), map_s, norm_f_g[0], w_out_bf, wr_t, rb,
        cnt_p, tm_s)

    n_p, n_s = bp * tp, bs * ts
    n_all = n_p + n_s
    n_rows = -(-(n_all * TOP_K) // EXPERT_TILE) * EXPERT_TILE + N_EXPERTS * EXPERT_TILE
    counts = cnt_all[:, 0]
    padded = (counts + EXPERT_TILE - 1) // EXPERT_TILE * EXPERT_TILE
    ends = jnp.cumsum(padded)
    starts = ends - padded
    tile_first_row = jnp.arange(n_rows // EXPERT_TILE, dtype=jnp.int32) * EXPERT_TILE
    tile_expert = jnp.minimum(jnp.sum(ends[None, :] <= tile_first_row[:, None], axis=1), N_EXPERTS - 1).astype(jnp.int32)
    n_used = (ends[-1:] // EXPERT_TILE).astype(jnp.int32)
    ek = jnp.concatenate([ek_p, ek_s], axis=1)
    expert_ids = jnp.arange(N_EXPERTS, dtype=jnp.int32)[:, None, None]
    dest = jnp.sum(jnp.where(ek[None] == expert_ids, starts[:, None, None], 0), axis=0) \
        + jnp.concatenate([pk_p, pk_s], axis=1)
    piece = jnp.arange(PARTS, dtype=jnp.int32)[None, :, None]
    idx = (dest[:, None, :] + piece * n_rows).reshape(1, TOP_K * PARTS * n_all).astype(jnp.int32)

    h2_all = jnp.concatenate([h2_p, h2_s], axis=1).reshape(PARTS * n_all, PART_W)
    x_sorted = _sc_scatter_rows(h2_all, idx, PARTS * n_rows).reshape(PARTS, n_rows, PART_W)
    y_sorted = _expert_call(tile_expert, n_used, x_sorted, w_gate[0], w_up[0], w_down[0])
    rows = _sc_gather_rows(y_sorted.reshape(PARTS * n_rows, PART_W), idx).reshape(TOP_K, PARTS, n_all, PART_W)

    shared = (ws_gate[0].astype(BF16), ws_up[0].astype(BF16), ws_down[0].astype(BF16))
    assert n_p % tm_s == 0
    y_p = _final_call(x1_p, h2_p, rows, 0, gk_p.T, exp_p(mod_p[5]), map_p, final_g, *shared, tm_p)
    y_s = _final_call(x1_s, h2_s, rows, n_p // tm_s, gk_s.T, exp_s(mod_s[5]), map_s, final_g, *shared, tm_s)
    return (y_p.reshape(bp, tp, d), y_s.reshape(bs, ts, d)) + rows_p + rows_s
```

```python
import functools
import math

import numpy as np
import jax
import jax.numpy as jnp
from jax import lax
from jax.experimental import pallas as pl
from jax.experimental.pallas import tpu as pltpu
from jax.experimental.pallas import tpu_sc as plsc

D_MODEL = 2048
CHUNK = 64
A_HEADS = 8
A_KV_HEADS = 2
A_REP = A_HEADS // A_KV_HEADS
A_HEAD_DIM = 128
IDX_HEADS = 16
IDX_DIM = 64
TOPK_MAX = 256
B_HEADS = 4
B_HEAD_DIM = 128
N_REL_BUCKETS = 32
REL_MAX_DIST = 128
N_ATTN_HEADS = A_HEADS + B_HEADS
N_EXPERTS = 64
TOP_K = 8
N_GROUPS = 8
TOPK_GROUPS = 4
EXPERT_DIM = 512
SHARED_DIM = 512
ROUTED_SCALE = 2.5
EPS = 1e-6

LANES = 128
INT_MIN = -(2 ** 31)
NEG_BIG = -1e30
TOP_LOW = 3
VMEM_LIMIT = 52 * 1024 * 1024

F32 = jnp.float32
BF16 = jnp.bfloat16
NT_DIMS = (((1,), (1,)), ((), ()))


def _params(n_axes):
    return pltpu.CompilerParams(dimension_semantics=("arbitrary",) * n_axes,
                                vmem_limit_bytes=VMEM_LIMIT)


def _resident(shape, index_map):
    return pl.BlockSpec(shape, index_map, pipeline_mode=pl.Buffered(1))


def _bucket_thresholds():
    nb = N_REL_BUCKETS // 2
    max_exact = nb // 2
    n = np.arange(max_exact, 4 * REL_MAX_DIST, dtype=np.int32)
    nf = n.astype(np.float32)
    large = max_exact + (np.log(nf / np.float32(max_exact)) / np.float32(math.log(REL_MAX_DIST / max_exact))
                         * np.float32(nb - max_exact)).astype(np.int32)
    large = np.minimum(large, nb - 1)
    assert np.all(np.diff(large) >= 0)
    ths = [int(n[np.argmax(large >= b)]) for b in range(max_exact + 1, nb)]
    return max_exact, nb, ths


_MAX_EXACT, _NB, _BUCKET_THS = _bucket_thresholds()
assert _BUCKET_THS[-1] <= LANES + 1
FAR_BUCKET = _NB - 1


def _ada_kernel(c_ref, w_ref, b_ref, o_ref):
    c = c_ref[...]
    s = c * jax.nn.sigmoid(c)
    o_ref[...] = jnp.dot(s, w_ref[...], preferred_element_type=F32,
                         precision=lax.Precision.HIGHEST) + b_ref[...]


def _ada_call(c, w_ada, b_ada):
    rows, d = c.shape
    n = w_ada.shape[1]
    tn = 1536
    assert n % tn == 0
    return pl.pallas_call(
        _ada_kernel,
        grid=(n // tn,),
        in_specs=[pl.BlockSpec((rows, d), lambda j: (0, 0)),
                  pl.BlockSpec((d, tn), lambda j: (0, j)),
                  pl.BlockSpec((1, tn), lambda j: (0, j))],
        out_specs=pl.BlockSpec((rows, tn), lambda j: (0, j)),
        out_shape=jax.ShapeDtypeStruct((rows, n), F32),
        compiler_params=_params(1),
        name="ada_mod",
    )(c, w_ada, b_ada.reshape(1, n))


def _modulated_norm(x, g, sc, sh):
    y = x * lax.rsqrt(jnp.mean(x * x, axis=-1, keepdims=True) + EPS) * g
    return y * (1.0 + sc) + sh


def _norm_proj_kernel(x_ref, sc_ref, sh_ref, g_ref, w_ref, *out_refs, defs):
    h = _modulated_norm(x_ref[...], g_ref[...], sc_ref[...], sh_ref[...]).astype(BF16)
    done = {}
    tm = x_ref.shape[0]
    for (start, width, store_w, mult, order), o_ref in zip(defs, out_refs):
        if (start, width) not in done:
            done[(start, width)] = jnp.dot(h, w_ref[:, start:start + width], preferred_element_type=F32)
        r = done[(start, width)][:, :store_w]
        if order is None:
            o_ref[...] = (r if mult is None else r * mult).astype(o_ref.dtype)
        else:
            for p, cb in enumerate(order):
                o_ref[pl.ds(p, tm, stride=len(order)), :] = r[:, cb * LANES:(cb + 1) * LANES].astype(o_ref.dtype)


def _norm_proj_call(x2d, sc, sh, mod_map, g, w_bf, defs, dtypes, tm, name):
    n, d = x2d.shape
    assert n % tm == 0
    r = sc.shape[1]
    shape = lambda rows, d_: (rows, d_[2]) if d_[4] is None else (rows * len(d_[4]), LANES)
    out_shapes = [jax.ShapeDtypeStruct(shape(n, d_), dt) for d_, dt in zip(defs, dtypes)]
    out_specs = [pl.BlockSpec(shape(tm, d_), lambda i: (i, 0)) for d_ in defs]
    return pl.pallas_call(
        functools.partial(_norm_proj_kernel, defs=tuple(defs)),
        grid=(n // tm,),
        in_specs=[pl.BlockSpec((tm, d), lambda i: (i, 0)),
                  pl.BlockSpec((None, r, d), mod_map),
                  pl.BlockSpec((None, r, d), mod_map),
                  pl.BlockSpec((1, d), lambda i: (0, 0)),
                  _resident(w_bf.shape, lambda i: (0, 0))],
        out_specs=out_specs,
        out_shape=out_shapes,
        compiler_params=_params(1),
        name=name,
    )(x2d, sc, sh, g.reshape(1, d), w_bf)


def _bias_tile_kernel(tab_ref, o_ref, *, qb):
    _, rows, width = o_ref.shape
    r = lax.broadcasted_iota(jnp.int32, (rows, width), 0)
    koff = lax.broadcasted_iota(jnp.int32, (rows, width), 1) - LANES
    rel = koff - r
    n = jnp.abs(rel)
    large = jnp.full((rows, width), _MAX_EXACT, jnp.int32)
    for th in _BUCKET_THS:
        large = large + jnp.where(n >= th, 1, 0)
    bucket = jnp.where(rel > 0, _NB, 0) + jnp.where(n < _MAX_EXACT, n, large)
    visible = ((koff >> 6) <= (r >> 6)) & (koff < qb)
    for h in range(N_ATTN_HEADS):
        b = jnp.zeros((rows, width), F32)
        for bk in range(N_REL_BUCKETS):
            b = jnp.where(bucket == bk, tab_ref[bk, h], b)
        o_ref[h] = jnp.where(visible, b, NEG_BIG)


def _bias_tile_call(rel_bias, qb):
    qbr = -(-qb // LANES) * LANES
    return pl.pallas_call(
        functools.partial(_bias_tile_kernel, qb=qb),
        in_specs=[pl.BlockSpec(memory_space=pltpu.SMEM)],
        out_specs=pl.BlockSpec(memory_space=pltpu.VMEM),
        out_shape=jax.ShapeDtypeStruct((N_ATTN_HEADS, qb, LANES + qbr), F32),
        name="bias_tiles",
    )(rel_bias)


def _sweep_far(q0, qb, kb, fn):
    far_end = jnp.maximum(q0 - LANES, 0)
    n_full = far_end // kb
    sub = kb // LANES

    def whole(j):
        fn(pl.multiple_of(j * kb, kb), j * sub, None)

    def pair(j, carry):
        whole(2 * j)
        whole(2 * j + 1)
        return carry

    lax.fori_loop(0, n_full // 2, pair, 0)
    pl.when(n_full % 2 == 1)(lambda: whole(n_full - 1))

    @pl.when(far_end > n_full * kb)
    def _():
        start = pl.multiple_of(n_full * kb, kb)
        col = start + lax.broadcasted_iota(jnp.int32, (qb, LANES), 1)
        fn(start, n_full * sub, [jnp.where(col + c * LANES < far_end, 0.0, NEG_BIG) for c in range(sub)])


def _key_rows(ref, start, width, pieces=None):
    if pieces is None:
        blk = ref[pl.ds(start, width), :].astype(BF16)
        pad = -blk.shape[1] % LANES
        return blk if pad == 0 else jnp.concatenate([blk, jnp.zeros((width, pad), BF16)], axis=1)
    r, rows = pieces
    window = ref.at[pl.ds(pl.multiple_of(start * r, LANES * r), width * r), :]
    return jnp.concatenate([window[pl.ds(p, width, stride=r), :].astype(BF16) for p in rows], axis=1)


def _dsa_kernel(tab_ref, q_ref, qi_ref, iw_ref, k_ref, v_ref, ki_ref, kc_ref, vc_ref, kic_ref, tile_ref, o_ref,
                keys_scr, half_scr, best_scr, thr_scr, mfar_scr, mnear_scr, l_scr, acc_scr, *, qb, kb, qoff, topk,
                kv_pieces):
    far_k = lambda start, width: _key_rows(k_ref, start, width, kv_pieces)
    far_v = lambda start, width: _key_rows(v_ref, start, width, kv_pieces)
    i = pl.program_id(1)
    q0 = qoff + i * qb
    qbr = tile_ref.shape[2] - LANES
    sub = kb // LANES
    w = iw_ref[:, :IDX_HEADS] * (IDX_HEADS ** -0.5 * IDX_DIM ** -0.5)

    def score_keys(kib):
        width = kib.shape[0]
        acc = jnp.zeros((qb, width), F32)
        for h in range(IDX_HEADS):
            d = lax.dot_general(qi_ref[:, h * LANES:(h + 1) * LANES], kib, NT_DIMS,
                                preferred_element_type=F32)
            acc = acc + w[:, h:h + 1] * jnp.maximum(d, 0.0)
        bits = pltpu.bitcast(acc, jnp.int32)
        return bits ^ ((bits >> 31) & 0x7FFFFFFF)

    n_score = (q0 + kb - 1) // kb

    def store_keys(slab0, key):
        for c in range(key.shape[1] // LANES):
            part = key[:, c * LANES:(c + 1) * LANES]
            keys_scr[slab0 + c] = part
            half_scr[slab0 + c] = (part >> 16).astype(jnp.int16)

    def score_body(j, carry):
        store_keys(j * sub, score_keys(_key_rows(ki_ref, pl.multiple_of(j * kb, kb), kb)))
        return carry

    lax.fori_loop(0, n_score, score_body, 0)

    cur_blk = q0 // LANES
    key = score_keys(kic_ref[...])
    store_keys(cur_blk, jnp.where(tile_ref[0, :, LANES:] > 0.5 * NEG_BIG, key, INT_MIN))
    n_blk = cur_blk + qbr // LANES

    rb = min(qb, LANES)
    lo16 = -(2 ** 15)
    n4 = n_blk // 4

    chunks = [pl.ds(c * rb, rb) for c in range(qb // rb)]
    ones = jnp.ones((LANES, LANES), BF16)

    def count_ge(ts):
        tbs = [t.astype(jnp.int16) for t in ts]

        def count(j, cnts):
            return tuple(cnt + jnp.where(half_scr[j, rows, :] >= tb, jnp.int16(1), jnp.int16(0))
                         for cnt, rows, tb in zip(cnts, chunks, tbs))

        def body4(j, cnts):
            for u in range(4):
                cnts = count(4 * j + u, cnts)
            return cnts

        cnts = lax.fori_loop(0, n4, body4, tuple(jnp.zeros((rb, LANES), jnp.int16) for _ in chunks))
        cnts = lax.fori_loop(4 * n4, n_blk, count, cnts)
        return [jnp.dot(cnt.astype(F32).astype(BF16), ones, preferred_element_type=F32) for cnt in cnts]

    def count_ge_best(ts):
        cnts = [sum(jnp.where(best_scr[u, rows, :] >= t.astype(jnp.int16), jnp.int16(1), jnp.int16(0))
                    for u in range(TOP_LOW)) for rows, t in zip(chunks, ts)]
        return [jnp.dot(cnt.astype(F32).astype(BF16), ones, preferred_element_type=F32) for cnt in cnts]

    def search(needed, counter):
        def bit_body(it, ts):
            cands = [t + (jnp.int32(1) << (15 - it)) for t in ts]
            return tuple(jnp.where(c >= n, cand, t) for c, n, cand, t in zip(counter(cands), needed, cands, ts))

        return lax.fori_loop(0, 16, bit_body, tuple(jnp.full((rb, LANES), lo16, jnp.int32) for _ in chunks))

    t_hi = search([float(topk)] * len(chunks), count_ge)
    top = -lo16 - 1
    above = [jnp.where(t == top, 0.0, c)
             for t, c in zip(t_hi, count_ge([jnp.minimum(t + 1, top) for t in t_hi]))]

    def low_plane(rows, tb):
        def body(j, carry):
            best, spill = carry[:TOP_LOW], carry[TOP_LOW]
            k32 = keys_scr[j, rows, :]
            a = jnp.where((k32 >> 16) == tb, (k32 & 0xFFFF) + lo16, lo16).astype(jnp.int16)
            half_scr[j, rows, :] = a
            kept = []
            for b in best:
                keep_b = b >= a
                kept.append(jnp.where(keep_b, b, a))
                a = jnp.where(keep_b, a, b)
            return tuple(kept) + (jnp.where(spill >= a, spill, a),)

        lowest = jnp.full((rb, LANES), lo16, jnp.int16)
        out = lax.fori_loop(0, n_blk, body, (lowest,) * (TOP_LOW + 1))
        for u in range(TOP_LOW):
            best_scr[u, rows, :] = out[u]
        return jnp.max(out[TOP_LOW].astype(jnp.int32))

    spilled = functools.reduce(jnp.maximum, [low_plane(rows, tb) for rows, tb in zip(chunks, t_hi)])
    needed = [topk - a for a in above]
    t_lo = lax.cond(spilled > lo16, lambda: search(needed, count_ge), lambda: search(needed, count_ge_best))
    for rows, th, tl in zip(chunks, t_hi, t_lo):
        thr_scr[rows, :] = jnp.maximum(th * 65536 + (tl - lo16), INT_MIN + 1)

    def count32(pred):
        def count(j, cnts):
            return tuple(cnt + jnp.where(pred(keys_scr[j, rows, :], thr_scr[rows, :]), 1, 0)
                         for cnt, rows in zip(cnts, chunks))

        cnts = lax.fori_loop(0, n_blk, count, tuple(jnp.zeros((rb, LANES), jnp.int32) for _ in chunks))
        return [jnp.dot(cnt.astype(F32).astype(BF16), ones, preferred_element_type=F32) for cnt in cnts]

    surplus = functools.reduce(jnp.maximum, [jnp.max(c) for c in count32(lambda k, t: k >= t)]) - topk

    @pl.when(surplus > 0)
    def _():
        keep = [topk - c for c in count32(lambda k, t: k > t)]
        upto = jnp.where(lax.broadcasted_iota(jnp.int32, (LANES, LANES), 0)
                         <= lax.broadcasted_iota(jnp.int32, (LANES, LANES), 1), 1.0, 0.0).astype(BF16)

        def demote(j, seen):
            out = []
            for rows, kp, sn in zip(chunks, keep, seen):
                k32 = keys_scr[j, rows, :]
                tied = k32 == thr_scr[rows, :]
                tied_bf = jnp.where(tied, 1.0, 0.0).astype(BF16)
                rank = sn + jnp.dot(tied_bf, upto, preferred_element_type=F32)
                keys_scr[j, rows, :] = k32 - jnp.where(tied & (rank > kp), 1, 0)
                out.append(sn + jnp.dot(tied_bf, ones, preferred_element_type=F32))
            return tuple(out)

        lax.fori_loop(0, n_blk, demote, tuple(jnp.zeros((rb, LANES), F32) for _ in chunks))

    hd = A_HEAD_DIM

    def select_mask(blk0, n):
        return [jnp.where(keys_scr[blk0 + c] >= thr_scr[...], 0.0, NEG_BIG) for c in range(n)]

    def logits(h, kblk):
        g = h // A_REP
        s = lax.dot_general(q_ref[:, h * hd:(h + 1) * hd], kblk[:, g * hd:(g + 1) * hd], NT_DIMS,
                            preferred_element_type=F32)
        return [s[:, c * LANES:(c + 1) * LANES] for c in range(s.shape[1] // LANES)]

    def tile_cols(h, col0, n):
        return [tile_ref[h, :, col0 + c * LANES:col0 + (c + 1) * LANES] for c in range(n)]

    mfar_scr[...] = jnp.full(mfar_scr.shape, NEG_BIG, F32)
    mnear_scr[...] = jnp.full(mnear_scr.shape, NEG_BIG, F32)

    def far_mask(blk0, colmask):
        am = select_mask(blk0, sub)
        return am if colmask is None else [a + c for a, c in zip(am, colmask)]

    def max_far(start, blk0, colmask):
        kblk = far_k(start, kb)
        am = far_mask(blk0, colmask)
        for h in range(A_HEADS):
            mfar_scr[h] = functools.reduce(
                jnp.maximum, [s + a for s, a in zip(logits(h, kblk), am)], mfar_scr[h])

    def max_near(kblk, blk0, col0):
        n = kblk.shape[0] // LANES
        am = select_mask(blk0, n)
        for h in range(A_HEADS):
            mnear_scr[h] = functools.reduce(
                jnp.maximum, [s + a + b for s, a, b in zip(logits(h, kblk), am, tile_cols(h, col0, n))],
                mnear_scr[h])

    prev_start = pl.multiple_of(q0 - LANES, LANES)
    _sweep_far(q0, qb, kb, max_far)
    pl.when(q0 > 0)(lambda: max_near(far_k(prev_start, LANES), cur_blk - 1, 0))
    max_near(kc_ref[...], cur_blk, LANES)

    for h in range(A_HEADS):
        fb = tab_ref[FAR_BUCKET, h]
        m = jnp.maximum(jnp.max(mfar_scr[h], axis=-1, keepdims=True) + fb,
                        jnp.max(mnear_scr[h], axis=-1, keepdims=True))
        mfar_scr[h] = jnp.broadcast_to(fb - m, (qb, LANES))
        mnear_scr[h] = jnp.broadcast_to(-m, (qb, LANES))
    l_scr[...] = jnp.zeros(l_scr.shape, F32)
    acc_scr[...] = jnp.zeros(acc_scr.shape, F32)

    def accumulate(h, p, vblk):
        g = h // A_REP
        l_scr[h] += functools.reduce(jnp.add, p)
        acc_scr[h] += jnp.dot(jnp.concatenate(p, axis=1).astype(BF16), vblk[:, g * hd:(g + 1) * hd],
                              preferred_element_type=F32)

    def acc_far(start, blk0, colmask):
        kblk = far_k(start, kb)
        vblk = far_v(start, kb)
        am = far_mask(blk0, colmask)
        for h in range(A_HEADS):
            shift = mfar_scr[h]
            accumulate(h, [jnp.exp(s + a + shift) for s, a in zip(logits(h, kblk), am)], vblk)

    def acc_near(kblk, vblk, blk0, col0):
        n = kblk.shape[0] // LANES
        am = select_mask(blk0, n)
        for h in range(A_HEADS):
            shift = mnear_scr[h]
            accumulate(h, [jnp.exp(s + a + b + shift) for s, a, b in
                           zip(logits(h, kblk), am, tile_cols(h, col0, n))], vblk)

    _sweep_far(q0, qb, kb, acc_far)
    pl.when(q0 > 0)(lambda: acc_near(far_k(prev_start, LANES), far_v(prev_start, LANES),
                                     cur_blk - 1, 0))
    acc_near(kc_ref[...], vc_ref[...], cur_blk, LANES)

    for h in range(A_HEADS):
        o_ref[:, h * hd:(h + 1) * hd] = (
            acc_scr[h] / jnp.sum(l_scr[h], axis=-1, keepdims=True)).astype(o_ref.dtype)


def _dsa_call(rel_bias, tiles, q, qi, iw, far, cur, *, qb, kb, qoff, topk, kv_pieces=None):
    b, t, _ = q.shape
    lf = far[2].shape[1]
    qbr = tiles.shape[2] - LANES
    q0_max = qoff + t - qb
    far_end = -(-q0_max // kb) * kb
    assert t % qb == 0 and qoff % LANES == 0 and far_end <= lf and cur[0].shape[1] == (t // qb) * qbr
    n_slabs = max(far_end, q0_max + qbr) // LANES
    mode = {"pipeline_mode": pl.Buffered(1)} if t // qb > 1 else {}
    far_spec = lambda a: pl.BlockSpec((None,) + a.shape[1:], lambda bi, i: (bi, 0, 0), **mode)
    cur_spec = lambda a: pl.BlockSpec((None, qbr, a.shape[2]), lambda bi, i: (bi, i, 0))
    return pl.pallas_call(
        functools.partial(_dsa_kernel, qb=qb, kb=kb, qoff=qoff, topk=topk, kv_pieces=kv_pieces),
        grid=(b, t // qb),
        in_specs=[pl.BlockSpec(memory_space=pltpu.SMEM),
                  pl.BlockSpec((None, qb, q.shape[2]), lambda bi, i: (bi, i, 0)),
                  pl.BlockSpec((None, qb, qi.shape[2]), lambda bi, i: (bi, i, 0)),
                  pl.BlockSpec((None, qb, iw.shape[2]), lambda bi, i: (bi, i, 0)),
                  far_spec(far[0]), far_spec(far[1]), far_spec(far[2]),
                  cur_spec(cur[0]), cur_spec(cur[1]), cur_spec(cur[2]),
                  _resident(tiles.shape, lambda bi, i: (0, 0, 0))],
        out_specs=pl.BlockSpec((None, qb, A_HEADS * A_HEAD_DIM), lambda bi, i: (bi, i, 0)),
        out_shape=jax.ShapeDtypeStruct((b, t, A_HEADS * A_HEAD_DIM), BF16),
        scratch_shapes=[pltpu.VMEM((n_slabs, qb, LANES), jnp.int32),
                        pltpu.VMEM((n_slabs, qb, LANES), jnp.int16),
                        pltpu.VMEM((TOP_LOW, qb, LANES), jnp.int16),
                        pltpu.VMEM((qb, LANES), jnp.int32),
                        pltpu.VMEM((A_HEADS, qb, LANES), F32),
                        pltpu.VMEM((A_HEADS, qb, LANES), F32),
                        pltpu.VMEM((A_HEADS, qb, LANES), F32),
                        pltpu.VMEM((A_HEADS, qb, A_HEAD_DIM), F32)],
        compiler_params=_params(2),
        name="mixer_a",
    )(rel_bias, q, qi, iw, *far, *cur, tiles)


def _diff_kernel(tab_ref, lam_ref, sg_ref, q_ref, k_ref, v_ref, kc_ref, vc_ref, tile_ref, o_ref,
                 mfar_scr, mnear_scr, l_scr, acc_scr, *, qb, kb, qoff, lam_init):
    head = pl.program_id(1)
    i = pl.program_id(2)
    q0 = qoff + i * qb
    hd = B_HEAD_DIM
    far_bias = tab_ref[FAR_BUCKET, A_HEADS + head]
    far_k = lambda start, width: _key_rows(k_ref, start, width)
    far_v = lambda start, width: _key_rows(v_ref, start, width)

    def logits(c, kblk):
        s = lax.dot_general(q_ref[:, c * hd:(c + 1) * hd], kblk[:, c * hd:(c + 1) * hd], NT_DIMS,
                            preferred_element_type=F32)
        return [s[:, u * LANES:(u + 1) * LANES] for u in range(s.shape[1] // LANES)]

    def tile_cols(col0, n):
        return [tile_ref[0, :, col0 + u * LANES:col0 + (u + 1) * LANES] for u in range(n)]

    mfar_scr[...] = jnp.full(mfar_scr.shape, NEG_BIG, F32)
    mnear_scr[...] = jnp.full(mnear_scr.shape, NEG_BIG, F32)

    def max_far(start, blk0, colmask):
        kblk = far_k(start, kb)
        for c in range(2):
            s = logits(c, kblk)
            if colmask is not None:
                s = [x + m for x, m in zip(s, colmask)]
            mfar_scr[c] = functools.reduce(jnp.maximum, s, mfar_scr[c])

    def max_near(kblk, col0):
        for c in range(2):
            mnear_scr[c] = functools.reduce(
                jnp.maximum, [s + b for s, b in zip(logits(c, kblk), tile_cols(col0, kblk.shape[0] // LANES))],
                mnear_scr[c])

    prev_start = pl.multiple_of(q0 - LANES, LANES)
    _sweep_far(q0, qb, kb, max_far)
    pl.when(q0 > 0)(lambda: max_near(far_k(prev_start, LANES), 0))
    max_near(kc_ref[...], LANES)

    for c in range(2):
        m = jnp.maximum(jnp.max(mfar_scr[c], axis=-1, keepdims=True) + far_bias,
                        jnp.max(mnear_scr[c], axis=-1, keepdims=True))
        mfar_scr[c] = jnp.broadcast_to(far_bias - m, (qb, LANES))
        mnear_scr[c] = jnp.broadcast_to(-m, (qb, LANES))
    l_scr[...] = jnp.zeros(l_scr.shape, F32)
    acc_scr[...] = jnp.zeros(acc_scr.shape, F32)

    def accumulate(c, p, vblk):
        l_scr[c] += functools.reduce(jnp.add, p)
        acc_scr[c] += jnp.dot(jnp.concatenate(p, axis=1).astype(BF16), vblk, preferred_element_type=F32)

    def acc_far(start, blk0, colmask):
        kblk = far_k(start, kb)
        vblk = far_v(start, kb)
        for c in range(2):
            shift = mfar_scr[c]
            shifts = [shift] * (kb // LANES) if colmask is None else [shift + m for m in colmask]
            accumulate(c, [jnp.exp(s + sh) for s, sh in zip(logits(c, kblk), shifts)], vblk)

    def acc_near(kblk, vblk, col0):
        for c in range(2):
            shift = mnear_scr[c]
            accumulate(c, [jnp.exp(s + b + shift) for s, b in
                           zip(logits(c, kblk), tile_cols(col0, kblk.shape[0] // LANES))], vblk)

    _sweep_far(q0, qb, kb, acc_far)
    pl.when(q0 > 0)(lambda: acc_near(far_k(prev_start, LANES), far_v(prev_start, LANES), 0))
    acc_near(kc_ref[...], vc_ref[...], LANES)

    lam = lam_ref[0, 0]
    o = (acc_scr[0] / jnp.sum(l_scr[0], axis=-1, keepdims=True)
         - lam * (acc_scr[1] / jnp.sum(l_scr[1], axis=-1, keepdims=True)))
    o = o * lax.rsqrt(jnp.mean(o * o, axis=-1, keepdims=True) + EPS) * sg_ref[...]
    o_ref[...] = (o * (1.0 - lam_init)).astype(o_ref.dtype)


def _diff_call(rel_bias, tiles, lam, subln_g, q, far, cur, *, qb, kb, qoff, lam_init):
    b, t, _ = q.shape
    lf = far[0].shape[1]
    hw = 2 * B_HEAD_DIM
    qbr = tiles.shape[2] - LANES
    q0_max = qoff + t - qb
    far_end = -(-max(q0_max - LANES, 0) // kb) * kb
    assert t % qb == 0 and qoff % LANES == 0 and far_end <= lf and cur[0].shape[1] == (t // qb) * qbr
    far_spec = pl.BlockSpec((None, lf, hw), lambda bi, h, i: (bi, 0, h))
    cur_spec = pl.BlockSpec((None, qbr, hw), lambda bi, h, i: (bi, i, h))
    return pl.pallas_call(
        functools.partial(_diff_kernel, qb=qb, kb=kb, qoff=qoff, lam_init=lam_init),
        grid=(b, B_HEADS, t // qb),
        in_specs=[pl.BlockSpec(memory_space=pltpu.SMEM),
                  pl.BlockSpec(memory_space=pltpu.SMEM),
                  pl.BlockSpec((1, hw), lambda bi, h, i: (0, 0)),
                  pl.BlockSpec((None, qb, hw), lambda bi, h, i: (bi, i, h)),
                  far_spec, far_spec, cur_spec, cur_spec,
                  pl.BlockSpec((1, qb, tiles.shape[2]), lambda bi, h, i: (A_HEADS + h, 0, 0))],
        out_specs=pl.BlockSpec((None, qb, hw), lambda bi, h, i: (bi, i, h)),
        out_shape=jax.ShapeDtypeStruct((b, t, B_HEADS * hw), BF16),
        scratch_shapes=[pltpu.VMEM((2, qb, LANES), F32),
                        pltpu.VMEM((2, qb, LANES), F32),
                        pltpu.VMEM((2, qb, LANES), F32),
                        pltpu.VMEM((2, qb, hw), F32)],
        compiler_params=_params(3),
        name="mixer_b",
    )(rel_bias, lam, subln_g.reshape(1, hw), q, *far, *cur, tiles)


def _lam_kernel(dl_ref, o_ref, *, lam_init):
    dl = dl_ref[...]
    s1 = jnp.sum(dl[0:1] * dl[1:2], axis=-1, keepdims=True)
    s2 = jnp.sum(dl[2:3] * dl[3:4], axis=-1, keepdims=True)
    o_ref[...] = jnp.exp(s1) - jnp.exp(s2) + lam_init


def _lam_call(diff_lam, lam_init):
    return pl.pallas_call(
        functools.partial(_lam_kernel, lam_init=lam_init),
        out_shape=jax.ShapeDtypeStruct((1, 1), F32),
        name="diff_lambda",
    )(diff_lam)


def _route(logits_t, rbias, counts):
    tm = logits_t.shape[1]
    per = N_EXPERTS // N_GROUPS
    scores = jax.nn.sigmoid(logits_t).reshape(N_GROUPS, per, tm)
    ch = scores + rbias.reshape(N_GROUPS, per, 1)
    sub = lax.broadcasted_iota(jnp.int32, ch.shape, 1)
    grp = lax.broadcasted_iota(jnp.int32, ch.shape, 0)
    m1 = jnp.max(ch, axis=1, keepdims=True)
    i1 = jnp.min(jnp.where(ch == m1, sub, per), axis=1, keepdims=True)
    m2 = jnp.max(jnp.where(sub == i1, -jnp.inf, ch), axis=1, keepdims=True)
    gs = jnp.broadcast_to(m1 + m2, ch.shape)
    rank = jnp.zeros(ch.shape, jnp.int32)
    for g2 in range(N_GROUPS):
        o = gs[g2:g2 + 1]
        rank = rank + jnp.where(o > gs, 1, 0) + jnp.where(o == gs, 1, 0) * jnp.where(grp > g2, 1, 0)
    cm = jnp.where(rank < TOPK_GROUPS, ch, -jnp.inf)
    eid = grp * per + sub
    erank = jnp.zeros(cm.shape, jnp.int32)
    for e2 in range(N_EXPERTS):
        o = cm[e2 // per:e2 // per + 1, e2 % per:e2 % per + 1, :]
        erank = erank + jnp.where(o > cm, 1, 0) + jnp.where(o == cm, 1, 0) * jnp.where(eid > e2, 1, 0)
    chosen = erank < TOP_K
    sel = jnp.where(chosen, scores, 0.0)
    denom = jnp.sum(jnp.sum(sel, axis=1, keepdims=True), axis=0, keepdims=True)
    gates = sel / denom * ROUTED_SCALE

    ones = jnp.where(chosen, 1.0, 0.0).reshape(N_EXPERTS, tm)
    earlier = (lax.broadcasted_iota(jnp.int32, (tm, tm), 0) < lax.broadcasted_iota(jnp.int32, (tm, tm), 1))
    before = jnp.dot(ones.astype(BF16), jnp.where(earlier, 1.0, 0.0).astype(BF16), preferred_element_type=F32)
    pos = (before + counts.astype(F32)).reshape(N_GROUPS, per, tm)
    new_counts = counts + jnp.sum(ones, axis=1, keepdims=True).astype(jnp.int32)

    def per_slot(values):
        rows = [jnp.sum(jnp.sum(jnp.where(erank == k, values, 0.0), axis=1, keepdims=True), axis=0)
                for k in range(TOP_K)]
        return jnp.concatenate(rows, axis=0)

    return (per_slot(eid.astype(F32)).astype(jnp.int32), per_slot(pos).astype(jnp.int32), per_slot(gates),
            new_counts)


def _pack_pair(a, b):
    ua = pltpu.bitcast(a.astype(BF16).astype(F32), jnp.uint32)
    ub = pltpu.bitcast(b.astype(BF16).astype(F32), jnp.uint32)
    return pltpu.bitcast(ua | (ub >> 16), jnp.int32)


def _unpack_pair(p):
    u = pltpu.bitcast(p, jnp.uint32)
    a = pltpu.bitcast(u & jnp.uint32(0xFFFF0000), F32)
    b = pltpu.bitcast(u << 16, F32)
    return a.astype(BF16), b.astype(BF16)


PARTS = 4
PART_W = D_MODEL // 2 // PARTS


def _pack_rows(x):
    half = x.shape[1] // 2
    return jnp.stack([_pack_pair(x[:, j * PART_W:(j + 1) * PART_W],
                                 x[:, half + j * PART_W:half + (j + 1) * PART_W]) for j in range(PARTS)])


def _packed_matmul(parts_ref, w_ref):
    half = w_ref.shape[0] // 2
    acc = None
    for j in range(PARTS):
        a, b = _unpack_pair(parts_ref[j])
        t = (jnp.dot(a, w_ref[j * PART_W:(j + 1) * PART_W, :], preferred_element_type=F32)
             + jnp.dot(b, w_ref[half + j * PART_W:half + (j + 1) * PART_W, :], preferred_element_type=F32))
        acc = t if acc is None else acc + t
    return acc


def _outproj_kernel(x_ref, oa_ref, ob_ref, ga_ref, sc_ref, sh_ref, g_ref, w_ref, wr_ref, rb_ref, cin_ref,
                    x1_ref, h2_ref, ek_ref, pk_ref, gk_ref, cout_ref, cnt_scr):
    @pl.when(pl.program_id(0) == 0)
    def _():
        cnt_scr[...] = cin_ref[...]

    half = oa_ref.shape[1]
    mix = (jnp.dot(oa_ref[...], w_ref[:half, :], preferred_element_type=F32)
           + jnp.dot(ob_ref[...], w_ref[half:, :], preferred_element_type=F32))
    x1 = x_ref[...] + ga_ref[...] * mix
    x1_ref[...] = x1
    h2 = _modulated_norm(x1, g_ref[...], sc_ref[...], sh_ref[...])
    h2_ref[...] = _pack_rows(h2)
    logits_t = lax.dot_general(wr_ref[...], h2, NT_DIMS, preferred_element_type=F32,
                               precision=lax.Precision.HIGHEST)
    ek, pk, gk, counts = _route(logits_t, rb_ref[...], cnt_scr[...])
    ek_ref[...] = ek
    pk_ref[...] = pk
    gk_ref[...] = gk
    cnt_scr[...] = counts
    cout_ref[...] = counts


def _outproj_call(x2d, oa, ob, ga, sc, sh, mod_map, g, w_bf, wr_t, rbias, counts_in, tm):
    n, d = x2d.shape
    assert n % tm == 0
    r = sc.shape[1]
    mod_spec = pl.BlockSpec((None, r, d), mod_map)
    slot_spec = pl.BlockSpec((TOP_K, tm), lambda i: (0, i))
    return pl.pallas_call(
        _outproj_kernel,
        grid=(n // tm,),
        in_specs=[pl.BlockSpec((tm, d), lambda i: (i, 0)),
                  pl.BlockSpec((tm, oa.shape[1]), lambda i: (i, 0)),
                  pl.BlockSpec((tm, ob.shape[1]), lambda i: (i, 0)),
                  mod_spec, mod_spec, mod_spec,
                  pl.BlockSpec((1, d), lambda i: (0, 0)),
                  _resident(w_bf.shape, lambda i: (0, 0)),
                  _resident(wr_t.shape, lambda i: (0, 0)),
                  pl.BlockSpec((N_EXPERTS, 1), lambda i: (0, 0)),
                  pl.BlockSpec((N_EXPERTS, 1), lambda i: (0, 0))],
        out_specs=[pl.BlockSpec((tm, d), lambda i: (i, 0)),
                   pl.BlockSpec((PARTS, tm, PART_W), lambda i: (0, i, 0)),
                   slot_spec, slot_spec, slot_spec,
                   pl.BlockSpec((N_EXPERTS, 1), lambda i: (0, 0))],
        out_shape=[jax.ShapeDtypeStruct((n, d), F32),
                   jax.ShapeDtypeStruct((PARTS, n, PART_W), jnp.int32),
                   jax.ShapeDtypeStruct((TOP_K, n), jnp.int32),
                   jax.ShapeDtypeStruct((TOP_K, n), jnp.int32),
                   jax.ShapeDtypeStruct((TOP_K, n), F32),
                   jax.ShapeDtypeStruct((N_EXPERTS, 1), jnp.int32)],
        scratch_shapes=[pltpu.VMEM((N_EXPERTS, 1), jnp.int32)],
        compiler_params=_params(1),
        name="out_proj_router",
    )(x2d, oa, ob, ga, sc, sh, g.reshape(1, d), w_bf, wr_t, rbias, counts_in)


SC_WINDOW = 128
EXPERT_TILE = 512


def _sc_mesh():
    return plsc.VectorSubcoreMesh(core_axis_name="core", subcore_axis_name="subcore")


def _sc_scatter_rows(x, idx, n_out):
    n, w = x.shape
    steps = n // SC_WINDOW
    total = idx.shape[1] // SC_WINDOW

    @pl.kernel(out_type=jax.ShapeDtypeStruct((n_out, w), x.dtype), mesh=_sc_mesh(), scratch_types=[],
               name="moe_dispatch")
    def scatter(x_hbm, i_hbm, o_hbm):
        def body(x_vmem, i_vmem):
            pltpu.sync_copy(x_vmem, o_hbm.at[i_vmem.at[0]])

        pltpu.emit_pipeline(
            body, grid=(total,),
            in_specs=[pl.BlockSpec((SC_WINDOW, w), lambda i: (i % steps, 0)),
                      pl.BlockSpec((1, SC_WINDOW), lambda i: (0, i))],
            out_specs=[], core_axis_name=("core", "subcore"),
            dimension_semantics=(pltpu.PARALLEL,))(x_hbm, i_hbm)

    return scatter(x, idx)


def _sc_gather_rows(y, idx):
    w = y.shape[1]
    n = idx.shape[1]

    @pl.kernel(out_type=jax.ShapeDtypeStruct((n, w), y.dtype), mesh=_sc_mesh(), scratch_types=[],
               name="moe_collect")
    def gather(y_hbm, i_hbm, o_hbm):
        def body(i_vmem, o_vmem):
            pltpu.sync_copy(y_hbm.at[i_vmem.at[0]], o_vmem)

        pltpu.emit_pipeline(
            body, grid=(n // SC_WINDOW,),
            in_specs=[pl.BlockSpec((1, SC_WINDOW), lambda i: (0, i))],
            out_specs=[pl.BlockSpec((SC_WINDOW, w), lambda i: (i, 0))],
            core_axis_name=("core", "subcore"),
            dimension_semantics=(pltpu.PARALLEL,))(i_hbm, o_hbm)

    return gather(y, idx)


def _swiglu(parts_ref, wg_ref, wu_ref, wd_ref):
    a = _packed_matmul(parts_ref, wg_ref)
    u = _packed_matmul(parts_ref, wu_ref)
    hid = (a * jax.nn.sigmoid(a) * u).astype(BF16)
    return jnp.dot(hid, wd_ref[...], preferred_element_type=F32)


def _expert_kernel(te_ref, nu_ref, x_ref, wg_ref, wu_ref, wd_ref, y_ref, wg_scr, wu_scr, wd_scr):
    i = pl.program_id(0)

    @pl.when(i < nu_ref[0])
    def _():
        @pl.when((i == 0) | (te_ref[i] != te_ref[jnp.maximum(i - 1, 0)]))
        def _():
            wg_scr[...] = wg_ref[...].astype(BF16)
            wu_scr[...] = wu_ref[...].astype(BF16)
            wd_scr[...] = wd_ref[...].astype(BF16)

        y_ref[...] = _pack_rows(_swiglu(x_ref, wg_scr, wu_scr, wd_scr))


def _expert_call(tile_expert, n_used, x_sorted, wg, wu, wd):
    _, r, _ = x_sorted.shape
    ne, d, de = wg.shape
    n_tiles = r // EXPERT_TILE
    row_map = lambda i, te, nu: (0, jnp.minimum(i, nu[0] - 1), 0)
    return pl.pallas_call(
        _expert_kernel,
        grid_spec=pltpu.PrefetchScalarGridSpec(
            num_scalar_prefetch=2,
            grid=(n_tiles,),
            in_specs=[pl.BlockSpec((PARTS, EXPERT_TILE, PART_W), row_map),
                      pl.BlockSpec((None, d, de), lambda i, te, nu: (te[i], 0, 0)),
                      pl.BlockSpec((None, d, de), lambda i, te, nu: (te[i], 0, 0)),
                      pl.BlockSpec((None, de, d), lambda i, te, nu: (te[i], 0, 0))],
            out_specs=pl.BlockSpec((PARTS, EXPERT_TILE, PART_W), row_map),
            scratch_shapes=[pltpu.VMEM((d, de), BF16), pltpu.VMEM((d, de), BF16), pltpu.VMEM((de, d), BF16)]),
        out_shape=jax.ShapeDtypeStruct(x_sorted.shape, jnp.int32),
        compiler_params=_params(1),
        name="moe_experts",
    )(tile_expert, n_used, x_sorted, wg, wu, wd)


def _final_kernel(x_ref, h_ref, rows_ref, gk_ref, gf_ref, g_ref, sg_ref, su_ref, sd_ref, o_ref):
    moe = _swiglu(h_ref, sg_ref, su_ref, sd_ref)
    half = moe.shape[1] // 2
    lo = [moe[:, j * PART_W:(j + 1) * PART_W] for j in range(PARTS)]
    hi = [moe[:, half + j * PART_W:half + (j + 1) * PART_W] for j in range(PARTS)]
    for k in range(TOP_K):
        gate = gk_ref[:, k:k + 1]
        for j in range(PARTS):
            a, b = _unpack_pair(rows_ref[k, j])
            lo[j] = lo[j] + gate * a.astype(F32)
            hi[j] = hi[j] + gate * b.astype(F32)
    x = x_ref[...] + gf_ref[...] * jnp.concatenate(lo + hi, axis=1)
    o_ref[...] = x * lax.rsqrt(jnp.mean(x * x, axis=-1, keepdims=True) + EPS) * g_ref[...]


def _final_call(x1, h2p, rows, row_block0, gk_t, gf, mod_map, g, sg, su, sd, tm):
    n, d = x1.shape
    assert n % tm == 0
    r = gf.shape[1]
    return pl.pallas_call(
        _final_kernel,
        grid=(n // tm,),
        in_specs=[pl.BlockSpec((tm, d), lambda i: (i, 0)),
                  pl.BlockSpec((PARTS, tm, PART_W), lambda i: (0, i, 0)),
                  pl.BlockSpec((TOP_K, PARTS, tm, PART_W), lambda i: (0, 0, i + row_block0, 0)),
                  pl.BlockSpec((tm, TOP_K), lambda i: (i, 0)),
                  pl.BlockSpec((None, r, d), mod_map),
                  pl.BlockSpec((1, d), lambda i: (0, 0)),
                  _resident(sg.shape, lambda i: (0, 0)),
                  _resident(su.shape, lambda i: (0, 0)),
                  _resident(sd.shape, lambda i: (0, 0))],
        out_specs=pl.BlockSpec((tm, d), lambda i: (i, 0)),
        out_shape=jax.ShapeDtypeStruct((n, d), F32),
        compiler_params=_params(1),
        name="final_norm",
    )(x1, h2p, rows, gk_t, gf, g.reshape(1, d), sg, su, sd)


_QA, _KA, _VA, _QI, _KI, _IW, _QB, _KB, _VB = range(9)
_IN_SIZES = (A_HEADS * A_HEAD_DIM, A_KV_HEADS * A_HEAD_DIM, A_KV_HEADS * A_HEAD_DIM,
             IDX_HEADS * IDX_DIM, IDX_DIM, IDX_HEADS,
             B_HEADS * 2 * B_HEAD_DIM, B_HEADS * 2 * B_HEAD_DIM, B_HEADS * 2 * B_HEAD_DIM)


def _split_w_in(w_in):
    d = w_in.shape[0]
    offs = np.concatenate([[0], np.cumsum(_IN_SIZES)])
    cols = [w_in[:, offs[k]:offs[k + 1]].astype(BF16) for k in range(9)]
    zero = lambda n: jnp.zeros((d, n), BF16)
    qi = jnp.concatenate([cols[_QI].reshape(d, IDX_HEADS, IDX_DIM),
                          jnp.zeros((d, IDX_HEADS, LANES - IDX_DIM), BF16)], axis=-1).reshape(d, IDX_HEADS * LANES)
    w_a = jnp.concatenate([cols[_QA], cols[_KA], cols[_VA], qi,
                           cols[_KI], zero(LANES - IDX_DIM), cols[_IW], zero(LANES - IDX_HEADS)], axis=1)
    w_b = jnp.concatenate([cols[_QB], cols[_KB], cols[_VB]], axis=1)
    return w_a, w_b


def _group_forward(x, mod, past, rel_bias, lam, weights, *, lam_init, qb, qb_b, kb, tm, per_row_mod):
    (norm_a_g, w_a, w_b, subln_g) = weights
    b, t, d = x.shape
    n = b * t
    x2d = x.reshape(n, d)
    sh_a, sc_a = mod[0], mod[1]
    if per_row_mod:
        expand = lambda m: jnp.repeat(m, t, axis=0).reshape(1, n, d)
        mod_map = lambda i: (0, i, 0)
    else:
        expand = lambda m: m.reshape(b, 1, d)
        mod_map = lambda i: ((i * tm) // t, 0, 0)
    sc3, sh3 = expand(sc_a), expand(sh_a)

    na = A_HEADS * A_HEAD_DIM
    nk = A_KV_HEADS * A_HEAD_DIM
    nq = IDX_HEADS * LANES
    c_ka, c_va, c_qi = na, na + nk, na + 2 * nk
    c_ki, c_iw = c_qi + nq, c_qi + nq + LANES
    kv_order = tuple(range(A_KV_HEADS))
    defs_a = [(0, na, na, A_HEAD_DIM ** -0.5, None), (c_ka, nk, nk, None, kv_order), (c_ka, nk, nk, None, None),
              (c_va, nk, nk, None, kv_order), (c_va, nk, nk, None, None), (c_qi, nq, nq, None, None),
              (c_ki, LANES, IDX_DIM, None, None), (c_ki, LANES, LANES, None, None), (c_iw, LANES, LANES, None, None)]
    dt_a = [BF16, F32, BF16, F32, BF16, BF16, F32, BF16, F32]
    qa, ka, ka_bf, va, va_bf, qi, ki, ki_bf, iw = _norm_proj_call(
        x2d, sc3, sh3, mod_map, norm_a_g, w_a, defs_a, dt_a, tm, "in_proj_a")
    nb = B_HEADS * 2 * B_HEAD_DIM
    kb_order = tuple(range(2 * B_HEADS))
    vb_order = tuple(2 * h + half for half in range(2) for h in range(B_HEADS))
    defs_b = [(0, nb, nb, B_HEAD_DIM ** -0.5, None), (nb, nb, nb, None, kb_order), (nb, nb, nb, None, None),
              (2 * nb, nb, nb, None, vb_order), (2 * nb, nb, nb, None, None)]
    dt_b = [BF16, F32, BF16, F32, BF16]
    qbm, kbm, kb_bf, vbm, vb_bf = _norm_proj_call(
        x2d, sc3, sh3, mod_map, norm_a_g, w_b, defs_b, dt_b, tm, "in_proj_b")

    r3 = lambda a: a.reshape(b, t, a.shape[-1])
    cur = [r3(ka_bf), r3(va_bf), r3(ki_bf), r3(kb_bf), r3(vb_bf)]
    kv_pieces = None
    if past is None:
        qoff = 0
        far = cur
    else:
        assert t == qb, "a group with cached keys is one query block per batch row"
        qoff = past[0].shape[1]
        qbr = -(-qb // LANES) * LANES
        cur = [jnp.concatenate([a, jnp.zeros((b, qbr - t, a.shape[-1]), BF16)], axis=1) for a in cur]
        pk, pv, pi, pbk, pbv = past
        if qoff % kb == 0:
            kv_pieces = (A_KV_HEADS, tuple(range(A_KV_HEADS)))
            far = [pk.reshape(b, qoff * A_KV_HEADS, A_HEAD_DIM), pv.reshape(b, qoff * A_KV_HEADS, A_HEAD_DIM),
                   pi.reshape(b, qoff, -1), pbk.reshape(b, qoff, -1), pbv.reshape(b, qoff, -1)]
        else:
            far = [jnp.concatenate([p.reshape(b, qoff, -1), jnp.zeros((b, -qoff % kb, p[0, 0].size), p.dtype)], axis=1)
                   for p in past]
    length = qoff + t
    topk = min(TOPK_MAX, length // 4)

    tiles = _bias_tile_call(rel_bias, qb)
    tiles_b = tiles if qb_b == qb else _bias_tile_call(rel_bias, qb_b)
    out_a = _dsa_call(rel_bias, tiles, r3(qa), r3(qi), r3(iw), far[:3], cur[:3], qb=qb, kb=kb, qoff=qoff, topk=topk,
                      kv_pieces=kv_pieces)
    out_b = _diff_call(rel_bias, tiles_b, lam, subln_g, r3(qbm), far[3:], cur[3:], qb=qb_b, kb=kb, qoff=qoff,
                       lam_init=lam_init)
    rows = (ka.reshape(1, b, t, A_KV_HEADS, A_HEAD_DIM), va.reshape(1, b, t, A_KV_HEADS, A_HEAD_DIM),
            ki.reshape(1, b, t, IDX_DIM), kbm.reshape(1, b, t, B_HEADS, 2, B_HEAD_DIM),
            vbm.reshape(b, t, 2, B_HEADS, B_HEAD_DIM).transpose(0, 1, 3, 2, 4)
               .reshape(1, b, t, B_HEADS, 2 * B_HEAD_DIM))
    return x2d, out_a.reshape(n, -1), out_b.reshape(n, -1), rows, expand, mod_map


def kernel(x_prompt, x_sample, c_prompt, c_sample, cache_a_k, cache_a_v, cache_a_kidx, cache_b_k, cache_b_v,
           rel_bias, w_ada, b_ada, norm_a_g, w_in, w_out, diff_lam, subln_g, norm_f_g, w_router, router_bias,
           w_gate, w_up, w_down, ws_gate, ws_up, ws_down, final_g):
    assert w_ada.shape[0] == 1, "single-layer model"
    d = x_prompt.shape[-1]
    bp, tp, _ = x_prompt.shape
    bs, ts, _ = x_sample.shape
    lam_init = 0.8 - 0.6 * math.exp(-0.3 * 0)

    c_all = jnp.concatenate([c_prompt, c_sample], axis=0)
    rows = -(-c_all.shape[0] // 8) * 8
    c_pad = jnp.concatenate([c_all, jnp.zeros((rows - c_all.shape[0], d), F32)], axis=0)
    mod_all = _ada_call(c_pad, w_ada[0], b_ada[0])
    mod_p = [mod_all[:bp, k * d:(k + 1) * d] for k in range(6)]
    mod_s = [mod_all[bp:bp + bs, k * d:(k + 1) * d] for k in range(6)]

    w_a, w_b = _split_w_in(w_in[0])
    lam = _lam_call(diff_lam[0], lam_init)
    weights = (norm_a_g[0], w_a, w_b, subln_g[0])
    past = (cache_a_k[0], cache_a_v[0], cache_a_kidx[0], cache_b_k[0], cache_b_v[0])

    tm_p = 256
    tm_s = bs * ts
    xp2, oa_p, ob_p, rows_p, exp_p, map_p = _group_forward(
        x_prompt, mod_p, None, rel_bias, lam, weights, lam_init=lam_init, qb=256, qb_b=512, kb=512, tm=tm_p,
        per_row_mod=False)
    xs2, oa_s, ob_s, rows_s, exp_s, map_s = _group_forward(
        x_sample, mod_s, past, rel_bias, lam, weights, lam_init=lam_init, qb=ts, qb_b=ts, kb=1024, tm=tm_s,
        per_row_mod=True)

    w_out_bf = w_out[0].astype(BF16)
    wr_t = w_router[0].T
    rb = router_bias[0].reshape(N_EXPERTS, 1)
    no_rows = jnp.zeros((N_EXPERTS, 1), jnp.int32)
    x1_p, h2_p, ek_p, pk_p, gk_p, cnt_p = _outproj_call(
        xp2, oa_p, ob_p, exp_p(mod_p[2]), exp_p(mod_p[4]), exp_p(mod_p[3]), lambda i: map_p(2 * i), norm_f_g[0],
        w_out_bf, wr_t, rb, no_rows, 2 * tm_p)
    x1_s, h2_s, ek_s, pk_s, gk_s, cnt_all = _outproj_call(
        xs2, oa_s, ob_s, exp_s(mod_s[2]), exp_s(mod_s[4]), exp_s(mod_s[3]), map_s, norm_f_g[0], w_out_bf, wr_t, rb,
        cnt_p, tm_s)

    n_p, n_s = bp * tp, bs * ts
    n_all = n_p + n_s
    n_rows = -(-(n_all * TOP_K) // EXPERT_TILE) * EXPERT_TILE + N_EXPERTS * EXPERT_TILE
    counts = cnt_all[:, 0]
    padded = (counts + EXPERT_TILE - 1) // EXPERT_TILE * EXPERT_TILE
    ends = jnp.cumsum(padded)
    starts = ends - padded
    tile_first_row = jnp.arange(n_rows // EXPERT_TILE, dtype=jnp.int32) * EXPERT_TILE
    tile_expert = jnp.minimum(jnp.sum(ends[None, :] <= tile_first_row[:, None], axis=1), N_EXPERTS - 1).astype(jnp.int32)
    n_used = (ends[-1:] // EXPERT_TILE).astype(jnp.int32)
    ek = jnp.concatenate([ek_p, ek_s], axis=1)
    expert_ids = jnp.arange(N_EXPERTS, dtype=jnp.int32)[:, None, None]
    dest = jnp.sum(jnp.where(ek[None] == expert_ids, starts[:, None, None], 0), axis=0) \
        + jnp.concatenate([pk_p, pk_s], axis=1)
    piece = jnp.arange(PARTS, dtype=jnp.int32)[None, :, None]
    idx = (dest[:, None, :] + piece * n_rows).reshape(1, TOP_K * PARTS * n_all).astype(jnp.int32)

    h2_all = jnp.concatenate([h2_p, h2_s], axis=1).reshape(PARTS * n_all, PART_W)
    x_sorted = _sc_scatter_rows(h2_all, idx, PARTS * n_rows).reshape(PARTS, n_rows, PART_W)
    y_sorted = _expert_call(tile_expert, n_used, x_sorted, w_gate[0], w_up[0], w_down[0])
    rows = _sc_gather_rows(y_sorted.reshape(PARTS * n_rows, PART_W), idx).reshape(TOP_K, PARTS, n_all, PART_W)

    shared = (ws_gate[0].astype(BF16), ws_up[0].astype(BF16), ws_down[0].astype(BF16))
    assert n_p % tm_s == 0
    y_p = _final_call(x1_p, h2_p, rows, 0, gk_p.T, exp_p(mod_p[5]), map_p, final_g, *shared, tm_p)
    y_s = _final_call(x1_s, h2_s, rows, n_p // tm_s, gk_s.T, exp_s(mod_s[5]), map_s, final_g, *shared, tm_s)
    return (y_p.reshape(bp, tp, d), y_s.reshape(bs, ts, d)) + rows_p + rows_s
```

```python
import functools
import math

import numpy as np
import jax
import jax.numpy as jnp
from jax import lax
from jax.experimental import pallas as pl
from jax.experimental.pallas import tpu as pltpu
from jax.experimental.pallas import tpu_sc as plsc

D_MODEL = 2048
CHUNK = 64
A_HEADS = 8
A_KV_HEADS = 2
A_REP = A_HEADS // A_KV_HEADS
A_HEAD_DIM = 128
IDX_HEADS = 16
IDX_DIM = 64
TOPK_MAX = 256
B_HEADS = 4
B_HEAD_DIM = 128
N_REL_BUCKETS = 32
REL_MAX_DIST = 128
N_ATTN_HEADS = A_HEADS + B_HEADS
N_EXPERTS = 64
TOP_K = 8
N_GROUPS = 8
TOPK_GROUPS = 4
EXPERT_DIM = 512
SHARED_DIM = 512
ROUTED_SCALE = 2.5
EPS = 1e-6

LANES = 128
INT_MIN = -(2 ** 31)
NEG_BIG = -1e30
TOP_LOW = 3
VMEM_LIMIT = 52 * 1024 * 1024

F32 = jnp.float32
BF16 = jnp.bfloat16
NT_DIMS = (((1,), (1,)), ((), ()))


def _params(n_axes):
    return pltpu.CompilerParams(dimension_semantics=("arbitrary",) * n_axes,
                                vmem_limit_bytes=VMEM_LIMIT)


def _resident(shape, index_map):
    return pl.BlockSpec(shape, index_map, pipeline_mode=pl.Buffered(1))


def _bucket_thresholds():
    nb = N_REL_BUCKETS // 2
    max_exact = nb // 2
    n = np.arange(max_exact, 4 * REL_MAX_DIST, dtype=np.int32)
    nf = n.astype(np.float32)
    large = max_exact + (np.log(nf / np.float32(max_exact)) / np.float32(math.log(REL_MAX_DIST / max_exact))
                         * np.float32(nb - max_exact)).astype(np.int32)
    large = np.minimum(large, nb - 1)
    assert np.all(np.diff(large) >= 0)
    ths = [int(n[np.argmax(large >= b)]) for b in range(max_exact + 1, nb)]
    return max_exact, nb, ths


_MAX_EXACT, _NB, _BUCKET_THS = _bucket_thresholds()
assert _BUCKET_THS[-1] <= LANES + 1
FAR_BUCKET = _NB - 1


def _ada_kernel(c_ref, w_ref, b_ref, o_ref):
    c = c_ref[...]
    s = c * jax.nn.sigmoid(c)
    o_ref[...] = jnp.dot(s, w_ref[...], preferred_element_type=F32,
                         precision=lax.Precision.HIGHEST) + b_ref[...]


def _ada_call(c, w_ada, b_ada):
    rows, d = c.shape
    n = w_ada.shape[1]
    tn = 1536
    assert n % tn == 0
    return pl.pallas_call(
        _ada_kernel,
        grid=(n // tn,),
        in_specs=[pl.BlockSpec((rows, d), lambda j: (0, 0)),
                  pl.BlockSpec((d, tn), lambda j: (0, j)),
                  pl.BlockSpec((1, tn), lambda j: (0, j))],
        out_specs=pl.BlockSpec((rows, tn), lambda j: (0, j)),
        out_shape=jax.ShapeDtypeStruct((rows, n), F32),
        compiler_params=_params(1),
        name="ada_mod",
    )(c, w_ada, b_ada.reshape(1, n))


def _modulated_norm(x, g, sc, sh):
    y = x * lax.rsqrt(jnp.mean(x * x, axis=-1, keepdims=True) + EPS) * g
    return y * (1.0 + sc) + sh


def _norm_proj_kernel(x_ref, sc_ref, sh_ref, g_ref, w_ref, *out_refs, defs):
    h = _modulated_norm(x_ref[...], g_ref[...], sc_ref[...], sh_ref[...]).astype(BF16)
    done = {}
    tm = x_ref.shape[0]
    for (start, width, store_w, mult, order), o_ref in zip(defs, out_refs):
        if (start, width) not in done:
            done[(start, width)] = jnp.dot(h, w_ref[:, start:start + width], preferred_element_type=F32)
        r = done[(start, width)][:, :store_w]
        if order is None:
            o_ref[...] = (r if mult is None else r * mult).astype(o_ref.dtype)
        else:
            for p, cb in enumerate(order):
                o_ref[pl.ds(p, tm, stride=len(order)), :] = r[:, cb * LANES:(cb + 1) * LANES].astype(o_ref.dtype)


def _norm_proj_call(x2d, sc, sh, mod_map, g, w_bf, defs, dtypes, tm, name):
    n, d = x2d.shape
    assert n % tm == 0
    r = sc.shape[1]
    shape = lambda rows, d_: (rows, d_[2]) if d_[4] is None else (rows * len(d_[4]), LANES)
    out_shapes = [jax.ShapeDtypeStruct(shape(n, d_), dt) for d_, dt in zip(defs, dtypes)]
    out_specs = [pl.BlockSpec(shape(tm, d_), lambda i: (i, 0)) for d_ in defs]
    return pl.pallas_call(
        functools.partial(_norm_proj_kernel, defs=tuple(defs)),
        grid=(n // tm,),
        in_specs=[pl.BlockSpec((tm, d), lambda i: (i, 0)),
                  pl.BlockSpec((None, r, d), mod_map),
                  pl.BlockSpec((None, r, d), mod_map),
                  pl.BlockSpec((1, d), lambda i: (0, 0)),
                  _resident(w_bf.shape, lambda i: (0, 0))],
        out_specs=out_specs,
        out_shape=out_shapes,
        compiler_params=_params(1),
        name=name,
    )(x2d, sc, sh, g.reshape(1, d), w_bf)


def _bias_tile_kernel(tab_ref, o_ref, *, qb):
    _, rows, width = o_ref.shape
    r = lax.broadcasted_iota(jnp.int32, (rows, width), 0)
    koff = lax.broadcasted_iota(jnp.int32, (rows, width), 1) - LANES
    rel = koff - r
    n = jnp.abs(rel)
    large = jnp.full((rows, width), _MAX_EXACT, jnp.int32)
    for th in _BUCKET_THS:
        large = large + jnp.where(n >= th, 1, 0)
    bucket = jnp.where(rel > 0, _NB, 0) + jnp.where(n < _MAX_EXACT, n, large)
    visible = ((koff >> 6) <= (r >> 6)) & (koff < qb)
    for h in range(N_ATTN_HEADS):
        b = jnp.zeros((rows, width), F32)
        for bk in range(N_REL_BUCKETS):
            b = jnp.where(bucket == bk, tab_ref[bk, h], b)
        o_ref[h] = jnp.where(visible, b, NEG_BIG)


def _bias_tile_call(rel_bias, qb):
    qbr = -(-qb // LANES) * LANES
    return pl.pallas_call(
        functools.partial(_bias_tile_kernel, qb=qb),
        in_specs=[pl.BlockSpec(memory_space=pltpu.SMEM)],
        out_specs=pl.BlockSpec(memory_space=pltpu.VMEM),
        out_shape=jax.ShapeDtypeStruct((N_ATTN_HEADS, qb, LANES + qbr), F32),
        name="bias_tiles",
    )(rel_bias)


def _sweep_far(q0, qb, kb, fn):
    far_end = jnp.maximum(q0 - LANES, 0)
    n_full = far_end // kb
    sub = kb // LANES

    def whole(j):
        fn(pl.multiple_of(j * kb, kb), j * sub, None)

    def pair(j, carry):
        whole(2 * j)
        whole(2 * j + 1)
        return carry

    lax.fori_loop(0, n_full // 2, pair, 0)
    pl.when(n_full % 2 == 1)(lambda: whole(n_full - 1))

    @pl.when(far_end > n_full * kb)
    def _():
        start = pl.multiple_of(n_full * kb, kb)
        col = start + lax.broadcasted_iota(jnp.int32, (qb, LANES), 1)
        fn(start, n_full * sub, [jnp.where(col + c * LANES < far_end, 0.0, NEG_BIG) for c in range(sub)])


def _key_rows(ref, start, width, pieces=None):
    if pieces is None:
        blk = ref[pl.ds(start, width), :].astype(BF16)
        pad = -blk.shape[1] % LANES
        return blk if pad == 0 else jnp.concatenate([blk, jnp.zeros((width, pad), BF16)], axis=1)
    r, rows = pieces
    window = ref.at[pl.ds(pl.multiple_of(start * r, LANES * r), width * r), :]
    return jnp.concatenate([window[pl.ds(p, width, stride=r), :].astype(BF16) for p in rows], axis=1)


def _dsa_kernel(tab_ref, q_ref, qi_ref, iw_ref, k_ref, v_ref, ki_ref, kc_ref, vc_ref, kic_ref, tile_ref, o_ref,
                keys_scr, half_scr, best_scr, thr_scr, mfar_scr, mnear_scr, l_scr, acc_scr, *, qb, kb, qoff, topk,
                kv_pieces):
    far_k = lambda start, width: _key_rows(k_ref, start, width, kv_pieces)
    far_v = lambda start, width: _key_rows(v_ref, start, width, kv_pieces)
    i = pl.program_id(1)
    q0 = qoff + i * qb
    qbr = tile_ref.shape[2] - LANES
    sub = kb // LANES
    w = iw_ref[:, :IDX_HEADS] * (IDX_HEADS ** -0.5 * IDX_DIM ** -0.5)

    def score_keys(kib):
        width = kib.shape[0]
        acc = jnp.zeros((qb, width), F32)
        for h in range(IDX_HEADS):
            d = lax.dot_general(qi_ref[:, h * LANES:(h + 1) * LANES], kib, NT_DIMS,
                                preferred_element_type=F32)
            acc = acc + w[:, h:h + 1] * jnp.maximum(d, 0.0)
        bits = pltpu.bitcast(acc, jnp.int32)
        return bits ^ ((bits >> 31) & 0x7FFFFFFF)

    n_score = (q0 + kb - 1) // kb

    def store_keys(slab0, key):
        for c in range(key.shape[1] // LANES):
            part = key[:, c * LANES:(c + 1) * LANES]
            keys_scr[slab0 + c] = part
            half_scr[slab0 + c] = (part >> 16).astype(jnp.int16)

    def score_body(j, carry):
        store_keys(j * sub, score_keys(_key_rows(ki_ref, pl.multiple_of(j * kb, kb), kb)))
        return carry

    lax.fori_loop(0, n_score, score_body, 0)

    cur_blk = q0 // LANES
    key = score_keys(kic_ref[...])
    store_keys(cur_blk, jnp.where(tile_ref[0, :, LANES:] > 0.5 * NEG_BIG, key, INT_MIN))
    n_blk = cur_blk + qbr // LANES

    rb = min(qb, LANES)
    lo16 = -(2 ** 15)
    n4 = n_blk // 4

    chunks = [pl.ds(c * rb, rb) for c in range(qb // rb)]
    ones = jnp.ones((LANES, LANES), BF16)

    def count_ge(ts):
        tbs = [t.astype(jnp.int16) for t in ts]

        def count(j, cnts):
            return tuple(cnt + jnp.where(half_scr[j, rows, :] >= tb, jnp.int16(1), jnp.int16(0))
                         for cnt, rows, tb in zip(cnts, chunks, tbs))

        def body4(j, cnts):
            for u in range(4):
                cnts = count(4 * j + u, cnts)
            return cnts

        cnts = lax.fori_loop(0, n4, body4, tuple(jnp.zeros((rb, LANES), jnp.int16) for _ in chunks))
        cnts = lax.fori_loop(4 * n4, n_blk, count, cnts)
        return [jnp.dot(cnt.astype(F32).astype(BF16), ones, preferred_element_type=F32) for cnt in cnts]

    def count_ge_best(ts):
        cnts = [sum(jnp.where(best_scr[u, rows, :] >= t.astype(jnp.int16), jnp.int16(1), jnp.int16(0))
                    for u in range(TOP_LOW)) for rows, t in zip(chunks, ts)]
        return [jnp.dot(cnt.astype(F32).astype(BF16), ones, preferred_element_type=F32) for cnt in cnts]

    def search(needed, counter):
        def bit_body(it, ts):
            cands = [t + (jnp.int32(1) << (15 - it)) for t in ts]
            return tuple(jnp.where(c >= n, cand, t) for c, n, cand, t in zip(counter(cands), needed, cands, ts))

        return lax.fori_loop(0, 16, bit_body, tuple(jnp.full((rb, LANES), lo16, jnp.int32) for _ in chunks))

    t_hi = search([float(topk)] * len(chunks), count_ge)
    top = -lo16 - 1
    above = [jnp.where(t == top, 0.0, c)
             for t, c in zip(t_hi, count_ge([jnp.minimum(t + 1, top) for t in t_hi]))]

    def low_plane(rows, tb):
        def body(j, carry):
            best, spill = carry[:TOP_LOW], carry[TOP_LOW]
            k32 = keys_scr[j, rows, :]
            a = jnp.where((k32 >> 16) == tb, (k32 & 0xFFFF) + lo16, lo16).astype(jnp.int16)
            half_scr[j, rows, :] = a
            kept = []
            for b in best:
                keep_b = b >= a
                kept.append(jnp.where(keep_b, b, a))
                a = jnp.where(keep_b, a, b)
            return tuple(kept) + (jnp.where(spill >= a, spill, a),)

        lowest = jnp.full((rb, LANES), lo16, jnp.int16)
        out = lax.fori_loop(0, n_blk, body, (lowest,) * (TOP_LOW + 1))
        for u in range(TOP_LOW):
            best_scr[u, rows, :] = out[u]
        return jnp.max(out[TOP_LOW].astype(jnp.int32))

    spilled = functools.reduce(jnp.maximum, [low_plane(rows, tb) for rows, tb in zip(chunks, t_hi)])
    needed = [topk - a for a in above]
    t_lo = lax.cond(spilled > lo16, lambda: search(needed, count_ge), lambda: search(needed, count_ge_best))
    for rows, th, tl in zip(chunks, t_hi, t_lo):
        thr_scr[rows, :] = jnp.maximum(th * 65536 + (tl - lo16), INT_MIN + 1)

    def count32(pred):
        def count(j, cnts):
            return tuple(cnt + jnp.where(pred(keys_scr[j, rows, :], thr_scr[rows, :]), 1, 0)
                         for cnt, rows in zip(cnts, chunks))

        cnts = lax.fori_loop(0, n_blk, count, tuple(jnp.zeros((rb, LANES), jnp.int32) for _ in chunks))
        return [jnp.dot(cnt.astype(F32).astype(BF16), ones, preferred_element_type=F32) for cnt in cnts]

    surplus = functools.reduce(jnp.maximum, [jnp.max(c) for c in count32(lambda k, t: k >= t)]) - topk

    @pl.when(surplus > 0)
    def _():
        keep = [topk - c for c in count32(lambda k, t: k > t)]
        upto = jnp.where(lax.broadcasted_iota(jnp.int32, (LANES, LANES), 0)
                         <= lax.broadcasted_iota(jnp.int32, (LANES, LANES), 1), 1.0, 0.0).astype(BF16)

        def demote(j, seen):
            out = []
            for rows, kp, sn in zip(chunks, keep, seen):
                k32 = keys_scr[j, rows, :]
                tied = k32 == thr_scr[rows, :]
                tied_bf = jnp.where(tied, 1.0, 0.0).astype(BF16)
                rank = sn + jnp.dot(tied_bf, upto, preferred_element_type=F32)
                keys_scr[j, rows, :] = k32 - jnp.where(tied & (rank > kp), 1, 0)
                out.append(sn + jnp.dot(tied_bf, ones, preferred_element_type=F32))
            return tuple(out)

        lax.fori_loop(0, n_blk, demote, tuple(jnp.zeros((rb, LANES), F32) for _ in chunks))

    hd = A_HEAD_DIM

    def select_mask(blk0, n):
        return [jnp.where(keys_scr[blk0 + c] >= thr_scr[...], 0.0, NEG_BIG) for c in range(n)]

    def logits(h, kblk):
        g = h // A_REP
        s = lax.dot_general(q_ref[:, h * hd:(h + 1) * hd], kblk[:, g * hd:(g + 1) * hd], NT_DIMS,
                            preferred_element_type=F32)
        return [s[:, c * LANES:(c + 1) * LANES] for c in range(s.shape[1] // LANES)]

    def tile_cols(h, col0, n):
        return [tile_ref[h, :, col0 + c * LANES:col0 + (c + 1) * LANES] for c in range(n)]

    mfar_scr[...] = jnp.full(mfar_scr.shape, NEG_BIG, F32)
    mnear_scr[...] = jnp.full(mnear_scr.shape, NEG_BIG, F32)

    def far_mask(blk0, colmask):
        am = select_mask(blk0, sub)
        return am if colmask is None else [a + c for a, c in zip(am, colmask)]

    def max_far(start, blk0, colmask):
        kblk = far_k(start, kb)
        am = far_mask(blk0, colmask)
        for h in range(A_HEADS):
            mfar_scr[h] = functools.reduce(
                jnp.maximum, [s + a for s, a in zip(logits(h, kblk), am)], mfar_scr[h])

    def max_near(kblk, blk0, col0):
        n = kblk.shape[0] // LANES
        am = select_mask(blk0, n)
        for h in range(A_HEADS):
            mnear_scr[h] = functools.reduce(
                jnp.maximum, [s + a + b for s, a, b in zip(logits(h, kblk), am, tile_cols(h, col0, n))],
                mnear_scr[h])

    prev_start = pl.multiple_of(q0 - LANES, LANES)
    _sweep_far(q0, qb, kb, max_far)
    pl.when(q0 > 0)(lambda: max_near(far_k(prev_start, LANES), cur_blk - 1, 0))
    max_near(kc_ref[...], cur_blk, LANES)

    for h in range(A_HEADS):
        fb = tab_ref[FAR_BUCKET, h]
        m = jnp.maximum(jnp.max(mfar_scr[h], axis=-1, keepdims=True) + fb,
                        jnp.max(mnear_scr[h], axis=-1, keepdims=True))
        mfar_scr[h] = jnp.broadcast_to(fb - m, (qb, LANES))
        mnear_scr[h] = jnp.broadcast_to(-m, (qb, LANES))
    l_scr[...] = jnp.zeros(l_scr.shape, F32)
    acc_scr[...] = jnp.zeros(acc_scr.shape, F32)

    def accumulate(h, p, vblk):
        g = h // A_REP
        l_scr[h] += functools.reduce(jnp.add, p)
        acc_scr[h] += jnp.dot(jnp.concatenate(p, axis=1).astype(BF16), vblk[:, g * hd:(g + 1) * hd],
                              preferred_element_type=F32)

    def acc_far(start, blk0, colmask):
        kblk = far_k(start, kb)
        vblk = far_v(start, kb)
        am = far_mask(blk0, colmask)
        for h in range(A_HEADS):
            shift = mfar_scr[h]
            accumulate(h, [jnp.exp(s + a + shift) for s, a in zip(logits(h, kblk), am)], vblk)

    def acc_near(kblk, vblk, blk0, col0):
        n = kblk.shape[0] // LANES
        am = select_mask(blk0, n)
        for h in range(A_HEADS):
            shift = mnear_scr[h]
            accumulate(h, [jnp.exp(s + a + b + shift) for s, a, b in
                           zip(logits(h, kblk), am, tile_cols(h, col0, n))], vblk)

    _sweep_far(q0, qb, kb, acc_far)
    pl.when(q0 > 0)(lambda: acc_near(far_k(prev_start, LANES), far_v(prev_start, LANES),
                                     cur_blk - 1, 0))
    acc_near(kc_ref[...], vc_ref[...], cur_blk, LANES)

    for h in range(A_HEADS):
        o_ref[:, h * hd:(h + 1) * hd] = (
            acc_scr[h] / jnp.sum(l_scr[h], axis=-1, keepdims=True)).astype(o_ref.dtype)


def _dsa_call(rel_bias, tiles, q, qi, iw, far, cur, *, qb, kb, qoff, topk, kv_pieces=None):
    b, t, _ = q.shape
    lf = far[2].shape[1]
    qbr = tiles.shape[2] - LANES
    q0_max = qoff + t - qb
    far_end = -(-q0_max // kb) * kb
    assert t % qb == 0 and qoff % LANES == 0 and far_end <= lf and cur[0].shape[1] == (t // qb) * qbr
    n_slabs = max(far_end, q0_max + qbr) // LANES
    mode = {"pipeline_mode": pl.Buffered(1)} if t // qb > 1 else {}
    far_spec = lambda a: pl.BlockSpec((None,) + a.shape[1:], lambda bi, i: (bi, 0, 0), **mode)
    cur_spec = lambda a: pl.BlockSpec((None, qbr, a.shape[2]), lambda bi, i: (bi, i, 0))
    return pl.pallas_call(
        functools.partial(_dsa_kernel, qb=qb, kb=kb, qoff=qoff, topk=topk, kv_pieces=kv_pieces),
        grid=(b, t // qb),
        in_specs=[pl.BlockSpec(memory_space=pltpu.SMEM),
                  pl.BlockSpec((None, qb, q.shape[2]), lambda bi, i: (bi, i, 0)),
                  pl.BlockSpec((None, qb, qi.shape[2]), lambda bi, i: (bi, i, 0)),
                  pl.BlockSpec((None, qb, iw.shape[2]), lambda bi, i: (bi, i, 0)),
                  far_spec(far[0]), far_spec(far[1]), far_spec(far[2]),
                  cur_spec(cur[0]), cur_spec(cur[1]), cur_spec(cur[2]),
                  _resident(tiles.shape, lambda bi, i: (0, 0, 0))],
        out_specs=pl.BlockSpec((None, qb, A_HEADS * A_HEAD_DIM), lambda bi, i: (bi, i, 0)),
        out_shape=jax.ShapeDtypeStruct((b, t, A_HEADS * A_HEAD_DIM), BF16),
        scratch_shapes=[pltpu.VMEM((n_slabs, qb, LANES), jnp.int32),
                        pltpu.VMEM((n_slabs, qb, LANES), jnp.int16),
                        pltpu.VMEM((TOP_LOW, qb, LANES), jnp.int16),
                        pltpu.VMEM((qb, LANES), jnp.int32),
                        pltpu.VMEM((A_HEADS, qb, LANES), F32),
                        pltpu.VMEM((A_HEADS, qb, LANES), F32),
                        pltpu.VMEM((A_HEADS, qb, LANES), F32),
                        pltpu.VMEM((A_HEADS, qb, A_HEAD_DIM), F32)],
        compiler_params=_params(2),
        name="mixer_a",
    )(rel_bias, q, qi, iw, *far, *cur, tiles)


def _diff_kernel(tab_ref, lam_ref, sg_ref, q_ref, k_ref, v_ref, kc_ref, vc_ref, tile_ref, o_ref,
                 mfar_scr, mnear_scr, l_scr, acc_scr, *, qb, kb, qoff, lam_init):
    head = pl.program_id(1)
    i = pl.program_id(2)
    q0 = qoff + i * qb
    hd = B_HEAD_DIM
    far_bias = tab_ref[FAR_BUCKET, A_HEADS + head]
    far_k = lambda start, width: _key_rows(k_ref, start, width)
    far_v = lambda start, width: _key_rows(v_ref, start, width)

    def logits(c, kblk):
        s = lax.dot_general(q_ref[:, c * hd:(c + 1) * hd], kblk[:, c * hd:(c + 1) * hd], NT_DIMS,
                            preferred_element_type=F32)
        return [s[:, u * LANES:(u + 1) * LANES] for u in range(s.shape[1] // LANES)]

    def tile_cols(col0, n):
        return [tile_ref[0, :, col0 + u * LANES:col0 + (u + 1) * LANES] for u in range(n)]

    mfar_scr[...] = jnp.full(mfar_scr.shape, NEG_BIG, F32)
    mnear_scr[...] = jnp.full(mnear_scr.shape, NEG_BIG, F32)

    def max_far(start, blk0, colmask):
        kblk = far_k(start, kb)
        for c in range(2):
            s = logits(c, kblk)
            if colmask is not None:
                s = [x + m for x, m in zip(s, colmask)]
            mfar_scr[c] = functools.reduce(jnp.maximum, s, mfar_scr[c])

    def max_near(kblk, col0):
        for c in range(2):
            mnear_scr[c] = functools.reduce(
                jnp.maximum, [s + b for s, b in zip(logits(c, kblk), tile_cols(col0, kblk.shape[0] // LANES))],
                mnear_scr[c])

    prev_start = pl.multiple_of(q0 - LANES, LANES)
    _sweep_far(q0, qb, kb, max_far)
    pl.when(q0 > 0)(lambda: max_near(far_k(prev_start, LANES), 0))
    max_near(kc_ref[...], LANES)

    for c in range(2):
        m = jnp.maximum(jnp.max(mfar_scr[c], axis=-1, keepdims=True) + far_bias,
                        jnp.max(mnear_scr[c], axis=-1, keepdims=True))
        mfar_scr[c] = jnp.broadcast_to(far_bias - m, (qb, LANES))
        mnear_scr[c] = jnp.broadcast_to(-m, (qb, LANES))
    l_scr[...] = jnp.zeros(l_scr.shape, F32)
    acc_scr[...] = jnp.zeros(acc_scr.shape, F32)

    def accumulate(c, p, vblk):
        l_scr[c] += functools.reduce(jnp.add, p)
        acc_scr[c] += jnp.dot(jnp.concatenate(p, axis=1).astype(BF16), vblk, preferred_element_type=F32)

    def acc_far(start, blk0, colmask):
        kblk = far_k(start, kb)
        vblk = far_v(start, kb)
        for c in range(2):
            shift = mfar_scr[c]
            shifts = [shift] * (kb // LANES) if colmask is None else [shift + m for m in colmask]
            accumulate(c, [jnp.exp(s + sh) for s, sh in zip(logits(c, kblk), shifts)], vblk)

    def acc_near(kblk, vblk, col0):
        for c in range(2):
            shift = mnear_scr[c]
            accumulate(c, [jnp.exp(s + b + shift) for s, b in
                           zip(logits(c, kblk), tile_cols(col0, kblk.shape[0] // LANES))], vblk)

    _sweep_far(q0, qb, kb, acc_far)
    pl.when(q0 > 0)(lambda: acc_near(far_k(prev_start, LANES), far_v(prev_start, LANES), 0))
    acc_near(kc_ref[...], vc_ref[...], LANES)

    lam = lam_ref[0, 0]
    o = (acc_scr[0] / jnp.sum(l_scr[0], axis=-1, keepdims=True)
         - lam * (acc_scr[1] / jnp.sum(l_scr[1], axis=-1, keepdims=True)))
    o = o * lax.rsqrt(jnp.mean(o * o, axis=-1, keepdims=True) + EPS) * sg_ref[...]
    o_ref[...] = (o * (1.0 - lam_init)).astype(o_ref.dtype)


def _diff_call(rel_bias, tiles, lam, subln_g, q, far, cur, *, qb, kb, qoff, lam_init):
    b, t, _ = q.shape
    lf = far[0].shape[1]
    hw = 2 * B_HEAD_DIM
    qbr = tiles.shape[2] - LANES
    q0_max = qoff + t - qb
    far_end = -(-max(q0_max - LANES, 0) // kb) * kb
    assert t % qb == 0 and qoff % LANES == 0 and far_end <= lf and cur[0].shape[1] == (t // qb) * qbr
    far_spec = pl.BlockSpec((None, lf, hw), lambda bi, h, i: (bi, 0, h))
    cur_spec = pl.BlockSpec((None, qbr, hw), lambda bi, h, i: (bi, i, h))
    return pl.pallas_call(
        functools.partial(_diff_kernel, qb=qb, kb=kb, qoff=qoff, lam_init=lam_init),
        grid=(b, B_HEADS, t // qb),
        in_specs=[pl.BlockSpec(memory_space=pltpu.SMEM),
                  pl.BlockSpec(memory_space=pltpu.SMEM),
                  pl.BlockSpec((1, hw), lambda bi, h, i: (0, 0)),
                  pl.BlockSpec((None, qb, hw), lambda bi, h, i: (bi, i, h)),
                  far_spec, far_spec, cur_spec, cur_spec,
                  pl.BlockSpec((1, qb, tiles.shape[2]), lambda bi, h, i: (A_HEADS + h, 0, 0))],
        out_specs=pl.BlockSpec((None, qb, hw), lambda bi, h, i: (bi, i, h)),
        out_shape=jax.ShapeDtypeStruct((b, t, B_HEADS * hw), BF16),
        scratch_shapes=[pltpu.VMEM((2, qb, LANES), F32),
                        pltpu.VMEM((2, qb, LANES), F32),
                        pltpu.VMEM((2, qb, LANES), F32),
                        pltpu.VMEM((2, qb, hw), F32)],
        compiler_params=_params(3),
        name="mixer_b",
    )(rel_bias, lam, subln_g.reshape(1, hw), q, *far, *cur, tiles)


def _lam_kernel(dl_ref, o_ref, *, lam_init):
    dl = dl_ref[...]
    s1 = jnp.sum(dl[0:1] * dl[1:2], axis=-1, keepdims=True)
    s2 = jnp.sum(dl[2:3] * dl[3:4], axis=-1, keepdims=True)
    o_ref[...] = jnp.exp(s1) - jnp.exp(s2) + lam_init


def _lam_call(diff_lam, lam_init):
    return pl.pallas_call(
        functools.partial(_lam_kernel, lam_init=lam_init),
        out_shape=jax.ShapeDtypeStruct((1, 1), F32),
        name="diff_lambda",
    )(diff_lam)


def _route(logits_t, rbias, counts):
    tm = logits_t.shape[1]
    per = N_EXPERTS // N_GROUPS
    scores = jax.nn.sigmoid(logits_t).reshape(N_GROUPS, per, tm)
    ch = scores + rbias.reshape(N_GROUPS, per, 1)
    sub = lax.broadcasted_iota(jnp.int32, ch.shape, 1)
    grp = lax.broadcasted_iota(jnp.int32, ch.shape, 0)
    m1 = jnp.max(ch, axis=1, keepdims=True)
    i1 = jnp.min(jnp.where(ch == m1, sub, per), axis=1, keepdims=True)
    m2 = jnp.max(jnp.where(sub == i1, -jnp.inf, ch), axis=1, keepdims=True)
    gs = jnp.broadcast_to(m1 + m2, ch.shape)
    rank = jnp.zeros(ch.shape, jnp.int32)
    for g2 in range(N_GROUPS):
        o = gs[g2:g2 + 1]
        rank = rank + jnp.where(o > gs, 1, 0) + jnp.where(o == gs, 1, 0) * jnp.where(grp > g2, 1, 0)
    cm = jnp.where(rank < TOPK_GROUPS, ch, -jnp.inf)
    eid = grp * per + sub
    erank = jnp.zeros(cm.shape, jnp.int32)
    for e2 in range(N_EXPERTS):
        o = cm[e2 // per:e2 // per + 1, e2 % per:e2 % per + 1, :]
        erank = erank + jnp.where(o > cm, 1, 0) + jnp.where(o == cm, 1, 0) * jnp.where(eid > e2, 1, 0)
    chosen = erank < TOP_K
    sel = jnp.where(chosen, scores, 0.0)
    denom = jnp.sum(jnp.sum(sel, axis=1, keepdims=True), axis=0, keepdims=True)
    gates = sel / denom * ROUTED_SCALE

    ones = jnp.where(chosen, 1.0, 0.0).reshape(N_EXPERTS, tm)
    earlier = (lax.broadcasted_iota(jnp.int32, (tm, tm), 0) < lax.broadcasted_iota(jnp.int32, (tm, tm), 1))
    before = jnp.dot(ones.astype(BF16), jnp.where(earlier, 1.0, 0.0).astype(BF16), preferred_element_type=F32)
    pos = (before + counts.astype(F32)).reshape(N_GROUPS, per, tm)
    new_counts = counts + jnp.sum(ones, axis=1, keepdims=True).astype(jnp.int32)

    def per_slot(values):
        rows = [jnp.sum(jnp.sum(jnp.where(erank == k, values, 0.0), axis=1, keepdims=True), axis=0)
                for k in range(TOP_K)]
        return jnp.concatenate(rows, axis=0)

    return (per_slot(eid.astype(F32)).astype(jnp.int32), per_slot(pos).astype(jnp.int32), per_slot(gates),
            new_counts)


def _pack_pair(a, b):
    ua = pltpu.bitcast(a.astype(BF16).astype(F32), jnp.uint32)
    ub = pltpu.bitcast(b.astype(BF16).astype(F32), jnp.uint32)
    return pltpu.bitcast(ua | (ub >> 16), jnp.int32)


def _unpack_pair(p):
    u = pltpu.bitcast(p, jnp.uint32)
    a = pltpu.bitcast(u & jnp.uint32(0xFFFF0000), F32)
    b = pltpu.bitcast(u << 16, F32)
    return a.astype(BF16), b.astype(BF16)


PARTS = 4
PART_W = D_MODEL // 2 // PARTS


def _pack_rows(x):
    half = x.shape[1] // 2
    return jnp.stack([_pack_pair(x[:, j * PART_W:(j + 1) * PART_W],
                                 x[:, half + j * PART_W:half + (j + 1) * PART_W]) for j in range(PARTS)])


def _packed_matmul(parts_ref, w_ref):
    half = w_ref.shape[0] // 2
    acc = None
    for j in range(PARTS):
        a, b = _unpack_pair(parts_ref[j])
        t = (jnp.dot(a, w_ref[j * PART_W:(j + 1) * PART_W, :], preferred_element_type=F32)
             + jnp.dot(b, w_ref[half + j * PART_W:half + (j + 1) * PART_W, :], preferred_element_type=F32))
        acc = t if acc is None else acc + t
    return acc


def _outproj_kernel(x_ref, oa_ref, ob_ref, ga_ref, sc_ref, sh_ref, g_ref, w_ref, wr_ref, rb_ref, cin_ref,
                    x1_ref, h2_ref, ek_ref, pk_ref, gk_ref, cout_ref, cnt_scr):
    @pl.when(pl.program_id(0) == 0)
    def _():
        cnt_scr[...] = cin_ref[...]

    half = oa_ref.shape[1]
    mix = (jnp.dot(oa_ref[...], w_ref[:half, :], preferred_element_type=F32)
           + jnp.dot(ob_ref[...], w_ref[half:, :], preferred_element_type=F32))
    x1 = x_ref[...] + ga_ref[...] * mix
    x1_ref[...] = x1
    h2 = _modulated_norm(x1, g_ref[...], sc_ref[...], sh_ref[...])
    h2_ref[...] = _pack_rows(h2)
    logits_t = lax.dot_general(wr_ref[...], h2, NT_DIMS, preferred_element_type=F32,
                               precision=lax.Precision.HIGHEST)
    ek, pk, gk, counts = _route(logits_t, rb_ref[...], cnt_scr[...])
    ek_ref[...] = ek
    pk_ref[...] = pk
    gk_ref[...] = gk
    cnt_scr[...] = counts
    cout_ref[...] = counts


def _outproj_call(x2d, oa, ob, ga, sc, sh, mod_map, g, w_bf, wr_t, rbias, counts_in, tm):
    n, d = x2d.shape
    assert n % tm == 0
    r = sc.shape[1]
    mod_spec = pl.BlockSpec((None, r, d), mod_map)
    slot_spec = pl.BlockSpec((TOP_K, tm), lambda i: (0, i))
    return pl.pallas_call(
        _outproj_kernel,
        grid=(n // tm,),
        in_specs=[pl.BlockSpec((tm, d), lambda i: (i, 0)),
                  pl.BlockSpec((tm, oa.shape[1]), lambda i: (i, 0)),
                  pl.BlockSpec((tm, ob.shape[1]), lambda i: (i, 0)),
                  mod_spec, mod_spec, mod_spec,
                  pl.BlockSpec((1, d), lambda i: (0, 0)),
                  _resident(w_bf.shape, lambda i: (0, 0)),
                  _resident(wr_t.shape, lambda i: (0, 0)),
                  pl.BlockSpec((N_EXPERTS, 1), lambda i: (0, 0)),
                  pl.BlockSpec((N_EXPERTS, 1), lambda i: (0, 0))],
        out_specs=[pl.BlockSpec((tm, d), lambda i: (i, 0)),
                   pl.BlockSpec((PARTS, tm, PART_W), lambda i: (0, i, 0)),
                   slot_spec, slot_spec, slot_spec,
                   pl.BlockSpec((N_EXPERTS, 1), lambda i: (0, 0))],
        out_shape=[jax.ShapeDtypeStruct((n, d), F32),
                   jax.ShapeDtypeStruct((PARTS, n, PART_W), jnp.int32),
                   jax.ShapeDtypeStruct((TOP_K, n), jnp.int32),
                   jax.ShapeDtypeStruct((TOP_K, n), jnp.int32),
                   jax.ShapeDtypeStruct((TOP_K, n), F32),
                   jax.ShapeDtypeStruct((N_EXPERTS, 1), jnp.int32)],
        scratch_shapes=[pltpu.VMEM((N_EXPERTS, 1), jnp.int32)],
        compiler_params=_params(1),
        name="out_proj_router",
    )(x2d, oa, ob, ga, sc, sh, g.reshape(1, d), w_bf, wr_t, rbias, counts_in)


SC_WINDOW = 128
EXPERT_TILE = 512


def _sc_mesh():
    return plsc.VectorSubcoreMesh(core_axis_name="core", subcore_axis_name="subcore")


def _sc_scatter_rows(x, idx, n_out):
    n, w = x.shape
    steps = n // SC_WINDOW
    total = idx.shape[1] // SC_WINDOW

    @pl.kernel(out_type=jax.ShapeDtypeStruct((n_out, w), x.dtype), mesh=_sc_mesh(), scratch_types=[],
               name="moe_dispatch")
    def scatter(x_hbm, i_hbm, o_hbm):
        def body(x_vmem, i_vmem):
            pltpu.sync_copy(x_vmem, o_hbm.at[i_vmem.at[0]])

        pltpu.emit_pipeline(
            body, grid=(total,),
            in_specs=[pl.BlockSpec((SC_WINDOW, w), lambda i: (i % steps, 0)),
                      pl.BlockSpec((1, SC_WINDOW), lambda i: (0, i))],
            out_specs=[], core_axis_name=("core", "subcore"),
            dimension_semantics=(pltpu.PARALLEL,))(x_hbm, i_hbm)

    return scatter(x, idx)


def _sc_gather_rows(y, idx):
    w = y.shape[1]
    n = idx.shape[1]

    @pl.kernel(out_type=jax.ShapeDtypeStruct((n, w), y.dtype), mesh=_sc_mesh(), scratch_types=[],
               name="moe_collect")
    def gather(y_hbm, i_hbm, o_hbm):
        def body(i_vmem, o_vmem):
            pltpu.sync_copy(y_hbm.at[i_vmem.at[0]], o_vmem)

        pltpu.emit_pipeline(
            body, grid=(n // SC_WINDOW,),
            in_specs=[pl.BlockSpec((1, SC_WINDOW), lambda i: (0, i))],
            out_specs=[pl.BlockSpec((SC_WINDOW, w), lambda i: (i, 0))],
            core_axis_name=("core", "subcore"),
            dimension_semantics=(pltpu.PARALLEL,))(i_hbm, o_hbm)

    return gather(y, idx)


def _swiglu(parts_ref, wg_ref, wu_ref, wd_ref):
    a = _packed_matmul(parts_ref, wg_ref)
    u = _packed_matmul(parts_ref, wu_ref)
    hid = (a * jax.nn.sigmoid(a) * u).astype(BF16)
    return jnp.dot(hid, wd_ref[...], preferred_element_type=F32)


def _expert_kernel(te_ref, nu_ref, x_ref, wg_ref, wu_ref, wd_ref, y_ref, wg_scr, wu_scr, wd_scr):
    i = pl.program_id(0)

    @pl.when(i < nu_ref[0])
    def _():
        @pl.when((i == 0) | (te_ref[i] != te_ref[jnp.maximum(i - 1, 0)]))
        def _():
            wg_scr[...] = wg_ref[...].astype(BF16)
            wu_scr[...] = wu_ref[...].astype(BF16)
            wd_scr[...] = wd_ref[...].astype(BF16)

        y_ref[...] = _pack_rows(_swiglu(x_ref, wg_scr, wu_scr, wd_scr))


def _expert_call(tile_expert, n_used, x_sorted, wg, wu, wd):
    _, r, _ = x_sorted.shape
    ne, d, de = wg.shape
    n_tiles = r // EXPERT_TILE
    row_map = lambda i, te, nu: (0, jnp.minimum(i, nu[0] - 1), 0)
    return pl.pallas_call(
        _expert_kernel,
        grid_spec=pltpu.PrefetchScalarGridSpec(
            num_scalar_prefetch=2,
            grid=(n_tiles,),
            in_specs=[pl.BlockSpec((PARTS, EXPERT_TILE, PART_W), row_map),
                      pl.BlockSpec((None, d, de), lambda i, te, nu: (te[i], 0, 0)),
                      pl.BlockSpec((None, d, de), lambda i, te, nu: (te[i], 0, 0)),
                      pl.BlockSpec((None, de, d), lambda i, te, nu: (te[i], 0, 0))],
            out_specs=pl.BlockSpec((PARTS, EXPERT_TILE, PART_W), row_map),
            scratch_shapes=[pltpu.VMEM((d, de), BF16), pltpu.VMEM((d, de), BF16), pltpu.VMEM((de, d), BF16)]),
        out_shape=jax.ShapeDtypeStruct(x_sorted.shape, jnp.int32),
        compiler_params=_params(1),
        name="moe_experts",
    )(tile_expert, n_used, x_sorted, wg, wu, wd)


def _final_kernel(x_ref, h_ref, rows_ref, gk_ref, gf_ref, g_ref, sg_ref, su_ref, sd_ref, o_ref):
    moe = _swiglu(h_ref, sg_ref, su_ref, sd_ref)
    half = moe.shape[1] // 2
    lo = [moe[:, j * PART_W:(j + 1) * PART_W] for j in range(PARTS)]
    hi = [moe[:, half + j * PART_W:half + (j + 1) * PART_W] for j in range(PARTS)]
    for k in range(TOP_K):
        gate = gk_ref[:, k:k + 1]
        for j in range(PARTS):
            a, b = _unpack_pair(rows_ref[k, j])
            lo[j] = lo[j] + gate * a.astype(F32)
            hi[j] = hi[j] + gate * b.astype(F32)
    x = x_ref[...] + gf_ref[...] * jnp.concatenate(lo + hi, axis=1)
    o_ref[...] = x * lax.rsqrt(jnp.mean(x * x, axis=-1, keepdims=True) + EPS) * g_ref[...]


def _final_call(x1, h2p, rows, row_block0, gk_t, gf, mod_map, g, sg, su, sd, tm):
    n, d = x1.shape
    assert n % tm == 0
    r = gf.shape[1]
    return pl.pallas_call(
        _final_kernel,
        grid=(n // tm,),
        in_specs=[pl.BlockSpec((tm, d), lambda i: (i, 0)),
                  pl.BlockSpec((PARTS, tm, PART_W), lambda i: (0, i, 0)),
                  pl.BlockSpec((TOP_K, PARTS, tm, PART_W), lambda i: (0, 0, i + row_block0, 0)),
                  pl.BlockSpec((tm, TOP_K), lambda i: (i, 0)),
                  pl.BlockSpec((None, r, d), mod_map),
                  pl.BlockSpec((1, d), lambda i: (0, 0)),
                  _resident(sg.shape, lambda i: (0, 0)),
                  _resident(su.shape, lambda i: (0, 0)),
                  _resident(sd.shape, lambda i: (0, 0))],
        out_specs=pl.BlockSpec((tm, d), lambda i: (i, 0)),
        out_shape=jax.ShapeDtypeStruct((n, d), F32),
        compiler_params=_params(1),
        name="final_norm",
    )(x1, h2p, rows, gk_t, gf, g.reshape(1, d), sg, su, sd)


_QA, _KA, _VA, _QI, _KI, _IW, _QB, _KB, _VB = range(9)
_IN_SIZES = (A_HEADS * A_HEAD_DIM, A_KV_HEADS * A_HEAD_DIM, A_KV_HEADS * A_HEAD_DIM,
             IDX_HEADS * IDX_DIM, IDX_DIM, IDX_HEADS,
             B_HEADS * 2 * B_HEAD_DIM, B_HEADS * 2 * B_HEAD_DIM, B_HEADS * 2 * B_HEAD_DIM)


def _split_w_in(w_in):
    d = w_in.shape[0]
    offs = np.concatenate([[0], np.cumsum(_IN_SIZES)])
    cols = [w_in[:, offs[k]:offs[k + 1]].astype(BF16) for k in range(9)]
    zero = lambda n: jnp.zeros((d, n), BF16)
    qi = jnp.concatenate([cols[_QI].reshape(d, IDX_HEADS, IDX_DIM),
                          jnp.zeros((d, IDX_HEADS, LANES - IDX_DIM), BF16)], axis=-1).reshape(d, IDX_HEADS * LANES)
    w_a = jnp.concatenate([cols[_QA], cols[_KA], cols[_VA], qi,
                           cols[_KI], zero(LANES - IDX_DIM), cols[_IW], zero(LANES - IDX_HEADS)], axis=1)
    w_b = jnp.concatenate([cols[_QB], cols[_KB], cols[_VB]], axis=1)
    return w_a, w_b


def _group_forward(x, mod, past, rel_bias, lam, weights, *, lam_init, qb, qb_b, kb, tm, per_row_mod):
    (norm_a_g, w_a, w_b, subln_g) = weights
    b, t, d = x.shape
    n = b * t
    x2d = x.reshape(n, d)
    sh_a, sc_a = mod[0], mod[1]
    if per_row_mod:
        expand = lambda m: jnp.repeat(m, t, axis=0).reshape(1, n, d)
        mod_map = lambda i: (0, i, 0)
    else:
        expand = lambda m: m.reshape(b, 1, d)
        mod_map = lambda i: ((i * tm) // t, 0, 0)
    sc3, sh3 = expand(sc_a), expand(sh_a)

    na = A_HEADS * A_HEAD_DIM
    nk = A_KV_HEADS * A_HEAD_DIM
    nq = IDX_HEADS * LANES
    c_ka, c_va, c_qi = na, na + nk, na + 2 * nk
    c_ki, c_iw = c_qi + nq, c_qi + nq + LANES
    kv_order = tuple(range(A_KV_HEADS))
    defs_a = [(0, na, na, A_HEAD_DIM ** -0.5, None), (c_ka, nk, nk, None, kv_order), (c_ka, nk, nk, None, None),
              (c_va, nk, nk, None, kv_order), (c_va, nk, nk, None, None), (c_qi, nq, nq, None, None),
              (c_ki, LANES, IDX_DIM, None, None), (c_ki, LANES, LANES, None, None), (c_iw, LANES, LANES, None, None)]
    dt_a = [BF16, F32, BF16, F32, BF16, BF16, F32, BF16, F32]
    qa, ka, ka_bf, va, va_bf, qi, ki, ki_bf, iw = _norm_proj_call(
        x2d, sc3, sh3, mod_map, norm_a_g, w_a, defs_a, dt_a, tm, "in_proj_a")
    nb = B_HEADS * 2 * B_HEAD_DIM
    kb_order = tuple(range(2 * B_HEADS))
    vb_order = tuple(2 * h + half for half in range(2) for h in range(B_HEADS))
    defs_b = [(0, nb, nb, B_HEAD_DIM ** -0.5, None), (nb, nb, nb, None, kb_order), (nb, nb, nb, None, None),
              (2 * nb, nb, nb, None, vb_order), (2 * nb, nb, nb, None, None)]
    dt_b = [BF16, F32, BF16, F32, BF16]
    qbm, kbm, kb_bf, vbm, vb_bf = _norm_proj_call(
        x2d, sc3, sh3, mod_map, norm_a_g, w_b, defs_b, dt_b, tm, "in_proj_b")

    r3 = lambda a: a.reshape(b, t, a.shape[-1])
    cur = [r3(ka_bf), r3(va_bf), r3(ki_bf), r3(kb_bf), r3(vb_bf)]
    kv_pieces = None
    if past is None:
        qoff = 0
        far = cur
    else:
        assert t == qb, "a group with cached keys is one query block per batch row"
        qoff = past[0].shape[1]
        qbr = -(-qb // LANES) * LANES
        cur = [jnp.concatenate([a, jnp.zeros((b, qbr - t, a.shape[-1]), BF16)], axis=1) for a in cur]
        pk, pv, pi, pbk, pbv = past
        if qoff % kb == 0:
            kv_pieces = (A_KV_HEADS, tuple(range(A_KV_HEADS)))
            far = [pk.reshape(b, qoff * A_KV_HEADS, A_HEAD_DIM), pv.reshape(b, qoff * A_KV_HEADS, A_HEAD_DIM),
                   pi.reshape(b, qoff, -1), pbk.reshape(b, qoff, -1), pbv.reshape(b, qoff, -1)]
        else:
            far = [jnp.concatenate([p.reshape(b, qoff, -1), jnp.zeros((b, -qoff % kb, p[0, 0].size), p.dtype)], axis=1)
                   for p in past]
    length = qoff + t
    topk = min(TOPK_MAX, length // 4)

    tiles = _bias_tile_call(rel_bias, qb)
    tiles_b = tiles if qb_b == qb else _bias_tile_call(rel_bias, qb_b)
    out_a = _dsa_call(rel_bias, tiles, r3(qa), r3(qi), r3(iw), far[:3], cur[:3], qb=qb, kb=kb, qoff=qoff, topk=topk,
                      kv_pieces=kv_pieces)
    out_b = _diff_call(rel_bias, tiles_b, lam, subln_g, r3(qbm), far[3:], cur[3:], qb=qb_b, kb=kb, qoff=qoff,
                       lam_init=lam_init)
    rows = (ka.reshape(1, b, t, A_KV_HEADS, A_HEAD_DIM), va.reshape(1, b, t, A_KV_HEADS, A_HEAD_DIM),
            ki.reshape(1, b, t, IDX_DIM), kbm.reshape(1, b, t, B_HEADS, 2, B_HEAD_DIM),
            vbm.reshape(b, t, 2, B_HEADS, B_HEAD_DIM).transpose(0, 1, 3, 2, 4)
               .reshape(1, b, t, B_HEADS, 2 * B_HEAD_DIM))
    return x2d, out_a.reshape(n, -1), out_b.reshape(n, -1), rows, expand, mod_map


def kernel(x_prompt, x_sample, c_prompt, c_sample, cache_a_k, cache_a_v, cache_a_kidx, cache_b_k, cache_b_v,
           rel_bias, w_ada, b_ada, norm_a_g, w_in, w_out, diff_lam, subln_g, norm_f_g, w_router, router_bias,
           w_gate, w_up, w_down, ws_gate, ws_up, ws_down, final_g):
    assert w_ada.shape[0] == 1, "single-layer model"
    d = x_prompt.shape[-1]
    bp, tp, _ = x_prompt.shape
    bs, ts, _ = x_sample.shape
    lam_init = 0.8 - 0.6 * math.exp(-0.3 * 0)

    c_all = jnp.concatenate([c_prompt, c_sample], axis=0)
    rows = -(-c_all.shape[0] // 8) * 8
    c_pad = jnp.concatenate([c_all, jnp.zeros((rows - c_all.shape[0], d), F32)], axis=0)
    mod_all = _ada_call(c_pad, w_ada[0], b_ada[0])
    mod_p = [mod_all[:bp, k * d:(k + 1) * d] for k in range(6)]
    mod_s = [mod_all[bp:bp + bs, k * d:(k + 1) * d] for k in range(6)]

    w_a, w_b = _split_w_in(w_in[0])
    lam = _lam_call(diff_lam[0], lam_init)
    weights = (norm_a_g[0], w_a, w_b, subln_g[0])
    past = (cache_a_k[0], cache_a_v[0], cache_a_kidx[0], cache_b_k[0], cache_b_v[0])

    tm_p = 256
    tm_s = bs * ts
    xp2, oa_p, ob_p, rows_p, exp_p, map_p = _group_forward(
        x_prompt, mod_p, None, rel_bias, lam, weights, lam_init=lam_init, qb=256, qb_b=512, kb=512, tm=tm_p,
        per_row_mod=False)
    xs2, oa_s, ob_s, rows_s, exp_s, map_s = _group_forward(
        x_sample, mod_s, past, rel_bias, lam, weights, lam_init=lam_init, qb=ts, qb_b=ts, kb=2048, tm=tm_s,
        per_row_mod=True)

    w_out_bf = w_out[0].astype(BF16)
    wr_t = w_router[0].T
    rb = router_bias[0].reshape(N_EXPERTS, 1)
    no_rows = jnp.zeros((N_EXPERTS, 1), jnp.int32)
    x1_p, h2_p, ek_p, pk_p, gk_p, cnt_p = _outproj_call(
        xp2, oa_p, ob_p, exp_p(mod_p[2]), exp_p(mod_p[4]), exp_p(mod_p[3]), lambda i: map_p(2 * i), norm_f_g[0],
        w_out_bf, wr_t, rb, no_rows, 2 * tm_p)
    x1_s, h2_s, ek_s, pk_s, gk_s, cnt_all = _outproj_call(
        xs2, oa_s, ob_s, exp_s(mod_s[2]), exp_s(mod_s[4]), exp_s(mod_s[3]), map_s, norm_f_g[0], w_out_bf, wr_t, rb,
        cnt_p, tm_s)

    n_p, n_s = bp * tp, bs * ts
    n_all = n_p + n_s
    n_rows = -(-(n_all * TOP_K) // EXPERT_TILE) * EXPERT_TILE + N_EXPERTS * EXPERT_TILE
    counts = cnt_all[:, 0]
    padded = (counts + EXPERT_TILE - 1) // EXPERT_TILE * EXPERT_TILE
    ends = jnp.cumsum(padded)
    starts = ends - padded
    tile_first_row = jnp.arange(n_rows // EXPERT_TILE, dtype=jnp.int32) * EXPERT_TILE
    tile_expert = jnp.minimum(jnp.sum(ends[None, :] <= tile_first_row[:, None], axis=1), N_EXPERTS - 1).astype(jnp.int32)
    n_used = (ends[-1:] // EXPERT_TILE).astype(jnp.int32)
    ek = jnp.concatenate([ek_p, ek_s], axis=1)
    expert_ids = jnp.arange(N_EXPERTS, dtype=jnp.int32)[:, None, None]
    dest = jnp.sum(jnp.where(ek[None] == expert_ids, starts[:, None, None], 0), axis=0) \
        + jnp.concatenate([pk_p, pk_s], axis=1)
    piece = jnp.arange(PARTS, dtype=jnp.int32)[None, :, None]
    idx = (dest[:, None, :] + piece * n_rows).reshape(1, TOP_K * PARTS * n_all).astype(jnp.int32)

    h2_all = jnp.concatenate([h2_p, h2_s], axis=1).reshape(PARTS * n_all, PART_W)
    x_sorted = _sc_scatter_rows(h2_all, idx, PARTS * n_rows).reshape(PARTS, n_rows, PART_W)
    y_sorted = _expert_call(tile_expert, n_used, x_sorted, w_gate[0], w_up[0], w_down[0])
    rows = _sc_gather_rows(y_sorted.reshape(PARTS * n_rows, PART_W), idx).reshape(TOP_K, PARTS, n_all, PART_W)

    shared = (ws_gate[0].astype(BF16), ws_up[0].astype(BF16), ws_down[0].astype(BF16))
    assert n_p % tm_s == 0
    y_p = _final_call(x1_p, h2_p, rows, 0, gk_p.T, exp_p(mod_p[5]), map_p, final_g, *shared, tm_p)
    y_s = _final_call(x1_s, h2_s, rows, n_p // tm_s, gk_s.T, exp_s(mod_s[5]), map_s, final_g, *shared, tm_s)
    return (y_p.reshape(bp, tp, d), y_s.reshape(bs, ts, d)) + rows_p + rows_s
```

```python
import functools
import math

import numpy as np
import jax
import jax.numpy as jnp
from jax import lax
from jax.experimental import pallas as pl
from jax.experimental.pallas import tpu as pltpu
from jax.experimental.pallas import tpu_sc as plsc

D_MODEL = 2048
CHUNK = 64
A_HEADS = 8
A_KV_HEADS = 2
A_REP = A_HEADS // A_KV_HEADS
A_HEAD_DIM = 128
IDX_HEADS = 16
IDX_DIM = 64
TOPK_MAX = 256
B_HEADS = 4
B_HEAD_DIM = 128
N_REL_BUCKETS = 32
REL_MAX_DIST = 128
N_ATTN_HEADS = A_HEADS + B_HEADS
N_EXPERTS = 64
TOP_K = 8
N_GROUPS = 8
TOPK_GROUPS = 4
EXPERT_DIM = 512
SHARED_DIM = 512
ROUTED_SCALE = 2.5
EPS = 1e-6

LANES = 128
INT_MIN = -(2 ** 31)
NEG_BIG = -1e30
TOP_LOW = 3
VMEM_LIMIT = 52 * 1024 * 1024

F32 = jnp.float32
BF16 = jnp.bfloat16
NT_DIMS = (((1,), (1,)), ((), ()))


def _params(n_axes):
    return pltpu.CompilerParams(dimension_semantics=("arbitrary",) * n_axes,
                                vmem_limit_bytes=VMEM_LIMIT)


def _resident(shape, index_map):
    return pl.BlockSpec(shape, index_map, pipeline_mode=pl.Buffered(1))


def _bucket_thresholds():
    nb = N_REL_BUCKETS // 2
    max_exact = nb // 2
    n = np.arange(max_exact, 4 * REL_MAX_DIST, dtype=np.int32)
    nf = n.astype(np.float32)
    large = max_exact + (np.log(nf / np.float32(max_exact)) / np.float32(math.log(REL_MAX_DIST / max_exact))
                         * np.float32(nb - max_exact)).astype(np.int32)
    large = np.minimum(large, nb - 1)
    assert np.all(np.diff(large) >= 0)
    ths = [int(n[np.argmax(large >= b)]) for b in range(max_exact + 1, nb)]
    return max_exact, nb, ths


_MAX_EXACT, _NB, _BUCKET_THS = _bucket_thresholds()
assert _BUCKET_THS[-1] <= LANES + 1
FAR_BUCKET = _NB - 1


def _ada_kernel(c_ref, w_ref, b_ref, o_ref):
    c = c_ref[...]
    s = c * jax.nn.sigmoid(c)
    o_ref[...] = jnp.dot(s, w_ref[...], preferred_element_type=F32,
                         precision=lax.Precision.HIGHEST) + b_ref[...]


def _ada_call(c, w_ada, b_ada):
    rows, d = c.shape
    n = w_ada.shape[1]
    tn = 1536
    assert n % tn == 0
    return pl.pallas_call(
        _ada_kernel,
        grid=(n // tn,),
        in_specs=[pl.BlockSpec((rows, d), lambda j: (0, 0)),
                  pl.BlockSpec((d, tn), lambda j: (0, j)),
                  pl.BlockSpec((1, tn), lambda j: (0, j))],
        out_specs=pl.BlockSpec((rows, tn), lambda j: (0, j)),
        out_shape=jax.ShapeDtypeStruct((rows, n), F32),
        compiler_params=_params(1),
        name="ada_mod",
    )(c, w_ada, b_ada.reshape(1, n))


def _modulated_norm(x, g, sc, sh):
    y = x * lax.rsqrt(jnp.mean(x * x, axis=-1, keepdims=True) + EPS) * g
    return y * (1.0 + sc) + sh


def _norm_proj_kernel(x_ref, sc_ref, sh_ref, g_ref, w_ref, *out_refs, defs):
    h = _modulated_norm(x_ref[...], g_ref[...], sc_ref[...], sh_ref[...]).astype(BF16)
    done = {}
    tm = x_ref.shape[0]
    for (start, width, store_w, mult, order), o_ref in zip(defs, out_refs):
        if (start, width) not in done:
            done[(start, width)] = jnp.dot(h, w_ref[:, start:start + width], preferred_element_type=F32)
        r = done[(start, width)][:, :store_w]
        if order is None:
            o_ref[...] = (r if mult is None else r * mult).astype(o_ref.dtype)
        else:
            for p, cb in enumerate(order):
                o_ref[pl.ds(p, tm, stride=len(order)), :] = r[:, cb * LANES:(cb + 1) * LANES].astype(o_ref.dtype)


def _norm_proj_call(x2d, sc, sh, mod_map, g, w_bf, defs, dtypes, tm, name):
    n, d = x2d.shape
    assert n % tm == 0
    r = sc.shape[1]
    shape = lambda rows, d_: (rows, d_[2]) if d_[4] is None else (rows * len(d_[4]), LANES)
    out_shapes = [jax.ShapeDtypeStruct(shape(n, d_), dt) for d_, dt in zip(defs, dtypes)]
    out_specs = [pl.BlockSpec(shape(tm, d_), lambda i: (i, 0)) for d_ in defs]
    return pl.pallas_call(
        functools.partial(_norm_proj_kernel, defs=tuple(defs)),
        grid=(n // tm,),
        in_specs=[pl.BlockSpec((tm, d), lambda i: (i, 0)),
                  pl.BlockSpec((None, r, d), mod_map),
                  pl.BlockSpec((None, r, d), mod_map),
                  pl.BlockSpec((1, d), lambda i: (0, 0)),
                  _resident(w_bf.shape, lambda i: (0, 0))],
        out_specs=out_specs,
        out_shape=out_shapes,
        compiler_params=_params(1),
        name=name,
    )(x2d, sc, sh, g.reshape(1, d), w_bf)


def _bias_tile_kernel(tab_ref, o_ref, *, qb):
    _, rows, width = o_ref.shape
    r = lax.broadcasted_iota(jnp.int32, (rows, width), 0)
    koff = lax.broadcasted_iota(jnp.int32, (rows, width), 1) - LANES
    rel = koff - r
    n = jnp.abs(rel)
    large = jnp.full((rows, width), _MAX_EXACT, jnp.int32)
    for th in _BUCKET_THS:
        large = large + jnp.where(n >= th, 1, 0)
    bucket = jnp.where(rel > 0, _NB, 0) + jnp.where(n < _MAX_EXACT, n, large)
    visible = ((koff >> 6) <= (r >> 6)) & (koff < qb)
    for h in range(N_ATTN_HEADS):
        b = jnp.zeros((rows, width), F32)
        for bk in range(N_REL_BUCKETS):
            b = jnp.where(bucket == bk, tab_ref[bk, h], b)
        o_ref[h] = jnp.where(visible, b, NEG_BIG)


def _bias_tile_call(rel_bias, qb):
    qbr = -(-qb // LANES) * LANES
    return pl.pallas_call(
        functools.partial(_bias_tile_kernel, qb=qb),
        in_specs=[pl.BlockSpec(memory_space=pltpu.SMEM)],
        out_specs=pl.BlockSpec(memory_space=pltpu.VMEM),
        out_shape=jax.ShapeDtypeStruct((N_ATTN_HEADS, qb, LANES + qbr), F32),
        name="bias_tiles",
    )(rel_bias)


def _sweep_far(q0, qb, kb, fn):
    far_end = jnp.maximum(q0 - LANES, 0)
    n_full = far_end // kb
    sub = kb // LANES

    def whole(j):
        fn(pl.multiple_of(j * kb, kb), j * sub, None)

    def pair(j, carry):
        whole(2 * j)
        whole(2 * j + 1)
        return carry

    lax.fori_loop(0, n_full // 2, pair, 0)
    pl.when(n_full % 2 == 1)(lambda: whole(n_full - 1))

    @pl.when(far_end > n_full * kb)
    def _():
        start = pl.multiple_of(n_full * kb, kb)
        col = start + lax.broadcasted_iota(jnp.int32, (qb, LANES), 1)
        fn(start, n_full * sub, [jnp.where(col + c * LANES < far_end, 0.0, NEG_BIG) for c in range(sub)])


def _key_rows(ref, start, width, pieces=None):
    if pieces is None:
        blk = ref[pl.ds(start, width), :].astype(BF16)
        pad = -blk.shape[1] % LANES
        return blk if pad == 0 else jnp.concatenate([blk, jnp.zeros((width, pad), BF16)], axis=1)
    r, rows = pieces
    window = ref.at[pl.ds(pl.multiple_of(start * r, LANES * r), width * r), :]
    return jnp.concatenate([window[pl.ds(p, width, stride=r), :].astype(BF16) for p in rows], axis=1)


def _dsa_kernel(tab_ref, q_ref, qi_ref, iw_ref, k_ref, v_ref, ki_ref, kc_ref, vc_ref, kic_ref, tile_ref, o_ref,
                qis_scr, keys_scr, half_scr, best_scr, thr_scr, mfar_scr, mnear_scr, l_scr, acc_scr, *, qb, kb, qoff, topk,
                kv_pieces):
    far_k = lambda start, width: _key_rows(k_ref, start, width, kv_pieces)
    far_v = lambda start, width: _key_rows(v_ref, start, width, kv_pieces)
    i = pl.program_id(1)
    q0 = qoff + i * qb
    qbr = tile_ref.shape[2] - LANES
    sub = kb // LANES
    w = iw_ref[:, :IDX_HEADS] * (IDX_HEADS ** -0.5 * IDX_DIM ** -0.5)

    for h in range(IDX_HEADS):
        qis_scr[h * qb:(h + 1) * qb, :] = qi_ref[:, h * LANES:(h + 1) * LANES]
    heads_per_dot = max(1, min(IDX_HEADS, 1024 // qb))

    def score_keys(kib):
        width = kib.shape[0]
        acc = jnp.zeros((qb, width), F32)
        for h0 in range(0, IDX_HEADS, heads_per_dot):
            d = lax.dot_general(qis_scr[h0 * qb:(h0 + heads_per_dot) * qb, :], kib, NT_DIMS,
                                preferred_element_type=F32)
            for h in range(h0, h0 + heads_per_dot):
                acc = acc + w[:, h:h + 1] * jnp.maximum(d[(h - h0) * qb:(h - h0 + 1) * qb], 0.0)
        bits = pltpu.bitcast(acc, jnp.int32)
        return bits ^ ((bits >> 31) & 0x7FFFFFFF)

    n_score = (q0 + kb - 1) // kb

    def store_keys(slab0, key):
        for c in range(key.shape[1] // LANES):
            part = key[:, c * LANES:(c + 1) * LANES]
            keys_scr[slab0 + c] = part
            half_scr[slab0 + c] = (part >> 16).astype(jnp.int16)

    def score_body(j, carry):
        store_keys(j * sub, score_keys(_key_rows(ki_ref, pl.multiple_of(j * kb, kb), kb)))
        return carry

    lax.fori_loop(0, n_score, score_body, 0)

    cur_blk = q0 // LANES
    key = score_keys(kic_ref[...])
    store_keys(cur_blk, jnp.where(tile_ref[0, :, LANES:] > 0.5 * NEG_BIG, key, INT_MIN))
    n_blk = cur_blk + qbr // LANES

    rb = min(qb, LANES)
    lo16 = -(2 ** 15)
    n4 = n_blk // 4

    chunks = [pl.ds(c * rb, rb) for c in range(qb // rb)]
    ones = jnp.ones((LANES, LANES), BF16)

    def count_ge(ts):
        tbs = [t.astype(jnp.int16) for t in ts]

        def count(j, cnts):
            return tuple(cnt + jnp.where(half_scr[j, rows, :] >= tb, jnp.int16(1), jnp.int16(0))
                         for cnt, rows, tb in zip(cnts, chunks, tbs))

        def body4(j, cnts):
            for u in range(4):
                cnts = count(4 * j + u, cnts)
            return cnts

        cnts = lax.fori_loop(0, n4, body4, tuple(jnp.zeros((rb, LANES), jnp.int16) for _ in chunks))
        cnts = lax.fori_loop(4 * n4, n_blk, count, cnts)
        return [jnp.dot(cnt.astype(F32).astype(BF16), ones, preferred_element_type=F32) for cnt in cnts]

    def count_ge_best(ts):
        cnts = [sum(jnp.where(best_scr[u, rows, :] >= t.astype(jnp.int16), jnp.int16(1), jnp.int16(0))
                    for u in range(TOP_LOW)) for rows, t in zip(chunks, ts)]
        return [jnp.dot(cnt.astype(F32).astype(BF16), ones, preferred_element_type=F32) for cnt in cnts]

    def search(needed, counter):
        def bit_body(it, ts):
            cands = [t + (jnp.int32(1) << (15 - it)) for t in ts]
            return tuple(jnp.where(c >= n, cand, t) for c, n, cand, t in zip(counter(cands), needed, cands, ts))

        return lax.fori_loop(0, 16, bit_body, tuple(jnp.full((rb, LANES), lo16, jnp.int32) for _ in chunks))

    t_hi = search([float(topk)] * len(chunks), count_ge)
    top = -lo16 - 1
    above = [jnp.where(t == top, 0.0, c)
             for t, c in zip(t_hi, count_ge([jnp.minimum(t + 1, top) for t in t_hi]))]

    def low_plane(rows, tb):
        def body(j, carry):
            best, spill = carry[:TOP_LOW], carry[TOP_LOW]
            k32 = keys_scr[j, rows, :]
            a = jnp.where((k32 >> 16) == tb, (k32 & 0xFFFF) + lo16, lo16).astype(jnp.int16)
            half_scr[j, rows, :] = a
            kept = []
            for b in best:
                keep_b = b >= a
                kept.append(jnp.where(keep_b, b, a))
                a = jnp.where(keep_b, a, b)
            return tuple(kept) + (jnp.where(spill >= a, spill, a),)

        lowest = jnp.full((rb, LANES), lo16, jnp.int16)
        out = lax.fori_loop(0, n_blk, body, (lowest,) * (TOP_LOW + 1))
        for u in range(TOP_LOW):
            best_scr[u, rows, :] = out[u]
        return jnp.max(out[TOP_LOW].astype(jnp.int32))

    spilled = functools.reduce(jnp.maximum, [low_plane(rows, tb) for rows, tb in zip(chunks, t_hi)])
    needed = [topk - a for a in above]
    t_lo = lax.cond(spilled > lo16, lambda: search(needed, count_ge), lambda: search(needed, count_ge_best))
    for rows, th, tl in zip(chunks, t_hi, t_lo):
        thr_scr[rows, :] = jnp.maximum(th * 65536 + (tl - lo16), INT_MIN + 1)

    def count32(pred):
        def count(j, cnts):
            return tuple(cnt + jnp.where(pred(keys_scr[j, rows, :], thr_scr[rows, :]), 1, 0)
                         for cnt, rows in zip(cnts, chunks))

        cnts = lax.fori_loop(0, n_blk, count, tuple(jnp.zeros((rb, LANES), jnp.int32) for _ in chunks))
        return [jnp.dot(cnt.astype(F32).astype(BF16), ones, preferred_element_type=F32) for cnt in cnts]

    surplus = functools.reduce(jnp.maximum, [jnp.max(c) for c in count32(lambda k, t: k >= t)]) - topk

    @pl.when(surplus > 0)
    def _():
        keep = [topk - c for c in count32(lambda k, t: k > t)]
        upto = jnp.where(lax.broadcasted_iota(jnp.int32, (LANES, LANES), 0)
                         <= lax.broadcasted_iota(jnp.int32, (LANES, LANES), 1), 1.0, 0.0).astype(BF16)

        def demote(j, seen):
            out = []
            for rows, kp, sn in zip(chunks, keep, seen):
                k32 = keys_scr[j, rows, :]
                tied = k32 == thr_scr[rows, :]
                tied_bf = jnp.where(tied, 1.0, 0.0).astype(BF16)
                rank = sn + jnp.dot(tied_bf, upto, preferred_element_type=F32)
                keys_scr[j, rows, :] = k32 - jnp.where(tied & (rank > kp), 1, 0)
                out.append(sn + jnp.dot(tied_bf, ones, preferred_element_type=F32))
            return tuple(out)

        lax.fori_loop(0, n_blk, demote, tuple(jnp.zeros((rb, LANES), F32) for _ in chunks))

    hd = A_HEAD_DIM

    def select_mask(blk0, n):
        return [jnp.where(keys_scr[blk0 + c] >= thr_scr[...], 0.0, NEG_BIG) for c in range(n)]

    def logits(h, kblk):
        g = h // A_REP
        s = lax.dot_general(q_ref[:, h * hd:(h + 1) * hd], kblk[:, g * hd:(g + 1) * hd], NT_DIMS,
                            preferred_element_type=F32)
        return [s[:, c * LANES:(c + 1) * LANES] for c in range(s.shape[1] // LANES)]

    def tile_cols(h, col0, n):
        return [tile_ref[h, :, col0 + c * LANES:col0 + (c + 1) * LANES] for c in range(n)]

    mfar_scr[...] = jnp.full(mfar_scr.shape, NEG_BIG, F32)
    mnear_scr[...] = jnp.full(mnear_scr.shape, NEG_BIG, F32)

    def far_mask(blk0, colmask):
        am = select_mask(blk0, sub)
        return am if colmask is None else [a + c for a, c in zip(am, colmask)]

    def max_far(start, blk0, colmask):
        kblk = far_k(start, kb)
        am = far_mask(blk0, colmask)
        for h in range(A_HEADS):
            mfar_scr[h] = functools.reduce(
                jnp.maximum, [s + a for s, a in zip(logits(h, kblk), am)], mfar_scr[h])

    def max_near(kblk, blk0, col0):
        n = kblk.shape[0] // LANES
        am = select_mask(blk0, n)
        for h in range(A_HEADS):
            mnear_scr[h] = functools.reduce(
                jnp.maximum, [s + a + b for s, a, b in zip(logits(h, kblk), am, tile_cols(h, col0, n))],
                mnear_scr[h])

    prev_start = pl.multiple_of(q0 - LANES, LANES)
    _sweep_far(q0, qb, kb, max_far)
    pl.when(q0 > 0)(lambda: max_near(far_k(prev_start, LANES), cur_blk - 1, 0))
    max_near(kc_ref[...], cur_blk, LANES)

    for h in range(A_HEADS):
        fb = tab_ref[FAR_BUCKET, h]
        m = jnp.maximum(jnp.max(mfar_scr[h], axis=-1, keepdims=True) + fb,
                        jnp.max(mnear_scr[h], axis=-1, keepdims=True))
        mfar_scr[h] = jnp.broadcast_to(fb - m, (qb, LANES))
        mnear_scr[h] = jnp.broadcast_to(-m, (qb, LANES))
    l_scr[...] = jnp.zeros(l_scr.shape, F32)
    acc_scr[...] = jnp.zeros(acc_scr.shape, F32)

    def accumulate(h, p, vblk):
        g = h // A_REP
        l_scr[h] += functools.reduce(jnp.add, p)
        acc_scr[h] += jnp.dot(jnp.concatenate(p, axis=1).astype(BF16), vblk[:, g * hd:(g + 1) * hd],
                              preferred_element_type=F32)

    def acc_far(start, blk0, colmask):
        kblk = far_k(start, kb)
        vblk = far_v(start, kb)
        am = far_mask(blk0, colmask)
        for h in range(A_HEADS):
            shift = mfar_scr[h]
            accumulate(h, [jnp.exp(s + a + shift) for s, a in zip(logits(h, kblk), am)], vblk)

    def acc_near(kblk, vblk, blk0, col0):
        n = kblk.shape[0] // LANES
        am = select_mask(blk0, n)
        for h in range(A_HEADS):
            shift = mnear_scr[h]
            accumulate(h, [jnp.exp(s + a + b + shift) for s, a, b in
                           zip(logits(h, kblk), am, tile_cols(h, col0, n))], vblk)

    _sweep_far(q0, qb, kb, acc_far)
    pl.when(q0 > 0)(lambda: acc_near(far_k(prev_start, LANES), far_v(prev_start, LANES),
                                     cur_blk - 1, 0))
    acc_near(kc_ref[...], vc_ref[...], cur_blk, LANES)

    for h in range(A_HEADS):
        o_ref[:, h * hd:(h + 1) * hd] = (
            acc_scr[h] / jnp.sum(l_scr[h], axis=-1, keepdims=True)).astype(o_ref.dtype)


def _dsa_call(rel_bias, tiles, q, qi, iw, far, cur, *, qb, kb, qoff, topk, kv_pieces=None):
    b, t, _ = q.shape
    lf = far[2].shape[1]
    qbr = tiles.shape[2] - LANES
    q0_max = qoff + t - qb
    far_end = -(-q0_max // kb) * kb
    assert t % qb == 0 and qoff % LANES == 0 and far_end <= lf and cur[0].shape[1] == (t // qb) * qbr
    n_slabs = max(far_end, q0_max + qbr) // LANES
    mode = {"pipeline_mode": pl.Buffered(1)} if t // qb > 1 else {}
    far_spec = lambda a: pl.BlockSpec((None,) + a.shape[1:], lambda bi, i: (bi, 0, 0), **mode)
    cur_spec = lambda a: pl.BlockSpec((None, qbr, a.shape[2]), lambda bi, i: (bi, i, 0))
    return pl.pallas_call(
        functools.partial(_dsa_kernel, qb=qb, kb=kb, qoff=qoff, topk=topk, kv_pieces=kv_pieces),
        grid=(b, t // qb),
        in_specs=[pl.BlockSpec(memory_space=pltpu.SMEM),
                  pl.BlockSpec((None, qb, q.shape[2]), lambda bi, i: (bi, i, 0)),
                  pl.BlockSpec((None, qb, qi.shape[2]), lambda bi, i: (bi, i, 0)),
                  pl.BlockSpec((None, qb, iw.shape[2]), lambda bi, i: (bi, i, 0)),
                  far_spec(far[0]), far_spec(far[1]), far_spec(far[2]),
                  cur_spec(cur[0]), cur_spec(cur[1]), cur_spec(cur[2]),
                  _resident(tiles.shape, lambda bi, i: (0, 0, 0))],
        out_specs=pl.BlockSpec((None, qb, A_HEADS * A_HEAD_DIM), lambda bi, i: (bi, i, 0)),
        out_shape=jax.ShapeDtypeStruct((b, t, A_HEADS * A_HEAD_DIM), BF16),
        scratch_shapes=[pltpu.VMEM((IDX_HEADS * qb, LANES), BF16),
                        pltpu.VMEM((n_slabs, qb, LANES), jnp.int32),
                        pltpu.VMEM((n_slabs, qb, LANES), jnp.int16),
                        pltpu.VMEM((TOP_LOW, qb, LANES), jnp.int16),
                        pltpu.VMEM((qb, LANES), jnp.int32),
                        pltpu.VMEM((A_HEADS, qb, LANES), F32),
                        pltpu.VMEM((A_HEADS, qb, LANES), F32),
                        pltpu.VMEM((A_HEADS, qb, LANES), F32),
                        pltpu.VMEM((A_HEADS, qb, A_HEAD_DIM), F32)],
        compiler_params=_params(2),
        name="mixer_a",
    )(rel_bias, q, qi, iw, *far, *cur, tiles)


def _diff_kernel(tab_ref, lam_ref, sg_ref, q_ref, k_ref, v_ref, kc_ref, vc_ref, tile_ref, o_ref,
                 mfar_scr, mnear_scr, l_scr, acc_scr, *, qb, kb, qoff, lam_init):
    head = pl.program_id(1)
    i = pl.program_id(2)
    q0 = qoff + i * qb
    hd = B_HEAD_DIM
    far_bias = tab_ref[FAR_BUCKET, A_HEADS + head]
    far_k = lambda start, width: _key_rows(k_ref, start, width)
    far_v = lambda start, width: _key_rows(v_ref, start, width)

    def logits(c, kblk):
        s = lax.dot_general(q_ref[:, c * hd:(c + 1) * hd], kblk[:, c * hd:(c + 1) * hd], NT_DIMS,
                            preferred_element_type=F32)
        return [s[:, u * LANES:(u + 1) * LANES] for u in range(s.shape[1] // LANES)]

    def tile_cols(col0, n):
        return [tile_ref[0, :, col0 + u * LANES:col0 + (u + 1) * LANES] for u in range(n)]

    mfar_scr[...] = jnp.full(mfar_scr.shape, NEG_BIG, F32)
    mnear_scr[...] = jnp.full(mnear_scr.shape, NEG_BIG, F32)

    def max_far(start, blk0, colmask):
        kblk = far_k(start, kb)
        for c in range(2):
            s = logits(c, kblk)
            if colmask is not None:
                s = [x + m for x, m in zip(s, colmask)]
            mfar_scr[c] = functools.reduce(jnp.maximum, s, mfar_scr[c])

    def max_near(kblk, col0):
        for c in range(2):
            mnear_scr[c] = functools.reduce(
                jnp.maximum, [s + b for s, b in zip(logits(c, kblk), tile_cols(col0, kblk.shape[0] // LANES))],
                mnear_scr[c])

    prev_start = pl.multiple_of(q0 - LANES, LANES)
    _sweep_far(q0, qb, kb, max_far)
    pl.when(q0 > 0)(lambda: max_near(far_k(prev_start, LANES), 0))
    max_near(kc_ref[...], LANES)

    for c in range(2):
        m = jnp.maximum(jnp.max(mfar_scr[c], axis=-1, keepdims=True) + far_bias,
                        jnp.max(mnear_scr[c], axis=-1, keepdims=True))
        mfar_scr[c] = jnp.broadcast_to(far_bias - m, (qb, LANES))
        mnear_scr[c] = jnp.broadcast_to(-m, (qb, LANES))
    l_scr[...] = jnp.zeros(l_scr.shape, F32)
    acc_scr[...] = jnp.zeros(acc_scr.shape, F32)

    def accumulate(c, p, vblk):
        l_scr[c] += functools.reduce(jnp.add, p)
        acc_scr[c] += jnp.dot(jnp.concatenate(p, axis=1).astype(BF16), vblk, preferred_element_type=F32)

    def acc_far(start, blk0, colmask):
        kblk = far_k(start, kb)
        vblk = far_v(start, kb)
        for c in range(2):
            shift = mfar_scr[c]
            shifts = [shift] * (kb // LANES) if colmask is None else [shift + m for m in colmask]
            accumulate(c, [jnp.exp(s + sh) for s, sh in zip(logits(c, kblk), shifts)], vblk)

    def acc_near(kblk, vblk, col0):
        for c in range(2):
            shift = mnear_scr[c]
            accumulate(c, [jnp.exp(s + b + shift) for s, b in
                           zip(logits(c, kblk), tile_cols(col0, kblk.shape[0] // LANES))], vblk)

    _sweep_far(q0, qb, kb, acc_far)
    pl.when(q0 > 0)(lambda: acc_near(far_k(prev_start, LANES), far_v(prev_start, LANES), 0))
    acc_near(kc_ref[...], vc_ref[...], LANES)

    lam = lam_ref[0, 0]
    o = (acc_scr[0] / jnp.sum(l_scr[0], axis=-1, keepdims=True)
         - lam * (acc_scr[1] / jnp.sum(l_scr[1], axis=-1, keepdims=True)))
    o = o * lax.rsqrt(jnp.mean(o * o, axis=-1, keepdims=True) + EPS) * sg_ref[...]
    o_ref[...] = (o * (1.0 - lam_init)).astype(o_ref.dtype)


def _diff_call(rel_bias, tiles, lam, subln_g, q, far, cur, *, qb, kb, qoff, lam_init):
    b, t, _ = q.shape
    lf = far[0].shape[1]
    hw = 2 * B_HEAD_DIM
    qbr = tiles.shape[2] - LANES
    q0_max = qoff + t - qb
    far_end = -(-max(q0_max - LANES, 0) // kb) * kb
    assert t % qb == 0 and qoff % LANES == 0 and far_end <= lf and cur[0].shape[1] == (t // qb) * qbr
    far_spec = pl.BlockSpec((None, lf, hw), lambda bi, h, i: (bi, 0, h))
    cur_spec = pl.BlockSpec((None, qbr, hw), lambda bi, h, i: (bi, i, h))
    return pl.pallas_call(
        functools.partial(_diff_kernel, qb=qb, kb=kb, qoff=qoff, lam_init=lam_init),
        grid=(b, B_HEADS, t // qb),
        in_specs=[pl.BlockSpec(memory_space=pltpu.SMEM),
                  pl.BlockSpec(memory_space=pltpu.SMEM),
                  pl.BlockSpec((1, hw), lambda bi, h, i: (0, 0)),
                  pl.BlockSpec((None, qb, hw), lambda bi, h, i: (bi, i, h)),
                  far_spec, far_spec, cur_spec, cur_spec,
                  pl.BlockSpec((1, qb, tiles.shape[2]), lambda bi, h, i: (A_HEADS + h, 0, 0))],
        out_specs=pl.BlockSpec((None, qb, hw), lambda bi, h, i: (bi, i, h)),
        out_shape=jax.ShapeDtypeStruct((b, t, B_HEADS * hw), BF16),
        scratch_shapes=[pltpu.VMEM((2, qb, LANES), F32),
                        pltpu.VMEM((2, qb, LANES), F32),
                        pltpu.VMEM((2, qb, LANES), F32),
                        pltpu.VMEM((2, qb, hw), F32)],
        compiler_params=_params(3),
        name="mixer_b",
    )(rel_bias, lam, subln_g.reshape(1, hw), q, *far, *cur, tiles)


def _lam_kernel(dl_ref, o_ref, *, lam_init):
    dl = dl_ref[...]
    s1 = jnp.sum(dl[0:1] * dl[1:2], axis=-1, keepdims=True)
    s2 = jnp.sum(dl[2:3] * dl[3:4], axis=-1, keepdims=True)
    o_ref[...] = jnp.exp(s1) - jnp.exp(s2) + lam_init


def _lam_call(diff_lam, lam_init):
    return pl.pallas_call(
        functools.partial(_lam_kernel, lam_init=lam_init),
        out_shape=jax.ShapeDtypeStruct((1, 1), F32),
        name="diff_lambda",
    )(diff_lam)


def _route(logits_t, rbias, counts):
    tm = logits_t.shape[1]
    per = N_EXPERTS // N_GROUPS
    scores = jax.nn.sigmoid(logits_t).reshape(N_GROUPS, per, tm)
    ch = scores + rbias.reshape(N_GROUPS, per, 1)
    sub = lax.broadcasted_iota(jnp.int32, ch.shape, 1)
    grp = lax.broadcasted_iota(jnp.int32, ch.shape, 0)
    m1 = jnp.max(ch, axis=1, keepdims=True)
    i1 = jnp.min(jnp.where(ch == m1, sub, per), axis=1, keepdims=True)
    m2 = jnp.max(jnp.where(sub == i1, -jnp.inf, ch), axis=1, keepdims=True)
    gs = jnp.broadcast_to(m1 + m2, ch.shape)
    rank = jnp.zeros(ch.shape, jnp.int32)
    for g2 in range(N_GROUPS):
        o = gs[g2:g2 + 1]
        rank = rank + jnp.where(o > gs, 1, 0) + jnp.where(o == gs, 1, 0) * jnp.where(grp > g2, 1, 0)
    cm = jnp.where(rank < TOPK_GROUPS, ch, -jnp.inf)
    eid = grp * per + sub
    erank = jnp.zeros(cm.shape, jnp.int32)
    for e2 in range(N_EXPERTS):
        o = cm[e2 // per:e2 // per + 1, e2 % per:e2 % per + 1, :]
        erank = erank + jnp.where(o > cm, 1, 0) + jnp.where(o == cm, 1, 0) * jnp.where(eid > e2, 1, 0)
    chosen = erank < TOP_K
    sel = jnp.where(chosen, scores, 0.0)
    denom = jnp.sum(jnp.sum(sel, axis=1, keepdims=True), axis=0, keepdims=True)
    gates = sel / denom * ROUTED_SCALE

    ones = jnp.where(chosen, 1.0, 0.0).reshape(N_EXPERTS, tm)
    earlier = (lax.broadcasted_iota(jnp.int32, (tm, tm), 0) < lax.broadcasted_iota(jnp.int32, (tm, tm), 1))
    before = jnp.dot(ones.astype(BF16), jnp.where(earlier, 1.0, 0.0).astype(BF16), preferred_element_type=F32)
    pos = (before + counts.astype(F32)).reshape(N_GROUPS, per, tm)
    new_counts = counts + jnp.sum(ones, axis=1, keepdims=True).astype(jnp.int32)

    def per_slot(values):
        rows = [jnp.sum(jnp.sum(jnp.where(erank == k, values, 0.0), axis=1, keepdims=True), axis=0)
                for k in range(TOP_K)]
        return jnp.concatenate(rows, axis=0)

    return (per_slot(eid.astype(F32)).astype(jnp.int32), per_slot(pos).astype(jnp.int32), per_slot(gates),
            new_counts)


def _pack_pair(a, b):
    ua = pltpu.bitcast(a.astype(BF16).astype(F32), jnp.uint32)
    ub = pltpu.bitcast(b.astype(BF16).astype(F32), jnp.uint32)
    return pltpu.bitcast(ua | (ub >> 16), jnp.int32)


def _unpack_pair(p):
    u = pltpu.bitcast(p, jnp.uint32)
    a = pltpu.bitcast(u & jnp.uint32(0xFFFF0000), F32)
    b = pltpu.bitcast(u << 16, F32)
    return a.astype(BF16), b.astype(BF16)


PARTS = 4
PART_W = D_MODEL // 2 // PARTS


def _pack_rows(x):
    half = x.shape[1] // 2
    return jnp.stack([_pack_pair(x[:, j * PART_W:(j + 1) * PART_W],
                                 x[:, half + j * PART_W:half + (j + 1) * PART_W]) for j in range(PARTS)])


def _packed_matmul(parts_ref, w_ref):
    half = w_ref.shape[0] // 2
    acc = None
    for j in range(PARTS):
        a, b = _unpack_pair(parts_ref[j])
        t = (jnp.dot(a, w_ref[j * PART_W:(j + 1) * PART_W, :], preferred_element_type=F32)
             + jnp.dot(b, w_ref[half + j * PART_W:half + (j + 1) * PART_W, :], preferred_element_type=F32))
        acc = t if acc is None else acc + t
    return acc


def _outproj_kernel(x_ref, oa_ref, ob_ref, ga_ref, sc_ref, sh_ref, g_ref, w_ref, wr_ref, rb_ref, cin_ref,
                    x1_ref, h2_ref, ek_ref, pk_ref, gk_ref, cout_ref, cnt_scr):
    @pl.when(pl.program_id(0) == 0)
    def _():
        cnt_scr[...] = cin_ref[...]

    half = oa_ref.shape[1]
    mix = (jnp.dot(oa_ref[...], w_ref[:half, :], preferred_element_type=F32)
           + jnp.dot(ob_ref[...], w_ref[half:, :], preferred_element_type=F32))
    x1 = x_ref[...] + ga_ref[...] * mix
    x1_ref[...] = x1
    h2 = _modulated_norm(x1, g_ref[...], sc_ref[...], sh_ref[...])
    h2_ref[...] = _pack_rows(h2)
    logits_t = lax.dot_general(wr_ref[...], h2, NT_DIMS, preferred_element_type=F32,
                               precision=lax.Precision.HIGHEST)
    ek, pk, gk, counts = _route(logits_t, rb_ref[...], cnt_scr[...])
    ek_ref[...] = ek
    pk_ref[...] = pk
    gk_ref[...] = gk
    cnt_scr[...] = counts
    cout_ref[...] = counts


def _outproj_call(x2d, oa, ob, ga, sc, sh, mod_map, g, w_bf, wr_t, rbias, counts_in, tm):
    n, d = x2d.shape
    assert n % tm == 0
    r = sc.shape[1]
    mod_spec = pl.BlockSpec((None, r, d), mod_map)
    slot_spec = pl.BlockSpec((TOP_K, tm), lambda i: (0, i))
    return pl.pallas_call(
        _outproj_kernel,
        grid=(n // tm,),
        in_specs=[pl.BlockSpec((tm, d), lambda i: (i, 0)),
                  pl.BlockSpec((tm, oa.shape[1]), lambda i: (i, 0)),
                  pl.BlockSpec((tm, ob.shape[1]), lambda i: (i, 0)),
                  mod_spec, mod_spec, mod_spec,
                  pl.BlockSpec((1, d), lambda i: (0, 0)),
                  _resident(w_bf.shape, lambda i: (0, 0)),
                  _resident(wr_t.shape, lambda i: (0, 0)),
                  pl.BlockSpec((N_EXPERTS, 1), lambda i: (0, 0)),
                  pl.BlockSpec((N_EXPERTS, 1), lambda i: (0, 0))],
        out_specs=[pl.BlockSpec((tm, d), lambda i: (i, 0)),
                   pl.BlockSpec((PARTS, tm, PART_W), lambda i: (0, i, 0)),
                   slot_spec, slot_spec, slot_spec,
                   pl.BlockSpec((N_EXPERTS, 1), lambda i: (0, 0))],
        out_shape=[jax.ShapeDtypeStruct((n, d), F32),
                   jax.ShapeDtypeStruct((PARTS, n, PART_W), jnp.int32),
                   jax.ShapeDtypeStruct((TOP_K, n), jnp.int32),
                   jax.ShapeDtypeStruct((TOP_K, n), jnp.int32),
                   jax.ShapeDtypeStruct((TOP_K, n), F32),
                   jax.ShapeDtypeStruct((N_EXPERTS, 1), jnp.int32)],
        scratch_shapes=[pltpu.VMEM((N_EXPERTS, 1), jnp.int32)],
        compiler_params=_params(1),
        name="out_proj_router",
    )(x2d, oa, ob, ga, sc, sh, g.reshape(1, d), w_bf, wr_t, rbias, counts_in)


SC_WINDOW = 128
EXPERT_TILE = 512


def _sc_mesh():
    return plsc.VectorSubcoreMesh(core_axis_name="core", subcore_axis_name="subcore")


def _sc_scatter_rows(x, idx, n_out):
    n, w = x.shape
    steps = n // SC_WINDOW
    total = idx.shape[1] // SC_WINDOW

    @pl.kernel(out_type=jax.ShapeDtypeStruct((n_out, w), x.dtype), mesh=_sc_mesh(), scratch_types=[],
               name="moe_dispatch")
    def scatter(x_hbm, i_hbm, o_hbm):
        def body(x_vmem, i_vmem):
            pltpu.sync_copy(x_vmem, o_hbm.at[i_vmem.at[0]])

        pltpu.emit_pipeline(
            body, grid=(total,),
            in_specs=[pl.BlockSpec((SC_WINDOW, w), lambda i: (i % steps, 0)),
                      pl.BlockSpec((1, SC_WINDOW), lambda i: (0, i))],
            out_specs=[], core_axis_name=("core", "subcore"),
            dimension_semantics=(pltpu.PARALLEL,))(x_hbm, i_hbm)

    return scatter(x, idx)


def _sc_gather_rows(y, idx):
    w = y.shape[1]
    n = idx.shape[1]

    @pl.kernel(out_type=jax.ShapeDtypeStruct((n, w), y.dtype), mesh=_sc_mesh(), scratch_types=[],
               name="moe_collect")
    def gather(y_hbm, i_hbm, o_hbm):
        def body(i_vmem, o_vmem):
            pltpu.sync_copy(y_hbm.at[i_vmem.at[0]], o_vmem)

        pltpu.emit_pipeline(
            body, grid=(n // SC_WINDOW,),
            in_specs=[pl.BlockSpec((1, SC_WINDOW), lambda i: (0, i))],
            out_specs=[pl.BlockSpec((SC_WINDOW, w), lambda i: (i, 0))],
            core_axis_name=("core", "subcore"),
            dimension_semantics=(pltpu.PARALLEL,))(i_hbm, o_hbm)

    return gather(y, idx)


def _swiglu(parts_ref, wg_ref, wu_ref, wd_ref):
    a = _packed_matmul(parts_ref, wg_ref)
    u = _packed_matmul(parts_ref, wu_ref)
    hid = (a * jax.nn.sigmoid(a) * u).astype(BF16)
    return jnp.dot(hid, wd_ref[...], preferred_element_type=F32)


def _expert_kernel(te_ref, nu_ref, x_ref, wg_ref, wu_ref, wd_ref, y_ref, wg_scr, wu_scr, wd_scr):
    i = pl.program_id(0)

    @pl.when(i < nu_ref[0])
    def _():
        @pl.when((i == 0) | (te_ref[i] != te_ref[jnp.maximum(i - 1, 0)]))
        def _():
            wg_scr[...] = wg_ref[...].astype(BF16)
            wu_scr[...] = wu_ref[...].astype(BF16)
            wd_scr[...] = wd_ref[...].astype(BF16)

        y_ref[...] = _pack_rows(_swiglu(x_ref, wg_scr, wu_scr, wd_scr))


def _expert_call(tile_expert, n_used, x_sorted, wg, wu, wd):
    _, r, _ = x_sorted.shape
    ne, d, de = wg.shape
    n_tiles = r // EXPERT_TILE
    row_map = lambda i, te, nu: (0, jnp.minimum(i, nu[0] - 1), 0)
    return pl.pallas_call(
        _expert_kernel,
        grid_spec=pltpu.PrefetchScalarGridSpec(
            num_scalar_prefetch=2,
            grid=(n_tiles,),
            in_specs=[pl.BlockSpec((PARTS, EXPERT_TILE, PART_W), row_map),
                      pl.BlockSpec((None, d, de), lambda i, te, nu: (te[i], 0, 0)),
                      pl.BlockSpec((None, d, de), lambda i, te, nu: (te[i], 0, 0)),
                      pl.BlockSpec((None, de, d), lambda i, te, nu: (te[i], 0, 0))],
            out_specs=pl.BlockSpec((PARTS, EXPERT_TILE, PART_W), row_map),
            scratch_shapes=[pltpu.VMEM((d, de), BF16), pltpu.VMEM((d, de), BF16), pltpu.VMEM((de, d), BF16)]),
        out_shape=jax.ShapeDtypeStruct(x_sorted.shape, jnp.int32),
        compiler_params=_params(1),
        name="moe_experts",
    )(tile_expert, n_used, x_sorted, wg, wu, wd)


def _final_kernel(x_ref, h_ref, rows_ref, gk_ref, gf_ref, g_ref, sg_ref, su_ref, sd_ref, o_ref):
    moe = _swiglu(h_ref, sg_ref, su_ref, sd_ref)
    half = moe.shape[1] // 2
    lo = [moe[:, j * PART_W:(j + 1) * PART_W] for j in range(PARTS)]
    hi = [moe[:, half + j * PART_W:half + (j + 1) * PART_W] for j in range(PARTS)]
    for k in range(TOP_K):
        gate = gk_ref[:, k:k + 1]
        for j in range(PARTS):
            a, b = _unpack_pair(rows_ref[k, j])
            lo[j] = lo[j] + gate * a.astype(F32)
            hi[j] = hi[j] + gate * b.astype(F32)
    x = x_ref[...] + gf_ref[...] * jnp.concatenate(lo + hi, axis=1)
    o_ref[...] = x * lax.rsqrt(jnp.mean(x * x, axis=-1, keepdims=True) + EPS) * g_ref[...]


def _final_call(x1, h2p, rows, row_block0, gk_t, gf, mod_map, g, sg, su, sd, tm):
    n, d = x1.shape
    assert n % tm == 0
    r = gf.shape[1]
    return pl.pallas_call(
        _final_kernel,
        grid=(n // tm,),
        in_specs=[pl.BlockSpec((tm, d), lambda i: (i, 0)),
                  pl.BlockSpec((PARTS, tm, PART_W), lambda i: (0, i, 0)),
                  pl.BlockSpec((TOP_K, PARTS, tm, PART_W), lambda i: (0, 0, i + row_block0, 0)),
                  pl.BlockSpec((tm, TOP_K), lambda i: (i, 0)),
                  pl.BlockSpec((None, r, d), mod_map),
                  pl.BlockSpec((1, d), lambda i: (0, 0)),
                  _resident(sg.shape, lambda i: (0, 0)),
                  _resident(su.shape, lambda i: (0, 0)),
                  _resident(sd.shape, lambda i: (0, 0))],
        out_specs=pl.BlockSpec((tm, d), lambda i: (i, 0)),
        out_shape=jax.ShapeDtypeStruct((n, d), F32),
        compiler_params=_params(1),
        name="final_norm",
    )(x1, h2p, rows, gk_t, gf, g.reshape(1, d), sg, su, sd)


_QA, _KA, _VA, _QI, _KI, _IW, _QB, _KB, _VB = range(9)
_IN_SIZES = (A_HEADS * A_HEAD_DIM, A_KV_HEADS * A_HEAD_DIM, A_KV_HEADS * A_HEAD_DIM,
             IDX_HEADS * IDX_DIM, IDX_DIM, IDX_HEADS,
             B_HEADS * 2 * B_HEAD_DIM, B_HEADS * 2 * B_HEAD_DIM, B_HEADS * 2 * B_HEAD_DIM)


def _split_w_in(w_in):
    d = w_in.shape[0]
    offs = np.concatenate([[0], np.cumsum(_IN_SIZES)])
    cols = [w_in[:, offs[k]:offs[k + 1]].astype(BF16) for k in range(9)]
    zero = lambda n: jnp.zeros((d, n), BF16)
    qi = jnp.concatenate([cols[_QI].reshape(d, IDX_HEADS, IDX_DIM),
                          jnp.zeros((d, IDX_HEADS, LANES - IDX_DIM), BF16)], axis=-1).reshape(d, IDX_HEADS * LANES)
    w_a = jnp.concatenate([cols[_QA], cols[_KA], cols[_VA], qi,
                           cols[_KI], zero(LANES - IDX_DIM), cols[_IW], zero(LANES - IDX_HEADS)], axis=1)
    w_b = jnp.concatenate([cols[_QB], cols[_KB], cols[_VB]], axis=1)
    return w_a, w_b


def _group_forward(x, mod, past, rel_bias, lam, weights, *, lam_init, qb, qb_b, kb, tm, per_row_mod):
    (norm_a_g, w_a, w_b, subln_g) = weights
    b, t, d = x.shape
    n = b * t
    x2d = x.reshape(n, d)
    sh_a, sc_a = mod[0], mod[1]
    if per_row_mod:
        expand = lambda m: jnp.repeat(m, t, axis=0).reshape(1, n, d)
        mod_map = lambda i: (0, i, 0)
    else:
        expand = lambda m: m.reshape(b, 1, d)
        mod_map = lambda i: ((i * tm) // t, 0, 0)
    sc3, sh3 = expand(sc_a), expand(sh_a)

    na = A_HEADS * A_HEAD_DIM
    nk = A_KV_HEADS * A_HEAD_DIM
    nq = IDX_HEADS * LANES
    c_ka, c_va, c_qi = na, na + nk, na + 2 * nk
    c_ki, c_iw = c_qi + nq, c_qi + nq + LANES
    kv_order = tuple(range(A_KV_HEADS))
    defs_a = [(0, na, na, A_HEAD_DIM ** -0.5, None), (c_ka, nk, nk, None, kv_order), (c_ka, nk, nk, None, None),
              (c_va, nk, nk, None, kv_order), (c_va, nk, nk, None, None), (c_qi, nq, nq, None, None),
              (c_ki, LANES, IDX_DIM, None, None), (c_ki, LANES, LANES, None, None), (c_iw, LANES, LANES, None, None)]
    dt_a = [BF16, F32, BF16, F32, BF16, BF16, F32, BF16, F32]
    qa, ka, ka_bf, va, va_bf, qi, ki, ki_bf, iw = _norm_proj_call(
        x2d, sc3, sh3, mod_map, norm_a_g, w_a, defs_a, dt_a, tm, "in_proj_a")
    nb = B_HEADS * 2 * B_HEAD_DIM
    kb_order = tuple(range(2 * B_HEADS))
    vb_order = tuple(2 * h + half for half in range(2) for h in range(B_HEADS))
    defs_b = [(0, nb, nb, B_HEAD_DIM ** -0.5, None), (nb, nb, nb, None, kb_order), (nb, nb, nb, None, None),
              (2 * nb, nb, nb, None, vb_order), (2 * nb, nb, nb, None, None)]
    dt_b = [BF16, F32, BF16, F32, BF16]
    qbm, kbm, kb_bf, vbm, vb_bf = _norm_proj_call(
        x2d, sc3, sh3, mod_map, norm_a_g, w_b, defs_b, dt_b, tm, "in_proj_b")

    r3 = lambda a: a.reshape(b, t, a.shape[-1])
    cur = [r3(ka_bf), r3(va_bf), r3(ki_bf), r3(kb_bf), r3(vb_bf)]
    kv_pieces = None
    if past is None:
        qoff = 0
        far = cur
    else:
        assert t == qb, "a group with cached keys is one query block per batch row"
        qoff = past[0].shape[1]
        qbr = -(-qb // LANES) * LANES
        cur = [jnp.concatenate([a, jnp.zeros((b, qbr - t, a.shape[-1]), BF16)], axis=1) for a in cur]
        pk, pv, pi, pbk, pbv = past
        if qoff % kb == 0:
            kv_pieces = (A_KV_HEADS, tuple(range(A_KV_HEADS)))
            far = [pk.reshape(b, qoff * A_KV_HEADS, A_HEAD_DIM), pv.reshape(b, qoff * A_KV_HEADS, A_HEAD_DIM),
                   pi.reshape(b, qoff, -1), pbk.reshape(b, qoff, -1), pbv.reshape(b, qoff, -1)]
        else:
            far = [jnp.concatenate([p.reshape(b, qoff, -1), jnp.zeros((b, -qoff % kb, p[0, 0].size), p.dtype)], axis=1)
                   for p in past]
    length = qoff + t
    topk = min(TOPK_MAX, length // 4)

    tiles = _bias_tile_call(rel_bias, qb)
    tiles_b = tiles if qb_b == qb else _bias_tile_call(rel_bias, qb_b)
    out_a = _dsa_call(rel_bias, tiles, r3(qa), r3(qi), r3(iw), far[:3], cur[:3], qb=qb, kb=kb, qoff=qoff, topk=topk,
                      kv_pieces=kv_pieces)
    out_b = _diff_call(rel_bias, tiles_b, lam, subln_g, r3(qbm), far[3:], cur[3:], qb=qb_b, kb=kb, qoff=qoff,
                       lam_init=lam_init)
    rows = (ka.reshape(1, b, t, A_KV_HEADS, A_HEAD_DIM), va.reshape(1, b, t, A_KV_HEADS, A_HEAD_DIM),
            ki.reshape(1, b, t, IDX_DIM), kbm.reshape(1, b, t, B_HEADS, 2, B_HEAD_DIM),
            vbm.reshape(b, t, 2, B_HEADS, B_HEAD_DIM).transpose(0, 1, 3, 2, 4)
               .reshape(1, b, t, B_HEADS, 2 * B_HEAD_DIM))
    return x2d, out_a.reshape(n, -1), out_b.reshape(n, -1), rows, expand, mod_map


def kernel(x_prompt, x_sample, c_prompt, c_sample, cache_a_k, cache_a_v, cache_a_kidx, cache_b_k, cache_b_v,
           rel_bias, w_ada, b_ada, norm_a_g, w_in, w_out, diff_lam, subln_g, norm_f_g, w_router, router_bias,
           w_gate, w_up, w_down, ws_gate, ws_up, ws_down, final_g):
    assert w_ada.shape[0] == 1, "single-layer model"
    d = x_prompt.shape[-1]
    bp, tp, _ = x_prompt.shape
    bs, ts, _ = x_sample.shape
    lam_init = 0.8 - 0.6 * math.exp(-0.3 * 0)

    c_all = jnp.concatenate([c_prompt, c_sample], axis=0)
    rows = -(-c_all.shape[0] // 8) * 8
    c_pad = jnp.concatenate([c_all, jnp.zeros((rows - c_all.shape[0], d), F32)], axis=0)
    mod_all = _ada_call(c_pad, w_ada[0], b_ada[0])
    mod_p = [mod_all[:bp, k * d:(k + 1) * d] for k in range(6)]
    mod_s = [mod_all[bp:bp + bs, k * d:(k + 1) * d] for k in range(6)]

    w_a, w_b = _split_w_in(w_in[0])
    lam = _lam_call(diff_lam[0], lam_init)
    weights = (norm_a_g[0], w_a, w_b, subln_g[0])
    past = (cache_a_k[0], cache_a_v[0], cache_a_kidx[0], cache_b_k[0], cache_b_v[0])

    tm_p = 256
    tm_s = bs * ts
    xp2, oa_p, ob_p, rows_p, exp_p, map_p = _group_forward(
        x_prompt, mod_p, None, rel_bias, lam, weights, lam_init=lam_init, qb=256, qb_b=512, kb=512, tm=tm_p,
        per_row_mod=False)
    xs2, oa_s, ob_s, rows_s, exp_s, map_s = _group_forward(
        x_sample, mod_s, past, rel_bias, lam, weights, lam_init=lam_init, qb=ts, qb_b=ts, kb=2048, tm=tm_s,
        per_row_mod=True)

    w_out_bf = w_out[0].astype(BF16)
    wr_t = w_router[0].T
    rb = router_bias[0].reshape(N_EXPERTS, 1)
    no_rows = jnp.zeros((N_EXPERTS, 1), jnp.int32)
    x1_p, h2_p, ek_p, pk_p, gk_p, cnt_p = _outproj_call(
        xp2, oa_p, ob_p, exp_p(mod_p[2]), exp_p(mod_p[4]), exp_p(mod_p[3]), lambda i: map_p(2 * i), norm_f_g[0],
        w_out_bf, wr_t, rb, no_rows, 2 * tm_p)
    x1_s, h2_s, ek_s, pk_s, gk_s, cnt_all = _outproj_call(
        xs2, oa_s, ob_s, exp_s(mod_s[2]), exp_s(mod_s[4]), exp_s(mod_s[3]), map_s, norm_f_g[0], w_out_bf, wr_t, rb,
        cnt_p, tm_s)

    n_p, n_s = bp * tp, bs * ts
    n_all = n_p + n_s
    n_rows = -(-(n_all * TOP_K) // EXPERT_TILE) * EXPERT_TILE + N_EXPERTS * EXPERT_TILE
    counts = cnt_all[:, 0]
    padded = (counts + EXPERT_TILE - 1) // EXPERT_TILE * EXPERT_TILE
    ends = jnp.cumsum(padded)
    starts = ends - padded
    tile_first_row = jnp.arange(n_rows // EXPERT_TILE, dtype=jnp.int32) * EXPERT_TILE
    tile_expert = jnp.minimum(jnp.sum(ends[None, :] <= tile_first_row[:, None], axis=1), N_EXPERTS - 1).astype(jnp.int32)
    n_used = (ends[-1:] // EXPERT_TILE).astype(jnp.int32)
    ek = jnp.concatenate([ek_p, ek_s], axis=1)
    expert_ids = jnp.arange(N_EXPERTS, dtype=jnp.int32)[:, None, None]
    dest = jnp.sum(jnp.where(ek[None] == expert_ids, starts[:, None, None], 0), axis=0) \
        + jnp.concatenate([pk_p, pk_s], axis=1)
    piece = jnp.arange(PARTS, dtype=jnp.int32)[None, :, None]
    idx = (dest[:, None, :] + piece * n_rows).reshape(1, TOP_K * PARTS * n_all).astype(jnp.int32)

    h2_all = jnp.concatenate([h2_p, h2_s], axis=1).reshape(PARTS * n_all, PART_W)
    x_sorted = _sc_scatter_rows(h2_all, idx, PARTS * n_rows).reshape(PARTS, n_rows, PART_W)
    y_sorted = _expert_call(tile_expert, n_used, x_sorted, w_gate[0], w_up[0], w_down[0])
    rows = _sc_gather_rows(y_sorted.reshape(PARTS * n_rows, PART_W), idx).reshape(TOP_K, PARTS, n_all, PART_W)

    shared = (ws_gate[0].astype(BF16), ws_up[0].astype(BF16), ws_down[0].astype(BF16))
    assert n_p % tm_s == 0
    y_p = _final_call(x1_p, h2_p, rows, 0, gk_p.T, exp_p(mod_p[5]), map_p, final_g, *shared, tm_p)
    y_s = _final_call(x1_s, h2_s, rows, n_p // tm_s, gk_s.T, exp_s(mod_s[5]), map_s, final_g, *shared, tm_s)
    return (y_p.reshape(bp, tp, d), y_s.reshape(bs, ts, d)) + rows_p + rows_s
```
